```python
import jax
import jax.numpy as jnp
from jax import lax
import numpy as np

D_MODEL = 1024
BATCH = 8
SEQ = 4096
DEPTH = 2

GRID_W = 64
CTX_LEN = 256
HEAD_DIM = 64
N_Q_HEADS = 8
N_KV_HEADS = 2
Q_PER_KV = N_Q_HEADS // N_KV_HEADS
ATT_Q_W = N_Q_HEADS * HEAD_DIM
ATT_KV_W = N_KV_HEADS * HEAD_DIM
ATT_SCALE = HEAD_DIM ** -0.5
ROPE_THETA = 10000.0
ROPE_AXIS_DIM = HEAD_DIM // 2
QBLK = 128
WINDOW = 128
SPAN = QBLK + 2 * WINDOW
N_FOURIER_GROUPS = 4
FOURIER_GROUP_W = D_MODEL // 8
FOURIER_W = N_FOURIER_GROUPS * FOURIER_GROUP_W
POOL_WINDOWS = (2, 4, 8, 16)
POOL_GROUP_W = D_MODEL // 8
POOL_W = len(POOL_WINDOWS) * POOL_GROUP_W
N_BRANCHES = 4
BRANCH_W = ATT_Q_W
IN_SPLITS = (FOURIER_W, POOL_W, ATT_Q_W, ATT_KV_W, ATT_KV_W, ATT_Q_W, ATT_KV_W, ATT_KV_W)
IN_W = sum(IN_SPLITS)
IN_OFFSETS = tuple(sum(IN_SPLITS[:i + 1]) for i in range(len(IN_SPLITS) - 1))
N_EXPERTS = 16
N_EXPERT_GROUPS = 4
EXPERTS_PER_GROUP = N_EXPERTS // N_EXPERT_GROUPS
TOP_K = 2
EXPERT_FF = D_MODEL // 2
EPS = 1e-6

kernel_name = 'hybrid_parallel_dit_block'


def rms_norm(x, gain):
    xf = x.astype(jnp.float32)
    y = xf * lax.rsqrt(jnp.mean(xf * xf, axis=-1, keepdims=True) + EPS)
    return (y * gain.astype(jnp.float32)).astype(x.dtype)


def modulate(x, shift, scale):
    return x * (1 + scale) + shift


def rope_tables(n_tokens):
    rows = n_tokens // GRID_W
    row = jnp.repeat(jnp.arange(rows, dtype=jnp.float32), GRID_W)
    col = jnp.tile(jnp.arange(GRID_W, dtype=jnp.float32), rows)
    inv_freq = ROPE_THETA ** (-jnp.arange(0, ROPE_AXIS_DIM, 2, dtype=jnp.float32) / ROPE_AXIS_DIM)
    ang = jnp.stack([row[:, None] * inv_freq, col[:, None] * inv_freq], axis=1)
    return jnp.cos(ang), jnp.sin(ang)


def apply_rope_2d(x, cos, sin):
    xs = x.astype(jnp.float32).reshape(*x.shape[:-1], 2, 2, ROPE_AXIS_DIM // 2)
    x1, x2 = xs[..., 0, :], xs[..., 1, :]
    c, s = cos[:, None], sin[:, None]
    out = jnp.stack([x1 * c - x2 * s, x2 * c + x1 * s], axis=-2)
    return out.reshape(x.shape).astype(x.dtype)


def heads(a, n):
    return a.reshape(a.shape[0], a.shape[1], n, HEAD_DIM)


def group_q(q):
    return q.reshape(q.shape[0], q.shape[1], N_KV_HEADS, Q_PER_KV, HEAD_DIM)


def to_blocks(q):
    b, t = q.shape[:2]
    return q.reshape(b, t // QBLK, QBLK, *q.shape[2:]).swapaxes(0, 1)


def from_blocks(o):
    o = o.swapaxes(0, 1)
    return o.reshape(o.shape[0], -1, ATT_Q_W)


def sink_softmax(s, sink):
    col = jnp.broadcast_to(sink.astype(jnp.float32)[None, :, :, None, None], s.shape[:-1] + (1,))
    p = jax.nn.softmax(jnp.concatenate([s, col], axis=-1), axis=-1)
    return p[..., :-1]


def context_attention(qc, kc, vc, sink=None):
    s = jnp.einsum('bqgrd,bkgd->bgrqk', qc, kc).astype(jnp.float32) * ATT_SCALE
    p = jax.nn.softmax(s, axis=-1) if sink is None else sink_softmax(s, sink)
    o = jnp.einsum('bgrqk,bkgd->bqgrd', p.astype(vc.dtype), vc)
    return o.reshape(o.shape[0], o.shape[1], ATT_Q_W)


def global_attention(q, k, v, kc, vc):
    k_all = jnp.concatenate([kc, k], axis=1)
    v_all = jnp.concatenate([vc, v], axis=1)

    def one_block(qb):
        s = jnp.einsum('bqgrd,bkgd->bgrqk', qb, k_all).astype(jnp.float32) * ATT_SCALE
        p = jax.nn.softmax(s, axis=-1).astype(v_all.dtype)
        return jnp.einsum('bgrqk,bkgd->bqgrd', p, v_all)

    return from_blocks(lax.map(one_block, to_blocks(q)))


def window_attention(q, k, v, kc, vc, sink):
    t = q.shape[1]
    n_ctx = kc.shape[1]
    pad = ((0, 0), (WINDOW, WINDOW), (0, 0), (0, 0))
    kp, vp = jnp.pad(k, pad), jnp.pad(v, pad)
    qi = jnp.arange(QBLK)[:, None]
    kj = jnp.arange(SPAN)[None, :]
    in_band = jnp.abs(kj - WINDOW - qi) <= WINDOW

    def one_block(args):
        qb, blk = args
        start = blk * QBLK
        kw = lax.dynamic_slice_in_dim(kp, start, SPAN, axis=1)
        vw = lax.dynamic_slice_in_dim(vp, start, SPAN, axis=1)
        kpos = start - WINDOW + kj
        valid = in_band & (kpos >= 0) & (kpos < t)
        s_ctx = jnp.einsum('bqgrd,bkgd->bgrqk', qb, kc).astype(jnp.float32)
        s_win = jnp.where(valid, jnp.einsum('bqgrd,bkgd->bgrqk', qb, kw).astype(jnp.float32), -jnp.inf)
        p = sink_softmax(jnp.concatenate([s_ctx, s_win], axis=-1) * ATT_SCALE, sink).astype(v.dtype)
        return (jnp.einsum('bgrqk,bkgd->bqgrd', p[..., :n_ctx], vc)
                + jnp.einsum('bgrqk,bkgd->bqgrd', p[..., n_ctx:], vw))

    return from_blocks(lax.map(one_block, (to_blocks(q), jnp.arange(t // QBLK))))


def fourier_mix(u):
    b, t = u.shape[:2]
    g = u.astype(jnp.float32).reshape(b, t, N_FOURIER_GROUPS, FOURIER_GROUP_W)
    f = jnp.fft.fft2(g, axes=(1, 3), norm='ortho').real
    return f.reshape(b, t, FOURIER_W).astype(u.dtype)


def pool_mix(u, pool_w, pool_scale):
    b, t = u.shape[:2]
    g = u.astype(jnp.float32).reshape(b, t, len(POOL_WINDOWS), POOL_GROUP_W)
    cs = jnp.pad(jnp.cumsum(g, axis=1), ((0, 0), (1, 0), (0, 0), (0, 0)))
    pos = jnp.arange(t)
    outs = []
    for gi, w in enumerate(POOL_WINDOWS):
        lo = jnp.clip(pos - w // 2, 0, t)
        hi = jnp.clip(pos + w // 2, 0, t)
        cs_g = cs[:, :, gi]
        mean = (cs_g[:, hi] - cs_g[:, lo]) / (hi - lo).astype(jnp.float32)[:, None]
        outs.append(mean - g[:, :, gi])
    pooled = jnp.stack(outs, axis=2).astype(u.dtype)
    mixed = jnp.einsum('btgc,gce->btge', pooled, pool_w)
    return mixed.reshape(b, t, POOL_W) * pool_scale


def branch_merge(h, branches, w_branch, w_gate, b_gate, w_out):
    merged = jnp.zeros_like(h)
    for i, br in enumerate(branches):
        gate = jax.nn.sigmoid(h @ w_gate[i] + b_gate[i])
        merged = merged + gate * (br @ w_branch[i])
    return merged @ w_out


def token_mixer(h, hc, cos, sin, w_in, q_gain, k_gain, sink, pool_w, pool_scale,
                w_branch, w_gate, b_gate, w_out, need_ctx_out):
    f_in, p_in, qb, kb, vb, qw, kw, vw = jnp.split(h @ w_in, IN_OFFSETS, axis=-1)
    fc_in, pc_in, qbc, kbc, vbc, qwc, kwc, vwc = jnp.split(hc @ w_in, IN_OFFSETS, axis=-1)
    q_b = apply_rope_2d(rms_norm(heads(qb, N_Q_HEADS), q_gain), cos, sin)
    k_b = apply_rope_2d(rms_norm(heads(kb, N_KV_HEADS), k_gain), cos, sin)
    v_b = heads(vb, N_KV_HEADS)
    k_bc = rms_norm(heads(kbc, N_KV_HEADS), k_gain)
    v_bc = heads(vbc, N_KV_HEADS)
    out_b = global_attention(group_q(q_b), k_b, v_b, k_bc, v_bc)
    sink_gr = sink.reshape(N_KV_HEADS, Q_PER_KV)
    q_c = apply_rope_2d(heads(qw, N_Q_HEADS), cos, sin)
    k_c = apply_rope_2d(heads(kw, N_KV_HEADS), cos, sin)
    v_c = heads(vw, N_KV_HEADS)
    k_cc = heads(kwc, N_KV_HEADS)
    v_cc = heads(vwc, N_KV_HEADS)
    out_c = window_attention(group_q(q_c), k_c, v_c, k_cc, v_cc, sink_gr)
    out_a = fourier_mix(f_in)
    out_d = pool_mix(p_in, pool_w, pool_scale)
    y = branch_merge(h, (out_a, out_b, out_c, out_d), w_branch, w_gate, b_gate, w_out)
    yc = None
    if need_ctx_out:
        oc_b = context_attention(group_q(rms_norm(heads(qbc, N_Q_HEADS), q_gain)), k_bc, v_bc)
        oc_c = context_attention(group_q(heads(qwc, N_Q_HEADS)), k_cc, v_cc, sink_gr)
        oc_a = fourier_mix(fc_in)
        oc_d = pool_mix(pc_in, pool_w, pool_scale)
        yc = branch_merge(hc, (oc_a, oc_b, oc_c, oc_d), w_branch, w_gate, b_gate, w_out)
    return y, yc


def moe_ffn(h, router_w, router_bias, w1, w3, w2):
    shape = h.shape
    tok = h.reshape(-1, D_MODEL)
    aff = jax.nn.sigmoid((tok @ router_w).astype(jnp.float32))
    sel = aff + router_bias.astype(jnp.float32)
    grp = sel.reshape(-1, N_EXPERT_GROUPS, EXPERTS_PER_GROUP)
    group_score = lax.top_k(grp, TOP_K)[0].sum(-1)
    best = jnp.argmax(group_score, axis=-1)
    in_group = best[:, None] == (jnp.arange(N_EXPERTS) // EXPERTS_PER_GROUP)[None, :]
    _, top_idx = lax.top_k(jnp.where(in_group, sel, -jnp.inf), TOP_K)
    w_top = jnp.take_along_axis(aff, top_idx, axis=-1)
    w_top = w_top / jnp.sum(w_top, axis=-1, keepdims=True)
    combine = jnp.sum(jax.nn.one_hot(top_idx, N_EXPERTS, dtype=jnp.float32) * w_top[..., None], axis=1)
    combine = combine.astype(h.dtype)
    out = jnp.zeros_like(tok)
    for e in range(N_EXPERTS):
        hid = jax.nn.silu(tok @ w1[e]) * (tok @ w3[e])
        out = out + combine[:, e:e + 1] * (hid @ w2[e])
    return out.reshape(shape)


def trunk_layer(x, xc, mod, mod_c, cos, sin, norm1, norm2, w_in, q_gain, k_gain, sink,
                pool_w, pool_scale, w_branch, w_gate, b_gate, w_out,
                router_w, router_bias, w1, w3, w2, need_ctx_out):
    sh1, sc1, g1, sh2, sc2, g2 = jnp.split(mod[:, None, :], 6, axis=-1)
    sh1c, sc1c, g1c, sh2c, sc2c, g2c = jnp.split(mod_c, 6)
    h = modulate(rms_norm(x, norm1), sh1, sc1)
    hc = modulate(rms_norm(xc, norm1), sh1c, sc1c)
    y, yc = token_mixer(h, hc, cos, sin, w_in, q_gain, k_gain, sink, pool_w, pool_scale,
                        w_branch, w_gate, b_gate, w_out, need_ctx_out)
    x = x + g1 * y
    x = x + g2 * moe_ffn(modulate(rms_norm(x, norm2), sh2, sc2), router_w, router_bias, w1, w3, w2)
    if need_ctx_out:
        xc = xc + g1c * yc
        xc = xc + g2c * moe_ffn(modulate(rms_norm(xc, norm2), sh2c, sc2c), router_w, router_bias, w1, w3, w2)
    return x, xc


def setup_inputs(seed: int = 0) -> dict:
    key = jax.random.key(seed)
    ks = jax.random.split(key, 24)
    f32 = jnp.float32

    def nrm(k, shape, scale):
        return jax.random.normal(k, shape, f32) * scale

    return {
        'x': nrm(ks[0], (BATCH, SEQ, D_MODEL), 1.0),
        'c': nrm(ks[1], (BATCH, D_MODEL), 1.0),
        'ctx': nrm(ks[2], (BATCH, CTX_LEN, D_MODEL), 1.0),
        'c_ctx': nrm(ks[3], (D_MODEL,), 1.0),
        'w_ada': nrm(ks[4], (DEPTH, D_MODEL, 6 * D_MODEL), 0.5 * D_MODEL ** -0.5),
        'b_ada': nrm(ks[5], (DEPTH, 6 * D_MODEL), 0.02),
        'norm1': 1.0 + nrm(ks[6], (DEPTH, D_MODEL), 0.05),
        'norm2': 1.0 + nrm(ks[7], (DEPTH, D_MODEL), 0.05),
        'w_in': nrm(ks[8], (DEPTH, D_MODEL, IN_W), D_MODEL ** -0.5),
        'q_gain': 1.0 + nrm(ks[9], (DEPTH, HEAD_DIM), 0.05),
        'k_gain': 1.0 + nrm(ks[10], (DEPTH, HEAD_DIM), 0.05),
        'sink': nrm(ks[11], (DEPTH, N_Q_HEADS), 0.5),
        'pool_w': nrm(ks[12], (DEPTH, len(POOL_WINDOWS), POOL_GROUP_W, POOL_GROUP_W), POOL_GROUP_W ** -0.5),
        'pool_scale': 1.0 + nrm(ks[13], (DEPTH, POOL_W), 0.1),
        'w_branch': nrm(ks[14], (DEPTH, N_BRANCHES, BRANCH_W, D_MODEL), BRANCH_W ** -0.5),
        'w_gate': nrm(ks[15], (DEPTH, N_BRANCHES, D_MODEL, D_MODEL), D_MODEL ** -0.5),
        'b_gate': nrm(ks[16], (DEPTH, N_BRANCHES, D_MODEL), 0.02),
        'w_out': nrm(ks[17], (DEPTH, D_MODEL, D_MODEL), D_MODEL ** -0.5),
        'router_w': nrm(ks[18], (D_MODEL, N_EXPERTS), D_MODEL ** -0.5),
        'router_bias': nrm(ks[19], (N_EXPERTS,), 0.01),
        'w1': nrm(ks[20], (DEPTH, N_EXPERTS, D_MODEL, EXPERT_FF), D_MODEL ** -0.5),
        'w3': nrm(ks[21], (DEPTH, N_EXPERTS, D_MODEL, EXPERT_FF), D_MODEL ** -0.5),
        'w2': nrm(ks[22], (DEPTH, N_EXPERTS, EXPERT_FF, D_MODEL), EXPERT_FF ** -0.5),
        'norm_f': 1.0 + nrm(ks[23], (D_MODEL,), 0.05),
    }


def reference(x, c, ctx, c_ctx, w_ada, b_ada, norm1, norm2, w_in, q_gain, k_gain, sink,
              pool_w, pool_scale, w_branch, w_gate, b_gate, w_out, router_w, router_bias,
              w1, w3, w2, norm_f):
    cos, sin = rope_tables(x.shape[1])
    xc = ctx
    for l in range(DEPTH):
        mod = jax.nn.silu(c) @ w_ada[l] + b_ada[l]
        mod_c = jax.nn.silu(c_ctx) @ w_ada[l] + b_ada[l]
        x, xc = trunk_layer(x, xc, mod, mod_c, cos, sin, norm1[l], norm2[l], w_in[l],
                            q_gain[l], k_gain[l], sink[l], pool_w[l], pool_scale[l],
                            w_branch[l], w_gate[l], b_gate[l], w_out[l],
                            router_w, router_bias, w1[l], w3[l], w2[l],
                            need_ctx_out=(l < DEPTH - 1))
    return rms_norm(x, norm_f)
```

```python
import functools
import math

import numpy as np
import jax
import jax.numpy as jnp
from jax import lax
from jax.experimental import pallas as pl
from jax.experimental.pallas import tpu as pltpu

F32 = jnp.float32
BF16 = jnp.bfloat16

D_MODEL = 1024
HEAD_DIM = 64
N_Q_HEADS = 8
N_KV_HEADS = 2
GRID_W = 64
ROPE_THETA = 10000.0
ROPE_AXIS_DIM = HEAD_DIM // 2
QBLK = 128
WINDOW = 128
BRANCH_W = 512
GROUP_W = 128
POOL_WINDOWS = (2, 4, 8, 16)
N_EXPERTS = 16
EXPERTS_PER_GROUP = 4
EXPERT_FF = 512
EPS = 1e-6
MOD_ROWS = 16
NEG_BIG = -1e30
LANES = 128
POOL_HALO = 16

HEAD_PERM = (0, 4, 1, 5, 2, 6, 3, 7)


def _cparams(sem, vmem_mb):
    return pltpu.CompilerParams(dimension_semantics=sem, vmem_limit_bytes=vmem_mb * 1024 * 1024)


def _const_spec(shape):
    nd = len(shape)
    return pl.BlockSpec(shape, lambda *_: (0,) * nd)


def _ada_kernel(c_ref, w_ref, b_ref, o_ref):
    c = c_ref[...]
    s = c * jax.nn.sigmoid(c)
    o_ref[0] = jnp.dot(s.astype(BF16), w_ref[0].astype(BF16), preferred_element_type=F32) + b_ref[0]


def _ada(cc, w_ada, b_ada):
    depth, d, n = w_ada.shape
    tn = 1536
    return pl.pallas_call(
        _ada_kernel,
        grid=(depth, n // tn),
        in_specs=[
            pl.BlockSpec((MOD_ROWS, d), lambda l, j: (0, 0)),
            pl.BlockSpec((1, d, tn), lambda l, j: (l, 0, j)),
            pl.BlockSpec((1, 1, tn), lambda l, j: (l, 0, j)),
        ],
        out_specs=pl.BlockSpec((1, MOD_ROWS, tn), lambda l, j: (l, 0, j)),
        out_shape=jax.ShapeDtypeStruct((depth, MOD_ROWS, n), F32),
        compiler_params=_cparams(("arbitrary", "arbitrary"), 40),
        name="ada",
    )(cc, w_ada, b_ada.reshape(depth, 1, n))


def _norm_mod(x, gain, shift, scale):
    ms = jnp.mean(x * x, axis=-1, keepdims=True)
    return (x * lax.rsqrt(ms + EPS) * gain) * (1.0 + scale) + shift


def _mod_slices(m):
    d = D_MODEL
    return [m[:, i * d:(i + 1) * d] for i in range(6)]


def _head_norm(z, seg, gain):
    ms = jnp.dot((z * z).astype(BF16), seg, preferred_element_type=F32)
    return z * lax.rsqrt(ms + EPS) * gain


def _rope(z, cos, sin_a, sin_b):
    outs = []
    for c in range(z.shape[1] // LANES):
        zc = z[:, c * LANES:(c + 1) * LANES]
        nxt = pltpu.roll(zc, LANES - ROPE_AXIS_DIM // 2, 1)
        prv = pltpu.roll(zc, ROPE_AXIS_DIM // 2, 1)
        outs.append(zc * cos + nxt * sin_a + prv * sin_b)
    return outs[0] if len(outs) == 1 else jnp.concatenate(outs, axis=-1)


def _inproj_kernel(x_ref, mod_ref, n1_ref, w_ref, qg_ref, kg_ref, seg_ref, cs_ref,
                   cos_ref, sa_ref, sb_ref,
                   zr_ref, zi_ref, p_ref, qb_ref, qw_ref, kv_ref):
    sh1, sc1 = _mod_slices(mod_ref[0])[:2]
    h = _norm_mod(x_ref[0], n1_ref[...], sh1, sc1)
    u = jnp.dot(h.astype(BF16), w_ref[...], preferred_element_type=F32)
    cos, sa, sb = cos_ref[...], sa_ref[...], sb_ref[...]
    w = BRANCH_W
    f_in = u[:, 0:w].astype(BF16)
    zr, zi = [], []
    for g in range(w // GROUP_W):
        z = jnp.dot(f_in[:, g * GROUP_W:(g + 1) * GROUP_W], cs_ref[...], preferred_element_type=F32)
        zr.append(z[:, :GROUP_W])
        zi.append(z[:, GROUP_W:])
    zr_ref[0] = jnp.concatenate(zr, axis=-1).astype(BF16)
    zi_ref[0] = jnp.concatenate(zi, axis=-1).astype(BF16)
    p_ref[0] = u[:, w:2 * w].astype(BF16)
    seg = seg_ref[...]
    qb = _rope(_head_norm(u[:, 2 * w:3 * w], seg, qg_ref[...]), cos, sa, sb)
    qb_ref[0] = (qb * (HEAD_DIM ** -0.5)).astype(BF16)
    qw = _rope(u[:, 3 * w:4 * w], cos, sa, sb)
    qw_ref[0] = (qw * (HEAD_DIM ** -0.5)).astype(BF16)
    o = 4 * w
    kb = _rope(_head_norm(u[:, o:o + LANES], seg[:LANES, :LANES], kg_ref[...]), cos, sa, sb)
    vb = u[:, o + LANES:o + 2 * LANES]
    kw = _rope(u[:, o + 2 * LANES:o + 3 * LANES], cos, sa, sb)
    vw = u[:, o + 3 * LANES:o + 4 * LANES]
    kv_ref[0] = jnp.concatenate([kb, vb, kw, vw], axis=-1).astype(BF16)


def _inproj(xa, mod, n1, w_in, qg, kg, seg, cs, cos, sa, sb, t_lat, tm=256):
    b, s, d = xa.shape
    nw = w_in.shape[1]
    n_lat = t_lat // tm
    tok = lambda bi, j: (bi, j, 0)
    tab = lambda bi, j: (j, 0)
    out = jax.ShapeDtypeStruct((b, s, BRANCH_W), BF16)
    return pl.pallas_call(
        _inproj_kernel,
        grid=(b, s // tm),
        in_specs=[
            pl.BlockSpec((1, tm, d), tok),
            pl.BlockSpec((1, 1, 6 * d), lambda bi, j: (jnp.where(j >= n_lat, b, bi), 0, 0)),
            _const_spec((1, d)),
            _const_spec((d, nw)),
            _const_spec((1, BRANCH_W)),
            _const_spec((1, LANES)),
            _const_spec((BRANCH_W, BRANCH_W)),
            _const_spec((GROUP_W, 2 * GROUP_W)),
            pl.BlockSpec((tm, LANES), tab),
            pl.BlockSpec((tm, LANES), tab),
            pl.BlockSpec((tm, LANES), tab),
        ],
        out_specs=[pl.BlockSpec((1, tm, BRANCH_W), tok)] * 6,
        out_shape=[out] * 6,
        compiler_params=_cparams(("parallel", "arbitrary"), 48),
        name="inproj",
    )(xa, mod, n1, w_in, qg, kg, seg, cs, cos, sa, sb)


def _split_heads(qc, lane):
    zero = jnp.zeros_like(qc)
    return jnp.concatenate([jnp.where(lane < HEAD_DIM, qc, zero),
                            jnp.where(lane >= HEAD_DIM, qc, zero)], axis=0)


def _gattn_kernel(q_ref, kv_ref, o_ref, *, tq, tk, t_lat, n_ctx):
    is_ctx_q = pl.program_id(1) * tq >= t_lat
    n_lat_tiles = jnp.where(is_ctx_q, 0, t_lat // tk)
    lane = lax.broadcasted_iota(jnp.int32, (1, LANES), 1)
    nt = (((1,), (1,)), ((), ()))

    def step(k, v, carry, q2):
        m, l, acc = carry
        s = lax.dot_general(q2, k, nt, preferred_element_type=F32)
        m_new = jnp.maximum(m, jnp.max(s, axis=-1, keepdims=True))
        alpha = jnp.exp(m - m_new)
        p = jnp.exp(s - m_new)
        l = alpha * l + jnp.sum(p, axis=-1, keepdims=True)
        acc = alpha * acc + jnp.dot(p.astype(BF16), v, preferred_element_type=F32)
        return m_new, l, acc

    for c in range(BRANCH_W // LANES):
        q2 = _split_heads(q_ref[0, :, c * LANES:(c + 1) * LANES], lane)
        carry = (jnp.full((2 * tq, 1), -jnp.inf, F32), jnp.zeros((2 * tq, 1), F32),
                 jnp.zeros((2 * tq, LANES), F32))
        carry = step(kv_ref[0, t_lat:t_lat + n_ctx, 0:LANES],
                     kv_ref[0, t_lat:t_lat + n_ctx, LANES:2 * LANES], carry, q2)

        def body(i, carry, q2=q2):
            ks = pl.multiple_of(i * tk, tk)
            return step(kv_ref[0, pl.ds(ks, tk), 0:LANES],
                        kv_ref[0, pl.ds(ks, tk), LANES:2 * LANES], carry, q2)

        _, l, acc = lax.fori_loop(0, n_lat_tiles, body, carry)
        o2 = acc / l
        o_ref[0, :, c * LANES:(c + 1) * LANES] = jnp.where(lane < HEAD_DIM, o2[:tq], o2[tq:]).astype(BF16)


def _gattn(qb, kv, s_out, t_lat, tq=256, tk=512):
    b, s, _ = qb.shape
    tk = min(tk, t_lat)
    return pl.pallas_call(
        functools.partial(_gattn_kernel, tq=tq, tk=tk, t_lat=t_lat, n_ctx=s - t_lat),
        grid=(b, s_out // tq),
        in_specs=[
            pl.BlockSpec((1, tq, BRANCH_W), lambda bi, j: (bi, j, 0)),
            pl.BlockSpec((1, s, 2 * LANES), lambda bi, j: (bi, 0, 0)),
        ],
        out_specs=pl.BlockSpec((1, tq, BRANCH_W), lambda bi, j: (bi, j, 0)),
        out_shape=jax.ShapeDtypeStruct((b, s_out, BRANCH_W), BF16),
        compiler_params=_cparams(("parallel", "arbitrary"), 48),
        name="gattn",
    )(qb, kv)


def _wattn_kernel(sink_ref, q_ref, kc_ref, k0_ref, k1_ref, k2_ref, o_ref, *, t_lat, n_ctx):
    start = pl.program_id(1) * QBLK
    is_lat = start < t_lat
    blocks = (kc_ref, k0_ref, k1_ref, k2_ref)
    kall = jnp.concatenate([r[0, :, 0:LANES] for r in blocks], axis=0)
    vall = jnp.concatenate([r[0, :, LANES:2 * LANES] for r in blocks], axis=0)
    nk = n_ctx + 3 * QBLK
    qi = lax.broadcasted_iota(jnp.int32, (QBLK, nk), 0)
    kj = lax.broadcasted_iota(jnp.int32, (QBLK, nk), 1) - n_ctx
    kpos = start - WINDOW + kj
    in_win = (jnp.abs(kj - WINDOW - qi) <= WINDOW) & (kpos >= 0) & (kpos < t_lat) & is_lat
    valid = (kj < 0) | in_win
    valid2 = jnp.concatenate([valid, valid], axis=0)
    lane = lax.broadcasted_iota(jnp.int32, (1, LANES), 1)
    row = lax.broadcasted_iota(jnp.int32, (2 * QBLK, 1), 0)
    nt = (((1,), (1,)), ((), ()))
    for c in range(BRANCH_W // LANES):
        q2 = _split_heads(q_ref[0, :, c * LANES:(c + 1) * LANES], lane)
        s = lax.dot_general(q2, kall, nt, preferred_element_type=F32)
        s = jnp.where(valid2, s, NEG_BIG)
        sk = jnp.where(row < QBLK, sink_ref[2 * c], sink_ref[2 * c + 1])
        m = jnp.maximum(jnp.max(s, axis=-1, keepdims=True), sk)
        p = jnp.exp(s - m)
        den = jnp.sum(p, axis=-1, keepdims=True) + jnp.exp(sk - m)
        o2 = jnp.dot(p.astype(BF16), vall, preferred_element_type=F32) / den
        o_ref[0, :, c * LANES:(c + 1) * LANES] = jnp.where(lane < HEAD_DIM, o2[:QBLK], o2[QBLK:]).astype(BF16)


def _wattn(sink, qw, kv, s_out, t_lat):
    b, s, _ = qw.shape
    n_ctx = s - t_lat
    last = s // QBLK - 1
    return pl.pallas_call(
        functools.partial(_wattn_kernel, t_lat=t_lat, n_ctx=n_ctx),
        grid=(b, s_out // QBLK),
        in_specs=[
            pl.BlockSpec(memory_space=pltpu.SMEM),
            pl.BlockSpec((1, QBLK, BRANCH_W), lambda bi, j: (bi, j, 0)),
            pl.BlockSpec((1, n_ctx, 2 * LANES), lambda bi, j: (bi, t_lat // n_ctx, 1)),
            pl.BlockSpec((1, QBLK, 2 * LANES), lambda bi, j: (bi, jnp.maximum(j - 1, 0), 1)),
            pl.BlockSpec((1, QBLK, 2 * LANES), lambda bi, j: (bi, j, 1)),
            pl.BlockSpec((1, QBLK, 2 * LANES), lambda bi, j: (bi, jnp.minimum(j + 1, last), 1)),
        ],
        out_specs=pl.BlockSpec((1, QBLK, BRANCH_W), lambda bi, j: (bi, j, 0)),
        out_shape=jax.ShapeDtypeStruct((b, s_out, BRANCH_W), BF16),
        compiler_params=_cparams(("parallel", "arbitrary"), 32),
        name="wattn",
    )(sink, qw, kv, kv, kv, kv)


def _fourier_tables(n1, n2):
    t = n1 * n2
    k2 = np.arange(n2)[None, :, None]
    t2 = np.arange(n2)[None, None, :]
    t1 = np.arange(n1)[:, None, None]
    theta = 2.0 * np.pi * ((k2 * t2 * n1 + k2 * t1) % t) / t
    er, ei = np.cos(theta) / math.sqrt(n2), -np.sin(theta) / math.sqrt(n2)
    e = np.concatenate([np.concatenate([er, -ei], axis=2), np.concatenate([ei, er], axis=2)], axis=1)
    k1 = np.arange(n1)[:, None]
    phi = 2.0 * np.pi * ((k1 * np.arange(n1)[None, :]) % n1) / n1
    dcat = np.concatenate([np.cos(phi), np.sin(phi)], axis=1) / math.sqrt(n1)
    return e.astype(np.float32), dcat.astype(np.float32)


def _channel_dft_table():
    c = np.arange(GROUP_W)
    ang = 2.0 * np.pi * ((c[:, None] * c[None, :]) % GROUP_W) / GROUP_W
    return (np.concatenate([np.cos(ang), -np.sin(ang)], axis=1) / math.sqrt(GROUP_W)).astype(np.float32)


def _fourier_kernel(zr_ref, zi_ref, e_ref, d_ref, o_ref, xr_s, xi_s, yr_s, yi_s, *, n1, n2):
    nc = xr_s.shape[0]
    chunk = lambda c: slice(c * LANES, (c + 1) * LANES)

    def gather(ref, start, size, stride):
        return jnp.concatenate([ref[c, pl.ds(start, size, stride=stride), :] for c in range(nc)], axis=-1)

    for c in range(nc):
        xr_s[c] = zr_ref[0, :, chunk(c)].astype(F32)
        xi_s[c] = zi_ref[0, :, chunk(c)].astype(F32)
    for t1 in range(n1):
        xs = jnp.concatenate([gather(xr_s, t1, n2, n1), gather(xi_s, t1, n2, n1)], axis=0).astype(BF16)
        y = jnp.dot(e_ref[t1], xs, preferred_element_type=F32)
        for c in range(nc):
            yr_s[c, t1 * n2:(t1 + 1) * n2, :] = y[:n2, chunk(c)]
            yi_s[c, t1 * n2:(t1 + 1) * n2, :] = y[n2:, chunk(c)]
    for k2 in range(n2):
        ys = jnp.concatenate([gather(yr_s, k2, n1, n2), gather(yi_s, k2, n1, n2)], axis=0).astype(BF16)
        o = jnp.dot(d_ref[...], ys, preferred_element_type=F32)
        for c in range(nc):
            xr_s[c, pl.ds(k2, n1, stride=n2), :] = o[:, chunk(c)]
    for c in range(nc):
        o_ref[0, :, chunk(c)] = xr_s[c].astype(BF16)


def _fourier(zr, zi, e_tab, d_tab, t_len, row_block, cw=256):
    b = zr.shape[0]
    n1 = d_tab.shape[0]
    n2 = t_len // n1
    zspec = pl.BlockSpec((1, t_len, cw), lambda bi, j: (bi, row_block, j))
    return pl.pallas_call(
        functools.partial(_fourier_kernel, n1=n1, n2=n2),
        grid=(b, BRANCH_W // cw),
        in_specs=[zspec, zspec, _const_spec(e_tab.shape), _const_spec(d_tab.shape)],
        out_specs=pl.BlockSpec((1, t_len, cw), lambda bi, j: (bi, 0, j)),
        out_shape=jax.ShapeDtypeStruct((b, t_len, BRANCH_W), BF16),
        scratch_shapes=[pltpu.VMEM((cw // LANES, t_len, LANES), F32)] * 4,
        compiler_params=_cparams(("parallel", "arbitrary"), 48),
        name="fourier",
    )(zr, zi, e_tab, d_tab)


def _pool_kernel(pc_ref, pp_ref, pn_ref, w_ref, sc_ref, o_ref, *, tp, t_lat, s_tot):
    row0 = pl.program_id(1) * tp
    in_lat = row0 < t_lat
    seq_lo = jnp.where(in_lat, 0, t_lat)
    seq_hi = jnp.where(in_lat, t_lat, s_tot)
    ext = jnp.concatenate([pp_ref[0], pc_ref[0], pn_ref[0]], axis=0).astype(F32)
    gpos = row0 - POOL_HALO + lax.broadcasted_iota(jnp.int32, (tp + 2 * POOL_HALO, 1), 0)
    ext = jnp.where((gpos >= seq_lo) & (gpos < seq_hi), ext, 0.0).astype(BF16)
    shape = (tp, tp + 2 * POOL_HALO)
    off = lax.broadcasted_iota(jnp.int32, shape, 1) - POOL_HALO - lax.broadcasted_iota(jnp.int32, shape, 0)
    pos = row0 - seq_lo + lax.broadcasted_iota(jnp.int32, (tp, 1), 0)
    n = seq_hi - seq_lo
    outs = []
    for gi, w in enumerate(POOL_WINDOWS):
        cols = slice(gi * GROUP_W, (gi + 1) * GROUP_W)
        band = jnp.where((off >= -(w // 2)) & (off <= w // 2 - 1), 1.0, 0.0).astype(BF16)
        wsum = jnp.dot(band, ext[:, cols], preferred_element_type=F32)
        cnt = jnp.minimum(pos + w // 2, n) - jnp.maximum(pos - w // 2, 0)
        pooled = wsum / cnt.astype(F32) - pc_ref[0, :, cols].astype(F32)
        outs.append(jnp.dot(pooled.astype(BF16), w_ref[gi], preferred_element_type=F32))
    o_ref[0] = (jnp.concatenate(outs, axis=-1) * sc_ref[...]).astype(BF16)


def _pool(p, pool_w, pool_scale, s_out, t_lat, tp=256):
    b, s, _ = p.shape
    hb = tp // POOL_HALO
    last = s // POOL_HALO - 1
    return pl.pallas_call(
        functools.partial(_pool_kernel, tp=tp, t_lat=t_lat, s_tot=s),
        grid=(b, s_out // tp),
        in_specs=[
            pl.BlockSpec((1, tp, BRANCH_W), lambda bi, j: (bi, j, 0)),
            pl.BlockSpec((1, POOL_HALO, BRANCH_W), lambda bi, j: (bi, jnp.maximum(j * hb - 1, 0), 0)),
            pl.BlockSpec((1, POOL_HALO, BRANCH_W), lambda bi, j: (bi, jnp.minimum((j + 1) * hb, last), 0)),
            _const_spec(pool_w.shape),
            _const_spec((1, BRANCH_W)),
        ],
        out_specs=pl.BlockSpec((1, tp, BRANCH_W), lambda bi, j: (bi, j, 0)),
        out_shape=jax.ShapeDtypeStruct((b, s_out, BRANCH_W), BF16),
        compiler_params=_cparams(("parallel", "arbitrary"), 32),
        name="pool",
    )(p, p, p, pool_w, pool_scale)


def _route(logits_t, bias):
    aff = jax.nn.sigmoid(logits_t)
    sel = aff + bias
    neg = -jnp.inf
    firsts, seconds, scores = [], [], []
    for g in range(N_EXPERTS // EXPERTS_PER_GROUP):
        s = [sel[EXPERTS_PER_GROUP * g + k:EXPERTS_PER_GROUP * g + k + 1, :] for k in range(EXPERTS_PER_GROUP)]
        m1 = jnp.maximum(jnp.maximum(s[0], s[1]), jnp.maximum(s[2], s[3]))
        i1 = jnp.where(s[0] == m1, 0, jnp.where(s[1] == m1, 1, jnp.where(s[2] == m1, 2, 3)))
        r = [jnp.where(i1 == k, neg, s[k]) for k in range(EXPERTS_PER_GROUP)]
        m2 = jnp.maximum(jnp.maximum(r[0], r[1]), jnp.maximum(r[2], r[3]))
        i2 = jnp.where(r[0] == m2, 0, jnp.where(r[1] == m2, 1, jnp.where(r[2] == m2, 2, 3)))
        firsts.append(i1 + EXPERTS_PER_GROUP * g)
        seconds.append(i2 + EXPERTS_PER_GROUP * g)
        scores.append(m1 + m2)
    best = jnp.maximum(jnp.maximum(scores[0], scores[1]), jnp.maximum(scores[2], scores[3]))
    pick = lambda v: jnp.where(scores[0] == best, v[0], jnp.where(scores[1] == best, v[1],
                                                                 jnp.where(scores[2] == best, v[2], v[3])))
    e1, e2 = pick(firsts), pick(seconds)
    eidx = lax.broadcasted_iota(jnp.int32, aff.shape, 0)
    is1, is2 = eidx == e1, eidx == e2
    a1 = jnp.sum(jnp.where(is1, aff, 0.0), axis=0, keepdims=True)
    a2 = jnp.sum(jnp.where(is2, aff, 0.0), axis=0, keepdims=True)
    tot = a1 + a2
    return jnp.where(is1, a1 / tot, 0.0) + jnp.where(is2, a2 / tot, 0.0)


def _merge_kernel(x_ref, mod_ref, n1_ref, n2_ref, a_ref, b_ref, c_ref, d_ref,
                  wg_ref, bg_ref, wb_ref, wo_ref, rw_ref, rb_ref,
                  xo_ref, h2_ref, comb_ref):
    sh1, sc1, g1, sh2, sc2, _ = _mod_slices(mod_ref[0])
    x = x_ref[0]
    hb = _norm_mod(x, n1_ref[...], sh1, sc1).astype(BF16)
    merged = None
    for i, br in enumerate((a_ref, b_ref, c_ref, d_ref)):
        gate = jax.nn.sigmoid(jnp.dot(hb, wg_ref[i], preferred_element_type=F32) + bg_ref[i])
        term = gate * jnp.dot(br[0], wb_ref[i], preferred_element_type=F32)
        merged = term if merged is None else merged + term
    y = jnp.dot(merged.astype(BF16), wo_ref[...], preferred_element_type=F32)
    xn = x + g1 * y
    xo_ref[0] = xn
    h2 = _norm_mod(xn, n2_ref[...], sh2, sc2)
    h2_ref[0] = h2.astype(BF16)
    logits_t = lax.dot_general(rw_ref[...], h2, (((1,), (1,)), ((), ())),
                               precision=lax.Precision.HIGHEST, preferred_element_type=F32)
    comb_ref[0] = _route(logits_t, rb_ref[...])


def _merge(xa, mod, n1, n2, br_a, br_b, br_c, br_d, wg, bg, wb, wo, rw_t, rb, s_out, t_lat, tm=256):
    b, s, d = xa.shape
    n_lat = t_lat // tm
    tok = lambda bi, j: (bi, j, 0)
    br_spec = pl.BlockSpec((1, tm, BRANCH_W), tok)
    return pl.pallas_call(
        _merge_kernel,
        grid=(b, s_out // tm),
        in_specs=[
            pl.BlockSpec((1, tm, d), tok),
            pl.BlockSpec((1, 1, 6 * d), lambda bi, j: (jnp.where(j >= n_lat, b, bi), 0, 0)),
            _const_spec((1, d)), _const_spec((1, d)),
            br_spec, br_spec, br_spec, br_spec,
            _const_spec(wg.shape), _const_spec(bg.shape), _const_spec(wb.shape), _const_spec(wo.shape),
            _const_spec(rw_t.shape), _const_spec(rb.shape),
        ],
        out_specs=[
            pl.BlockSpec((1, tm, d), tok),
            pl.BlockSpec((1, tm, d), tok),
            pl.BlockSpec((1, N_EXPERTS, tm), lambda bi, j: (bi, 0, j)),
        ],
        out_shape=[
            jax.ShapeDtypeStruct((b, s_out, d), F32),
            jax.ShapeDtypeStruct((b, s_out, d), BF16),
            jax.ShapeDtypeStruct((b, N_EXPERTS, s_out), F32),
        ],
        compiler_params=_cparams(("parallel", "arbitrary"), 56),
        name="merge",
    )(xa, mod, n1, n2, br_a, br_b, br_c, br_d, wg, bg, wb, wo, rw_t, rb)


def _moe_kernel(h_ref, comb_ref, x_ref, modb_ref, modc_ref, w13_ref, w2_ref, o_ref, acc_ref, *, tm, t_lat):
    e = pl.program_id(2)

    @pl.when(e == 0)
    def _():
        acc_ref[...] = jnp.zeros_like(acc_ref)

    ab = jnp.dot(h_ref[0], w13_ref[0], preferred_element_type=F32)
    a, bgate = ab[:, :EXPERT_FF], ab[:, EXPERT_FF:]
    hid = (a * jax.nn.sigmoid(a)) * bgate
    comb = comb_ref[0]
    lane = lax.broadcasted_iota(jnp.int32, comb.shape, 1)
    ce = jnp.sum(jnp.where(lane == e, comb, 0.0), axis=1, keepdims=True)
    acc_ref[...] += ce * jnp.dot(hid.astype(BF16), w2_ref[0], preferred_element_type=F32)

    @pl.when(e == N_EXPERTS - 1)
    def _():
        row = pl.program_id(1) * tm + lax.broadcasted_iota(jnp.int32, (tm, 1), 0)
        g2 = jnp.where(row >= t_lat, _mod_slices(modc_ref[0])[5], _mod_slices(modb_ref[0])[5])
        o_ref[0] = x_ref[0] + g2 * acc_ref[...]


def _moe(h2, comb, x1, mod, w13, w2, t_lat, tm):
    b, s, d = x1.shape
    tok = lambda bi, j, e: (bi, j, 0)
    return pl.pallas_call(
        functools.partial(_moe_kernel, tm=tm, t_lat=t_lat),
        grid=(b, s // tm, N_EXPERTS),
        in_specs=[
            pl.BlockSpec((1, tm, d), tok),
            pl.BlockSpec((1, tm, N_EXPERTS), tok),
            pl.BlockSpec((1, tm, d), tok),
            pl.BlockSpec((1, 1, 6 * d), lambda bi, j, e: (bi, 0, 0)),
            pl.BlockSpec((1, 1, 6 * d), lambda bi, j, e: (b, 0, 0)),
            pl.BlockSpec((1, d, 2 * EXPERT_FF), lambda bi, j, e: (e, 0, 0)),
            pl.BlockSpec((1, EXPERT_FF, d), lambda bi, j, e: (e, 0, 0)),
        ],
        out_specs=pl.BlockSpec((1, tm, d), tok),
        out_shape=jax.ShapeDtypeStruct((b, s, d), F32),
        scratch_shapes=[pltpu.VMEM((tm, d), F32)],
        compiler_params=_cparams(("parallel", "arbitrary", "arbitrary"), 56),
        name="moe",
    )(h2, comb, x1, mod, mod, w13, w2)


def _final_norm_kernel(x_ref, g_ref, o_ref):
    x = x_ref[0]
    o_ref[0] = x * lax.rsqrt(jnp.mean(x * x, axis=-1, keepdims=True) + EPS) * g_ref[...]


def _final_norm(x, gain, tm=512):
    b, t, d = x.shape
    return pl.pallas_call(
        _final_norm_kernel,
        grid=(b, t // tm),
        in_specs=[pl.BlockSpec((1, tm, d), lambda bi, j: (bi, j, 0)), _const_spec((1, d))],
        out_specs=pl.BlockSpec((1, tm, d), lambda bi, j: (bi, j, 0)),
        out_shape=jax.ShapeDtypeStruct((b, t, d), F32),
        compiler_params=_cparams(("parallel", "arbitrary"), 32),
        name="final_norm",
    )(x, gain)


def _rope_tables(t_lat, n_ctx):
    rows = t_lat // GRID_W
    row = jnp.repeat(jnp.arange(rows, dtype=F32), GRID_W)
    col = jnp.tile(jnp.arange(GRID_W, dtype=F32), rows)
    inv_freq = ROPE_THETA ** (-jnp.arange(0, ROPE_AXIS_DIM, 2, dtype=F32) / ROPE_AXIS_DIM)
    ang = jnp.stack([row[:, None] * inv_freq, col[:, None] * inv_freq], axis=1)
    cos, sin = jnp.cos(ang), jnp.sin(ang)
    zero = jnp.zeros_like(sin)
    cos_h = jnp.concatenate([cos, cos], axis=-1).reshape(t_lat, HEAD_DIM)
    sa_h = jnp.concatenate([-sin, zero], axis=-1).reshape(t_lat, HEAD_DIM)
    sb_h = jnp.concatenate([zero, sin], axis=-1).reshape(t_lat, HEAD_DIM)
    reps = LANES // HEAD_DIM
    pad = lambda tbl, fill: jnp.concatenate(
        [jnp.tile(tbl, (1, reps)), jnp.full((n_ctx, LANES), fill, F32)], axis=0)
    return pad(cos_h, 1.0), pad(sa_h, 0.0), pad(sb_h, 0.0)


def _permute_heads(w, axis):
    shp = w.shape
    w = w.reshape(shp[:axis] + (N_Q_HEADS, HEAD_DIM) + shp[axis + 1:])
    w = jnp.take(w, jnp.array(HEAD_PERM), axis=axis)
    return w.reshape(shp)


def _square_factor(n):
    r = int(round(math.sqrt(n)))
    assert r * r == n, "sequence lengths must be perfect squares for the two-stage DFT"
    return r


def kernel(x, c, ctx, c_ctx, w_ada, b_ada, norm1, norm2, w_in, q_gain, k_gain, sink, pool_w, pool_scale,
           w_branch, w_gate, b_gate, w_out, router_w, router_bias, w1, w3, w2, norm_f):
    b, t_lat, d = x.shape
    n_ctx = ctx.shape[1]
    s = t_lat + n_ctx
    depth = w_ada.shape[0]
    assert d == D_MODEL and b < MOD_ROWS and t_lat % 256 == 0 and n_ctx % 256 == 0 and t_lat % n_ctx == 0

    xa = jnp.concatenate([x, ctx], axis=1)
    cc = jnp.zeros((MOD_ROWS, d), F32).at[:b].set(c).at[b].set(c_ctx)
    mod_all = _ada(cc, w_ada, b_ada).reshape(depth, MOD_ROWS, 1, 6 * d)

    cos, sa, sb = _rope_tables(t_lat, n_ctx)
    seg = jnp.asarray(np.kron(np.eye(N_Q_HEADS), np.full((HEAD_DIM, HEAD_DIM), 1.0 / HEAD_DIM)), BF16)
    cs = jnp.asarray(_channel_dft_table()).astype(BF16)
    f_lat = [jnp.asarray(a).astype(BF16) for a in _fourier_tables(*(_square_factor(t_lat),) * 2)]
    f_ctx = [jnp.asarray(a).astype(BF16) for a in _fourier_tables(*(_square_factor(n_ctx),) * 2)]
    rw_t = router_w.T
    rb = router_bias.reshape(N_EXPERTS, 1)

    for l in range(depth):
        need_ctx = l < depth - 1
        s_out = s if need_ctx else t_lat
        cols = jnp.split(w_in[l], np.cumsum((512, 512, 512, 128, 128, 512, 128))[:], axis=1)
        f_w, p_w, qb_w, kb_w, vb_w, qw_w, kw_w, vw_w = cols
        w_in_l = jnp.concatenate([f_w, p_w, _permute_heads(qb_w, 1), _permute_heads(qw_w, 1),
                                  kb_w, vb_w, kw_w, vw_w], axis=1).astype(BF16)
        wb_l = jnp.stack([w_branch[l, 0], _permute_heads(w_branch[l, 1], 0),
                          _permute_heads(w_branch[l, 2], 0), w_branch[l, 3]]).astype(BF16)
        mod = mod_all[l]
        n1 = norm1[l].reshape(1, d)
        n2 = norm2[l].reshape(1, d)
        qg = jnp.tile(q_gain[l], N_Q_HEADS).reshape(1, BRANCH_W)
        kg = jnp.tile(k_gain[l], LANES // HEAD_DIM).reshape(1, LANES)

        zr, zi, p_in, qb, qw, kv = _inproj(xa, mod, n1, w_in_l, qg, kg, seg, cs, cos, sa, sb, t_lat)

        out_b = _gattn(qb, kv, s_out, t_lat)
        out_c = _wattn(jnp.take(sink[l], jnp.array(HEAD_PERM)), qw, kv, s_out, t_lat)
        out_a = _fourier(zr, zi, f_lat[0], f_lat[1], t_lat, 0)
        if need_ctx:
            out_a = jnp.concatenate(
                [out_a, _fourier(zr, zi, f_ctx[0], f_ctx[1], n_ctx, t_lat // n_ctx)], axis=1)
        out_d = _pool(p_in, pool_w[l].astype(BF16), pool_scale[l].reshape(1, BRANCH_W), s_out, t_lat)

        x1, h2, comb_t = _merge(xa, mod, n1, n2, out_a, out_b, out_c, out_d,
                                w_gate[l].astype(BF16), b_gate[l].reshape(4, 1, d), wb_l,
                                w_out[l].astype(BF16), rw_t, rb, s_out, t_lat)
        comb = jnp.swapaxes(comb_t, 1, 2)
        w13 = jnp.concatenate([w1[l], w3[l]], axis=-1).astype(BF16)
        tm_moe = s_out // 4 if s_out % 1024 else 1024
        xa = _moe(h2, comb, x1, mod, w13, w2[l].astype(BF16), t_lat, tm_moe)

    return _final_norm(xa, norm_f.reshape(1, d))
```

```python
import functools
import math

import numpy as np
import jax
import jax.numpy as jnp
from jax import lax
from jax.experimental import pallas as pl
from jax.experimental.pallas import tpu as pltpu
from jax.experimental.pallas import tpu_sc as plsc

F32 = jnp.float32
BF16 = jnp.bfloat16

D_MODEL = 1024
HEAD_DIM = 64
N_Q_HEADS = 8
N_KV_HEADS = 2
GRID_W = 64
ROPE_THETA = 10000.0
ROPE_AXIS_DIM = HEAD_DIM // 2
QBLK = 128
WINDOW = 128
BRANCH_W = 512
GROUP_W = 128
POOL_WINDOWS = (2, 4, 8, 16)
N_EXPERTS = 16
EXPERTS_PER_GROUP = 4
EXPERT_FF = 512
EPS = 1e-6
MOD_ROWS = 16
NEG_BIG = -1e30
LOG2E = math.log2(math.e)
LANES = 128
POOL_HALO = 16
PAIRS_PER_GROUP = 6
N_CLASSES = 24
CLASS_ROWS = 32
ROUTE_ROWS = 8
SC_MAX_CHUNK = 128
SC_WEIGHT_COLS = 128

HEAD_PERM = (0, 4, 1, 5, 2, 6, 3, 7)


def _cparams(sem, vmem_mb):
    return pltpu.CompilerParams(dimension_semantics=sem, vmem_limit_bytes=vmem_mb * 1024 * 1024)


def _const_spec(shape):
    nd = len(shape)
    return pl.BlockSpec(shape, lambda *_: (0,) * nd)


def _ada_kernel(c_ref, w_ref, b_ref, o_ref):
    c = c_ref[...]
    s = c * jax.nn.sigmoid(c)
    o_ref[0] = jnp.dot(s.astype(BF16), w_ref[0].astype(BF16), preferred_element_type=F32) + b_ref[0]


def _ada(cc, w_ada, b_ada):
    depth, d, n = w_ada.shape
    tn = 1536
    return pl.pallas_call(
        _ada_kernel,
        grid=(depth, n // tn),
        in_specs=[
            pl.BlockSpec((MOD_ROWS, d), lambda l, j: (0, 0)),
            pl.BlockSpec((1, d, tn), lambda l, j: (l, 0, j)),
            pl.BlockSpec((1, 1, tn), lambda l, j: (l, 0, j)),
        ],
        out_specs=pl.BlockSpec((1, MOD_ROWS, tn), lambda l, j: (l, 0, j)),
        out_shape=jax.ShapeDtypeStruct((depth, MOD_ROWS, n), F32),
        compiler_params=_cparams(("arbitrary", "arbitrary"), 40),
        name="ada",
    )(cc, w_ada, b_ada.reshape(depth, 1, n))


def _norm_mod(x, gain, shift, scale):
    ms = jnp.mean(x * x, axis=-1, keepdims=True)
    return (x * lax.rsqrt(ms + EPS) * gain) * (1.0 + scale) + shift


def _mod_slices(m):
    d = D_MODEL
    return [m[:, i * d:(i + 1) * d] for i in range(6)]


def _head_norm(z, seg, gain):
    ms = jnp.dot((z * z).astype(BF16), seg, preferred_element_type=F32)
    return z * lax.rsqrt(ms + EPS) * gain


def _rope(z, cos, sin_a, sin_b):
    outs = []
    for c in range(z.shape[1] // LANES):
        zc = z[:, c * LANES:(c + 1) * LANES]
        nxt = pltpu.roll(zc, LANES - ROPE_AXIS_DIM // 2, 1)
        prv = pltpu.roll(zc, ROPE_AXIS_DIM // 2, 1)
        outs.append(zc * cos + nxt * sin_a + prv * sin_b)
    return outs[0] if len(outs) == 1 else jnp.concatenate(outs, axis=-1)


def _inproj_kernel(x_ref, mod_ref, n1_ref, w_ref, qg_ref, kg_ref, seg_ref, cs_ref,
                   cos_ref, sa_ref, sb_ref,
                   zr_ref, zi_ref, p_ref, qb_ref, qw_ref, kvb_ref, kvw_ref):
    sh1, sc1 = _mod_slices(mod_ref[0])[:2]
    h = _norm_mod(x_ref[0], n1_ref[...], sh1, sc1)
    u = jnp.dot(h.astype(BF16), w_ref[...], preferred_element_type=F32)
    cos, sa, sb = cos_ref[...], sa_ref[...], sb_ref[...]
    w = BRANCH_W
    f_in = u[:, 0:w].astype(BF16)
    zr, zi = [], []
    for g in range(w // GROUP_W):
        z = jnp.dot(f_in[:, g * GROUP_W:(g + 1) * GROUP_W], cs_ref[...], preferred_element_type=F32)
        zr.append(z[:, :GROUP_W])
        zi.append(z[:, GROUP_W:])
    zr_ref[0] = jnp.concatenate(zr, axis=-1).astype(BF16)
    zi_ref[0] = jnp.concatenate(zi, axis=-1).astype(BF16)
    p_ref[0] = u[:, w:2 * w].astype(BF16)
    seg = seg_ref[...]
    qb = _rope(_head_norm(u[:, 2 * w:3 * w], seg, qg_ref[...]), cos, sa, sb)
    qb_ref[0] = (qb * (HEAD_DIM ** -0.5 * LOG2E)).astype(BF16)
    qw = _rope(u[:, 3 * w:4 * w], cos, sa, sb)
    qw_ref[0] = (qw * (HEAD_DIM ** -0.5)).astype(BF16)
    o = 4 * w
    kb = _rope(_head_norm(u[:, o:o + LANES], seg[:LANES, :LANES], kg_ref[...]), cos, sa, sb)
    vb = u[:, o + LANES:o + 2 * LANES]
    kw = _rope(u[:, o + 2 * LANES:o + 3 * LANES], cos, sa, sb)
    vw = u[:, o + 3 * LANES:o + 4 * LANES]
    kvb_ref[0] = jnp.concatenate([kb, vb, jnp.ones_like(vb)], axis=-1).astype(BF16)
    kvw_ref[0] = jnp.concatenate([kw, vw], axis=-1).astype(BF16)


def _inproj(xa, mod, n1, w_in, qg, kg, seg, cs, cos, sa, sb, t_lat, tm=256):
    b, s, d = xa.shape
    nw = w_in.shape[1]
    n_lat = t_lat // tm
    tok = lambda bi, j: (bi, j, 0)
    tab = lambda bi, j: (j, 0)
    widths = (BRANCH_W,) * 5 + (3 * LANES, 2 * LANES)
    return pl.pallas_call(
        _inproj_kernel,
        grid=(b, s // tm),
        in_specs=[
            pl.BlockSpec((1, tm, d), tok),
            pl.BlockSpec((1, 1, 6 * d), lambda bi, j: (jnp.where(j >= n_lat, b, bi), 0, 0)),
            _const_spec((1, d)),
            _const_spec((d, nw)),
            _const_spec((1, BRANCH_W)),
            _const_spec((1, LANES)),
            _const_spec((BRANCH_W, BRANCH_W)),
            _const_spec((GROUP_W, 2 * GROUP_W)),
            pl.BlockSpec((tm, LANES), tab),
            pl.BlockSpec((tm, LANES), tab),
            pl.BlockSpec((tm, LANES), tab),
        ],
        out_specs=[pl.BlockSpec((1, tm, wd), tok) for wd in widths],
        out_shape=[jax.ShapeDtypeStruct((b, s, wd), BF16) for wd in widths],
        compiler_params=_cparams(("parallel", "arbitrary"), 48),
        name="inproj",
    )(xa, mod, n1, w_in, qg, kg, seg, cs, cos, sa, sb)


def _split_heads(qc, lane):
    zero = jnp.zeros_like(qc)
    return jnp.concatenate([jnp.where(lane < HEAD_DIM, qc, zero),
                            jnp.where(lane >= HEAD_DIM, qc, zero)], axis=0)


def _gattn_kernel(q_ref, kv_ref, o_ref, q2_s, m_s, acc_s, *, tq, tk, sub, t_lat, n_ctx):
    is_ctx_q = pl.program_id(1) * tq >= t_lat
    n_lat_tiles = jnp.where(is_ctx_q, 0, t_lat // tk)
    lane = lax.broadcasted_iota(jnp.int32, (1, LANES), 1)
    nt = (((1,), (1,)), ((), ()))
    n_chunks = BRANCH_W // LANES
    for c in range(n_chunks):
        q2_s[2 * c * tq:(2 * c + 2) * tq, :] = _split_heads(q_ref[0, :, c * LANES:(c + 1) * LANES], lane)

    def tile(k, v, first):
        for r in range(2 * n_chunks * tq // sub):
            rows = slice(r * sub, (r + 1) * sub)
            s = lax.dot_general(q2_s[rows, :], k, nt, preferred_element_type=F32)
            rm = jnp.max(s, axis=-1, keepdims=True)
            if first:
                m_new = rm
                acc_s[rows, :] = jnp.dot(jnp.exp2(s - m_new).astype(BF16), v, preferred_element_type=F32)
            else:
                m_old = m_s[rows, :]
                m_new = jnp.maximum(m_old, rm)
                alpha = jnp.exp2(m_old - m_new)
                pv = jnp.dot(jnp.exp2(s - m_new).astype(BF16), v, preferred_element_type=F32)
                acc_s[rows, :] = alpha * acc_s[rows, :] + pv
            m_s[rows, :] = m_new

    tile(kv_ref[0, t_lat:t_lat + n_ctx, 0:LANES], kv_ref[0, t_lat:t_lat + n_ctx, LANES:3 * LANES], True)

    def body(i, carry):
        ks = pl.multiple_of(i * tk, tk)
        tile(kv_ref[0, pl.ds(ks, tk), 0:LANES], kv_ref[0, pl.ds(ks, tk), LANES:3 * LANES], False)
        return carry

    lax.fori_loop(0, n_lat_tiles, body, 0)
    for c in range(n_chunks):
        lo = acc_s[2 * c * tq:(2 * c + 1) * tq, :]
        hi = acc_s[(2 * c + 1) * tq:(2 * c + 2) * tq, :]
        o = jnp.where(lane < HEAD_DIM, lo[:, :LANES] / lo[:, LANES:], hi[:, :LANES] / hi[:, LANES:])
        o_ref[0, :, c * LANES:(c + 1) * LANES] = o.astype(BF16)


def _gattn(qb, kv, s_out, t_lat, tq=256, tk=1024, sub=128):
    b, s, _ = qb.shape
    tk = min(tk, t_lat)
    rows = 2 * tq * (BRANCH_W // LANES)
    return pl.pallas_call(
        functools.partial(_gattn_kernel, tq=tq, tk=tk, sub=sub, t_lat=t_lat, n_ctx=s - t_lat),
        grid=(b, s_out // tq),
        in_specs=[
            pl.BlockSpec((1, tq, BRANCH_W), lambda bi, j: (bi, j, 0)),
            pl.BlockSpec((1, s, 3 * LANES), lambda bi, j: (bi, 0, 0)),
        ],
        out_specs=pl.BlockSpec((1, tq, BRANCH_W), lambda bi, j: (bi, j, 0)),
        out_shape=jax.ShapeDtypeStruct((b, s_out, BRANCH_W), BF16),
        scratch_shapes=[pltpu.VMEM((rows, LANES), BF16), pltpu.VMEM((rows, 1), F32),
                        pltpu.VMEM((rows, 2 * LANES), F32)],
        compiler_params=_cparams(("parallel", "arbitrary"), 48),
        name="gattn",
    )(qb, kv)


def _wattn_kernel(sink_ref, q_ref, kc_ref, k0_ref, k1_ref, k2_ref, o_ref, *, t_lat, n_ctx):
    start = pl.program_id(1) * QBLK
    is_lat = start < t_lat
    blocks = (kc_ref, k0_ref, k1_ref, k2_ref)
    kall = jnp.concatenate([r[0, :, 0:LANES] for r in blocks], axis=0)
    vall = jnp.concatenate([r[0, :, LANES:2 * LANES] for r in blocks], axis=0)
    nk = n_ctx + 3 * QBLK
    qi = lax.broadcasted_iota(jnp.int32, (QBLK, nk), 0)
    kj = lax.broadcasted_iota(jnp.int32, (QBLK, nk), 1) - n_ctx
    kpos = start - WINDOW + kj
    in_win = (jnp.abs(kj - WINDOW - qi) <= WINDOW) & (kpos >= 0) & (kpos < t_lat) & is_lat
    valid = (kj < 0) | in_win
    valid2 = jnp.concatenate([valid, valid], axis=0)
    lane = lax.broadcasted_iota(jnp.int32, (1, LANES), 1)
    row = lax.broadcasted_iota(jnp.int32, (2 * QBLK, 1), 0)
    nt = (((1,), (1,)), ((), ()))
    for c in range(BRANCH_W // LANES):
        q2 = _split_heads(q_ref[0, :, c * LANES:(c + 1) * LANES], lane)
        s = lax.dot_general(q2, kall, nt, preferred_element_type=F32)
        s = jnp.where(valid2, s, NEG_BIG)
        sk = jnp.where(row < QBLK, sink_ref[2 * c], sink_ref[2 * c + 1])
        m = jnp.maximum(jnp.max(s, axis=-1, keepdims=True), sk)
        p = jnp.exp(s - m)
        den = jnp.sum(p, axis=-1, keepdims=True) + jnp.exp(sk - m)
        o2 = jnp.dot(p.astype(BF16), vall, preferred_element_type=F32) / den
        o_ref[0, :, c * LANES:(c + 1) * LANES] = jnp.where(lane < HEAD_DIM, o2[:QBLK], o2[QBLK:]).astype(BF16)


def _wattn(sink, qw, kv, s_out, t_lat):
    b, s, _ = qw.shape
    n_ctx = s - t_lat
    last = s // QBLK - 1
    return pl.pallas_call(
        functools.partial(_wattn_kernel, t_lat=t_lat, n_ctx=n_ctx),
        grid=(b, s_out // QBLK),
        in_specs=[
            pl.BlockSpec(memory_space=pltpu.SMEM),
            pl.BlockSpec((1, QBLK, BRANCH_W), lambda bi, j: (bi, j, 0)),
            pl.BlockSpec((1, n_ctx, 2 * LANES), lambda bi, j: (bi, t_lat // n_ctx, 0)),
            pl.BlockSpec((1, QBLK, 2 * LANES), lambda bi, j: (bi, jnp.maximum(j - 1, 0), 0)),
            pl.BlockSpec((1, QBLK, 2 * LANES), lambda bi, j: (bi, j, 0)),
            pl.BlockSpec((1, QBLK, 2 * LANES), lambda bi, j: (bi, jnp.minimum(j + 1, last), 0)),
        ],
        out_specs=pl.BlockSpec((1, QBLK, BRANCH_W), lambda bi, j: (bi, j, 0)),
        out_shape=jax.ShapeDtypeStruct((b, s_out, BRANCH_W), BF16),
        compiler_params=_cparams(("parallel", "arbitrary"), 32),
        name="wattn",
    )(sink, qw, kv, kv, kv, kv)


def _fourier_tables(n1, n2):
    t = n1 * n2
    k2 = np.arange(n2)[None, :, None]
    t2 = np.arange(n2)[None, None, :]
    t1 = np.arange(n1)[:, None, None]
    theta = 2.0 * np.pi * ((k2 * t2 * n1 + k2 * t1) % t) / t
    er, ei = np.cos(theta) / math.sqrt(n2), -np.sin(theta) / math.sqrt(n2)
    e = np.concatenate([np.concatenate([er, -ei], axis=2), np.concatenate([ei, er], axis=2)], axis=1)
    k1 = np.arange(n1)[:, None]
    phi = 2.0 * np.pi * ((k1 * np.arange(n1)[None, :]) % n1) / n1
    dcat = np.concatenate([np.cos(phi), np.sin(phi)], axis=1) / math.sqrt(n1)
    return e.astype(np.float32), dcat.astype(np.float32)


def _channel_dft_table():
    c = np.arange(GROUP_W)
    ang = 2.0 * np.pi * ((c[:, None] * c[None, :]) % GROUP_W) / GROUP_W
    return (np.concatenate([np.cos(ang), -np.sin(ang)], axis=1) / math.sqrt(GROUP_W)).astype(np.float32)


def _fourier_kernel(zr_ref, zi_ref, e_ref, d_ref, o_ref, xr_s, xi_s, yr_s, yi_s, *, n1, n2):
    nc = xr_s.shape[0]
    chunk = lambda c: slice(c * LANES, (c + 1) * LANES)

    def gather(ref, start, size, stride):
        return jnp.concatenate([ref[c, pl.ds(start, size, stride=stride), :] for c in range(nc)], axis=-1)

    for c in range(nc):
        xr_s[c] = zr_ref[0, :, chunk(c)].astype(F32)
        xi_s[c] = zi_ref[0, :, chunk(c)].astype(F32)
    for t1 in range(n1):
        xs = jnp.concatenate([gather(xr_s, t1, n2, n1), gather(xi_s, t1, n2, n1)], axis=0).astype(BF16)
        y = jnp.dot(e_ref[t1], xs, preferred_element_type=F32)
        for c in range(nc):
            yr_s[c, t1 * n2:(t1 + 1) * n2, :] = y[:n2, chunk(c)]
            yi_s[c, t1 * n2:(t1 + 1) * n2, :] = y[n2:, chunk(c)]
    for k2 in range(n2):
        ys = jnp.concatenate([gather(yr_s, k2, n1, n2), gather(yi_s, k2, n1, n2)], axis=0).astype(BF16)
        o = jnp.dot(d_ref[...], ys, preferred_element_type=F32)
        for c in range(nc):
            xr_s[c, pl.ds(k2, n1, stride=n2), :] = o[:, chunk(c)]
    for c in range(nc):
        o_ref[0, :, chunk(c)] = xr_s[c].astype(BF16)


def _fourier(zr, zi, e_tab, d_tab, t_len, row_block, cw=256):
    b = zr.shape[0]
    n1 = d_tab.shape[0]
    n2 = t_len // n1
    zspec = pl.BlockSpec((1, t_len, cw), lambda bi, j: (bi, row_block, j))
    return pl.pallas_call(
        functools.partial(_fourier_kernel, n1=n1, n2=n2),
        grid=(b, BRANCH_W // cw),
        in_specs=[zspec, zspec, _const_spec(e_tab.shape), _const_spec(d_tab.shape)],
        out_specs=pl.BlockSpec((1, t_len, cw), lambda bi, j: (bi, 0, j)),
        out_shape=jax.ShapeDtypeStruct((b, t_len, BRANCH_W), BF16),
        scratch_shapes=[pltpu.VMEM((cw // LANES, t_len, LANES), F32)] * 4,
        compiler_params=_cparams(("parallel", "arbitrary"), 48),
        name="fourier",
    )(zr, zi, e_tab, d_tab)


def _pool_kernel(pc_ref, pp_ref, pn_ref, w_ref, sc_ref, o_ref, *, tp, t_lat, s_tot):
    row0 = pl.program_id(1) * tp
    in_lat = row0 < t_lat
    seq_lo = jnp.where(in_lat, 0, t_lat)
    seq_hi = jnp.where(in_lat, t_lat, s_tot)
    ext = jnp.concatenate([pp_ref[0], pc_ref[0], pn_ref[0]], axis=0).astype(F32)
    gpos = row0 - POOL_HALO + lax.broadcasted_iota(jnp.int32, (tp + 2 * POOL_HALO, 1), 0)
    ext = jnp.where((gpos >= seq_lo) & (gpos < seq_hi), ext, 0.0).astype(BF16)
    shape = (tp, tp + 2 * POOL_HALO)
    off = lax.broadcasted_iota(jnp.int32, shape, 1) - POOL_HALO - lax.broadcasted_iota(jnp.int32, shape, 0)
    pos = row0 - seq_lo + lax.broadcasted_iota(jnp.int32, (tp, 1), 0)
    n = seq_hi - seq_lo
    outs = []
    for gi, w in enumerate(POOL_WINDOWS):
        cols = slice(gi * GROUP_W, (gi + 1) * GROUP_W)
        band = jnp.where((off >= -(w // 2)) & (off <= w // 2 - 1), 1.0, 0.0).astype(BF16)
        wsum = jnp.dot(band, ext[:, cols], preferred_element_type=F32)
        cnt = jnp.minimum(pos + w // 2, n) - jnp.maximum(pos - w // 2, 0)
        pooled = wsum / cnt.astype(F32) - pc_ref[0, :, cols].astype(F32)
        outs.append(jnp.dot(pooled.astype(BF16), w_ref[gi], preferred_element_type=F32))
    o_ref[0] = (jnp.concatenate(outs, axis=-1) * sc_ref[...]).astype(BF16)


def _pool(p, pool_w, pool_scale, s_out, t_lat, tp=256):
    b, s, _ = p.shape
    hb = tp // POOL_HALO
    last = s // POOL_HALO - 1
    return pl.pallas_call(
        functools.partial(_pool_kernel, tp=tp, t_lat=t_lat, s_tot=s),
        grid=(b, s_out // tp),
        in_specs=[
            pl.BlockSpec((1, tp, BRANCH_W), lambda bi, j: (bi, j, 0)),
            pl.BlockSpec((1, POOL_HALO, BRANCH_W), lambda bi, j: (bi, jnp.maximum(j * hb - 1, 0), 0)),
            pl.BlockSpec((1, POOL_HALO, BRANCH_W), lambda bi, j: (bi, jnp.minimum((j + 1) * hb, last), 0)),
            _const_spec(pool_w.shape),
            _const_spec((1, BRANCH_W)),
        ],
        out_specs=pl.BlockSpec((1, tp, BRANCH_W), lambda bi, j: (bi, j, 0)),
        out_shape=jax.ShapeDtypeStruct((b, s_out, BRANCH_W), BF16),
        compiler_params=_cparams(("parallel", "arbitrary"), 32),
        name="pool",
    )(p, p, p, pool_w, pool_scale)


def _route(logits_t, bias):
    aff = jax.nn.sigmoid(logits_t)
    sel = aff + bias
    neg = -jnp.inf
    firsts, seconds, scores = [], [], []
    for g in range(N_EXPERTS // EXPERTS_PER_GROUP):
        s = [sel[EXPERTS_PER_GROUP * g + k:EXPERTS_PER_GROUP * g + k + 1, :] for k in range(EXPERTS_PER_GROUP)]
        m1 = jnp.maximum(jnp.maximum(s[0], s[1]), jnp.maximum(s[2], s[3]))
        i1 = jnp.where(s[0] == m1, 0, jnp.where(s[1] == m1, 1, jnp.where(s[2] == m1, 2, 3)))
        r = [jnp.where(i1 == k, neg, s[k]) for k in range(EXPERTS_PER_GROUP)]
        m2 = jnp.maximum(jnp.maximum(r[0], r[1]), jnp.maximum(r[2], r[3]))
        i2 = jnp.where(r[0] == m2, 0, jnp.where(r[1] == m2, 1, jnp.where(r[2] == m2, 2, 3)))
        firsts.append(i1 + EXPERTS_PER_GROUP * g)
        seconds.append(i2 + EXPERTS_PER_GROUP * g)
        scores.append(m1 + m2)
    best = jnp.maximum(jnp.maximum(scores[0], scores[1]), jnp.maximum(scores[2], scores[3]))
    pick = lambda v: jnp.where(scores[0] == best, v[0], jnp.where(scores[1] == best, v[1],
                                                                 jnp.where(scores[2] == best, v[2], v[3])))
    e1, e2 = pick(firsts), pick(seconds)
    eidx = lax.broadcasted_iota(jnp.int32, aff.shape, 0)
    a1 = jnp.sum(jnp.where(eidx == e1, aff, 0.0), axis=0, keepdims=True)
    a2 = jnp.sum(jnp.where(eidx == e2, aff, 0.0), axis=0, keepdims=True)
    tot = a1 + a2
    w1, w2 = a1 / tot, a2 / tot
    swap = e1 > e2
    lo = jnp.where(swap, e2, e1) & (EXPERTS_PER_GROUP - 1)
    hi = jnp.where(swap, e1, e2) & (EXPERTS_PER_GROUP - 1)
    pair = jnp.where(lo == 0, 0, jnp.where(lo == 1, 3, 5)) + hi - lo - 1
    cls = (e1 >> 2) * PAIRS_PER_GROUP + pair
    rows = [cls.astype(F32), jnp.where(swap, w2, w1), jnp.where(swap, w1, w2)]
    return jnp.concatenate(rows + [jnp.zeros_like(w1)] * (ROUTE_ROWS - len(rows)), axis=0)


def _pack_bf16_pairs(v):
    w = v.shape[1] // 2
    bits = pltpu.bitcast(v.astype(BF16).astype(F32), jnp.uint32)
    return pltpu.bitcast(bits[:, :w] | (bits[:, w:] >> 16), jnp.int32)


def _unpack_bf16_pairs(p):
    bits = pltpu.bitcast(p, jnp.uint32)
    hi = pltpu.bitcast(bits & jnp.uint32(0xFFFF0000), F32)
    lo = pltpu.bitcast(bits << 16, F32)
    return jnp.concatenate([hi, lo], axis=-1)


def _merge_kernel(x_ref, mod_ref, n1_ref, n2_ref, a_ref, b_ref, c_ref, d_ref,
                  wg_ref, bg_ref, wb_ref, wo_ref, rw_ref, rb_ref,
                  xo_ref, h2_ref, route_ref):
    sh1, sc1, g1, sh2, sc2, _ = _mod_slices(mod_ref[0])
    x = x_ref[0]
    hb = _norm_mod(x, n1_ref[...], sh1, sc1).astype(BF16)
    merged = None
    for i, br in enumerate((a_ref, b_ref, c_ref, d_ref)):
        gate = jax.nn.sigmoid(jnp.dot(hb, wg_ref[i], preferred_element_type=F32) + bg_ref[i])
        term = gate * jnp.dot(br[0], wb_ref[i], preferred_element_type=F32)
        merged = term if merged is None else merged + term
    y = jnp.dot(merged.astype(BF16), wo_ref[...], preferred_element_type=F32)
    xn = x + g1 * y
    xo_ref[0] = xn
    h2 = _norm_mod(xn, n2_ref[...], sh2, sc2)
    h2_ref[0] = _pack_bf16_pairs(h2)
    logits_t = lax.dot_general(rw_ref[...], h2, (((1,), (1,)), ((), ())),
                               precision=lax.Precision.HIGHEST, preferred_element_type=F32)
    route_ref[0] = _route(logits_t, rb_ref[...])


def _merge(xa, mod, n1, n2, br_a, br_b, br_c, br_d, wg, bg, wb, wo, rw_t, rb, s_out, t_lat, tm=256):
    b, s, d = xa.shape
    n_lat = t_lat // tm
    tok = lambda bi, j: (bi, j, 0)
    br_spec = pl.BlockSpec((1, tm, BRANCH_W), tok)
    return pl.pallas_call(
        _merge_kernel,
        grid=(b, s_out // tm),
        in_specs=[
            pl.BlockSpec((1, tm, d), tok),
            pl.BlockSpec((1, 1, 6 * d), lambda bi, j: (jnp.where(j >= n_lat, b, bi), 0, 0)),
            _const_spec((1, d)), _const_spec((1, d)),
            br_spec, br_spec, br_spec, br_spec,
            _const_spec(wg.shape), _const_spec(bg.shape), _const_spec(wb.shape), _const_spec(wo.shape),
            _const_spec(rw_t.shape), _const_spec(rb.shape),
        ],
        out_specs=[
            pl.BlockSpec((1, tm, d), tok),
            pl.BlockSpec((1, tm, d // 2), tok),
            pl.BlockSpec((1, ROUTE_ROWS, tm), lambda bi, j: (bi, 0, j)),
        ],
        out_shape=[
            jax.ShapeDtypeStruct((b, s_out, d), F32),
            jax.ShapeDtypeStruct((b, s_out, d // 2), jnp.int32),
            jax.ShapeDtypeStruct((b, ROUTE_ROWS, s_out), F32),
        ],
        compiler_params=_cparams(("parallel", "arbitrary"), 56),
        name="merge",
    )(xa, mod, n1, n2, br_a, br_b, br_c, br_d, wg, bg, wb, wo, rw_t, rb)


def _rank_kernel(cls_ref, rank_ref, cnt_ref, cnt_s, *, tr):
    @pl.when(pl.program_id(0) == 0)
    def _():
        cnt_s[...] = jnp.zeros_like(cnt_s)

    cls = cls_ref[0]
    cid = lax.broadcasted_iota(jnp.int32, (CLASS_ROWS, tr), 0).astype(F32)
    onehot = cid == cls
    before = lax.broadcasted_iota(jnp.int32, (tr, tr), 0) < lax.broadcasted_iota(jnp.int32, (tr, tr), 1)
    prefix = jnp.dot(jnp.where(onehot, 1.0, 0.0).astype(BF16), jnp.where(before, 1.0, 0.0).astype(BF16),
                     preferred_element_type=F32)
    carry = cnt_s[...][:, 0:1]
    rank_ref[0] = jnp.sum(jnp.where(onehot, prefix + carry, 0.0), axis=0, keepdims=True)
    cnt_s[...] += jnp.sum(jnp.where(onehot, 1.0, 0.0), axis=1, keepdims=True)
    cnt_ref[...] = cnt_s[...]


def _rank(cls_flat, tr=512):
    n = cls_flat.shape[0]
    tr = math.gcd(n, tr)
    cls3 = cls_flat.reshape(n // tr, 1, tr)
    rank, cnt = pl.pallas_call(
        functools.partial(_rank_kernel, tr=tr),
        grid=(n // tr,),
        in_specs=[pl.BlockSpec((1, 1, tr), lambda i: (i, 0, 0))],
        out_specs=[pl.BlockSpec((1, 1, tr), lambda i: (i, 0, 0)), _const_spec((CLASS_ROWS, LANES))],
        out_shape=[jax.ShapeDtypeStruct((n // tr, 1, tr), F32), jax.ShapeDtypeStruct((CLASS_ROWS, LANES), F32)],
        scratch_shapes=[pltpu.VMEM((CLASS_ROWS, LANES), F32)],
        compiler_params=_cparams(("arbitrary",), 32),
        name="rank",
    )(cls3)
    return rank.reshape(n), cnt[:N_CLASSES, 0]


def _sc_layout(n):
    info = plsc.get_sparse_core_info()
    nw = info.num_cores * info.num_subcores
    per_worker = n // nw
    assert per_worker * nw == n
    chunk = max(c for c in range(8, SC_MAX_CHUNK + 1, 8) if per_worker % c == 0)
    return info.num_cores, nw, per_worker // chunk, chunk


def _sc_scatter_rows(src, wts, pos, n_out):
    n, w = src.shape
    nc, nw, k, c = _sc_layout(n)
    mesh = plsc.VectorSubcoreMesh(core_axis_name="c", subcore_axis_name="s")

    @functools.partial(
        pl.kernel, mesh=mesh,
        out_type=(jax.ShapeDtypeStruct((n_out, w), src.dtype), jax.ShapeDtypeStruct((n_out, wts.shape[1]), wts.dtype)),
        scratch_types=[pltpu.VMEM((k, c), jnp.int32), pltpu.VMEM((c, w), src.dtype),
                       pltpu.VMEM((c, wts.shape[1]), wts.dtype), pltpu.SemaphoreType.DMA],
        name="moe_scatter",
    )
    def scatter(src_hbm, wts_hbm, pos_hbm, out_hbm, wout_hbm, idx_v, rows_v, wrows_v, sem):
        wid = lax.axis_index("s") * nc + lax.axis_index("c")
        pltpu.sync_copy(pos_hbm.at[wid], idx_v)

        @pl.loop(0, k)
        def _(j):
            off = pl.multiple_of(wid * (k * c) + j * c, 8)
            pltpu.sync_copy(src_hbm.at[pl.ds(off, c)], rows_v)
            pltpu.sync_copy(wts_hbm.at[pl.ds(off, c)], wrows_v)
            pltpu.async_copy(rows_v, out_hbm.at[idx_v.at[j]], sem).wait()
            pltpu.async_copy(wrows_v, wout_hbm.at[idx_v.at[j]], sem).wait()

    return scatter(src, wts, pos.reshape(nw, k, c))


def _sc_gather_rows(src, pos):
    n = pos.shape[0]
    w = src.shape[1]
    nc, nw, k, c = _sc_layout(n)
    mesh = plsc.VectorSubcoreMesh(core_axis_name="c", subcore_axis_name="s")

    @functools.partial(
        pl.kernel, mesh=mesh,
        out_type=jax.ShapeDtypeStruct((n, w), src.dtype),
        scratch_types=[pltpu.VMEM((k, c), jnp.int32), pltpu.VMEM((c, w), src.dtype), pltpu.SemaphoreType.DMA],
        name="moe_gather",
    )
    def gather(src_hbm, pos_hbm, out_hbm, idx_v, rows_v, sem):
        wid = lax.axis_index("s") * nc + lax.axis_index("c")
        pltpu.sync_copy(pos_hbm.at[wid], idx_v)

        @pl.loop(0, k)
        def _(j):
            off = pl.multiple_of(wid * (k * c) + j * c, 8)
            pltpu.async_copy(src_hbm.at[idx_v.at[j]], rows_v, sem).wait()
            pltpu.sync_copy(rows_v, out_hbm.at[pl.ds(off, c)])

    return gather(src, pos.reshape(nw, k, c))


def _gmm_kernel(lo_ref, hi_ref, nact_ref, h_ref, wt_ref, w13a_ref, w13b_ref, w2a_ref, w2b_ref, o_ref):
    @pl.when(pl.program_id(0) < nact_ref[0])
    def _():
        x = _unpack_bf16_pairs(h_ref[...]).astype(BF16)
        wts = wt_ref[...]

        def expert(w13_ref, w2_ref):
            ab = jnp.dot(x, w13_ref[0], preferred_element_type=F32)
            a, gate = ab[:, :EXPERT_FF], ab[:, EXPERT_FF:]
            hid = (a * jax.nn.sigmoid(a)) * gate
            return jnp.dot(hid.astype(BF16), w2_ref[0], preferred_element_type=F32)

        y = wts[:, 0:1] * expert(w13a_ref, w2a_ref) + wts[:, 1:2] * expert(w13b_ref, w2b_ref)
        o_ref[...] = _pack_bf16_pairs(y)


def _gmm(tile_lo, tile_hi, n_act, hs, ws, w13, w2, tm):
    n_pad, half = hs.shape
    d = 2 * half
    row = lambda t, lo, hi, na: (jnp.minimum(t, na[0] - 1), 0)
    e_lo = lambda t, lo, hi, na: (lo[jnp.minimum(t, na[0] - 1)], 0, 0)
    e_hi = lambda t, lo, hi, na: (hi[jnp.minimum(t, na[0] - 1)], 0, 0)
    return pl.pallas_call(
        _gmm_kernel,
        grid_spec=pltpu.PrefetchScalarGridSpec(
            num_scalar_prefetch=3,
            grid=(n_pad // tm,),
            in_specs=[
                pl.BlockSpec((tm, half), row),
                pl.BlockSpec((tm, ws.shape[1]), row),
                pl.BlockSpec((1, d, 2 * EXPERT_FF), e_lo),
                pl.BlockSpec((1, d, 2 * EXPERT_FF), e_hi),
                pl.BlockSpec((1, EXPERT_FF, d), e_lo),
                pl.BlockSpec((1, EXPERT_FF, d), e_hi),
            ],
            out_specs=pl.BlockSpec((tm, half), row),
        ),
        out_shape=jax.ShapeDtypeStruct((n_pad, half), jnp.int32),
        compiler_params=_cparams(("arbitrary",), 48),
        name="moe_gmm",
    )(tile_lo, tile_hi, n_act, hs, ws, w13, w13, w2, w2)


def _moe_routed(h2p, route, w13, w2, tm=256):
    b, s, half = h2p.shape
    n = b * s
    n_pad = n + N_CLASSES * tm
    cls = route[:, 0, :].reshape(n)
    rank, counts = _rank(cls)
    counts = counts.astype(jnp.int32)
    padded = (counts + tm - 1) // tm * tm
    ends = jnp.cumsum(padded)
    pos = jnp.take(ends - padded, cls.astype(jnp.int32)) + rank.astype(jnp.int32)
    n_act = (ends[-1] // tm).reshape(1)
    tile_row = jnp.arange(n_pad // tm, dtype=jnp.int32) * tm
    tile_cls = jnp.minimum(jnp.sum(tile_row[:, None] >= ends[None, :], axis=1), N_CLASSES - 1)
    pair_lo, pair_hi = (jnp.asarray(a, jnp.int32) for a in _class_experts())
    wts = jnp.concatenate([route[:, 1, :].reshape(n, 1), route[:, 2, :].reshape(n, 1),
                           jnp.zeros((n, SC_WEIGHT_COLS - 2), F32)], axis=1)
    hs, ws = _sc_scatter_rows(h2p.reshape(n, half), wts, pos, n_pad)
    ys = _gmm(jnp.take(pair_lo, tile_cls), jnp.take(pair_hi, tile_cls), n_act, hs, ws, w13, w2, tm)
    return _sc_gather_rows(ys, pos).reshape(b, s, half)


def _class_experts():
    lo, hi = [], []
    for g in range(N_EXPERTS // EXPERTS_PER_GROUP):
        for i in range(EXPERTS_PER_GROUP):
            for j in range(i + 1, EXPERTS_PER_GROUP):
                lo.append(EXPERTS_PER_GROUP * g + i)
                hi.append(EXPERTS_PER_GROUP * g + j)
    return np.array(lo), np.array(hi)


def _residual_kernel(x_ref, y_ref, mod_ref, g_ref, o_ref, *, final):
    g2 = _mod_slices(mod_ref[0])[5]
    x = x_ref[0] + g2 * _unpack_bf16_pairs(y_ref[0])
    if final:
        x = x * lax.rsqrt(jnp.mean(x * x, axis=-1, keepdims=True) + EPS) * g_ref[...]
    o_ref[0] = x


def _residual(x1, yp, mod, gain, t_lat, final, tm=256):
    b, s, d = x1.shape
    n_lat = t_lat // tm
    tok = lambda bi, j: (bi, j, 0)
    return pl.pallas_call(
        functools.partial(_residual_kernel, final=final),
        grid=(b, s // tm),
        in_specs=[
            pl.BlockSpec((1, tm, d), tok),
            pl.BlockSpec((1, tm, d // 2), tok),
            pl.BlockSpec((1, 1, 6 * d), lambda bi, j: (jnp.where(j >= n_lat, b, bi), 0, 0)),
            _const_spec((1, d)),
        ],
        out_specs=pl.BlockSpec((1, tm, d), tok),
        out_shape=jax.ShapeDtypeStruct((b, s, d), F32),
        compiler_params=_cparams(("parallel", "arbitrary"), 32),
        name="residual",
    )(x1, yp, mod, gain)


def _rope_tables(t_lat, n_ctx):
    rows = t_lat // GRID_W
    row = jnp.repeat(jnp.arange(rows, dtype=F32), GRID_W)
    col = jnp.tile(jnp.arange(GRID_W, dtype=F32), rows)
    inv_freq = ROPE_THETA ** (-jnp.arange(0, ROPE_AXIS_DIM, 2, dtype=F32) / ROPE_AXIS_DIM)
    ang = jnp.stack([row[:, None] * inv_freq, col[:, None] * inv_freq], axis=1)
    cos, sin = jnp.cos(ang), jnp.sin(ang)
    zero = jnp.zeros_like(sin)
    cos_h = jnp.concatenate([cos, cos], axis=-1).reshape(t_lat, HEAD_DIM)
    sa_h = jnp.concatenate([-sin, zero], axis=-1).reshape(t_lat, HEAD_DIM)
    sb_h = jnp.concatenate([zero, sin], axis=-1).reshape(t_lat, HEAD_DIM)
    reps = LANES // HEAD_DIM
    pad = lambda tbl, fill: jnp.concatenate(
        [jnp.tile(tbl, (1, reps)), jnp.full((n_ctx, LANES), fill, F32)], axis=0)
    return pad(cos_h, 1.0), pad(sa_h, 0.0), pad(sb_h, 0.0)


def _permute_heads(w, axis):
    shp = w.shape
    w = w.reshape(shp[:axis] + (N_Q_HEADS, HEAD_DIM) + shp[axis + 1:])
    w = jnp.take(w, jnp.array(HEAD_PERM), axis=axis)
    return w.reshape(shp)


def _square_factor(n):
    r = int(round(math.sqrt(n)))
    assert r * r == n, "sequence lengths must be perfect squares for the two-stage DFT"
    return r


def kernel(x, c, ctx, c_ctx, w_ada, b_ada, norm1, norm2, w_in, q_gain, k_gain, sink, pool_w, pool_scale,
           w_branch, w_gate, b_gate, w_out, router_w, router_bias, w1, w3, w2, norm_f):
    b, t_lat, d = x.shape
    n_ctx = ctx.shape[1]
    s = t_lat + n_ctx
    depth = w_ada.shape[0]
    assert d == D_MODEL and b < MOD_ROWS and t_lat % 256 == 0 and n_ctx % 256 == 0 and t_lat % n_ctx == 0

    xa = jnp.concatenate([x, ctx], axis=1)
    cc = jnp.zeros((MOD_ROWS, d), F32).at[:b].set(c).at[b].set(c_ctx)
    mod_all = _ada(cc, w_ada, b_ada).reshape(depth, MOD_ROWS, 1, 6 * d)

    cos, sa, sb = _rope_tables(t_lat, n_ctx)
    seg = jnp.asarray(np.kron(np.eye(N_Q_HEADS), np.full((HEAD_DIM, HEAD_DIM), 1.0 / HEAD_DIM)), BF16)
    cs = jnp.asarray(_channel_dft_table()).astype(BF16)
    f_lat = [jnp.asarray(a).astype(BF16) for a in _fourier_tables(*(_square_factor(t_lat),) * 2)]
    f_ctx = [jnp.asarray(a).astype(BF16) for a in _fourier_tables(*(_square_factor(n_ctx),) * 2)]
    rw_t = router_w.T
    rb = router_bias.reshape(N_EXPERTS, 1)

    for l in range(depth):
        need_ctx = l < depth - 1
        s_out = s if need_ctx else t_lat
        cols = jnp.split(w_in[l], np.cumsum((512, 512, 512, 128, 128, 512, 128))[:], axis=1)
        f_w, p_w, qb_w, kb_w, vb_w, qw_w, kw_w, vw_w = cols
        w_in_l = jnp.concatenate([f_w, p_w, _permute_heads(qb_w, 1), _permute_heads(qw_w, 1),
                                  kb_w, vb_w, kw_w, vw_w], axis=1).astype(BF16)
        wb_l = jnp.stack([w_branch[l, 0], _permute_heads(w_branch[l, 1], 0),
                          _permute_heads(w_branch[l, 2], 0), w_branch[l, 3]]).astype(BF16)
        mod = mod_all[l]
        n1 = norm1[l].reshape(1, d)
        n2 = norm2[l].reshape(1, d)
        qg = jnp.tile(q_gain[l], N_Q_HEADS).reshape(1, BRANCH_W)
        kg = jnp.tile(k_gain[l], LANES // HEAD_DIM).reshape(1, LANES)

        zr, zi, p_in, qb, qw, kvb, kvw = _inproj(xa, mod, n1, w_in_l, qg, kg, seg, cs, cos, sa, sb, t_lat)

        out_b = _gattn(qb, kvb, s_out, t_lat)
        out_c = _wattn(jnp.take(sink[l], jnp.array(HEAD_PERM)), qw, kvw, s_out, t_lat)
        out_a = _fourier(zr, zi, f_lat[0], f_lat[1], t_lat, 0)
        if need_ctx:
            out_a = jnp.concatenate(
                [out_a, _fourier(zr, zi, f_ctx[0], f_ctx[1], n_ctx, t_lat // n_ctx)], axis=1)
        out_d = _pool(p_in, pool_w[l].astype(BF16), pool_scale[l].reshape(1, BRANCH_W), s_out, t_lat)

        x1, h2p, route = _merge(xa, mod, n1, n2, out_a, out_b, out_c, out_d,
                                w_gate[l].astype(BF16), b_gate[l].reshape(4, 1, d), wb_l,
                                w_out[l].astype(BF16), rw_t, rb, s_out, t_lat)
        w13 = jnp.concatenate([w1[l], w3[l]], axis=-1).astype(BF16)
        yp = _moe_routed(h2p, route, w13, w2[l].astype(BF16))
        xa = _residual(x1, yp, mod, norm_f.reshape(1, d), t_lat, final=(l == depth - 1))

    return xa
```

```python
import functools
import math

import numpy as np
import jax
import jax.numpy as jnp
from jax import lax
from jax.experimental import pallas as pl
from jax.experimental.pallas import tpu as pltpu
from jax.experimental.pallas import tpu_sc as plsc

F32 = jnp.float32
BF16 = jnp.bfloat16

D_MODEL = 1024
HEAD_DIM = 64
N_Q_HEADS = 8
N_KV_HEADS = 2
GRID_W = 64
ROPE_THETA = 10000.0
ROPE_AXIS_DIM = HEAD_DIM // 2
QBLK = 128
WINDOW = 128
BRANCH_W = 512
GROUP_W = 128
POOL_WINDOWS = (2, 4, 8, 16)
N_EXPERTS = 16
EXPERTS_PER_GROUP = 4
EXPERT_FF = 512
EPS = 1e-6
MOD_ROWS = 16
NEG_BIG = -1e30
LOG2E = math.log2(math.e)
LANES = 128
POOL_HALO = 16
PAIRS_PER_GROUP = 6
N_CLASSES = 24
CLASS_ROWS = 32
ROUTE_ROWS = 8
SC_MAX_CHUNK = 128
SC_WEIGHT_COLS = 128

HEAD_PERM = (0, 4, 1, 5, 2, 6, 3, 7)


def _cparams(sem, vmem_mb):
    return pltpu.CompilerParams(dimension_semantics=sem, vmem_limit_bytes=vmem_mb * 1024 * 1024)


def _const_spec(shape):
    nd = len(shape)
    return pl.BlockSpec(shape, lambda *_: (0,) * nd)


def _ada_kernel(c_ref, w_ref, b_ref, o_ref):
    c = c_ref[...]
    s = c * jax.nn.sigmoid(c)
    o_ref[0] = jnp.dot(s.astype(BF16), w_ref[0].astype(BF16), preferred_element_type=F32) + b_ref[0]


def _ada(cc, w_ada, b_ada):
    depth, d, n = w_ada.shape
    tn = 1536
    return pl.pallas_call(
        _ada_kernel,
        grid=(depth, n // tn),
        in_specs=[
            pl.BlockSpec((MOD_ROWS, d), lambda l, j: (0, 0)),
            pl.BlockSpec((1, d, tn), lambda l, j: (l, 0, j)),
            pl.BlockSpec((1, 1, tn), lambda l, j: (l, 0, j)),
        ],
        out_specs=pl.BlockSpec((1, MOD_ROWS, tn), lambda l, j: (l, 0, j)),
        out_shape=jax.ShapeDtypeStruct((depth, MOD_ROWS, n), F32),
        compiler_params=_cparams(("arbitrary", "arbitrary"), 40),
        name="ada",
    )(cc, w_ada, b_ada.reshape(depth, 1, n))


def _norm_mod(x, gain, shift, scale):
    ms = jnp.mean(x * x, axis=-1, keepdims=True)
    return (x * lax.rsqrt(ms + EPS) * gain) * (1.0 + scale) + shift


def _mod_slices(m):
    d = D_MODEL
    return [m[:, i * d:(i + 1) * d] for i in range(6)]


def _head_norm(z, seg, gain):
    ms = jnp.dot((z * z).astype(BF16), seg, preferred_element_type=F32)
    return z * lax.rsqrt(ms + EPS) * gain


def _rope(z, cos, sin_a, sin_b):
    outs = []
    for c in range(z.shape[1] // LANES):
        zc = z[:, c * LANES:(c + 1) * LANES]
        nxt = pltpu.roll(zc, LANES - ROPE_AXIS_DIM // 2, 1)
        prv = pltpu.roll(zc, ROPE_AXIS_DIM // 2, 1)
        outs.append(zc * cos + nxt * sin_a + prv * sin_b)
    return outs[0] if len(outs) == 1 else jnp.concatenate(outs, axis=-1)


def _inproj_kernel(x_ref, mod_ref, n1_ref, w_ref, qg_ref, kg_ref, seg_ref, cs_ref,
                   cos_ref, sa_ref, sb_ref,
                   zr_ref, zi_ref, p_ref, qb_ref, qw_ref, kvb_ref, kvw_ref):
    sh1, sc1 = _mod_slices(mod_ref[0])[:2]
    h = _norm_mod(x_ref[0], n1_ref[...], sh1, sc1)
    u = jnp.dot(h.astype(BF16), w_ref[...], preferred_element_type=F32)
    cos, sa, sb = cos_ref[...], sa_ref[...], sb_ref[...]
    w = BRANCH_W
    f_in = u[:, 0:w].astype(BF16)
    zr, zi = [], []
    for g in range(w // GROUP_W):
        z = jnp.dot(f_in[:, g * GROUP_W:(g + 1) * GROUP_W], cs_ref[...], preferred_element_type=F32)
        zr.append(z[:, :GROUP_W])
        zi.append(z[:, GROUP_W:])
    zr_ref[0] = jnp.concatenate(zr, axis=-1).astype(BF16)
    zi_ref[0] = jnp.concatenate(zi, axis=-1).astype(BF16)
    p_ref[0] = u[:, w:2 * w].astype(BF16)
    seg = seg_ref[...]
    qb = _rope(_head_norm(u[:, 2 * w:3 * w], seg, qg_ref[...]), cos, sa, sb)
    qb_ref[0] = (qb * (HEAD_DIM ** -0.5 * LOG2E)).astype(BF16)
    qw = _rope(u[:, 3 * w:4 * w], cos, sa, sb)
    qw_ref[0] = (qw * (HEAD_DIM ** -0.5 * LOG2E)).astype(BF16)
    o = 4 * w
    kb = _rope(_head_norm(u[:, o:o + LANES], seg[:LANES, :LANES], kg_ref[...]), cos, sa, sb)
    vb = u[:, o + LANES:o + 2 * LANES]
    kw = _rope(u[:, o + 2 * LANES:o + 3 * LANES], cos, sa, sb)
    vw = u[:, o + 3 * LANES:o + 4 * LANES]
    kvb_ref[0] = jnp.concatenate([kb, vb, jnp.ones_like(vb)], axis=-1).astype(BF16)
    kvw_ref[0] = jnp.concatenate([kw, vw, jnp.ones_like(vw)], axis=-1).astype(BF16)


def _inproj(xa, mod, n1, w_in, qg, kg, seg, cs, cos, sa, sb, t_lat, tm=256):
    b, s, d = xa.shape
    nw = w_in.shape[1]
    n_lat = t_lat // tm
    tok = lambda bi, j: (bi, j, 0)
    tab = lambda bi, j: (j, 0)
    widths = (BRANCH_W,) * 5 + (3 * LANES, 3 * LANES)
    return pl.pallas_call(
        _inproj_kernel,
        grid=(b, s // tm),
        in_specs=[
            pl.BlockSpec((1, tm, d), tok),
            pl.BlockSpec((1, 1, 6 * d), lambda bi, j: (jnp.where(j >= n_lat, b, bi), 0, 0)),
            _const_spec((1, d)),
            _const_spec((d, nw)),
            _const_spec((1, BRANCH_W)),
            _const_spec((1, LANES)),
            _const_spec((BRANCH_W, BRANCH_W)),
            _const_spec((GROUP_W, 2 * GROUP_W)),
            pl.BlockSpec((tm, LANES), tab),
            pl.BlockSpec((tm, LANES), tab),
            pl.BlockSpec((tm, LANES), tab),
        ],
        out_specs=[pl.BlockSpec((1, tm, wd), tok) for wd in widths],
        out_shape=[jax.ShapeDtypeStruct((b, s, wd), BF16) for wd in widths],
        compiler_params=_cparams(("parallel", "arbitrary"), 48),
        name="inproj",
    )(xa, mod, n1, w_in, qg, kg, seg, cs, cos, sa, sb)


def _split_heads(qc, lane):
    zero = jnp.zeros_like(qc)
    return jnp.concatenate([jnp.where(lane < HEAD_DIM, qc, zero),
                            jnp.where(lane >= HEAD_DIM, qc, zero)], axis=0)


def _gattn_kernel(q_ref, kv_ref, o_ref, q2_s, m_s, acc_s, *, tq, tk, sub, t_lat, n_ctx):
    is_ctx_q = pl.program_id(1) * tq >= t_lat
    n_lat_tiles = jnp.where(is_ctx_q, 0, t_lat // tk)
    lane = lax.broadcasted_iota(jnp.int32, (1, LANES), 1)
    nt = (((1,), (1,)), ((), ()))
    n_chunks = BRANCH_W // LANES
    for c in range(n_chunks):
        q2_s[2 * c * tq:(2 * c + 2) * tq, :] = _split_heads(q_ref[0, :, c * LANES:(c + 1) * LANES], lane)

    def tile(k, v, first):
        for r in range(2 * n_chunks * tq // sub):
            rows = slice(r * sub, (r + 1) * sub)
            s = lax.dot_general(q2_s[rows, :], k, nt, preferred_element_type=F32)
            rm = jnp.max(s, axis=-1, keepdims=True)
            if first:
                m_new = rm
                acc_s[rows, :] = jnp.dot(jnp.exp2(s - m_new).astype(BF16), v, preferred_element_type=F32)
            else:
                m_old = m_s[rows, :]
                m_new = jnp.maximum(m_old, rm)
                alpha = jnp.exp2(m_old - m_new)
                pv = jnp.dot(jnp.exp2(s - m_new).astype(BF16), v, preferred_element_type=F32)
                acc_s[rows, :] = alpha * acc_s[rows, :] + pv
            m_s[rows, :] = m_new

    tile(kv_ref[0, t_lat:t_lat + n_ctx, 0:LANES], kv_ref[0, t_lat:t_lat + n_ctx, LANES:3 * LANES], True)

    def body(i, carry):
        ks = pl.multiple_of(i * tk, tk)
        tile(kv_ref[0, pl.ds(ks, tk), 0:LANES], kv_ref[0, pl.ds(ks, tk), LANES:3 * LANES], False)
        return carry

    lax.fori_loop(0, n_lat_tiles, body, 0)
    for c in range(n_chunks):
        lo = acc_s[2 * c * tq:(2 * c + 1) * tq, :]
        hi = acc_s[(2 * c + 1) * tq:(2 * c + 2) * tq, :]
        o = jnp.where(lane < HEAD_DIM, lo[:, :LANES] / lo[:, LANES:], hi[:, :LANES] / hi[:, LANES:])
        o_ref[0, :, c * LANES:(c + 1) * LANES] = o.astype(BF16)


def _gattn(qb, kv, s_out, t_lat, tq=256, tk=2048, sub=128):
    b, s, _ = qb.shape
    tk = min(tk, t_lat)
    rows = 2 * tq * (BRANCH_W // LANES)
    return pl.pallas_call(
        functools.partial(_gattn_kernel, tq=tq, tk=tk, sub=sub, t_lat=t_lat, n_ctx=s - t_lat),
        grid=(b, s_out // tq),
        in_specs=[
            pl.BlockSpec((1, tq, BRANCH_W), lambda bi, j: (bi, j, 0)),
            pl.BlockSpec((1, s, 3 * LANES), lambda bi, j: (bi, 0, 0)),
        ],
        out_specs=pl.BlockSpec((1, tq, BRANCH_W), lambda bi, j: (bi, j, 0)),
        out_shape=jax.ShapeDtypeStruct((b, s_out, BRANCH_W), BF16),
        scratch_shapes=[pltpu.VMEM((rows, LANES), BF16), pltpu.VMEM((rows, 1), F32),
                        pltpu.VMEM((rows, 2 * LANES), F32)],
        compiler_params=_cparams(("parallel", "arbitrary"), 48),
        name="gattn",
    )(qb, kv)


def _window_bias(n_ctx):
    qi = np.arange(QBLK)[:, None]
    kj = np.arange(3 * QBLK)[None, :]
    band = np.abs(kj - WINDOW - qi) <= WINDOW
    blk = kj // QBLK
    variants = [band & (blk != 0), band, band & (blk != 2), np.zeros_like(band)]
    out = [np.concatenate([np.ones((QBLK, n_ctx), bool), v], axis=1) for v in variants]
    bias = np.where(np.stack(out), 0.0, NEG_BIG).astype(np.float32)
    return np.concatenate([bias, bias], axis=1)


def _wattn_kernel(sink_ref, bias_ref, q_ref, kc_ref, k0_ref, k1_ref, k2_ref, o_ref):
    blocks = (kc_ref, k0_ref, k1_ref, k2_ref)
    kall = jnp.concatenate([r[0, :, 0:LANES] for r in blocks], axis=0)
    vall = jnp.concatenate([r[0, :, LANES:3 * LANES] for r in blocks], axis=0)
    bias = bias_ref[0]
    lane = lax.broadcasted_iota(jnp.int32, (1, LANES), 1)
    row = lax.broadcasted_iota(jnp.int32, (2 * QBLK, 1), 0)
    nt = (((1,), (1,)), ((), ()))
    for c in range(BRANCH_W // LANES):
        q2 = _split_heads(q_ref[0, :, c * LANES:(c + 1) * LANES], lane)
        s = lax.dot_general(q2, kall, nt, preferred_element_type=F32) + bias
        sk = jnp.where(row < QBLK, sink_ref[2 * c], sink_ref[2 * c + 1])
        m = jnp.maximum(jnp.max(s, axis=-1, keepdims=True), sk)
        pv = jnp.dot(jnp.exp2(s - m).astype(BF16), vall, preferred_element_type=F32)
        o2 = pv[:, :LANES] / (pv[:, LANES:] + jnp.exp2(sk - m))
        o_ref[0, :, c * LANES:(c + 1) * LANES] = jnp.where(lane < HEAD_DIM, o2[:QBLK], o2[QBLK:]).astype(BF16)


def _wattn(sink, bias, qw, kv, s_out, t_lat):
    b, s, _ = qw.shape
    n_ctx = s - t_lat
    last = s // QBLK - 1
    n_lat = t_lat // QBLK
    variant = lambda j: jnp.where(j >= n_lat, 3, jnp.where(j == 0, 0, jnp.where(j == n_lat - 1, 2, 1)))
    return pl.pallas_call(
        _wattn_kernel,
        grid=(b, s_out // QBLK),
        in_specs=[
            pl.BlockSpec(memory_space=pltpu.SMEM),
            pl.BlockSpec((1,) + bias.shape[1:], lambda bi, j: (variant(j), 0, 0)),
            pl.BlockSpec((1, QBLK, BRANCH_W), lambda bi, j: (bi, j, 0)),
            pl.BlockSpec((1, n_ctx, 3 * LANES), lambda bi, j: (bi, t_lat // n_ctx, 0)),
            pl.BlockSpec((1, QBLK, 3 * LANES), lambda bi, j: (bi, jnp.maximum(j - 1, 0), 0)),
            pl.BlockSpec((1, QBLK, 3 * LANES), lambda bi, j: (bi, j, 0)),
            pl.BlockSpec((1, QBLK, 3 * LANES), lambda bi, j: (bi, jnp.minimum(j + 1, last), 0)),
        ],
        out_specs=pl.BlockSpec((1, QBLK, BRANCH_W), lambda bi, j: (bi, j, 0)),
        out_shape=jax.ShapeDtypeStruct((b, s_out, BRANCH_W), BF16),
        compiler_params=_cparams(("parallel", "arbitrary"), 32),
        name="wattn",
    )(sink, bias, qw, kv, kv, kv, kv)


def _fourier_tables(n1, n2):
    t = n1 * n2
    k2 = np.arange(n2)[None, :, None]
    t2 = np.arange(n2)[None, None, :]
    t1 = np.arange(n1)[:, None, None]
    theta = 2.0 * np.pi * ((k2 * t2 * n1 + k2 * t1) % t) / t
    er, ei = np.cos(theta) / math.sqrt(n2), -np.sin(theta) / math.sqrt(n2)
    e = np.concatenate([np.concatenate([er, -ei], axis=2), np.concatenate([ei, er], axis=2)], axis=1)
    k1 = np.arange(n1)[:, None]
    phi = 2.0 * np.pi * ((k1 * np.arange(n1)[None, :]) % n1) / n1
    dcat = np.concatenate([np.cos(phi), np.sin(phi)], axis=1) / math.sqrt(n1)
    return e.astype(np.float32), dcat.astype(np.float32)


def _channel_dft_table():
    c = np.arange(GROUP_W)
    ang = 2.0 * np.pi * ((c[:, None] * c[None, :]) % GROUP_W) / GROUP_W
    return (np.concatenate([np.cos(ang), -np.sin(ang)], axis=1) / math.sqrt(GROUP_W)).astype(np.float32)


def _fourier_kernel(zr_ref, zi_ref, e_ref, d_ref, o_ref, xr_s, xi_s, yr_s, yi_s, *, n1, n2):
    nc = xr_s.shape[0]
    chunk = lambda c: slice(c * LANES, (c + 1) * LANES)

    def gather(ref, start, size, stride):
        return jnp.concatenate([ref[c, pl.ds(start, size, stride=stride), :] for c in range(nc)], axis=-1)

    for c in range(nc):
        xr_s[c] = zr_ref[0, :, chunk(c)].astype(F32)
        xi_s[c] = zi_ref[0, :, chunk(c)].astype(F32)
    for t1 in range(n1):
        xs = jnp.concatenate([gather(xr_s, t1, n2, n1), gather(xi_s, t1, n2, n1)], axis=0).astype(BF16)
        y = jnp.dot(e_ref[t1], xs, preferred_element_type=F32)
        for c in range(nc):
            yr_s[c, t1 * n2:(t1 + 1) * n2, :] = y[:n2, chunk(c)]
            yi_s[c, t1 * n2:(t1 + 1) * n2, :] = y[n2:, chunk(c)]
    for k2 in range(n2):
        ys = jnp.concatenate([gather(yr_s, k2, n1, n2), gather(yi_s, k2, n1, n2)], axis=0).astype(BF16)
        o = jnp.dot(d_ref[...], ys, preferred_element_type=F32)
        for c in range(nc):
            xr_s[c, pl.ds(k2, n1, stride=n2), :] = o[:, chunk(c)]
    for c in range(nc):
        o_ref[0, :, chunk(c)] = xr_s[c].astype(BF16)


def _fourier(zr, zi, e_tab, d_tab, t_len, row_block, cw=256):
    b = zr.shape[0]
    n1 = d_tab.shape[0]
    n2 = t_len // n1
    zspec = pl.BlockSpec((1, t_len, cw), lambda bi, j: (bi, row_block, j))
    return pl.pallas_call(
        functools.partial(_fourier_kernel, n1=n1, n2=n2),
        grid=(b, BRANCH_W // cw),
        in_specs=[zspec, zspec, _const_spec(e_tab.shape), _const_spec(d_tab.shape)],
        out_specs=pl.BlockSpec((1, t_len, cw), lambda bi, j: (bi, 0, j)),
        out_shape=jax.ShapeDtypeStruct((b, t_len, BRANCH_W), BF16),
        scratch_shapes=[pltpu.VMEM((cw // LANES, t_len, LANES), F32)] * 4,
        compiler_params=_cparams(("parallel", "arbitrary"), 48),
        name="fourier",
    )(zr, zi, e_tab, d_tab)


def _pool_kernel(pc_ref, pp_ref, pn_ref, w_ref, sc_ref, o_ref, *, tp, t_lat, s_tot):
    row0 = pl.program_id(1) * tp
    in_lat = row0 < t_lat
    seq_lo = jnp.where(in_lat, 0, t_lat)
    seq_hi = jnp.where(in_lat, t_lat, s_tot)
    ext = jnp.concatenate([pp_ref[0], pc_ref[0], pn_ref[0]], axis=0).astype(F32)
    gpos = row0 - POOL_HALO + lax.broadcasted_iota(jnp.int32, (tp + 2 * POOL_HALO, 1), 0)
    ext = jnp.where((gpos >= seq_lo) & (gpos < seq_hi), ext, 0.0).astype(BF16)
    shape = (tp, tp + 2 * POOL_HALO)
    off = lax.broadcasted_iota(jnp.int32, shape, 1) - POOL_HALO - lax.broadcasted_iota(jnp.int32, shape, 0)
    pos = row0 - seq_lo + lax.broadcasted_iota(jnp.int32, (tp, 1), 0)
    n = seq_hi - seq_lo
    outs = []
    for gi, w in enumerate(POOL_WINDOWS):
        cols = slice(gi * GROUP_W, (gi + 1) * GROUP_W)
        band = jnp.where((off >= -(w // 2)) & (off <= w // 2 - 1), 1.0, 0.0).astype(BF16)
        wsum = jnp.dot(band, ext[:, cols], preferred_element_type=F32)
        cnt = jnp.minimum(pos + w // 2, n) - jnp.maximum(pos - w // 2, 0)
        pooled = wsum / cnt.astype(F32) - pc_ref[0, :, cols].astype(F32)
        outs.append(jnp.dot(pooled.astype(BF16), w_ref[gi], preferred_element_type=F32))
    o_ref[0] = (jnp.concatenate(outs, axis=-1) * sc_ref[...]).astype(BF16)


def _pool(p, pool_w, pool_scale, s_out, t_lat, tp=256):
    b, s, _ = p.shape
    hb = tp // POOL_HALO
    last = s // POOL_HALO - 1
    return pl.pallas_call(
        functools.partial(_pool_kernel, tp=tp, t_lat=t_lat, s_tot=s),
        grid=(b, s_out // tp),
        in_specs=[
            pl.BlockSpec((1, tp, BRANCH_W), lambda bi, j: (bi, j, 0)),
            pl.BlockSpec((1, POOL_HALO, BRANCH_W), lambda bi, j: (bi, jnp.maximum(j * hb - 1, 0), 0)),
            pl.BlockSpec((1, POOL_HALO, BRANCH_W), lambda bi, j: (bi, jnp.minimum((j + 1) * hb, last), 0)),
            _const_spec(pool_w.shape),
            _const_spec((1, BRANCH_W)),
        ],
        out_specs=pl.BlockSpec((1, tp, BRANCH_W), lambda bi, j: (bi, j, 0)),
        out_shape=jax.ShapeDtypeStruct((b, s_out, BRANCH_W), BF16),
        compiler_params=_cparams(("parallel", "arbitrary"), 32),
        name="pool",
    )(p, p, p, pool_w, pool_scale)


def _route(logits_t, bias):
    aff = jax.nn.sigmoid(logits_t)
    sel = aff + bias
    neg = -jnp.inf
    firsts, seconds, scores = [], [], []
    for g in range(N_EXPERTS // EXPERTS_PER_GROUP):
        s = [sel[EXPERTS_PER_GROUP * g + k:EXPERTS_PER_GROUP * g + k + 1, :] for k in range(EXPERTS_PER_GROUP)]
        m1 = jnp.maximum(jnp.maximum(s[0], s[1]), jnp.maximum(s[2], s[3]))
        i1 = jnp.where(s[0] == m1, 0, jnp.where(s[1] == m1, 1, jnp.where(s[2] == m1, 2, 3)))
        r = [jnp.where(i1 == k, neg, s[k]) for k in range(EXPERTS_PER_GROUP)]
        m2 = jnp.maximum(jnp.maximum(r[0], r[1]), jnp.maximum(r[2], r[3]))
        i2 = jnp.where(r[0] == m2, 0, jnp.where(r[1] == m2, 1, jnp.where(r[2] == m2, 2, 3)))
        firsts.append(i1 + EXPERTS_PER_GROUP * g)
        seconds.append(i2 + EXPERTS_PER_GROUP * g)
        scores.append(m1 + m2)
    best = jnp.maximum(jnp.maximum(scores[0], scores[1]), jnp.maximum(scores[2], scores[3]))
    pick = lambda v: jnp.where(scores[0] == best, v[0], jnp.where(scores[1] == best, v[1],
                                                                 jnp.where(scores[2] == best, v[2], v[3])))
    e1, e2 = pick(firsts), pick(seconds)
    eidx = lax.broadcasted_iota(jnp.int32, aff.shape, 0)
    a1 = jnp.sum(jnp.where(eidx == e1, aff, 0.0), axis=0, keepdims=True)
    a2 = jnp.sum(jnp.where(eidx == e2, aff, 0.0), axis=0, keepdims=True)
    tot = a1 + a2
    w1, w2 = a1 / tot, a2 / tot
    swap = e1 > e2
    lo = jnp.where(swap, e2, e1) & (EXPERTS_PER_GROUP - 1)
    hi = jnp.where(swap, e1, e2) & (EXPERTS_PER_GROUP - 1)
    pair = jnp.where(lo == 0, 0, jnp.where(lo == 1, 3, 5)) + hi - lo - 1
    cls = (e1 >> 2) * PAIRS_PER_GROUP + pair
    rows = [cls.astype(F32), jnp.where(swap, w2, w1), jnp.where(swap, w1, w2)]
    return jnp.concatenate(rows + [jnp.zeros_like(w1)] * (ROUTE_ROWS - len(rows)), axis=0)


def _pack_bf16_pairs(v):
    w = v.shape[1] // 2
    bits = pltpu.bitcast(v.astype(BF16).astype(F32), jnp.uint32)
    return pltpu.bitcast(bits[:, :w] | (bits[:, w:] >> 16), jnp.int32)


def _unpack_bf16_pairs(p):
    bits = pltpu.bitcast(p, jnp.uint32)
    hi = pltpu.bitcast(bits & jnp.uint32(0xFFFF0000), F32)
    lo = pltpu.bitcast(bits << 16, F32)
    return jnp.concatenate([hi, lo], axis=-1)


def _merge_kernel(x_ref, mod_ref, n1_ref, n2_ref, a_ref, b_ref, c_ref, d_ref,
                  wg_ref, bg_ref, wb_ref, wo_ref, rw_ref, rb_ref,
                  xo_ref, h2_ref, route_ref, merged_s, *, nw, sub):
    sh1, sc1, g1, sh2, sc2, _ = _mod_slices(mod_ref[0])
    for r in range(x_ref.shape[1] // sub):
        rows = slice(r * sub, (r + 1) * sub)
        x = x_ref[0, rows, :]
        hb = _norm_mod(x, n1_ref[...], sh1, sc1).astype(BF16)
        for n in range(D_MODEL // nw):
            cols = slice(n * nw, (n + 1) * nw)
            merged = None
            for i, br in enumerate((a_ref, b_ref, c_ref, d_ref)):
                gate = jax.nn.sigmoid(
                    jnp.dot(hb, wg_ref[i, :, cols], preferred_element_type=F32) + bg_ref[i, :, cols])
                term = gate * jnp.dot(br[0, rows, :], wb_ref[i, :, cols], preferred_element_type=F32)
                merged = term if merged is None else merged + term
            merged_s[rows, cols] = merged.astype(BF16)
        y = jnp.dot(merged_s[rows, :], wo_ref[...], preferred_element_type=F32)
        xn = x + g1 * y
        xo_ref[0, rows, :] = xn
        h2 = _norm_mod(xn, n2_ref[...], sh2, sc2)
        h2_ref[0, rows, :] = _pack_bf16_pairs(h2)
        h_hi = h2.astype(BF16)
        h_lo = (h2 - h_hi.astype(F32)).astype(BF16)
        nt = (((1,), (1,)), ((), ()))
        by_hi = lax.dot_general(rw_ref[...], h_hi, nt, preferred_element_type=F32)
        by_lo = lax.dot_general(rw_ref[:N_EXPERTS, :], h_lo, nt, preferred_element_type=F32)
        logits_t = by_hi[:N_EXPERTS] + by_hi[N_EXPERTS:] + by_lo
        route_ref[0, :, rows] = _route(logits_t, rb_ref[...])


def _merge(xa, mod, n1, n2, br_a, br_b, br_c, br_d, wg, bg, wb, wo, rw_t, rb, s_out, t_lat, tm=256):
    b, s, d = xa.shape
    n_lat = t_lat // tm
    tok = lambda bi, j: (bi, j, 0)
    br_spec = pl.BlockSpec((1, tm, BRANCH_W), tok)
    return pl.pallas_call(
        functools.partial(_merge_kernel, nw=256, sub=tm // 2),
        scratch_shapes=[pltpu.VMEM((tm, d), BF16)],
        grid=(b, s_out // tm),
        in_specs=[
            pl.BlockSpec((1, tm, d), tok),
            pl.BlockSpec((1, 1, 6 * d), lambda bi, j: (jnp.where(j >= n_lat, b, bi), 0, 0)),
            _const_spec((1, d)), _const_spec((1, d)),
            br_spec, br_spec, br_spec, br_spec,
            _const_spec(wg.shape), _const_spec(bg.shape), _const_spec(wb.shape), _const_spec(wo.shape),
            _const_spec(rw_t.shape), _const_spec(rb.shape),
        ],
        out_specs=[
            pl.BlockSpec((1, tm, d), tok),
            pl.BlockSpec((1, tm, d // 2), tok),
            pl.BlockSpec((1, ROUTE_ROWS, tm), lambda bi, j: (bi, 0, j)),
        ],
        out_shape=[
            jax.ShapeDtypeStruct((b, s_out, d), F32),
            jax.ShapeDtypeStruct((b, s_out, d // 2), jnp.int32),
            jax.ShapeDtypeStruct((b, ROUTE_ROWS, s_out), F32),
        ],
        compiler_params=_cparams(("parallel", "arbitrary"), 56),
        name="merge",
    )(xa, mod, n1, n2, br_a, br_b, br_c, br_d, wg, bg, wb, wo, rw_t, rb)


def _rank_kernel(cls_ref, rank_ref, cnt_ref, cnt_s, *, tr):
    @pl.when(pl.program_id(0) == 0)
    def _():
        cnt_s[...] = jnp.zeros_like(cnt_s)

    cls = cls_ref[0]
    cid = lax.broadcasted_iota(jnp.int32, (CLASS_ROWS, tr), 0).astype(F32)
    onehot = cid == cls
    before = lax.broadcasted_iota(jnp.int32, (tr, tr), 0) < lax.broadcasted_iota(jnp.int32, (tr, tr), 1)
    prefix = jnp.dot(jnp.where(onehot, 1.0, 0.0).astype(BF16), jnp.where(before, 1.0, 0.0).astype(BF16),
                     preferred_element_type=F32)
    carry = cnt_s[...][:, 0:1]
    rank_ref[0] = jnp.sum(jnp.where(onehot, prefix + carry, 0.0), axis=0, keepdims=True)
    cnt_s[...] += jnp.sum(jnp.where(onehot, 1.0, 0.0), axis=1, keepdims=True)
    cnt_ref[...] = cnt_s[...]


def _rank(cls_flat, tr=512):
    n = cls_flat.shape[0]
    tr = math.gcd(n, tr)
    cls3 = cls_flat.reshape(n // tr, 1, tr)
    rank, cnt = pl.pallas_call(
        functools.partial(_rank_kernel, tr=tr),
        grid=(n // tr,),
        in_specs=[pl.BlockSpec((1, 1, tr), lambda i: (i, 0, 0))],
        out_specs=[pl.BlockSpec((1, 1, tr), lambda i: (i, 0, 0)), _const_spec((CLASS_ROWS, LANES))],
        out_shape=[jax.ShapeDtypeStruct((n // tr, 1, tr), F32), jax.ShapeDtypeStruct((CLASS_ROWS, LANES), F32)],
        scratch_shapes=[pltpu.VMEM((CLASS_ROWS, LANES), F32)],
        compiler_params=_cparams(("arbitrary",), 32),
        name="rank",
    )(cls3)
    return rank.reshape(n), cnt[:N_CLASSES, 0]


def _sc_layout(n):
    info = plsc.get_sparse_core_info()
    nw = info.num_cores * info.num_subcores
    per_worker = n // nw
    assert per_worker * nw == n
    chunk = max(c for c in range(8, SC_MAX_CHUNK + 1, 8) if per_worker % c == 0)
    return info.num_cores, nw, per_worker // chunk, chunk


def _sc_scatter_rows(src, wts, pos, n_out):
    n, w = src.shape
    nc, nw, k, c = _sc_layout(n)
    mesh = plsc.VectorSubcoreMesh(core_axis_name="c", subcore_axis_name="s")

    @functools.partial(
        pl.kernel, mesh=mesh,
        out_type=(jax.ShapeDtypeStruct((n_out, w), src.dtype), jax.ShapeDtypeStruct((n_out, wts.shape[1]), wts.dtype)),
        scratch_types=[pltpu.VMEM((k, c), jnp.int32), pltpu.VMEM((c, w), src.dtype),
                       pltpu.VMEM((c, wts.shape[1]), wts.dtype), pltpu.SemaphoreType.DMA],
        name="moe_scatter",
    )
    def scatter(src_hbm, wts_hbm, pos_hbm, out_hbm, wout_hbm, idx_v, rows_v, wrows_v, sem):
        wid = lax.axis_index("s") * nc + lax.axis_index("c")
        pltpu.sync_copy(pos_hbm.at[wid], idx_v)

        @pl.loop(0, k)
        def _(j):
            off = pl.multiple_of(wid * (k * c) + j * c, 8)
            pltpu.sync_copy(src_hbm.at[pl.ds(off, c)], rows_v)
            pltpu.sync_copy(wts_hbm.at[pl.ds(off, c)], wrows_v)
            pltpu.async_copy(rows_v, out_hbm.at[idx_v.at[j]], sem).wait()
            pltpu.async_copy(wrows_v, wout_hbm.at[idx_v.at[j]], sem).wait()

    return scatter(src, wts, pos.reshape(nw, k, c))


def _sc_gather_rows(src, pos):
    n = pos.shape[0]
    w = src.shape[1]
    nc, nw, k, c = _sc_layout(n)
    mesh = plsc.VectorSubcoreMesh(core_axis_name="c", subcore_axis_name="s")

    @functools.partial(
        pl.kernel, mesh=mesh,
        out_type=jax.ShapeDtypeStruct((n, w), src.dtype),
        scratch_types=[pltpu.VMEM((k, c), jnp.int32), pltpu.VMEM((c, w), src.dtype), pltpu.SemaphoreType.DMA],
        name="moe_gather",
    )
    def gather(src_hbm, pos_hbm, out_hbm, idx_v, rows_v, sem):
        wid = lax.axis_index("s") * nc + lax.axis_index("c")
        pltpu.sync_copy(pos_hbm.at[wid], idx_v)

        @pl.loop(0, k)
        def _(j):
            off = pl.multiple_of(wid * (k * c) + j * c, 8)
            pltpu.async_copy(src_hbm.at[idx_v.at[j]], rows_v, sem).wait()
            pltpu.sync_copy(rows_v, out_hbm.at[pl.ds(off, c)])

    return gather(src, pos.reshape(nw, k, c))


def _gmm_kernel(lo_ref, hi_ref, nact_ref, h_ref, wt_ref, w13a_ref, w13b_ref, w2a_ref, w2b_ref, o_ref):
    @pl.when(pl.program_id(0) < nact_ref[0])
    def _():
        x = _unpack_bf16_pairs(h_ref[...]).astype(BF16)
        wts = wt_ref[...]

        def expert(w13_ref, w2_ref):
            ab = jnp.dot(x, w13_ref[0], preferred_element_type=F32)
            a, gate = ab[:, :EXPERT_FF], ab[:, EXPERT_FF:]
            hid = (a * jax.nn.sigmoid(a)) * gate
            return jnp.dot(hid.astype(BF16), w2_ref[0], preferred_element_type=F32)

        y = wts[:, 0:1] * expert(w13a_ref, w2a_ref) + wts[:, 1:2] * expert(w13b_ref, w2b_ref)
        o_ref[...] = _pack_bf16_pairs(y)


def _gmm(tile_lo, tile_hi, n_act, hs, ws, w13, w2, tm):
    n_pad, half = hs.shape
    d = 2 * half
    row = lambda t, lo, hi, na: (jnp.minimum(t, na[0] - 1), 0)
    e_lo = lambda t, lo, hi, na: (lo[jnp.minimum(t, na[0] - 1)], 0, 0)
    e_hi = lambda t, lo, hi, na: (hi[jnp.minimum(t, na[0] - 1)], 0, 0)
    return pl.pallas_call(
        _gmm_kernel,
        grid_spec=pltpu.PrefetchScalarGridSpec(
            num_scalar_prefetch=3,
            grid=(n_pad // tm,),
            in_specs=[
                pl.BlockSpec((tm, half), row),
                pl.BlockSpec((tm, ws.shape[1]), row),
                pl.BlockSpec((1, d, 2 * EXPERT_FF), e_lo),
                pl.BlockSpec((1, d, 2 * EXPERT_FF), e_hi),
                pl.BlockSpec((1, EXPERT_FF, d), e_lo),
                pl.BlockSpec((1, EXPERT_FF, d), e_hi),
            ],
            out_specs=pl.BlockSpec((tm, half), row),
        ),
        out_shape=jax.ShapeDtypeStruct((n_pad, half), jnp.int32),
        compiler_params=_cparams(("arbitrary",), 48),
        name="moe_gmm",
    )(tile_lo, tile_hi, n_act, hs, ws, w13, w13, w2, w2)


def _moe_routed(h2p, route, w13, w2, tm=256):
    b, s, half = h2p.shape
    n = b * s
    n_pad = n + N_CLASSES * tm
    cls = route[:, 0, :].reshape(n)
    rank, counts = _rank(cls)
    counts = counts.astype(jnp.int32)
    padded = (counts + tm - 1) // tm * tm
    ends = jnp.cumsum(padded)
    pos = jnp.take(ends - padded, cls.astype(jnp.int32)) + rank.astype(jnp.int32)
    n_act = (ends[-1] // tm).reshape(1)
    tile_row = jnp.arange(n_pad // tm, dtype=jnp.int32) * tm
    tile_cls = jnp.minimum(jnp.sum(tile_row[:, None] >= ends[None, :], axis=1), N_CLASSES - 1)
    pair_lo, pair_hi = (jnp.asarray(a, jnp.int32) for a in _class_experts())
    wts = jnp.concatenate([route[:, 1, :].reshape(n, 1), route[:, 2, :].reshape(n, 1),
                           jnp.zeros((n, SC_WEIGHT_COLS - 2), F32)], axis=1)
    hs, ws = _sc_scatter_rows(h2p.reshape(n, half), wts, pos, n_pad)
    ys = _gmm(jnp.take(pair_lo, tile_cls), jnp.take(pair_hi, tile_cls), n_act, hs, ws, w13, w2, tm)
    return _sc_gather_rows(ys, pos).reshape(b, s, half)


def _class_experts():
    lo, hi = [], []
    for g in range(N_EXPERTS // EXPERTS_PER_GROUP):
        for i in range(EXPERTS_PER_GROUP):
            for j in range(i + 1, EXPERTS_PER_GROUP):
                lo.append(EXPERTS_PER_GROUP * g + i)
                hi.append(EXPERTS_PER_GROUP * g + j)
    return np.array(lo), np.array(hi)


def _residual_kernel(x_ref, y_ref, mod_ref, g_ref, o_ref, *, final):
    g2 = _mod_slices(mod_ref[0])[5]
    x = x_ref[0] + g2 * _unpack_bf16_pairs(y_ref[0])
    if final:
        x = x * lax.rsqrt(jnp.mean(x * x, axis=-1, keepdims=True) + EPS) * g_ref[...]
    o_ref[0] = x


def _residual(x1, yp, mod, gain, t_lat, final, tm=256):
    b, s, d = x1.shape
    n_lat = t_lat // tm
    tok = lambda bi, j: (bi, j, 0)
    return pl.pallas_call(
        functools.partial(_residual_kernel, final=final),
        grid=(b, s // tm),
        in_specs=[
            pl.BlockSpec((1, tm, d), tok),
            pl.BlockSpec((1, tm, d // 2), tok),
            pl.BlockSpec((1, 1, 6 * d), lambda bi, j: (jnp.where(j >= n_lat, b, bi), 0, 0)),
            _const_spec((1, d)),
        ],
        out_specs=pl.BlockSpec((1, tm, d), tok),
        out_shape=jax.ShapeDtypeStruct((b, s, d), F32),
        compiler_params=_cparams(("parallel", "arbitrary"), 32),
        name="residual",
    )(x1, yp, mod, gain)


def _rope_tables(t_lat, n_ctx):
    rows = t_lat // GRID_W
    row = jnp.repeat(jnp.arange(rows, dtype=F32), GRID_W)
    col = jnp.tile(jnp.arange(GRID_W, dtype=F32), rows)
    inv_freq = ROPE_THETA ** (-jnp.arange(0, ROPE_AXIS_DIM, 2, dtype=F32) / ROPE_AXIS_DIM)
    ang = jnp.stack([row[:, None] * inv_freq, col[:, None] * inv_freq], axis=1)
    cos, sin = jnp.cos(ang), jnp.sin(ang)
    zero = jnp.zeros_like(sin)
    cos_h = jnp.concatenate([cos, cos], axis=-1).reshape(t_lat, HEAD_DIM)
    sa_h = jnp.concatenate([-sin, zero], axis=-1).reshape(t_lat, HEAD_DIM)
    sb_h = jnp.concatenate([zero, sin], axis=-1).reshape(t_lat, HEAD_DIM)
    reps = LANES // HEAD_DIM
    pad = lambda tbl, fill: jnp.concatenate(
        [jnp.tile(tbl, (1, reps)), jnp.full((n_ctx, LANES), fill, F32)], axis=0)
    return pad(cos_h, 1.0), pad(sa_h, 0.0), pad(sb_h, 0.0)


def _permute_heads(w, axis):
    shp = w.shape
    w = w.reshape(shp[:axis] + (N_Q_HEADS, HEAD_DIM) + shp[axis + 1:])
    w = jnp.take(w, jnp.array(HEAD_PERM), axis=axis)
    return w.reshape(shp)


def _square_factor(n):
    r = int(round(math.sqrt(n)))
    assert r * r == n, "sequence lengths must be perfect squares for the two-stage DFT"
    return r


def kernel(x, c, ctx, c_ctx, w_ada, b_ada, norm1, norm2, w_in, q_gain, k_gain, sink, pool_w, pool_scale,
           w_branch, w_gate, b_gate, w_out, router_w, router_bias, w1, w3, w2, norm_f):
    b, t_lat, d = x.shape
    n_ctx = ctx.shape[1]
    s = t_lat + n_ctx
    depth = w_ada.shape[0]
    assert d == D_MODEL and b < MOD_ROWS and t_lat % 256 == 0 and n_ctx % 256 == 0 and t_lat % n_ctx == 0

    xa = jnp.concatenate([x, ctx], axis=1)
    cc = jnp.zeros((MOD_ROWS, d), F32).at[:b].set(c).at[b].set(c_ctx)
    mod_all = _ada(cc, w_ada, b_ada).reshape(depth, MOD_ROWS, 1, 6 * d)

    cos, sa, sb = _rope_tables(t_lat, n_ctx)
    seg = jnp.asarray(np.kron(np.eye(N_Q_HEADS), np.full((HEAD_DIM, HEAD_DIM), 1.0 / HEAD_DIM)), BF16)
    cs = jnp.asarray(_channel_dft_table()).astype(BF16)
    f_lat = [jnp.asarray(a).astype(BF16) for a in _fourier_tables(*(_square_factor(t_lat),) * 2)]
    f_ctx = [jnp.asarray(a).astype(BF16) for a in _fourier_tables(*(_square_factor(n_ctx),) * 2)]
    wbias = jnp.asarray(_window_bias(n_ctx))
    rw_hi = router_w.T.astype(BF16)
    rw_t = jnp.concatenate([rw_hi, (router_w.T - rw_hi.astype(F32)).astype(BF16)], axis=0)
    rb = router_bias.reshape(N_EXPERTS, 1)

    for l in range(depth):
        need_ctx = l < depth - 1
        s_out = s if need_ctx else t_lat
        cols = jnp.split(w_in[l], np.cumsum((512, 512, 512, 128, 128, 512, 128))[:], axis=1)
        f_w, p_w, qb_w, kb_w, vb_w, qw_w, kw_w, vw_w = cols
        w_in_l = jnp.concatenate([f_w, p_w, _permute_heads(qb_w, 1), _permute_heads(qw_w, 1),
                                  kb_w, vb_w, kw_w, vw_w], axis=1).astype(BF16)
        wb_l = jnp.stack([w_branch[l, 0], _permute_heads(w_branch[l, 1], 0),
                          _permute_heads(w_branch[l, 2], 0), w_branch[l, 3]]).astype(BF16)
        mod = mod_all[l]
        n1 = norm1[l].reshape(1, d)
        n2 = norm2[l].reshape(1, d)
        qg = jnp.tile(q_gain[l], N_Q_HEADS).reshape(1, BRANCH_W)
        kg = jnp.tile(k_gain[l], LANES // HEAD_DIM).reshape(1, LANES)

        zr, zi, p_in, qb, qw, kvb, kvw = _inproj(xa, mod, n1, w_in_l, qg, kg, seg, cs, cos, sa, sb, t_lat)

        out_b = _gattn(qb, kvb, s_out, t_lat)
        out_c = _wattn(jnp.take(sink[l], jnp.array(HEAD_PERM)) * LOG2E, wbias, qw, kvw, s_out, t_lat)
        out_a = _fourier(zr, zi, f_lat[0], f_lat[1], t_lat, 0)
        if need_ctx:
            out_a = jnp.concatenate(
                [out_a, _fourier(zr, zi, f_ctx[0], f_ctx[1], n_ctx, t_lat // n_ctx)], axis=1)
        out_d = _pool(p_in, pool_w[l].astype(BF16), pool_scale[l].reshape(1, BRANCH_W), s_out, t_lat)

        x1, h2p, route = _merge(xa, mod, n1, n2, out_a, out_b, out_c, out_d,
                                w_gate[l].astype(BF16), b_gate[l].reshape(4, 1, d), wb_l,
                                w_out[l].astype(BF16), rw_t, rb, s_out, t_lat)
        w13 = jnp.concatenate([w1[l], w3[l]], axis=-1).astype(BF16)
        yp = _moe_routed(h2p, route, w13, w2[l].astype(BF16))
        xa = _residual(x1, yp, mod, norm_f.reshape(1, d), t_lat, final=(l == depth - 1))

    return xa
```

```python
import functools
import math

import numpy as np
import jax
import jax.numpy as jnp
from jax import lax
from jax.experimental import pallas as pl
from jax.experimental.pallas import tpu as pltpu
from jax.experimental.pallas import tpu_sc as plsc

F32 = jnp.float32
BF16 = jnp.bfloat16

D_MODEL = 1024
HEAD_DIM = 64
N_Q_HEADS = 8
N_KV_HEADS = 2
GRID_W = 64
ROPE_THETA = 10000.0
ROPE_AXIS_DIM = HEAD_DIM // 2
QBLK = 128
WINDOW = 128
BRANCH_W = 512
GROUP_W = 128
POOL_WINDOWS = (2, 4, 8, 16)
N_EXPERTS = 16
EXPERTS_PER_GROUP = 4
EXPERT_FF = 512
EPS = 1e-6
MOD_ROWS = 16
NEG_BIG = -1e30
LOG2E = math.log2(math.e)
LANES = 128
POOL_HALO = 16
PAIRS_PER_GROUP = 6
N_CLASSES = 24
CLASS_ROWS = 32
ROUTE_ROWS = 8
SC_MAX_CHUNK = 128
SC_WEIGHT_COLS = 128

HEAD_PERM = (0, 4, 1, 5, 2, 6, 3, 7)


def _cparams(sem, vmem_mb):
    return pltpu.CompilerParams(dimension_semantics=sem, vmem_limit_bytes=vmem_mb * 1024 * 1024)


def _const_spec(shape):
    nd = len(shape)
    return pl.BlockSpec(shape, lambda *_: (0,) * nd)


def _ada_kernel(c_ref, w_ref, b_ref, o_ref):
    c = c_ref[...]
    s = c * jax.nn.sigmoid(c)
    o_ref[0] = jnp.dot(s.astype(BF16), w_ref[0].astype(BF16), preferred_element_type=F32) + b_ref[0]


def _ada(cc, w_ada, b_ada):
    depth, d, n = w_ada.shape
    tn = 1536
    return pl.pallas_call(
        _ada_kernel,
        grid=(depth, n // tn),
        in_specs=[
            pl.BlockSpec((MOD_ROWS, d), lambda l, j: (0, 0)),
            pl.BlockSpec((1, d, tn), lambda l, j: (l, 0, j)),
            pl.BlockSpec((1, 1, tn), lambda l, j: (l, 0, j)),
        ],
        out_specs=pl.BlockSpec((1, MOD_ROWS, tn), lambda l, j: (l, 0, j)),
        out_shape=jax.ShapeDtypeStruct((depth, MOD_ROWS, n), F32),
        compiler_params=_cparams(("arbitrary", "arbitrary"), 40),
        name="ada",
    )(cc, w_ada, b_ada.reshape(depth, 1, n))


def _norm_mod(x, gain, shift, scale):
    ms = jnp.mean(x * x, axis=-1, keepdims=True)
    return (x * lax.rsqrt(ms + EPS) * gain) * (1.0 + scale) + shift


def _mod_slices(m):
    d = D_MODEL
    return [m[:, i * d:(i + 1) * d] for i in range(6)]


def _head_norm(z, seg, gain):
    ms = jnp.dot((z * z).astype(BF16), seg, preferred_element_type=F32)
    return z * lax.rsqrt(ms + EPS) * gain


def _rope(z, cos, sin_a, sin_b):
    outs = []
    for c in range(z.shape[1] // LANES):
        zc = z[:, c * LANES:(c + 1) * LANES]
        nxt = pltpu.roll(zc, LANES - ROPE_AXIS_DIM // 2, 1)
        prv = pltpu.roll(zc, ROPE_AXIS_DIM // 2, 1)
        outs.append(zc * cos + nxt * sin_a + prv * sin_b)
    return outs[0] if len(outs) == 1 else jnp.concatenate(outs, axis=-1)


def _stream_specs(tm, d, n_lat):
    return [pl.BlockSpec((1, tm, d), lambda bi, j: (bi, jnp.minimum(j, n_lat - 1), 0)),
            pl.BlockSpec((1, tm, d), lambda bi, j: (bi, jnp.maximum(j - n_lat, 0), 0))]


def _inproj_kernel(xl_ref, xc_ref, mod_ref, n1_ref, w_ref, qg_ref, kg_ref, seg_ref, cs_ref,
                   cos_ref, sa_ref, sb_ref,
                   zr_ref, zi_ref, p_ref, qb_ref, qw_ref, kvb_ref, kvw_ref, *, n_lat):
    sh1, sc1 = _mod_slices(mod_ref[0])[:2]
    x = jnp.where(pl.program_id(1) >= n_lat, xc_ref[0], xl_ref[0])
    h = _norm_mod(x, n1_ref[...], sh1, sc1)
    u = jnp.dot(h.astype(BF16), w_ref[...], preferred_element_type=F32)
    cos, sa, sb = cos_ref[...], sa_ref[...], sb_ref[...]
    w = BRANCH_W
    f_in = u[:, 0:w].astype(BF16)
    zr, zi = [], []
    for g in range(w // GROUP_W):
        z = jnp.dot(f_in[:, g * GROUP_W:(g + 1) * GROUP_W], cs_ref[...], preferred_element_type=F32)
        zr.append(z[:, :GROUP_W])
        zi.append(z[:, GROUP_W:])
    zr_ref[0] = jnp.concatenate(zr, axis=-1).astype(BF16)
    zi_ref[0] = jnp.concatenate(zi, axis=-1).astype(BF16)
    p_ref[0] = u[:, w:2 * w].astype(BF16)
    seg = seg_ref[...]
    qb = _rope(_head_norm(u[:, 2 * w:3 * w], seg, qg_ref[...]), cos, sa, sb)
    qb_ref[0] = (qb * (HEAD_DIM ** -0.5 * LOG2E)).astype(BF16)
    qw = _rope(u[:, 3 * w:4 * w], cos, sa, sb)
    qw_ref[0] = (qw * (HEAD_DIM ** -0.5 * LOG2E)).astype(BF16)
    o = 4 * w
    kb = _rope(_head_norm(u[:, o:o + LANES], seg[:LANES, :LANES], kg_ref[...]), cos, sa, sb)
    vb = u[:, o + LANES:o + 2 * LANES]
    kw = _rope(u[:, o + 2 * LANES:o + 3 * LANES], cos, sa, sb)
    vw = u[:, o + 3 * LANES:o + 4 * LANES]
    kvb_ref[0] = jnp.concatenate([kb, vb, jnp.ones_like(vb)], axis=-1).astype(BF16)
    kvw_ref[0] = jnp.concatenate([kw, vw, jnp.ones_like(vw)], axis=-1).astype(BF16)


def _inproj(xl, xc, mod, n1, w_in, qg, kg, seg, cs, cos, sa, sb, tm=256):
    b, t_lat, d = xl.shape
    s = t_lat + xc.shape[1]
    nw = w_in.shape[1]
    n_lat = t_lat // tm
    tok = lambda bi, j: (bi, j, 0)
    tab = lambda bi, j: (j, 0)
    widths = (BRANCH_W,) * 5 + (3 * LANES, 3 * LANES)
    return pl.pallas_call(
        functools.partial(_inproj_kernel, n_lat=n_lat),
        grid=(b, s // tm),
        in_specs=_stream_specs(tm, d, n_lat) + [
            pl.BlockSpec((1, 1, 6 * d), lambda bi, j: (jnp.where(j >= n_lat, b, bi), 0, 0)),
            _const_spec((1, d)),
            _const_spec((d, nw)),
            _const_spec((1, BRANCH_W)),
            _const_spec((1, LANES)),
            _const_spec((BRANCH_W, BRANCH_W)),
            _const_spec((GROUP_W, 2 * GROUP_W)),
            pl.BlockSpec((tm, LANES), tab),
            pl.BlockSpec((tm, LANES), tab),
            pl.BlockSpec((tm, LANES), tab),
        ],
        out_specs=[pl.BlockSpec((1, tm, wd), tok) for wd in widths],
        out_shape=[jax.ShapeDtypeStruct((b, s, wd), BF16) for wd in widths],
        compiler_params=_cparams(("parallel", "arbitrary"), 48),
        name="inproj",
    )(xl, xc, mod, n1, w_in, qg, kg, seg, cs, cos, sa, sb)


def _split_heads(qc, lane):
    zero = jnp.zeros_like(qc)
    return jnp.concatenate([jnp.where(lane < HEAD_DIM, qc, zero),
                            jnp.where(lane >= HEAD_DIM, qc, zero)], axis=0)


def _gattn_kernel(q_ref, kv_ref, o_ref, q2_s, m_s, acc_s, *, tq, tk, sub, t_lat, n_ctx):
    is_ctx_q = pl.program_id(1) * tq >= t_lat
    n_lat_tiles = jnp.where(is_ctx_q, 0, t_lat // tk)
    lane = lax.broadcasted_iota(jnp.int32, (1, LANES), 1)
    nt = (((1,), (1,)), ((), ()))
    n_chunks = BRANCH_W // LANES
    for c in range(n_chunks):
        q2_s[2 * c * tq:(2 * c + 2) * tq, :] = _split_heads(q_ref[0, :, c * LANES:(c + 1) * LANES], lane)

    def tile(k, v, first):
        for r in range(2 * n_chunks * tq // sub):
            rows = slice(r * sub, (r + 1) * sub)
            s = lax.dot_general(q2_s[rows, :], k, nt, preferred_element_type=F32)
            rm = jnp.max(s, axis=-1, keepdims=True)
            if first:
                m_new = rm
                acc_s[rows, :] = jnp.dot(jnp.exp2(s - m_new).astype(BF16), v, preferred_element_type=F32)
            else:
                m_old = m_s[rows, :]
                m_new = jnp.maximum(m_old, rm)
                alpha = jnp.exp2(m_old - m_new)
                pv = jnp.dot(jnp.exp2(s - m_new).astype(BF16), v, preferred_element_type=F32)
                acc_s[rows, :] = alpha * acc_s[rows, :] + pv
            m_s[rows, :] = m_new

    tile(kv_ref[0, t_lat:t_lat + n_ctx, 0:LANES], kv_ref[0, t_lat:t_lat + n_ctx, LANES:3 * LANES], True)

    def body(i, carry):
        ks = pl.multiple_of(i * tk, tk)
        tile(kv_ref[0, pl.ds(ks, tk), 0:LANES], kv_ref[0, pl.ds(ks, tk), LANES:3 * LANES], False)
        return carry

    lax.fori_loop(0, n_lat_tiles, body, 0)
    for c in range(n_chunks):
        lo = acc_s[2 * c * tq:(2 * c + 1) * tq, :]
        hi = acc_s[(2 * c + 1) * tq:(2 * c + 2) * tq, :]
        o = jnp.where(lane < HEAD_DIM, lo[:, :LANES] / lo[:, LANES:], hi[:, :LANES] / hi[:, LANES:])
        o_ref[0, :, c * LANES:(c + 1) * LANES] = o.astype(BF16)


def _gattn(qb, kv, s_out, t_lat, tq=256, tk=2048, sub=128):
    b, s, _ = qb.shape
    tk = min(tk, t_lat)
    rows = 2 * tq * (BRANCH_W // LANES)
    return pl.pallas_call(
        functools.partial(_gattn_kernel, tq=tq, tk=tk, sub=sub, t_lat=t_lat, n_ctx=s - t_lat),
        grid=(b, s_out // tq),
        in_specs=[
            pl.BlockSpec((1, tq, BRANCH_W), lambda bi, j: (bi, j, 0)),
            pl.BlockSpec((1, s, 3 * LANES), lambda bi, j: (bi, 0, 0)),
        ],
        out_specs=pl.BlockSpec((1, tq, BRANCH_W), lambda bi, j: (bi, j, 0)),
        out_shape=jax.ShapeDtypeStruct((b, s_out, BRANCH_W), BF16),
        scratch_shapes=[pltpu.VMEM((rows, LANES), BF16), pltpu.VMEM((rows, 1), F32),
                        pltpu.VMEM((rows, 2 * LANES), F32)],
        compiler_params=_cparams(("parallel", "arbitrary"), 48),
        name="gattn",
    )(qb, kv)


WATTN_QBLOCKS = 2


def _window_bias(n_ctx):
    tq = WATTN_QBLOCKS * QBLK
    qi = np.arange(tq)[:, None]
    kj = np.arange(tq + 2 * QBLK)[None, :]
    band = np.abs(kj - WINDOW - qi) <= WINDOW
    blk = kj // QBLK
    variants = [band & (blk != 0), band, band & (blk != WATTN_QBLOCKS + 1), np.zeros_like(band)]
    out = [np.concatenate([np.ones((tq, n_ctx), bool), v], axis=1) for v in variants]
    bias = np.where(np.stack(out), 0.0, NEG_BIG).astype(np.float32)
    return np.concatenate([bias, bias], axis=1)


def _wattn_kernel(sink_ref, bias_ref, q_ref, *refs):
    blocks, o_ref = refs[:-1], refs[-1]
    tq = q_ref.shape[1]
    kall = jnp.concatenate([r[0, :, 0:LANES] for r in blocks], axis=0)
    vall = jnp.concatenate([r[0, :, LANES:3 * LANES] for r in blocks], axis=0)
    bias = bias_ref[0]
    lane = lax.broadcasted_iota(jnp.int32, (1, LANES), 1)
    row = lax.broadcasted_iota(jnp.int32, (2 * tq, 1), 0)
    nt = (((1,), (1,)), ((), ()))
    for c in range(BRANCH_W // LANES):
        q2 = _split_heads(q_ref[0, :, c * LANES:(c + 1) * LANES], lane)
        s = lax.dot_general(q2, kall, nt, preferred_element_type=F32) + bias
        sk = jnp.where(row < tq, sink_ref[2 * c], sink_ref[2 * c + 1])
        m = jnp.maximum(jnp.max(s, axis=-1, keepdims=True), sk)
        pv = jnp.dot(jnp.exp2(s - m).astype(BF16), vall, preferred_element_type=F32)
        o2 = pv[:, :LANES] / (pv[:, LANES:] + jnp.exp2(sk - m))
        o_ref[0, :, c * LANES:(c + 1) * LANES] = jnp.where(lane < HEAD_DIM, o2[:tq], o2[tq:]).astype(BF16)


def _wattn(sink, bias, qw, kv, s_out, t_lat):
    b, s, _ = qw.shape
    n_ctx = s - t_lat
    nq = WATTN_QBLOCKS
    tq = nq * QBLK
    assert n_ctx % tq == 0 and t_lat // tq >= 2
    last = s // QBLK - 1
    n_lat = t_lat // tq
    variant = lambda j: jnp.where(j >= n_lat, 3, jnp.where(j == 0, 0, jnp.where(j == n_lat - 1, 2, 1)))
    key_block = lambda off: pl.BlockSpec(
        (1, QBLK, 3 * LANES), lambda bi, j: (bi, jnp.clip(j * nq + off, 0, last), 0))
    return pl.pallas_call(
        _wattn_kernel,
        grid=(b, s_out // tq),
        in_specs=[
            pl.BlockSpec(memory_space=pltpu.SMEM),
            pl.BlockSpec((1,) + bias.shape[1:], lambda bi, j: (variant(j), 0, 0)),
            pl.BlockSpec((1, tq, BRANCH_W), lambda bi, j: (bi, j, 0)),
            pl.BlockSpec((1, n_ctx, 3 * LANES), lambda bi, j: (bi, t_lat // n_ctx, 0)),
        ] + [key_block(off) for off in range(-1, nq + 1)],
        out_specs=pl.BlockSpec((1, tq, BRANCH_W), lambda bi, j: (bi, j, 0)),
        out_shape=jax.ShapeDtypeStruct((b, s_out, BRANCH_W), BF16),
        compiler_params=_cparams(("parallel", "arbitrary"), 32),
        name="wattn",
    )(sink, bias, qw, *([kv] * (nq + 3)))


def _fourier_tables(n1, n2):
    t = n1 * n2
    k2 = np.arange(n2)[None, :, None]
    t2 = np.arange(n2)[None, None, :]
    t1 = np.arange(n1)[:, None, None]
    theta = 2.0 * np.pi * ((k2 * t2 * n1 + k2 * t1) % t) / t
    er, ei = np.cos(theta) / math.sqrt(n2), -np.sin(theta) / math.sqrt(n2)
    e = np.concatenate([np.concatenate([er, -ei], axis=2), np.concatenate([ei, er], axis=2)], axis=1)
    k1 = np.arange(n1)[:, None]
    phi = 2.0 * np.pi * ((k1 * np.arange(n1)[None, :]) % n1) / n1
    dcat = np.concatenate([np.cos(phi), np.sin(phi)], axis=1) / math.sqrt(n1)
    return e.astype(np.float32), dcat.astype(np.float32)


def _channel_dft_table():
    c = np.arange(GROUP_W)
    ang = 2.0 * np.pi * ((c[:, None] * c[None, :]) % GROUP_W) / GROUP_W
    return (np.concatenate([np.cos(ang), -np.sin(ang)], axis=1) / math.sqrt(GROUP_W)).astype(np.float32)


def _fourier_kernel(zr_ref, zi_ref, e_ref, d_ref, o_ref, xr_s, xi_s, yr_s, yi_s, *, n1, n2):
    nc = xr_s.shape[0]
    chunk = lambda c: slice(c * LANES, (c + 1) * LANES)

    def gather(ref, start, size, stride):
        return jnp.concatenate([ref[c, pl.ds(start, size, stride=stride), :] for c in range(nc)], axis=-1)

    for c in range(nc):
        xr_s[c] = zr_ref[0, :, chunk(c)].astype(F32)
        xi_s[c] = zi_ref[0, :, chunk(c)].astype(F32)
    for t1 in range(n1):
        xs = jnp.concatenate([gather(xr_s, t1, n2, n1), gather(xi_s, t1, n2, n1)], axis=0).astype(BF16)
        y = jnp.dot(e_ref[t1], xs, preferred_element_type=F32)
        for c in range(nc):
            yr_s[c, t1 * n2:(t1 + 1) * n2, :] = y[:n2, chunk(c)]
            yi_s[c, t1 * n2:(t1 + 1) * n2, :] = y[n2:, chunk(c)]
    for k2 in range(n2):
        ys = jnp.concatenate([gather(yr_s, k2, n1, n2), gather(yi_s, k2, n1, n2)], axis=0).astype(BF16)
        o = jnp.dot(d_ref[...], ys, preferred_element_type=F32)
        for c in range(nc):
            xr_s[c, pl.ds(k2, n1, stride=n2), :] = o[:, chunk(c)]
    for c in range(nc):
        o_ref[0, :, chunk(c)] = xr_s[c].astype(BF16)


def _fourier(zr, zi, e_tab, d_tab, t_len, row_block, cw=256):
    b = zr.shape[0]
    n1 = d_tab.shape[0]
    n2 = t_len // n1
    zspec = pl.BlockSpec((1, t_len, cw), lambda bi, j: (bi, row_block, j))
    return pl.pallas_call(
        functools.partial(_fourier_kernel, n1=n1, n2=n2),
        grid=(b, BRANCH_W // cw),
        in_specs=[zspec, zspec, _const_spec(e_tab.shape), _const_spec(d_tab.shape)],
        out_specs=pl.BlockSpec((1, t_len, cw), lambda bi, j: (bi, 0, j)),
        out_shape=jax.ShapeDtypeStruct((b, t_len, BRANCH_W), BF16),
        scratch_shapes=[pltpu.VMEM((cw // LANES, t_len, LANES), F32)] * 4,
        compiler_params=_cparams(("parallel", "arbitrary"), 48),
        name="fourier",
    )(zr, zi, e_tab, d_tab)


def _pool_kernel(pc_ref, pp_ref, pn_ref, w_ref, sc_ref, o_ref, *, tp, t_lat, s_tot):
    row0 = pl.program_id(1) * tp
    in_lat = row0 < t_lat
    seq_lo = jnp.where(in_lat, 0, t_lat)
    seq_hi = jnp.where(in_lat, t_lat, s_tot)
    ext = jnp.concatenate([pp_ref[0], pc_ref[0], pn_ref[0]], axis=0).astype(F32)
    gpos = row0 - POOL_HALO + lax.broadcasted_iota(jnp.int32, (tp + 2 * POOL_HALO, 1), 0)
    ext = jnp.where((gpos >= seq_lo) & (gpos < seq_hi), ext, 0.0).astype(BF16)
    shape = (tp, tp + 2 * POOL_HALO)
    off = lax.broadcasted_iota(jnp.int32, shape, 1) - POOL_HALO - lax.broadcasted_iota(jnp.int32, shape, 0)
    pos = row0 - seq_lo + lax.broadcasted_iota(jnp.int32, (tp, 1), 0)
    n = seq_hi - seq_lo
    outs = []
    for gi, w in enumerate(POOL_WINDOWS):
        cols = slice(gi * GROUP_W, (gi + 1) * GROUP_W)
        band = jnp.where((off >= -(w // 2)) & (off <= w // 2 - 1), 1.0, 0.0).astype(BF16)
        wsum = jnp.dot(band, ext[:, cols], preferred_element_type=F32)
        cnt = jnp.minimum(pos + w // 2, n) - jnp.maximum(pos - w // 2, 0)
        pooled = wsum / cnt.astype(F32) - pc_ref[0, :, cols].astype(F32)
        outs.append(jnp.dot(pooled.astype(BF16), w_ref[gi], preferred_element_type=F32))
    o_ref[0] = (jnp.concatenate(outs, axis=-1) * sc_ref[...]).astype(BF16)


def _pool(p, pool_w, pool_scale, s_out, t_lat, tp=256):
    b, s, _ = p.shape
    hb = tp // POOL_HALO
    last = s // POOL_HALO - 1
    return pl.pallas_call(
        functools.partial(_pool_kernel, tp=tp, t_lat=t_lat, s_tot=s),
        grid=(b, s_out // tp),
        in_specs=[
            pl.BlockSpec((1, tp, BRANCH_W), lambda bi, j: (bi, j, 0)),
            pl.BlockSpec((1, POOL_HALO, BRANCH_W), lambda bi, j: (bi, jnp.maximum(j * hb - 1, 0), 0)),
            pl.BlockSpec((1, POOL_HALO, BRANCH_W), lambda bi, j: (bi, jnp.minimum((j + 1) * hb, last), 0)),
            _const_spec(pool_w.shape),
            _const_spec((1, BRANCH_W)),
        ],
        out_specs=pl.BlockSpec((1, tp, BRANCH_W), lambda bi, j: (bi, j, 0)),
        out_shape=jax.ShapeDtypeStruct((b, s_out, BRANCH_W), BF16),
        compiler_params=_cparams(("parallel", "arbitrary"), 32),
        name="pool",
    )(p, p, p, pool_w, pool_scale)


def _route(logits_t, bias):
    aff = jax.nn.sigmoid(logits_t)
    sel = aff + bias
    neg = -jnp.inf
    firsts, seconds, scores = [], [], []
    for g in range(N_EXPERTS // EXPERTS_PER_GROUP):
        s = [sel[EXPERTS_PER_GROUP * g + k:EXPERTS_PER_GROUP * g + k + 1, :] for k in range(EXPERTS_PER_GROUP)]
        m1 = jnp.maximum(jnp.maximum(s[0], s[1]), jnp.maximum(s[2], s[3]))
        i1 = jnp.where(s[0] == m1, 0, jnp.where(s[1] == m1, 1, jnp.where(s[2] == m1, 2, 3)))
        r = [jnp.where(i1 == k, neg, s[k]) for k in range(EXPERTS_PER_GROUP)]
        m2 = jnp.maximum(jnp.maximum(r[0], r[1]), jnp.maximum(r[2], r[3]))
        i2 = jnp.where(r[0] == m2, 0, jnp.where(r[1] == m2, 1, jnp.where(r[2] == m2, 2, 3)))
        firsts.append(i1 + EXPERTS_PER_GROUP * g)
        seconds.append(i2 + EXPERTS_PER_GROUP * g)
        scores.append(m1 + m2)
    best = jnp.maximum(jnp.maximum(scores[0], scores[1]), jnp.maximum(scores[2], scores[3]))
    pick = lambda v: jnp.where(scores[0] == best, v[0], jnp.where(scores[1] == best, v[1],
                                                                 jnp.where(scores[2] == best, v[2], v[3])))
    e1, e2 = pick(firsts), pick(seconds)
    eidx = lax.broadcasted_iota(jnp.int32, aff.shape, 0)
    a1 = jnp.sum(jnp.where(eidx == e1, aff, 0.0), axis=0, keepdims=True)
    a2 = jnp.sum(jnp.where(eidx == e2, aff, 0.0), axis=0, keepdims=True)
    tot = a1 + a2
    w1, w2 = a1 / tot, a2 / tot
    swap = e1 > e2
    lo = jnp.where(swap, e2, e1) & (EXPERTS_PER_GROUP - 1)
    hi = jnp.where(swap, e1, e2) & (EXPERTS_PER_GROUP - 1)
    pair = jnp.where(lo == 0, 0, jnp.where(lo == 1, 3, 5)) + hi - lo - 1
    cls = (e1 >> 2) * PAIRS_PER_GROUP + pair
    rows = [cls.astype(F32), jnp.where(swap, w2, w1), jnp.where(swap, w1, w2)]
    return jnp.concatenate(rows + [jnp.zeros_like(w1)] * (ROUTE_ROWS - len(rows)), axis=0)


def _pack_bf16_pairs(v):
    w = v.shape[1] // 2
    bits = pltpu.bitcast(v.astype(BF16).astype(F32), jnp.uint32)
    return pltpu.bitcast(bits[:, :w] | (bits[:, w:] >> 16), jnp.int32)


def _unpack_bf16_pairs(p):
    bits = pltpu.bitcast(p, jnp.uint32)
    hi = pltpu.bitcast(bits & jnp.uint32(0xFFFF0000), F32)
    lo = pltpu.bitcast(bits << 16, F32)
    return jnp.concatenate([hi, lo], axis=-1)


def _merge_kernel(xl_ref, xc_ref, mod_ref, n1_ref, n2_ref, al_ref, ac_ref, b_ref, c_ref, d_ref,
                  wg_ref, bg_ref, wb_ref, wo_ref, rw_ref, rb_ref,
                  xo_ref, h2_ref, route_ref, merged_s, *, nw, sub, n_lat):
    sh1, sc1, g1, sh2, sc2, _ = _mod_slices(mod_ref[0])
    is_ctx = pl.program_id(1) >= n_lat
    for r in range(xl_ref.shape[1] // sub):
        rows = slice(r * sub, (r + 1) * sub)
        x = jnp.where(is_ctx, xc_ref[0, rows, :], xl_ref[0, rows, :])
        hb = _norm_mod(x, n1_ref[...], sh1, sc1).astype(BF16)
        branches = (jnp.where(is_ctx, ac_ref[0, rows, :], al_ref[0, rows, :]),
                    b_ref[0, rows, :], c_ref[0, rows, :], d_ref[0, rows, :])
        for n in range(D_MODEL // nw):
            cols = slice(n * nw, (n + 1) * nw)
            merged = None
            for i, br in enumerate(branches):
                gate = jax.nn.sigmoid(
                    jnp.dot(hb, wg_ref[i, :, cols], preferred_element_type=F32) + bg_ref[i, :, cols])
                term = gate * jnp.dot(br, wb_ref[i, :, cols], preferred_element_type=F32)
                merged = term if merged is None else merged + term
            merged_s[rows, cols] = merged.astype(BF16)
        y = jnp.dot(merged_s[rows, :], wo_ref[...], preferred_element_type=F32)
        xn = x + g1 * y
        xo_ref[0, rows, :] = xn
        h2 = _norm_mod(xn, n2_ref[...], sh2, sc2)
        h2_ref[0, rows, :] = _pack_bf16_pairs(h2)
        h_hi = h2.astype(BF16)
        h_lo = (h2 - h_hi.astype(F32)).astype(BF16)
        nt = (((1,), (1,)), ((), ()))
        by_hi = lax.dot_general(rw_ref[...], h_hi, nt, preferred_element_type=F32)
        by_lo = lax.dot_general(rw_ref[:N_EXPERTS, :], h_lo, nt, preferred_element_type=F32)
        logits_t = by_hi[:N_EXPERTS] + by_hi[N_EXPERTS:] + by_lo
        route_ref[0, :, rows] = _route(logits_t, rb_ref[...])


def _merge(xl, xc, mod, n1, n2, br_al, br_ac, br_b, br_c, br_d, wg, bg, wb, wo, rw_t, rb, s_out, tm=256):
    b, t_lat, d = xl.shape
    n_lat = t_lat // tm
    tok = lambda bi, j: (bi, j, 0)
    br_spec = pl.BlockSpec((1, tm, BRANCH_W), tok)
    return pl.pallas_call(
        functools.partial(_merge_kernel, nw=256, sub=tm // 2, n_lat=n_lat),
        scratch_shapes=[pltpu.VMEM((tm, d), BF16)],
        grid=(b, s_out // tm),
        in_specs=_stream_specs(tm, d, n_lat) + [
            pl.BlockSpec((1, 1, 6 * d), lambda bi, j: (jnp.where(j >= n_lat, b, bi), 0, 0)),
            _const_spec((1, d)), _const_spec((1, d)),
            *_stream_specs(tm, BRANCH_W, n_lat), br_spec, br_spec, br_spec,
            _const_spec(wg.shape), _const_spec(bg.shape), _const_spec(wb.shape), _const_spec(wo.shape),
            _const_spec(rw_t.shape), _const_spec(rb.shape),
        ],
        out_specs=[
            pl.BlockSpec((1, tm, d), tok),
            pl.BlockSpec((1, tm, d // 2), tok),
            pl.BlockSpec((1, ROUTE_ROWS, tm), lambda bi, j: (bi, 0, j)),
        ],
        out_shape=[
            jax.ShapeDtypeStruct((b, s_out, d), F32),
            jax.ShapeDtypeStruct((b, s_out, d // 2), jnp.int32),
            jax.ShapeDtypeStruct((b, ROUTE_ROWS, s_out), F32),
        ],
        compiler_params=_cparams(("parallel", "arbitrary"), 56),
        name="merge",
    )(xl, xc, mod, n1, n2, br_al, br_ac, br_b, br_c, br_d, wg, bg, wb, wo, rw_t, rb)


def _rank_kernel(cls_ref, rank_ref, cnt_ref, cnt_s, *, tr):
    @pl.when(pl.program_id(0) == 0)
    def _():
        cnt_s[...] = jnp.zeros_like(cnt_s)

    cls = cls_ref[0]
    cid = lax.broadcasted_iota(jnp.int32, (CLASS_ROWS, tr), 0).astype(F32)
    onehot = cid == cls
    before = lax.broadcasted_iota(jnp.int32, (tr, tr), 0) < lax.broadcasted_iota(jnp.int32, (tr, tr), 1)
    prefix = jnp.dot(jnp.where(onehot, 1.0, 0.0).astype(BF16), jnp.where(before, 1.0, 0.0).astype(BF16),
                     preferred_element_type=F32)
    carry = cnt_s[...][:, 0:1]
    rank_ref[0] = jnp.sum(jnp.where(onehot, prefix + carry, 0.0), axis=0, keepdims=True)
    cnt_s[...] += jnp.sum(jnp.where(onehot, 1.0, 0.0), axis=1, keepdims=True)
    cnt_ref[...] = cnt_s[...]


def _rank(cls_flat, tr=512):
    n = cls_flat.shape[0]
    tr = math.gcd(n, tr)
    cls3 = cls_flat.reshape(n // tr, 1, tr)
    rank, cnt = pl.pallas_call(
        functools.partial(_rank_kernel, tr=tr),
        grid=(n // tr,),
        in_specs=[pl.BlockSpec((1, 1, tr), lambda i: (i, 0, 0))],
        out_specs=[pl.BlockSpec((1, 1, tr), lambda i: (i, 0, 0)), _const_spec((CLASS_ROWS, LANES))],
        out_shape=[jax.ShapeDtypeStruct((n // tr, 1, tr), F32), jax.ShapeDtypeStruct((CLASS_ROWS, LANES), F32)],
        scratch_shapes=[pltpu.VMEM((CLASS_ROWS, LANES), F32)],
        compiler_params=_cparams(("arbitrary",), 32),
        name="rank",
    )(cls3)
    return rank.reshape(n), cnt[:N_CLASSES, 0]


def _sc_layout(n):
    info = plsc.get_sparse_core_info()
    nw = info.num_cores * info.num_subcores
    per_worker = n // nw
    assert per_worker * nw == n
    chunk = max(c for c in range(8, SC_MAX_CHUNK + 1, 8) if per_worker % c == 0)
    return info.num_cores, nw, per_worker // chunk, chunk


def _sc_scatter_rows(src, wts, pos, n_out):
    n, w = src.shape
    nc, nw, k, c = _sc_layout(n)
    mesh = plsc.VectorSubcoreMesh(core_axis_name="c", subcore_axis_name="s")

    @functools.partial(
        pl.kernel, mesh=mesh,
        out_type=(jax.ShapeDtypeStruct((n_out, w), src.dtype), jax.ShapeDtypeStruct((n_out, wts.shape[1]), wts.dtype)),
        scratch_types=[pltpu.VMEM((k, c), jnp.int32), pltpu.VMEM((c, w), src.dtype),
                       pltpu.VMEM((c, wts.shape[1]), wts.dtype), pltpu.SemaphoreType.DMA],
        name="moe_scatter",
    )
    def scatter(src_hbm, wts_hbm, pos_hbm, out_hbm, wout_hbm, idx_v, rows_v, wrows_v, sem):
        wid = lax.axis_index("s") * nc + lax.axis_index("c")
        pltpu.sync_copy(pos_hbm.at[wid], idx_v)

        @pl.loop(0, k)
        def _(j):
            off = pl.multiple_of(wid * (k * c) + j * c, 8)
            pltpu.sync_copy(src_hbm.at[pl.ds(off, c)], rows_v)
            pltpu.sync_copy(wts_hbm.at[pl.ds(off, c)], wrows_v)
            pltpu.async_copy(rows_v, out_hbm.at[idx_v.at[j]], sem).wait()
            pltpu.async_copy(wrows_v, wout_hbm.at[idx_v.at[j]], sem).wait()

    return scatter(src, wts, pos.reshape(nw, k, c))


def _sc_gather_rows(src, pos):
    n = pos.shape[0]
    w = src.shape[1]
    nc, nw, k, c = _sc_layout(n)
    mesh = plsc.VectorSubcoreMesh(core_axis_name="c", subcore_axis_name="s")

    @functools.partial(
        pl.kernel, mesh=mesh,
        out_type=jax.ShapeDtypeStruct((n, w), src.dtype),
        scratch_types=[pltpu.VMEM((k, c), jnp.int32), pltpu.VMEM((c, w), src.dtype), pltpu.SemaphoreType.DMA],
        name="moe_gather",
    )
    def gather(src_hbm, pos_hbm, out_hbm, idx_v, rows_v, sem):
        wid = lax.axis_index("s") * nc + lax.axis_index("c")
        pltpu.sync_copy(pos_hbm.at[wid], idx_v)

        @pl.loop(0, k)
        def _(j):
            off = pl.multiple_of(wid * (k * c) + j * c, 8)
            pltpu.async_copy(src_hbm.at[idx_v.at[j]], rows_v, sem).wait()
            pltpu.sync_copy(rows_v, out_hbm.at[pl.ds(off, c)])

    return gather(src, pos.reshape(nw, k, c))


def _gmm_kernel(lo_ref, hi_ref, nact_ref, h_ref, wt_ref, w13a_ref, w13b_ref, w2a_ref, w2b_ref, o_ref):
    @pl.when(pl.program_id(0) < nact_ref[0])
    def _():
        x = _unpack_bf16_pairs(h_ref[...]).astype(BF16)
        wts = wt_ref[...]

        def expert(w13_ref, w2_ref):
            ab = jnp.dot(x, w13_ref[0], preferred_element_type=F32)
            a, gate = ab[:, :EXPERT_FF], ab[:, EXPERT_FF:]
            hid = (a * jax.nn.sigmoid(a)) * gate
            return jnp.dot(hid.astype(BF16), w2_ref[0], preferred_element_type=F32)

        y = wts[:, 0:1] * expert(w13a_ref, w2a_ref) + wts[:, 1:2] * expert(w13b_ref, w2b_ref)
        o_ref[...] = _pack_bf16_pairs(y)


def _gmm(tile_lo, tile_hi, n_act, hs, ws, w13, w2, tm):
    n_pad, half = hs.shape
    d = 2 * half
    row = lambda t, lo, hi, na: (jnp.minimum(t, na[0] - 1), 0)
    e_lo = lambda t, lo, hi, na: (lo[jnp.minimum(t, na[0] - 1)], 0, 0)
    e_hi = lambda t, lo, hi, na: (hi[jnp.minimum(t, na[0] - 1)], 0, 0)
    return pl.pallas_call(
        _gmm_kernel,
        grid_spec=pltpu.PrefetchScalarGridSpec(
            num_scalar_prefetch=3,
            grid=(n_pad // tm,),
            in_specs=[
                pl.BlockSpec((tm, half), row),
                pl.BlockSpec((tm, ws.shape[1]), row),
                pl.BlockSpec((1, d, 2 * EXPERT_FF), e_lo),
                pl.BlockSpec((1, d, 2 * EXPERT_FF), e_hi),
                pl.BlockSpec((1, EXPERT_FF, d), e_lo),
                pl.BlockSpec((1, EXPERT_FF, d), e_hi),
            ],
            out_specs=pl.BlockSpec((tm, half), row),
        ),
        out_shape=jax.ShapeDtypeStruct((n_pad, half), jnp.int32),
        compiler_params=_cparams(("arbitrary",), 48),
        name="moe_gmm",
    )(tile_lo, tile_hi, n_act, hs, ws, w13, w13, w2, w2)


def _moe_routed(h2p, route, w13, w2, tm=256):
    b, s, half = h2p.shape
    n = b * s
    n_pad = n + N_CLASSES * tm
    cls = route[:, 0, :].reshape(n)
    rank, counts = _rank(cls)
    counts = counts.astype(jnp.int32)
    padded = (counts + tm - 1) // tm * tm
    ends = jnp.cumsum(padded)
    pos = jnp.take(ends - padded, cls.astype(jnp.int32)) + rank.astype(jnp.int32)
    n_act = (ends[-1] // tm).reshape(1)
    tile_row = jnp.arange(n_pad // tm, dtype=jnp.int32) * tm
    tile_cls = jnp.minimum(jnp.sum(tile_row[:, None] >= ends[None, :], axis=1), N_CLASSES - 1)
    pair_lo, pair_hi = (jnp.asarray(a, jnp.int32) for a in _class_experts())
    wts = jnp.concatenate([route[:, 1, :].reshape(n, 1), route[:, 2, :].reshape(n, 1),
                           jnp.zeros((n, SC_WEIGHT_COLS - 2), F32)], axis=1)
    hs, ws = _sc_scatter_rows(h2p.reshape(n, half), wts, pos, n_pad)
    ys = _gmm(jnp.take(pair_lo, tile_cls), jnp.take(pair_hi, tile_cls), n_act, hs, ws, w13, w2, tm)
    return _sc_gather_rows(ys, pos).reshape(b, s, half)


def _class_experts():
    lo, hi = [], []
    for g in range(N_EXPERTS // EXPERTS_PER_GROUP):
        for i in range(EXPERTS_PER_GROUP):
            for j in range(i + 1, EXPERTS_PER_GROUP):
                lo.append(EXPERTS_PER_GROUP * g + i)
                hi.append(EXPERTS_PER_GROUP * g + j)
    return np.array(lo), np.array(hi)


def _residual_kernel(x_ref, y_ref, mod_ref, g_ref, *out_refs, final, n_lat):
    g2 = _mod_slices(mod_ref[0])[5]
    x = x_ref[0] + g2 * _unpack_bf16_pairs(y_ref[0])
    if final:
        out_refs[0][0] = x * lax.rsqrt(jnp.mean(x * x, axis=-1, keepdims=True) + EPS) * g_ref[...]
        return
    ol_ref, oc_ref = out_refs
    is_ctx = pl.program_id(1) >= n_lat

    @pl.when(is_ctx)
    def _():
        oc_ref[0] = x

    @pl.when(jnp.logical_not(is_ctx))
    def _():
        ol_ref[0] = x


def _residual(x1, yp, mod, gain, t_lat, final, tm=256):
    b, s, d = x1.shape
    n_lat = t_lat // tm
    tok = lambda bi, j: (bi, j, 0)
    if final:
        out_specs = pl.BlockSpec((1, tm, d), tok)
        out_shape = jax.ShapeDtypeStruct((b, s, d), F32)
    else:
        out_specs = _stream_specs(tm, d, n_lat)
        out_shape = [jax.ShapeDtypeStruct((b, t_lat, d), F32), jax.ShapeDtypeStruct((b, s - t_lat, d), F32)]
    return pl.pallas_call(
        functools.partial(_residual_kernel, final=final, n_lat=n_lat),
        grid=(b, s // tm),
        in_specs=[
            pl.BlockSpec((1, tm, d), tok),
            pl.BlockSpec((1, tm, d // 2), tok),
            pl.BlockSpec((1, 1, 6 * d), lambda bi, j: (jnp.where(j >= n_lat, b, bi), 0, 0)),
            _const_spec((1, d)),
        ],
        out_specs=out_specs,
        out_shape=out_shape,
        compiler_params=_cparams(("arbitrary", "arbitrary"), 32),
        name="residual",
    )(x1, yp, mod, gain)


def _rope_tables(t_lat, n_ctx):
    rows = t_lat // GRID_W
    row = jnp.repeat(jnp.arange(rows, dtype=F32), GRID_W)
    col = jnp.tile(jnp.arange(GRID_W, dtype=F32), rows)
    inv_freq = ROPE_THETA ** (-jnp.arange(0, ROPE_AXIS_DIM, 2, dtype=F32) / ROPE_AXIS_DIM)
    ang = jnp.stack([row[:, None] * inv_freq, col[:, None] * inv_freq], axis=1)
    cos, sin = jnp.cos(ang), jnp.sin(ang)
    zero = jnp.zeros_like(sin)
    cos_h = jnp.concatenate([cos, cos], axis=-1).reshape(t_lat, HEAD_DIM)
    sa_h = jnp.concatenate([-sin, zero], axis=-1).reshape(t_lat, HEAD_DIM)
    sb_h = jnp.concatenate([zero, sin], axis=-1).reshape(t_lat, HEAD_DIM)
    reps = LANES // HEAD_DIM
    pad = lambda tbl, fill: jnp.concatenate(
        [jnp.tile(tbl, (1, reps)), jnp.full((n_ctx, LANES), fill, F32)], axis=0)
    return pad(cos_h, 1.0), pad(sa_h, 0.0), pad(sb_h, 0.0)


def _permute_heads(w, axis):
    shp = w.shape
    w = w.reshape(shp[:axis] + (N_Q_HEADS, HEAD_DIM) + shp[axis + 1:])
    w = jnp.take(w, jnp.array(HEAD_PERM), axis=axis)
    return w.reshape(shp)


def _square_factor(n):
    r = int(round(math.sqrt(n)))
    assert r * r == n, "sequence lengths must be perfect squares for the two-stage DFT"
    return r


def kernel(x, c, ctx, c_ctx, w_ada, b_ada, norm1, norm2, w_in, q_gain, k_gain, sink, pool_w, pool_scale,
           w_branch, w_gate, b_gate, w_out, router_w, router_bias, w1, w3, w2, norm_f):
    b, t_lat, d = x.shape
    n_ctx = ctx.shape[1]
    s = t_lat + n_ctx
    depth = w_ada.shape[0]
    assert d == D_MODEL and b < MOD_ROWS and t_lat % 256 == 0 and n_ctx % 256 == 0 and t_lat % n_ctx == 0

    xl, xc = x, ctx
    cc = jnp.zeros((MOD_ROWS, d), F32).at[:b].set(c).at[b].set(c_ctx)
    mod_all = _ada(cc, w_ada, b_ada).reshape(depth, MOD_ROWS, 1, 6 * d)

    cos, sa, sb = _rope_tables(t_lat, n_ctx)
    seg = jnp.asarray(np.kron(np.eye(N_Q_HEADS), np.full((HEAD_DIM, HEAD_DIM), 1.0 / HEAD_DIM)), BF16)
    cs = jnp.asarray(_channel_dft_table()).astype(BF16)
    f_lat = [jnp.asarray(a).astype(BF16) for a in _fourier_tables(*(_square_factor(t_lat),) * 2)]
    f_ctx = [jnp.asarray(a).astype(BF16) for a in _fourier_tables(*(_square_factor(n_ctx),) * 2)]
    wbias = jnp.asarray(_window_bias(n_ctx))
    rw_hi = router_w.T.astype(BF16)
    rw_t = jnp.concatenate([rw_hi, (router_w.T - rw_hi.astype(F32)).astype(BF16)], axis=0)
    rb = router_bias.reshape(N_EXPERTS, 1)

    for l in range(depth):
        need_ctx = l < depth - 1
        s_out = s if need_ctx else t_lat
        cols = jnp.split(w_in[l], np.cumsum((512, 512, 512, 128, 128, 512, 128))[:], axis=1)
        f_w, p_w, qb_w, kb_w, vb_w, qw_w, kw_w, vw_w = cols
        w_in_l = jnp.concatenate([f_w, p_w, _permute_heads(qb_w, 1), _permute_heads(qw_w, 1),
                                  kb_w, vb_w, kw_w, vw_w], axis=1).astype(BF16)
        wb_l = jnp.stack([w_branch[l, 0], _permute_heads(w_branch[l, 1], 0),
                          _permute_heads(w_branch[l, 2], 0), w_branch[l, 3]]).astype(BF16)
        mod = mod_all[l]
        n1 = norm1[l].reshape(1, d)
        n2 = norm2[l].reshape(1, d)
        qg = jnp.tile(q_gain[l], N_Q_HEADS).reshape(1, BRANCH_W)
        kg = jnp.tile(k_gain[l], LANES // HEAD_DIM).reshape(1, LANES)

        zr, zi, p_in, qb, qw, kvb, kvw = _inproj(xl, xc, mod, n1, w_in_l, qg, kg, seg, cs, cos, sa, sb)

        out_b = _gattn(qb, kvb, s_out, t_lat)
        out_c = _wattn(jnp.take(sink[l], jnp.array(HEAD_PERM)) * LOG2E, wbias, qw, kvw, s_out, t_lat)
        out_a = _fourier(zr, zi, f_lat[0], f_lat[1], t_lat, 0)
        out_ac = _fourier(zr, zi, f_ctx[0], f_ctx[1], n_ctx, t_lat // n_ctx) if need_ctx else out_a
        out_d = _pool(p_in, pool_w[l].astype(BF16), pool_scale[l].reshape(1, BRANCH_W), s_out, t_lat)

        x1, h2p, route = _merge(xl, xc, mod, n1, n2, out_a, out_ac, out_b, out_c, out_d,
                                w_gate[l].astype(BF16), b_gate[l].reshape(4, 1, d), wb_l,
                                w_out[l].astype(BF16), rw_t, rb, s_out)
        w13 = jnp.concatenate([w1[l], w3[l]], axis=-1).astype(BF16)
        yp = _moe_routed(h2p, route, w13, w2[l].astype(BF16))
        if l == depth - 1:
            return _residual(x1, yp, mod, norm_f.reshape(1, d), t_lat, final=True)
        xl, xc = _residual(x1, yp, mod, norm_f.reshape(1, d), t_lat, final=False)
```

```python
import functools
import math

import numpy as np
import jax
import jax.numpy as jnp
from jax import lax
from jax.experimental import pallas as pl
from jax.experimental.pallas import tpu as pltpu
from jax.experimental.pallas import tpu_sc as plsc

F32 = jnp.float32
BF16 = jnp.bfloat16

D_MODEL = 1024
HEAD_DIM = 64
N_Q_HEADS = 8
N_KV_HEADS = 2
GRID_W = 64
ROPE_THETA = 10000.0
ROPE_AXIS_DIM = HEAD_DIM // 2
QBLK = 128
WINDOW = 128
BRANCH_W = 512
GROUP_W = 128
POOL_WINDOWS = (2, 4, 8, 16)
N_EXPERTS = 16
EXPERTS_PER_GROUP = 4
EXPERT_FF = 512
EPS = 1e-6
MOD_ROWS = 16
NEG_BIG = -1e30
LOG2E = math.log2(math.e)
LANES = 128
POOL_HALO = 16
PAIRS_PER_GROUP = 6
N_CLASSES = 24
CLASS_ROWS = 32
ROUTE_ROWS = 8
SC_MAX_CHUNK = 128
SC_WEIGHT_COLS = 128

HEAD_PERM = (0, 4, 1, 5, 2, 6, 3, 7)


def _cparams(sem, vmem_mb):
    return pltpu.CompilerParams(dimension_semantics=sem, vmem_limit_bytes=vmem_mb * 1024 * 1024)


def _const_spec(shape):
    nd = len(shape)
    return pl.BlockSpec(shape, lambda *_: (0,) * nd)


def _ada_kernel(c_ref, w_ref, b_ref, o_ref):
    c = c_ref[...]
    s = c * jax.nn.sigmoid(c)
    o_ref[0] = jnp.dot(s.astype(BF16), w_ref[0].astype(BF16), preferred_element_type=F32) + b_ref[0]


def _ada(cc, w_ada, b_ada):
    depth, d, n = w_ada.shape
    tn = 1536
    return pl.pallas_call(
        _ada_kernel,
        grid=(depth, n // tn),
        in_specs=[
            pl.BlockSpec((MOD_ROWS, d), lambda l, j: (0, 0)),
            pl.BlockSpec((1, d, tn), lambda l, j: (l, 0, j)),
            pl.BlockSpec((1, 1, tn), lambda l, j: (l, 0, j)),
        ],
        out_specs=pl.BlockSpec((1, MOD_ROWS, tn), lambda l, j: (l, 0, j)),
        out_shape=jax.ShapeDtypeStruct((depth, MOD_ROWS, n), F32),
        compiler_params=_cparams(("arbitrary", "arbitrary"), 40),
        name="ada",
    )(cc, w_ada, b_ada.reshape(depth, 1, n))


def _norm_mod(x, gain, shift, scale):
    ms = jnp.mean(x * x, axis=-1, keepdims=True)
    return (x * lax.rsqrt(ms + EPS) * gain) * (1.0 + scale) + shift


def _mod_slices(m):
    d = D_MODEL
    return [m[:, i * d:(i + 1) * d] for i in range(6)]


def _head_norm(z, seg, gain):
    ms = jnp.dot((z * z).astype(BF16), seg, preferred_element_type=F32)
    return z * lax.rsqrt(ms + EPS) * gain


def _rope(z, cos, sin_a, sin_b):
    outs = []
    for c in range(z.shape[1] // LANES):
        zc = z[:, c * LANES:(c + 1) * LANES]
        nxt = pltpu.roll(zc, LANES - ROPE_AXIS_DIM // 2, 1)
        prv = pltpu.roll(zc, ROPE_AXIS_DIM // 2, 1)
        outs.append(zc * cos + nxt * sin_a + prv * sin_b)
    return outs[0] if len(outs) == 1 else jnp.concatenate(outs, axis=-1)


def _stream_specs(tm, d, n_lat):
    return [pl.BlockSpec((1, tm, d), lambda bi, j: (bi, jnp.minimum(j, n_lat - 1), 0)),
            pl.BlockSpec((1, tm, d), lambda bi, j: (bi, jnp.maximum(j - n_lat, 0), 0))]


def _inproj_kernel(xl_ref, xc_ref, mod_ref, n1_ref, w_ref, qg_ref, kg_ref, seg_ref, cs_ref,
                   cos_ref, sa_ref, sb_ref,
                   zr_ref, zi_ref, p_ref, qb_ref, qw_ref, kvb_ref, kvw_ref, *, n_lat):
    sh1, sc1 = _mod_slices(mod_ref[0])[:2]
    x = jnp.where(pl.program_id(1) >= n_lat, xc_ref[0], xl_ref[0])
    h = _norm_mod(x, n1_ref[...], sh1, sc1)
    u = jnp.dot(h.astype(BF16), w_ref[...], preferred_element_type=F32)
    cos, sa, sb = cos_ref[...], sa_ref[...], sb_ref[...]
    w = BRANCH_W
    f_in = u[:, 0:w].astype(BF16)
    zr, zi = [], []
    for g in range(w // GROUP_W):
        z = jnp.dot(f_in[:, g * GROUP_W:(g + 1) * GROUP_W], cs_ref[...], preferred_element_type=F32)
        zr.append(z[:, :GROUP_W])
        zi.append(z[:, GROUP_W:])
    zr_ref[0] = jnp.concatenate(zr, axis=-1).astype(BF16)
    zi_ref[0] = jnp.concatenate(zi, axis=-1).astype(BF16)
    p_ref[0] = u[:, w:2 * w].astype(BF16)
    seg = seg_ref[...]
    qb = _rope(_head_norm(u[:, 2 * w:3 * w], seg, qg_ref[...]), cos, sa, sb)
    qb_ref[0] = (qb * (HEAD_DIM ** -0.5 * LOG2E)).astype(BF16)
    qw = _rope(u[:, 3 * w:4 * w], cos, sa, sb)
    qw_ref[0] = (qw * (HEAD_DIM ** -0.5 * LOG2E)).astype(BF16)
    o = 4 * w
    kb = _rope(_head_norm(u[:, o:o + LANES], seg[:LANES, :LANES], kg_ref[...]), cos, sa, sb)
    vb = u[:, o + LANES:o + 2 * LANES]
    kw = _rope(u[:, o + 2 * LANES:o + 3 * LANES], cos, sa, sb)
    vw = u[:, o + 3 * LANES:o + 4 * LANES]
    kvb_ref[0] = jnp.concatenate([kb, vb, jnp.ones_like(vb)], axis=-1).astype(BF16)
    kvw_ref[0] = jnp.concatenate([kw, vw, jnp.ones_like(vw)], axis=-1).astype(BF16)


def _inproj(xl, xc, mod, n1, w_in, qg, kg, seg, cs, cos, sa, sb, tm=256):
    b, t_lat, d = xl.shape
    s = t_lat + xc.shape[1]
    nw = w_in.shape[1]
    n_lat = t_lat // tm
    tok = lambda bi, j: (bi, j, 0)
    tab = lambda bi, j: (j, 0)
    widths = (BRANCH_W,) * 5 + (3 * LANES, 3 * LANES)
    return pl.pallas_call(
        functools.partial(_inproj_kernel, n_lat=n_lat),
        grid=(b, s // tm),
        in_specs=_stream_specs(tm, d, n_lat) + [
            pl.BlockSpec((1, 1, 6 * d), lambda bi, j: (jnp.where(j >= n_lat, b, bi), 0, 0)),
            _const_spec((1, d)),
            _const_spec((d, nw)),
            _const_spec((1, BRANCH_W)),
            _const_spec((1, LANES)),
            _const_spec((BRANCH_W, BRANCH_W)),
            _const_spec((GROUP_W, 2 * GROUP_W)),
            pl.BlockSpec((tm, LANES), tab),
            pl.BlockSpec((tm, LANES), tab),
            pl.BlockSpec((tm, LANES), tab),
        ],
        out_specs=[pl.BlockSpec((1, tm, wd), tok) for wd in widths],
        out_shape=[jax.ShapeDtypeStruct((b, s, wd), BF16) for wd in widths],
        compiler_params=_cparams(("parallel", "arbitrary"), 48),
        name="inproj",
    )(xl, xc, mod, n1, w_in, qg, kg, seg, cs, cos, sa, sb)


def _split_heads(qc, lane):
    zero = jnp.zeros_like(qc)
    return jnp.concatenate([jnp.where(lane < HEAD_DIM, qc, zero),
                            jnp.where(lane >= HEAD_DIM, qc, zero)], axis=0)


def _gattn_kernel(q_ref, kv_ref, o_ref, q2_s, m_s, acc_s, *, n_tiles, sub):
    tq = q_ref.shape[1]
    tk = kv_ref.shape[1] // n_tiles
    lane = lax.broadcasted_iota(jnp.int32, (1, LANES), 1)
    nt = (((1,), (1,)), ((), ()))
    n_chunks = BRANCH_W // LANES
    for c in range(n_chunks):
        q2_s[2 * c * tq:(2 * c + 2) * tq, :] = _split_heads(q_ref[0, :, c * LANES:(c + 1) * LANES], lane)

    def tile(k, v, first):
        for r in range(2 * n_chunks * tq // sub):
            rows = slice(r * sub, (r + 1) * sub)
            s = lax.dot_general(q2_s[rows, :], k, nt, preferred_element_type=F32)
            rm = jnp.max(s, axis=-1, keepdims=True)
            if first:
                m_new = rm
                acc_s[rows, :] = jnp.dot(jnp.exp2(s - m_new).astype(BF16), v, preferred_element_type=F32)
            else:
                m_old = m_s[rows, :]
                m_new = jnp.maximum(m_old, rm)
                alpha = jnp.exp2(m_old - m_new)
                pv = jnp.dot(jnp.exp2(s - m_new).astype(BF16), v, preferred_element_type=F32)
                acc_s[rows, :] = alpha * acc_s[rows, :] + pv
            m_s[rows, :] = m_new

    for t in range(n_tiles):
        keys = slice(t * tk, (t + 1) * tk)
        tile(kv_ref[0, keys, 0:LANES], kv_ref[0, keys, LANES:3 * LANES], t == 0)
    for c in range(n_chunks):
        lo = acc_s[2 * c * tq:(2 * c + 1) * tq, :]
        hi = acc_s[(2 * c + 1) * tq:(2 * c + 2) * tq, :]
        o = jnp.where(lane < HEAD_DIM, lo[:, :LANES] / lo[:, LANES:], hi[:, :LANES] / hi[:, LANES:])
        o_ref[0, :, c * LANES:(c + 1) * LANES] = o.astype(BF16)


def _gattn(qb, kv, q_start, q_len, k_start, k_len, n_tiles, tq=256, sub=128):
    b = qb.shape[0]
    assert q_start % tq == 0 and q_len % tq == 0 and k_start % k_len == 0 and k_len % (n_tiles * LANES) == 0
    rows = 2 * tq * (BRANCH_W // LANES)
    return pl.pallas_call(
        functools.partial(_gattn_kernel, n_tiles=n_tiles, sub=sub),
        grid=(b, q_len // tq),
        in_specs=[
            pl.BlockSpec((1, tq, BRANCH_W), lambda bi, j: (bi, q_start // tq + j, 0)),
            pl.BlockSpec((1, k_len, 3 * LANES), lambda bi, j: (bi, k_start // k_len, 0)),
        ],
        out_specs=pl.BlockSpec((1, tq, BRANCH_W), lambda bi, j: (bi, j, 0)),
        out_shape=jax.ShapeDtypeStruct((b, q_len, BRANCH_W), BF16),
        scratch_shapes=[pltpu.VMEM((rows, LANES), BF16), pltpu.VMEM((rows, 1), F32),
                        pltpu.VMEM((rows, 2 * LANES), F32)],
        compiler_params=_cparams(("parallel", "arbitrary"), 48),
        name="gattn",
    )(qb, kv)


WATTN_QBLOCKS = 2
GATTN_KEY_TILES = 2


def _window_bias(n_ctx):
    tq = WATTN_QBLOCKS * QBLK
    qi = np.arange(tq)[:, None]
    kj = np.arange(tq + 2 * QBLK)[None, :]
    band = np.abs(kj - WINDOW - qi) <= WINDOW
    blk = kj // QBLK
    variants = [band & (blk != 0), band, band & (blk != WATTN_QBLOCKS + 1), np.zeros_like(band)]
    out = [np.concatenate([np.ones((tq, n_ctx), bool), v], axis=1) for v in variants]
    bias = np.where(np.stack(out), 0.0, NEG_BIG).astype(np.float32)
    return np.concatenate([bias, bias], axis=1)


def _wattn_kernel(sink_ref, bias_ref, q_ref, *refs):
    blocks, o_ref = refs[:-1], refs[-1]
    tq = q_ref.shape[1]
    kall = jnp.concatenate([r[0, :, 0:LANES] for r in blocks], axis=0)
    vall = jnp.concatenate([r[0, :, LANES:3 * LANES] for r in blocks], axis=0)
    bias = bias_ref[0]
    lane = lax.broadcasted_iota(jnp.int32, (1, LANES), 1)
    row = lax.broadcasted_iota(jnp.int32, (2 * tq, 1), 0)
    nt = (((1,), (1,)), ((), ()))
    for c in range(BRANCH_W // LANES):
        q2 = _split_heads(q_ref[0, :, c * LANES:(c + 1) * LANES], lane)
        s = lax.dot_general(q2, kall, nt, preferred_element_type=F32) + bias
        sk = jnp.where(row < tq, sink_ref[2 * c], sink_ref[2 * c + 1])
        m = jnp.maximum(jnp.max(s, axis=-1, keepdims=True), sk)
        pv = jnp.dot(jnp.exp2(s - m).astype(BF16), vall, preferred_element_type=F32)
        o2 = pv[:, :LANES] / (pv[:, LANES:] + jnp.exp2(sk - m))
        o_ref[0, :, c * LANES:(c + 1) * LANES] = jnp.where(lane < HEAD_DIM, o2[:tq], o2[tq:]).astype(BF16)


def _wattn(sink, bias, qw, kv, s_out, t_lat):
    b, s, _ = qw.shape
    n_ctx = s - t_lat
    nq = WATTN_QBLOCKS
    tq = nq * QBLK
    assert n_ctx % tq == 0 and t_lat // tq >= 2
    last = s // QBLK - 1
    n_lat = t_lat // tq
    variant = lambda j: jnp.where(j >= n_lat, 3, jnp.where(j == 0, 0, jnp.where(j == n_lat - 1, 2, 1)))
    key_block = lambda off: pl.BlockSpec(
        (1, QBLK, 3 * LANES), lambda bi, j: (bi, jnp.clip(j * nq + off, 0, last), 0))
    return pl.pallas_call(
        _wattn_kernel,
        grid=(b, s_out // tq),
        in_specs=[
            pl.BlockSpec(memory_space=pltpu.SMEM),
            pl.BlockSpec((1,) + bias.shape[1:], lambda bi, j: (variant(j), 0, 0)),
            pl.BlockSpec((1, tq, BRANCH_W), lambda bi, j: (bi, j, 0)),
            pl.BlockSpec((1, n_ctx, 3 * LANES), lambda bi, j: (bi, t_lat // n_ctx, 0)),
        ] + [key_block(off) for off in range(-1, nq + 1)],
        out_specs=pl.BlockSpec((1, tq, BRANCH_W), lambda bi, j: (bi, j, 0)),
        out_shape=jax.ShapeDtypeStruct((b, s_out, BRANCH_W), BF16),
        compiler_params=_cparams(("parallel", "arbitrary"), 32),
        name="wattn",
    )(sink, bias, qw, *([kv] * (nq + 3)))


def _fourier_tables(n1, n2):
    t = n1 * n2
    k2 = np.arange(n2)[None, :, None]
    t2 = np.arange(n2)[None, None, :]
    t1 = np.arange(n1)[:, None, None]
    theta = 2.0 * np.pi * ((k2 * t2 * n1 + k2 * t1) % t) / t
    er, ei = np.cos(theta) / math.sqrt(n2), -np.sin(theta) / math.sqrt(n2)
    e = np.concatenate([np.concatenate([er, -ei], axis=2), np.concatenate([ei, er], axis=2)], axis=1)
    k1 = np.arange(n1)[:, None]
    phi = 2.0 * np.pi * ((k1 * np.arange(n1)[None, :]) % n1) / n1
    dcat = np.concatenate([np.cos(phi), np.sin(phi)], axis=1) / math.sqrt(n1)
    return e.astype(np.float32), dcat.astype(np.float32)


def _channel_dft_table():
    c = np.arange(GROUP_W)
    ang = 2.0 * np.pi * ((c[:, None] * c[None, :]) % GROUP_W) / GROUP_W
    return (np.concatenate([np.cos(ang), -np.sin(ang)], axis=1) / math.sqrt(GROUP_W)).astype(np.float32)


def _fourier_kernel(zr_ref, zi_ref, e_ref, d_ref, o_ref, xr_s, xi_s, yr_s, yi_s, *, n1, n2):
    nc = xr_s.shape[0]
    chunk = lambda c: slice(c * LANES, (c + 1) * LANES)

    def gather(ref, start, size, stride):
        return jnp.concatenate([ref[c, pl.ds(start, size, stride=stride), :] for c in range(nc)], axis=-1)

    for c in range(nc):
        xr_s[c] = zr_ref[0, :, chunk(c)].astype(F32)
        xi_s[c] = zi_ref[0, :, chunk(c)].astype(F32)
    for t1 in range(n1):
        xs = jnp.concatenate([gather(xr_s, t1, n2, n1), gather(xi_s, t1, n2, n1)], axis=0).astype(BF16)
        y = jnp.dot(e_ref[t1], xs, preferred_element_type=F32)
        for c in range(nc):
            yr_s[c, t1 * n2:(t1 + 1) * n2, :] = y[:n2, chunk(c)]
            yi_s[c, t1 * n2:(t1 + 1) * n2, :] = y[n2:, chunk(c)]
    for k2 in range(n2):
        ys = jnp.concatenate([gather(yr_s, k2, n1, n2), gather(yi_s, k2, n1, n2)], axis=0).astype(BF16)
        o = jnp.dot(d_ref[...], ys, preferred_element_type=F32)
        for c in range(nc):
            xr_s[c, pl.ds(k2, n1, stride=n2), :] = o[:, chunk(c)]
    for c in range(nc):
        o_ref[0, :, chunk(c)] = xr_s[c].astype(BF16)


def _fourier(zr, zi, e_tab, d_tab, t_len, row_block, cw=256):
    b = zr.shape[0]
    n1 = d_tab.shape[0]
    n2 = t_len // n1
    zspec = pl.BlockSpec((1, t_len, cw), lambda bi, j: (bi, row_block, j))
    return pl.pallas_call(
        functools.partial(_fourier_kernel, n1=n1, n2=n2),
        grid=(b, BRANCH_W // cw),
        in_specs=[zspec, zspec, _const_spec(e_tab.shape), _const_spec(d_tab.shape)],
        out_specs=pl.BlockSpec((1, t_len, cw), lambda bi, j: (bi, 0, j)),
        out_shape=jax.ShapeDtypeStruct((b, t_len, BRANCH_W), BF16),
        scratch_shapes=[pltpu.VMEM((cw // LANES, t_len, LANES), F32)] * 4,
        compiler_params=_cparams(("parallel", "arbitrary"), 48),
        name="fourier",
    )(zr, zi, e_tab, d_tab)


def _pool_bands(tp):
    off = np.arange(tp + 2 * POOL_HALO)[None, :] - POOL_HALO - np.arange(tp)[:, None]
    return np.stack([(off >= -(w // 2)) & (off <= w // 2 - 1) for w in POOL_WINDOWS]).astype(np.float32)


def _pool_kernel(pc_ref, pp_ref, pn_ref, band_ref, w_ref, sc_ref, o_ref, *, tp, t_lat, s_tot):
    row0 = pl.program_id(1) * tp
    in_lat = row0 < t_lat
    seq_lo = jnp.where(in_lat, 0, t_lat)
    seq_hi = jnp.where(in_lat, t_lat, s_tot)
    ext = jnp.concatenate([pp_ref[0], pc_ref[0], pn_ref[0]], axis=0).astype(F32)
    gpos = row0 - POOL_HALO + lax.broadcasted_iota(jnp.int32, (tp + 2 * POOL_HALO, 1), 0)
    ext = jnp.where((gpos >= seq_lo) & (gpos < seq_hi), ext, 0.0).astype(BF16)
    pos = row0 - seq_lo + lax.broadcasted_iota(jnp.int32, (tp, 1), 0)
    n = seq_hi - seq_lo
    outs = []
    for gi, w in enumerate(POOL_WINDOWS):
        cols = slice(gi * GROUP_W, (gi + 1) * GROUP_W)
        wsum = jnp.dot(band_ref[gi], ext[:, cols], preferred_element_type=F32)
        cnt = jnp.minimum(pos + w // 2, n) - jnp.maximum(pos - w // 2, 0)
        pooled = wsum / cnt.astype(F32) - pc_ref[0, :, cols].astype(F32)
        outs.append(jnp.dot(pooled.astype(BF16), w_ref[gi], preferred_element_type=F32))
    o_ref[0] = (jnp.concatenate(outs, axis=-1) * sc_ref[...]).astype(BF16)


def _pool(p, pool_w, pool_scale, s_out, t_lat, tp=256):
    b, s, _ = p.shape
    hb = tp // POOL_HALO
    last = s // POOL_HALO - 1
    bands = jnp.asarray(_pool_bands(tp)).astype(BF16)
    return pl.pallas_call(
        functools.partial(_pool_kernel, tp=tp, t_lat=t_lat, s_tot=s),
        grid=(b, s_out // tp),
        in_specs=[
            pl.BlockSpec((1, tp, BRANCH_W), lambda bi, j: (bi, j, 0)),
            pl.BlockSpec((1, POOL_HALO, BRANCH_W), lambda bi, j: (bi, jnp.maximum(j * hb - 1, 0), 0)),
            pl.BlockSpec((1, POOL_HALO, BRANCH_W), lambda bi, j: (bi, jnp.minimum((j + 1) * hb, last), 0)),
            _const_spec(bands.shape),
            _const_spec(pool_w.shape),
            _const_spec((1, BRANCH_W)),
        ],
        out_specs=pl.BlockSpec((1, tp, BRANCH_W), lambda bi, j: (bi, j, 0)),
        out_shape=jax.ShapeDtypeStruct((b, s_out, BRANCH_W), BF16),
        compiler_params=_cparams(("parallel", "arbitrary"), 32),
        name="pool",
    )(p, p, p, bands, pool_w, pool_scale)


def _route(logits_t, bias):
    aff = jax.nn.sigmoid(logits_t)
    sel = aff + bias
    neg = -jnp.inf
    firsts, seconds, scores = [], [], []
    for g in range(N_EXPERTS // EXPERTS_PER_GROUP):
        s = [sel[EXPERTS_PER_GROUP * g + k:EXPERTS_PER_GROUP * g + k + 1, :] for k in range(EXPERTS_PER_GROUP)]
        m1 = jnp.maximum(jnp.maximum(s[0], s[1]), jnp.maximum(s[2], s[3]))
        i1 = jnp.where(s[0] == m1, 0, jnp.where(s[1] == m1, 1, jnp.where(s[2] == m1, 2, 3)))
        r = [jnp.where(i1 == k, neg, s[k]) for k in range(EXPERTS_PER_GROUP)]
        m2 = jnp.maximum(jnp.maximum(r[0], r[1]), jnp.maximum(r[2], r[3]))
        i2 = jnp.where(r[0] == m2, 0, jnp.where(r[1] == m2, 1, jnp.where(r[2] == m2, 2, 3)))
        firsts.append(i1 + EXPERTS_PER_GROUP * g)
        seconds.append(i2 + EXPERTS_PER_GROUP * g)
        scores.append(m1 + m2)
    best = jnp.maximum(jnp.maximum(scores[0], scores[1]), jnp.maximum(scores[2], scores[3]))
    pick = lambda v: jnp.where(scores[0] == best, v[0], jnp.where(scores[1] == best, v[1],
                                                                 jnp.where(scores[2] == best, v[2], v[3])))
    e1, e2 = pick(firsts), pick(seconds)
    eidx = lax.broadcasted_iota(jnp.int32, aff.shape, 0)
    a1 = jnp.sum(jnp.where(eidx == e1, aff, 0.0), axis=0, keepdims=True)
    a2 = jnp.sum(jnp.where(eidx == e2, aff, 0.0), axis=0, keepdims=True)
    tot = a1 + a2
    w1, w2 = a1 / tot, a2 / tot
    swap = e1 > e2
    lo = jnp.where(swap, e2, e1) & (EXPERTS_PER_GROUP - 1)
    hi = jnp.where(swap, e1, e2) & (EXPERTS_PER_GROUP - 1)
    pair = jnp.where(lo == 0, 0, jnp.where(lo == 1, 3, 5)) + hi - lo - 1
    cls = (e1 >> 2) * PAIRS_PER_GROUP + pair
    rows = [cls.astype(F32), jnp.where(swap, w2, w1), jnp.where(swap, w1, w2)]
    return jnp.concatenate(rows + [jnp.zeros_like(w1)] * (ROUTE_ROWS - len(rows)), axis=0)


def _pack_bf16_pairs(v):
    w = v.shape[1] // 2
    bits = pltpu.bitcast(v.astype(BF16).astype(F32), jnp.uint32)
    return pltpu.bitcast(bits[:, :w] | (bits[:, w:] >> 16), jnp.int32)


def _unpack_bf16_pairs(p):
    bits = pltpu.bitcast(p, jnp.uint32)
    hi = pltpu.bitcast(bits & jnp.uint32(0xFFFF0000), F32)
    lo = pltpu.bitcast(bits << 16, F32)
    return jnp.concatenate([hi, lo], axis=-1)


def _merge_kernel(xl_ref, xc_ref, mod_ref, n1_ref, n2_ref, al_ref, ac_ref, bl_ref, bc_ref, c_ref, d_ref,
                  wg_ref, bg_ref, wb_ref, wo_ref, rw_ref, rb_ref,
                  xo_ref, h2_ref, route_ref, merged_s, *, nw, sub, n_lat):
    sh1, sc1, g1, sh2, sc2, _ = _mod_slices(mod_ref[0])
    is_ctx = pl.program_id(1) >= n_lat
    for r in range(xl_ref.shape[1] // sub):
        rows = slice(r * sub, (r + 1) * sub)
        x = jnp.where(is_ctx, xc_ref[0, rows, :], xl_ref[0, rows, :])
        hb = _norm_mod(x, n1_ref[...], sh1, sc1).astype(BF16)
        branches = (jnp.where(is_ctx, ac_ref[0, rows, :], al_ref[0, rows, :]),
                    jnp.where(is_ctx, bc_ref[0, rows, :], bl_ref[0, rows, :]),
                    c_ref[0, rows, :], d_ref[0, rows, :])
        for n in range(D_MODEL // nw):
            cols = slice(n * nw, (n + 1) * nw)
            merged = None
            for i, br in enumerate(branches):
                gate = jax.nn.sigmoid(
                    jnp.dot(hb, wg_ref[i, :, cols], preferred_element_type=F32) + bg_ref[i, :, cols])
                term = gate * jnp.dot(br, wb_ref[i, :, cols], preferred_element_type=F32)
                merged = term if merged is None else merged + term
            merged_s[rows, cols] = merged.astype(BF16)
        y = jnp.dot(merged_s[rows, :], wo_ref[...], preferred_element_type=F32)
        xn = x + g1 * y
        xo_ref[0, rows, :] = xn
        h2 = _norm_mod(xn, n2_ref[...], sh2, sc2)
        h2_ref[0, rows, :] = _pack_bf16_pairs(h2)
        h_hi = h2.astype(BF16)
        h_lo = (h2 - h_hi.astype(F32)).astype(BF16)
        nt = (((1,), (1,)), ((), ()))
        by_hi = lax.dot_general(rw_ref[...], h_hi, nt, preferred_element_type=F32)
        by_lo = lax.dot_general(rw_ref[:N_EXPERTS, :], h_lo, nt, preferred_element_type=F32)
        logits_t = by_hi[:N_EXPERTS] + by_hi[N_EXPERTS:] + by_lo
        route_ref[0, :, rows] = _route(logits_t, rb_ref[...])


def _merge(xl, xc, mod, n1, n2, br_al, br_ac, br_bl, br_bc, br_c, br_d, wg, bg, wb, wo, rw_t, rb, s_out,
           tm=256):
    b, t_lat, d = xl.shape
    n_lat = t_lat // tm
    tok = lambda bi, j: (bi, j, 0)
    br_spec = pl.BlockSpec((1, tm, BRANCH_W), tok)
    return pl.pallas_call(
        functools.partial(_merge_kernel, nw=512, sub=tm, n_lat=n_lat),
        scratch_shapes=[pltpu.VMEM((tm, d), BF16)],
        grid=(b, s_out // tm),
        in_specs=_stream_specs(tm, d, n_lat) + [
            pl.BlockSpec((1, 1, 6 * d), lambda bi, j: (jnp.where(j >= n_lat, b, bi), 0, 0)),
            _const_spec((1, d)), _const_spec((1, d)),
            *_stream_specs(tm, BRANCH_W, n_lat), *_stream_specs(tm, BRANCH_W, n_lat), br_spec, br_spec,
            _const_spec(wg.shape), _const_spec(bg.shape), _const_spec(wb.shape), _const_spec(wo.shape),
            _const_spec(rw_t.shape), _const_spec(rb.shape),
        ],
        out_specs=[
            pl.BlockSpec((1, tm, d), tok),
            pl.BlockSpec((1, tm, d // 2), tok),
            pl.BlockSpec((1, ROUTE_ROWS, tm), lambda bi, j: (bi, 0, j)),
        ],
        out_shape=[
            jax.ShapeDtypeStruct((b, s_out, d), F32),
            jax.ShapeDtypeStruct((b, s_out, d // 2), jnp.int32),
            jax.ShapeDtypeStruct((b, ROUTE_ROWS, s_out), F32),
        ],
        compiler_params=_cparams(("parallel", "arbitrary"), 56),
        name="merge",
    )(xl, xc, mod, n1, n2, br_al, br_ac, br_bl, br_bc, br_c, br_d, wg, bg, wb, wo, rw_t, rb)


def _rank_kernel(cls_ref, rank_ref, cnt_ref, cnt_s, *, tr):
    @pl.when(pl.program_id(0) == 0)
    def _():
        cnt_s[...] = jnp.zeros_like(cnt_s)

    cls = cls_ref[0]
    cid = lax.broadcasted_iota(jnp.int32, (CLASS_ROWS, tr), 0).astype(F32)
    onehot = cid == cls
    before = lax.broadcasted_iota(jnp.int32, (tr, tr), 0) < lax.broadcasted_iota(jnp.int32, (tr, tr), 1)
    prefix = jnp.dot(jnp.where(onehot, 1.0, 0.0).astype(BF16), jnp.where(before, 1.0, 0.0).astype(BF16),
                     preferred_element_type=F32)
    carry = cnt_s[...][:, 0:1]
    rank_ref[0] = jnp.sum(jnp.where(onehot, prefix + carry, 0.0), axis=0, keepdims=True)
    cnt_s[...] += jnp.sum(jnp.where(onehot, 1.0, 0.0), axis=1, keepdims=True)
    cnt_ref[...] = cnt_s[...]


def _rank(cls_flat, tr=512):
    n = cls_flat.shape[0]
    tr = math.gcd(n, tr)
    cls3 = cls_flat.reshape(n // tr, 1, tr)
    rank, cnt = pl.pallas_call(
        functools.partial(_rank_kernel, tr=tr),
        grid=(n // tr,),
        in_specs=[pl.BlockSpec((1, 1, tr), lambda i: (i, 0, 0))],
        out_specs=[pl.BlockSpec((1, 1, tr), lambda i: (i, 0, 0)), _const_spec((CLASS_ROWS, LANES))],
        out_shape=[jax.ShapeDtypeStruct((n // tr, 1, tr), F32), jax.ShapeDtypeStruct((CLASS_ROWS, LANES), F32)],
        scratch_shapes=[pltpu.VMEM((CLASS_ROWS, LANES), F32)],
        compiler_params=_cparams(("arbitrary",), 32),
        name="rank",
    )(cls3)
    return rank.reshape(n), cnt[:N_CLASSES, 0]


def _sc_layout(n):
    info = plsc.get_sparse_core_info()
    nw = info.num_cores * info.num_subcores
    per_worker = n // nw
    assert per_worker * nw == n
    chunk = max(c for c in range(8, SC_MAX_CHUNK + 1, 8) if per_worker % c == 0)
    return info.num_cores, nw, per_worker // chunk, chunk


def _sc_scatter_rows(src, wts, pos, n_out):
    n, w = src.shape
    nc, nw, k, c = _sc_layout(n)
    mesh = plsc.VectorSubcoreMesh(core_axis_name="c", subcore_axis_name="s")

    @functools.partial(
        pl.kernel, mesh=mesh,
        out_type=(jax.ShapeDtypeStruct((n_out, w), src.dtype), jax.ShapeDtypeStruct((n_out, wts.shape[1]), wts.dtype)),
        scratch_types=[pltpu.VMEM((k, c), jnp.int32), pltpu.VMEM((c, w), src.dtype),
                       pltpu.VMEM((c, wts.shape[1]), wts.dtype), pltpu.SemaphoreType.DMA],
        name="moe_scatter",
    )
    def scatter(src_hbm, wts_hbm, pos_hbm, out_hbm, wout_hbm, idx_v, rows_v, wrows_v, sem):
        wid = lax.axis_index("s") * nc + lax.axis_index("c")
        pltpu.sync_copy(pos_hbm.at[wid], idx_v)

        @pl.loop(0, k)
        def _(j):
            off = pl.multiple_of(wid * (k * c) + j * c, 8)
            pltpu.sync_copy(src_hbm.at[pl.ds(off, c)], rows_v)
            pltpu.sync_copy(wts_hbm.at[pl.ds(off, c)], wrows_v)
            pltpu.async_copy(rows_v, out_hbm.at[idx_v.at[j]], sem).wait()
            pltpu.async_copy(wrows_v, wout_hbm.at[idx_v.at[j]], sem).wait()

    return scatter(src, wts, pos.reshape(nw, k, c))


def _sc_gather_rows(src, pos):
    n = pos.shape[0]
    w = src.shape[1]
    nc, nw, k, c = _sc_layout(n)
    mesh = plsc.VectorSubcoreMesh(core_axis_name="c", subcore_axis_name="s")

    @functools.partial(
        pl.kernel, mesh=mesh,
        out_type=jax.ShapeDtypeStruct((n, w), src.dtype),
        scratch_types=[pltpu.VMEM((k, c), jnp.int32), pltpu.VMEM((c, w), src.dtype), pltpu.SemaphoreType.DMA],
        name="moe_gather",
    )
    def gather(src_hbm, pos_hbm, out_hbm, idx_v, rows_v, sem):
        wid = lax.axis_index("s") * nc + lax.axis_index("c")
        pltpu.sync_copy(pos_hbm.at[wid], idx_v)

        @pl.loop(0, k)
        def _(j):
            off = pl.multiple_of(wid * (k * c) + j * c, 8)
            pltpu.async_copy(src_hbm.at[idx_v.at[j]], rows_v, sem).wait()
            pltpu.sync_copy(rows_v, out_hbm.at[pl.ds(off, c)])

    return gather(src, pos.reshape(nw, k, c))


def _gmm_kernel(lo_ref, hi_ref, nact_ref, h_ref, wt_ref, w13a_ref, w13b_ref, w2a_ref, w2b_ref, o_ref):
    @pl.when(pl.program_id(0) < nact_ref[0])
    def _():
        x = _unpack_bf16_pairs(h_ref[...]).astype(BF16)
        wts = wt_ref[...]

        def expert(w13_ref, w2_ref):
            ab = jnp.dot(x, w13_ref[0], preferred_element_type=F32)
            a, gate = ab[:, :EXPERT_FF], ab[:, EXPERT_FF:]
            hid = (a * jax.nn.sigmoid(a)) * gate
            return jnp.dot(hid.astype(BF16), w2_ref[0], preferred_element_type=F32)

        y = wts[:, 0:1] * expert(w13a_ref, w2a_ref) + wts[:, 1:2] * expert(w13b_ref, w2b_ref)
        o_ref[...] = _pack_bf16_pairs(y)


def _gmm(tile_lo, tile_hi, n_act, hs, ws, w13, w2, tm):
    n_pad, half = hs.shape
    d = 2 * half
    row = lambda t, lo, hi, na: (jnp.minimum(t, na[0] - 1), 0)
    e_lo = lambda t, lo, hi, na: (lo[jnp.minimum(t, na[0] - 1)], 0, 0)
    e_hi = lambda t, lo, hi, na: (hi[jnp.minimum(t, na[0] - 1)], 0, 0)
    return pl.pallas_call(
        _gmm_kernel,
        grid_spec=pltpu.PrefetchScalarGridSpec(
            num_scalar_prefetch=3,
            grid=(n_pad // tm,),
            in_specs=[
                pl.BlockSpec((tm, half), row),
                pl.BlockSpec((tm, ws.shape[1]), row),
                pl.BlockSpec((1, d, 2 * EXPERT_FF), e_lo),
                pl.BlockSpec((1, d, 2 * EXPERT_FF), e_hi),
                pl.BlockSpec((1, EXPERT_FF, d), e_lo),
                pl.BlockSpec((1, EXPERT_FF, d), e_hi),
            ],
            out_specs=pl.BlockSpec((tm, half), row),
        ),
        out_shape=jax.ShapeDtypeStruct((n_pad, half), jnp.int32),
        compiler_params=_cparams(("arbitrary",), 48),
        name="moe_gmm",
    )(tile_lo, tile_hi, n_act, hs, ws, w13, w13, w2, w2)


def _moe_routed(h2p, route, w13, w2, tm=256):
    b, s, half = h2p.shape
    n = b * s
    n_pad = n + N_CLASSES * tm
    cls = route[:, 0, :].reshape(n)
    rank, counts = _rank(cls)
    counts = counts.astype(jnp.int32)
    padded = (counts + tm - 1) // tm * tm
    ends = jnp.cumsum(padded)
    pos = jnp.take(ends - padded, cls.astype(jnp.int32)) + rank.astype(jnp.int32)
    n_act = (ends[-1] // tm).reshape(1)
    tile_row = jnp.arange(n_pad // tm, dtype=jnp.int32) * tm
    tile_cls = jnp.minimum(jnp.sum(tile_row[:, None] >= ends[None, :], axis=1), N_CLASSES - 1)
    pair_lo, pair_hi = (jnp.asarray(a, jnp.int32) for a in _class_experts())
    wts = jnp.concatenate([route[:, 1, :].reshape(n, 1), route[:, 2, :].reshape(n, 1),
                           jnp.zeros((n, SC_WEIGHT_COLS - 2), F32)], axis=1)
    hs, ws = _sc_scatter_rows(h2p.reshape(n, half), wts, pos, n_pad)
    ys = _gmm(jnp.take(pair_lo, tile_cls), jnp.take(pair_hi, tile_cls), n_act, hs, ws, w13, w2, tm)
    return _sc_gather_rows(ys, pos).reshape(b, s, half)


def _class_experts():
    lo, hi = [], []
    for g in range(N_EXPERTS // EXPERTS_PER_GROUP):
        for i in range(EXPERTS_PER_GROUP):
            for j in range(i + 1, EXPERTS_PER_GROUP):
                lo.append(EXPERTS_PER_GROUP * g + i)
                hi.append(EXPERTS_PER_GROUP * g + j)
    return np.array(lo), np.array(hi)


def _residual_kernel(x_ref, y_ref, mod_ref, g_ref, *out_refs, final, n_lat):
    g2 = _mod_slices(mod_ref[0])[5]
    x = x_ref[0] + g2 * _unpack_bf16_pairs(y_ref[0])
    if final:
        out_refs[0][0] = x * lax.rsqrt(jnp.mean(x * x, axis=-1, keepdims=True) + EPS) * g_ref[...]
        return
    ol_ref, oc_ref = out_refs
    is_ctx = pl.program_id(1) >= n_lat

    @pl.when(is_ctx)
    def _():
        oc_ref[0] = x

    @pl.when(jnp.logical_not(is_ctx))
    def _():
        ol_ref[0] = x


def _residual(x1, yp, mod, gain, t_lat, final, tm=256):
    b, s, d = x1.shape
    n_lat = t_lat // tm
    tok = lambda bi, j: (bi, j, 0)
    if final:
        out_specs = pl.BlockSpec((1, tm, d), tok)
        out_shape = jax.ShapeDtypeStruct((b, s, d), F32)
    else:
        out_specs = _stream_specs(tm, d, n_lat)
        out_shape = [jax.ShapeDtypeStruct((b, t_lat, d), F32), jax.ShapeDtypeStruct((b, s - t_lat, d), F32)]
    return pl.pallas_call(
        functools.partial(_residual_kernel, final=final, n_lat=n_lat),
        grid=(b, s // tm),
        in_specs=[
            pl.BlockSpec((1, tm, d), tok),
            pl.BlockSpec((1, tm, d // 2), tok),
            pl.BlockSpec((1, 1, 6 * d), lambda bi, j: (jnp.where(j >= n_lat, b, bi), 0, 0)),
            _const_spec((1, d)),
        ],
        out_specs=out_specs,
        out_shape=out_shape,
        compiler_params=_cparams(("arbitrary", "arbitrary"), 32),
        name="residual",
    )(x1, yp, mod, gain)


def _rope_tables(t_lat, n_ctx):
    rows = t_lat // GRID_W
    row = jnp.repeat(jnp.arange(rows, dtype=F32), GRID_W)
    col = jnp.tile(jnp.arange(GRID_W, dtype=F32), rows)
    inv_freq = ROPE_THETA ** (-jnp.arange(0, ROPE_AXIS_DIM, 2, dtype=F32) / ROPE_AXIS_DIM)
    ang = jnp.stack([row[:, None] * inv_freq, col[:, None] * inv_freq], axis=1)
    cos, sin = jnp.cos(ang), jnp.sin(ang)
    zero = jnp.zeros_like(sin)
    cos_h = jnp.concatenate([cos, cos], axis=-1).reshape(t_lat, HEAD_DIM)
    sa_h = jnp.concatenate([-sin, zero], axis=-1).reshape(t_lat, HEAD_DIM)
    sb_h = jnp.concatenate([zero, sin], axis=-1).reshape(t_lat, HEAD_DIM)
    reps = LANES // HEAD_DIM
    pad = lambda tbl, fill: jnp.concatenate(
        [jnp.tile(tbl, (1, reps)), jnp.full((n_ctx, LANES), fill, F32)], axis=0)
    return pad(cos_h, 1.0), pad(sa_h, 0.0), pad(sb_h, 0.0)


def _permute_heads(w, axis):
    shp = w.shape
    w = w.reshape(shp[:axis] + (N_Q_HEADS, HEAD_DIM) + shp[axis + 1:])
    w = jnp.take(w, jnp.array(HEAD_PERM), axis=axis)
    return w.reshape(shp)


def _square_factor(n):
    r = int(round(math.sqrt(n)))
    assert r * r == n, "sequence lengths must be perfect squares for the two-stage DFT"
    return r


def kernel(x, c, ctx, c_ctx, w_ada, b_ada, norm1, norm2, w_in, q_gain, k_gain, sink, pool_w, pool_scale,
           w_branch, w_gate, b_gate, w_out, router_w, router_bias, w1, w3, w2, norm_f):
    b, t_lat, d = x.shape
    n_ctx = ctx.shape[1]
    s = t_lat + n_ctx
    depth = w_ada.shape[0]
    assert d == D_MODEL and b < MOD_ROWS and t_lat % 256 == 0 and n_ctx % 256 == 0 and t_lat % n_ctx == 0

    xl, xc = x, ctx
    cc = jnp.zeros((MOD_ROWS, d), F32).at[:b].set(c).at[b].set(c_ctx)
    mod_all = _ada(cc, w_ada, b_ada).reshape(depth, MOD_ROWS, 1, 6 * d)

    cos, sa, sb = _rope_tables(t_lat, n_ctx)
    seg = jnp.asarray(np.kron(np.eye(N_Q_HEADS), np.full((HEAD_DIM, HEAD_DIM), 1.0 / HEAD_DIM)), BF16)
    cs = jnp.asarray(_channel_dft_table()).astype(BF16)
    f_lat = [jnp.asarray(a).astype(BF16) for a in _fourier_tables(*(_square_factor(t_lat),) * 2)]
    f_ctx = [jnp.asarray(a).astype(BF16) for a in _fourier_tables(*(_square_factor(n_ctx),) * 2)]
    wbias = jnp.asarray(_window_bias(n_ctx))
    rw_hi = router_w.T.astype(BF16)
    rw_t = jnp.concatenate([rw_hi, (router_w.T - rw_hi.astype(F32)).astype(BF16)], axis=0)
    rb = router_bias.reshape(N_EXPERTS, 1)

    for l in range(depth):
        need_ctx = l < depth - 1
        s_out = s if need_ctx else t_lat
        cols = jnp.split(w_in[l], np.cumsum((512, 512, 512, 128, 128, 512, 128))[:], axis=1)
        f_w, p_w, qb_w, kb_w, vb_w, qw_w, kw_w, vw_w = cols
        w_in_l = jnp.concatenate([f_w, p_w, _permute_heads(qb_w, 1), _permute_heads(qw_w, 1),
                                  kb_w, vb_w, kw_w, vw_w], axis=1).astype(BF16)
        wb_l = jnp.stack([w_branch[l, 0], _permute_heads(w_branch[l, 1], 0),
                          _permute_heads(w_branch[l, 2], 0), w_branch[l, 3]]).astype(BF16)
        mod = mod_all[l]
        n1 = norm1[l].reshape(1, d)
        n2 = norm2[l].reshape(1, d)
        qg = jnp.tile(q_gain[l], N_Q_HEADS).reshape(1, BRANCH_W)
        kg = jnp.tile(k_gain[l], LANES // HEAD_DIM).reshape(1, LANES)

        zr, zi, p_in, qb, qw, kvb, kvw = _inproj(xl, xc, mod, n1, w_in_l, qg, kg, seg, cs, cos, sa, sb)

        out_b = _gattn(qb, kvb, 0, t_lat, 0, s, GATTN_KEY_TILES)
        out_bc = _gattn(qb, kvb, t_lat, n_ctx, t_lat, n_ctx, 1) if need_ctx else out_b
        out_c = _wattn(jnp.take(sink[l], jnp.array(HEAD_PERM)) * LOG2E, wbias, qw, kvw, s_out, t_lat)
        out_a = _fourier(zr, zi, f_lat[0], f_lat[1], t_lat, 0)
        out_ac = _fourier(zr, zi, f_ctx[0], f_ctx[1], n_ctx, t_lat // n_ctx) if need_ctx else out_a
        out_d = _pool(p_in, pool_w[l].astype(BF16), pool_scale[l].reshape(1, BRANCH_W), s_out, t_lat)

        x1, h2p, route = _merge(xl, xc, mod, n1, n2, out_a, out_ac, out_b, out_bc, out_c, out_d,
                                w_gate[l].astype(BF16), b_gate[l].reshape(4, 1, d), wb_l,
                                w_out[l].astype(BF16), rw_t, rb, s_out)
        w13 = jnp.concatenate([w1[l], w3[l]], axis=-1).astype(BF16)
        yp = _moe_routed(h2p, route, w13, w2[l].astype(BF16))
        if l == depth - 1:
            return _residual(x1, yp, mod, norm_f.reshape(1, d), t_lat, final=True)
        xl, xc = _residual(x1, yp, mod, norm_f.reshape(1, d), t_lat, final=False)
```

```python
import functools
import math

import numpy as np
import jax
import jax.numpy as jnp
from jax import lax
from jax.experimental import pallas as pl
from jax.experimental.pallas import tpu as pltpu
from jax.experimental.pallas import tpu_sc as plsc

F32 = jnp.float32
BF16 = jnp.bfloat16

D_MODEL = 1024
HEAD_DIM = 64
N_Q_HEADS = 8
N_KV_HEADS = 2
GRID_W = 64
ROPE_THETA = 10000.0
ROPE_AXIS_DIM = HEAD_DIM // 2
QBLK = 128
WINDOW = 128
BRANCH_W = 512
GROUP_W = 128
POOL_WINDOWS = (2, 4, 8, 16)
N_EXPERTS = 16
EXPERTS_PER_GROUP = 4
EXPERT_FF = 512
EPS = 1e-6
MOD_ROWS = 16
NEG_BIG = -1e30
LOG2E = math.log2(math.e)
LANES = 128
POOL_HALO = 16
PAIRS_PER_GROUP = 6
N_CLASSES = 24
CLASS_ROWS = 32
ROUTE_ROWS = 8
SC_MAX_CHUNK = 128
SC_WEIGHT_COLS = 128

HEAD_PERM = (0, 4, 1, 5, 2, 6, 3, 7)


def _cparams(sem, vmem_mb):
    return pltpu.CompilerParams(dimension_semantics=sem, vmem_limit_bytes=vmem_mb * 1024 * 1024)


def _const_spec(shape):
    nd = len(shape)
    return pl.BlockSpec(shape, lambda *_: (0,) * nd)


def _ada_kernel(c_ref, w_ref, b_ref, o_ref):
    c = c_ref[...]
    s = c * jax.nn.sigmoid(c)
    o_ref[0] = jnp.dot(s.astype(BF16), w_ref[0].astype(BF16), preferred_element_type=F32) + b_ref[0]


def _ada(cc, w_ada, b_ada):
    depth, d, n = w_ada.shape
    tn = 1536
    return pl.pallas_call(
        _ada_kernel,
        grid=(depth, n // tn),
        in_specs=[
            pl.BlockSpec((MOD_ROWS, d), lambda l, j: (0, 0)),
            pl.BlockSpec((1, d, tn), lambda l, j: (l, 0, j)),
            pl.BlockSpec((1, 1, tn), lambda l, j: (l, 0, j)),
        ],
        out_specs=pl.BlockSpec((1, MOD_ROWS, tn), lambda l, j: (l, 0, j)),
        out_shape=jax.ShapeDtypeStruct((depth, MOD_ROWS, n), F32),
        compiler_params=_cparams(("arbitrary", "arbitrary"), 40),
        name="ada",
    )(cc, w_ada, b_ada.reshape(depth, 1, n))


def _norm_mod(x, gain, shift, scale):
    ms = jnp.mean(x * x, axis=-1, keepdims=True)
    return (x * lax.rsqrt(ms + EPS) * gain) * (1.0 + scale) + shift


def _mod_slices(m):
    d = D_MODEL
    return [m[:, i * d:(i + 1) * d] for i in range(6)]


def _head_norm(z, seg, gain):
    ms = jnp.dot((z * z).astype(BF16), seg, preferred_element_type=F32)
    return z * lax.rsqrt(ms + EPS) * gain


def _rope(z, cos, sin_a, sin_b):
    outs = []
    for c in range(z.shape[1] // LANES):
        zc = z[:, c * LANES:(c + 1) * LANES]
        nxt = pltpu.roll(zc, LANES - ROPE_AXIS_DIM // 2, 1)
        prv = pltpu.roll(zc, ROPE_AXIS_DIM // 2, 1)
        outs.append(zc * cos + nxt * sin_a + prv * sin_b)
    return outs[0] if len(outs) == 1 else jnp.concatenate(outs, axis=-1)


def _stream_specs(tm, d, n_lat):
    return [pl.BlockSpec((1, tm, d), lambda bi, j: (bi, jnp.minimum(j, n_lat - 1), 0)),
            pl.BlockSpec((1, tm, d), lambda bi, j: (bi, jnp.maximum(j - n_lat, 0), 0))]


def _inproj_kernel(xl_ref, xc_ref, mod_ref, n1_ref, w_ref, qg_ref, kg_ref, seg_ref, cs_ref,
                   cos_ref, sa_ref, sb_ref,
                   zr_ref, zi_ref, p_ref, qb_ref, qw_ref, kvb_ref, kvw_ref, *, n_lat):
    sh1, sc1 = _mod_slices(mod_ref[0])[:2]
    x = jnp.where(pl.program_id(1) >= n_lat, xc_ref[0], xl_ref[0])
    h = _norm_mod(x, n1_ref[...], sh1, sc1)
    u = jnp.dot(h.astype(BF16), w_ref[...], preferred_element_type=F32)
    cos, sa, sb = cos_ref[...], sa_ref[...], sb_ref[...]
    w = BRANCH_W
    f_in = u[:, 0:w].astype(BF16)
    zr, zi = [], []
    for g in range(w // GROUP_W):
        z = jnp.dot(f_in[:, g * GROUP_W:(g + 1) * GROUP_W], cs_ref[...], preferred_element_type=F32)
        zr.append(z[:, :GROUP_W])
        zi.append(z[:, GROUP_W:])
    zr_ref[0] = jnp.concatenate(zr, axis=-1).astype(BF16)
    zi_ref[0] = jnp.concatenate(zi, axis=-1).astype(BF16)
    p_ref[0] = u[:, w:2 * w].astype(BF16)
    seg = seg_ref[...]
    qb = _rope(_head_norm(u[:, 2 * w:3 * w], seg, qg_ref[...]), cos, sa, sb)
    qb_ref[0] = (qb * (HEAD_DIM ** -0.5 * LOG2E)).astype(BF16)
    qw = _rope(u[:, 3 * w:4 * w], cos, sa, sb)
    qw_ref[0] = (qw * (HEAD_DIM ** -0.5 * LOG2E)).astype(BF16)
    o = 4 * w
    kb = _rope(_head_norm(u[:, o:o + LANES], seg[:LANES, :LANES], kg_ref[...]), cos, sa, sb)
    vb = u[:, o + LANES:o + 2 * LANES]
    kw = _rope(u[:, o + 2 * LANES:o + 3 * LANES], cos, sa, sb)
    vw = u[:, o + 3 * LANES:o + 4 * LANES]
    kvb_ref[0] = jnp.concatenate([kb, vb, jnp.ones_like(vb)], axis=-1).astype(BF16)
    kvw_ref[0] = jnp.concatenate([kw, vw, jnp.ones_like(vw)], axis=-1).astype(BF16)


def _inproj(xl, xc, mod, n1, w_in, qg, kg, seg, cs, cos, sa, sb, tm=256):
    b, t_lat, d = xl.shape
    s = t_lat + xc.shape[1]
    nw = w_in.shape[1]
    n_lat = t_lat // tm
    tok = lambda bi, j: (bi, j, 0)
    tab = lambda bi, j: (j, 0)
    widths = (BRANCH_W,) * 5 + (3 * LANES, 3 * LANES)
    return pl.pallas_call(
        functools.partial(_inproj_kernel, n_lat=n_lat),
        grid=(b, s // tm),
        in_specs=_stream_specs(tm, d, n_lat) + [
            pl.BlockSpec((1, 1, 6 * d), lambda bi, j: (jnp.where(j >= n_lat, b, bi), 0, 0)),
            _const_spec((1, d)),
            _const_spec((d, nw)),
            _const_spec((1, BRANCH_W)),
            _const_spec((1, LANES)),
            _const_spec((BRANCH_W, BRANCH_W)),
            _const_spec((GROUP_W, 2 * GROUP_W)),
            pl.BlockSpec((tm, LANES), tab),
            pl.BlockSpec((tm, LANES), tab),
            pl.BlockSpec((tm, LANES), tab),
        ],
        out_specs=[pl.BlockSpec((1, tm, wd), tok) for wd in widths],
        out_shape=[jax.ShapeDtypeStruct((b, s, wd), BF16) for wd in widths],
        compiler_params=_cparams(("parallel", "arbitrary"), 48),
        name="inproj",
    )(xl, xc, mod, n1, w_in, qg, kg, seg, cs, cos, sa, sb)


def _split_heads(qc, lane):
    zero = jnp.zeros_like(qc)
    return jnp.concatenate([jnp.where(lane < HEAD_DIM, qc, zero),
                            jnp.where(lane >= HEAD_DIM, qc, zero)], axis=0)


def _gattn_kernel(q_ref, kv_ref, o_ref, q2_s, m_s, acc_s, *, n_tiles, sub):
    tq = q_ref.shape[1]
    tk = kv_ref.shape[1] // n_tiles
    lane = lax.broadcasted_iota(jnp.int32, (1, LANES), 1)
    nt = (((1,), (1,)), ((), ()))
    n_chunks = BRANCH_W // LANES
    for c in range(n_chunks):
        q2_s[2 * c * tq:(2 * c + 2) * tq, :] = _split_heads(q_ref[0, :, c * LANES:(c + 1) * LANES], lane)

    def tile(k, v, first):
        for r in range(2 * n_chunks * tq // sub):
            rows = slice(r * sub, (r + 1) * sub)
            s = lax.dot_general(q2_s[rows, :], k, nt, preferred_element_type=F32)
            rm = jnp.max(s, axis=-1, keepdims=True)
            if first:
                m_new = rm
                acc_s[rows, :] = jnp.dot(jnp.exp2(s - m_new).astype(BF16), v, preferred_element_type=F32)
            else:
                m_old = m_s[rows, :]
                m_new = jnp.maximum(m_old, rm)
                alpha = jnp.exp2(m_old - m_new)
                pv = jnp.dot(jnp.exp2(s - m_new).astype(BF16), v, preferred_element_type=F32)
                acc_s[rows, :] = alpha * acc_s[rows, :] + pv
            m_s[rows, :] = m_new

    for t in range(n_tiles):
        keys = slice(t * tk, (t + 1) * tk)
        tile(kv_ref[0, keys, 0:LANES], kv_ref[0, keys, LANES:3 * LANES], t == 0)
    for c in range(n_chunks):
        lo = acc_s[2 * c * tq:(2 * c + 1) * tq, :]
        hi = acc_s[(2 * c + 1) * tq:(2 * c + 2) * tq, :]
        o = jnp.where(lane < HEAD_DIM, lo[:, :LANES] / lo[:, LANES:], hi[:, :LANES] / hi[:, LANES:])
        o_ref[0, :, c * LANES:(c + 1) * LANES] = o.astype(BF16)


def _gattn(qb, kv, q_start, q_len, k_start, k_len, n_tiles, tq=256, sub=128):
    b = qb.shape[0]
    assert q_start % tq == 0 and q_len % tq == 0 and k_start % k_len == 0 and k_len % (n_tiles * LANES) == 0
    rows = 2 * tq * (BRANCH_W // LANES)
    return pl.pallas_call(
        functools.partial(_gattn_kernel, n_tiles=n_tiles, sub=sub),
        grid=(b, q_len // tq),
        in_specs=[
            pl.BlockSpec((1, tq, BRANCH_W), lambda bi, j: (bi, q_start // tq + j, 0)),
            pl.BlockSpec((1, k_len, 3 * LANES), lambda bi, j: (bi, k_start // k_len, 0)),
        ],
        out_specs=pl.BlockSpec((1, tq, BRANCH_W), lambda bi, j: (bi, j, 0)),
        out_shape=jax.ShapeDtypeStruct((b, q_len, BRANCH_W), BF16),
        scratch_shapes=[pltpu.VMEM((rows, LANES), BF16), pltpu.VMEM((rows, 1), F32),
                        pltpu.VMEM((rows, 2 * LANES), F32)],
        compiler_params=_cparams(("parallel", "arbitrary"), 48),
        name="gattn",
    )(qb, kv)


WATTN_QBLOCKS = 2
GATTN_KEY_TILES = 2
MERGE_CHAIN_ROWS = 256


def _window_bias(n_ctx):
    tq = WATTN_QBLOCKS * QBLK
    qi = np.arange(tq)[:, None]
    kj = np.arange(tq + 2 * QBLK)[None, :]
    band = np.abs(kj - WINDOW - qi) <= WINDOW
    blk = kj // QBLK
    variants = [band & (blk != 0), band, band & (blk != WATTN_QBLOCKS + 1), np.zeros_like(band)]
    out = [np.concatenate([np.ones((tq, n_ctx), bool), v], axis=1) for v in variants]
    return np.where(np.stack(out), 0.0, NEG_BIG).astype(np.float32)


def _wattn_kernel(sink_ref, bias_ref, q_ref, *refs, sub):
    o_ref, q2_s, kv_s, acc_s = refs[-4:]
    tq = q_ref.shape[1]
    off = 0
    for blk in refs[:-4]:
        kv_s[off:off + blk.shape[1], :] = blk[0]
        off += blk.shape[1]
    lane = lax.broadcasted_iota(jnp.int32, (1, LANES), 1)
    nt = (((1,), (1,)), ((), ()))
    n_chunks = BRANCH_W // LANES
    for c in range(n_chunks):
        q2_s[2 * c * tq:(2 * c + 2) * tq, :] = _split_heads(q_ref[0, :, c * LANES:(c + 1) * LANES], lane)
    k, v = kv_s[:, 0:LANES], kv_s[:, LANES:3 * LANES]
    for r in range(2 * n_chunks * tq // sub):
        rows = slice(r * sub, (r + 1) * sub)
        q_off = (r * sub) % tq
        sk = sink_ref[(r * sub) // tq]
        s = lax.dot_general(q2_s[rows, :], k, nt, preferred_element_type=F32) + bias_ref[0, q_off:q_off + sub, :]
        m = jnp.maximum(jnp.max(s, axis=-1, keepdims=True), sk)
        pv = jnp.dot(jnp.exp2(s - m).astype(BF16), v, preferred_element_type=F32)
        acc_s[rows, :LANES] = pv[:, :LANES]
        acc_s[rows, LANES:] = pv[:, LANES:] + jnp.exp2(sk - m)
    for c in range(n_chunks):
        lo = acc_s[2 * c * tq:(2 * c + 1) * tq, :]
        hi = acc_s[(2 * c + 1) * tq:(2 * c + 2) * tq, :]
        o = jnp.where(lane < HEAD_DIM, lo[:, :LANES] / lo[:, LANES:], hi[:, :LANES] / hi[:, LANES:])
        o_ref[0, :, c * LANES:(c + 1) * LANES] = o.astype(BF16)


def _wattn(sink, bias, qw, kv, s_out, t_lat):
    b, s, _ = qw.shape
    n_ctx = s - t_lat
    nq = WATTN_QBLOCKS
    tq = nq * QBLK
    assert n_ctx % tq == 0 and t_lat // tq >= 2
    last = s // QBLK - 1
    n_lat = t_lat // tq
    variant = lambda j: jnp.where(j >= n_lat, 3, jnp.where(j == 0, 0, jnp.where(j == n_lat - 1, 2, 1)))
    key_block = lambda off: pl.BlockSpec(
        (1, QBLK, 3 * LANES), lambda bi, j: (bi, jnp.clip(j * nq + off, 0, last), 0))
    rows = 2 * tq * (BRANCH_W // LANES)
    return pl.pallas_call(
        functools.partial(_wattn_kernel, sub=128),
        scratch_shapes=[pltpu.VMEM((rows, LANES), BF16), pltpu.VMEM((bias.shape[2], 3 * LANES), BF16),
                        pltpu.VMEM((rows, 2 * LANES), F32)],
        grid=(b, s_out // tq),
        in_specs=[
            pl.BlockSpec(memory_space=pltpu.SMEM),
            pl.BlockSpec((1,) + bias.shape[1:], lambda bi, j: (variant(j), 0, 0)),
            pl.BlockSpec((1, tq, BRANCH_W), lambda bi, j: (bi, j, 0)),
            pl.BlockSpec((1, n_ctx, 3 * LANES), lambda bi, j: (bi, t_lat // n_ctx, 0)),
        ] + [key_block(off) for off in range(-1, nq + 1)],
        out_specs=pl.BlockSpec((1, tq, BRANCH_W), lambda bi, j: (bi, j, 0)),
        out_shape=jax.ShapeDtypeStruct((b, s_out, BRANCH_W), BF16),
        compiler_params=_cparams(("parallel", "arbitrary"), 32),
        name="wattn",
    )(sink, bias, qw, *([kv] * (nq + 3)))


def _fourier_tables(n1, n2):
    t = n1 * n2
    k2 = np.arange(n2)[None, :, None]
    t2 = np.arange(n2)[None, None, :]
    t1 = np.arange(n1)[:, None, None]
    theta = 2.0 * np.pi * ((k2 * t2 * n1 + k2 * t1) % t) / t
    er, ei = np.cos(theta) / math.sqrt(n2), -np.sin(theta) / math.sqrt(n2)
    e = np.concatenate([np.concatenate([er, -ei], axis=2), np.concatenate([ei, er], axis=2)], axis=1)
    k1 = np.arange(n1)[:, None]
    phi = 2.0 * np.pi * ((k1 * np.arange(n1)[None, :]) % n1) / n1
    dcat = np.concatenate([np.cos(phi), np.sin(phi)], axis=1) / math.sqrt(n1)
    return e.astype(np.float32), dcat.astype(np.float32)


def _channel_dft_table():
    c = np.arange(GROUP_W)
    ang = 2.0 * np.pi * ((c[:, None] * c[None, :]) % GROUP_W) / GROUP_W
    return (np.concatenate([np.cos(ang), -np.sin(ang)], axis=1) / math.sqrt(GROUP_W)).astype(np.float32)


def _fourier_kernel(zr_ref, zi_ref, e_ref, d_ref, o_ref, xr_s, xi_s, yr_s, yi_s, *, n1, n2):
    nc = xr_s.shape[0]
    chunk = lambda c: slice(c * LANES, (c + 1) * LANES)

    def gather(ref, start, size, stride):
        return jnp.concatenate([ref[c, pl.ds(start, size, stride=stride), :] for c in range(nc)], axis=-1)

    for c in range(nc):
        xr_s[c] = zr_ref[0, :, chunk(c)].astype(F32)
        xi_s[c] = zi_ref[0, :, chunk(c)].astype(F32)
    for t1 in range(n1):
        xs = jnp.concatenate([gather(xr_s, t1, n2, n1), gather(xi_s, t1, n2, n1)], axis=0).astype(BF16)
        y = jnp.dot(e_ref[t1], xs, preferred_element_type=F32)
        for c in range(nc):
            yr_s[c, t1 * n2:(t1 + 1) * n2, :] = y[:n2, chunk(c)]
            yi_s[c, t1 * n2:(t1 + 1) * n2, :] = y[n2:, chunk(c)]
    for k2 in range(n2):
        ys = jnp.concatenate([gather(yr_s, k2, n1, n2), gather(yi_s, k2, n1, n2)], axis=0).astype(BF16)
        o = jnp.dot(d_ref[...], ys, preferred_element_type=F32)
        for c in range(nc):
            xr_s[c, pl.ds(k2, n1, stride=n2), :] = o[:, chunk(c)]
    for c in range(nc):
        o_ref[0, :, chunk(c)] = xr_s[c].astype(BF16)


def _fourier(zr, zi, e_tab, d_tab, t_len, row_block, cw=256):
    b = zr.shape[0]
    n1 = d_tab.shape[0]
    n2 = t_len // n1
    zspec = pl.BlockSpec((1, t_len, cw), lambda bi, j: (bi, row_block, j))
    return pl.pallas_call(
        functools.partial(_fourier_kernel, n1=n1, n2=n2),
        grid=(b, BRANCH_W // cw),
        in_specs=[zspec, zspec, _const_spec(e_tab.shape), _const_spec(d_tab.shape)],
        out_specs=pl.BlockSpec((1, t_len, cw), lambda bi, j: (bi, 0, j)),
        out_shape=jax.ShapeDtypeStruct((b, t_len, BRANCH_W), BF16),
        scratch_shapes=[pltpu.VMEM((cw // LANES, t_len, LANES), F32)] * 4,
        compiler_params=_cparams(("parallel", "arbitrary"), 48),
        name="fourier",
    )(zr, zi, e_tab, d_tab)


def _pool_bands(tp):
    off = np.arange(tp + 2 * POOL_HALO)[None, :] - POOL_HALO - np.arange(tp)[:, None]
    return np.stack([(off >= -(w // 2)) & (off <= w // 2 - 1) for w in POOL_WINDOWS]).astype(np.float32)


def _pool_kernel(pc_ref, pp_ref, pn_ref, band_ref, w_ref, sc_ref, o_ref, *, tp, t_lat, s_tot):
    row0 = pl.program_id(1) * tp
    in_lat = row0 < t_lat
    seq_lo = jnp.where(in_lat, 0, t_lat)
    seq_hi = jnp.where(in_lat, t_lat, s_tot)
    ext = jnp.concatenate([pp_ref[0], pc_ref[0], pn_ref[0]], axis=0).astype(F32)
    gpos = row0 - POOL_HALO + lax.broadcasted_iota(jnp.int32, (tp + 2 * POOL_HALO, 1), 0)
    ext = jnp.where((gpos >= seq_lo) & (gpos < seq_hi), ext, 0.0).astype(BF16)
    pos = row0 - seq_lo + lax.broadcasted_iota(jnp.int32, (tp, 1), 0)
    n = seq_hi - seq_lo
    outs = []
    for gi, w in enumerate(POOL_WINDOWS):
        cols = slice(gi * GROUP_W, (gi + 1) * GROUP_W)
        wsum = jnp.dot(band_ref[gi], ext[:, cols], preferred_element_type=F32)
        cnt = jnp.minimum(pos + w // 2, n) - jnp.maximum(pos - w // 2, 0)
        pooled = wsum / cnt.astype(F32) - pc_ref[0, :, cols].astype(F32)
        outs.append(jnp.dot(pooled.astype(BF16), w_ref[gi], preferred_element_type=F32))
    o_ref[0] = (jnp.concatenate(outs, axis=-1) * sc_ref[...]).astype(BF16)


def _pool(p, pool_w, pool_scale, s_out, t_lat, tp=256):
    b, s, _ = p.shape
    hb = tp // POOL_HALO
    last = s // POOL_HALO - 1
    bands = jnp.asarray(_pool_bands(tp)).astype(BF16)
    return pl.pallas_call(
        functools.partial(_pool_kernel, tp=tp, t_lat=t_lat, s_tot=s),
        grid=(b, s_out // tp),
        in_specs=[
            pl.BlockSpec((1, tp, BRANCH_W), lambda bi, j: (bi, j, 0)),
            pl.BlockSpec((1, POOL_HALO, BRANCH_W), lambda bi, j: (bi, jnp.maximum(j * hb - 1, 0), 0)),
            pl.BlockSpec((1, POOL_HALO, BRANCH_W), lambda bi, j: (bi, jnp.minimum((j + 1) * hb, last), 0)),
            _const_spec(bands.shape),
            _const_spec(pool_w.shape),
            _const_spec((1, BRANCH_W)),
        ],
        out_specs=pl.BlockSpec((1, tp, BRANCH_W), lambda bi, j: (bi, j, 0)),
        out_shape=jax.ShapeDtypeStruct((b, s_out, BRANCH_W), BF16),
        compiler_params=_cparams(("parallel", "arbitrary"), 32),
        name="pool",
    )(p, p, p, bands, pool_w, pool_scale)


def _route(logits_t, bias):
    aff = jax.nn.sigmoid(logits_t)
    sel = aff + bias
    neg = -jnp.inf
    firsts, seconds, scores = [], [], []
    for g in range(N_EXPERTS // EXPERTS_PER_GROUP):
        s = [sel[EXPERTS_PER_GROUP * g + k:EXPERTS_PER_GROUP * g + k + 1, :] for k in range(EXPERTS_PER_GROUP)]
        m1 = jnp.maximum(jnp.maximum(s[0], s[1]), jnp.maximum(s[2], s[3]))
        i1 = jnp.where(s[0] == m1, 0, jnp.where(s[1] == m1, 1, jnp.where(s[2] == m1, 2, 3)))
        r = [jnp.where(i1 == k, neg, s[k]) for k in range(EXPERTS_PER_GROUP)]
        m2 = jnp.maximum(jnp.maximum(r[0], r[1]), jnp.maximum(r[2], r[3]))
        i2 = jnp.where(r[0] == m2, 0, jnp.where(r[1] == m2, 1, jnp.where(r[2] == m2, 2, 3)))
        firsts.append(i1 + EXPERTS_PER_GROUP * g)
        seconds.append(i2 + EXPERTS_PER_GROUP * g)
        scores.append(m1 + m2)
    best = jnp.maximum(jnp.maximum(scores[0], scores[1]), jnp.maximum(scores[2], scores[3]))
    pick = lambda v: jnp.where(scores[0] == best, v[0], jnp.where(scores[1] == best, v[1],
                                                                 jnp.where(scores[2] == best, v[2], v[3])))
    e1, e2 = pick(firsts), pick(seconds)
    eidx = lax.broadcasted_iota(jnp.int32, aff.shape, 0)
    a1 = jnp.sum(jnp.where(eidx == e1, aff, 0.0), axis=0, keepdims=True)
    a2 = jnp.sum(jnp.where(eidx == e2, aff, 0.0), axis=0, keepdims=True)
    tot = a1 + a2
    w1, w2 = a1 / tot, a2 / tot
    swap = e1 > e2
    lo = jnp.where(swap, e2, e1) & (EXPERTS_PER_GROUP - 1)
    hi = jnp.where(swap, e1, e2) & (EXPERTS_PER_GROUP - 1)
    pair = jnp.where(lo == 0, 0, jnp.where(lo == 1, 3, 5)) + hi - lo - 1
    cls = (e1 >> 2) * PAIRS_PER_GROUP + pair
    rows = [cls.astype(F32), jnp.where(swap, w2, w1), jnp.where(swap, w1, w2)]
    return jnp.concatenate(rows + [jnp.zeros_like(w1)] * (ROUTE_ROWS - len(rows)), axis=0)


def _pack_bf16_pairs(v):
    w = v.shape[1] // 2
    bits = pltpu.bitcast(v.astype(BF16).astype(F32), jnp.uint32)
    return pltpu.bitcast(bits[:, :w] | (bits[:, w:] >> 16), jnp.int32)


def _unpack_bf16_pairs(p):
    bits = pltpu.bitcast(p, jnp.uint32)
    hi = pltpu.bitcast(bits & jnp.uint32(0xFFFF0000), F32)
    lo = pltpu.bitcast(bits << 16, F32)
    return jnp.concatenate([hi, lo], axis=-1)


def _merge_kernel(x_ref, mod_ref, n1_ref, n2_ref, a_ref, b_ref, c_ref, d_ref,
                  wg_ref, bg_ref, wb_ref, wo_ref, rw_ref, rb_ref, *refs, nw, sub, n_real, n_fill):
    xo_ref, h2_ref, route_ref, merged_s = refs[-4:]
    if n_fill:
        @pl.when(pl.program_id(1) >= n_real)
        def _():
            for ref in (xo_ref, h2_ref, route_ref):
                ref[...] = jnp.zeros_like(ref)

        pl.when(pl.program_id(1) < n_real)(
            lambda: _merge_tile(x_ref, mod_ref, n1_ref, n2_ref, a_ref, b_ref, c_ref, d_ref, wg_ref, bg_ref,
                                wb_ref, wo_ref, rw_ref, rb_ref, xo_ref, h2_ref, route_ref, merged_s, nw, sub))
    else:
        _merge_tile(x_ref, mod_ref, n1_ref, n2_ref, a_ref, b_ref, c_ref, d_ref, wg_ref, bg_ref,
                    wb_ref, wo_ref, rw_ref, rb_ref, xo_ref, h2_ref, route_ref, merged_s, nw, sub)


def _merge_tile(x_ref, mod_ref, n1_ref, n2_ref, a_ref, b_ref, c_ref, d_ref, wg_ref, bg_ref,
                wb_ref, wo_ref, rw_ref, rb_ref, xo_ref, h2_ref, route_ref, merged_s, nw, sub):
    sh1, sc1, g1, sh2, sc2, _ = _mod_slices(mod_ref[0])
    for r in range(x_ref.shape[1] // sub):
        rows = slice(r * sub, (r + 1) * sub)
        x = x_ref[0, rows, :]
        hb = _norm_mod(x, n1_ref[...], sh1, sc1).astype(BF16)
        branches = tuple(ref[0, rows, :] for ref in (a_ref, b_ref, c_ref, d_ref))
        for n in range(D_MODEL // nw):
            cols = slice(n * nw, (n + 1) * nw)
            merged = None
            for i, br in enumerate(branches):
                gate = jax.nn.sigmoid(
                    jnp.dot(hb, wg_ref[i, :, cols], preferred_element_type=F32) + bg_ref[i, :, cols])
                term = gate * jnp.dot(br, wb_ref[i, :, cols], preferred_element_type=F32)
                merged = term if merged is None else merged + term
            merged_s[rows, cols] = merged.astype(BF16)
        y = jnp.dot(merged_s[rows, :], wo_ref[...], preferred_element_type=F32)
        xn = x + g1 * y
        xo_ref[0, rows, :] = xn
        h2 = _norm_mod(xn, n2_ref[...], sh2, sc2)
        h2_ref[0, rows, :] = _pack_bf16_pairs(h2)
        h_hi = h2.astype(BF16)
        h_lo = (h2 - h_hi.astype(F32)).astype(BF16)
        nt = (((1,), (1,)), ((), ()))
        by_hi = lax.dot_general(rw_ref[...], h_hi, nt, preferred_element_type=F32)
        by_lo = lax.dot_general(rw_ref[:N_EXPERTS, :], h_lo, nt, preferred_element_type=F32)
        logits_t = by_hi[:N_EXPERTS] + by_hi[N_EXPERTS:] + by_lo
        route_ref[0, :, rows] = _route(logits_t, rb_ref[...])


def _merge(x, mod, mod_row, n1, n2, branches, offsets, weights, s_out, out_off, rows, tm, prev=None):
    b, _, d = x.shape
    wg, bg, wb, wo, rw_t, rb = weights
    n_real = rows // tm
    n_fill = -(-(s_out - out_off - rows) // tm) if prev is None else 0
    blk = lambda width, off: pl.BlockSpec(
        (1, tm, width), lambda bi, j: (bi, off // tm + jnp.minimum(j, n_real - 1), 0))
    out_blk = lambda width: pl.BlockSpec((1, tm, width), lambda bi, j: (bi, out_off // tm + j, 0))
    assert all(o % tm == 0 for o in offsets) and out_off % tm == 0 and rows % tm == 0
    out_shape = [
        jax.ShapeDtypeStruct((b, s_out, d), F32),
        jax.ShapeDtypeStruct((b, s_out, d // 2), jnp.int32),
        jax.ShapeDtypeStruct((b, ROUTE_ROWS, s_out), F32),
    ]
    n_in = 14
    extra_specs = [] if prev is None else [pl.BlockSpec(memory_space=pl.ANY)] * 3
    return pl.pallas_call(
        functools.partial(_merge_kernel, nw=512, sub=MERGE_CHAIN_ROWS, n_real=n_real, n_fill=n_fill),
        scratch_shapes=[pltpu.VMEM((tm, d), BF16)],
        grid=(b, n_real + n_fill),
        in_specs=[
            blk(d, 0),
            pl.BlockSpec((1, 1, 6 * d), lambda bi, j: (bi if mod_row is None else mod_row, 0, 0)),
            _const_spec((1, d)), _const_spec((1, d)),
            *[blk(BRANCH_W, off) for off in offsets],
            _const_spec(wg.shape), _const_spec(bg.shape), _const_spec(wb.shape), _const_spec(wo.shape),
            _const_spec(rw_t.shape), _const_spec(rb.shape),
        ] + extra_specs,
        out_specs=[
            out_blk(d),
            out_blk(d // 2),
            pl.BlockSpec((1, ROUTE_ROWS, tm), lambda bi, j: (bi, 0, out_off // tm + j)),
        ],
        out_shape=out_shape,
        input_output_aliases={} if prev is None else {n_in + i: i for i in range(3)},
        compiler_params=_cparams(("parallel", "arbitrary"), 56),
        name="merge",
    )(x, mod, n1, n2, *branches, wg, bg, wb, wo, rw_t, rb, *(() if prev is None else prev))


def _rank_kernel(cls_ref, rank_ref, cnt_ref, cnt_s, *, tr):
    @pl.when(pl.program_id(0) == 0)
    def _():
        cnt_s[...] = jnp.zeros_like(cnt_s)

    cls = cls_ref[0]
    cid = lax.broadcasted_iota(jnp.int32, (CLASS_ROWS, tr), 0).astype(F32)
    onehot = cid == cls
    before = lax.broadcasted_iota(jnp.int32, (tr, tr), 0) < lax.broadcasted_iota(jnp.int32, (tr, tr), 1)
    prefix = jnp.dot(jnp.where(onehot, 1.0, 0.0).astype(BF16), jnp.where(before, 1.0, 0.0).astype(BF16),
                     preferred_element_type=F32)
    carry = cnt_s[...][:, 0:1]
    rank_ref[0] = jnp.sum(jnp.where(onehot, prefix + carry, 0.0), axis=0, keepdims=True)
    cnt_s[...] += jnp.sum(jnp.where(onehot, 1.0, 0.0), axis=1, keepdims=True)
    cnt_ref[...] = cnt_s[...]


def _rank(cls_flat, tr=512):
    n = cls_flat.shape[0]
    tr = math.gcd(n, tr)
    cls3 = cls_flat.reshape(n // tr, 1, tr)
    rank, cnt = pl.pallas_call(
        functools.partial(_rank_kernel, tr=tr),
        grid=(n // tr,),
        in_specs=[pl.BlockSpec((1, 1, tr), lambda i: (i, 0, 0))],
        out_specs=[pl.BlockSpec((1, 1, tr), lambda i: (i, 0, 0)), _const_spec((CLASS_ROWS, LANES))],
        out_shape=[jax.ShapeDtypeStruct((n // tr, 1, tr), F32), jax.ShapeDtypeStruct((CLASS_ROWS, LANES), F32)],
        scratch_shapes=[pltpu.VMEM((CLASS_ROWS, LANES), F32)],
        compiler_params=_cparams(("arbitrary",), 32),
        name="rank",
    )(cls3)
    return rank.reshape(n), cnt[:N_CLASSES, 0]


def _sc_layout(n):
    info = plsc.get_sparse_core_info()
    nw = info.num_cores * info.num_subcores
    per_worker = n // nw
    assert per_worker * nw == n
    chunk = max(c for c in range(8, SC_MAX_CHUNK + 1, 8) if per_worker % c == 0)
    return info.num_cores, nw, per_worker // chunk, chunk


def _sc_scatter_rows(src, wts, pos, n_out):
    n, w = src.shape
    nc, nw, k, c = _sc_layout(n)
    mesh = plsc.VectorSubcoreMesh(core_axis_name="c", subcore_axis_name="s")

    @functools.partial(
        pl.kernel, mesh=mesh,
        out_type=(jax.ShapeDtypeStruct((n_out, w), src.dtype), jax.ShapeDtypeStruct((n_out, wts.shape[1]), wts.dtype)),
        scratch_types=[pltpu.VMEM((k, c), jnp.int32), pltpu.VMEM((c, w), src.dtype),
                       pltpu.VMEM((c, wts.shape[1]), wts.dtype), pltpu.SemaphoreType.DMA],
        name="moe_scatter",
    )
    def scatter(src_hbm, wts_hbm, pos_hbm, out_hbm, wout_hbm, idx_v, rows_v, wrows_v, sem):
        wid = lax.axis_index("s") * nc + lax.axis_index("c")
        pltpu.sync_copy(pos_hbm.at[wid], idx_v)

        @pl.loop(0, k)
        def _(j):
            off = pl.multiple_of(wid * (k * c) + j * c, 8)
            pltpu.sync_copy(src_hbm.at[pl.ds(off, c)], rows_v)
            pltpu.sync_copy(wts_hbm.at[pl.ds(off, c)], wrows_v)
            pltpu.async_copy(rows_v, out_hbm.at[idx_v.at[j]], sem).wait()
            pltpu.async_copy(wrows_v, wout_hbm.at[idx_v.at[j]], sem).wait()

    return scatter(src, wts, pos.reshape(nw, k, c))


def _sc_gather_rows(src, pos):
    n = pos.shape[0]
    w = src.shape[1]
    nc, nw, k, c = _sc_layout(n)
    mesh = plsc.VectorSubcoreMesh(core_axis_name="c", subcore_axis_name="s")

    @functools.partial(
        pl.kernel, mesh=mesh,
        out_type=jax.ShapeDtypeStruct((n, w), src.dtype),
        scratch_types=[pltpu.VMEM((k, c), jnp.int32), pltpu.VMEM((c, w), src.dtype), pltpu.SemaphoreType.DMA],
        name="moe_gather",
    )
    def gather(src_hbm, pos_hbm, out_hbm, idx_v, rows_v, sem):
        wid = lax.axis_index("s") * nc + lax.axis_index("c")
        pltpu.sync_copy(pos_hbm.at[wid], idx_v)

        @pl.loop(0, k)
        def _(j):
            off = pl.multiple_of(wid * (k * c) + j * c, 8)
            pltpu.async_copy(src_hbm.at[idx_v.at[j]], rows_v, sem).wait()
            pltpu.sync_copy(rows_v, out_hbm.at[pl.ds(off, c)])

    return gather(src, pos.reshape(nw, k, c))


def _gmm_kernel(lo_ref, hi_ref, new_ref, nact_ref, h_ref, wt_ref,
                w1a_ref, w1b_ref, w3a_ref, w3b_ref, w2a_ref, w2b_ref, o_ref, w13_s, w2_s):
    t = pl.program_id(0)

    @pl.when(new_ref[t] == 1)
    def _():
        for i, (w1_ref, w3_ref, w2_ref) in enumerate(((w1a_ref, w3a_ref, w2a_ref), (w1b_ref, w3b_ref, w2b_ref))):
            w13_s[i, :, :EXPERT_FF] = w1_ref[0, 0].astype(BF16)
            w13_s[i, :, EXPERT_FF:] = w3_ref[0, 0].astype(BF16)
            w2_s[i] = w2_ref[0, 0].astype(BF16)

    @pl.when(t < nact_ref[0])
    def _():
        x = _unpack_bf16_pairs(h_ref[...]).astype(BF16)
        wts = wt_ref[...]

        def expert(i):
            ab = jnp.dot(x, w13_s[i], preferred_element_type=F32)
            a, gate = ab[:, :EXPERT_FF], ab[:, EXPERT_FF:]
            hid = (a * jax.nn.sigmoid(a)) * gate
            return jnp.dot(hid.astype(BF16), w2_s[i], preferred_element_type=F32)

        y = wts[:, 0:1] * expert(0) + wts[:, 1:2] * expert(1)
        o_ref[...] = _pack_bf16_pairs(y)


def _gmm(tile_lo, tile_hi, tile_new, n_act, hs, ws, w1, w3, w2, layer, tm):
    n_pad, half = hs.shape
    d = 2 * half
    row = lambda t, lo, hi, new, na: (jnp.minimum(t, na[0] - 1), 0)
    e_lo = lambda t, lo, hi, new, na: (layer, lo[jnp.minimum(t, na[0] - 1)], 0, 0)
    e_hi = lambda t, lo, hi, new, na: (layer, hi[jnp.minimum(t, na[0] - 1)], 0, 0)
    up = lambda e: pl.BlockSpec((1, 1, d, EXPERT_FF), e)
    down = lambda e: pl.BlockSpec((1, 1, EXPERT_FF, d), e)
    return pl.pallas_call(
        _gmm_kernel,
        grid_spec=pltpu.PrefetchScalarGridSpec(
            num_scalar_prefetch=4,
            grid=(n_pad // tm,),
            in_specs=[
                pl.BlockSpec((tm, half), row),
                pl.BlockSpec((tm, ws.shape[1]), row),
                up(e_lo), up(e_hi), up(e_lo), up(e_hi), down(e_lo), down(e_hi),
            ],
            out_specs=pl.BlockSpec((tm, half), row),
            scratch_shapes=[pltpu.VMEM((2, d, 2 * EXPERT_FF), BF16), pltpu.VMEM((2, EXPERT_FF, d), BF16)],
        ),
        out_shape=jax.ShapeDtypeStruct((n_pad, half), jnp.int32),
        compiler_params=_cparams(("arbitrary",), 56),
        name="moe_gmm",
    )(tile_lo, tile_hi, tile_new, n_act, hs, ws, w1, w1, w3, w3, w2, w2)


def _moe_routed(h2p, route, w1, w3, w2, layer, tm=256):
    b, s, half = h2p.shape
    n = b * s
    n_pad = n + N_CLASSES * tm
    cls = route[:, 0, :].reshape(n)
    rank, counts = _rank(cls)
    counts = counts.astype(jnp.int32)
    padded = (counts + tm - 1) // tm * tm
    ends = jnp.cumsum(padded)
    pos = jnp.take(ends - padded, cls.astype(jnp.int32)) + rank.astype(jnp.int32)
    n_act = (ends[-1] // tm).reshape(1)
    tile_row = jnp.arange(n_pad // tm, dtype=jnp.int32) * tm
    tile_cls = jnp.minimum(jnp.sum(tile_row[:, None] >= ends[None, :], axis=1), N_CLASSES - 1)
    pair_lo, pair_hi = (jnp.asarray(a, jnp.int32) for a in _class_experts())
    wts = jnp.concatenate([route[:, 1, :].reshape(n, 1), route[:, 2, :].reshape(n, 1),
                           jnp.zeros((n, SC_WEIGHT_COLS - 2), F32)], axis=1)
    hs, ws = _sc_scatter_rows(h2p.reshape(n, half), wts, pos, n_pad)
    prev_cls = jnp.concatenate([jnp.full((1,), -1, tile_cls.dtype), tile_cls[:-1]])
    tile_new = ((tile_cls != prev_cls) & (tile_row < ends[-1])).astype(jnp.int32)
    ys = _gmm(jnp.take(pair_lo, tile_cls), jnp.take(pair_hi, tile_cls), tile_new, n_act, hs, ws,
              w1, w3, w2, layer, tm)
    return _sc_gather_rows(ys, pos).reshape(b, s, half)


def _class_experts():
    lo, hi = [], []
    for g in range(N_EXPERTS // EXPERTS_PER_GROUP):
        for i in range(EXPERTS_PER_GROUP):
            for j in range(i + 1, EXPERTS_PER_GROUP):
                lo.append(EXPERTS_PER_GROUP * g + i)
                hi.append(EXPERTS_PER_GROUP * g + j)
    return np.array(lo), np.array(hi)


def _residual_kernel(x_ref, y_ref, mod_ref, g_ref, *out_refs, final, n_lat):
    g2 = _mod_slices(mod_ref[0])[5]
    x = x_ref[0] + g2 * _unpack_bf16_pairs(y_ref[0])
    if final:
        out_refs[0][0] = x * lax.rsqrt(jnp.mean(x * x, axis=-1, keepdims=True) + EPS) * g_ref[...]
        return
    ol_ref, oc_ref = out_refs
    is_ctx = pl.program_id(1) >= n_lat

    @pl.when(is_ctx)
    def _():
        oc_ref[0] = x

    @pl.when(jnp.logical_not(is_ctx))
    def _():
        ol_ref[0] = x


def _residual(x1, yp, mod, gain, t_lat, final, tm=256):
    b, s, d = x1.shape
    n_lat = t_lat // tm
    tok = lambda bi, j: (bi, j, 0)
    if final:
        out_specs = pl.BlockSpec((1, tm, d), tok)
        out_shape = jax.ShapeDtypeStruct((b, s, d), F32)
    else:
        out_specs = _stream_specs(tm, d, n_lat)
        out_shape = [jax.ShapeDtypeStruct((b, t_lat, d), F32), jax.ShapeDtypeStruct((b, s - t_lat, d), F32)]
    return pl.pallas_call(
        functools.partial(_residual_kernel, final=final, n_lat=n_lat),
        grid=(b, s // tm),
        in_specs=[
            pl.BlockSpec((1, tm, d), tok),
            pl.BlockSpec((1, tm, d // 2), tok),
            pl.BlockSpec((1, 1, 6 * d), lambda bi, j: (jnp.where(j >= n_lat, b, bi), 0, 0)),
            _const_spec((1, d)),
        ],
        out_specs=out_specs,
        out_shape=out_shape,
        compiler_params=_cparams(("arbitrary", "arbitrary"), 32),
        name="residual",
    )(x1, yp, mod, gain)


def _rope_tables(t_lat, n_ctx):
    rows = t_lat // GRID_W
    row = jnp.repeat(jnp.arange(rows, dtype=F32), GRID_W)
    col = jnp.tile(jnp.arange(GRID_W, dtype=F32), rows)
    inv_freq = ROPE_THETA ** (-jnp.arange(0, ROPE_AXIS_DIM, 2, dtype=F32) / ROPE_AXIS_DIM)
    ang = jnp.stack([row[:, None] * inv_freq, col[:, None] * inv_freq], axis=1)
    cos, sin = jnp.cos(ang), jnp.sin(ang)
    zero = jnp.zeros_like(sin)
    cos_h = jnp.concatenate([cos, cos], axis=-1).reshape(t_lat, HEAD_DIM)
    sa_h = jnp.concatenate([-sin, zero], axis=-1).reshape(t_lat, HEAD_DIM)
    sb_h = jnp.concatenate([zero, sin], axis=-1).reshape(t_lat, HEAD_DIM)
    reps = LANES // HEAD_DIM
    pad = lambda tbl, fill: jnp.concatenate(
        [jnp.tile(tbl, (1, reps)), jnp.full((n_ctx, LANES), fill, F32)], axis=0)
    return pad(cos_h, 1.0), pad(sa_h, 0.0), pad(sb_h, 0.0)


def _permute_heads(w, axis):
    shp = w.shape
    w = w.reshape(shp[:axis] + (N_Q_HEADS, HEAD_DIM) + shp[axis + 1:])
    w = jnp.take(w, jnp.array(HEAD_PERM), axis=axis)
    return w.reshape(shp)


def _square_factor(n):
    r = int(round(math.sqrt(n)))
    assert r * r == n, "sequence lengths must be perfect squares for the two-stage DFT"
    return r


def kernel(x, c, ctx, c_ctx, w_ada, b_ada, norm1, norm2, w_in, q_gain, k_gain, sink, pool_w, pool_scale,
           w_branch, w_gate, b_gate, w_out, router_w, router_bias, w1, w3, w2, norm_f):
    b, t_lat, d = x.shape
    n_ctx = ctx.shape[1]
    s = t_lat + n_ctx
    depth = w_ada.shape[0]
    assert d == D_MODEL and b < MOD_ROWS and t_lat % 256 == 0 and n_ctx % 256 == 0 and t_lat % n_ctx == 0

    xl, xc = x, ctx
    cc = jnp.zeros((MOD_ROWS, d), F32).at[:b].set(c).at[b].set(c_ctx)
    mod_all = _ada(cc, w_ada, b_ada).reshape(depth, MOD_ROWS, 1, 6 * d)

    cos, sa, sb = _rope_tables(t_lat, n_ctx)
    seg = jnp.asarray(np.kron(np.eye(N_Q_HEADS), np.full((HEAD_DIM, HEAD_DIM), 1.0 / HEAD_DIM)), BF16)
    cs = jnp.asarray(_channel_dft_table()).astype(BF16)
    f_lat = [jnp.asarray(a).astype(BF16) for a in _fourier_tables(*(_square_factor(t_lat),) * 2)]
    f_ctx = [jnp.asarray(a).astype(BF16) for a in _fourier_tables(*(_square_factor(n_ctx),) * 2)]
    wbias = jnp.asarray(_window_bias(n_ctx))
    rw_hi = router_w.T.astype(BF16)
    rw_t = jnp.concatenate([rw_hi, (router_w.T - rw_hi.astype(F32)).astype(BF16)], axis=0)
    rb = router_bias.reshape(N_EXPERTS, 1)

    for l in range(depth):
        need_ctx = l < depth - 1
        s_out = s if need_ctx else t_lat
        cols = jnp.split(w_in[l], np.cumsum((512, 512, 512, 128, 128, 512, 128))[:], axis=1)
        f_w, p_w, qb_w, kb_w, vb_w, qw_w, kw_w, vw_w = cols
        w_in_l = jnp.concatenate([f_w, p_w, _permute_heads(qb_w, 1), _permute_heads(qw_w, 1),
                                  kb_w, vb_w, kw_w, vw_w], axis=1).astype(BF16)
        wb_l = jnp.stack([w_branch[l, 0], _permute_heads(w_branch[l, 1], 0),
                          _permute_heads(w_branch[l, 2], 0), w_branch[l, 3]]).astype(BF16)
        mod = mod_all[l]
        n1 = norm1[l].reshape(1, d)
        n2 = norm2[l].reshape(1, d)
        qg = jnp.tile(q_gain[l], N_Q_HEADS).reshape(1, BRANCH_W)
        kg = jnp.tile(k_gain[l], LANES // HEAD_DIM).reshape(1, LANES)

        zr, zi, p_in, qb, qw, kvb, kvw = _inproj(xl, xc, mod, n1, w_in_l, qg, kg, seg, cs, cos, sa, sb)

        out_b = _gattn(qb, kvb, 0, t_lat, 0, s, GATTN_KEY_TILES)
        out_bc = _gattn(qb, kvb, t_lat, n_ctx, t_lat, n_ctx, 1) if need_ctx else out_b
        out_c = _wattn(jnp.take(sink[l], jnp.array(HEAD_PERM)) * LOG2E, wbias, qw, kvw, s_out, t_lat)
        out_a = _fourier(zr, zi, f_lat[0], f_lat[1], t_lat, 0)
        out_ac = _fourier(zr, zi, f_ctx[0], f_ctx[1], n_ctx, t_lat // n_ctx) if need_ctx else out_a
        out_d = _pool(p_in, pool_w[l].astype(BF16), pool_scale[l].reshape(1, BRANCH_W), s_out, t_lat)

        weights = (w_gate[l].astype(BF16), b_gate[l].reshape(4, 1, d), wb_l, w_out[l].astype(BF16), rw_t, rb)
        merged = _merge(xl, mod, None, n1, n2, (out_a, out_b, out_c, out_d), (0, 0, 0, 0), weights,
                        s_out, 0, t_lat, 2 * MERGE_CHAIN_ROWS)
        if need_ctx:
            merged = _merge(xc, mod, b, n1, n2, (out_ac, out_bc, out_c, out_d), (0, 0, t_lat, t_lat), weights,
                            s_out, t_lat, n_ctx, MERGE_CHAIN_ROWS, prev=merged)
        x1, h2p, route = merged
        yp = _moe_routed(h2p, route, w1, w3, w2, l)
        if l == depth - 1:
            return _residual(x1, yp, mod, norm_f.reshape(1, d), t_lat, final=True)
        xl, xc = _residual(x1, yp, mod, norm_f.reshape(1, d), t_lat, final=False)
```

```python
import functools
import math

import numpy as np
import jax
import jax.numpy as jnp
from jax import lax
from jax.experimental import pallas as pl
from jax.experimental.pallas import tpu as pltpu
from jax.experimental.pallas import tpu_sc as plsc

F32 = jnp.float32
BF16 = jnp.bfloat16

D_MODEL = 1024
HEAD_DIM = 64
N_Q_HEADS = 8
N_KV_HEADS = 2
GRID_W = 64
ROPE_THETA = 10000.0
ROPE_AXIS_DIM = HEAD_DIM // 2
QBLK = 128
WINDOW = 128
BRANCH_W = 512
GROUP_W = 128
POOL_WINDOWS = (2, 4, 8, 16)
N_EXPERTS = 16
EXPERTS_PER_GROUP = 4
EXPERT_FF = 512
EPS = 1e-6
MOD_ROWS = 16
NEG_BIG = -1e30
LOG2E = math.log2(math.e)
LANES = 128
POOL_HALO = 16
PAIRS_PER_GROUP = 6
N_CLASSES = 24
CLASS_ROWS = 32
ROUTE_ROWS = 8
SC_MAX_CHUNK = 128
SC_WEIGHT_COLS = 128

HEAD_PERM = (0, 4, 1, 5, 2, 6, 3, 7)


def _cparams(sem, vmem_mb):
    return pltpu.CompilerParams(dimension_semantics=sem, vmem_limit_bytes=vmem_mb * 1024 * 1024)


def _const_spec(shape):
    nd = len(shape)
    return pl.BlockSpec(shape, lambda *_: (0,) * nd)


def _ada_kernel(c_ref, w_ref, b_ref, o_ref):
    c = c_ref[...]
    s = c * jax.nn.sigmoid(c)
    o_ref[0] = jnp.dot(s.astype(BF16), w_ref[0].astype(BF16), preferred_element_type=F32) + b_ref[0]


def _ada(cc, w_ada, b_ada):
    depth, d, n = w_ada.shape
    tn = 1536
    return pl.pallas_call(
        _ada_kernel,
        grid=(depth, n // tn),
        in_specs=[
            pl.BlockSpec((MOD_ROWS, d), lambda l, j: (0, 0)),
            pl.BlockSpec((1, d, tn), lambda l, j: (l, 0, j)),
            pl.BlockSpec((1, 1, tn), lambda l, j: (l, 0, j)),
        ],
        out_specs=pl.BlockSpec((1, MOD_ROWS, tn), lambda l, j: (l, 0, j)),
        out_shape=jax.ShapeDtypeStruct((depth, MOD_ROWS, n), F32),
        compiler_params=_cparams(("arbitrary", "arbitrary"), 40),
        name="ada",
    )(cc, w_ada, b_ada.reshape(depth, 1, n))


def _norm_mod(x, gain, shift, scale):
    ms = jnp.mean(x * x, axis=-1, keepdims=True)
    return (x * lax.rsqrt(ms + EPS) * gain) * (1.0 + scale) + shift


def _mod_slices(m):
    d = D_MODEL
    return [m[:, i * d:(i + 1) * d] for i in range(6)]


def _head_norm(z, seg, gain):
    ms = jnp.dot((z * z).astype(BF16), seg, preferred_element_type=F32)
    return z * lax.rsqrt(ms + EPS) * gain


def _rope(z, cos, sin_a, sin_b):
    outs = []
    for c in range(z.shape[1] // LANES):
        zc = z[:, c * LANES:(c + 1) * LANES]
        nxt = pltpu.roll(zc, LANES - ROPE_AXIS_DIM // 2, 1)
        prv = pltpu.roll(zc, ROPE_AXIS_DIM // 2, 1)
        outs.append(zc * cos + nxt * sin_a + prv * sin_b)
    return outs[0] if len(outs) == 1 else jnp.concatenate(outs, axis=-1)


def _stream_specs(tm, d, n_lat):
    return [pl.BlockSpec((1, tm, d), lambda bi, j: (bi, jnp.minimum(j, n_lat - 1), 0)),
            pl.BlockSpec((1, tm, d), lambda bi, j: (bi, jnp.maximum(j - n_lat, 0), 0))]


def _pending_residual(x, y_ref, modp_ref, rows=slice(None)):
    return x + _mod_slices(modp_ref[0])[5] * _unpack_bf16_pairs(y_ref[0, rows, :])


def _inproj_kernel(*refs, n_lat, pending):
    if pending:
        x_ref, y_ref, modp_ref = refs[:3]
        x = _pending_residual(x_ref[0], y_ref, modp_ref)
    else:
        xl_ref, xc_ref = refs[:2]
        x = jnp.where(pl.program_id(1) >= n_lat, xc_ref[0], xl_ref[0])
    (mod_ref, n1_ref, w_ref, qg_ref, kg_ref, seg_ref, cs_ref, cos_ref, sa_ref, sb_ref,
     zr_ref, zi_ref, p_ref, qb_ref, qw_ref, kvb_ref, kvw_ref) = refs[3 if pending else 2:]
    sh1, sc1 = _mod_slices(mod_ref[0])[:2]
    h = _norm_mod(x, n1_ref[...], sh1, sc1)
    u = jnp.dot(h.astype(BF16), w_ref[...], preferred_element_type=F32)
    cos, sa, sb = cos_ref[...], sa_ref[...], sb_ref[...]
    w = BRANCH_W
    f_in = u[:, 0:w].astype(BF16)
    zr, zi = [], []
    for g in range(w // GROUP_W):
        z = jnp.dot(f_in[:, g * GROUP_W:(g + 1) * GROUP_W], cs_ref[...], preferred_element_type=F32)
        zr.append(z[:, :GROUP_W])
        zi.append(z[:, GROUP_W:])
    zr_ref[0] = jnp.concatenate(zr, axis=-1).astype(BF16)
    zi_ref[0] = jnp.concatenate(zi, axis=-1).astype(BF16)
    p_ref[0] = u[:, w:2 * w].astype(BF16)
    seg = seg_ref[...]
    qb = _rope(_head_norm(u[:, 2 * w:3 * w], seg, qg_ref[...]), cos, sa, sb)
    qb_ref[0] = (qb * (HEAD_DIM ** -0.5 * LOG2E)).astype(BF16)
    qw = _rope(u[:, 3 * w:4 * w], cos, sa, sb)
    qw_ref[0] = (qw * (HEAD_DIM ** -0.5 * LOG2E)).astype(BF16)
    o = 4 * w
    kb = _rope(_head_norm(u[:, o:o + LANES], seg[:LANES, :LANES], kg_ref[...]), cos, sa, sb)
    vb = u[:, o + LANES:o + 2 * LANES]
    kw = _rope(u[:, o + 2 * LANES:o + 3 * LANES], cos, sa, sb)
    vw = u[:, o + 3 * LANES:o + 4 * LANES]
    kvb_ref[0] = jnp.concatenate([kb, vb, jnp.ones_like(vb)], axis=-1).astype(BF16)
    kvw_ref[0] = jnp.concatenate([kw, vw, jnp.ones_like(vw)], axis=-1).astype(BF16)


def _inproj(tokens, t_lat, mod, n1, w_in, qg, kg, seg, cs, cos, sa, sb, tm=256):
    pending = len(tokens) == 3
    b, _, d = tokens[0].shape
    s = tokens[0].shape[1] if pending else t_lat + tokens[1].shape[1]
    nw = w_in.shape[1]
    n_lat = t_lat // tm
    tok = lambda bi, j: (bi, j, 0)
    tab = lambda bi, j: (j, 0)
    mod_spec = pl.BlockSpec((1, 1, 6 * d), lambda bi, j: (jnp.where(j >= n_lat, b, bi), 0, 0))
    if pending:
        token_specs = [pl.BlockSpec((1, tm, d), tok), pl.BlockSpec((1, tm, d // 2), tok), mod_spec]
    else:
        token_specs = _stream_specs(tm, d, n_lat)
    widths = (BRANCH_W,) * 5 + (3 * LANES, 3 * LANES)
    return pl.pallas_call(
        functools.partial(_inproj_kernel, n_lat=n_lat, pending=pending),
        grid=(b, s // tm),
        in_specs=token_specs + [
            mod_spec,
            _const_spec((1, d)),
            _const_spec((d, nw)),
            _const_spec((1, BRANCH_W)),
            _const_spec((1, LANES)),
            _const_spec((BRANCH_W, BRANCH_W)),
            _const_spec((GROUP_W, 2 * GROUP_W)),
            pl.BlockSpec((tm, LANES), tab),
            pl.BlockSpec((tm, LANES), tab),
            pl.BlockSpec((tm, LANES), tab),
        ],
        out_specs=[pl.BlockSpec((1, tm, wd), tok) for wd in widths],
        out_shape=[jax.ShapeDtypeStruct((b, s, wd), BF16) for wd in widths],
        compiler_params=_cparams(("parallel", "arbitrary"), 48),
        name="inproj",
    )(*tokens, mod, n1, w_in, qg, kg, seg, cs, cos, sa, sb)


def _split_heads(qc, lane):
    zero = jnp.zeros_like(qc)
    return jnp.concatenate([jnp.where(lane < HEAD_DIM, qc, zero),
                            jnp.where(lane >= HEAD_DIM, qc, zero)], axis=0)


def _gattn_kernel(q_ref, kv_ref, o_ref, q2_s, m_s, acc_s, *, n_tiles, sub):
    tq = q_ref.shape[1]
    tk = kv_ref.shape[1] // n_tiles
    lane = lax.broadcasted_iota(jnp.int32, (1, LANES), 1)
    nt = (((1,), (1,)), ((), ()))
    n_chunks = BRANCH_W // LANES
    for c in range(n_chunks):
        q2_s[2 * c * tq:(2 * c + 2) * tq, :] = _split_heads(q_ref[0, :, c * LANES:(c + 1) * LANES], lane)

    def tile(k, v, first):
        for r in range(2 * n_chunks * tq // sub):
            rows = slice(r * sub, (r + 1) * sub)
            s = lax.dot_general(q2_s[rows, :], k, nt, preferred_element_type=F32)
            rm = jnp.max(s, axis=-1, keepdims=True)
            if first:
                m_new = rm
                acc_s[rows, :] = jnp.dot(jnp.exp2(s - m_new).astype(BF16), v, preferred_element_type=F32)
            else:
                m_old = m_s[rows, :]
                m_new = jnp.maximum(m_old, rm)
                alpha = jnp.exp2(m_old - m_new)
                pv = jnp.dot(jnp.exp2(s - m_new).astype(BF16), v, preferred_element_type=F32)
                acc_s[rows, :] = alpha * acc_s[rows, :] + pv
            m_s[rows, :] = m_new

    for t in range(n_tiles):
        keys = slice(t * tk, (t + 1) * tk)
        tile(kv_ref[0, keys, 0:LANES], kv_ref[0, keys, LANES:3 * LANES], t == 0)
    for c in range(n_chunks):
        lo = acc_s[2 * c * tq:(2 * c + 1) * tq, :]
        hi = acc_s[(2 * c + 1) * tq:(2 * c + 2) * tq, :]
        o = jnp.where(lane < HEAD_DIM, lo[:, :LANES] / lo[:, LANES:], hi[:, :LANES] / hi[:, LANES:])
        o_ref[0, :, c * LANES:(c + 1) * LANES] = o.astype(BF16)


def _gattn(qb, kv, q_start, q_len, k_start, k_len, n_tiles, tq=256, sub=128):
    b = qb.shape[0]
    assert q_start % tq == 0 and q_len % tq == 0 and k_start % k_len == 0 and k_len % (n_tiles * LANES) == 0
    rows = 2 * tq * (BRANCH_W // LANES)
    return pl.pallas_call(
        functools.partial(_gattn_kernel, n_tiles=n_tiles, sub=sub),
        grid=(b, q_len // tq),
        in_specs=[
            pl.BlockSpec((1, tq, BRANCH_W), lambda bi, j: (bi, q_start // tq + j, 0)),
            pl.BlockSpec((1, k_len, 3 * LANES), lambda bi, j: (bi, k_start // k_len, 0)),
        ],
        out_specs=pl.BlockSpec((1, tq, BRANCH_W), lambda bi, j: (bi, j, 0)),
        out_shape=jax.ShapeDtypeStruct((b, q_len, BRANCH_W), BF16),
        scratch_shapes=[pltpu.VMEM((rows, LANES), BF16), pltpu.VMEM((rows, 1), F32),
                        pltpu.VMEM((rows, 2 * LANES), F32)],
        compiler_params=_cparams(("parallel", "arbitrary"), 48),
        name="gattn",
    )(qb, kv)


WATTN_QBLOCKS = 2
GATTN_KEY_TILES = 2
MERGE_CHAIN_ROWS = 256


def _window_bias(n_ctx):
    tq = WATTN_QBLOCKS * QBLK
    qi = np.arange(tq)[:, None]
    kj = np.arange(tq + 2 * QBLK)[None, :]
    band = np.abs(kj - WINDOW - qi) <= WINDOW
    blk = kj // QBLK
    variants = [band & (blk != 0), band, band & (blk != WATTN_QBLOCKS + 1), np.zeros_like(band)]
    out = [np.concatenate([np.ones((tq, n_ctx), bool), v], axis=1) for v in variants]
    return np.where(np.stack(out), 0.0, NEG_BIG).astype(np.float32)


def _wattn_kernel(sink_ref, bias_ref, q_ref, *refs, sub):
    o_ref, q2_s, kv_s, acc_s = refs[-4:]
    tq = q_ref.shape[1]
    off = 0
    for blk in refs[:-4]:
        kv_s[off:off + blk.shape[1], :] = blk[0]
        off += blk.shape[1]
    lane = lax.broadcasted_iota(jnp.int32, (1, LANES), 1)
    nt = (((1,), (1,)), ((), ()))
    n_chunks = BRANCH_W // LANES
    for c in range(n_chunks):
        q2_s[2 * c * tq:(2 * c + 2) * tq, :] = _split_heads(q_ref[0, :, c * LANES:(c + 1) * LANES], lane)
    k, v = kv_s[:, 0:LANES], kv_s[:, LANES:3 * LANES]
    for r in range(2 * n_chunks * tq // sub):
        rows = slice(r * sub, (r + 1) * sub)
        q_off = (r * sub) % tq
        sk = sink_ref[(r * sub) // tq]
        s = lax.dot_general(q2_s[rows, :], k, nt, preferred_element_type=F32) + bias_ref[0, q_off:q_off + sub, :]
        m = jnp.maximum(jnp.max(s, axis=-1, keepdims=True), sk)
        pv = jnp.dot(jnp.exp2(s - m).astype(BF16), v, preferred_element_type=F32)
        acc_s[rows, :LANES] = pv[:, :LANES]
        acc_s[rows, LANES:] = pv[:, LANES:] + jnp.exp2(sk - m)
    for c in range(n_chunks):
        lo = acc_s[2 * c * tq:(2 * c + 1) * tq, :]
        hi = acc_s[(2 * c + 1) * tq:(2 * c + 2) * tq, :]
        o = jnp.where(lane < HEAD_DIM, lo[:, :LANES] / lo[:, LANES:], hi[:, :LANES] / hi[:, LANES:])
        o_ref[0, :, c * LANES:(c + 1) * LANES] = o.astype(BF16)


def _wattn(sink, bias, qw, kv, s_out, t_lat):
    b, s, _ = qw.shape
    n_ctx = s - t_lat
    nq = WATTN_QBLOCKS
    tq = nq * QBLK
    assert n_ctx % tq == 0 and t_lat // tq >= 2
    last = s // QBLK - 1
    n_lat = t_lat // tq
    variant = lambda j: jnp.where(j >= n_lat, 3, jnp.where(j == 0, 0, jnp.where(j == n_lat - 1, 2, 1)))
    key_block = lambda off: pl.BlockSpec(
        (1, QBLK, 3 * LANES), lambda bi, j: (bi, jnp.clip(j * nq + off, 0, last), 0))
    rows = 2 * tq * (BRANCH_W // LANES)
    return pl.pallas_call(
        functools.partial(_wattn_kernel, sub=128),
        scratch_shapes=[pltpu.VMEM((rows, LANES), BF16), pltpu.VMEM((bias.shape[2], 3 * LANES), BF16),
                        pltpu.VMEM((rows, 2 * LANES), F32)],
        grid=(b, s_out // tq),
        in_specs=[
            pl.BlockSpec(memory_space=pltpu.SMEM),
            pl.BlockSpec((1,) + bias.shape[1:], lambda bi, j: (variant(j), 0, 0)),
            pl.BlockSpec((1, tq, BRANCH_W), lambda bi, j: (bi, j, 0)),
            pl.BlockSpec((1, n_ctx, 3 * LANES), lambda bi, j: (bi, t_lat // n_ctx, 0)),
        ] + [key_block(off) for off in range(-1, nq + 1)],
        out_specs=pl.BlockSpec((1, tq, BRANCH_W), lambda bi, j: (bi, j, 0)),
        out_shape=jax.ShapeDtypeStruct((b, s_out, BRANCH_W), BF16),
        compiler_params=_cparams(("parallel", "arbitrary"), 32),
        name="wattn",
    )(sink, bias, qw, *([kv] * (nq + 3)))


def _fourier_tables(n1, n2):
    t = n1 * n2
    k2 = np.arange(n2)[None, :, None]
    t2 = np.arange(n2)[None, None, :]
    t1 = np.arange(n1)[:, None, None]
    theta = 2.0 * np.pi * ((k2 * t2 * n1 + k2 * t1) % t) / t
    er, ei = np.cos(theta) / math.sqrt(n2), -np.sin(theta) / math.sqrt(n2)
    e = np.concatenate([np.concatenate([er, -ei], axis=2), np.concatenate([ei, er], axis=2)], axis=1)
    k1 = np.arange(n1)[:, None]
    phi = 2.0 * np.pi * ((k1 * np.arange(n1)[None, :]) % n1) / n1
    dcat = np.concatenate([np.cos(phi), np.sin(phi)], axis=1) / math.sqrt(n1)
    return e.astype(np.float32), dcat.astype(np.float32)


def _channel_dft_table():
    c = np.arange(GROUP_W)
    ang = 2.0 * np.pi * ((c[:, None] * c[None, :]) % GROUP_W) / GROUP_W
    return (np.concatenate([np.cos(ang), -np.sin(ang)], axis=1) / math.sqrt(GROUP_W)).astype(np.float32)


def _fourier_kernel(zr_ref, zi_ref, e_ref, d_ref, o_ref, xr_s, xi_s, yr_s, yi_s, *, n1, n2):
    nc = xr_s.shape[0]
    chunk = lambda c: slice(c * LANES, (c + 1) * LANES)

    def gather(ref, start, size, stride):
        return jnp.concatenate([ref[c, pl.ds(start, size, stride=stride), :] for c in range(nc)], axis=-1)

    for c in range(nc):
        xr_s[c] = zr_ref[0, :, chunk(c)].astype(F32)
        xi_s[c] = zi_ref[0, :, chunk(c)].astype(F32)
    for t1 in range(n1):
        xs = jnp.concatenate([gather(xr_s, t1, n2, n1), gather(xi_s, t1, n2, n1)], axis=0).astype(BF16)
        y = jnp.dot(e_ref[t1], xs, preferred_element_type=F32)
        for c in range(nc):
            yr_s[c, t1 * n2:(t1 + 1) * n2, :] = y[:n2, chunk(c)]
            yi_s[c, t1 * n2:(t1 + 1) * n2, :] = y[n2:, chunk(c)]
    for k2 in range(n2):
        ys = jnp.concatenate([gather(yr_s, k2, n1, n2), gather(yi_s, k2, n1, n2)], axis=0).astype(BF16)
        o = jnp.dot(d_ref[...], ys, preferred_element_type=F32)
        for c in range(nc):
            xr_s[c, pl.ds(k2, n1, stride=n2), :] = o[:, chunk(c)]
    for c in range(nc):
        o_ref[0, :, chunk(c)] = xr_s[c].astype(BF16)


def _fourier(zr, zi, e_tab, d_tab, t_len, row_block, cw=256):
    b = zr.shape[0]
    n1 = d_tab.shape[0]
    n2 = t_len // n1
    zspec = pl.BlockSpec((1, t_len, cw), lambda bi, j: (bi, row_block, j))
    return pl.pallas_call(
        functools.partial(_fourier_kernel, n1=n1, n2=n2),
        grid=(b, BRANCH_W // cw),
        in_specs=[zspec, zspec, _const_spec(e_tab.shape), _const_spec(d_tab.shape)],
        out_specs=pl.BlockSpec((1, t_len, cw), lambda bi, j: (bi, 0, j)),
        out_shape=jax.ShapeDtypeStruct((b, t_len, BRANCH_W), BF16),
        scratch_shapes=[pltpu.VMEM((cw // LANES, t_len, LANES), F32)] * 4,
        compiler_params=_cparams(("parallel", "arbitrary"), 48),
        name="fourier",
    )(zr, zi, e_tab, d_tab)


def _pool_rows(ext, pos, n, w_ref, scale):
    n_ext = ext.shape[0]
    rows = n_ext - 2 * POOL_HALO
    back = lambda v, k: pltpu.roll(v, k, 0)
    fwd = lambda v, k: pltpu.roll(v, n_ext - k, 0)
    outs = []
    for gi, w in enumerate(POOL_WINDOWS):
        e = ext[:, gi * GROUP_W:(gi + 1) * GROUP_W]
        wsum = e + back(e, 1)
        half = 1
        while 2 * half < w:
            wsum = back(wsum, half) + fwd(wsum, half)
            half *= 2
        own = slice(POOL_HALO, POOL_HALO + rows)
        cnt = jnp.minimum(pos + w // 2, n) - jnp.maximum(pos - w // 2, 0)
        pooled = wsum[own] / cnt.astype(F32) - e[own]
        outs.append(jnp.dot(pooled.astype(BF16), w_ref[gi], preferred_element_type=F32))
    return (jnp.concatenate(outs, axis=-1) * scale).astype(BF16)


def _route(logits_t, bias):
    aff = jax.nn.sigmoid(logits_t)
    sel = aff + bias
    neg = -jnp.inf
    firsts, seconds, scores = [], [], []
    for g in range(N_EXPERTS // EXPERTS_PER_GROUP):
        s = [sel[EXPERTS_PER_GROUP * g + k:EXPERTS_PER_GROUP * g + k + 1, :] for k in range(EXPERTS_PER_GROUP)]
        m1 = jnp.maximum(jnp.maximum(s[0], s[1]), jnp.maximum(s[2], s[3]))
        i1 = jnp.where(s[0] == m1, 0, jnp.where(s[1] == m1, 1, jnp.where(s[2] == m1, 2, 3)))
        r = [jnp.where(i1 == k, neg, s[k]) for k in range(EXPERTS_PER_GROUP)]
        m2 = jnp.maximum(jnp.maximum(r[0], r[1]), jnp.maximum(r[2], r[3]))
        i2 = jnp.where(r[0] == m2, 0, jnp.where(r[1] == m2, 1, jnp.where(r[2] == m2, 2, 3)))
        firsts.append(i1 + EXPERTS_PER_GROUP * g)
        seconds.append(i2 + EXPERTS_PER_GROUP * g)
        scores.append(m1 + m2)
    best = jnp.maximum(jnp.maximum(scores[0], scores[1]), jnp.maximum(scores[2], scores[3]))
    pick = lambda v: jnp.where(scores[0] == best, v[0], jnp.where(scores[1] == best, v[1],
                                                                 jnp.where(scores[2] == best, v[2], v[3])))
    e1, e2 = pick(firsts), pick(seconds)
    eidx = lax.broadcasted_iota(jnp.int32, aff.shape, 0)
    a1 = jnp.sum(jnp.where(eidx == e1, aff, 0.0), axis=0, keepdims=True)
    a2 = jnp.sum(jnp.where(eidx == e2, aff, 0.0), axis=0, keepdims=True)
    tot = a1 + a2
    w1, w2 = a1 / tot, a2 / tot
    swap = e1 > e2
    lo = jnp.where(swap, e2, e1) & (EXPERTS_PER_GROUP - 1)
    hi = jnp.where(swap, e1, e2) & (EXPERTS_PER_GROUP - 1)
    pair = jnp.where(lo == 0, 0, jnp.where(lo == 1, 3, 5)) + hi - lo - 1
    cls = (e1 >> 2) * PAIRS_PER_GROUP + pair
    rows = [cls.astype(F32), jnp.where(swap, w2, w1), jnp.where(swap, w1, w2)]
    return jnp.concatenate(rows + [jnp.zeros_like(w1)] * (ROUTE_ROWS - len(rows)), axis=0)


def _pack_bf16_pairs(v):
    w = v.shape[1] // 2
    bits = pltpu.bitcast(v.astype(BF16).astype(F32), jnp.uint32)
    return pltpu.bitcast(bits[:, :w] | (bits[:, w:] >> 16), jnp.int32)


def _unpack_bf16_pairs(p):
    bits = pltpu.bitcast(p, jnp.uint32)
    hi = pltpu.bitcast(bits & jnp.uint32(0xFFFF0000), F32)
    lo = pltpu.bitcast(bits << 16, F32)
    return jnp.concatenate([hi, lo], axis=-1)


def _merge_kernel(*refs, n_real, n_fill, pending, **static):
    n_tok = 3 if pending else 1
    (mod_ref, n1_ref, n2_ref, a_ref, b_ref, c_ref, pc_ref, pp_ref, pn_ref, pw_ref, psc_ref,
     wg_ref, bg_ref, wb_ref, wo_ref, rw_ref, rb_ref) = refs[n_tok:n_tok + 17]
    outs = refs[-4:-1]
    ins = (refs[:n_tok], mod_ref, n1_ref, n2_ref, a_ref, b_ref, c_ref,
           (pc_ref, pp_ref, pn_ref, pw_ref, psc_ref),
           wg_ref, bg_ref, wb_ref, wo_ref, rw_ref, rb_ref)
    if n_fill:
        @pl.when(pl.program_id(1) >= n_real)
        def _():
            for ref in outs:
                ref[...] = jnp.zeros_like(ref)

        pl.when(pl.program_id(1) < n_real)(lambda: _merge_tile(ins, outs, refs[-1], **static))
    else:
        _merge_tile(ins, outs, refs[-1], **static)


def _merge_tile(ins, outs, merged_s, *, nw, sub, p_off, seq_lo, seq_hi):
    (tok_refs, mod_ref, n1_ref, n2_ref, a_ref, b_ref, c_ref, pool_refs,
     wg_ref, bg_ref, wb_ref, wo_ref, rw_ref, rb_ref) = ins
    xo_ref, h2_ref, route_ref = outs
    pc_ref, pp_ref, pn_ref, pw_ref, psc_ref = pool_refs
    x_ref = tok_refs[0]
    tm = x_ref.shape[1]
    sh1, sc1, g1, sh2, sc2, _ = _mod_slices(mod_ref[0])
    row0 = p_off + pl.program_id(1) * tm
    ext = jnp.concatenate([pp_ref[0], pc_ref[0], pn_ref[0]], axis=0).astype(F32)
    gpos = row0 - POOL_HALO + lax.broadcasted_iota(jnp.int32, (tm + 2 * POOL_HALO, 1), 0)
    ext = jnp.where((gpos >= seq_lo) & (gpos < seq_hi), ext, 0.0)
    for r in range(tm // sub):
        rows = slice(r * sub, (r + 1) * sub)
        x = x_ref[0, rows, :]
        if len(tok_refs) == 3:
            x = _pending_residual(x, tok_refs[1], tok_refs[2], rows)
        hb = _norm_mod(x, n1_ref[...], sh1, sc1).astype(BF16)
        pos = row0 - seq_lo + r * sub + lax.broadcasted_iota(jnp.int32, (sub, 1), 0)
        pooled = _pool_rows(ext[r * sub:(r + 1) * sub + 2 * POOL_HALO], pos, seq_hi - seq_lo,
                            pw_ref, psc_ref[...])
        branches = (a_ref[0, rows, :], b_ref[0, rows, :], c_ref[0, rows, :], pooled)
        for n in range(D_MODEL // nw):
            cols = slice(n * nw, (n + 1) * nw)
            merged = None
            for i, br in enumerate(branches):
                gate = jax.nn.sigmoid(
                    jnp.dot(hb, wg_ref[i, :, cols], preferred_element_type=F32) + bg_ref[i, :, cols])
                term = gate * jnp.dot(br, wb_ref[i, :, cols], preferred_element_type=F32)
                merged = term if merged is None else merged + term
            merged_s[rows, cols] = merged.astype(BF16)
        y = jnp.dot(merged_s[rows, :], wo_ref[...], preferred_element_type=F32)
        xn = x + g1 * y
        xo_ref[0, rows, :] = xn
        h2 = _norm_mod(xn, n2_ref[...], sh2, sc2)
        h2_ref[0, rows, :] = _pack_bf16_pairs(h2)
        h_hi = h2.astype(BF16)
        h_lo = (h2 - h_hi.astype(F32)).astype(BF16)
        nt = (((1,), (1,)), ((), ()))
        by_hi = lax.dot_general(rw_ref[...], h_hi, nt, preferred_element_type=F32)
        by_lo = lax.dot_general(rw_ref[:N_EXPERTS, :], h_lo, nt, preferred_element_type=F32)
        logits_t = by_hi[:N_EXPERTS] + by_hi[N_EXPERTS:] + by_lo
        route_ref[0, :, rows] = _route(logits_t, rb_ref[...])


def _merge(tokens, mod, mod_row, n1, n2, branches, offsets, p_in, pool_params, seq, weights, s_out, out_off,
           rows, tm, prev=None):
    pending = len(tokens) == 3
    b, _, d = tokens[0].shape
    wg, bg, wb, wo, rw_t, rb = weights
    pool_w, pool_scale = pool_params
    seq_lo, seq_hi = seq
    n_real = rows // tm
    n_fill = -(-(s_out - out_off - rows) // tm) if prev is None else 0
    step = lambda j: jnp.minimum(j, n_real - 1)
    blk = lambda width, off: pl.BlockSpec((1, tm, width), lambda bi, j: (bi, off // tm + step(j), 0))
    out_blk = lambda width: pl.BlockSpec((1, tm, width), lambda bi, j: (bi, out_off // tm + j, 0))
    assert all(o % tm == 0 for o in offsets) and out_off % tm == 0 and rows % tm == 0 and seq_lo % tm == 0
    hb = tm // POOL_HALO
    last_halo = p_in.shape[1] // POOL_HALO - 1
    halo = lambda shift: pl.BlockSpec(
        (1, POOL_HALO, BRANCH_W),
        lambda bi, j: (bi, jnp.clip((seq_lo // tm + step(j) + shift) * hb - 1 + shift, 0, last_halo), 0))
    out_shape = [
        jax.ShapeDtypeStruct((b, s_out, d), F32),
        jax.ShapeDtypeStruct((b, s_out, d // 2), jnp.int32),
        jax.ShapeDtypeStruct((b, ROUTE_ROWS, s_out), F32),
    ]
    mod_spec = pl.BlockSpec((1, 1, 6 * d), lambda bi, j: (bi if mod_row is None else mod_row, 0, 0))
    tok_off = out_off if pending else 0
    token_specs = [blk(d, tok_off)] + ([blk(d // 2, tok_off), mod_spec] if pending else [])
    n_in = len(tokens) + 17
    extra_specs = [] if prev is None else [pl.BlockSpec(memory_space=pl.ANY)] * 3
    return pl.pallas_call(
        functools.partial(_merge_kernel, nw=512, sub=MERGE_CHAIN_ROWS, n_real=n_real, n_fill=n_fill,
                          pending=pending, p_off=seq_lo, seq_lo=seq_lo, seq_hi=seq_hi),
        scratch_shapes=[pltpu.VMEM((tm, d), BF16)],
        grid=(b, n_real + n_fill),
        in_specs=token_specs + [
            mod_spec,
            _const_spec((1, d)), _const_spec((1, d)),
            *[blk(BRANCH_W, off) for off in offsets],
            blk(BRANCH_W, seq_lo), halo(0), halo(1),
            _const_spec(pool_w.shape), _const_spec((1, BRANCH_W)),
            _const_spec(wg.shape), _const_spec(bg.shape), _const_spec(wb.shape), _const_spec(wo.shape),
            _const_spec(rw_t.shape), _const_spec(rb.shape),
        ] + extra_specs,
        out_specs=[
            out_blk(d),
            out_blk(d // 2),
            pl.BlockSpec((1, ROUTE_ROWS, tm), lambda bi, j: (bi, 0, out_off // tm + j)),
        ],
        out_shape=out_shape,
        input_output_aliases={} if prev is None else {n_in + i: i for i in range(3)},
        compiler_params=_cparams(("parallel", "arbitrary"), 56),
        name="merge",
    )(*tokens, mod, n1, n2, *branches, p_in, p_in, p_in, pool_w, pool_scale,
      wg, bg, wb, wo, rw_t, rb, *(() if prev is None else prev))


def _rank_kernel(cls_ref, rank_ref, cnt_ref, cnt_s, *, tr):
    @pl.when(pl.program_id(0) == 0)
    def _():
        cnt_s[...] = jnp.zeros_like(cnt_s)

    cls = cls_ref[0]
    cid = lax.broadcasted_iota(jnp.int32, (CLASS_ROWS, tr), 0).astype(F32)
    onehot = cid == cls
    before = lax.broadcasted_iota(jnp.int32, (tr, tr), 0) < lax.broadcasted_iota(jnp.int32, (tr, tr), 1)
    prefix = jnp.dot(jnp.where(onehot, 1.0, 0.0).astype(BF16), jnp.where(before, 1.0, 0.0).astype(BF16),
                     preferred_element_type=F32)
    carry = cnt_s[...][:, 0:1]
    rank_ref[0] = jnp.sum(jnp.where(onehot, prefix + carry, 0.0), axis=0, keepdims=True)
    cnt_s[...] += jnp.sum(jnp.where(onehot, 1.0, 0.0), axis=1, keepdims=True)
    cnt_ref[...] = cnt_s[...]


def _rank(cls_flat, tr=512):
    n = cls_flat.shape[0]
    tr = math.gcd(n, tr)
    cls3 = cls_flat.reshape(n // tr, 1, tr)
    rank, cnt = pl.pallas_call(
        functools.partial(_rank_kernel, tr=tr),
        grid=(n // tr,),
        in_specs=[pl.BlockSpec((1, 1, tr), lambda i: (i, 0, 0))],
        out_specs=[pl.BlockSpec((1, 1, tr), lambda i: (i, 0, 0)), _const_spec((CLASS_ROWS, LANES))],
        out_shape=[jax.ShapeDtypeStruct((n // tr, 1, tr), F32), jax.ShapeDtypeStruct((CLASS_ROWS, LANES), F32)],
        scratch_shapes=[pltpu.VMEM((CLASS_ROWS, LANES), F32)],
        compiler_params=_cparams(("arbitrary",), 32),
        name="rank",
    )(cls3)
    return rank.reshape(n), cnt[:N_CLASSES, 0]


def _sc_layout(n):
    info = plsc.get_sparse_core_info()
    nw = info.num_cores * info.num_subcores
    per_worker = n // nw
    assert per_worker * nw == n
    chunk = max(c for c in range(8, SC_MAX_CHUNK + 1, 8) if per_worker % c == 0)
    return info.num_cores, nw, per_worker // chunk, chunk


def _sc_scatter_rows(src, wts, pos, n_out):
    n, w = src.shape
    nc, nw, k, c = _sc_layout(n)
    mesh = plsc.VectorSubcoreMesh(core_axis_name="c", subcore_axis_name="s")

    @functools.partial(
        pl.kernel, mesh=mesh,
        out_type=(jax.ShapeDtypeStruct((n_out, w), src.dtype), jax.ShapeDtypeStruct((n_out, wts.shape[1]), wts.dtype)),
        scratch_types=[pltpu.VMEM((k, c), jnp.int32), pltpu.VMEM((c, w), src.dtype),
                       pltpu.VMEM((c, wts.shape[1]), wts.dtype), pltpu.SemaphoreType.DMA],
        name="moe_scatter",
    )
    def scatter(src_hbm, wts_hbm, pos_hbm, out_hbm, wout_hbm, idx_v, rows_v, wrows_v, sem):
        wid = lax.axis_index("s") * nc + lax.axis_index("c")
        pltpu.sync_copy(pos_hbm.at[wid], idx_v)

        @pl.loop(0, k)
        def _(j):
            off = pl.multiple_of(wid * (k * c) + j * c, 8)
            pltpu.sync_copy(src_hbm.at[pl.ds(off, c)], rows_v)
            pltpu.sync_copy(wts_hbm.at[pl.ds(off, c)], wrows_v)
            pltpu.async_copy(rows_v, out_hbm.at[idx_v.at[j]], sem).wait()
            pltpu.async_copy(wrows_v, wout_hbm.at[idx_v.at[j]], sem).wait()

    return scatter(src, wts, pos.reshape(nw, k, c))


def _sc_gather_rows(src, pos):
    n = pos.shape[0]
    w = src.shape[1]
    nc, nw, k, c = _sc_layout(n)
    mesh = plsc.VectorSubcoreMesh(core_axis_name="c", subcore_axis_name="s")

    @functools.partial(
        pl.kernel, mesh=mesh,
        out_type=jax.ShapeDtypeStruct((n, w), src.dtype),
        scratch_types=[pltpu.VMEM((k, c), jnp.int32), pltpu.VMEM((c, w), src.dtype), pltpu.SemaphoreType.DMA],
        name="moe_gather",
    )
    def gather(src_hbm, pos_hbm, out_hbm, idx_v, rows_v, sem):
        wid = lax.axis_index("s") * nc + lax.axis_index("c")
        pltpu.sync_copy(pos_hbm.at[wid], idx_v)

        @pl.loop(0, k)
        def _(j):
            off = pl.multiple_of(wid * (k * c) + j * c, 8)
            pltpu.async_copy(src_hbm.at[idx_v.at[j]], rows_v, sem).wait()
            pltpu.sync_copy(rows_v, out_hbm.at[pl.ds(off, c)])

    return gather(src, pos.reshape(nw, k, c))


def _gmm_kernel(lo_ref, hi_ref, new_ref, nact_ref, h_ref, wt_ref,
                w1a_ref, w1b_ref, w3a_ref, w3b_ref, w2a_ref, w2b_ref, o_ref, w13_s, w2_s):
    t = pl.program_id(0)

    @pl.when(new_ref[t] == 1)
    def _():
        for i, (w1_ref, w3_ref, w2_ref) in enumerate(((w1a_ref, w3a_ref, w2a_ref), (w1b_ref, w3b_ref, w2b_ref))):
            w13_s[i, :, :EXPERT_FF] = w1_ref[0, 0].astype(BF16)
            w13_s[i, :, EXPERT_FF:] = w3_ref[0, 0].astype(BF16)
            w2_s[i] = w2_ref[0, 0].astype(BF16)

    @pl.when(t < nact_ref[0])
    def _():
        x = _unpack_bf16_pairs(h_ref[...]).astype(BF16)
        wts = wt_ref[...]

        def expert(i):
            ab = jnp.dot(x, w13_s[i], preferred_element_type=F32)
            a, gate = ab[:, :EXPERT_FF], ab[:, EXPERT_FF:]
            hid = (a * jax.nn.sigmoid(a)) * gate
            return jnp.dot(hid.astype(BF16), w2_s[i], preferred_element_type=F32)

        y = wts[:, 0:1] * expert(0) + wts[:, 1:2] * expert(1)
        o_ref[...] = _pack_bf16_pairs(y)


def _gmm(tile_lo, tile_hi, tile_new, n_act, hs, ws, w1, w3, w2, layer, tm):
    n_pad, half = hs.shape
    d = 2 * half
    row = lambda t, lo, hi, new, na: (jnp.minimum(t, na[0] - 1), 0)
    e_lo = lambda t, lo, hi, new, na: (layer, lo[jnp.minimum(t, na[0] - 1)], 0, 0)
    e_hi = lambda t, lo, hi, new, na: (layer, hi[jnp.minimum(t, na[0] - 1)], 0, 0)
    up = lambda e: pl.BlockSpec((1, 1, d, EXPERT_FF), e)
    down = lambda e: pl.BlockSpec((1, 1, EXPERT_FF, d), e)
    return pl.pallas_call(
        _gmm_kernel,
        grid_spec=pltpu.PrefetchScalarGridSpec(
            num_scalar_prefetch=4,
            grid=(n_pad // tm,),
            in_specs=[
                pl.BlockSpec((tm, half), row),
                pl.BlockSpec((tm, ws.shape[1]), row),
                up(e_lo), up(e_hi), up(e_lo), up(e_hi), down(e_lo), down(e_hi),
            ],
            out_specs=pl.BlockSpec((tm, half), row),
            scratch_shapes=[pltpu.VMEM((2, d, 2 * EXPERT_FF), BF16), pltpu.VMEM((2, EXPERT_FF, d), BF16)],
        ),
        out_shape=jax.ShapeDtypeStruct((n_pad, half), jnp.int32),
        compiler_params=_cparams(("arbitrary",), 56),
        name="moe_gmm",
    )(tile_lo, tile_hi, tile_new, n_act, hs, ws, w1, w1, w3, w3, w2, w2)


def _moe_routed(h2p, route, w1, w3, w2, layer, tm=256):
    b, s, half = h2p.shape
    n = b * s
    n_pad = n + N_CLASSES * tm
    cls = route[:, 0, :].reshape(n)
    rank, counts = _rank(cls)
    counts = counts.astype(jnp.int32)
    padded = (counts + tm - 1) // tm * tm
    ends = jnp.cumsum(padded)
    pos = jnp.take(ends - padded, cls.astype(jnp.int32)) + rank.astype(jnp.int32)
    n_act = (ends[-1] // tm).reshape(1)
    tile_row = jnp.arange(n_pad // tm, dtype=jnp.int32) * tm
    tile_cls = jnp.minimum(jnp.sum(tile_row[:, None] >= ends[None, :], axis=1), N_CLASSES - 1)
    pair_lo, pair_hi = (jnp.asarray(a, jnp.int32) for a in _class_experts())
    wts = jnp.concatenate([route[:, 1, :].reshape(n, 1), route[:, 2, :].reshape(n, 1),
                           jnp.zeros((n, SC_WEIGHT_COLS - 2), F32)], axis=1)
    hs, ws = _sc_scatter_rows(h2p.reshape(n, half), wts, pos, n_pad)
    prev_cls = jnp.concatenate([jnp.full((1,), -1, tile_cls.dtype), tile_cls[:-1]])
    tile_new = ((tile_cls != prev_cls) & (tile_row < ends[-1])).astype(jnp.int32)
    ys = _gmm(jnp.take(pair_lo, tile_cls), jnp.take(pair_hi, tile_cls), tile_new, n_act, hs, ws,
              w1, w3, w2, layer, tm)
    return _sc_gather_rows(ys, pos).reshape(b, s, half)


def _class_experts():
    lo, hi = [], []
    for g in range(N_EXPERTS // EXPERTS_PER_GROUP):
        for i in range(EXPERTS_PER_GROUP):
            for j in range(i + 1, EXPERTS_PER_GROUP):
                lo.append(EXPERTS_PER_GROUP * g + i)
                hi.append(EXPERTS_PER_GROUP * g + j)
    return np.array(lo), np.array(hi)


def _final_residual_kernel(x_ref, y_ref, mod_ref, g_ref, o_ref):
    x = _pending_residual(x_ref[0], y_ref, mod_ref)
    o_ref[0] = x * lax.rsqrt(jnp.mean(x * x, axis=-1, keepdims=True) + EPS) * g_ref[...]


def _final_residual(x1, yp, mod, gain, tm=512):
    b, t, d = x1.shape
    tok = lambda bi, j: (bi, j, 0)
    return pl.pallas_call(
        _final_residual_kernel,
        grid=(b, t // tm),
        in_specs=[
            pl.BlockSpec((1, tm, d), tok),
            pl.BlockSpec((1, tm, d // 2), tok),
            pl.BlockSpec((1, 1, 6 * d), lambda bi, j: (bi, 0, 0)),
            _const_spec((1, d)),
        ],
        out_specs=pl.BlockSpec((1, tm, d), tok),
        out_shape=jax.ShapeDtypeStruct((b, t, d), F32),
        compiler_params=_cparams(("parallel", "arbitrary"), 32),
        name="final_residual",
    )(x1, yp, mod, gain)


def _rope_tables(t_lat, n_ctx):
    rows = t_lat // GRID_W
    row = jnp.repeat(jnp.arange(rows, dtype=F32), GRID_W)
    col = jnp.tile(jnp.arange(GRID_W, dtype=F32), rows)
    inv_freq = ROPE_THETA ** (-jnp.arange(0, ROPE_AXIS_DIM, 2, dtype=F32) / ROPE_AXIS_DIM)
    ang = jnp.stack([row[:, None] * inv_freq, col[:, None] * inv_freq], axis=1)
    cos, sin = jnp.cos(ang), jnp.sin(ang)
    zero = jnp.zeros_like(sin)
    cos_h = jnp.concatenate([cos, cos], axis=-1).reshape(t_lat, HEAD_DIM)
    sa_h = jnp.concatenate([-sin, zero], axis=-1).reshape(t_lat, HEAD_DIM)
    sb_h = jnp.concatenate([zero, sin], axis=-1).reshape(t_lat, HEAD_DIM)
    reps = LANES // HEAD_DIM
    pad = lambda tbl, fill: jnp.concatenate(
        [jnp.tile(tbl, (1, reps)), jnp.full((n_ctx, LANES), fill, F32)], axis=0)
    return pad(cos_h, 1.0), pad(sa_h, 0.0), pad(sb_h, 0.0)


def _permute_heads(w, axis):
    shp = w.shape
    w = w.reshape(shp[:axis] + (N_Q_HEADS, HEAD_DIM) + shp[axis + 1:])
    w = jnp.take(w, jnp.array(HEAD_PERM), axis=axis)
    return w.reshape(shp)


def _square_factor(n):
    r = int(round(math.sqrt(n)))
    assert r * r == n, "sequence lengths must be perfect squares for the two-stage DFT"
    return r


def kernel(x, c, ctx, c_ctx, w_ada, b_ada, norm1, norm2, w_in, q_gain, k_gain, sink, pool_w, pool_scale,
           w_branch, w_gate, b_gate, w_out, router_w, router_bias, w1, w3, w2, norm_f):
    b, t_lat, d = x.shape
    n_ctx = ctx.shape[1]
    s = t_lat + n_ctx
    depth = w_ada.shape[0]
    assert d == D_MODEL and b < MOD_ROWS and t_lat % 256 == 0 and n_ctx % 256 == 0 and t_lat % n_ctx == 0

    tokens = (x, ctx)
    cc = jnp.zeros((MOD_ROWS, d), F32).at[:b].set(c).at[b].set(c_ctx)
    mod_all = _ada(cc, w_ada, b_ada).reshape(depth, MOD_ROWS, 1, 6 * d)

    cos, sa, sb = _rope_tables(t_lat, n_ctx)
    seg = jnp.asarray(np.kron(np.eye(N_Q_HEADS), np.full((HEAD_DIM, HEAD_DIM), 1.0 / HEAD_DIM)), BF16)
    cs = jnp.asarray(_channel_dft_table()).astype(BF16)
    f_lat = [jnp.asarray(a).astype(BF16) for a in _fourier_tables(*(_square_factor(t_lat),) * 2)]
    f_ctx = [jnp.asarray(a).astype(BF16) for a in _fourier_tables(*(_square_factor(n_ctx),) * 2)]
    wbias = jnp.asarray(_window_bias(n_ctx))
    rw_hi = router_w.T.astype(BF16)
    rw_t = jnp.concatenate([rw_hi, (router_w.T - rw_hi.astype(F32)).astype(BF16)], axis=0)
    rb = router_bias.reshape(N_EXPERTS, 1)

    for l in range(depth):
        need_ctx = l < depth - 1
        s_out = s if need_ctx else t_lat
        cols = jnp.split(w_in[l], np.cumsum((512, 512, 512, 128, 128, 512, 128))[:], axis=1)
        f_w, p_w, qb_w, kb_w, vb_w, qw_w, kw_w, vw_w = cols
        w_in_l = jnp.concatenate([f_w, p_w, _permute_heads(qb_w, 1), _permute_heads(qw_w, 1),
                                  kb_w, vb_w, kw_w, vw_w], axis=1).astype(BF16)
        wb_l = jnp.stack([w_branch[l, 0], _permute_heads(w_branch[l, 1], 0),
                          _permute_heads(w_branch[l, 2], 0), w_branch[l, 3]]).astype(BF16)
        mod = mod_all[l]
        n1 = norm1[l].reshape(1, d)
        n2 = norm2[l].reshape(1, d)
        qg = jnp.tile(q_gain[l], N_Q_HEADS).reshape(1, BRANCH_W)
        kg = jnp.tile(k_gain[l], LANES // HEAD_DIM).reshape(1, LANES)

        zr, zi, p_in, qb, qw, kvb, kvw = _inproj(tokens, t_lat, mod, n1, w_in_l, qg, kg, seg, cs, cos, sa, sb)

        out_b = _gattn(qb, kvb, 0, t_lat, 0, s, GATTN_KEY_TILES)
        out_bc = _gattn(qb, kvb, t_lat, n_ctx, t_lat, n_ctx, 1) if need_ctx else out_b
        out_c = _wattn(jnp.take(sink[l], jnp.array(HEAD_PERM)) * LOG2E, wbias, qw, kvw, s_out, t_lat)
        out_a = _fourier(zr, zi, f_lat[0], f_lat[1], t_lat, 0)
        out_ac = _fourier(zr, zi, f_ctx[0], f_ctx[1], n_ctx, t_lat // n_ctx) if need_ctx else out_a

        weights = (w_gate[l].astype(BF16), b_gate[l].reshape(4, 1, d), wb_l, w_out[l].astype(BF16), rw_t, rb)
        pool_params = (pool_w[l].astype(BF16), pool_scale[l].reshape(1, BRANCH_W))
        pending = len(tokens) == 3
        merged = _merge(tokens if pending else tokens[:1], mod, None, n1, n2, (out_a, out_b, out_c), (0, 0, 0),
                        p_in, pool_params, (0, t_lat), weights, s_out, 0, t_lat, 2 * MERGE_CHAIN_ROWS)
        if need_ctx:
            merged = _merge(tokens if pending else tokens[1:], mod, b, n1, n2, (out_ac, out_bc, out_c),
                            (0, 0, t_lat), p_in, pool_params, (t_lat, s), weights, s_out, t_lat, n_ctx,
                            MERGE_CHAIN_ROWS, prev=merged)
        x1, h2p, route = merged
        yp = _moe_routed(h2p, route, w1, w3, w2, l)
        tokens = (x1, yp, mod)

    return _final_residual(*tokens, norm_f.reshape(1, d))
```

```python
import functools
import math

import numpy as np
import jax
import jax.numpy as jnp
from jax import lax
from jax.experimental import pallas as pl
from jax.experimental.pallas import tpu as pltpu
from jax.experimental.pallas import tpu_sc as plsc

F32 = jnp.float32
BF16 = jnp.bfloat16

D_MODEL = 1024
HEAD_DIM = 64
N_Q_HEADS = 8
N_KV_HEADS = 2
GRID_W = 64
ROPE_THETA = 10000.0
ROPE_AXIS_DIM = HEAD_DIM // 2
QBLK = 128
WINDOW = 128
BRANCH_W = 512
GROUP_W = 128
POOL_WINDOWS = (2, 4, 8, 16)
N_EXPERTS = 16
EXPERTS_PER_GROUP = 4
EXPERT_FF = 512
EPS = 1e-6
MOD_ROWS = 16
NEG_BIG = -1e30
LOG2E = math.log2(math.e)
LANES = 128
POOL_HALO = 16
PAIRS_PER_GROUP = 6
N_CLASSES = 24
CLASS_ROWS = 32
ROUTE_ROWS = 8
SC_MAX_CHUNK = 128
SC_WEIGHT_COLS = 128

HEAD_PERM = (0, 4, 1, 5, 2, 6, 3, 7)


def _cparams(sem, vmem_mb):
    return pltpu.CompilerParams(dimension_semantics=sem, vmem_limit_bytes=vmem_mb * 1024 * 1024)


def _const_spec(shape):
    nd = len(shape)
    return pl.BlockSpec(shape, lambda *_: (0,) * nd)


def _ada_kernel(c_ref, w_ref, b_ref, o_ref):
    c = c_ref[...]
    s = c * jax.nn.sigmoid(c)
    o_ref[0] = jnp.dot(s.astype(BF16), w_ref[0].astype(BF16), preferred_element_type=F32) + b_ref[0]


def _ada(cc, w_ada, b_ada):
    depth, d, n = w_ada.shape
    tn = 1536
    return pl.pallas_call(
        _ada_kernel,
        grid=(depth, n // tn),
        in_specs=[
            pl.BlockSpec((MOD_ROWS, d), lambda l, j: (0, 0)),
            pl.BlockSpec((1, d, tn), lambda l, j: (l, 0, j)),
            pl.BlockSpec((1, 1, tn), lambda l, j: (l, 0, j)),
        ],
        out_specs=pl.BlockSpec((1, MOD_ROWS, tn), lambda l, j: (l, 0, j)),
        out_shape=jax.ShapeDtypeStruct((depth, MOD_ROWS, n), F32),
        compiler_params=_cparams(("arbitrary", "arbitrary"), 40),
        name="ada",
    )(cc, w_ada, b_ada.reshape(depth, 1, n))


def _norm_mod(x, gain, shift, scale):
    ms = jnp.mean(x * x, axis=-1, keepdims=True)
    return (x * lax.rsqrt(ms + EPS) * gain) * (1.0 + scale) + shift


def _mod_slices(m):
    d = D_MODEL
    return [m[:, i * d:(i + 1) * d] for i in range(6)]


def _head_norm(z, seg, gain):
    ms = jnp.dot((z * z).astype(BF16), seg, preferred_element_type=F32)
    return z * lax.rsqrt(ms + EPS) * gain


def _rope(z, cos, sin_a, sin_b):
    outs = []
    for c in range(z.shape[1] // LANES):
        zc = z[:, c * LANES:(c + 1) * LANES]
        nxt = pltpu.roll(zc, LANES - ROPE_AXIS_DIM // 2, 1)
        prv = pltpu.roll(zc, ROPE_AXIS_DIM // 2, 1)
        outs.append(zc * cos + nxt * sin_a + prv * sin_b)
    return outs[0] if len(outs) == 1 else jnp.concatenate(outs, axis=-1)


def _stream_specs(tm, d, n_lat):
    return [pl.BlockSpec((1, tm, d), lambda bi, j: (bi, jnp.minimum(j, n_lat - 1), 0)),
            pl.BlockSpec((1, tm, d), lambda bi, j: (bi, jnp.maximum(j - n_lat, 0), 0))]


def _pending_residual(x, y_ref, modp_ref, rows=slice(None)):
    return x + _mod_slices(modp_ref[0])[5] * _unpack_bf16_pairs(y_ref[0, rows, :])


def _inproj_kernel(*refs, n_lat, pending):
    if pending:
        x_ref, y_ref, modp_ref = refs[:3]
        x = _pending_residual(x_ref[0], y_ref, modp_ref)
    else:
        xl_ref, xc_ref = refs[:2]
        x = jnp.where(pl.program_id(1) >= n_lat, xc_ref[0], xl_ref[0])
    (mod_ref, n1_ref, w_ref, qg_ref, kg_ref, seg_ref, cs_ref, cos_ref, sa_ref, sb_ref,
     zr_ref, zi_ref, p_ref, qb_ref, qw_ref, kvb_ref, kvw_ref) = refs[3 if pending else 2:]
    sh1, sc1 = _mod_slices(mod_ref[0])[:2]
    h = _norm_mod(x, n1_ref[...], sh1, sc1)
    u = jnp.dot(h.astype(BF16), w_ref[...], preferred_element_type=F32)
    cos, sa, sb = cos_ref[...], sa_ref[...], sb_ref[...]
    w = BRANCH_W
    f_in = u[:, 0:w].astype(BF16)
    zr, zi = [], []
    for g in range(w // GROUP_W):
        z = jnp.dot(f_in[:, g * GROUP_W:(g + 1) * GROUP_W], cs_ref[...], preferred_element_type=F32)
        zr.append(z[:, :GROUP_W])
        zi.append(z[:, GROUP_W:])
    per_half = len(zr) // FOURIER_HALVES
    for hf in range(FOURIER_HALVES):
        groups = slice(hf * per_half, (hf + 1) * per_half)
        zr_ref[hf, 0] = jnp.concatenate(zr[groups], axis=-1).astype(BF16)
        zi_ref[hf, 0] = jnp.concatenate(zi[groups], axis=-1).astype(BF16)
    p_ref[0] = u[:, w:2 * w].astype(BF16)
    seg = seg_ref[...]
    qb = _rope(_head_norm(u[:, 2 * w:3 * w], seg, qg_ref[...]), cos, sa, sb)
    qb_ref[0] = (qb * (HEAD_DIM ** -0.5 * LOG2E)).astype(BF16)
    qw = _rope(u[:, 3 * w:4 * w], cos, sa, sb)
    qw_ref[0] = (qw * (HEAD_DIM ** -0.5 * LOG2E)).astype(BF16)
    o = 4 * w
    kb = _rope(_head_norm(u[:, o:o + LANES], seg[:LANES, :LANES], kg_ref[...]), cos, sa, sb)
    vb = u[:, o + LANES:o + 2 * LANES]
    kw = _rope(u[:, o + 2 * LANES:o + 3 * LANES], cos, sa, sb)
    vw = u[:, o + 3 * LANES:o + 4 * LANES]
    kvb_ref[0] = jnp.concatenate([kb, vb, jnp.ones_like(vb)], axis=-1).astype(BF16)
    kvw_ref[0] = jnp.concatenate([kw, vw, jnp.ones_like(vw)], axis=-1).astype(BF16)


def _inproj(tokens, t_lat, mod, n1, w_in, qg, kg, seg, cs, cos, sa, sb, tm=256):
    pending = len(tokens) == 3
    b, _, d = tokens[0].shape
    s = tokens[0].shape[1] if pending else t_lat + tokens[1].shape[1]
    nw = w_in.shape[1]
    n_lat = t_lat // tm
    tok = lambda bi, j: (bi, j, 0)
    tab = lambda bi, j: (j, 0)
    mod_spec = pl.BlockSpec((1, 1, 6 * d), lambda bi, j: (jnp.where(j >= n_lat, b, bi), 0, 0))
    if pending:
        token_specs = [pl.BlockSpec((1, tm, d), tok), pl.BlockSpec((1, tm, d // 2), tok), mod_spec]
    else:
        token_specs = _stream_specs(tm, d, n_lat)
    widths = (BRANCH_W,) * 3 + (3 * LANES, 3 * LANES)
    zw = BRANCH_W // FOURIER_HALVES
    z_spec = pl.BlockSpec((FOURIER_HALVES, 1, tm, zw), lambda bi, j: (0, bi, j, 0))
    z_shape = jax.ShapeDtypeStruct((FOURIER_HALVES, b, s, zw), BF16)
    return pl.pallas_call(
        functools.partial(_inproj_kernel, n_lat=n_lat, pending=pending),
        grid=(b, s // tm),
        in_specs=token_specs + [
            mod_spec,
            _const_spec((1, d)),
            _const_spec((d, nw)),
            _const_spec((1, BRANCH_W)),
            _const_spec((1, LANES)),
            _const_spec((BRANCH_W, BRANCH_W)),
            _const_spec((GROUP_W, 2 * GROUP_W)),
            pl.BlockSpec((tm, LANES), tab),
            pl.BlockSpec((tm, LANES), tab),
            pl.BlockSpec((tm, LANES), tab),
        ],
        out_specs=[z_spec, z_spec] + [pl.BlockSpec((1, tm, wd), tok) for wd in widths],
        out_shape=[z_shape, z_shape] + [jax.ShapeDtypeStruct((b, s, wd), BF16) for wd in widths],
        compiler_params=_cparams(("parallel", "arbitrary"), 48),
        name="inproj",
    )(*tokens, mod, n1, w_in, qg, kg, seg, cs, cos, sa, sb)


def _split_heads(qc, lane):
    zero = jnp.zeros_like(qc)
    return jnp.concatenate([jnp.where(lane < HEAD_DIM, qc, zero),
                            jnp.where(lane >= HEAD_DIM, qc, zero)], axis=0)


def _gattn_kernel(q_ref, kv_ref, o_ref, q2_s, m_s, acc_s, *, n_tiles, sub):
    tq = q_ref.shape[1]
    tk = kv_ref.shape[1] // n_tiles
    lane = lax.broadcasted_iota(jnp.int32, (1, LANES), 1)
    nt = (((1,), (1,)), ((), ()))
    n_chunks = BRANCH_W // LANES
    for c in range(n_chunks):
        q2_s[2 * c * tq:(2 * c + 2) * tq, :] = _split_heads(q_ref[0, :, c * LANES:(c + 1) * LANES], lane)

    def tile(k, v, first):
        for r in range(2 * n_chunks * tq // sub):
            rows = slice(r * sub, (r + 1) * sub)
            s = lax.dot_general(q2_s[rows, :], k, nt, preferred_element_type=F32)
            rm = jnp.max(s, axis=-1, keepdims=True)
            if first:
                m_new = rm
                acc_s[rows, :] = jnp.dot(jnp.exp2(s - m_new).astype(BF16), v, preferred_element_type=F32)
            else:
                m_old = m_s[rows, :]
                m_new = jnp.maximum(m_old, rm)
                alpha = jnp.exp2(m_old - m_new)
                pv = jnp.dot(jnp.exp2(s - m_new).astype(BF16), v, preferred_element_type=F32)
                acc_s[rows, :] = alpha * acc_s[rows, :] + pv
            m_s[rows, :] = m_new

    for t in range(n_tiles):
        keys = slice(t * tk, (t + 1) * tk)
        tile(kv_ref[0, keys, 0:LANES], kv_ref[0, keys, LANES:3 * LANES], t == 0)
    for c in range(n_chunks):
        lo = acc_s[2 * c * tq:(2 * c + 1) * tq, :]
        hi = acc_s[(2 * c + 1) * tq:(2 * c + 2) * tq, :]
        o = jnp.where(lane < HEAD_DIM, lo[:, :LANES] / lo[:, LANES:], hi[:, :LANES] / hi[:, LANES:])
        o_ref[0, :, c * LANES:(c + 1) * LANES] = o.astype(BF16)


def _gattn(qb, kv, q_start, q_len, k_start, k_len, n_tiles, tq=256, sub=128):
    b = qb.shape[0]
    assert q_start % tq == 0 and q_len % tq == 0 and k_start % k_len == 0 and k_len % (n_tiles * LANES) == 0
    rows = 2 * tq * (BRANCH_W // LANES)
    return pl.pallas_call(
        functools.partial(_gattn_kernel, n_tiles=n_tiles, sub=sub),
        grid=(b, q_len // tq),
        in_specs=[
            pl.BlockSpec((1, tq, BRANCH_W), lambda bi, j: (bi, q_start // tq + j, 0)),
            pl.BlockSpec((1, k_len, 3 * LANES), lambda bi, j: (bi, k_start // k_len, 0)),
        ],
        out_specs=pl.BlockSpec((1, tq, BRANCH_W), lambda bi, j: (bi, j, 0)),
        out_shape=jax.ShapeDtypeStruct((b, q_len, BRANCH_W), BF16),
        scratch_shapes=[pltpu.VMEM((rows, LANES), BF16), pltpu.VMEM((rows, 1), F32),
                        pltpu.VMEM((rows, 2 * LANES), F32)],
        compiler_params=_cparams(("parallel", "arbitrary"), 48),
        name="gattn",
    )(qb, kv)


WATTN_QBLOCKS = 2
GATTN_KEY_TILES = 2
MERGE_CHAIN_ROWS = 256
FOURIER_HALVES = 2


def _window_bias(n_ctx):
    tq = WATTN_QBLOCKS * QBLK
    qi = np.arange(tq)[:, None]
    kj = np.arange(tq + 2 * QBLK)[None, :]
    band = np.abs(kj - WINDOW - qi) <= WINDOW
    blk = kj // QBLK
    variants = [band & (blk != 0), band, band & (blk != WATTN_QBLOCKS + 1), np.zeros_like(band)]
    out = [np.concatenate([np.ones((tq, n_ctx), bool), v], axis=1) for v in variants]
    return np.where(np.stack(out), 0.0, NEG_BIG).astype(np.float32)


def _wattn_kernel(sink_ref, bias_ref, q_ref, *refs, sub):
    o_ref, q2_s, kv_s, acc_s = refs[-4:]
    tq = q_ref.shape[1]
    off = 0
    for blk in refs[:-4]:
        kv_s[off:off + blk.shape[1], :] = blk[0]
        off += blk.shape[1]
    lane = lax.broadcasted_iota(jnp.int32, (1, LANES), 1)
    nt = (((1,), (1,)), ((), ()))
    n_chunks = BRANCH_W // LANES
    for c in range(n_chunks):
        q2_s[2 * c * tq:(2 * c + 2) * tq, :] = _split_heads(q_ref[0, :, c * LANES:(c + 1) * LANES], lane)
    k, v = kv_s[:, 0:LANES], kv_s[:, LANES:3 * LANES]
    for r in range(2 * n_chunks * tq // sub):
        rows = slice(r * sub, (r + 1) * sub)
        q_off = (r * sub) % tq
        sk = sink_ref[(r * sub) // tq]
        s = lax.dot_general(q2_s[rows, :], k, nt, preferred_element_type=F32) + bias_ref[0, q_off:q_off + sub, :]
        m = jnp.maximum(jnp.max(s, axis=-1, keepdims=True), sk)
        pv = jnp.dot(jnp.exp2(s - m).astype(BF16), v, preferred_element_type=F32)
        acc_s[rows, :LANES] = pv[:, :LANES]
        acc_s[rows, LANES:] = pv[:, LANES:] + jnp.exp2(sk - m)
    for c in range(n_chunks):
        lo = acc_s[2 * c * tq:(2 * c + 1) * tq, :]
        hi = acc_s[(2 * c + 1) * tq:(2 * c + 2) * tq, :]
        o = jnp.where(lane < HEAD_DIM, lo[:, :LANES] / lo[:, LANES:], hi[:, :LANES] / hi[:, LANES:])
        o_ref[0, :, c * LANES:(c + 1) * LANES] = o.astype(BF16)


def _wattn(sink, bias, qw, kv, s_out, t_lat):
    b, s, _ = qw.shape
    n_ctx = s - t_lat
    nq = WATTN_QBLOCKS
    tq = nq * QBLK
    assert n_ctx % tq == 0 and t_lat // tq >= 2
    last = s // QBLK - 1
    n_lat = t_lat // tq
    variant = lambda j: jnp.where(j >= n_lat, 3, jnp.where(j == 0, 0, jnp.where(j == n_lat - 1, 2, 1)))
    key_block = lambda off: pl.BlockSpec(
        (1, QBLK, 3 * LANES), lambda bi, j: (bi, jnp.clip(j * nq + off, 0, last), 0))
    rows = 2 * tq * (BRANCH_W // LANES)
    return pl.pallas_call(
        functools.partial(_wattn_kernel, sub=128),
        scratch_shapes=[pltpu.VMEM((rows, LANES), BF16), pltpu.VMEM((bias.shape[2], 3 * LANES), BF16),
                        pltpu.VMEM((rows, 2 * LANES), F32)],
        grid=(b, s_out // tq),
        in_specs=[
            pl.BlockSpec(memory_space=pltpu.SMEM),
            pl.BlockSpec((1,) + bias.shape[1:], lambda bi, j: (variant(j), 0, 0)),
            pl.BlockSpec((1, tq, BRANCH_W), lambda bi, j: (bi, j, 0)),
            pl.BlockSpec((1, n_ctx, 3 * LANES), lambda bi, j: (bi, t_lat // n_ctx, 0)),
        ] + [key_block(off) for off in range(-1, nq + 1)],
        out_specs=pl.BlockSpec((1, tq, BRANCH_W), lambda bi, j: (bi, j, 0)),
        out_shape=jax.ShapeDtypeStruct((b, s_out, BRANCH_W), BF16),
        compiler_params=_cparams(("parallel", "arbitrary"), 32),
        name="wattn",
    )(sink, bias, qw, *([kv] * (nq + 3)))


def _fourier_tables(n1, n2):
    t = n1 * n2
    k2 = np.arange(n2)[None, :, None]
    t2 = np.arange(n2)[None, None, :]
    t1 = np.arange(n1)[:, None, None]
    theta = 2.0 * np.pi * ((k2 * t2 * n1 + k2 * t1) % t) / t
    er, ei = np.cos(theta) / math.sqrt(n2), -np.sin(theta) / math.sqrt(n2)
    e = np.concatenate([np.concatenate([er, -ei], axis=2), np.concatenate([ei, er], axis=2)], axis=1)
    k1 = np.arange(n1)[:, None]
    phi = 2.0 * np.pi * ((k1 * np.arange(n1)[None, :]) % n1) / n1
    dcat = np.concatenate([np.cos(phi), np.sin(phi)], axis=1) / math.sqrt(n1)
    return e.astype(np.float32), dcat.astype(np.float32)


def _channel_dft_table():
    c = np.arange(GROUP_W)
    ang = 2.0 * np.pi * ((c[:, None] * c[None, :]) % GROUP_W) / GROUP_W
    return (np.concatenate([np.cos(ang), -np.sin(ang)], axis=1) / math.sqrt(GROUP_W)).astype(np.float32)


def _fourier_kernel(zr_ref, zi_ref, e_ref, d_ref, o_ref, yr_s, yi_s, st_s, *, n1, n2):
    nc = st_s.shape[0]
    cw = nc * LANES
    chunk = lambda c: slice(c * LANES, (c + 1) * LANES)

    def gather(ref, start, size, stride):
        return jnp.concatenate([ref[c, pl.ds(start, size, stride=stride), :] for c in range(nc)], axis=-1)

    for t1 in range(n1):
        lanes = slice(t1 * cw, (t1 + 1) * cw)
        xs = jnp.concatenate([zr_ref[0, 0, :, lanes], zi_ref[0, 0, :, lanes]], axis=0)
        y = jnp.dot(e_ref[t1], xs, preferred_element_type=F32)
        for c in range(nc):
            yr_s[c, t1 * n2:(t1 + 1) * n2, :] = y[:n2, chunk(c)]
            yi_s[c, t1 * n2:(t1 + 1) * n2, :] = y[n2:, chunk(c)]
    for k2 in range(n2):
        ys = jnp.concatenate([gather(yr_s, k2, n1, n2), gather(yi_s, k2, n1, n2)], axis=0).astype(BF16)
        o = jnp.dot(d_ref[...], ys, preferred_element_type=F32)
        for c in range(nc):
            st_s[c, pl.ds(k2, n1, stride=n2), :] = o[:, chunk(c)]
    for c in range(nc):
        o_ref[0, :, chunk(c)] = st_s[c].astype(BF16)


def _fourier(zr, zi, e_tab, d_tab, t_len, row_block):
    n_half, b, s, cw = zr.shape
    n1 = d_tab.shape[0]
    n2 = t_len // n1
    view = lambda z: z.reshape(n_half, b, s // n1, n1 * cw)
    zspec = pl.BlockSpec((1, 1, n2, n1 * cw), lambda bi, j: (j, bi, row_block, 0))
    return pl.pallas_call(
        functools.partial(_fourier_kernel, n1=n1, n2=n2),
        grid=(b, n_half),
        in_specs=[zspec, zspec, _const_spec(e_tab.shape), _const_spec(d_tab.shape)],
        out_specs=pl.BlockSpec((1, t_len, cw), lambda bi, j: (bi, 0, j)),
        out_shape=jax.ShapeDtypeStruct((b, t_len, n_half * cw), BF16),
        scratch_shapes=[pltpu.VMEM((cw // LANES, t_len, LANES), F32)] * 3,
        compiler_params=_cparams(("parallel", "arbitrary"), 48),
        name="fourier",
    )(view(zr), view(zi), e_tab, d_tab)


def _pool_rows(ext, pos, n, w_ref, scale):
    n_ext = ext.shape[0]
    rows = n_ext - 2 * POOL_HALO
    back = lambda v, k: pltpu.roll(v, k, 0)
    fwd = lambda v, k: pltpu.roll(v, n_ext - k, 0)
    outs = []
    for gi, w in enumerate(POOL_WINDOWS):
        e = ext[:, gi * GROUP_W:(gi + 1) * GROUP_W]
        wsum = e + back(e, 1)
        half = 1
        while 2 * half < w:
            wsum = back(wsum, half) + fwd(wsum, half)
            half *= 2
        own = slice(POOL_HALO, POOL_HALO + rows)
        cnt = jnp.minimum(pos + w // 2, n) - jnp.maximum(pos - w // 2, 0)
        pooled = wsum[own] / cnt.astype(F32) - e[own]
        outs.append(jnp.dot(pooled.astype(BF16), w_ref[gi], preferred_element_type=F32))
    return (jnp.concatenate(outs, axis=-1) * scale).astype(BF16)


def _route(logits_t, bias):
    aff = jax.nn.sigmoid(logits_t)
    sel = aff + bias
    neg = -jnp.inf
    firsts, seconds, scores = [], [], []
    for g in range(N_EXPERTS // EXPERTS_PER_GROUP):
        s = [sel[EXPERTS_PER_GROUP * g + k:EXPERTS_PER_GROUP * g + k + 1, :] for k in range(EXPERTS_PER_GROUP)]
        m1 = jnp.maximum(jnp.maximum(s[0], s[1]), jnp.maximum(s[2], s[3]))
        i1 = jnp.where(s[0] == m1, 0, jnp.where(s[1] == m1, 1, jnp.where(s[2] == m1, 2, 3)))
        r = [jnp.where(i1 == k, neg, s[k]) for k in range(EXPERTS_PER_GROUP)]
        m2 = jnp.maximum(jnp.maximum(r[0], r[1]), jnp.maximum(r[2], r[3]))
        i2 = jnp.where(r[0] == m2, 0, jnp.where(r[1] == m2, 1, jnp.where(r[2] == m2, 2, 3)))
        firsts.append(i1 + EXPERTS_PER_GROUP * g)
        seconds.append(i2 + EXPERTS_PER_GROUP * g)
        scores.append(m1 + m2)
    best = jnp.maximum(jnp.maximum(scores[0], scores[1]), jnp.maximum(scores[2], scores[3]))
    pick = lambda v: jnp.where(scores[0] == best, v[0], jnp.where(scores[1] == best, v[1],
                                                                 jnp.where(scores[2] == best, v[2], v[3])))
    e1, e2 = pick(firsts), pick(seconds)
    eidx = lax.broadcasted_iota(jnp.int32, aff.shape, 0)
    a1 = jnp.sum(jnp.where(eidx == e1, aff, 0.0), axis=0, keepdims=True)
    a2 = jnp.sum(jnp.where(eidx == e2, aff, 0.0), axis=0, keepdims=True)
    tot = a1 + a2
    w1, w2 = a1 / tot, a2 / tot
    swap = e1 > e2
    lo = jnp.where(swap, e2, e1) & (EXPERTS_PER_GROUP - 1)
    hi = jnp.where(swap, e1, e2) & (EXPERTS_PER_GROUP - 1)
    pair = jnp.where(lo == 0, 0, jnp.where(lo == 1, 3, 5)) + hi - lo - 1
    cls = (e1 >> 2) * PAIRS_PER_GROUP + pair
    rows = [cls.astype(F32), jnp.where(swap, w2, w1), jnp.where(swap, w1, w2)]
    return jnp.concatenate(rows + [jnp.zeros_like(w1)] * (ROUTE_ROWS - len(rows)), axis=0)


def _pack_bf16_pairs(v):
    w = v.shape[1] // 2
    bits = pltpu.bitcast(v.astype(BF16).astype(F32), jnp.uint32)
    return pltpu.bitcast(bits[:, :w] | (bits[:, w:] >> 16), jnp.int32)


def _unpack_bf16_pairs(p):
    bits = pltpu.bitcast(p, jnp.uint32)
    hi = pltpu.bitcast(bits & jnp.uint32(0xFFFF0000), F32)
    lo = pltpu.bitcast(bits << 16, F32)
    return jnp.concatenate([hi, lo], axis=-1)


def _merge_kernel(*refs, n_real, n_fill, pending, **static):
    n_tok = 3 if pending else 1
    (mod_ref, n1_ref, n2_ref, a_ref, b_ref, c_ref, pc_ref, pp_ref, pn_ref, pw_ref, psc_ref,
     wg_ref, bg_ref, wb_ref, wo_ref, rw_ref, rb_ref) = refs[n_tok:n_tok + 17]
    outs = refs[-4:-1]
    ins = (refs[:n_tok], mod_ref, n1_ref, n2_ref, a_ref, b_ref, c_ref,
           (pc_ref, pp_ref, pn_ref, pw_ref, psc_ref),
           wg_ref, bg_ref, wb_ref, wo_ref, rw_ref, rb_ref)
    if n_fill:
        @pl.when(pl.program_id(1) >= n_real)
        def _():
            for ref in outs:
                ref[...] = jnp.zeros_like(ref)

        pl.when(pl.program_id(1) < n_real)(lambda: _merge_tile(ins, outs, refs[-1], **static))
    else:
        _merge_tile(ins, outs, refs[-1], **static)


def _merge_tile(ins, outs, merged_s, *, nw, sub, p_off, seq_lo, seq_hi):
    (tok_refs, mod_ref, n1_ref, n2_ref, a_ref, b_ref, c_ref, pool_refs,
     wg_ref, bg_ref, wb_ref, wo_ref, rw_ref, rb_ref) = ins
    xo_ref, h2_ref, route_ref = outs
    pc_ref, pp_ref, pn_ref, pw_ref, psc_ref = pool_refs
    x_ref = tok_refs[0]
    tm = x_ref.shape[1]
    sh1, sc1, g1, sh2, sc2, _ = _mod_slices(mod_ref[0])
    row0 = p_off + pl.program_id(1) * tm
    ext = jnp.concatenate([pp_ref[0], pc_ref[0], pn_ref[0]], axis=0).astype(F32)
    gpos = row0 - POOL_HALO + lax.broadcasted_iota(jnp.int32, (tm + 2 * POOL_HALO, 1), 0)
    ext = jnp.where((gpos >= seq_lo) & (gpos < seq_hi), ext, 0.0)
    for r in range(tm // sub):
        rows = slice(r * sub, (r + 1) * sub)
        x = x_ref[0, rows, :]
        if len(tok_refs) == 3:
            x = _pending_residual(x, tok_refs[1], tok_refs[2], rows)
        hb = _norm_mod(x, n1_ref[...], sh1, sc1).astype(BF16)
        pos = row0 - seq_lo + r * sub + lax.broadcasted_iota(jnp.int32, (sub, 1), 0)
        pooled = _pool_rows(ext[r * sub:(r + 1) * sub + 2 * POOL_HALO], pos, seq_hi - seq_lo,
                            pw_ref, psc_ref[...])
        branches = (a_ref[0, rows, :], b_ref[0, rows, :], c_ref[0, rows, :], pooled)
        for n in range(D_MODEL // nw):
            cols = slice(n * nw, (n + 1) * nw)
            merged = None
            for i, br in enumerate(branches):
                gate = jax.nn.sigmoid(
                    jnp.dot(hb, wg_ref[i, :, cols], preferred_element_type=F32) + bg_ref[i, :, cols])
                term = gate * jnp.dot(br, wb_ref[i, :, cols], preferred_element_type=F32)
                merged = term if merged is None else merged + term
            merged_s[rows, cols] = merged.astype(BF16)
        y = jnp.dot(merged_s[rows, :], wo_ref[...], preferred_element_type=F32)
        xn = x + g1 * y
        xo_ref[0, rows, :] = xn
        h2 = _norm_mod(xn, n2_ref[...], sh2, sc2)
        h2_ref[0, rows, :] = _pack_bf16_pairs(h2)
        h_hi = h2.astype(BF16)
        h_lo = (h2 - h_hi.astype(F32)).astype(BF16)
        nt = (((1,), (1,)), ((), ()))
        by_hi = lax.dot_general(rw_ref[...], h_hi, nt, preferred_element_type=F32)
        by_lo = lax.dot_general(rw_ref[:N_EXPERTS, :], h_lo, nt, preferred_element_type=F32)
        logits_t = by_hi[:N_EXPERTS] + by_hi[N_EXPERTS:] + by_lo
        route_ref[0, :, rows] = _route(logits_t, rb_ref[...])


def _merge(tokens, mod, mod_row, n1, n2, branches, offsets, p_in, pool_params, seq, weights, s_out, out_off,
           rows, tm, prev=None):
    pending = len(tokens) == 3
    b, _, d = tokens[0].shape
    wg, bg, wb, wo, rw_t, rb = weights
    pool_w, pool_scale = pool_params
    seq_lo, seq_hi = seq
    n_real = rows // tm
    n_fill = -(-(s_out - out_off - rows) // tm) if prev is None else 0
    step = lambda j: jnp.minimum(j, n_real - 1)
    blk = lambda width, off: pl.BlockSpec((1, tm, width), lambda bi, j: (bi, off // tm + step(j), 0))
    out_blk = lambda width: pl.BlockSpec((1, tm, width), lambda bi, j: (bi, out_off // tm + j, 0))
    assert all(o % tm == 0 for o in offsets) and out_off % tm == 0 and rows % tm == 0 and seq_lo % tm == 0
    hb = tm // POOL_HALO
    last_halo = p_in.shape[1] // POOL_HALO - 1
    halo = lambda shift: pl.BlockSpec(
        (1, POOL_HALO, BRANCH_W),
        lambda bi, j: (bi, jnp.clip((seq_lo // tm + step(j) + shift) * hb - 1 + shift, 0, last_halo), 0))
    out_shape = [
        jax.ShapeDtypeStruct((b, s_out, d), F32),
        jax.ShapeDtypeStruct((b, s_out, d // 2), jnp.int32),
        jax.ShapeDtypeStruct((b, ROUTE_ROWS, s_out), F32),
    ]
    mod_spec = pl.BlockSpec((1, 1, 6 * d), lambda bi, j: (bi if mod_row is None else mod_row, 0, 0))
    tok_off = out_off if pending else 0
    token_specs = [blk(d, tok_off)] + ([blk(d // 2, tok_off), mod_spec] if pending else [])
    n_in = len(tokens) + 17
    extra_specs = [] if prev is None else [pl.BlockSpec(memory_space=pl.ANY)] * 3
    return pl.pallas_call(
        functools.partial(_merge_kernel, nw=512, sub=MERGE_CHAIN_ROWS, n_real=n_real, n_fill=n_fill,
                          pending=pending, p_off=seq_lo, seq_lo=seq_lo, seq_hi=seq_hi),
        scratch_shapes=[pltpu.VMEM((tm, d), BF16)],
        grid=(b, n_real + n_fill),
        in_specs=token_specs + [
            mod_spec,
            _const_spec((1, d)), _const_spec((1, d)),
            *[blk(BRANCH_W, off) for off in offsets],
            blk(BRANCH_W, seq_lo), halo(0), halo(1),
            _const_spec(pool_w.shape), _const_spec((1, BRANCH_W)),
            _const_spec(wg.shape), _const_spec(bg.shape), _const_spec(wb.shape), _const_spec(wo.shape),
            _const_spec(rw_t.shape), _const_spec(rb.shape),
        ] + extra_specs,
        out_specs=[
            out_blk(d),
            out_blk(d // 2),
            pl.BlockSpec((1, ROUTE_ROWS, tm), lambda bi, j: (bi, 0, out_off // tm + j)),
        ],
        out_shape=out_shape,
        input_output_aliases={} if prev is None else {n_in + i: i for i in range(3)},
        compiler_params=_cparams(("parallel", "arbitrary"), 56),
        name="merge",
    )(*tokens, mod, n1, n2, *branches, p_in, p_in, p_in, pool_w, pool_scale,
      wg, bg, wb, wo, rw_t, rb, *(() if prev is None else prev))


def _rank_kernel(cls_ref, rank_ref, cnt_ref, cnt_s, *, tr):
    @pl.when(pl.program_id(0) == 0)
    def _():
        cnt_s[...] = jnp.zeros_like(cnt_s)

    cls = cls_ref[0]
    cid = lax.broadcasted_iota(jnp.int32, (CLASS_ROWS, tr), 0).astype(F32)
    onehot = cid == cls
    before = lax.broadcasted_iota(jnp.int32, (tr, tr), 0) < lax.broadcasted_iota(jnp.int32, (tr, tr), 1)
    prefix = jnp.dot(jnp.where(onehot, 1.0, 0.0).astype(BF16), jnp.where(before, 1.0, 0.0).astype(BF16),
                     preferred_element_type=F32)
    carry = cnt_s[...][:, 0:1]
    rank_ref[0] = jnp.sum(jnp.where(onehot, prefix + carry, 0.0), axis=0, keepdims=True)
    cnt_s[...] += jnp.sum(jnp.where(onehot, 1.0, 0.0), axis=1, keepdims=True)
    cnt_ref[...] = cnt_s[...]


def _rank(cls_flat, tr=512):
    n = cls_flat.shape[0]
    tr = math.gcd(n, tr)
    cls3 = cls_flat.reshape(n // tr, 1, tr)
    rank, cnt = pl.pallas_call(
        functools.partial(_rank_kernel, tr=tr),
        grid=(n // tr,),
        in_specs=[pl.BlockSpec((1, 1, tr), lambda i: (i, 0, 0))],
        out_specs=[pl.BlockSpec((1, 1, tr), lambda i: (i, 0, 0)), _const_spec((CLASS_ROWS, LANES))],
        out_shape=[jax.ShapeDtypeStruct((n // tr, 1, tr), F32), jax.ShapeDtypeStruct((CLASS_ROWS, LANES), F32)],
        scratch_shapes=[pltpu.VMEM((CLASS_ROWS, LANES), F32)],
        compiler_params=_cparams(("arbitrary",), 32),
        name="rank",
    )(cls3)
    return rank.reshape(n), cnt[:N_CLASSES, 0]


def _sc_layout(n):
    info = plsc.get_sparse_core_info()
    nw = info.num_cores * info.num_subcores
    per_worker = n // nw
    assert per_worker * nw == n
    chunk = max(c for c in range(8, SC_MAX_CHUNK + 1, 8) if per_worker % c == 0)
    return info.num_cores, nw, per_worker // chunk, chunk


def _sc_scatter_rows(src, wts, pos, n_out):
    n, w = src.shape
    nc, nw, k, c = _sc_layout(n)
    mesh = plsc.VectorSubcoreMesh(core_axis_name="c", subcore_axis_name="s")

    @functools.partial(
        pl.kernel, mesh=mesh,
        out_type=(jax.ShapeDtypeStruct((n_out, w), src.dtype), jax.ShapeDtypeStruct((n_out, wts.shape[1]), wts.dtype)),
        scratch_types=[pltpu.VMEM((k, c), jnp.int32), pltpu.VMEM((c, w), src.dtype),
                       pltpu.VMEM((c, wts.shape[1]), wts.dtype), pltpu.SemaphoreType.DMA],
        name="moe_scatter",
    )
    def scatter(src_hbm, wts_hbm, pos_hbm, out_hbm, wout_hbm, idx_v, rows_v, wrows_v, sem):
        wid = lax.axis_index("s") * nc + lax.axis_index("c")
        pltpu.sync_copy(pos_hbm.at[wid], idx_v)

        @pl.loop(0, k)
        def _(j):
            off = pl.multiple_of(wid * (k * c) + j * c, 8)
            pltpu.sync_copy(src_hbm.at[pl.ds(off, c)], rows_v)
            pltpu.sync_copy(wts_hbm.at[pl.ds(off, c)], wrows_v)
            pltpu.async_copy(rows_v, out_hbm.at[idx_v.at[j]], sem).wait()
            pltpu.async_copy(wrows_v, wout_hbm.at[idx_v.at[j]], sem).wait()

    return scatter(src, wts, pos.reshape(nw, k, c))


def _sc_gather_rows(src, pos):
    n = pos.shape[0]
    w = src.shape[1]
    nc, nw, k, c = _sc_layout(n)
    mesh = plsc.VectorSubcoreMesh(core_axis_name="c", subcore_axis_name="s")

    @functools.partial(
        pl.kernel, mesh=mesh,
        out_type=jax.ShapeDtypeStruct((n, w), src.dtype),
        scratch_types=[pltpu.VMEM((k, c), jnp.int32), pltpu.VMEM((c, w), src.dtype), pltpu.SemaphoreType.DMA],
        name="moe_gather",
    )
    def gather(src_hbm, pos_hbm, out_hbm, idx_v, rows_v, sem):
        wid = lax.axis_index("s") * nc + lax.axis_index("c")
        pltpu.sync_copy(pos_hbm.at[wid], idx_v)

        @pl.loop(0, k)
        def _(j):
            off = pl.multiple_of(wid * (k * c) + j * c, 8)
            pltpu.async_copy(src_hbm.at[idx_v.at[j]], rows_v, sem).wait()
            pltpu.sync_copy(rows_v, out_hbm.at[pl.ds(off, c)])

    return gather(src, pos.reshape(nw, k, c))


def _gmm_kernel(lo_ref, hi_ref, new_ref, nact_ref, h_ref, wt_ref,
                w1a_ref, w1b_ref, w3a_ref, w3b_ref, w2a_ref, w2b_ref, o_ref, w13_s, w2_s):
    t = pl.program_id(0)

    @pl.when(new_ref[t] == 1)
    def _():
        for i, (w1_ref, w3_ref, w2_ref) in enumerate(((w1a_ref, w3a_ref, w2a_ref), (w1b_ref, w3b_ref, w2b_ref))):
            w13_s[i, :, :EXPERT_FF] = w1_ref[0, 0].astype(BF16)
            w13_s[i, :, EXPERT_FF:] = w3_ref[0, 0].astype(BF16)
            w2_s[i] = w2_ref[0, 0].astype(BF16)

    @pl.when(t < nact_ref[0])
    def _():
        x = _unpack_bf16_pairs(h_ref[...]).astype(BF16)
        wts = wt_ref[...]

        def expert(i):
            ab = jnp.dot(x, w13_s[i], preferred_element_type=F32)
            a, gate = ab[:, :EXPERT_FF], ab[:, EXPERT_FF:]
            hid = (a * jax.nn.sigmoid(a)) * gate
            return jnp.dot(hid.astype(BF16), w2_s[i], preferred_element_type=F32)

        y = wts[:, 0:1] * expert(0) + wts[:, 1:2] * expert(1)
        o_ref[...] = _pack_bf16_pairs(y)


def _gmm(tile_lo, tile_hi, tile_new, n_act, hs, ws, w1, w3, w2, layer, tm):
    n_pad, half = hs.shape
    d = 2 * half
    row = lambda t, lo, hi, new, na: (jnp.minimum(t, na[0] - 1), 0)
    e_lo = lambda t, lo, hi, new, na: (layer, lo[jnp.minimum(t, na[0] - 1)], 0, 0)
    e_hi = lambda t, lo, hi, new, na: (layer, hi[jnp.minimum(t, na[0] - 1)], 0, 0)
    up = lambda e: pl.BlockSpec((1, 1, d, EXPERT_FF), e)
    down = lambda e: pl.BlockSpec((1, 1, EXPERT_FF, d), e)
    return pl.pallas_call(
        _gmm_kernel,
        grid_spec=pltpu.PrefetchScalarGridSpec(
            num_scalar_prefetch=4,
            grid=(n_pad // tm,),
            in_specs=[
                pl.BlockSpec((tm, half), row),
                pl.BlockSpec((tm, ws.shape[1]), row),
                up(e_lo), up(e_hi), up(e_lo), up(e_hi), down(e_lo), down(e_hi),
            ],
            out_specs=pl.BlockSpec((tm, half), row),
            scratch_shapes=[pltpu.VMEM((2, d, 2 * EXPERT_FF), BF16), pltpu.VMEM((2, EXPERT_FF, d), BF16)],
        ),
        out_shape=jax.ShapeDtypeStruct((n_pad, half), jnp.int32),
        compiler_params=_cparams(("arbitrary",), 56),
        name="moe_gmm",
    )(tile_lo, tile_hi, tile_new, n_act, hs, ws, w1, w1, w3, w3, w2, w2)


def _moe_routed(h2p, route, w1, w3, w2, layer, tm=256):
    b, s, half = h2p.shape
    n = b * s
    n_pad = n + N_CLASSES * tm
    cls = route[:, 0, :].reshape(n)
    rank, counts = _rank(cls)
    counts = counts.astype(jnp.int32)
    padded = (counts + tm - 1) // tm * tm
    ends = jnp.cumsum(padded)
    pos = jnp.take(ends - padded, cls.astype(jnp.int32)) + rank.astype(jnp.int32)
    n_act = (ends[-1] // tm).reshape(1)
    tile_row = jnp.arange(n_pad // tm, dtype=jnp.int32) * tm
    tile_cls = jnp.minimum(jnp.sum(tile_row[:, None] >= ends[None, :], axis=1), N_CLASSES - 1)
    pair_lo, pair_hi = (jnp.asarray(a, jnp.int32) for a in _class_experts())
    wts = jnp.concatenate([route[:, 1, :].reshape(n, 1), route[:, 2, :].reshape(n, 1),
                           jnp.zeros((n, SC_WEIGHT_COLS - 2), F32)], axis=1)
    hs, ws = _sc_scatter_rows(h2p.reshape(n, half), wts, pos, n_pad)
    prev_cls = jnp.concatenate([jnp.full((1,), -1, tile_cls.dtype), tile_cls[:-1]])
    tile_new = ((tile_cls != prev_cls) & (tile_row < ends[-1])).astype(jnp.int32)
    ys = _gmm(jnp.take(pair_lo, tile_cls), jnp.take(pair_hi, tile_cls), tile_new, n_act, hs, ws,
              w1, w3, w2, layer, tm)
    return _sc_gather_rows(ys, pos).reshape(b, s, half)


def _class_experts():
    lo, hi = [], []
    for g in range(N_EXPERTS // EXPERTS_PER_GROUP):
        for i in range(EXPERTS_PER_GROUP):
            for j in range(i + 1, EXPERTS_PER_GROUP):
                lo.append(EXPERTS_PER_GROUP * g + i)
                hi.append(EXPERTS_PER_GROUP * g + j)
    return np.array(lo), np.array(hi)


def _final_residual_kernel(x_ref, y_ref, mod_ref, g_ref, o_ref):
    x = _pending_residual(x_ref[0], y_ref, mod_ref)
    o_ref[0] = x * lax.rsqrt(jnp.mean(x * x, axis=-1, keepdims=True) + EPS) * g_ref[...]


def _final_residual(x1, yp, mod, gain, tm=512):
    b, t, d = x1.shape
    tok = lambda bi, j: (bi, j, 0)
    return pl.pallas_call(
        _final_residual_kernel,
        grid=(b, t // tm),
        in_specs=[
            pl.BlockSpec((1, tm, d), tok),
            pl.BlockSpec((1, tm, d // 2), tok),
            pl.BlockSpec((1, 1, 6 * d), lambda bi, j: (bi, 0, 0)),
            _const_spec((1, d)),
        ],
        out_specs=pl.BlockSpec((1, tm, d), tok),
        out_shape=jax.ShapeDtypeStruct((b, t, d), F32),
        compiler_params=_cparams(("parallel", "arbitrary"), 32),
        name="final_residual",
    )(x1, yp, mod, gain)


def _rope_tables(t_lat, n_ctx):
    rows = t_lat // GRID_W
    row = jnp.repeat(jnp.arange(rows, dtype=F32), GRID_W)
    col = jnp.tile(jnp.arange(GRID_W, dtype=F32), rows)
    inv_freq = ROPE_THETA ** (-jnp.arange(0, ROPE_AXIS_DIM, 2, dtype=F32) / ROPE_AXIS_DIM)
    ang = jnp.stack([row[:, None] * inv_freq, col[:, None] * inv_freq], axis=1)
    cos, sin = jnp.cos(ang), jnp.sin(ang)
    zero = jnp.zeros_like(sin)
    cos_h = jnp.concatenate([cos, cos], axis=-1).reshape(t_lat, HEAD_DIM)
    sa_h = jnp.concatenate([-sin, zero], axis=-1).reshape(t_lat, HEAD_DIM)
    sb_h = jnp.concatenate([zero, sin], axis=-1).reshape(t_lat, HEAD_DIM)
    reps = LANES // HEAD_DIM
    pad = lambda tbl, fill: jnp.concatenate(
        [jnp.tile(tbl, (1, reps)), jnp.full((n_ctx, LANES), fill, F32)], axis=0)
    return pad(cos_h, 1.0), pad(sa_h, 0.0), pad(sb_h, 0.0)


def _permute_heads(w, axis):
    shp = w.shape
    w = w.reshape(shp[:axis] + (N_Q_HEADS, HEAD_DIM) + shp[axis + 1:])
    w = jnp.take(w, jnp.array(HEAD_PERM), axis=axis)
    return w.reshape(shp)


def _square_factor(n):
    r = int(round(math.sqrt(n)))
    assert r * r == n, "sequence lengths must be perfect squares for the two-stage DFT"
    return r


def kernel(x, c, ctx, c_ctx, w_ada, b_ada, norm1, norm2, w_in, q_gain, k_gain, sink, pool_w, pool_scale,
           w_branch, w_gate, b_gate, w_out, router_w, router_bias, w1, w3, w2, norm_f):
    b, t_lat, d = x.shape
    n_ctx = ctx.shape[1]
    s = t_lat + n_ctx
    depth = w_ada.shape[0]
    assert d == D_MODEL and b < MOD_ROWS and t_lat % 256 == 0 and n_ctx % 256 == 0 and t_lat % n_ctx == 0

    tokens = (x, ctx)
    cc = jnp.zeros((MOD_ROWS, d), F32).at[:b].set(c).at[b].set(c_ctx)
    mod_all = _ada(cc, w_ada, b_ada).reshape(depth, MOD_ROWS, 1, 6 * d)

    cos, sa, sb = _rope_tables(t_lat, n_ctx)
    seg = jnp.asarray(np.kron(np.eye(N_Q_HEADS), np.full((HEAD_DIM, HEAD_DIM), 1.0 / HEAD_DIM)), BF16)
    cs = jnp.asarray(_channel_dft_table()).astype(BF16)
    f_lat = [jnp.asarray(a).astype(BF16) for a in _fourier_tables(*(_square_factor(t_lat),) * 2)]
    f_ctx = [jnp.asarray(a).astype(BF16) for a in _fourier_tables(*(_square_factor(n_ctx),) * 2)]
    wbias = jnp.asarray(_window_bias(n_ctx))
    rw_hi = router_w.T.astype(BF16)
    rw_t = jnp.concatenate([rw_hi, (router_w.T - rw_hi.astype(F32)).astype(BF16)], axis=0)
    rb = router_bias.reshape(N_EXPERTS, 1)

    for l in range(depth):
        need_ctx = l < depth - 1
        s_out = s if need_ctx else t_lat
        cols = jnp.split(w_in[l], np.cumsum((512, 512, 512, 128, 128, 512, 128))[:], axis=1)
        f_w, p_w, qb_w, kb_w, vb_w, qw_w, kw_w, vw_w = cols
        w_in_l = jnp.concatenate([f_w, p_w, _permute_heads(qb_w, 1), _permute_heads(qw_w, 1),
                                  kb_w, vb_w, kw_w, vw_w], axis=1).astype(BF16)
        wb_l = jnp.stack([w_branch[l, 0], _permute_heads(w_branch[l, 1], 0),
                          _permute_heads(w_branch[l, 2], 0), w_branch[l, 3]]).astype(BF16)
        mod = mod_all[l]
        n1 = norm1[l].reshape(1, d)
        n2 = norm2[l].reshape(1, d)
        qg = jnp.tile(q_gain[l], N_Q_HEADS).reshape(1, BRANCH_W)
        kg = jnp.tile(k_gain[l], LANES // HEAD_DIM).reshape(1, LANES)

        zr, zi, p_in, qb, qw, kvb, kvw = _inproj(tokens, t_lat, mod, n1, w_in_l, qg, kg, seg, cs, cos, sa, sb)

        out_b = _gattn(qb, kvb, 0, t_lat, 0, s, GATTN_KEY_TILES)
        out_bc = _gattn(qb, kvb, t_lat, n_ctx, t_lat, n_ctx, 1) if need_ctx else out_b
        out_c = _wattn(jnp.take(sink[l], jnp.array(HEAD_PERM)) * LOG2E, wbias, qw, kvw, s_out, t_lat)
        out_a = _fourier(zr, zi, f_lat[0], f_lat[1], t_lat, 0)
        out_ac = _fourier(zr, zi, f_ctx[0], f_ctx[1], n_ctx, t_lat // n_ctx) if need_ctx else out_a

        weights = (w_gate[l].astype(BF16), b_gate[l].reshape(4, 1, d), wb_l, w_out[l].astype(BF16), rw_t, rb)
        pool_params = (pool_w[l].astype(BF16), pool_scale[l].reshape(1, BRANCH_W))
        pending = len(tokens) == 3
        merged = _merge(tokens if pending else tokens[:1], mod, None, n1, n2, (out_a, out_b, out_c), (0, 0, 0),
                        p_in, pool_params, (0, t_lat), weights, s_out, 0, t_lat, 2 * MERGE_CHAIN_ROWS)
        if need_ctx:
            merged = _merge(tokens if pending else tokens[1:], mod, b, n1, n2, (out_ac, out_bc, out_c),
                            (0, 0, t_lat), p_in, pool_params, (t_lat, s), weights, s_out, t_lat, n_ctx,
                            MERGE_CHAIN_ROWS, prev=merged)
        x1, h2p, route = merged
        yp = _moe_routed(h2p, route, w1, w3, w2, l)
        tokens = (x1, yp, mod)

    return _final_residual(*tokens, norm_f.reshape(1, d))
```

```python
import functools
import math

import numpy as np
import jax
import jax.numpy as jnp
from jax import lax
from jax.experimental import pallas as pl
from jax.experimental.pallas import tpu as pltpu
from jax.experimental.pallas import tpu_sc as plsc

F32 = jnp.float32
BF16 = jnp.bfloat16

D_MODEL = 1024
HEAD_DIM = 64
N_Q_HEADS = 8
N_KV_HEADS = 2
GRID_W = 64
ROPE_THETA = 10000.0
ROPE_AXIS_DIM = HEAD_DIM // 2
QBLK = 128
WINDOW = 128
BRANCH_W = 512
GROUP_W = 128
POOL_WINDOWS = (2, 4, 8, 16)
N_EXPERTS = 16
EXPERTS_PER_GROUP = 4
EXPERT_FF = 512
EPS = 1e-6
MOD_ROWS = 16
NEG_BIG = -1e30
LOG2E = math.log2(math.e)
LANES = 128
POOL_HALO = 16
PAIRS_PER_GROUP = 6
N_CLASSES = 24
CLASS_ROWS = 32
ROUTE_ROWS = 8
SC_MAX_CHUNK = 128
SC_WEIGHT_COLS = 128
WATTN_QBLOCKS = 2
MERGE_CHAIN_ROWS = 256

HEAD_PERM = (0, 4, 1, 5, 2, 6, 3, 7)


def _cparams(sem, vmem_mb):
    return pltpu.CompilerParams(dimension_semantics=sem, vmem_limit_bytes=vmem_mb * 1024 * 1024)


def _const_spec(shape):
    nd = len(shape)
    return pl.BlockSpec(shape, lambda *_: (0,) * nd)


def _ada_kernel(c_ref, w_ref, b_ref, o_ref):
    c = c_ref[...]
    s = c * jax.nn.sigmoid(c)
    o_ref[0] = jnp.dot(s.astype(BF16), w_ref[0].astype(BF16), preferred_element_type=F32) + b_ref[0]


def _ada(cc, w_ada, b_ada):
    depth, d, n = w_ada.shape
    tn = 1536
    return pl.pallas_call(
        _ada_kernel,
        grid=(depth, n // tn),
        in_specs=[
            pl.BlockSpec((MOD_ROWS, d), lambda l, j: (0, 0)),
            pl.BlockSpec((1, d, tn), lambda l, j: (l, 0, j)),
            pl.BlockSpec((1, 1, tn), lambda l, j: (l, 0, j)),
        ],
        out_specs=pl.BlockSpec((1, MOD_ROWS, tn), lambda l, j: (l, 0, j)),
        out_shape=jax.ShapeDtypeStruct((depth, MOD_ROWS, n), F32),
        compiler_params=_cparams(("arbitrary", "arbitrary"), 40),
        name="ada",
    )(cc, w_ada, b_ada.reshape(depth, 1, n))


def _norm_mod(x, gain, shift, scale):
    ms = jnp.mean(x * x, axis=-1, keepdims=True)
    return (x * lax.rsqrt(ms + EPS) * gain) * (1.0 + scale) + shift


def _mod_slices(m):
    d = D_MODEL
    return [m[:, i * d:(i + 1) * d] for i in range(6)]


def _head_norm(z, seg, gain):
    ms = jnp.dot((z * z).astype(BF16), seg, preferred_element_type=F32)
    return z * lax.rsqrt(ms + EPS) * gain


def _rope(z, cos, sin_a, sin_b):
    outs = []
    for c in range(z.shape[1] // LANES):
        zc = z[:, c * LANES:(c + 1) * LANES]
        nxt = pltpu.roll(zc, LANES - ROPE_AXIS_DIM // 2, 1)
        prv = pltpu.roll(zc, ROPE_AXIS_DIM // 2, 1)
        outs.append(zc * cos + nxt * sin_a + prv * sin_b)
    return outs[0] if len(outs) == 1 else jnp.concatenate(outs, axis=-1)


def _stream_specs(tm, d, n_lat):
    return [pl.BlockSpec((1, tm, d), lambda bi, j: (bi, jnp.minimum(j, n_lat - 1), 0)),
            pl.BlockSpec((1, tm, d), lambda bi, j: (bi, jnp.maximum(j - n_lat, 0), 0))]


def _pending_residual(x, y_ref, modp_ref, rows=slice(None)):
    return x + _mod_slices(modp_ref[0])[5] * _unpack_bf16_pairs(y_ref[0, rows, :])


def _inproj_kernel(*refs, n_lat, pending):
    if pending:
        x_ref, y_ref, modp_ref = refs[:3]
        x = _pending_residual(x_ref[0], y_ref, modp_ref)
    else:
        xl_ref, xc_ref = refs[:2]
        x = jnp.where(pl.program_id(1) >= n_lat, xc_ref[0], xl_ref[0])
    (mod_ref, n1_ref, w_ref, qg_ref, kg_ref, seg_ref, cs_ref, cos_ref, sa_ref, sb_ref,
     zr_ref, zi_ref, p_ref, qb_ref, qw_ref, kvb_ref, kvw_ref) = refs[3 if pending else 2:]
    sh1, sc1 = _mod_slices(mod_ref[0])[:2]
    h = _norm_mod(x, n1_ref[...], sh1, sc1)
    u = jnp.dot(h.astype(BF16), w_ref[...], preferred_element_type=F32)
    cos, sa, sb = cos_ref[...], sa_ref[...], sb_ref[...]
    w = BRANCH_W
    f_in = u[:, 0:w].astype(BF16)
    zr, zi = [], []
    for g in range(w // GROUP_W):
        z = jnp.dot(f_in[:, g * GROUP_W:(g + 1) * GROUP_W], cs_ref[...], preferred_element_type=F32)
        zr.append(z[:, :GROUP_W])
        zi.append(z[:, GROUP_W:])
    zr_ref[0] = jnp.concatenate(zr, axis=-1).astype(BF16)
    zi_ref[0] = jnp.concatenate(zi, axis=-1).astype(BF16)
    p_ref[0] = u[:, w:2 * w].astype(BF16)
    seg = seg_ref[...]
    qb = _rope(_head_norm(u[:, 2 * w:3 * w], seg, qg_ref[...]), cos, sa, sb)
    qb_ref[0] = (qb * (HEAD_DIM ** -0.5 * LOG2E)).astype(BF16)
    qw = _rope(u[:, 3 * w:4 * w], cos, sa, sb)
    qw_ref[0] = (qw * (HEAD_DIM ** -0.5 * LOG2E)).astype(BF16)
    o = 4 * w
    kb = _rope(_head_norm(u[:, o:o + LANES], seg[:LANES, :LANES], kg_ref[...]), cos, sa, sb)
    vb = u[:, o + LANES:o + 2 * LANES]
    kw = _rope(u[:, o + 2 * LANES:o + 3 * LANES], cos, sa, sb)
    vw = u[:, o + 3 * LANES:o + 4 * LANES]
    kvb_ref[0] = jnp.concatenate([kb, vb, jnp.ones_like(vb)], axis=-1).astype(BF16)
    kvw_ref[0] = jnp.concatenate([kw, vw, jnp.ones_like(vw)], axis=-1).astype(BF16)


def _inproj(tokens, t_lat, mod, n1, w_in, qg, kg, seg, cs, cos, sa, sb, tm=256):
    pending = len(tokens) == 3
    b, _, d = tokens[0].shape
    s = tokens[0].shape[1] if pending else t_lat + tokens[1].shape[1]
    nw = w_in.shape[1]
    n_lat = t_lat // tm
    tok = lambda bi, j: (bi, j, 0)
    tab = lambda bi, j: (j, 0)
    mod_spec = pl.BlockSpec((1, 1, 6 * d), lambda bi, j: (jnp.where(j >= n_lat, b, bi), 0, 0))
    if pending:
        token_specs = [pl.BlockSpec((1, tm, d), tok), pl.BlockSpec((1, tm, d // 2), tok), mod_spec]
    else:
        token_specs = _stream_specs(tm, d, n_lat)
    widths = (BRANCH_W,) * 5 + (3 * LANES, 3 * LANES)
    return pl.pallas_call(
        functools.partial(_inproj_kernel, n_lat=n_lat, pending=pending),
        grid=(b, s // tm),
        in_specs=token_specs + [
            mod_spec,
            _const_spec((1, d)),
            _const_spec((d, nw)),
            _const_spec((1, BRANCH_W)),
            _const_spec((1, LANES)),
            _const_spec((BRANCH_W, BRANCH_W)),
            _const_spec((GROUP_W, 2 * GROUP_W)),
            pl.BlockSpec((tm, LANES), tab),
            pl.BlockSpec((tm, LANES), tab),
            pl.BlockSpec((tm, LANES), tab),
        ],
        out_specs=[pl.BlockSpec((1, tm, wd), tok) for wd in widths],
        out_shape=[jax.ShapeDtypeStruct((b, s, wd), BF16) for wd in widths],
        compiler_params=_cparams(("parallel", "arbitrary"), 48),
        name="inproj",
    )(*tokens, mod, n1, w_in, qg, kg, seg, cs, cos, sa, sb)


def _split_heads(qc, lane):
    zero = jnp.zeros_like(qc)
    return jnp.concatenate([jnp.where(lane < HEAD_DIM, qc, zero),
                            jnp.where(lane >= HEAD_DIM, qc, zero)], axis=0)


def _gattn_kernel(q_ref, kv_ref, o_ref, q2_s, acc_s, *, sub):
    tq = q_ref.shape[1]
    lane = lax.broadcasted_iota(jnp.int32, (1, LANES), 1)
    nt = (((1,), (1,)), ((), ()))
    n_chunks = BRANCH_W // LANES
    for c in range(n_chunks):
        q2_s[2 * c * tq:(2 * c + 2) * tq, :] = _split_heads(q_ref[0, :, c * LANES:(c + 1) * LANES], lane)
    k = kv_ref[0, :, 0:LANES]
    v = kv_ref[0, :, LANES:3 * LANES]
    for r in range(2 * n_chunks * tq // sub):
        rows = slice(r * sub, (r + 1) * sub)
        s = lax.dot_general(q2_s[rows, :], k, nt, preferred_element_type=F32)
        p = jnp.exp2(s - jnp.max(s, axis=-1, keepdims=True))
        acc_s[rows, :] = jnp.dot(p.astype(BF16), v, preferred_element_type=F32)
    for c in range(n_chunks):
        lo = acc_s[2 * c * tq:(2 * c + 1) * tq, :]
        hi = acc_s[(2 * c + 1) * tq:(2 * c + 2) * tq, :]
        o = jnp.where(lane < HEAD_DIM, lo[:, :LANES] / lo[:, LANES:], hi[:, :LANES] / hi[:, LANES:])
        o_ref[0, :, c * LANES:(c + 1) * LANES] = o.astype(BF16)


def _gattn(qb, kv, q_start, q_len, k_start, k_len, tq=256, sub=128):
    b = qb.shape[0]
    assert q_start % tq == 0 and q_len % tq == 0 and k_start % k_len == 0 and k_len % LANES == 0
    rows = 2 * tq * (BRANCH_W // LANES)
    return pl.pallas_call(
        functools.partial(_gattn_kernel, sub=sub),
        grid=(b, q_len // tq),
        in_specs=[
            pl.BlockSpec((1, tq, BRANCH_W), lambda bi, j: (bi, q_start // tq + j, 0)),
            pl.BlockSpec((1, k_len, 3 * LANES), lambda bi, j: (bi, k_start // k_len, 0)),
        ],
        out_specs=pl.BlockSpec((1, tq, BRANCH_W), lambda bi, j: (bi, j, 0)),
        out_shape=jax.ShapeDtypeStruct((b, q_len, BRANCH_W), BF16),
        scratch_shapes=[pltpu.VMEM((rows, LANES), BF16), pltpu.VMEM((rows, 2 * LANES), F32)],
        compiler_params=_cparams(("parallel", "arbitrary"), 48),
        name="gattn",
    )(qb, kv)


def _window_bias(n_ctx):
    tq = WATTN_QBLOCKS * QBLK
    qi = np.arange(tq)[:, None]
    kj = np.arange(tq + 2 * QBLK)[None, :]
    band = np.abs(kj - WINDOW - qi) <= WINDOW
    blk = kj // QBLK
    variants = [band & (blk != 0), band, band & (blk != WATTN_QBLOCKS + 1), np.zeros_like(band)]
    out = [np.concatenate([np.ones((tq, n_ctx), bool), v], axis=1) for v in variants]
    return np.where(np.stack(out), 0.0, NEG_BIG).astype(np.float32)


def _wattn_kernel(sink_ref, bias_ref, q_ref, *refs, sub):
    o_ref, q2_s, kv_s, acc_s = refs[-4:]
    tq = q_ref.shape[1]
    off = 0
    for blk in refs[:-4]:
        kv_s[off:off + blk.shape[1], :] = blk[0]
        off += blk.shape[1]
    lane = lax.broadcasted_iota(jnp.int32, (1, LANES), 1)
    nt = (((1,), (1,)), ((), ()))
    n_chunks = BRANCH_W // LANES
    for c in range(n_chunks):
        q2_s[2 * c * tq:(2 * c + 2) * tq, :] = _split_heads(q_ref[0, :, c * LANES:(c + 1) * LANES], lane)
    k, v = kv_s[:, 0:LANES], kv_s[:, LANES:3 * LANES]
    for r in range(2 * n_chunks * tq // sub):
        rows = slice(r * sub, (r + 1) * sub)
        q_off = (r * sub) % tq
        sk = sink_ref[(r * sub) // tq]
        s = lax.dot_general(q2_s[rows, :], k, nt, preferred_element_type=F32) + bias_ref[0, q_off:q_off + sub, :]
        m = jnp.maximum(jnp.max(s, axis=-1, keepdims=True), sk)
        pv = jnp.dot(jnp.exp2(s - m).astype(BF16), v, preferred_element_type=F32)
        acc_s[rows, :LANES] = pv[:, :LANES]
        acc_s[rows, LANES:] = pv[:, LANES:] + jnp.exp2(sk - m)
    for c in range(n_chunks):
        lo = acc_s[2 * c * tq:(2 * c + 1) * tq, :]
        hi = acc_s[(2 * c + 1) * tq:(2 * c + 2) * tq, :]
        o = jnp.where(lane < HEAD_DIM, lo[:, :LANES] / lo[:, LANES:], hi[:, :LANES] / hi[:, LANES:])
        o_ref[0, :, c * LANES:(c + 1) * LANES] = o.astype(BF16)


def _wattn(sink, bias, qw, kv, s_out, t_lat):
    b, s, _ = qw.shape
    n_ctx = s - t_lat
    nq = WATTN_QBLOCKS
    tq = nq * QBLK
    assert n_ctx % tq == 0 and t_lat // tq >= 2
    last = s // QBLK - 1
    n_lat = t_lat // tq
    variant = lambda j: jnp.where(j >= n_lat, 3, jnp.where(j == 0, 0, jnp.where(j == n_lat - 1, 2, 1)))
    key_block = lambda off: pl.BlockSpec(
        (1, QBLK, 3 * LANES), lambda bi, j: (bi, jnp.clip(j * nq + off, 0, last), 0))
    rows = 2 * tq * (BRANCH_W // LANES)
    return pl.pallas_call(
        functools.partial(_wattn_kernel, sub=128),
        scratch_shapes=[pltpu.VMEM((rows, LANES), BF16), pltpu.VMEM((bias.shape[2], 3 * LANES), BF16),
                        pltpu.VMEM((rows, 2 * LANES), F32)],
        grid=(b, s_out // tq),
        in_specs=[
            pl.BlockSpec(memory_space=pltpu.SMEM),
            pl.BlockSpec((1,) + bias.shape[1:], lambda bi, j: (variant(j), 0, 0)),
            pl.BlockSpec((1, tq, BRANCH_W), lambda bi, j: (bi, j, 0)),
            pl.BlockSpec((1, n_ctx, 3 * LANES), lambda bi, j: (bi, t_lat // n_ctx, 0)),
        ] + [key_block(off) for off in range(-1, nq + 1)],
        out_specs=pl.BlockSpec((1, tq, BRANCH_W), lambda bi, j: (bi, j, 0)),
        out_shape=jax.ShapeDtypeStruct((b, s_out, BRANCH_W), BF16),
        compiler_params=_cparams(("parallel", "arbitrary"), 32),
        name="wattn",
    )(sink, bias, qw, *([kv] * (nq + 3)))


def _fourier_tables(n1, n2):
    t = n1 * n2
    k2 = np.arange(n2)[None, :, None]
    t2 = np.arange(n2)[None, None, :]
    t1 = np.arange(n1)[:, None, None]
    theta = 2.0 * np.pi * ((k2 * t2 * n1 + k2 * t1) % t) / t
    er, ei = np.cos(theta) / math.sqrt(n2), -np.sin(theta) / math.sqrt(n2)
    e = np.concatenate([np.concatenate([er, -ei], axis=2), np.concatenate([ei, er], axis=2)], axis=1)
    k1 = np.arange(n1)[:, None]
    phi = 2.0 * np.pi * ((k1 * np.arange(n1)[None, :]) % n1) / n1
    dcat = np.concatenate([np.cos(phi), np.sin(phi)], axis=1) / math.sqrt(n1)
    return e.astype(np.float32), dcat.astype(np.float32)


def _channel_dft_table():
    c = np.arange(GROUP_W)
    ang = 2.0 * np.pi * ((c[:, None] * c[None, :]) % GROUP_W) / GROUP_W
    return (np.concatenate([np.cos(ang), -np.sin(ang)], axis=1) / math.sqrt(GROUP_W)).astype(np.float32)


def _fourier_kernel(zr_ref, zi_ref, e_ref, d_ref, o_ref, xr_s, xi_s, yr_s, yi_s, *, n1, n2):
    nc = xr_s.shape[0]
    chunk = lambda c: slice(c * LANES, (c + 1) * LANES)

    def gather(ref, start, size, stride):
        return jnp.concatenate([ref[c, pl.ds(start, size, stride=stride), :] for c in range(nc)], axis=-1)

    for c in range(nc):
        xr_s[c] = zr_ref[0, :, chunk(c)].astype(F32)
        xi_s[c] = zi_ref[0, :, chunk(c)].astype(F32)
    for t1 in range(n1):
        xs = jnp.concatenate([gather(xr_s, t1, n2, n1), gather(xi_s, t1, n2, n1)], axis=0).astype(BF16)
        y = jnp.dot(e_ref[t1], xs, preferred_element_type=F32)
        for c in range(nc):
            yr_s[c, t1 * n2:(t1 + 1) * n2, :] = y[:n2, chunk(c)]
            yi_s[c, t1 * n2:(t1 + 1) * n2, :] = y[n2:, chunk(c)]
    for k2 in range(n2):
        ys = jnp.concatenate([gather(yr_s, k2, n1, n2), gather(yi_s, k2, n1, n2)], axis=0).astype(BF16)
        o = jnp.dot(d_ref[...], ys, preferred_element_type=F32)
        for c in range(nc):
            xr_s[c, pl.ds(k2, n1, stride=n2), :] = o[:, chunk(c)]
    for c in range(nc):
        o_ref[0, :, chunk(c)] = xr_s[c].astype(BF16)


def _fourier(zr, zi, e_tab, d_tab, t_len, row_block, cw=256):
    b = zr.shape[0]
    n1 = d_tab.shape[0]
    n2 = t_len // n1
    zspec = pl.BlockSpec((1, t_len, cw), lambda bi, j: (bi, row_block, j))
    return pl.pallas_call(
        functools.partial(_fourier_kernel, n1=n1, n2=n2),
        grid=(b, BRANCH_W // cw),
        in_specs=[zspec, zspec, _const_spec(e_tab.shape), _const_spec(d_tab.shape)],
        out_specs=pl.BlockSpec((1, t_len, cw), lambda bi, j: (bi, 0, j)),
        out_shape=jax.ShapeDtypeStruct((b, t_len, BRANCH_W), BF16),
        scratch_shapes=[pltpu.VMEM((cw // LANES, t_len, LANES), F32)] * 4,
        compiler_params=_cparams(("parallel", "arbitrary"), 48),
        name="fourier",
    )(zr, zi, e_tab, d_tab)


def _pool_rows(ext, pos, n, w_ref, scale):
    n_ext = ext.shape[0]
    rows = n_ext - 2 * POOL_HALO
    back = lambda v, k: pltpu.roll(v, k, 0)
    fwd = lambda v, k: pltpu.roll(v, n_ext - k, 0)
    outs = []
    for gi, w in enumerate(POOL_WINDOWS):
        e = ext[:, gi * GROUP_W:(gi + 1) * GROUP_W]
        wsum = e + back(e, 1)
        half = 1
        while 2 * half < w:
            wsum = back(wsum, half) + fwd(wsum, half)
            half *= 2
        own = slice(POOL_HALO, POOL_HALO + rows)
        cnt = jnp.minimum(pos + w // 2, n) - jnp.maximum(pos - w // 2, 0)
        pooled = wsum[own] / cnt.astype(F32) - e[own]
        outs.append(jnp.dot(pooled.astype(BF16), w_ref[gi], preferred_element_type=F32))
    return (jnp.concatenate(outs, axis=-1) * scale).astype(BF16)


def _route(logits_t, bias):
    aff = jax.nn.sigmoid(logits_t)
    sel = aff + bias
    neg = -jnp.inf
    firsts, seconds, scores = [], [], []
    for g in range(N_EXPERTS // EXPERTS_PER_GROUP):
        s = [sel[EXPERTS_PER_GROUP * g + k:EXPERTS_PER_GROUP * g + k + 1, :] for k in range(EXPERTS_PER_GROUP)]
        m1 = jnp.maximum(jnp.maximum(s[0], s[1]), jnp.maximum(s[2], s[3]))
        i1 = jnp.where(s[0] == m1, 0, jnp.where(s[1] == m1, 1, jnp.where(s[2] == m1, 2, 3)))
        r = [jnp.where(i1 == k, neg, s[k]) for k in range(EXPERTS_PER_GROUP)]
        m2 = jnp.maximum(jnp.maximum(r[0], r[1]), jnp.maximum(r[2], r[3]))
        i2 = jnp.where(r[0] == m2, 0, jnp.where(r[1] == m2, 1, jnp.where(r[2] == m2, 2, 3)))
        firsts.append(i1 + EXPERTS_PER_GROUP * g)
        seconds.append(i2 + EXPERTS_PER_GROUP * g)
        scores.append(m1 + m2)
    best = jnp.maximum(jnp.maximum(scores[0], scores[1]), jnp.maximum(scores[2], scores[3]))
    pick = lambda v: jnp.where(scores[0] == best, v[0], jnp.where(scores[1] == best, v[1],
                                                                 jnp.where(scores[2] == best, v[2], v[3])))
    e1, e2 = pick(firsts), pick(seconds)
    eidx = lax.broadcasted_iota(jnp.int32, aff.shape, 0)
    a1 = jnp.sum(jnp.where(eidx == e1, aff, 0.0), axis=0, keepdims=True)
    a2 = jnp.sum(jnp.where(eidx == e2, aff, 0.0), axis=0, keepdims=True)
    tot = a1 + a2
    w1, w2 = a1 / tot, a2 / tot
    swap = e1 > e2
    lo = jnp.where(swap, e2, e1) & (EXPERTS_PER_GROUP - 1)
    hi = jnp.where(swap, e1, e2) & (EXPERTS_PER_GROUP - 1)
    pair = jnp.where(lo == 0, 0, jnp.where(lo == 1, 3, 5)) + hi - lo - 1
    cls = (e1 >> 2) * PAIRS_PER_GROUP + pair
    rows = [cls.astype(F32), jnp.where(swap, w2, w1), jnp.where(swap, w1, w2)]
    return jnp.concatenate(rows + [jnp.zeros_like(w1)] * (ROUTE_ROWS - len(rows)), axis=0)


def _pack_bf16_pairs(v):
    w = v.shape[1] // 2
    bits = pltpu.bitcast(v.astype(BF16).astype(F32), jnp.uint32)
    return pltpu.bitcast(bits[:, :w] | (bits[:, w:] >> 16), jnp.int32)


def _unpack_bf16_pairs(p):
    bits = pltpu.bitcast(p, jnp.uint32)
    hi = pltpu.bitcast(bits & jnp.uint32(0xFFFF0000), F32)
    lo = pltpu.bitcast(bits << 16, F32)
    return jnp.concatenate([hi, lo], axis=-1)


def _merge_kernel(*refs, n_real, n_fill, pending, **static):
    n_tok = 3 if pending else 1
    (mod_ref, n1_ref, n2_ref, a_ref, b_ref, c_ref, pc_ref, pp_ref, pn_ref, pw_ref, psc_ref,
     wg_ref, bg_ref, wb_ref, wo_ref, rw_ref, rb_ref) = refs[n_tok:n_tok + 17]
    outs = refs[-4:-1]
    ins = (refs[:n_tok], mod_ref, n1_ref, n2_ref, a_ref, b_ref, c_ref,
           (pc_ref, pp_ref, pn_ref, pw_ref, psc_ref),
           wg_ref, bg_ref, wb_ref, wo_ref, rw_ref, rb_ref)
    if n_fill:
        @pl.when(pl.program_id(1) >= n_real)
        def _():
            for ref in outs:
                ref[...] = jnp.zeros_like(ref)

        pl.when(pl.program_id(1) < n_real)(lambda: _merge_tile(ins, outs, refs[-1], **static))
    else:
        _merge_tile(ins, outs, refs[-1], **static)


def _merge_tile(ins, outs, merged_s, *, nw, sub, p_off, seq_lo, seq_hi):
    (tok_refs, mod_ref, n1_ref, n2_ref, a_ref, b_ref, c_ref, pool_refs,
     wg_ref, bg_ref, wb_ref, wo_ref, rw_ref, rb_ref) = ins
    xo_ref, h2_ref, route_ref = outs
    pc_ref, pp_ref, pn_ref, pw_ref, psc_ref = pool_refs
    x_ref = tok_refs[0]
    tm = x_ref.shape[1]
    sh1, sc1, g1, sh2, sc2, _ = _mod_slices(mod_ref[0])
    row0 = p_off + pl.program_id(1) * tm
    ext = jnp.concatenate([pp_ref[0], pc_ref[0], pn_ref[0]], axis=0).astype(F32)
    gpos = row0 - POOL_HALO + lax.broadcasted_iota(jnp.int32, (tm + 2 * POOL_HALO, 1), 0)
    ext = jnp.where((gpos >= seq_lo) & (gpos < seq_hi), ext, 0.0)
    for r in range(tm // sub):
        rows = slice(r * sub, (r + 1) * sub)
        x = x_ref[0, rows, :]
        if len(tok_refs) == 3:
            x = _pending_residual(x, tok_refs[1], tok_refs[2], rows)
        hb = _norm_mod(x, n1_ref[...], sh1, sc1).astype(BF16)
        pos = row0 - seq_lo + r * sub + lax.broadcasted_iota(jnp.int32, (sub, 1), 0)
        pooled = _pool_rows(ext[r * sub:(r + 1) * sub + 2 * POOL_HALO], pos, seq_hi - seq_lo,
                            pw_ref, psc_ref[...])
        branches = (a_ref[0, rows, :], b_ref[0, rows, :], c_ref[0, rows, :], pooled)
        for n in range(D_MODEL // nw):
            cols = slice(n * nw, (n + 1) * nw)
            merged = None
            for i, br in enumerate(branches):
                gate = jax.nn.sigmoid(
                    jnp.dot(hb, wg_ref[i, :, cols], preferred_element_type=F32) + bg_ref[i, :, cols])
                term = gate * jnp.dot(br, wb_ref[i, :, cols], preferred_element_type=F32)
                merged = term if merged is None else merged + term
            merged_s[rows, cols] = merged.astype(BF16)
        y = jnp.dot(merged_s[rows, :], wo_ref[...], preferred_element_type=F32)
        xn = x + g1 * y
        xo_ref[0, rows, :] = xn
        h2 = _norm_mod(xn, n2_ref[...], sh2, sc2)
        h2_ref[0, rows, :] = _pack_bf16_pairs(h2)
        h_hi = h2.astype(BF16)
        h_lo = (h2 - h_hi.astype(F32)).astype(BF16)
        nt = (((1,), (1,)), ((), ()))
        by_hi = lax.dot_general(rw_ref[...], h_hi, nt, preferred_element_type=F32)
        by_lo = lax.dot_general(rw_ref[:N_EXPERTS, :], h_lo, nt, preferred_element_type=F32)
        logits_t = by_hi[:N_EXPERTS] + by_hi[N_EXPERTS:] + by_lo
        route_ref[0, :, rows] = _route(logits_t, rb_ref[...])


def _merge(tokens, mod, mod_row, n1, n2, branches, offsets, p_in, pool_params, seq, weights, s_out, out_off,
           rows, tm, prev=None):
    pending = len(tokens) == 3
    b, _, d = tokens[0].shape
    wg, bg, wb, wo, rw_t, rb = weights
    pool_w, pool_scale = pool_params
    seq_lo, seq_hi = seq
    n_real = rows // tm
    n_fill = -(-(s_out - out_off - rows) // tm) if prev is None else 0
    step = lambda j: jnp.minimum(j, n_real - 1)
    blk = lambda width, off: pl.BlockSpec((1, tm, width), lambda bi, j: (bi, off // tm + step(j), 0))
    out_blk = lambda width: pl.BlockSpec((1, tm, width), lambda bi, j: (bi, out_off // tm + j, 0))
    assert all(o % tm == 0 for o in offsets) and out_off % tm == 0 and rows % tm == 0 and seq_lo % tm == 0
    hb = tm // POOL_HALO
    last_halo = p_in.shape[1] // POOL_HALO - 1
    halo = lambda shift: pl.BlockSpec(
        (1, POOL_HALO, BRANCH_W),
        lambda bi, j: (bi, jnp.clip((seq_lo // tm + step(j) + shift) * hb - 1 + shift, 0, last_halo), 0))
    out_shape = [
        jax.ShapeDtypeStruct((b, s_out, d), F32),
        jax.ShapeDtypeStruct((b, s_out, d // 2), jnp.int32),
        jax.ShapeDtypeStruct((b, ROUTE_ROWS, s_out), F32),
    ]
    mod_spec = pl.BlockSpec((1, 1, 6 * d), lambda bi, j: (bi if mod_row is None else mod_row, 0, 0))
    tok_off = out_off if pending else 0
    token_specs = [blk(d, tok_off)] + ([blk(d // 2, tok_off), mod_spec] if pending else [])
    n_in = len(tokens) + 17
    extra_specs = [] if prev is None else [pl.BlockSpec(memory_space=pl.ANY)] * 3
    return pl.pallas_call(
        functools.partial(_merge_kernel, nw=512, sub=MERGE_CHAIN_ROWS, n_real=n_real, n_fill=n_fill,
                          pending=pending, p_off=seq_lo, seq_lo=seq_lo, seq_hi=seq_hi),
        scratch_shapes=[pltpu.VMEM((tm, d), BF16)],
        grid=(b, n_real + n_fill),
        in_specs=token_specs + [
            mod_spec,
            _const_spec((1, d)), _const_spec((1, d)),
            *[blk(BRANCH_W, off) for off in offsets],
            blk(BRANCH_W, seq_lo), halo(0), halo(1),
            _const_spec(pool_w.shape), _const_spec((1, BRANCH_W)),
            _const_spec(wg.shape), _const_spec(bg.shape), _const_spec(wb.shape), _const_spec(wo.shape),
            _const_spec(rw_t.shape), _const_spec(rb.shape),
        ] + extra_specs,
        out_specs=[
            out_blk(d),
            out_blk(d // 2),
            pl.BlockSpec((1, ROUTE_ROWS, tm), lambda bi, j: (bi, 0, out_off // tm + j)),
        ],
        out_shape=out_shape,
        input_output_aliases={} if prev is None else {n_in + i: i for i in range(3)},
        compiler_params=_cparams(("parallel", "arbitrary"), 56),
        name="merge",
    )(*tokens, mod, n1, n2, *branches, p_in, p_in, p_in, pool_w, pool_scale,
      wg, bg, wb, wo, rw_t, rb, *(() if prev is None else prev))


def _rank_kernel(cls_ref, rank_ref, cnt_ref, cnt_s, *, tr):
    @pl.when(pl.program_id(0) == 0)
    def _():
        cnt_s[...] = jnp.zeros_like(cnt_s)

    cls = cls_ref[0]
    cid = lax.broadcasted_iota(jnp.int32, (CLASS_ROWS, tr), 0).astype(F32)
    onehot = cid == cls
    before = lax.broadcasted_iota(jnp.int32, (tr, tr), 0) < lax.broadcasted_iota(jnp.int32, (tr, tr), 1)
    prefix = jnp.dot(jnp.where(onehot, 1.0, 0.0).astype(BF16), jnp.where(before, 1.0, 0.0).astype(BF16),
                     preferred_element_type=F32)
    carry = cnt_s[...][:, 0:1]
    rank_ref[0] = jnp.sum(jnp.where(onehot, prefix + carry, 0.0), axis=0, keepdims=True)
    cnt_s[...] += jnp.sum(jnp.where(onehot, 1.0, 0.0), axis=1, keepdims=True)
    cnt_ref[...] = cnt_s[...]


def _rank(cls_flat, tr=512):
    n = cls_flat.shape[0]
    tr = math.gcd(n, tr)
    cls3 = cls_flat.reshape(n // tr, 1, tr)
    rank, cnt = pl.pallas_call(
        functools.partial(_rank_kernel, tr=tr),
        grid=(n // tr,),
        in_specs=[pl.BlockSpec((1, 1, tr), lambda i: (i, 0, 0))],
        out_specs=[pl.BlockSpec((1, 1, tr), lambda i: (i, 0, 0)), _const_spec((CLASS_ROWS, LANES))],
        out_shape=[jax.ShapeDtypeStruct((n // tr, 1, tr), F32), jax.ShapeDtypeStruct((CLASS_ROWS, LANES), F32)],
        scratch_shapes=[pltpu.VMEM((CLASS_ROWS, LANES), F32)],
        compiler_params=_cparams(("arbitrary",), 32),
        name="rank",
    )(cls3)
    return rank.reshape(n), cnt[:N_CLASSES, 0]


def _sc_layout(n):
    info = plsc.get_sparse_core_info()
    nw = info.num_cores * info.num_subcores
    per_worker = n // nw
    assert per_worker * nw == n
    chunk = max(c for c in range(8, SC_MAX_CHUNK + 1, 8) if per_worker % c == 0)
    return info.num_cores, nw, per_worker // chunk, chunk


def _sc_scatter_rows(src, wts, pos, n_out):
    n, w = src.shape
    nc, nw, k, c = _sc_layout(n)
    mesh = plsc.VectorSubcoreMesh(core_axis_name="c", subcore_axis_name="s")

    @functools.partial(
        pl.kernel, mesh=mesh,
        out_type=(jax.ShapeDtypeStruct((n_out, w), src.dtype), jax.ShapeDtypeStruct((n_out, wts.shape[1]), wts.dtype)),
        scratch_types=[pltpu.VMEM((k, c), jnp.int32), pltpu.VMEM((c, w), src.dtype),
                       pltpu.VMEM((c, wts.shape[1]), wts.dtype), pltpu.SemaphoreType.DMA],
        name="moe_scatter",
    )
    def scatter(src_hbm, wts_hbm, pos_hbm, out_hbm, wout_hbm, idx_v, rows_v, wrows_v, sem):
        wid = lax.axis_index("s") * nc + lax.axis_index("c")
        pltpu.sync_copy(pos_hbm.at[wid], idx_v)

        @pl.loop(0, k)
        def _(j):
            off = pl.multiple_of(wid * (k * c) + j * c, 8)
            pltpu.sync_copy(src_hbm.at[pl.ds(off, c)], rows_v)
            pltpu.sync_copy(wts_hbm.at[pl.ds(off, c)], wrows_v)
            pltpu.async_copy(rows_v, out_hbm.at[idx_v.at[j]], sem).wait()
            pltpu.async_copy(wrows_v, wout_hbm.at[idx_v.at[j]], sem).wait()

    return scatter(src, wts, pos.reshape(nw, k, c))


def _sc_gather_rows(src, pos):
    n = pos.shape[0]
    w = src.shape[1]
    nc, nw, k, c = _sc_layout(n)
    mesh = plsc.VectorSubcoreMesh(core_axis_name="c", subcore_axis_name="s")

    @functools.partial(
        pl.kernel, mesh=mesh,
        out_type=jax.ShapeDtypeStruct((n, w), src.dtype),
        scratch_types=[pltpu.VMEM((k, c), jnp.int32), pltpu.VMEM((c, w), src.dtype), pltpu.SemaphoreType.DMA],
        name="moe_gather",
    )
    def gather(src_hbm, pos_hbm, out_hbm, idx_v, rows_v, sem):
        wid = lax.axis_index("s") * nc + lax.axis_index("c")
        pltpu.sync_copy(pos_hbm.at[wid], idx_v)

        @pl.loop(0, k)
        def _(j):
            off = pl.multiple_of(wid * (k * c) + j * c, 8)
            pltpu.async_copy(src_hbm.at[idx_v.at[j]], rows_v, sem).wait()
            pltpu.sync_copy(rows_v, out_hbm.at[pl.ds(off, c)])

    return gather(src, pos.reshape(nw, k, c))


def _gmm_kernel(lo_ref, hi_ref, new_ref, nact_ref, h_ref, wt_ref,
                w1a_ref, w1b_ref, w3a_ref, w3b_ref, w2a_ref, w2b_ref, o_ref, w13_s, w2_s):
    t = pl.program_id(0)

    @pl.when(new_ref[t] == 1)
    def _():
        for i, (w1_ref, w3_ref, w2_ref) in enumerate(((w1a_ref, w3a_ref, w2a_ref), (w1b_ref, w3b_ref, w2b_ref))):
            w13_s[i, :, :EXPERT_FF] = w1_ref[0, 0].astype(BF16)
            w13_s[i, :, EXPERT_FF:] = w3_ref[0, 0].astype(BF16)
            w2_s[i] = w2_ref[0, 0].astype(BF16)

    @pl.when(t < nact_ref[0])
    def _():
        x = _unpack_bf16_pairs(h_ref[...]).astype(BF16)
        wts = wt_ref[...]

        def expert(i):
            ab = jnp.dot(x, w13_s[i], preferred_element_type=F32)
            a, gate = ab[:, :EXPERT_FF], ab[:, EXPERT_FF:]
            hid = (a * jax.nn.sigmoid(a)) * gate
            return jnp.dot(hid.astype(BF16), w2_s[i], preferred_element_type=F32)

        y = wts[:, 0:1] * expert(0) + wts[:, 1:2] * expert(1)
        o_ref[...] = _pack_bf16_pairs(y)


def _gmm(tile_lo, tile_hi, tile_new, n_act, hs, ws, w1, w3, w2, layer, tm):
    n_pad, half = hs.shape
    d = 2 * half
    row = lambda t, lo, hi, new, na: (jnp.minimum(t, na[0] - 1), 0)
    e_lo = lambda t, lo, hi, new, na: (layer, lo[jnp.minimum(t, na[0] - 1)], 0, 0)
    e_hi = lambda t, lo, hi, new, na: (layer, hi[jnp.minimum(t, na[0] - 1)], 0, 0)
    up = lambda e: pl.BlockSpec((1, 1, d, EXPERT_FF), e)
    down = lambda e: pl.BlockSpec((1, 1, EXPERT_FF, d), e)
    return pl.pallas_call(
        _gmm_kernel,
        grid_spec=pltpu.PrefetchScalarGridSpec(
            num_scalar_prefetch=4,
            grid=(n_pad // tm,),
            in_specs=[
                pl.BlockSpec((tm, half), row),
                pl.BlockSpec((tm, ws.shape[1]), row),
                up(e_lo), up(e_hi), up(e_lo), up(e_hi), down(e_lo), down(e_hi),
            ],
            out_specs=pl.BlockSpec((tm, half), row),
            scratch_shapes=[pltpu.VMEM((2, d, 2 * EXPERT_FF), BF16), pltpu.VMEM((2, EXPERT_FF, d), BF16)],
        ),
        out_shape=jax.ShapeDtypeStruct((n_pad, half), jnp.int32),
        compiler_params=_cparams(("arbitrary",), 56),
        name="moe_gmm",
    )(tile_lo, tile_hi, tile_new, n_act, hs, ws, w1, w1, w3, w3, w2, w2)


def _moe_routed(h2p, route, w1, w3, w2, layer, tm=256):
    b, s, half = h2p.shape
    n = b * s
    n_pad = n + N_CLASSES * tm
    cls = route[:, 0, :].reshape(n)
    rank, counts = _rank(cls)
    counts = counts.astype(jnp.int32)
    padded = (counts + tm - 1) // tm * tm
    ends = jnp.cumsum(padded)
    pos = jnp.take(ends - padded, cls.astype(jnp.int32)) + rank.astype(jnp.int32)
    n_act = (ends[-1] // tm).reshape(1)
    tile_row = jnp.arange(n_pad // tm, dtype=jnp.int32) * tm
    tile_cls = jnp.minimum(jnp.sum(tile_row[:, None] >= ends[None, :], axis=1), N_CLASSES - 1)
    pair_lo, pair_hi = (jnp.asarray(a, jnp.int32) for a in _class_experts())
    wts = jnp.concatenate([route[:, 1, :].reshape(n, 1), route[:, 2, :].reshape(n, 1),
                           jnp.zeros((n, SC_WEIGHT_COLS - 2), F32)], axis=1)
    hs, ws = _sc_scatter_rows(h2p.reshape(n, half), wts, pos, n_pad)
    prev_cls = jnp.concatenate([jnp.full((1,), -1, tile_cls.dtype), tile_cls[:-1]])
    tile_new = ((tile_cls != prev_cls) & (tile_row < ends[-1])).astype(jnp.int32)
    ys = _gmm(jnp.take(pair_lo, tile_cls), jnp.take(pair_hi, tile_cls), tile_new, n_act, hs, ws,
              w1, w3, w2, layer, tm)
    return _sc_gather_rows(ys, pos).reshape(b, s, half)


def _class_experts():
    lo, hi = [], []
    for g in range(N_EXPERTS // EXPERTS_PER_GROUP):
        for i in range(EXPERTS_PER_GROUP):
            for j in range(i + 1, EXPERTS_PER_GROUP):
                lo.append(EXPERTS_PER_GROUP * g + i)
                hi.append(EXPERTS_PER_GROUP * g + j)
    return np.array(lo), np.array(hi)


def _final_residual_kernel(x_ref, y_ref, mod_ref, g_ref, o_ref):
    x = _pending_residual(x_ref[0], y_ref, mod_ref)
    o_ref[0] = x * lax.rsqrt(jnp.mean(x * x, axis=-1, keepdims=True) + EPS) * g_ref[...]


def _final_residual(x1, yp, mod, gain, tm=512):
    b, t, d = x1.shape
    tok = lambda bi, j: (bi, j, 0)
    return pl.pallas_call(
        _final_residual_kernel,
        grid=(b, t // tm),
        in_specs=[
            pl.BlockSpec((1, tm, d), tok),
            pl.BlockSpec((1, tm, d // 2), tok),
            pl.BlockSpec((1, 1, 6 * d), lambda bi, j: (bi, 0, 0)),
            _const_spec((1, d)),
        ],
        out_specs=pl.BlockSpec((1, tm, d), tok),
        out_shape=jax.ShapeDtypeStruct((b, t, d), F32),
        compiler_params=_cparams(("parallel", "arbitrary"), 32),
        name="final_residual",
    )(x1, yp, mod, gain)


def _rope_tables(t_lat, n_ctx):
    rows = t_lat // GRID_W
    row = jnp.repeat(jnp.arange(rows, dtype=F32), GRID_W)
    col = jnp.tile(jnp.arange(GRID_W, dtype=F32), rows)
    inv_freq = ROPE_THETA ** (-jnp.arange(0, ROPE_AXIS_DIM, 2, dtype=F32) / ROPE_AXIS_DIM)
    ang = jnp.stack([row[:, None] * inv_freq, col[:, None] * inv_freq], axis=1)
    cos, sin = jnp.cos(ang), jnp.sin(ang)
    zero = jnp.zeros_like(sin)
    cos_h = jnp.concatenate([cos, cos], axis=-1).reshape(t_lat, HEAD_DIM)
    sa_h = jnp.concatenate([-sin, zero], axis=-1).reshape(t_lat, HEAD_DIM)
    sb_h = jnp.concatenate([zero, sin], axis=-1).reshape(t_lat, HEAD_DIM)
    reps = LANES // HEAD_DIM
    pad = lambda tbl, fill: jnp.concatenate(
        [jnp.tile(tbl, (1, reps)), jnp.full((n_ctx, LANES), fill, F32)], axis=0)
    return pad(cos_h, 1.0), pad(sa_h, 0.0), pad(sb_h, 0.0)


def _permute_heads(w, axis):
    shp = w.shape
    w = w.reshape(shp[:axis] + (N_Q_HEADS, HEAD_DIM) + shp[axis + 1:])
    w = jnp.take(w, jnp.array(HEAD_PERM), axis=axis)
    return w.reshape(shp)


def _square_factor(n):
    r = int(round(math.sqrt(n)))
    assert r * r == n, "sequence lengths must be perfect squares for the two-stage DFT"
    return r


def kernel(x, c, ctx, c_ctx, w_ada, b_ada, norm1, norm2, w_in, q_gain, k_gain, sink, pool_w, pool_scale,
           w_branch, w_gate, b_gate, w_out, router_w, router_bias, w1, w3, w2, norm_f):
    b, t_lat, d = x.shape
    n_ctx = ctx.shape[1]
    s = t_lat + n_ctx
    depth = w_ada.shape[0]
    assert d == D_MODEL and b < MOD_ROWS and t_lat % 256 == 0 and n_ctx % 256 == 0 and t_lat % n_ctx == 0

    tokens = (x, ctx)
    cc = jnp.zeros((MOD_ROWS, d), F32).at[:b].set(c).at[b].set(c_ctx)
    mod_all = _ada(cc, w_ada, b_ada).reshape(depth, MOD_ROWS, 1, 6 * d)

    cos, sa, sb = _rope_tables(t_lat, n_ctx)
    seg = jnp.asarray(np.kron(np.eye(N_Q_HEADS), np.full((HEAD_DIM, HEAD_DIM), 1.0 / HEAD_DIM)), BF16)
    cs = jnp.asarray(_channel_dft_table()).astype(BF16)
    f_lat = [jnp.asarray(a).astype(BF16) for a in _fourier_tables(*(_square_factor(t_lat),) * 2)]
    f_ctx = [jnp.asarray(a).astype(BF16) for a in _fourier_tables(*(_square_factor(n_ctx),) * 2)]
    wbias = jnp.asarray(_window_bias(n_ctx))
    rw_hi = router_w.T.astype(BF16)
    rw_t = jnp.concatenate([rw_hi, (router_w.T - rw_hi.astype(F32)).astype(BF16)], axis=0)
    rb = router_bias.reshape(N_EXPERTS, 1)

    for l in range(depth):
        need_ctx = l < depth - 1
        s_out = s if need_ctx else t_lat
        cols = jnp.split(w_in[l], np.cumsum((512, 512, 512, 128, 128, 512, 128))[:], axis=1)
        f_w, p_w, qb_w, kb_w, vb_w, qw_w, kw_w, vw_w = cols
        w_in_l = jnp.concatenate([f_w, p_w, _permute_heads(qb_w, 1), _permute_heads(qw_w, 1),
                                  kb_w, vb_w, kw_w, vw_w], axis=1).astype(BF16)
        wb_l = jnp.stack([w_branch[l, 0], _permute_heads(w_branch[l, 1], 0),
                          _permute_heads(w_branch[l, 2], 0), w_branch[l, 3]]).astype(BF16)
        mod = mod_all[l]
        n1 = norm1[l].reshape(1, d)
        n2 = norm2[l].reshape(1, d)
        qg = jnp.tile(q_gain[l], N_Q_HEADS).reshape(1, BRANCH_W)
        kg = jnp.tile(k_gain[l], LANES // HEAD_DIM).reshape(1, LANES)

        zr, zi, p_in, qb, qw, kvb, kvw = _inproj(tokens, t_lat, mod, n1, w_in_l, qg, kg, seg, cs, cos, sa, sb)

        out_a = _fourier(zr, zi, f_lat[0], f_lat[1], t_lat, 0)
        out_b = _gattn(qb, kvb, 0, t_lat, 0, s)
        out_c = _wattn(jnp.take(sink[l], jnp.array(HEAD_PERM)) * LOG2E, wbias, qw, kvw, s_out, t_lat)

        weights = (w_gate[l].astype(BF16), b_gate[l].reshape(4, 1, d), wb_l, w_out[l].astype(BF16), rw_t, rb)
        pool_params = (pool_w[l].astype(BF16), pool_scale[l].reshape(1, BRANCH_W))
        pending = len(tokens) == 3
        merged = _merge(tokens if pending else tokens[:1], mod, None, n1, n2, (out_a, out_b, out_c), (0, 0, 0),
                        p_in, pool_params, (0, t_lat), weights, s_out, 0, t_lat, 2 * MERGE_CHAIN_ROWS)
        if need_ctx:
            out_ac = _fourier(zr, zi, f_ctx[0], f_ctx[1], n_ctx, t_lat // n_ctx)
            out_bc = _gattn(qb, kvb, t_lat, n_ctx, t_lat, n_ctx)
            merged = _merge(tokens if pending else tokens[1:], mod, b, n1, n2, (out_ac, out_bc, out_c),
                            (0, 0, t_lat), p_in, pool_params, (t_lat, s), weights, s_out, t_lat, n_ctx,
                            MERGE_CHAIN_ROWS, prev=merged)
        x1, h2p, route = merged
        yp = _moe_routed(h2p, route, w1, w3, w2, l)
        tokens = (x1, yp, mod)

    return _final_residual(*tokens, norm_f.reshape(1, d))
```

```python
import functools
import math

import numpy as np
import jax
import jax.numpy as jnp
from jax import lax
from jax.experimental import pallas as pl
from jax.experimental.pallas import tpu as pltpu
from jax.experimental.pallas import tpu_sc as plsc

F32 = jnp.float32
BF16 = jnp.bfloat16

D_MODEL = 1024
HEAD_DIM = 64
N_Q_HEADS = 8
N_KV_HEADS = 2
GRID_W = 64
ROPE_THETA = 10000.0
ROPE_AXIS_DIM = HEAD_DIM // 2
QBLK = 128
WINDOW = 128
BRANCH_W = 512
GROUP_W = 128
POOL_WINDOWS = (2, 4, 8, 16)
N_EXPERTS = 16
EXPERTS_PER_GROUP = 4
EXPERT_FF = 512
EPS = 1e-6
MOD_ROWS = 16
NEG_BIG = -1e30
LOG2E = math.log2(math.e)
LANES = 128
POOL_HALO = 16
PAIRS_PER_GROUP = 6
N_CLASSES = 24
CLASS_ROWS = 32
ROUTE_ROWS = 8
SC_MAX_CHUNK = 128
WATTN_QBLOCKS = 2
MERGE_CHAIN_ROWS = 256

HEAD_PERM = (0, 4, 1, 5, 2, 6, 3, 7)


def _cparams(sem, vmem_mb):
    return pltpu.CompilerParams(dimension_semantics=sem, vmem_limit_bytes=vmem_mb * 1024 * 1024)


def _const_spec(shape):
    nd = len(shape)
    return pl.BlockSpec(shape, lambda *_: (0,) * nd)


def _ada_kernel(c_ref, w_ref, b_ref, o_ref):
    c = c_ref[...]
    s = c * jax.nn.sigmoid(c)
    o_ref[0] = jnp.dot(s.astype(BF16), w_ref[0].astype(BF16), preferred_element_type=F32) + b_ref[0]


def _ada(cc, w_ada, b_ada):
    depth, d, n = w_ada.shape
    tn = 1536
    return pl.pallas_call(
        _ada_kernel,
        grid=(depth, n // tn),
        in_specs=[
            pl.BlockSpec((MOD_ROWS, d), lambda l, j: (0, 0)),
            pl.BlockSpec((1, d, tn), lambda l, j: (l, 0, j)),
            pl.BlockSpec((1, 1, tn), lambda l, j: (l, 0, j)),
        ],
        out_specs=pl.BlockSpec((1, MOD_ROWS, tn), lambda l, j: (l, 0, j)),
        out_shape=jax.ShapeDtypeStruct((depth, MOD_ROWS, n), F32),
        compiler_params=_cparams(("arbitrary", "arbitrary"), 40),
        name="ada",
    )(cc, w_ada, b_ada.reshape(depth, 1, n))


def _norm_mod(x, gain, shift, scale):
    ms = jnp.mean(x * x, axis=-1, keepdims=True)
    return (x * lax.rsqrt(ms + EPS) * gain) * (1.0 + scale) + shift


def _mod_slices(m):
    d = D_MODEL
    return [m[:, i * d:(i + 1) * d] for i in range(6)]


def _head_norm(z, seg, gain):
    ms = jnp.dot((z * z).astype(BF16), seg, preferred_element_type=F32)
    return z * lax.rsqrt(ms + EPS) * gain


def _rope(z, cos, sin_a, sin_b):
    outs = []
    for c in range(z.shape[1] // LANES):
        zc = z[:, c * LANES:(c + 1) * LANES]
        nxt = pltpu.roll(zc, LANES - ROPE_AXIS_DIM // 2, 1)
        prv = pltpu.roll(zc, ROPE_AXIS_DIM // 2, 1)
        outs.append(zc * cos + nxt * sin_a + prv * sin_b)
    return outs[0] if len(outs) == 1 else jnp.concatenate(outs, axis=-1)


def _stream_specs(tm, d, n_lat):
    return [pl.BlockSpec((1, tm, d), lambda bi, j: (bi, jnp.minimum(j, n_lat - 1), 0)),
            pl.BlockSpec((1, tm, d), lambda bi, j: (bi, jnp.maximum(j - n_lat, 0), 0))]


def _pending_residual(x, y_ref, modp_ref, rows=slice(None)):
    return x + _mod_slices(modp_ref[0])[5] * _unpack_bf16_pairs(y_ref[0, rows, :])


def _inproj_kernel(*refs, n_lat, pending):
    if pending:
        x_ref, y_ref, modp_ref = refs[:3]
        x = _pending_residual(x_ref[0], y_ref, modp_ref)
    else:
        xl_ref, xc_ref = refs[:2]
        x = jnp.where(pl.program_id(1) >= n_lat, xc_ref[0], xl_ref[0])
    (mod_ref, n1_ref, w_ref, qg_ref, kg_ref, seg_ref, cs_ref, cos_ref, sa_ref, sb_ref,
     zr_ref, zi_ref, p_ref, qb_ref, qw_ref, kvb_ref, kvw_ref) = refs[3 if pending else 2:]
    sh1, sc1 = _mod_slices(mod_ref[0])[:2]
    h = _norm_mod(x, n1_ref[...], sh1, sc1)
    u = jnp.dot(h.astype(BF16), w_ref[...], preferred_element_type=F32)
    cos, sa, sb = cos_ref[...], sa_ref[...], sb_ref[...]
    w = BRANCH_W
    f_in = u[:, 0:w].astype(BF16)
    zr, zi = [], []
    for g in range(w // GROUP_W):
        z = jnp.dot(f_in[:, g * GROUP_W:(g + 1) * GROUP_W], cs_ref[...], preferred_element_type=F32)
        zr.append(z[:, :GROUP_W])
        zi.append(z[:, GROUP_W:])
    zr_ref[0] = jnp.concatenate(zr, axis=-1).astype(BF16)
    zi_ref[0] = jnp.concatenate(zi, axis=-1).astype(BF16)
    p_ref[0] = u[:, w:2 * w].astype(BF16)
    seg = seg_ref[...]
    qb = _rope(_head_norm(u[:, 2 * w:3 * w], seg, qg_ref[...]), cos, sa, sb)
    qb_ref[0] = (qb * (HEAD_DIM ** -0.5 * LOG2E)).astype(BF16)
    qw = _rope(u[:, 3 * w:4 * w], cos, sa, sb)
    qw_ref[0] = (qw * (HEAD_DIM ** -0.5 * LOG2E)).astype(BF16)
    o = 4 * w
    kb = _rope(_head_norm(u[:, o:o + LANES], seg[:LANES, :LANES], kg_ref[...]), cos, sa, sb)
    vb = u[:, o + LANES:o + 2 * LANES]
    kw = _rope(u[:, o + 2 * LANES:o + 3 * LANES], cos, sa, sb)
    vw = u[:, o + 3 * LANES:o + 4 * LANES]
    kvb_ref[0] = jnp.concatenate([kb, vb, jnp.ones_like(vb)], axis=-1).astype(BF16)
    kvw_ref[0] = jnp.concatenate([kw, vw, jnp.ones_like(vw)], axis=-1).astype(BF16)


def _inproj(tokens, t_lat, mod, n1, w_in, qg, kg, seg, cs, cos, sa, sb, tm=256):
    pending = len(tokens) == 3
    b, _, d = tokens[0].shape
    s = tokens[0].shape[1] if pending else t_lat + tokens[1].shape[1]
    nw = w_in.shape[1]
    n_lat = t_lat // tm
    tok = lambda bi, j: (bi, j, 0)
    tab = lambda bi, j: (j, 0)
    mod_spec = pl.BlockSpec((1, 1, 6 * d), lambda bi, j: (jnp.where(j >= n_lat, b, bi), 0, 0))
    if pending:
        token_specs = [pl.BlockSpec((1, tm, d), tok), pl.BlockSpec((1, tm, d // 2), tok), mod_spec]
    else:
        token_specs = _stream_specs(tm, d, n_lat)
    widths = (BRANCH_W,) * 5 + (3 * LANES, 3 * LANES)
    return pl.pallas_call(
        functools.partial(_inproj_kernel, n_lat=n_lat, pending=pending),
        grid=(b, s // tm),
        in_specs=token_specs + [
            mod_spec,
            _const_spec((1, d)),
            _const_spec((d, nw)),
            _const_spec((1, BRANCH_W)),
            _const_spec((1, LANES)),
            _const_spec((BRANCH_W, BRANCH_W)),
            _const_spec((GROUP_W, 2 * GROUP_W)),
            pl.BlockSpec((tm, LANES), tab),
            pl.BlockSpec((tm, LANES), tab),
            pl.BlockSpec((tm, LANES), tab),
        ],
        out_specs=[pl.BlockSpec((1, tm, wd), tok) for wd in widths],
        out_shape=[jax.ShapeDtypeStruct((b, s, wd), BF16) for wd in widths],
        compiler_params=_cparams(("parallel", "arbitrary"), 48),
        name="inproj",
    )(*tokens, mod, n1, w_in, qg, kg, seg, cs, cos, sa, sb)


def _split_heads(qc, lane):
    zero = jnp.zeros_like(qc)
    return jnp.concatenate([jnp.where(lane < HEAD_DIM, qc, zero),
                            jnp.where(lane >= HEAD_DIM, qc, zero)], axis=0)


def _gattn_kernel(q_ref, kv_ref, o_ref, q2_s, acc_s, *, sub):
    tq = q_ref.shape[1]
    lane = lax.broadcasted_iota(jnp.int32, (1, LANES), 1)
    nt = (((1,), (1,)), ((), ()))
    n_chunks = BRANCH_W // LANES
    for c in range(n_chunks):
        q2_s[2 * c * tq:(2 * c + 2) * tq, :] = _split_heads(q_ref[0, :, c * LANES:(c + 1) * LANES], lane)
    k = kv_ref[0, :, 0:LANES]
    v = kv_ref[0, :, LANES:3 * LANES]
    for r in range(2 * n_chunks * tq // sub):
        rows = slice(r * sub, (r + 1) * sub)
        s = lax.dot_general(q2_s[rows, :], k, nt, preferred_element_type=F32)
        p = jnp.exp2(s - jnp.max(s, axis=-1, keepdims=True))
        acc_s[rows, :] = jnp.dot(p.astype(BF16), v, preferred_element_type=F32)
    for c in range(n_chunks):
        lo = acc_s[2 * c * tq:(2 * c + 1) * tq, :]
        hi = acc_s[(2 * c + 1) * tq:(2 * c + 2) * tq, :]
        o = jnp.where(lane < HEAD_DIM, lo[:, :LANES] / lo[:, LANES:], hi[:, :LANES] / hi[:, LANES:])
        o_ref[0, :, c * LANES:(c + 1) * LANES] = o.astype(BF16)


def _gattn(qb, kv, q_start, q_len, k_start, k_len, tq=256, sub=128):
    b = qb.shape[0]
    assert q_start % tq == 0 and q_len % tq == 0 and k_start % k_len == 0 and k_len % LANES == 0
    rows = 2 * tq * (BRANCH_W // LANES)
    return pl.pallas_call(
        functools.partial(_gattn_kernel, sub=sub),
        grid=(b, q_len // tq),
        in_specs=[
            pl.BlockSpec((1, tq, BRANCH_W), lambda bi, j: (bi, q_start // tq + j, 0)),
            pl.BlockSpec((1, k_len, 3 * LANES), lambda bi, j: (bi, k_start // k_len, 0)),
        ],
        out_specs=pl.BlockSpec((1, tq, BRANCH_W), lambda bi, j: (bi, j, 0)),
        out_shape=jax.ShapeDtypeStruct((b, q_len, BRANCH_W), BF16),
        scratch_shapes=[pltpu.VMEM((rows, LANES), BF16), pltpu.VMEM((rows, 2 * LANES), F32)],
        compiler_params=_cparams(("parallel", "arbitrary"), 48),
        name="gattn",
    )(qb, kv)


def _window_bias(n_ctx):
    tq = WATTN_QBLOCKS * QBLK
    qi = np.arange(tq)[:, None]
    kj = np.arange(tq + 2 * QBLK)[None, :]
    band = np.abs(kj - WINDOW - qi) <= WINDOW
    blk = kj // QBLK
    variants = [band & (blk != 0), band, band & (blk != WATTN_QBLOCKS + 1), np.zeros_like(band)]
    out = [np.concatenate([np.ones((tq, n_ctx), bool), v], axis=1) for v in variants]
    return np.where(np.stack(out), 0.0, NEG_BIG).astype(np.float32)


def _wattn_kernel(sink_ref, bias_ref, q_ref, *refs, sub):
    o_ref, q2_s, kv_s, acc_s = refs[-4:]
    tq = q_ref.shape[1]
    off = 0
    for blk in refs[:-4]:
        kv_s[off:off + blk.shape[1], :] = blk[0]
        off += blk.shape[1]
    lane = lax.broadcasted_iota(jnp.int32, (1, LANES), 1)
    nt = (((1,), (1,)), ((), ()))
    n_chunks = BRANCH_W // LANES
    for c in range(n_chunks):
        q2_s[2 * c * tq:(2 * c + 2) * tq, :] = _split_heads(q_ref[0, :, c * LANES:(c + 1) * LANES], lane)
    k, v = kv_s[:, 0:LANES], kv_s[:, LANES:3 * LANES]
    for r in range(2 * n_chunks * tq // sub):
        rows = slice(r * sub, (r + 1) * sub)
        q_off = (r * sub) % tq
        sk = sink_ref[(r * sub) // tq]
        s = lax.dot_general(q2_s[rows, :], k, nt, preferred_element_type=F32) + bias_ref[0, q_off:q_off + sub, :]
        m = jnp.maximum(jnp.max(s, axis=-1, keepdims=True), sk)
        pv = jnp.dot(jnp.exp2(s - m).astype(BF16), v, preferred_element_type=F32)
        acc_s[rows, :LANES] = pv[:, :LANES]
        acc_s[rows, LANES:] = pv[:, LANES:] + jnp.exp2(sk - m)
    for c in range(n_chunks):
        lo = acc_s[2 * c * tq:(2 * c + 1) * tq, :]
        hi = acc_s[(2 * c + 1) * tq:(2 * c + 2) * tq, :]
        o = jnp.where(lane < HEAD_DIM, lo[:, :LANES] / lo[:, LANES:], hi[:, :LANES] / hi[:, LANES:])
        o_ref[0, :, c * LANES:(c + 1) * LANES] = o.astype(BF16)


def _wattn(sink, bias, qw, kv, s_out, t_lat):
    b, s, _ = qw.shape
    n_ctx = s - t_lat
    nq = WATTN_QBLOCKS
    tq = nq * QBLK
    assert n_ctx % tq == 0 and t_lat // tq >= 2
    last = s // QBLK - 1
    n_lat = t_lat // tq
    variant = lambda j: jnp.where(j >= n_lat, 3, jnp.where(j == 0, 0, jnp.where(j == n_lat - 1, 2, 1)))
    key_block = lambda off: pl.BlockSpec(
        (1, QBLK, 3 * LANES), lambda bi, j: (bi, jnp.clip(j * nq + off, 0, last), 0))
    rows = 2 * tq * (BRANCH_W // LANES)
    return pl.pallas_call(
        functools.partial(_wattn_kernel, sub=128),
        scratch_shapes=[pltpu.VMEM((rows, LANES), BF16), pltpu.VMEM((bias.shape[2], 3 * LANES), BF16),
                        pltpu.VMEM((rows, 2 * LANES), F32)],
        grid=(b, s_out // tq),
        in_specs=[
            pl.BlockSpec(memory_space=pltpu.SMEM),
            pl.BlockSpec((1,) + bias.shape[1:], lambda bi, j: (variant(j), 0, 0)),
            pl.BlockSpec((1, tq, BRANCH_W), lambda bi, j: (bi, j, 0)),
            pl.BlockSpec((1, n_ctx, 3 * LANES), lambda bi, j: (bi, t_lat // n_ctx, 0)),
        ] + [key_block(off) for off in range(-1, nq + 1)],
        out_specs=pl.BlockSpec((1, tq, BRANCH_W), lambda bi, j: (bi, j, 0)),
        out_shape=jax.ShapeDtypeStruct((b, s_out, BRANCH_W), BF16),
        compiler_params=_cparams(("parallel", "arbitrary"), 32),
        name="wattn",
    )(sink, bias, qw, *([kv] * (nq + 3)))


def _fourier_tables(n1, n2):
    t = n1 * n2
    k2 = np.arange(n2)[None, :, None]
    t2 = np.arange(n2)[None, None, :]
    t1 = np.arange(n1)[:, None, None]
    theta = 2.0 * np.pi * ((k2 * t2 * n1 + k2 * t1) % t) / t
    er, ei = np.cos(theta) / math.sqrt(n2), -np.sin(theta) / math.sqrt(n2)
    e = np.concatenate([np.concatenate([er, -ei], axis=2), np.concatenate([ei, er], axis=2)], axis=1)
    k1 = np.arange(n1)[:, None]
    phi = 2.0 * np.pi * ((k1 * np.arange(n1)[None, :]) % n1) / n1
    dcat = np.concatenate([np.cos(phi), np.sin(phi)], axis=1) / math.sqrt(n1)
    return e.astype(np.float32), dcat.astype(np.float32)


def _channel_dft_table():
    c = np.arange(GROUP_W)
    ang = 2.0 * np.pi * ((c[:, None] * c[None, :]) % GROUP_W) / GROUP_W
    return (np.concatenate([np.cos(ang), -np.sin(ang)], axis=1) / math.sqrt(GROUP_W)).astype(np.float32)


def _fourier_kernel(zr_ref, zi_ref, e_ref, d_ref, o_ref, xr_s, xi_s, yr_s, yi_s, *, n1, n2):
    nc = xr_s.shape[0]
    chunk = lambda c: slice(c * LANES, (c + 1) * LANES)

    def gather(ref, start, size, stride):
        return jnp.concatenate([ref[c, pl.ds(start, size, stride=stride), :] for c in range(nc)], axis=-1)

    for c in range(nc):
        xr_s[c] = zr_ref[0, :, chunk(c)].astype(F32)
        xi_s[c] = zi_ref[0, :, chunk(c)].astype(F32)
    for t1 in range(n1):
        xs = jnp.concatenate([gather(xr_s, t1, n2, n1), gather(xi_s, t1, n2, n1)], axis=0).astype(BF16)
        y = jnp.dot(e_ref[t1], xs, preferred_element_type=F32)
        for c in range(nc):
            yr_s[c, t1 * n2:(t1 + 1) * n2, :] = y[:n2, chunk(c)]
            yi_s[c, t1 * n2:(t1 + 1) * n2, :] = y[n2:, chunk(c)]
    for k2 in range(n2):
        ys = jnp.concatenate([gather(yr_s, k2, n1, n2), gather(yi_s, k2, n1, n2)], axis=0).astype(BF16)
        o = jnp.dot(d_ref[...], ys, preferred_element_type=F32)
        for c in range(nc):
            xr_s[c, pl.ds(k2, n1, stride=n2), :] = o[:, chunk(c)]
    for c in range(nc):
        o_ref[0, :, chunk(c)] = xr_s[c].astype(BF16)


def _fourier(zr, zi, e_tab, d_tab, t_len, row_block, cw=256):
    b = zr.shape[0]
    n1 = d_tab.shape[0]
    n2 = t_len // n1
    zspec = pl.BlockSpec((1, t_len, cw), lambda bi, j: (bi, row_block, j))
    return pl.pallas_call(
        functools.partial(_fourier_kernel, n1=n1, n2=n2),
        grid=(b, BRANCH_W // cw),
        in_specs=[zspec, zspec, _const_spec(e_tab.shape), _const_spec(d_tab.shape)],
        out_specs=pl.BlockSpec((1, t_len, cw), lambda bi, j: (bi, 0, j)),
        out_shape=jax.ShapeDtypeStruct((b, t_len, BRANCH_W), BF16),
        scratch_shapes=[pltpu.VMEM((cw // LANES, t_len, LANES), F32)] * 4,
        compiler_params=_cparams(("parallel", "arbitrary"), 48),
        name="fourier",
    )(zr, zi, e_tab, d_tab)


def _pool_rows(ext, pos, n, w_ref, scale):
    n_ext = ext.shape[0]
    rows = n_ext - 2 * POOL_HALO
    back = lambda v, k: pltpu.roll(v, k, 0)
    fwd = lambda v, k: pltpu.roll(v, n_ext - k, 0)
    outs = []
    for gi, w in enumerate(POOL_WINDOWS):
        e = ext[:, gi * GROUP_W:(gi + 1) * GROUP_W]
        wsum = e + back(e, 1)
        half = 1
        while 2 * half < w:
            wsum = back(wsum, half) + fwd(wsum, half)
            half *= 2
        own = slice(POOL_HALO, POOL_HALO + rows)
        cnt = jnp.minimum(pos + w // 2, n) - jnp.maximum(pos - w // 2, 0)
        pooled = wsum[own] / cnt.astype(F32) - e[own]
        outs.append(jnp.dot(pooled.astype(BF16), w_ref[gi], preferred_element_type=F32))
    return (jnp.concatenate(outs, axis=-1) * scale).astype(BF16)


def _route(logits_t, bias):
    aff = jax.nn.sigmoid(logits_t)
    sel = aff + bias
    neg = -jnp.inf
    firsts, seconds, scores = [], [], []
    for g in range(N_EXPERTS // EXPERTS_PER_GROUP):
        s = [sel[EXPERTS_PER_GROUP * g + k:EXPERTS_PER_GROUP * g + k + 1, :] for k in range(EXPERTS_PER_GROUP)]
        m1 = jnp.maximum(jnp.maximum(s[0], s[1]), jnp.maximum(s[2], s[3]))
        i1 = jnp.where(s[0] == m1, 0, jnp.where(s[1] == m1, 1, jnp.where(s[2] == m1, 2, 3)))
        r = [jnp.where(i1 == k, neg, s[k]) for k in range(EXPERTS_PER_GROUP)]
        m2 = jnp.maximum(jnp.maximum(r[0], r[1]), jnp.maximum(r[2], r[3]))
        i2 = jnp.where(r[0] == m2, 0, jnp.where(r[1] == m2, 1, jnp.where(r[2] == m2, 2, 3)))
        firsts.append(i1 + EXPERTS_PER_GROUP * g)
        seconds.append(i2 + EXPERTS_PER_GROUP * g)
        scores.append(m1 + m2)
    best = jnp.maximum(jnp.maximum(scores[0], scores[1]), jnp.maximum(scores[2], scores[3]))
    pick = lambda v: jnp.where(scores[0] == best, v[0], jnp.where(scores[1] == best, v[1],
                                                                 jnp.where(scores[2] == best, v[2], v[3])))
    e1, e2 = pick(firsts), pick(seconds)
    lo = jnp.minimum(e1, e2) & (EXPERTS_PER_GROUP - 1)
    hi = jnp.maximum(e1, e2) & (EXPERTS_PER_GROUP - 1)
    pair = jnp.where(lo == 0, 0, jnp.where(lo == 1, 3, 5)) + hi - lo - 1
    cls = ((e1 >> 2) * PAIRS_PER_GROUP + pair).astype(F32)
    return jnp.concatenate([cls] + [jnp.zeros_like(cls)] * (ROUTE_ROWS - 1), axis=0)


def _pack_bf16_pairs(v):
    w = v.shape[1] // 2
    bits = pltpu.bitcast(v.astype(BF16).astype(F32), jnp.uint32)
    return pltpu.bitcast(bits[:, :w] | (bits[:, w:] >> 16), jnp.int32)


def _unpack_bf16_pairs(p):
    bits = pltpu.bitcast(p, jnp.uint32)
    hi = pltpu.bitcast(bits & jnp.uint32(0xFFFF0000), F32)
    lo = pltpu.bitcast(bits << 16, F32)
    return jnp.concatenate([hi, lo], axis=-1)


def _merge_kernel(*refs, n_real, n_fill, pending, **static):
    n_tok = 3 if pending else 1
    (mod_ref, n1_ref, n2_ref, a_ref, b_ref, c_ref, pc_ref, pp_ref, pn_ref, pw_ref, psc_ref,
     wg_ref, bg_ref, wb_ref, wo_ref, rw_ref, rb_ref) = refs[n_tok:n_tok + 17]
    outs = refs[-4:-1]
    ins = (refs[:n_tok], mod_ref, n1_ref, n2_ref, a_ref, b_ref, c_ref,
           (pc_ref, pp_ref, pn_ref, pw_ref, psc_ref),
           wg_ref, bg_ref, wb_ref, wo_ref, rw_ref, rb_ref)
    if n_fill:
        @pl.when(pl.program_id(1) >= n_real)
        def _():
            for ref in outs:
                ref[...] = jnp.zeros_like(ref)

        pl.when(pl.program_id(1) < n_real)(lambda: _merge_tile(ins, outs, refs[-1], **static))
    else:
        _merge_tile(ins, outs, refs[-1], **static)


def _merge_tile(ins, outs, merged_s, *, nw, sub, p_off, seq_lo, seq_hi):
    (tok_refs, mod_ref, n1_ref, n2_ref, a_ref, b_ref, c_ref, pool_refs,
     wg_ref, bg_ref, wb_ref, wo_ref, rw_ref, rb_ref) = ins
    xo_ref, h2_ref, route_ref = outs
    pc_ref, pp_ref, pn_ref, pw_ref, psc_ref = pool_refs
    x_ref = tok_refs[0]
    tm = x_ref.shape[1]
    sh1, sc1, g1, sh2, sc2, _ = _mod_slices(mod_ref[0])
    row0 = p_off + pl.program_id(1) * tm
    ext = jnp.concatenate([pp_ref[0], pc_ref[0], pn_ref[0]], axis=0).astype(F32)
    gpos = row0 - POOL_HALO + lax.broadcasted_iota(jnp.int32, (tm + 2 * POOL_HALO, 1), 0)
    ext = jnp.where((gpos >= seq_lo) & (gpos < seq_hi), ext, 0.0)
    for r in range(tm // sub):
        rows = slice(r * sub, (r + 1) * sub)
        x = x_ref[0, rows, :]
        if len(tok_refs) == 3:
            x = _pending_residual(x, tok_refs[1], tok_refs[2], rows)
        hb = _norm_mod(x, n1_ref[...], sh1, sc1).astype(BF16)
        pos = row0 - seq_lo + r * sub + lax.broadcasted_iota(jnp.int32, (sub, 1), 0)
        pooled = _pool_rows(ext[r * sub:(r + 1) * sub + 2 * POOL_HALO], pos, seq_hi - seq_lo,
                            pw_ref, psc_ref[...])
        branches = (a_ref[0, rows, :], b_ref[0, rows, :], c_ref[0, rows, :], pooled)
        for n in range(D_MODEL // nw):
            cols = slice(n * nw, (n + 1) * nw)
            merged = None
            for i, br in enumerate(branches):
                gate = jax.nn.sigmoid(
                    jnp.dot(hb, wg_ref[i, :, cols], preferred_element_type=F32) + bg_ref[i, :, cols])
                term = gate * jnp.dot(br, wb_ref[i, :, cols], preferred_element_type=F32)
                merged = term if merged is None else merged + term
            merged_s[rows, cols] = merged.astype(BF16)
        y = jnp.dot(merged_s[rows, :], wo_ref[...], preferred_element_type=F32)
        xn = x + g1 * y
        xo_ref[0, rows, :] = xn
        h2 = _norm_mod(xn, n2_ref[...], sh2, sc2)
        h2_ref[0, rows, :] = _pack_bf16_pairs(h2)
        h_hi = h2.astype(BF16)
        h_lo = (h2 - h_hi.astype(F32)).astype(BF16)
        nt = (((1,), (1,)), ((), ()))
        by_hi = lax.dot_general(rw_ref[...], h_hi, nt, preferred_element_type=F32)
        by_lo = lax.dot_general(rw_ref[:N_EXPERTS, :], h_lo, nt, preferred_element_type=F32)
        logits_t = by_hi[:N_EXPERTS] + by_hi[N_EXPERTS:] + by_lo
        route_ref[0, :, rows] = _route(logits_t, rb_ref[...])


def _merge(tokens, mod, mod_row, n1, n2, branches, offsets, p_in, pool_params, seq, weights, s_out, out_off,
           rows, tm, prev=None):
    pending = len(tokens) == 3
    b, _, d = tokens[0].shape
    wg, bg, wb, wo, rw_t, rb = weights
    pool_w, pool_scale = pool_params
    seq_lo, seq_hi = seq
    n_real = rows // tm
    n_fill = -(-(s_out - out_off - rows) // tm) if prev is None else 0
    step = lambda j: jnp.minimum(j, n_real - 1)
    blk = lambda width, off: pl.BlockSpec((1, tm, width), lambda bi, j: (bi, off // tm + step(j), 0))
    out_blk = lambda width: pl.BlockSpec((1, tm, width), lambda bi, j: (bi, out_off // tm + j, 0))
    assert all(o % tm == 0 for o in offsets) and out_off % tm == 0 and rows % tm == 0 and seq_lo % tm == 0
    hb = tm // POOL_HALO
    last_halo = p_in.shape[1] // POOL_HALO - 1
    halo = lambda shift: pl.BlockSpec(
        (1, POOL_HALO, BRANCH_W),
        lambda bi, j: (bi, jnp.clip((seq_lo // tm + step(j) + shift) * hb - 1 + shift, 0, last_halo), 0))
    out_shape = [
        jax.ShapeDtypeStruct((b, s_out, d), F32),
        jax.ShapeDtypeStruct((b, s_out, d // 2), jnp.int32),
        jax.ShapeDtypeStruct((b, ROUTE_ROWS, s_out), F32),
    ]
    mod_spec = pl.BlockSpec((1, 1, 6 * d), lambda bi, j: (bi if mod_row is None else mod_row, 0, 0))
    tok_off = out_off if pending else 0
    token_specs = [blk(d, tok_off)] + ([blk(d // 2, tok_off), mod_spec] if pending else [])
    n_in = len(tokens) + 17
    extra_specs = [] if prev is None else [pl.BlockSpec(memory_space=pl.ANY)] * 3
    return pl.pallas_call(
        functools.partial(_merge_kernel, nw=512, sub=MERGE_CHAIN_ROWS, n_real=n_real, n_fill=n_fill,
                          pending=pending, p_off=seq_lo, seq_lo=seq_lo, seq_hi=seq_hi),
        scratch_shapes=[pltpu.VMEM((tm, d), BF16)],
        grid=(b, n_real + n_fill),
        in_specs=token_specs + [
            mod_spec,
            _const_spec((1, d)), _const_spec((1, d)),
            *[blk(BRANCH_W, off) for off in offsets],
            blk(BRANCH_W, seq_lo), halo(0), halo(1),
            _const_spec(pool_w.shape), _const_spec((1, BRANCH_W)),
            _const_spec(wg.shape), _const_spec(bg.shape), _const_spec(wb.shape), _const_spec(wo.shape),
            _const_spec(rw_t.shape), _const_spec(rb.shape),
        ] + extra_specs,
        out_specs=[
            out_blk(d),
            out_blk(d // 2),
            pl.BlockSpec((1, ROUTE_ROWS, tm), lambda bi, j: (bi, 0, out_off // tm + j)),
        ],
        out_shape=out_shape,
        input_output_aliases={} if prev is None else {n_in + i: i for i in range(3)},
        compiler_params=_cparams(("parallel", "arbitrary"), 56),
        name="merge",
    )(*tokens, mod, n1, n2, *branches, p_in, p_in, p_in, pool_w, pool_scale,
      wg, bg, wb, wo, rw_t, rb, *(() if prev is None else prev))


def _rank_kernel(cls_ref, rank_ref, cnt_ref, cnt_s, *, tr):
    @pl.when(pl.program_id(0) == 0)
    def _():
        cnt_s[...] = jnp.zeros_like(cnt_s)

    cls = cls_ref[0]
    cid = lax.broadcasted_iota(jnp.int32, (CLASS_ROWS, tr), 0).astype(F32)
    onehot = cid == cls
    before = lax.broadcasted_iota(jnp.int32, (tr, tr), 0) < lax.broadcasted_iota(jnp.int32, (tr, tr), 1)
    prefix = jnp.dot(jnp.where(onehot, 1.0, 0.0).astype(BF16), jnp.where(before, 1.0, 0.0).astype(BF16),
                     preferred_element_type=F32)
    carry = cnt_s[...][:, 0:1]
    rank_ref[0] = jnp.sum(jnp.where(onehot, prefix + carry, 0.0), axis=0, keepdims=True)
    cnt_s[...] += jnp.sum(jnp.where(onehot, 1.0, 0.0), axis=1, keepdims=True)
    cnt_ref[...] = cnt_s[...]


def _rank(cls_flat, tr=512):
    n = cls_flat.shape[0]
    tr = math.gcd(n, tr)
    cls3 = cls_flat.reshape(n // tr, 1, tr)
    rank, cnt = pl.pallas_call(
        functools.partial(_rank_kernel, tr=tr),
        grid=(n // tr,),
        in_specs=[pl.BlockSpec((1, 1, tr), lambda i: (i, 0, 0))],
        out_specs=[pl.BlockSpec((1, 1, tr), lambda i: (i, 0, 0)), _const_spec((CLASS_ROWS, LANES))],
        out_shape=[jax.ShapeDtypeStruct((n // tr, 1, tr), F32), jax.ShapeDtypeStruct((CLASS_ROWS, LANES), F32)],
        scratch_shapes=[pltpu.VMEM((CLASS_ROWS, LANES), F32)],
        compiler_params=_cparams(("arbitrary",), 32),
        name="rank",
    )(cls3)
    return rank.reshape(n), cnt[:N_CLASSES, 0]


def _sc_layout(n):
    info = plsc.get_sparse_core_info()
    nw = info.num_cores * info.num_subcores
    per_worker = n // nw
    assert per_worker * nw == n
    chunk = max(c for c in range(8, SC_MAX_CHUNK + 1, 8) if per_worker % c == 0)
    return info.num_cores, nw, per_worker // chunk, chunk


def _sc_scatter_rows(src, pos, n_out):
    n, w = src.shape
    nc, nw, k, c = _sc_layout(n)
    mesh = plsc.VectorSubcoreMesh(core_axis_name="c", subcore_axis_name="s")

    @functools.partial(
        pl.kernel, mesh=mesh,
        out_type=jax.ShapeDtypeStruct((n_out, w), src.dtype),
        scratch_types=[pltpu.VMEM((k, c), jnp.int32), pltpu.VMEM((c, w), src.dtype), pltpu.SemaphoreType.DMA],
        name="moe_scatter",
    )
    def scatter(src_hbm, pos_hbm, out_hbm, idx_v, rows_v, sem):
        wid = lax.axis_index("s") * nc + lax.axis_index("c")
        pltpu.sync_copy(pos_hbm.at[wid], idx_v)

        @pl.loop(0, k)
        def _(j):
            off = pl.multiple_of(wid * (k * c) + j * c, 8)
            pltpu.sync_copy(src_hbm.at[pl.ds(off, c)], rows_v)
            pltpu.async_copy(rows_v, out_hbm.at[idx_v.at[j]], sem).wait()

    return scatter(src, pos.reshape(nw, k, c))


def _sc_gather_rows(src, pos):
    n = pos.shape[0]
    w = src.shape[1]
    nc, nw, k, c = _sc_layout(n)
    mesh = plsc.VectorSubcoreMesh(core_axis_name="c", subcore_axis_name="s")

    @functools.partial(
        pl.kernel, mesh=mesh,
        out_type=jax.ShapeDtypeStruct((n, w), src.dtype),
        scratch_types=[pltpu.VMEM((k, c), jnp.int32), pltpu.VMEM((c, w), src.dtype), pltpu.SemaphoreType.DMA],
        name="moe_gather",
    )
    def gather(src_hbm, pos_hbm, out_hbm, idx_v, rows_v, sem):
        wid = lax.axis_index("s") * nc + lax.axis_index("c")
        pltpu.sync_copy(pos_hbm.at[wid], idx_v)

        @pl.loop(0, k)
        def _(j):
            off = pl.multiple_of(wid * (k * c) + j * c, 8)
            pltpu.async_copy(src_hbm.at[idx_v.at[j]], rows_v, sem).wait()
            pltpu.sync_copy(rows_v, out_hbm.at[pl.ds(off, c)])

    return gather(src, pos.reshape(nw, k, c))


def _gmm_kernel(lo_ref, hi_ref, new_ref, nact_ref, h_ref, rw_ref,
                w1a_ref, w1b_ref, w3a_ref, w3b_ref, w2a_ref, w2b_ref, o_ref, w13_s, w2_s):
    t = pl.program_id(0)

    @pl.when(new_ref[t] == 1)
    def _():
        for i, (w1_ref, w3_ref, w2_ref) in enumerate(((w1a_ref, w3a_ref, w2a_ref), (w1b_ref, w3b_ref, w2b_ref))):
            w13_s[i, :, :EXPERT_FF] = w1_ref[0, 0].astype(BF16)
            w13_s[i, :, EXPERT_FF:] = w3_ref[0, 0].astype(BF16)
            w2_s[i] = w2_ref[0, 0].astype(BF16)

    @pl.when(t < nact_ref[0])
    def _():
        xf = _unpack_bf16_pairs(h_ref[...])
        x = xf.astype(BF16)

        def expert(i):
            ab = jnp.dot(x, w13_s[i], preferred_element_type=F32)
            a, gate = ab[:, :EXPERT_FF], ab[:, EXPERT_FF:]
            hid = (a * jax.nn.sigmoid(a)) * gate
            return jnp.dot(hid.astype(BF16), w2_s[i], preferred_element_type=F32)

        aff = [jax.nn.sigmoid(jnp.sum(xf * rw_ref[pl.ds(e_ref[t], 1), :], axis=-1, keepdims=True))
               for e_ref in (lo_ref, hi_ref)]
        total = aff[0] + aff[1]
        y = (aff[0] / total) * expert(0) + (aff[1] / total) * expert(1)
        o_ref[...] = _pack_bf16_pairs(y)


def _gmm(tile_lo, tile_hi, tile_new, n_act, hs, rw_t, w1, w3, w2, layer, tm):
    n_pad, half = hs.shape
    d = 2 * half
    row = lambda t, lo, hi, new, na: (jnp.minimum(t, na[0] - 1), 0)
    e_lo = lambda t, lo, hi, new, na: (layer, lo[jnp.minimum(t, na[0] - 1)], 0, 0)
    e_hi = lambda t, lo, hi, new, na: (layer, hi[jnp.minimum(t, na[0] - 1)], 0, 0)
    up = lambda e: pl.BlockSpec((1, 1, d, EXPERT_FF), e)
    down = lambda e: pl.BlockSpec((1, 1, EXPERT_FF, d), e)
    return pl.pallas_call(
        _gmm_kernel,
        grid_spec=pltpu.PrefetchScalarGridSpec(
            num_scalar_prefetch=4,
            grid=(n_pad // tm,),
            in_specs=[
                pl.BlockSpec((tm, half), row),
                pl.BlockSpec(rw_t.shape, lambda t, lo, hi, new, na: (0, 0)),
                up(e_lo), up(e_hi), up(e_lo), up(e_hi), down(e_lo), down(e_hi),
            ],
            out_specs=pl.BlockSpec((tm, half), row),
            scratch_shapes=[pltpu.VMEM((2, d, 2 * EXPERT_FF), BF16), pltpu.VMEM((2, EXPERT_FF, d), BF16)],
        ),
        out_shape=jax.ShapeDtypeStruct((n_pad, half), jnp.int32),
        compiler_params=_cparams(("arbitrary",), 56),
        name="moe_gmm",
    )(tile_lo, tile_hi, tile_new, n_act, hs, rw_t, w1, w1, w3, w3, w2, w2)


def _moe_routed(h2p, route, rw_t, w1, w3, w2, layer, tm=256):
    b, s, half = h2p.shape
    n = b * s
    n_pad = n + N_CLASSES * tm
    cls = route[:, 0, :].reshape(n)
    rank, counts = _rank(cls)
    counts = counts.astype(jnp.int32)
    padded = (counts + tm - 1) // tm * tm
    ends = jnp.cumsum(padded)
    pos = jnp.take(ends - padded, cls.astype(jnp.int32)) + rank.astype(jnp.int32)
    n_act = (ends[-1] // tm).reshape(1)
    tile_row = jnp.arange(n_pad // tm, dtype=jnp.int32) * tm
    tile_cls = jnp.minimum(jnp.sum(tile_row[:, None] >= ends[None, :], axis=1), N_CLASSES - 1)
    pair_lo, pair_hi = (jnp.asarray(a, jnp.int32) for a in _class_experts())
    hs = _sc_scatter_rows(h2p.reshape(n, half), pos, n_pad)
    prev_cls = jnp.concatenate([jnp.full((1,), -1, tile_cls.dtype), tile_cls[:-1]])
    tile_new = ((tile_cls != prev_cls) & (tile_row < ends[-1])).astype(jnp.int32)
    ys = _gmm(jnp.take(pair_lo, tile_cls), jnp.take(pair_hi, tile_cls), tile_new, n_act, hs, rw_t,
              w1, w3, w2, layer, tm)
    return _sc_gather_rows(ys, pos).reshape(b, s, half)


def _class_experts():
    lo, hi = [], []
    for g in range(N_EXPERTS // EXPERTS_PER_GROUP):
        for i in range(EXPERTS_PER_GROUP):
            for j in range(i + 1, EXPERTS_PER_GROUP):
                lo.append(EXPERTS_PER_GROUP * g + i)
                hi.append(EXPERTS_PER_GROUP * g + j)
    return np.array(lo), np.array(hi)


def _final_residual_kernel(x_ref, y_ref, mod_ref, g_ref, o_ref):
    x = _pending_residual(x_ref[0], y_ref, mod_ref)
    o_ref[0] = x * lax.rsqrt(jnp.mean(x * x, axis=-1, keepdims=True) + EPS) * g_ref[...]


def _final_residual(x1, yp, mod, gain, tm=512):
    b, t, d = x1.shape
    tok = lambda bi, j: (bi, j, 0)
    return pl.pallas_call(
        _final_residual_kernel,
        grid=(b, t // tm),
        in_specs=[
            pl.BlockSpec((1, tm, d), tok),
            pl.BlockSpec((1, tm, d // 2), tok),
            pl.BlockSpec((1, 1, 6 * d), lambda bi, j: (bi, 0, 0)),
            _const_spec((1, d)),
        ],
        out_specs=pl.BlockSpec((1, tm, d), tok),
        out_shape=jax.ShapeDtypeStruct((b, t, d), F32),
        compiler_params=_cparams(("parallel", "arbitrary"), 32),
        name="final_residual",
    )(x1, yp, mod, gain)


def _rope_tables(t_lat, n_ctx):
    rows = t_lat // GRID_W
    row = jnp.repeat(jnp.arange(rows, dtype=F32), GRID_W)
    col = jnp.tile(jnp.arange(GRID_W, dtype=F32), rows)
    inv_freq = ROPE_THETA ** (-jnp.arange(0, ROPE_AXIS_DIM, 2, dtype=F32) / ROPE_AXIS_DIM)
    ang = jnp.stack([row[:, None] * inv_freq, col[:, None] * inv_freq], axis=1)
    cos, sin = jnp.cos(ang), jnp.sin(ang)
    zero = jnp.zeros_like(sin)
    cos_h = jnp.concatenate([cos, cos], axis=-1).reshape(t_lat, HEAD_DIM)
    sa_h = jnp.concatenate([-sin, zero], axis=-1).reshape(t_lat, HEAD_DIM)
    sb_h = jnp.concatenate([zero, sin], axis=-1).reshape(t_lat, HEAD_DIM)
    reps = LANES // HEAD_DIM
    pad = lambda tbl, fill: jnp.concatenate(
        [jnp.tile(tbl, (1, reps)), jnp.full((n_ctx, LANES), fill, F32)], axis=0)
    return pad(cos_h, 1.0), pad(sa_h, 0.0), pad(sb_h, 0.0)


def _permute_heads(w, axis):
    shp = w.shape
    w = w.reshape(shp[:axis] + (N_Q_HEADS, HEAD_DIM) + shp[axis + 1:])
    w = jnp.take(w, jnp.array(HEAD_PERM), axis=axis)
    return w.reshape(shp)


def _square_factor(n):
    r = int(round(math.sqrt(n)))
    assert r * r == n, "sequence lengths must be perfect squares for the two-stage DFT"
    return r


def kernel(x, c, ctx, c_ctx, w_ada, b_ada, norm1, norm2, w_in, q_gain, k_gain, sink, pool_w, pool_scale,
           w_branch, w_gate, b_gate, w_out, router_w, router_bias, w1, w3, w2, norm_f):
    b, t_lat, d = x.shape
    n_ctx = ctx.shape[1]
    s = t_lat + n_ctx
    depth = w_ada.shape[0]
    assert d == D_MODEL and b < MOD_ROWS and t_lat % 256 == 0 and n_ctx % 256 == 0 and t_lat % n_ctx == 0

    tokens = (x, ctx)
    cc = jnp.zeros((MOD_ROWS, d), F32).at[:b].set(c).at[b].set(c_ctx)
    mod_all = _ada(cc, w_ada, b_ada).reshape(depth, MOD_ROWS, 1, 6 * d)

    cos, sa, sb = _rope_tables(t_lat, n_ctx)
    seg = jnp.asarray(np.kron(np.eye(N_Q_HEADS), np.full((HEAD_DIM, HEAD_DIM), 1.0 / HEAD_DIM)), BF16)
    cs = jnp.asarray(_channel_dft_table()).astype(BF16)
    f_lat = [jnp.asarray(a).astype(BF16) for a in _fourier_tables(*(_square_factor(t_lat),) * 2)]
    f_ctx = [jnp.asarray(a).astype(BF16) for a in _fourier_tables(*(_square_factor(n_ctx),) * 2)]
    wbias = jnp.asarray(_window_bias(n_ctx))
    rw_hi = router_w.T.astype(BF16)
    rw_t = jnp.concatenate([rw_hi, (router_w.T - rw_hi.astype(F32)).astype(BF16)], axis=0)
    rb = router_bias.reshape(N_EXPERTS, 1)

    for l in range(depth):
        need_ctx = l < depth - 1
        s_out = s if need_ctx else t_lat
        cols = jnp.split(w_in[l], np.cumsum((512, 512, 512, 128, 128, 512, 128))[:], axis=1)
        f_w, p_w, qb_w, kb_w, vb_w, qw_w, kw_w, vw_w = cols
        w_in_l = jnp.concatenate([f_w, p_w, _permute_heads(qb_w, 1), _permute_heads(qw_w, 1),
                                  kb_w, vb_w, kw_w, vw_w], axis=1).astype(BF16)
        wb_l = jnp.stack([w_branch[l, 0], _permute_heads(w_branch[l, 1], 0),
                          _permute_heads(w_branch[l, 2], 0), w_branch[l, 3]]).astype(BF16)
        mod = mod_all[l]
        n1 = norm1[l].reshape(1, d)
        n2 = norm2[l].reshape(1, d)
        qg = jnp.tile(q_gain[l], N_Q_HEADS).reshape(1, BRANCH_W)
        kg = jnp.tile(k_gain[l], LANES // HEAD_DIM).reshape(1, LANES)

        zr, zi, p_in, qb, qw, kvb, kvw = _inproj(tokens, t_lat, mod, n1, w_in_l, qg, kg, seg, cs, cos, sa, sb)

        out_a = _fourier(zr, zi, f_lat[0], f_lat[1], t_lat, 0)
        out_b = _gattn(qb, kvb, 0, t_lat, 0, s)
        out_c = _wattn(jnp.take(sink[l], jnp.array(HEAD_PERM)) * LOG2E, wbias, qw, kvw, s_out, t_lat)

        weights = (w_gate[l].astype(BF16), b_gate[l].reshape(4, 1, d), wb_l, w_out[l].astype(BF16), rw_t, rb)
        pool_params = (pool_w[l].astype(BF16), pool_scale[l].reshape(1, BRANCH_W))
        pending = len(tokens) == 3
        merged = _merge(tokens if pending else tokens[:1], mod, None, n1, n2, (out_a, out_b, out_c), (0, 0, 0),
                        p_in, pool_params, (0, t_lat), weights, s_out, 0, t_lat, 2 * MERGE_CHAIN_ROWS)
        if need_ctx:
            out_ac = _fourier(zr, zi, f_ctx[0], f_ctx[1], n_ctx, t_lat // n_ctx)
            out_bc = _gattn(qb, kvb, t_lat, n_ctx, t_lat, n_ctx)
            merged = _merge(tokens if pending else tokens[1:], mod, b, n1, n2, (out_ac, out_bc, out_c),
                            (0, 0, t_lat), p_in, pool_params, (t_lat, s), weights, s_out, t_lat, n_ctx,
                            MERGE_CHAIN_ROWS, prev=merged)
        x1, h2p, route = merged
        yp = _moe_routed(h2p, route, router_w.T, w1, w3, w2, l)
        tokens = (x1, yp, mod)

    return _final_residual(*tokens, norm_f.reshape(1, d))
```

```python
import functools
import math

import numpy as np
import jax
import jax.numpy as jnp
from jax import lax
from jax.experimental import pallas as pl
from jax.experimental.pallas import tpu as pltpu
from jax.experimental.pallas import tpu_sc as plsc

F32 = jnp.float32
BF16 = jnp.bfloat16

D_MODEL = 1024
HEAD_DIM = 64
N_Q_HEADS = 8
N_KV_HEADS = 2
GRID_W = 64
ROPE_THETA = 10000.0
ROPE_AXIS_DIM = HEAD_DIM // 2
QBLK = 128
WINDOW = 128
BRANCH_W = 512
GROUP_W = 128
POOL_WINDOWS = (2, 4, 8, 16)
N_EXPERTS = 16
EXPERTS_PER_GROUP = 4
EXPERT_FF = 512
EPS = 1e-6
MOD_ROWS = 16
NEG_BIG = -1e30
LOG2E = math.log2(math.e)
LANES = 128
POOL_HALO = 16
PAIRS_PER_GROUP = 6
N_CLASSES = 24
CLASS_ROWS = 32
ROUTE_ROWS = 8
SC_MAX_CHUNK = 128
WATTN_QBLOCKS = 2
MERGE_CHAIN_ROWS = 256

HEAD_PERM = (0, 4, 1, 5, 2, 6, 3, 7)


def _cparams(sem, vmem_mb):
    return pltpu.CompilerParams(dimension_semantics=sem, vmem_limit_bytes=vmem_mb * 1024 * 1024)


def _const_spec(shape):
    nd = len(shape)
    return pl.BlockSpec(shape, lambda *_: (0,) * nd)


def _ada_kernel(c_ref, w_ref, b_ref, o_ref):
    c = c_ref[...]
    s = c * jax.nn.sigmoid(c)
    o_ref[0] = jnp.dot(s.astype(BF16), w_ref[0].astype(BF16), preferred_element_type=F32) + b_ref[0]


def _ada(cc, w_ada, b_ada):
    depth, d, n = w_ada.shape
    tn = 1536
    return pl.pallas_call(
        _ada_kernel,
        grid=(depth, n // tn),
        in_specs=[
            pl.BlockSpec((MOD_ROWS, d), lambda l, j: (0, 0)),
            pl.BlockSpec((1, d, tn), lambda l, j: (l, 0, j)),
            pl.BlockSpec((1, 1, tn), lambda l, j: (l, 0, j)),
        ],
        out_specs=pl.BlockSpec((1, MOD_ROWS, tn), lambda l, j: (l, 0, j)),
        out_shape=jax.ShapeDtypeStruct((depth, MOD_ROWS, n), F32),
        compiler_params=_cparams(("arbitrary", "arbitrary"), 40),
        name="ada",
    )(cc, w_ada, b_ada.reshape(depth, 1, n))


def _norm_mod(x, gain, shift, scale):
    ms = jnp.mean(x * x, axis=-1, keepdims=True)
    return (x * lax.rsqrt(ms + EPS) * gain) * (1.0 + scale) + shift


def _mod_slices(m):
    d = D_MODEL
    return [m[:, i * d:(i + 1) * d] for i in range(6)]


def _head_norm(z, seg, gain):
    ms = jnp.dot((z * z).astype(BF16), seg, preferred_element_type=F32)
    return z * lax.rsqrt(ms + EPS) * gain


def _rope(z, cos, sin):
    outs = []
    for c in range(z.shape[1] // LANES):
        zc = z[:, c * LANES:(c + 1) * LANES]
        outs.append(zc * cos + pltpu.roll(zc, LANES // 2, 1) * sin)
    return outs[0] if len(outs) == 1 else jnp.concatenate(outs, axis=-1)


def _stream_specs(tm, d, n_lat):
    return [pl.BlockSpec((1, tm, d), lambda bi, j: (bi, jnp.minimum(j, n_lat - 1), 0)),
            pl.BlockSpec((1, tm, d), lambda bi, j: (bi, jnp.maximum(j - n_lat, 0), 0))]


def _pending_residual(x, y_ref, modp_ref, rows=slice(None)):
    return x + _mod_slices(modp_ref[0])[5] * _unpack_bf16_pairs(y_ref[0, rows, :])


def _inproj_kernel(*refs, n_lat, pending):
    if pending:
        x_ref, y_ref, modp_ref = refs[:3]
        x = _pending_residual(x_ref[0], y_ref, modp_ref)
    else:
        xl_ref, xc_ref = refs[:2]
        x = jnp.where(pl.program_id(1) >= n_lat, xc_ref[0], xl_ref[0])
    (mod_ref, n1_ref, w_ref, qg_ref, kg_ref, seg_ref, cs_ref, cos_ref, sin_ref,
     zr_ref, zi_ref, p_ref, qb_ref, qw_ref, kvb_ref, kvw_ref) = refs[3 if pending else 2:]
    sh1, sc1 = _mod_slices(mod_ref[0])[:2]
    h = _norm_mod(x, n1_ref[...], sh1, sc1)
    u = jnp.dot(h.astype(BF16), w_ref[...], preferred_element_type=F32)
    cos, sin = cos_ref[...], sin_ref[...]
    w = BRANCH_W
    seg = seg_ref[...]
    qb = _rope(_head_norm(u[:, 0:w], seg, qg_ref[...]), cos, sin)
    qb_ref[0] = (qb * (HEAD_DIM ** -0.5 * LOG2E)).astype(BF16)
    qw = _rope(u[:, w:2 * w], cos, sin)
    qw_ref[0] = (qw * (HEAD_DIM ** -0.5 * LOG2E)).astype(BF16)
    o = 2 * w
    kb = _rope(_head_norm(u[:, o:o + LANES], seg[:LANES, :LANES], kg_ref[...]), cos, sin)
    kw = _rope(u[:, o + LANES:o + 2 * LANES], cos, sin)
    o += 2 * LANES
    f_in = u[:, o:o + w].astype(BF16)
    zr, zi = [], []
    for g in range(w // GROUP_W):
        z = jnp.dot(f_in[:, g * GROUP_W:(g + 1) * GROUP_W], cs_ref[...], preferred_element_type=F32)
        zr.append(z[:, :GROUP_W])
        zi.append(z[:, GROUP_W:])
    zr_ref[0] = jnp.concatenate(zr, axis=-1).astype(BF16)
    zi_ref[0] = jnp.concatenate(zi, axis=-1).astype(BF16)
    o += w
    vb = u[:, o:o + LANES]
    vw = u[:, o + LANES:o + 2 * LANES]
    kvb_ref[0] = jnp.concatenate([kb, vb, jnp.ones_like(vb)], axis=-1).astype(BF16)
    kvw_ref[0] = jnp.concatenate([kw, vw, jnp.ones_like(vw)], axis=-1).astype(BF16)
    p_ref[0] = u[:, o + 2 * LANES:o + 2 * LANES + w].astype(BF16)


def _inproj(tokens, t_lat, mod, n1, w_in, qg, kg, seg, cs, cos, sin, tm=256):
    pending = len(tokens) == 3
    b, _, d = tokens[0].shape
    s = tokens[0].shape[1] if pending else t_lat + tokens[1].shape[1]
    nw = w_in.shape[1]
    n_lat = t_lat // tm
    tok = lambda bi, j: (bi, j, 0)
    tab = lambda bi, j: (j, 0)
    mod_spec = pl.BlockSpec((1, 1, 6 * d), lambda bi, j: (jnp.where(j >= n_lat, b, bi), 0, 0))
    if pending:
        token_specs = [pl.BlockSpec((1, tm, d), tok), pl.BlockSpec((1, tm, d // 2), tok), mod_spec]
    else:
        token_specs = _stream_specs(tm, d, n_lat)
    widths = (BRANCH_W,) * 5 + (3 * LANES, 3 * LANES)
    return pl.pallas_call(
        functools.partial(_inproj_kernel, n_lat=n_lat, pending=pending),
        grid=(b, s // tm),
        in_specs=token_specs + [
            mod_spec,
            _const_spec((1, d)),
            _const_spec((d, nw)),
            _const_spec((1, BRANCH_W)),
            _const_spec((1, LANES)),
            _const_spec((BRANCH_W, BRANCH_W)),
            _const_spec((GROUP_W, 2 * GROUP_W)),
            pl.BlockSpec((tm, LANES), tab),
            pl.BlockSpec((tm, LANES), tab),
        ],
        out_specs=[pl.BlockSpec((1, tm, wd), tok) for wd in widths],
        out_shape=[jax.ShapeDtypeStruct((b, s, wd), BF16) for wd in widths],
        compiler_params=_cparams(("parallel", "arbitrary"), 48),
        name="inproj",
    )(*tokens, mod, n1, w_in, qg, kg, seg, cs, cos, sin)


def _split_heads(qc, lane):
    first = (lane & (HEAD_DIM // 2)) == 0
    zero = jnp.zeros_like(qc)
    return jnp.concatenate([jnp.where(first, qc, zero), jnp.where(first, zero, qc)], axis=0)


def _gattn_kernel(q_ref, kv_ref, o_ref, q2_s, acc_s, *, sub):
    tq = q_ref.shape[1]
    lane = lax.broadcasted_iota(jnp.int32, (1, LANES), 1)
    nt = (((1,), (1,)), ((), ()))
    n_chunks = BRANCH_W // LANES
    for c in range(n_chunks):
        q2_s[2 * c * tq:(2 * c + 2) * tq, :] = _split_heads(q_ref[0, :, c * LANES:(c + 1) * LANES], lane)
    k = kv_ref[0, :, 0:LANES]
    v = kv_ref[0, :, LANES:3 * LANES]
    for r in range(2 * n_chunks * tq // sub):
        rows = slice(r * sub, (r + 1) * sub)
        s = lax.dot_general(q2_s[rows, :], k, nt, preferred_element_type=F32)
        p = jnp.exp2(s - jnp.max(s, axis=-1, keepdims=True))
        acc_s[rows, :] = jnp.dot(p.astype(BF16), v, preferred_element_type=F32)
    for c in range(n_chunks):
        lo = acc_s[2 * c * tq:(2 * c + 1) * tq, :]
        hi = acc_s[(2 * c + 1) * tq:(2 * c + 2) * tq, :]
        o = jnp.where(lane < HEAD_DIM, lo[:, :LANES] / lo[:, LANES:], hi[:, :LANES] / hi[:, LANES:])
        o_ref[0, :, c * LANES:(c + 1) * LANES] = o.astype(BF16)


def _gattn(qb, kv, q_start, q_len, k_start, k_len, tq=256, sub=128):
    b = qb.shape[0]
    assert q_start % tq == 0 and q_len % tq == 0 and k_start % k_len == 0 and k_len % LANES == 0
    rows = 2 * tq * (BRANCH_W // LANES)
    return pl.pallas_call(
        functools.partial(_gattn_kernel, sub=sub),
        grid=(b, q_len // tq),
        in_specs=[
            pl.BlockSpec((1, tq, BRANCH_W), lambda bi, j: (bi, q_start // tq + j, 0)),
            pl.BlockSpec((1, k_len, 3 * LANES), lambda bi, j: (bi, k_start // k_len, 0)),
        ],
        out_specs=pl.BlockSpec((1, tq, BRANCH_W), lambda bi, j: (bi, j, 0)),
        out_shape=jax.ShapeDtypeStruct((b, q_len, BRANCH_W), BF16),
        scratch_shapes=[pltpu.VMEM((rows, LANES), BF16), pltpu.VMEM((rows, 2 * LANES), F32)],
        compiler_params=_cparams(("parallel", "arbitrary"), 48),
        name="gattn",
    )(qb, kv)


def _window_bias(n_ctx):
    tq = WATTN_QBLOCKS * QBLK
    qi = np.arange(tq)[:, None]
    kj = np.arange(tq + 2 * QBLK)[None, :]
    band = np.abs(kj - WINDOW - qi) <= WINDOW
    blk = kj // QBLK
    variants = [band & (blk != 0), band, band & (blk != WATTN_QBLOCKS + 1), np.zeros_like(band)]
    out = [np.concatenate([np.ones((tq, n_ctx), bool), v], axis=1) for v in variants]
    return np.where(np.stack(out), 0.0, NEG_BIG).astype(np.float32)


def _wattn_kernel(sink_ref, bias_ref, q_ref, *refs, sub):
    o_ref, q2_s, kv_s, acc_s = refs[-4:]
    tq = q_ref.shape[1]
    off = 0
    for blk in refs[:-4]:
        kv_s[off:off + blk.shape[1], :] = blk[0]
        off += blk.shape[1]
    lane = lax.broadcasted_iota(jnp.int32, (1, LANES), 1)
    nt = (((1,), (1,)), ((), ()))
    n_chunks = BRANCH_W // LANES
    for c in range(n_chunks):
        q2_s[2 * c * tq:(2 * c + 2) * tq, :] = _split_heads(q_ref[0, :, c * LANES:(c + 1) * LANES], lane)
    k, v = kv_s[:, 0:LANES], kv_s[:, LANES:3 * LANES]
    for r in range(2 * n_chunks * tq // sub):
        rows = slice(r * sub, (r + 1) * sub)
        q_off = (r * sub) % tq
        sk = sink_ref[(r * sub) // tq]
        s = lax.dot_general(q2_s[rows, :], k, nt, preferred_element_type=F32) + bias_ref[0, q_off:q_off + sub, :]
        m = jnp.maximum(jnp.max(s, axis=-1, keepdims=True), sk)
        pv = jnp.dot(jnp.exp2(s - m).astype(BF16), v, preferred_element_type=F32)
        acc_s[rows, :LANES] = pv[:, :LANES]
        acc_s[rows, LANES:] = pv[:, LANES:] + jnp.exp2(sk - m)
    for c in range(n_chunks):
        lo = acc_s[2 * c * tq:(2 * c + 1) * tq, :]
        hi = acc_s[(2 * c + 1) * tq:(2 * c + 2) * tq, :]
        o = jnp.where(lane < HEAD_DIM, lo[:, :LANES] / lo[:, LANES:], hi[:, :LANES] / hi[:, LANES:])
        o_ref[0, :, c * LANES:(c + 1) * LANES] = o.astype(BF16)


def _wattn(sink, bias, qw, kv, s_out, t_lat):
    b, s, _ = qw.shape
    n_ctx = s - t_lat
    nq = WATTN_QBLOCKS
    tq = nq * QBLK
    assert n_ctx % tq == 0 and t_lat // tq >= 2
    last = s // QBLK - 1
    n_lat = t_lat // tq
    variant = lambda j: jnp.where(j >= n_lat, 3, jnp.where(j == 0, 0, jnp.where(j == n_lat - 1, 2, 1)))
    key_block = lambda off: pl.BlockSpec(
        (1, QBLK, 3 * LANES), lambda bi, j: (bi, jnp.clip(j * nq + off, 0, last), 0))
    rows = 2 * tq * (BRANCH_W // LANES)
    return pl.pallas_call(
        functools.partial(_wattn_kernel, sub=128),
        scratch_shapes=[pltpu.VMEM((rows, LANES), BF16), pltpu.VMEM((bias.shape[2], 3 * LANES), BF16),
                        pltpu.VMEM((rows, 2 * LANES), F32)],
        grid=(b, s_out // tq),
        in_specs=[
            pl.BlockSpec(memory_space=pltpu.SMEM),
            pl.BlockSpec((1,) + bias.shape[1:], lambda bi, j: (variant(j), 0, 0)),
            pl.BlockSpec((1, tq, BRANCH_W), lambda bi, j: (bi, j, 0)),
            pl.BlockSpec((1, n_ctx, 3 * LANES), lambda bi, j: (bi, t_lat // n_ctx, 0)),
        ] + [key_block(off) for off in range(-1, nq + 1)],
        out_specs=pl.BlockSpec((1, tq, BRANCH_W), lambda bi, j: (bi, j, 0)),
        out_shape=jax.ShapeDtypeStruct((b, s_out, BRANCH_W), BF16),
        compiler_params=_cparams(("parallel", "arbitrary"), 32),
        name="wattn",
    )(sink, bias, qw, *([kv] * (nq + 3)))


def _fourier_tables(n1, n2):
    t = n1 * n2
    k2 = np.arange(n2)[None, :, None]
    t2 = np.arange(n2)[None, None, :]
    t1 = np.arange(n1)[:, None, None]
    theta = 2.0 * np.pi * ((k2 * t2 * n1 + k2 * t1) % t) / t
    er, ei = np.cos(theta) / math.sqrt(n2), -np.sin(theta) / math.sqrt(n2)
    e = np.concatenate([np.concatenate([er, -ei], axis=2), np.concatenate([ei, er], axis=2)], axis=1)
    k1 = np.arange(n1)[:, None]
    phi = 2.0 * np.pi * ((k1 * np.arange(n1)[None, :]) % n1) / n1
    dcat = np.concatenate([np.cos(phi), np.sin(phi)], axis=1) / math.sqrt(n1)
    return e.astype(np.float32), dcat.astype(np.float32)


def _channel_dft_table():
    c = np.arange(GROUP_W)
    ang = 2.0 * np.pi * ((c[:, None] * c[None, :]) % GROUP_W) / GROUP_W
    return (np.concatenate([np.cos(ang), -np.sin(ang)], axis=1) / math.sqrt(GROUP_W)).astype(np.float32)


def _fourier_kernel(zr_ref, zi_ref, e_ref, d_ref, o_ref, xr_s, xi_s, yr_s, yi_s, *, n1, n2):
    nc = xr_s.shape[0]
    chunk = lambda c: slice(c * LANES, (c + 1) * LANES)

    def gather(ref, start, size, stride):
        return jnp.concatenate([ref[c, pl.ds(start, size, stride=stride), :] for c in range(nc)], axis=-1)

    for c in range(nc):
        xr_s[c] = zr_ref[0, :, chunk(c)].astype(F32)
        xi_s[c] = zi_ref[0, :, chunk(c)].astype(F32)
    for t1 in range(n1):
        xs = jnp.concatenate([gather(xr_s, t1, n2, n1), gather(xi_s, t1, n2, n1)], axis=0).astype(BF16)
        y = jnp.dot(e_ref[t1], xs, preferred_element_type=F32)
        for c in range(nc):
            yr_s[c, t1 * n2:(t1 + 1) * n2, :] = y[:n2, chunk(c)]
            yi_s[c, t1 * n2:(t1 + 1) * n2, :] = y[n2:, chunk(c)]
    for k2 in range(n2):
        ys = jnp.concatenate([gather(yr_s, k2, n1, n2), gather(yi_s, k2, n1, n2)], axis=0).astype(BF16)
        o = jnp.dot(d_ref[...], ys, preferred_element_type=F32)
        for c in range(nc):
            xr_s[c, pl.ds(k2, n1, stride=n2), :] = o[:, chunk(c)]
    for c in range(nc):
        o_ref[0, :, chunk(c)] = xr_s[c].astype(BF16)


def _fourier(zr, zi, e_tab, d_tab, t_len, row_block, cw=256):
    b = zr.shape[0]
    n1 = d_tab.shape[0]
    n2 = t_len // n1
    zspec = pl.BlockSpec((1, t_len, cw), lambda bi, j: (bi, row_block, j))
    return pl.pallas_call(
        functools.partial(_fourier_kernel, n1=n1, n2=n2),
        grid=(b, BRANCH_W // cw),
        in_specs=[zspec, zspec, _const_spec(e_tab.shape), _const_spec(d_tab.shape)],
        out_specs=pl.BlockSpec((1, t_len, cw), lambda bi, j: (bi, 0, j)),
        out_shape=jax.ShapeDtypeStruct((b, t_len, BRANCH_W), BF16),
        scratch_shapes=[pltpu.VMEM((cw // LANES, t_len, LANES), F32)] * 4,
        compiler_params=_cparams(("parallel", "arbitrary"), 48),
        name="fourier",
    )(zr, zi, e_tab, d_tab)


def _pool_rows(ext, pos, n, w_ref, scale):
    n_ext = ext.shape[0]
    rows = n_ext - 2 * POOL_HALO
    back = lambda v, k: pltpu.roll(v, k, 0)
    fwd = lambda v, k: pltpu.roll(v, n_ext - k, 0)
    outs = []
    for gi, w in enumerate(POOL_WINDOWS):
        e = ext[:, gi * GROUP_W:(gi + 1) * GROUP_W]
        wsum = e + back(e, 1)
        half = 1
        while 2 * half < w:
            wsum = back(wsum, half) + fwd(wsum, half)
            half *= 2
        own = slice(POOL_HALO, POOL_HALO + rows)
        cnt = jnp.minimum(pos + w // 2, n) - jnp.maximum(pos - w // 2, 0)
        pooled = wsum[own] / cnt.astype(F32) - e[own]
        outs.append(jnp.dot(pooled.astype(BF16), w_ref[gi], preferred_element_type=F32))
    return (jnp.concatenate(outs, axis=-1) * scale).astype(BF16)


def _route(logits_t, bias):
    aff = jax.nn.sigmoid(logits_t)
    sel = aff + bias
    neg = -jnp.inf
    firsts, seconds, scores = [], [], []
    for g in range(N_EXPERTS // EXPERTS_PER_GROUP):
        s = [sel[EXPERTS_PER_GROUP * g + k:EXPERTS_PER_GROUP * g + k + 1, :] for k in range(EXPERTS_PER_GROUP)]
        m1 = jnp.maximum(jnp.maximum(s[0], s[1]), jnp.maximum(s[2], s[3]))
        i1 = jnp.where(s[0] == m1, 0, jnp.where(s[1] == m1, 1, jnp.where(s[2] == m1, 2, 3)))
        r = [jnp.where(i1 == k, neg, s[k]) for k in range(EXPERTS_PER_GROUP)]
        m2 = jnp.maximum(jnp.maximum(r[0], r[1]), jnp.maximum(r[2], r[3]))
        i2 = jnp.where(r[0] == m2, 0, jnp.where(r[1] == m2, 1, jnp.where(r[2] == m2, 2, 3)))
        firsts.append(i1 + EXPERTS_PER_GROUP * g)
        seconds.append(i2 + EXPERTS_PER_GROUP * g)
        scores.append(m1 + m2)
    best = jnp.maximum(jnp.maximum(scores[0], scores[1]), jnp.maximum(scores[2], scores[3]))
    pick = lambda v: jnp.where(scores[0] == best, v[0], jnp.where(scores[1] == best, v[1],
                                                                 jnp.where(scores[2] == best, v[2], v[3])))
    e1, e2 = pick(firsts), pick(seconds)
    lo = jnp.minimum(e1, e2) & (EXPERTS_PER_GROUP - 1)
    hi = jnp.maximum(e1, e2) & (EXPERTS_PER_GROUP - 1)
    pair = jnp.where(lo == 0, 0, jnp.where(lo == 1, 3, 5)) + hi - lo - 1
    cls = ((e1 >> 2) * PAIRS_PER_GROUP + pair).astype(F32)
    return jnp.concatenate([cls] + [jnp.zeros_like(cls)] * (ROUTE_ROWS - 1), axis=0)


def _pack_bf16_pairs(v):
    w = v.shape[1] // 2
    bits = pltpu.bitcast(v.astype(BF16).astype(F32), jnp.uint32)
    return pltpu.bitcast(bits[:, :w] | (bits[:, w:] >> 16), jnp.int32)


def _unpack_bf16_pairs(p):
    bits = pltpu.bitcast(p, jnp.uint32)
    hi = pltpu.bitcast(bits & jnp.uint32(0xFFFF0000), F32)
    lo = pltpu.bitcast(bits << 16, F32)
    return jnp.concatenate([hi, lo], axis=-1)


def _merge_kernel(*refs, n_real, n_fill, pending, **static):
    n_tok = 3 if pending else 1
    (mod_ref, n1_ref, n2_ref, a_ref, b_ref, c_ref, pc_ref, pp_ref, pn_ref, pw_ref, psc_ref,
     wg_ref, bg_ref, wb_ref, wo_ref, rw_ref, rb_ref) = refs[n_tok:n_tok + 17]
    outs = refs[-4:-1]
    ins = (refs[:n_tok], mod_ref, n1_ref, n2_ref, a_ref, b_ref, c_ref,
           (pc_ref, pp_ref, pn_ref, pw_ref, psc_ref),
           wg_ref, bg_ref, wb_ref, wo_ref, rw_ref, rb_ref)
    if n_fill:
        @pl.when(pl.program_id(1) >= n_real)
        def _():
            for ref in outs:
                ref[...] = jnp.zeros_like(ref)

        pl.when(pl.program_id(1) < n_real)(lambda: _merge_tile(ins, outs, refs[-1], **static))
    else:
        _merge_tile(ins, outs, refs[-1], **static)


def _merge_tile(ins, outs, merged_s, *, nw, sub, p_off, seq_lo, seq_hi):
    (tok_refs, mod_ref, n1_ref, n2_ref, a_ref, b_ref, c_ref, pool_refs,
     wg_ref, bg_ref, wb_ref, wo_ref, rw_ref, rb_ref) = ins
    xo_ref, h2_ref, route_ref = outs
    pc_ref, pp_ref, pn_ref, pw_ref, psc_ref = pool_refs
    x_ref = tok_refs[0]
    tm = x_ref.shape[1]
    sh1, sc1, g1, sh2, sc2, _ = _mod_slices(mod_ref[0])
    row0 = p_off + pl.program_id(1) * tm
    ext = jnp.concatenate([pp_ref[0], pc_ref[0], pn_ref[0]], axis=0).astype(F32)
    gpos = row0 - POOL_HALO + lax.broadcasted_iota(jnp.int32, (tm + 2 * POOL_HALO, 1), 0)
    ext = jnp.where((gpos >= seq_lo) & (gpos < seq_hi), ext, 0.0)
    for r in range(tm // sub):
        rows = slice(r * sub, (r + 1) * sub)
        x = x_ref[0, rows, :]
        if len(tok_refs) == 3:
            x = _pending_residual(x, tok_refs[1], tok_refs[2], rows)
        hb = _norm_mod(x, n1_ref[...], sh1, sc1).astype(BF16)
        pos = row0 - seq_lo + r * sub + lax.broadcasted_iota(jnp.int32, (sub, 1), 0)
        pooled = _pool_rows(ext[r * sub:(r + 1) * sub + 2 * POOL_HALO], pos, seq_hi - seq_lo,
                            pw_ref, psc_ref[...])
        branches = (a_ref[0, rows, :], b_ref[0, rows, :], c_ref[0, rows, :], pooled)
        for n in range(D_MODEL // nw):
            cols = slice(n * nw, (n + 1) * nw)
            merged = None
            for i, br in enumerate(branches):
                gate = jax.nn.sigmoid(
                    jnp.dot(hb, wg_ref[i, :, cols], preferred_element_type=F32) + bg_ref[i, :, cols])
                term = gate * jnp.dot(br, wb_ref[i, :, cols], preferred_element_type=F32)
                merged = term if merged is None else merged + term
            merged_s[rows, cols] = merged.astype(BF16)
        y = jnp.dot(merged_s[rows, :], wo_ref[...], preferred_element_type=F32)
        xn = x + g1 * y
        xo_ref[0, rows, :] = xn
        h2 = _norm_mod(xn, n2_ref[...], sh2, sc2)
        h2_ref[0, rows, :] = _pack_bf16_pairs(h2)
        h_hi = h2.astype(BF16)
        h_lo = (h2 - h_hi.astype(F32)).astype(BF16)
        nt = (((1,), (1,)), ((), ()))
        by_hi = lax.dot_general(rw_ref[...], h_hi, nt, preferred_element_type=F32)
        by_lo = lax.dot_general(rw_ref[:N_EXPERTS, :], h_lo, nt, preferred_element_type=F32)
        logits_t = by_hi[:N_EXPERTS] + by_hi[N_EXPERTS:] + by_lo
        route_ref[0, :, rows] = _route(logits_t, rb_ref[...])


def _merge(tokens, mod, mod_row, n1, n2, branches, offsets, p_in, pool_params, seq, weights, s_out, out_off,
           rows, tm, prev=None):
    pending = len(tokens) == 3
    b, _, d = tokens[0].shape
    wg, bg, wb, wo, rw_t, rb = weights
    pool_w, pool_scale = pool_params
    seq_lo, seq_hi = seq
    n_real = rows // tm
    n_fill = -(-(s_out - out_off - rows) // tm) if prev is None else 0
    step = lambda j: jnp.minimum(j, n_real - 1)
    blk = lambda width, off: pl.BlockSpec((1, tm, width), lambda bi, j: (bi, off // tm + step(j), 0))
    out_blk = lambda width: pl.BlockSpec((1, tm, width), lambda bi, j: (bi, out_off // tm + j, 0))
    assert all(o % tm == 0 for o in offsets) and out_off % tm == 0 and rows % tm == 0 and seq_lo % tm == 0
    hb = tm // POOL_HALO
    last_halo = p_in.shape[1] // POOL_HALO - 1
    halo = lambda shift: pl.BlockSpec(
        (1, POOL_HALO, BRANCH_W),
        lambda bi, j: (bi, jnp.clip((seq_lo // tm + step(j) + shift) * hb - 1 + shift, 0, last_halo), 0))
    out_shape = [
        jax.ShapeDtypeStruct((b, s_out, d), F32),
        jax.ShapeDtypeStruct((b, s_out, d // 2), jnp.int32),
        jax.ShapeDtypeStruct((b, ROUTE_ROWS, s_out), F32),
    ]
    mod_spec = pl.BlockSpec((1, 1, 6 * d), lambda bi, j: (bi if mod_row is None else mod_row, 0, 0))
    tok_off = out_off if pending else 0
    token_specs = [blk(d, tok_off)] + ([blk(d // 2, tok_off), mod_spec] if pending else [])
    n_in = len(tokens) + 17
    extra_specs = [] if prev is None else [pl.BlockSpec(memory_space=pl.ANY)] * 3
    return pl.pallas_call(
        functools.partial(_merge_kernel, nw=512, sub=MERGE_CHAIN_ROWS, n_real=n_real, n_fill=n_fill,
                          pending=pending, p_off=seq_lo, seq_lo=seq_lo, seq_hi=seq_hi),
        scratch_shapes=[pltpu.VMEM((tm, d), BF16)],
        grid=(b, n_real + n_fill),
        in_specs=token_specs + [
            mod_spec,
            _const_spec((1, d)), _const_spec((1, d)),
            *[blk(BRANCH_W, off) for off in offsets],
            blk(BRANCH_W, seq_lo), halo(0), halo(1),
            _const_spec(pool_w.shape), _const_spec((1, BRANCH_W)),
            _const_spec(wg.shape), _const_spec(bg.shape), _const_spec(wb.shape), _const_spec(wo.shape),
            _const_spec(rw_t.shape), _const_spec(rb.shape),
        ] + extra_specs,
        out_specs=[
            out_blk(d),
            out_blk(d // 2),
            pl.BlockSpec((1, ROUTE_ROWS, tm), lambda bi, j: (bi, 0, out_off // tm + j)),
        ],
        out_shape=out_shape,
        input_output_aliases={} if prev is None else {n_in + i: i for i in range(3)},
        compiler_params=_cparams(("parallel", "arbitrary"), 56),
        name="merge",
    )(*tokens, mod, n1, n2, *branches, p_in, p_in, p_in, pool_w, pool_scale,
      wg, bg, wb, wo, rw_t, rb, *(() if prev is None else prev))


def _rank_kernel(cls_ref, rank_ref, cnt_ref, cnt_s, *, tr):
    @pl.when(pl.program_id(0) == 0)
    def _():
        cnt_s[...] = jnp.zeros_like(cnt_s)

    cls = cls_ref[0]
    cid = lax.broadcasted_iota(jnp.int32, (CLASS_ROWS, tr), 0).astype(F32)
    onehot = cid == cls
    before = lax.broadcasted_iota(jnp.int32, (tr, tr), 0) < lax.broadcasted_iota(jnp.int32, (tr, tr), 1)
    prefix = jnp.dot(jnp.where(onehot, 1.0, 0.0).astype(BF16), jnp.where(before, 1.0, 0.0).astype(BF16),
                     preferred_element_type=F32)
    carry = cnt_s[...][:, 0:1]
    rank_ref[0] = jnp.sum(jnp.where(onehot, prefix + carry, 0.0), axis=0, keepdims=True)
    cnt_s[...] += jnp.sum(jnp.where(onehot, 1.0, 0.0), axis=1, keepdims=True)
    cnt_ref[...] = cnt_s[...]


def _rank(cls_flat, tr=512):
    n = cls_flat.shape[0]
    tr = math.gcd(n, tr)
    cls3 = cls_flat.reshape(n // tr, 1, tr)
    rank, cnt = pl.pallas_call(
        functools.partial(_rank_kernel, tr=tr),
        grid=(n // tr,),
        in_specs=[pl.BlockSpec((1, 1, tr), lambda i: (i, 0, 0))],
        out_specs=[pl.BlockSpec((1, 1, tr), lambda i: (i, 0, 0)), _const_spec((CLASS_ROWS, LANES))],
        out_shape=[jax.ShapeDtypeStruct((n // tr, 1, tr), F32), jax.ShapeDtypeStruct((CLASS_ROWS, LANES), F32)],
        scratch_shapes=[pltpu.VMEM((CLASS_ROWS, LANES), F32)],
        compiler_params=_cparams(("arbitrary",), 32),
        name="rank",
    )(cls3)
    return rank.reshape(n), cnt[:N_CLASSES, 0]


def _sc_layout(n):
    info = plsc.get_sparse_core_info()
    nw = info.num_cores * info.num_subcores
    per_worker = n // nw
    assert per_worker * nw == n
    chunk = max(c for c in range(8, SC_MAX_CHUNK + 1, 8) if per_worker % c == 0)
    return info.num_cores, nw, per_worker // chunk, chunk


def _sc_scatter_rows(src, pos, n_out):
    n, w = src.shape
    nc, nw, k, c = _sc_layout(n)
    mesh = plsc.VectorSubcoreMesh(core_axis_name="c", subcore_axis_name="s")

    @functools.partial(
        pl.kernel, mesh=mesh,
        out_type=jax.ShapeDtypeStruct((n_out, w), src.dtype),
        scratch_types=[pltpu.VMEM((k, c), jnp.int32), pltpu.VMEM((c, w), src.dtype), pltpu.SemaphoreType.DMA],
        name="moe_scatter",
    )
    def scatter(src_hbm, pos_hbm, out_hbm, idx_v, rows_v, sem):
        wid = lax.axis_index("s") * nc + lax.axis_index("c")
        pltpu.sync_copy(pos_hbm.at[wid], idx_v)

        @pl.loop(0, k)
        def _(j):
            off = pl.multiple_of(wid * (k * c) + j * c, 8)
            pltpu.sync_copy(src_hbm.at[pl.ds(off, c)], rows_v)
            pltpu.async_copy(rows_v, out_hbm.at[idx_v.at[j]], sem).wait()

    return scatter(src, pos.reshape(nw, k, c))


def _sc_gather_rows(src, pos):
    n = pos.shape[0]
    w = src.shape[1]
    nc, nw, k, c = _sc_layout(n)
    mesh = plsc.VectorSubcoreMesh(core_axis_name="c", subcore_axis_name="s")

    @functools.partial(
        pl.kernel, mesh=mesh,
        out_type=jax.ShapeDtypeStruct((n, w), src.dtype),
        scratch_types=[pltpu.VMEM((k, c), jnp.int32), pltpu.VMEM((c, w), src.dtype), pltpu.SemaphoreType.DMA],
        name="moe_gather",
    )
    def gather(src_hbm, pos_hbm, out_hbm, idx_v, rows_v, sem):
        wid = lax.axis_index("s") * nc + lax.axis_index("c")
        pltpu.sync_copy(pos_hbm.at[wid], idx_v)

        @pl.loop(0, k)
        def _(j):
            off = pl.multiple_of(wid * (k * c) + j * c, 8)
            pltpu.async_copy(src_hbm.at[idx_v.at[j]], rows_v, sem).wait()
            pltpu.sync_copy(rows_v, out_hbm.at[pl.ds(off, c)])

    return gather(src, pos.reshape(nw, k, c))


def _gmm_kernel(lo_ref, hi_ref, new_ref, nact_ref, h_ref, rw_ref,
                w1a_ref, w1b_ref, w3a_ref, w3b_ref, w2a_ref, w2b_ref, o_ref, w13_s, w2_s):
    t = pl.program_id(0)

    @pl.when(new_ref[t] == 1)
    def _():
        for i, (w1_ref, w3_ref, w2_ref) in enumerate(((w1a_ref, w3a_ref, w2a_ref), (w1b_ref, w3b_ref, w2b_ref))):
            w13_s[i, :, :EXPERT_FF] = w1_ref[0, 0].astype(BF16)
            w13_s[i, :, EXPERT_FF:] = w3_ref[0, 0].astype(BF16)
            w2_s[i] = w2_ref[0, 0].astype(BF16)

    @pl.when(t < nact_ref[0])
    def _():
        xf = _unpack_bf16_pairs(h_ref[...])
        x = xf.astype(BF16)

        def expert(i):
            ab = jnp.dot(x, w13_s[i], preferred_element_type=F32)
            a, gate = ab[:, :EXPERT_FF], ab[:, EXPERT_FF:]
            hid = (a * jax.nn.sigmoid(a)) * gate
            return jnp.dot(hid.astype(BF16), w2_s[i], preferred_element_type=F32)

        aff = [jax.nn.sigmoid(jnp.sum(xf * rw_ref[pl.ds(e_ref[t], 1), :], axis=-1, keepdims=True))
               for e_ref in (lo_ref, hi_ref)]
        total = aff[0] + aff[1]
        y = (aff[0] / total) * expert(0) + (aff[1] / total) * expert(1)
        o_ref[...] = _pack_bf16_pairs(y)


def _gmm(tile_lo, tile_hi, tile_new, n_act, hs, rw_t, w1, w3, w2, layer, tm):
    n_pad, half = hs.shape
    d = 2 * half
    row = lambda t, lo, hi, new, na: (jnp.minimum(t, na[0] - 1), 0)
    e_lo = lambda t, lo, hi, new, na: (layer, lo[jnp.minimum(t, na[0] - 1)], 0, 0)
    e_hi = lambda t, lo, hi, new, na: (layer, hi[jnp.minimum(t, na[0] - 1)], 0, 0)
    up = lambda e: pl.BlockSpec((1, 1, d, EXPERT_FF), e)
    down = lambda e: pl.BlockSpec((1, 1, EXPERT_FF, d), e)
    return pl.pallas_call(
        _gmm_kernel,
        grid_spec=pltpu.PrefetchScalarGridSpec(
            num_scalar_prefetch=4,
            grid=(n_pad // tm,),
            in_specs=[
                pl.BlockSpec((tm, half), row),
                pl.BlockSpec(rw_t.shape, lambda t, lo, hi, new, na: (0, 0)),
                up(e_lo), up(e_hi), up(e_lo), up(e_hi), down(e_lo), down(e_hi),
            ],
            out_specs=pl.BlockSpec((tm, half), row),
            scratch_shapes=[pltpu.VMEM((2, d, 2 * EXPERT_FF), BF16), pltpu.VMEM((2, EXPERT_FF, d), BF16)],
        ),
        out_shape=jax.ShapeDtypeStruct((n_pad, half), jnp.int32),
        compiler_params=_cparams(("arbitrary",), 56),
        name="moe_gmm",
    )(tile_lo, tile_hi, tile_new, n_act, hs, rw_t, w1, w1, w3, w3, w2, w2)


def _moe_routed(h2p, route, rw_t, w1, w3, w2, layer, tm=256):
    b, s, half = h2p.shape
    n = b * s
    n_pad = n + N_CLASSES * tm
    cls = route[:, 0, :].reshape(n)
    rank, counts = _rank(cls)
    counts = counts.astype(jnp.int32)
    padded = (counts + tm - 1) // tm * tm
    ends = jnp.cumsum(padded)
    pos = jnp.take(ends - padded, cls.astype(jnp.int32)) + rank.astype(jnp.int32)
    n_act = (ends[-1] // tm).reshape(1)
    tile_row = jnp.arange(n_pad // tm, dtype=jnp.int32) * tm
    tile_cls = jnp.minimum(jnp.sum(tile_row[:, None] >= ends[None, :], axis=1), N_CLASSES - 1)
    pair_lo, pair_hi = (jnp.asarray(a, jnp.int32) for a in _class_experts())
    hs = _sc_scatter_rows(h2p.reshape(n, half), pos, n_pad)
    prev_cls = jnp.concatenate([jnp.full((1,), -1, tile_cls.dtype), tile_cls[:-1]])
    tile_new = ((tile_cls != prev_cls) & (tile_row < ends[-1])).astype(jnp.int32)
    ys = _gmm(jnp.take(pair_lo, tile_cls), jnp.take(pair_hi, tile_cls), tile_new, n_act, hs, rw_t,
              w1, w3, w2, layer, tm)
    return _sc_gather_rows(ys, pos).reshape(b, s, half)


def _class_experts():
    lo, hi = [], []
    for g in range(N_EXPERTS // EXPERTS_PER_GROUP):
        for i in range(EXPERTS_PER_GROUP):
            for j in range(i + 1, EXPERTS_PER_GROUP):
                lo.append(EXPERTS_PER_GROUP * g + i)
                hi.append(EXPERTS_PER_GROUP * g + j)
    return np.array(lo), np.array(hi)


def _final_residual_kernel(x_ref, y_ref, mod_ref, g_ref, o_ref):
    x = _pending_residual(x_ref[0], y_ref, mod_ref)
    o_ref[0] = x * lax.rsqrt(jnp.mean(x * x, axis=-1, keepdims=True) + EPS) * g_ref[...]


def _final_residual(x1, yp, mod, gain, tm=512):
    b, t, d = x1.shape
    tok = lambda bi, j: (bi, j, 0)
    return pl.pallas_call(
        _final_residual_kernel,
        grid=(b, t // tm),
        in_specs=[
            pl.BlockSpec((1, tm, d), tok),
            pl.BlockSpec((1, tm, d // 2), tok),
            pl.BlockSpec((1, 1, 6 * d), lambda bi, j: (bi, 0, 0)),
            _const_spec((1, d)),
        ],
        out_specs=pl.BlockSpec((1, tm, d), tok),
        out_shape=jax.ShapeDtypeStruct((b, t, d), F32),
        compiler_params=_cparams(("parallel", "arbitrary"), 32),
        name="final_residual",
    )(x1, yp, mod, gain)


ROPE_FREQS = ROPE_AXIS_DIM // 2


def _rope_tables(t_lat, n_ctx):
    rows = t_lat // GRID_W
    row = jnp.repeat(jnp.arange(rows, dtype=F32), GRID_W)
    col = jnp.tile(jnp.arange(GRID_W, dtype=F32), rows)
    inv_freq = ROPE_THETA ** (-jnp.arange(0, ROPE_AXIS_DIM, 2, dtype=F32) / ROPE_AXIS_DIM)
    ang = jnp.stack([row[:, None] * inv_freq, col[:, None] * inv_freq], axis=1)
    lanes = (2, LANES // HEAD_DIM, 2, ROPE_FREQS)
    sign = jnp.array([-1.0, 1.0], F32).reshape(1, 2, 1, 1, 1)
    cos = jnp.broadcast_to(jnp.cos(ang)[:, None, None], (t_lat,) + lanes).reshape(t_lat, LANES)
    sin = jnp.broadcast_to(sign * jnp.sin(ang)[:, None, None], (t_lat,) + lanes).reshape(t_lat, LANES)
    pad = lambda tbl, fill: jnp.concatenate([tbl, jnp.full((n_ctx, LANES), fill, F32)], axis=0)
    return pad(cos, 1.0), pad(sin, 0.0)


def _qk_lanes(w, n_heads):
    lead = w.shape[:-1]
    n_chunks = n_heads // N_KV_HEADS
    w = w.reshape(lead + (N_KV_HEADS, n_chunks, 2, 2, ROPE_FREQS))
    nl = len(lead)
    w = jnp.transpose(w, tuple(range(nl)) + (nl + 1, nl + 3, nl, nl + 2, nl + 4))
    return w.reshape(lead + (n_heads * HEAD_DIM,))


def _permute_heads(w, axis):
    shp = w.shape
    w = w.reshape(shp[:axis] + (N_Q_HEADS, HEAD_DIM) + shp[axis + 1:])
    w = jnp.take(w, jnp.array(HEAD_PERM), axis=axis)
    return w.reshape(shp)


def _square_factor(n):
    r = int(round(math.sqrt(n)))
    assert r * r == n, "sequence lengths must be perfect squares for the two-stage DFT"
    return r


def kernel(x, c, ctx, c_ctx, w_ada, b_ada, norm1, norm2, w_in, q_gain, k_gain, sink, pool_w, pool_scale,
           w_branch, w_gate, b_gate, w_out, router_w, router_bias, w1, w3, w2, norm_f):
    b, t_lat, d = x.shape
    n_ctx = ctx.shape[1]
    s = t_lat + n_ctx
    depth = w_ada.shape[0]
    assert d == D_MODEL and b < MOD_ROWS and t_lat % 256 == 0 and n_ctx % 256 == 0 and t_lat % n_ctx == 0

    tokens = (x, ctx)
    cc = jnp.zeros((MOD_ROWS, d), F32).at[:b].set(c).at[b].set(c_ctx)
    mod_all = _ada(cc, w_ada, b_ada).reshape(depth, MOD_ROWS, 1, 6 * d)

    cos, sin = _rope_tables(t_lat, n_ctx)
    lane_head = (np.arange(BRANCH_W) // LANES) * 2 + (np.arange(BRANCH_W) // (HEAD_DIM // 2)) % 2
    seg = jnp.asarray((lane_head[:, None] == lane_head[None, :]) / HEAD_DIM, BF16)
    cs = jnp.asarray(_channel_dft_table()).astype(BF16)
    f_lat = [jnp.asarray(a).astype(BF16) for a in _fourier_tables(*(_square_factor(t_lat),) * 2)]
    f_ctx = [jnp.asarray(a).astype(BF16) for a in _fourier_tables(*(_square_factor(n_ctx),) * 2)]
    wbias = jnp.asarray(_window_bias(n_ctx))
    rw_hi = router_w.T.astype(BF16)
    rw_t = jnp.concatenate([rw_hi, (router_w.T - rw_hi.astype(F32)).astype(BF16)], axis=0)
    rb = router_bias.reshape(N_EXPERTS, 1)

    for l in range(depth):
        need_ctx = l < depth - 1
        s_out = s if need_ctx else t_lat
        cols = jnp.split(w_in[l], np.cumsum((512, 512, 512, 128, 128, 512, 128))[:], axis=1)
        f_w, p_w, qb_w, kb_w, vb_w, qw_w, kw_w, vw_w = cols
        w_in_l = jnp.concatenate([_qk_lanes(qb_w, N_Q_HEADS), _qk_lanes(qw_w, N_Q_HEADS),
                                  _qk_lanes(kb_w, N_KV_HEADS), _qk_lanes(kw_w, N_KV_HEADS),
                                  f_w, vb_w, vw_w, p_w], axis=1).astype(BF16)
        wb_l = jnp.stack([w_branch[l, 0], _permute_heads(w_branch[l, 1], 0),
                          _permute_heads(w_branch[l, 2], 0), w_branch[l, 3]]).astype(BF16)
        mod = mod_all[l]
        n1 = norm1[l].reshape(1, d)
        n2 = norm2[l].reshape(1, d)
        qg = _qk_lanes(jnp.tile(q_gain[l], N_Q_HEADS), N_Q_HEADS).reshape(1, BRANCH_W)
        kg = _qk_lanes(jnp.tile(k_gain[l], N_KV_HEADS), N_KV_HEADS).reshape(1, LANES)

        zr, zi, p_in, qb, qw, kvb, kvw = _inproj(tokens, t_lat, mod, n1, w_in_l, qg, kg, seg, cs, cos, sin)

        out_a = _fourier(zr, zi, f_lat[0], f_lat[1], t_lat, 0)
        out_b = _gattn(qb, kvb, 0, t_lat, 0, s)
        out_c = _wattn(jnp.take(sink[l], jnp.array(HEAD_PERM)) * LOG2E, wbias, qw, kvw, s_out, t_lat)

        weights = (w_gate[l].astype(BF16), b_gate[l].reshape(4, 1, d), wb_l, w_out[l].astype(BF16), rw_t, rb)
        pool_params = (pool_w[l].astype(BF16), pool_scale[l].reshape(1, BRANCH_W))
        pending = len(tokens) == 3
        merged = _merge(tokens if pending else tokens[:1], mod, None, n1, n2, (out_a, out_b, out_c), (0, 0, 0),
                        p_in, pool_params, (0, t_lat), weights, s_out, 0, t_lat, 2 * MERGE_CHAIN_ROWS)
        if need_ctx:
            out_ac = _fourier(zr, zi, f_ctx[0], f_ctx[1], n_ctx, t_lat // n_ctx)
            out_bc = _gattn(qb, kvb, t_lat, n_ctx, t_lat, n_ctx)
            merged = _merge(tokens if pending else tokens[1:], mod, b, n1, n2, (out_ac, out_bc, out_c),
                            (0, 0, t_lat), p_in, pool_params, (t_lat, s), weights, s_out, t_lat, n_ctx,
                            MERGE_CHAIN_ROWS, prev=merged)
        x1, h2p, route = merged
        yp = _moe_routed(h2p, route, router_w.T, w1, w3, w2, l)
        tokens = (x1, yp, mod)

    return _final_residual(*tokens, norm_f.reshape(1, d))
```

```python
import functools
import math

import numpy as np
import jax
import jax.numpy as jnp
from jax import lax
from jax.experimental import pallas as pl
from jax.experimental.pallas import tpu as pltpu
from jax.experimental.pallas import tpu_sc as plsc

F32 = jnp.float32
BF16 = jnp.bfloat16

D_MODEL = 1024
HEAD_DIM = 64
N_Q_HEADS = 8
N_KV_HEADS = 2
GRID_W = 64
ROPE_THETA = 10000.0
ROPE_AXIS_DIM = HEAD_DIM // 2
QBLK = 128
WINDOW = 128
BRANCH_W = 512
GROUP_W = 128
POOL_WINDOWS = (2, 4, 8, 16)
N_EXPERTS = 16
EXPERTS_PER_GROUP = 4
EXPERT_FF = 512
EPS = 1e-6
MOD_ROWS = 16
NEG_BIG = -1e30
LOG2E = math.log2(math.e)
LANES = 128
POOL_HALO = 16
PAIRS_PER_GROUP = 6
N_CLASSES = 24
CLASS_ROWS = 32
ROUTE_ROWS = 8
SC_MAX_CHUNK = 128
WATTN_QBLOCKS = 2
MERGE_CHAIN_ROWS = 256

HEAD_PERM = (0, 4, 1, 5, 2, 6, 3, 7)


def _cparams(sem, vmem_mb):
    return pltpu.CompilerParams(dimension_semantics=sem, vmem_limit_bytes=vmem_mb * 1024 * 1024)


def _const_spec(shape):
    nd = len(shape)
    return pl.BlockSpec(shape, lambda *_: (0,) * nd)


def _ada_kernel(c_ref, w_ref, b_ref, o_ref):
    c = c_ref[...]
    s = c * jax.nn.sigmoid(c)
    o_ref[0] = jnp.dot(s.astype(BF16), w_ref[0].astype(BF16), preferred_element_type=F32) + b_ref[0]


def _ada(cc, w_ada, b_ada):
    depth, d, n = w_ada.shape
    tn = 1536
    return pl.pallas_call(
        _ada_kernel,
        grid=(depth, n // tn),
        in_specs=[
            pl.BlockSpec((MOD_ROWS, d), lambda l, j: (0, 0)),
            pl.BlockSpec((1, d, tn), lambda l, j: (l, 0, j)),
            pl.BlockSpec((1, 1, tn), lambda l, j: (l, 0, j)),
        ],
        out_specs=pl.BlockSpec((1, MOD_ROWS, tn), lambda l, j: (l, 0, j)),
        out_shape=jax.ShapeDtypeStruct((depth, MOD_ROWS, n), F32),
        compiler_params=_cparams(("arbitrary", "arbitrary"), 40),
        name="ada",
    )(cc, w_ada, b_ada.reshape(depth, 1, n))


def _norm_mod(x, gain, shift, scale):
    ms = jnp.mean(x * x, axis=-1, keepdims=True)
    return (x * lax.rsqrt(ms + EPS) * gain) * (1.0 + scale) + shift


def _mod_slices(m):
    d = D_MODEL
    return [m[:, i * d:(i + 1) * d] for i in range(6)]


def _head_norm(z, seg, gain):
    ms = jnp.dot((z * z).astype(BF16), seg, preferred_element_type=F32)
    return z * lax.rsqrt(ms + EPS) * gain


def _rope(z, cos, sin):
    outs = []
    for c in range(z.shape[1] // LANES):
        zc = z[:, c * LANES:(c + 1) * LANES]
        outs.append(zc * cos + pltpu.roll(zc, LANES // 2, 1) * sin)
    return outs[0] if len(outs) == 1 else jnp.concatenate(outs, axis=-1)


def _stream_specs(tm, d, n_lat):
    return [pl.BlockSpec((1, tm, d), lambda bi, j: (bi, jnp.minimum(j, n_lat - 1), 0)),
            pl.BlockSpec((1, tm, d), lambda bi, j: (bi, jnp.maximum(j - n_lat, 0), 0))]


def _pending_residual(x, y_ref, modp_ref, rows=slice(None)):
    return x + _mod_slices(modp_ref[0])[5] * _unpack_bf16_pairs(y_ref[0, rows, :])


def _inproj_kernel(*refs, n_lat, pending):
    if pending:
        x_ref, y_ref, modp_ref = refs[:3]
        x = _pending_residual(x_ref[0], y_ref, modp_ref)
    else:
        xl_ref, xc_ref = refs[:2]
        x = jnp.where(pl.program_id(1) >= n_lat, xc_ref[0], xl_ref[0])
    (mod_ref, n1_ref, w_ref, qg_ref, kg_ref, seg_ref, cs_ref, cos_ref, sin_ref,
     zr_ref, zi_ref, p_ref, qb_ref, qw_ref, kvb_ref, kvw_ref) = refs[3 if pending else 2:]
    sh1, sc1 = _mod_slices(mod_ref[0])[:2]
    h = _norm_mod(x, n1_ref[...], sh1, sc1)
    u = jnp.dot(h.astype(BF16), w_ref[...], preferred_element_type=F32)
    cos, sin = cos_ref[...], sin_ref[...]
    w = BRANCH_W
    seg = seg_ref[...]
    qb = _rope(_head_norm(u[:, 0:w], seg, qg_ref[...]), cos, sin)
    qb_ref[0] = (qb * (HEAD_DIM ** -0.5 * LOG2E)).astype(BF16)
    qw = _rope(u[:, w:2 * w], cos, sin)
    qw_ref[0] = (qw * (HEAD_DIM ** -0.5 * LOG2E)).astype(BF16)
    o = 2 * w
    kb = _rope(_head_norm(u[:, o:o + LANES], seg[:LANES, :LANES], kg_ref[...]), cos, sin)
    kw = _rope(u[:, o + LANES:o + 2 * LANES], cos, sin)
    o += 2 * LANES
    f_in = u[:, o:o + w].astype(BF16)
    zr, zi = [], []
    for g in range(w // GROUP_W):
        z = jnp.dot(f_in[:, g * GROUP_W:(g + 1) * GROUP_W], cs_ref[...], preferred_element_type=F32)
        zr.append(z[:, :GROUP_W])
        zi.append(z[:, GROUP_W:])
    zr_ref[0] = jnp.concatenate(zr, axis=-1).astype(BF16)
    zi_ref[0] = jnp.concatenate(zi, axis=-1).astype(BF16)
    o += w
    vb = u[:, o:o + LANES]
    vw = u[:, o + LANES:o + 2 * LANES]
    kvb_ref[0] = jnp.concatenate([kb, vb, jnp.ones_like(vb)], axis=-1).astype(BF16)
    kvw_ref[0] = jnp.concatenate([kw, vw, jnp.ones_like(vw)], axis=-1).astype(BF16)
    p_ref[0] = u[:, o + 2 * LANES:o + 2 * LANES + w].astype(BF16)


def _inproj(tokens, t_lat, mod, n1, w_in, qg, kg, seg, cs, cos, sin, tm=256):
    pending = len(tokens) == 3
    b, _, d = tokens[0].shape
    s = tokens[0].shape[1] if pending else t_lat + tokens[1].shape[1]
    nw = w_in.shape[1]
    n_lat = t_lat // tm
    tok = lambda bi, j: (bi, j, 0)
    tab = lambda bi, j: (j, 0)
    mod_spec = pl.BlockSpec((1, 1, 6 * d), lambda bi, j: (jnp.where(j >= n_lat, b, bi), 0, 0))
    if pending:
        token_specs = [pl.BlockSpec((1, tm, d), tok), pl.BlockSpec((1, tm, d // 2), tok), mod_spec]
    else:
        token_specs = _stream_specs(tm, d, n_lat)
    widths = (BRANCH_W,) * 5 + (3 * LANES, 3 * LANES)
    return pl.pallas_call(
        functools.partial(_inproj_kernel, n_lat=n_lat, pending=pending),
        grid=(b, s // tm),
        in_specs=token_specs + [
            mod_spec,
            _const_spec((1, d)),
            _const_spec((d, nw)),
            _const_spec((1, BRANCH_W)),
            _const_spec((1, LANES)),
            _const_spec((BRANCH_W, BRANCH_W)),
            _const_spec((GROUP_W, 2 * GROUP_W)),
            pl.BlockSpec((tm, LANES), tab),
            pl.BlockSpec((tm, LANES), tab),
        ],
        out_specs=[pl.BlockSpec((1, tm, wd), tok) for wd in widths],
        out_shape=[jax.ShapeDtypeStruct((b, s, wd), BF16) for wd in widths],
        compiler_params=_cparams(("parallel", "arbitrary"), 48),
        name="inproj",
    )(*tokens, mod, n1, w_in, qg, kg, seg, cs, cos, sin)


def _split_heads(qc, lane):
    first = (lane & (HEAD_DIM // 2)) == 0
    zero = jnp.zeros_like(qc)
    return jnp.concatenate([jnp.where(first, qc, zero), jnp.where(first, zero, qc)], axis=0)


def _gattn_kernel(q_ref, kv_ref, o_ref, q2_s, acc_s, *, sub):
    tq = q_ref.shape[1]
    lane = lax.broadcasted_iota(jnp.int32, (1, LANES), 1)
    nt = (((1,), (1,)), ((), ()))
    n_chunks = BRANCH_W // LANES
    for c in range(n_chunks):
        q2_s[2 * c * tq:(2 * c + 2) * tq, :] = _split_heads(q_ref[0, :, c * LANES:(c + 1) * LANES], lane)
    k = kv_ref[0, :, 0:LANES]
    v = kv_ref[0, :, LANES:3 * LANES]
    for r in range(2 * n_chunks * tq // sub):
        rows = slice(r * sub, (r + 1) * sub)
        s = lax.dot_general(q2_s[rows, :], k, nt, preferred_element_type=F32)
        p = jnp.exp2(s - jnp.max(s, axis=-1, keepdims=True))
        acc_s[rows, :] = jnp.dot(p.astype(BF16), v, preferred_element_type=F32)
    for c in range(n_chunks):
        lo = acc_s[2 * c * tq:(2 * c + 1) * tq, :]
        hi = acc_s[(2 * c + 1) * tq:(2 * c + 2) * tq, :]
        o = jnp.where(lane < HEAD_DIM, lo[:, :LANES] / lo[:, LANES:], hi[:, :LANES] / hi[:, LANES:])
        o_ref[0, :, c * LANES:(c + 1) * LANES] = o.astype(BF16)


def _gattn(qb, kv, q_start, q_len, k_start, k_len, tq=256, sub=128):
    b = qb.shape[0]
    assert q_start % tq == 0 and q_len % tq == 0 and k_start % k_len == 0 and k_len % LANES == 0
    rows = 2 * tq * (BRANCH_W // LANES)
    return pl.pallas_call(
        functools.partial(_gattn_kernel, sub=sub),
        grid=(b, q_len // tq),
        in_specs=[
            pl.BlockSpec((1, tq, BRANCH_W), lambda bi, j: (bi, q_start // tq + j, 0)),
            pl.BlockSpec((1, k_len, 3 * LANES), lambda bi, j: (bi, k_start // k_len, 0)),
        ],
        out_specs=pl.BlockSpec((1, tq, BRANCH_W), lambda bi, j: (bi, j, 0)),
        out_shape=jax.ShapeDtypeStruct((b, q_len, BRANCH_W), BF16),
        scratch_shapes=[pltpu.VMEM((rows, LANES), BF16), pltpu.VMEM((rows, 2 * LANES), F32)],
        compiler_params=_cparams(("parallel", "arbitrary"), 48),
        name="gattn",
    )(qb, kv)


def _window_bias(n_ctx):
    tq = WATTN_QBLOCKS * QBLK
    qi = np.arange(tq)[:, None]
    kj = np.arange(tq + 2 * QBLK)[None, :]
    band = np.abs(kj - WINDOW - qi) <= WINDOW
    blk = kj // QBLK
    variants = [band & (blk != 0), band, band & (blk != WATTN_QBLOCKS + 1), np.zeros_like(band)]
    out = [np.concatenate([np.ones((tq, n_ctx), bool), v], axis=1) for v in variants]
    return np.where(np.stack(out), 0.0, NEG_BIG).astype(np.float32)


def _wattn_kernel(sink_ref, bias_ref, q_ref, *refs, sub):
    o_ref, q2_s, kv_s, acc_s = refs[-4:]
    tq = q_ref.shape[1]
    off = 0
    for blk in refs[:-4]:
        kv_s[off:off + blk.shape[1], :] = blk[0]
        off += blk.shape[1]
    lane = lax.broadcasted_iota(jnp.int32, (1, LANES), 1)
    nt = (((1,), (1,)), ((), ()))
    n_chunks = BRANCH_W // LANES
    for c in range(n_chunks):
        q2_s[2 * c * tq:(2 * c + 2) * tq, :] = _split_heads(q_ref[0, :, c * LANES:(c + 1) * LANES], lane)
    k, v = kv_s[:, 0:LANES], kv_s[:, LANES:3 * LANES]
    for r in range(2 * n_chunks * tq // sub):
        rows = slice(r * sub, (r + 1) * sub)
        q_off = (r * sub) % tq
        sk = sink_ref[(r * sub) // tq]
        s = lax.dot_general(q2_s[rows, :], k, nt, preferred_element_type=F32) + bias_ref[0, q_off:q_off + sub, :]
        m = jnp.maximum(jnp.max(s, axis=-1, keepdims=True), sk)
        pv = jnp.dot(jnp.exp2(s - m).astype(BF16), v, preferred_element_type=F32)
        acc_s[rows, :LANES] = pv[:, :LANES]
        acc_s[rows, LANES:] = pv[:, LANES:] + jnp.exp2(sk - m)
    for c in range(n_chunks):
        lo = acc_s[2 * c * tq:(2 * c + 1) * tq, :]
        hi = acc_s[(2 * c + 1) * tq:(2 * c + 2) * tq, :]
        o = jnp.where(lane < HEAD_DIM, lo[:, :LANES] / lo[:, LANES:], hi[:, :LANES] / hi[:, LANES:])
        o_ref[0, :, c * LANES:(c + 1) * LANES] = o.astype(BF16)


def _wattn(sink, bias, qw, kv, s_out, t_lat):
    b, s, _ = qw.shape
    n_ctx = s - t_lat
    nq = WATTN_QBLOCKS
    tq = nq * QBLK
    assert n_ctx % tq == 0 and t_lat // tq >= 2
    last = s // QBLK - 1
    n_lat = t_lat // tq
    variant = lambda j: jnp.where(j >= n_lat, 3, jnp.where(j == 0, 0, jnp.where(j == n_lat - 1, 2, 1)))
    key_block = lambda off: pl.BlockSpec(
        (1, QBLK, 3 * LANES), lambda bi, j: (bi, jnp.clip(j * nq + off, 0, last), 0))
    rows = 2 * tq * (BRANCH_W // LANES)
    return pl.pallas_call(
        functools.partial(_wattn_kernel, sub=128),
        scratch_shapes=[pltpu.VMEM((rows, LANES), BF16), pltpu.VMEM((bias.shape[2], 3 * LANES), BF16),
                        pltpu.VMEM((rows, 2 * LANES), F32)],
        grid=(b, s_out // tq),
        in_specs=[
            pl.BlockSpec(memory_space=pltpu.SMEM),
            pl.BlockSpec((1,) + bias.shape[1:], lambda bi, j: (variant(j), 0, 0)),
            pl.BlockSpec((1, tq, BRANCH_W), lambda bi, j: (bi, j, 0)),
            pl.BlockSpec((1, n_ctx, 3 * LANES), lambda bi, j: (bi, t_lat // n_ctx, 0)),
        ] + [key_block(off) for off in range(-1, nq + 1)],
        out_specs=pl.BlockSpec((1, tq, BRANCH_W), lambda bi, j: (bi, j, 0)),
        out_shape=jax.ShapeDtypeStruct((b, s_out, BRANCH_W), BF16),
        compiler_params=_cparams(("parallel", "arbitrary"), 32),
        name="wattn",
    )(sink, bias, qw, *([kv] * (nq + 3)))


def _fourier_tables(n1, n2):
    t = n1 * n2
    k2 = np.arange(n2)[None, :, None]
    t2 = np.arange(n2)[None, None, :]
    t1 = np.arange(n1)[:, None, None]
    theta = 2.0 * np.pi * ((k2 * t2 * n1 + k2 * t1) % t) / t
    er, ei = np.cos(theta) / math.sqrt(n2), -np.sin(theta) / math.sqrt(n2)
    e = np.concatenate([np.concatenate([er, -ei], axis=2), np.concatenate([ei, er], axis=2)], axis=1)
    k1 = np.arange(n1)[:, None]
    phi = 2.0 * np.pi * ((k1 * np.arange(n1)[None, :]) % n1) / n1
    dcat = np.concatenate([np.cos(phi), np.sin(phi)], axis=1) / math.sqrt(n1)
    return e.astype(np.float32), dcat.astype(np.float32)


def _channel_dft_table():
    c = np.arange(GROUP_W)
    ang = 2.0 * np.pi * ((c[:, None] * c[None, :]) % GROUP_W) / GROUP_W
    return (np.concatenate([np.cos(ang), -np.sin(ang)], axis=1) / math.sqrt(GROUP_W)).astype(np.float32)


def _fourier_kernel(zr_ref, zi_ref, e_ref, d_ref, o_ref, xr_s, xi_s, yr_s, yi_s, *, n1, n2):
    nc = xr_s.shape[0]
    chunk = lambda c: slice(c * LANES, (c + 1) * LANES)

    def gather(ref, start, size, stride):
        return jnp.concatenate([ref[c, pl.ds(start, size, stride=stride), :] for c in range(nc)], axis=-1)

    for c in range(nc):
        xr_s[c] = zr_ref[0, :, chunk(c)].astype(F32)
        xi_s[c] = zi_ref[0, :, chunk(c)].astype(F32)
    for t1 in range(n1):
        xs = jnp.concatenate([gather(xr_s, t1, n2, n1), gather(xi_s, t1, n2, n1)], axis=0).astype(BF16)
        y = jnp.dot(e_ref[t1], xs, preferred_element_type=F32)
        for c in range(nc):
            yr_s[c, t1 * n2:(t1 + 1) * n2, :] = y[:n2, chunk(c)]
            yi_s[c, t1 * n2:(t1 + 1) * n2, :] = y[n2:, chunk(c)]
    for k2 in range(n2):
        ys = jnp.concatenate([gather(yr_s, k2, n1, n2), gather(yi_s, k2, n1, n2)], axis=0).astype(BF16)
        o = jnp.dot(d_ref[...], ys, preferred_element_type=F32)
        for c in range(nc):
            xr_s[c, pl.ds(k2, n1, stride=n2), :] = o[:, chunk(c)]
    for c in range(nc):
        o_ref[0, :, chunk(c)] = xr_s[c].astype(BF16)


def _fourier(zr, zi, e_tab, d_tab, t_len, row_block, cw=256):
    b = zr.shape[0]
    n1 = d_tab.shape[0]
    n2 = t_len // n1
    zspec = pl.BlockSpec((1, t_len, cw), lambda bi, j: (bi, row_block, j))
    return pl.pallas_call(
        functools.partial(_fourier_kernel, n1=n1, n2=n2),
        grid=(b, BRANCH_W // cw),
        in_specs=[zspec, zspec, _const_spec(e_tab.shape), _const_spec(d_tab.shape)],
        out_specs=pl.BlockSpec((1, t_len, cw), lambda bi, j: (bi, 0, j)),
        out_shape=jax.ShapeDtypeStruct((b, t_len, BRANCH_W), BF16),
        scratch_shapes=[pltpu.VMEM((cw // LANES, t_len, LANES), F32)] * 4,
        compiler_params=_cparams(("parallel", "arbitrary"), 48),
        name="fourier",
    )(zr, zi, e_tab, d_tab)


def _pool_rows(ext, pos, n, w_ref, scale):
    n_ext = ext.shape[0]
    rows = n_ext - 2 * POOL_HALO
    back = lambda v, k: pltpu.roll(v, k, 0)
    fwd = lambda v, k: pltpu.roll(v, n_ext - k, 0)
    outs = []
    for gi, w in enumerate(POOL_WINDOWS):
        e = ext[:, gi * GROUP_W:(gi + 1) * GROUP_W]
        wsum = e + back(e, 1)
        half = 1
        while 2 * half < w:
            wsum = back(wsum, half) + fwd(wsum, half)
            half *= 2
        own = slice(POOL_HALO, POOL_HALO + rows)
        cnt = jnp.minimum(pos + w // 2, n) - jnp.maximum(pos - w // 2, 0)
        pooled = wsum[own] / cnt.astype(F32) - e[own]
        outs.append(jnp.dot(pooled.astype(BF16), w_ref[gi], preferred_element_type=F32))
    return (jnp.concatenate(outs, axis=-1) * scale).astype(BF16)


def _route(logits_t, bias):
    aff = jax.nn.sigmoid(logits_t)
    sel = aff + bias
    neg = -jnp.inf
    firsts, seconds, scores = [], [], []
    for g in range(N_EXPERTS // EXPERTS_PER_GROUP):
        s = [sel[EXPERTS_PER_GROUP * g + k:EXPERTS_PER_GROUP * g + k + 1, :] for k in range(EXPERTS_PER_GROUP)]
        m1 = jnp.maximum(jnp.maximum(s[0], s[1]), jnp.maximum(s[2], s[3]))
        i1 = jnp.where(s[0] == m1, 0, jnp.where(s[1] == m1, 1, jnp.where(s[2] == m1, 2, 3)))
        r = [jnp.where(i1 == k, neg, s[k]) for k in range(EXPERTS_PER_GROUP)]
        m2 = jnp.maximum(jnp.maximum(r[0], r[1]), jnp.maximum(r[2], r[3]))
        i2 = jnp.where(r[0] == m2, 0, jnp.where(r[1] == m2, 1, jnp.where(r[2] == m2, 2, 3)))
        firsts.append(i1 + EXPERTS_PER_GROUP * g)
        seconds.append(i2 + EXPERTS_PER_GROUP * g)
        scores.append(m1 + m2)
    best = jnp.maximum(jnp.maximum(scores[0], scores[1]), jnp.maximum(scores[2], scores[3]))
    pick = lambda v: jnp.where(scores[0] == best, v[0], jnp.where(scores[1] == best, v[1],
                                                                 jnp.where(scores[2] == best, v[2], v[3])))
    e1, e2 = pick(firsts), pick(seconds)
    lo = jnp.minimum(e1, e2) & (EXPERTS_PER_GROUP - 1)
    hi = jnp.maximum(e1, e2) & (EXPERTS_PER_GROUP - 1)
    pair = jnp.where(lo == 0, 0, jnp.where(lo == 1, 3, 5)) + hi - lo - 1
    cls = ((e1 >> 2) * PAIRS_PER_GROUP + pair).astype(F32)
    return jnp.concatenate([cls] + [jnp.zeros_like(cls)] * (ROUTE_ROWS - 1), axis=0)


def _pack_bf16_pairs(v):
    w = v.shape[1] // 2
    bits = pltpu.bitcast(v.astype(BF16).astype(F32), jnp.uint32)
    return pltpu.bitcast(bits[:, :w] | (bits[:, w:] >> 16), jnp.int32)


def _unpack_bf16_pairs(p):
    bits = pltpu.bitcast(p, jnp.uint32)
    hi = pltpu.bitcast(bits & jnp.uint32(0xFFFF0000), F32)
    lo = pltpu.bitcast(bits << 16, F32)
    return jnp.concatenate([hi, lo], axis=-1)


def _merge_kernel(*refs, n_real, n_fill, pending, **static):
    n_tok = 3 if pending else 1
    (mod_ref, n1_ref, n2_ref, a_ref, b_ref, c_ref, pc_ref, pp_ref, pn_ref, pw_ref, psc_ref,
     wg_ref, bg_ref, wb_ref, wo_ref, rw_ref, rb_ref) = refs[n_tok:n_tok + 17]
    outs = refs[-4:-1]
    ins = (refs[:n_tok], mod_ref, n1_ref, n2_ref, a_ref, b_ref, c_ref,
           (pc_ref, pp_ref, pn_ref, pw_ref, psc_ref),
           wg_ref, bg_ref, wb_ref, wo_ref, rw_ref, rb_ref)
    if n_fill:
        @pl.when(pl.program_id(1) >= n_real)
        def _():
            for ref in outs:
                ref[...] = jnp.zeros_like(ref)

        pl.when(pl.program_id(1) < n_real)(lambda: _merge_tile(ins, outs, refs[-1], **static))
    else:
        _merge_tile(ins, outs, refs[-1], **static)


def _merge_tile(ins, outs, merged_s, *, nw, sub, p_off, seq_lo, seq_hi):
    (tok_refs, mod_ref, n1_ref, n2_ref, a_ref, b_ref, c_ref, pool_refs,
     wg_ref, bg_ref, wb_ref, wo_ref, rw_ref, rb_ref) = ins
    xo_ref, h2_ref, route_ref = outs
    pc_ref, pp_ref, pn_ref, pw_ref, psc_ref = pool_refs
    x_ref = tok_refs[0]
    tm = x_ref.shape[1]
    sh1, sc1, g1, sh2, sc2, _ = _mod_slices(mod_ref[0])
    row0 = p_off + pl.program_id(1) * tm
    ext = jnp.concatenate([pp_ref[0], pc_ref[0], pn_ref[0]], axis=0).astype(F32)
    gpos = row0 - POOL_HALO + lax.broadcasted_iota(jnp.int32, (tm + 2 * POOL_HALO, 1), 0)
    ext = jnp.where((gpos >= seq_lo) & (gpos < seq_hi), ext, 0.0)
    for r in range(tm // sub):
        rows = slice(r * sub, (r + 1) * sub)
        x = x_ref[0, rows, :]
        if len(tok_refs) == 3:
            x = _pending_residual(x, tok_refs[1], tok_refs[2], rows)
        hb = _norm_mod(x, n1_ref[...], sh1, sc1).astype(BF16)
        pos = row0 - seq_lo + r * sub + lax.broadcasted_iota(jnp.int32, (sub, 1), 0)
        pooled = _pool_rows(ext[r * sub:(r + 1) * sub + 2 * POOL_HALO], pos, seq_hi - seq_lo,
                            pw_ref, psc_ref[...])
        branches = (a_ref[0, rows, :], b_ref[0, rows, :], c_ref[0, rows, :], pooled)
        for n in range(D_MODEL // nw):
            cols = slice(n * nw, (n + 1) * nw)
            merged = None
            for i, br in enumerate(branches):
                gate = jax.nn.sigmoid(
                    jnp.dot(hb, wg_ref[i, :, cols], preferred_element_type=F32) + bg_ref[i, :, cols])
                term = gate * jnp.dot(br, wb_ref[i, :, cols], preferred_element_type=F32)
                merged = term if merged is None else merged + term
            merged_s[rows, cols] = merged.astype(BF16)
        y = jnp.dot(merged_s[rows, :], wo_ref[...], preferred_element_type=F32)
        xn = x + g1 * y
        xo_ref[0, rows, :] = xn
        h2 = _norm_mod(xn, n2_ref[...], sh2, sc2)
        h2_ref[0, rows, :] = _pack_bf16_pairs(h2)
        h_hi = h2.astype(BF16)
        h_lo = (h2 - h_hi.astype(F32)).astype(BF16)
        nt = (((1,), (1,)), ((), ()))
        by_hi = lax.dot_general(rw_ref[...], h_hi, nt, preferred_element_type=F32)
        by_lo = lax.dot_general(rw_ref[:N_EXPERTS, :], h_lo, nt, preferred_element_type=F32)
        logits_t = by_hi[:N_EXPERTS] + by_hi[N_EXPERTS:] + by_lo
        route_ref[0, :, rows] = _route(logits_t, rb_ref[...])


def _merge(tokens, mod, mod_row, n1, n2, branches, offsets, p_in, pool_params, seq, weights, s_out, out_off,
           rows, tm, prev=None):
    pending = len(tokens) == 3
    b, _, d = tokens[0].shape
    wg, bg, wb, wo, rw_t, rb = weights
    pool_w, pool_scale = pool_params
    seq_lo, seq_hi = seq
    n_real = rows // tm
    n_fill = -(-(s_out - out_off - rows) // tm) if prev is None else 0
    step = lambda j: jnp.minimum(j, n_real - 1)
    blk = lambda width, off: pl.BlockSpec((1, tm, width), lambda bi, j: (bi, off // tm + step(j), 0))
    out_blk = lambda width: pl.BlockSpec((1, tm, width), lambda bi, j: (bi, out_off // tm + j, 0))
    assert all(o % tm == 0 for o in offsets) and out_off % tm == 0 and rows % tm == 0 and seq_lo % tm == 0
    hb = tm // POOL_HALO
    last_halo = p_in.shape[1] // POOL_HALO - 1
    halo = lambda shift: pl.BlockSpec(
        (1, POOL_HALO, BRANCH_W),
        lambda bi, j: (bi, jnp.clip((seq_lo // tm + step(j) + shift) * hb - 1 + shift, 0, last_halo), 0))
    out_shape = [
        jax.ShapeDtypeStruct((b, s_out, d), F32),
        jax.ShapeDtypeStruct((b, s_out, d // 2), jnp.int32),
        jax.ShapeDtypeStruct((b, ROUTE_ROWS, s_out), F32),
    ]
    mod_spec = pl.BlockSpec((1, 1, 6 * d), lambda bi, j: (bi if mod_row is None else mod_row, 0, 0))
    tok_off = out_off if pending else 0
    token_specs = [blk(d, tok_off)] + ([blk(d // 2, tok_off), mod_spec] if pending else [])
    n_in = len(tokens) + 17
    extra_specs = [] if prev is None else [pl.BlockSpec(memory_space=pl.ANY)] * 3
    return pl.pallas_call(
        functools.partial(_merge_kernel, nw=512, sub=MERGE_CHAIN_ROWS, n_real=n_real, n_fill=n_fill,
                          pending=pending, p_off=seq_lo, seq_lo=seq_lo, seq_hi=seq_hi),
        scratch_shapes=[pltpu.VMEM((tm, d), BF16)],
        grid=(b, n_real + n_fill),
        in_specs=token_specs + [
            mod_spec,
            _const_spec((1, d)), _const_spec((1, d)),
            *[blk(BRANCH_W, off) for off in offsets],
            blk(BRANCH_W, seq_lo), halo(0), halo(1),
            _const_spec(pool_w.shape), _const_spec((1, BRANCH_W)),
            _const_spec(wg.shape), _const_spec(bg.shape), _const_spec(wb.shape), _const_spec(wo.shape),
            _const_spec(rw_t.shape), _const_spec(rb.shape),
        ] + extra_specs,
        out_specs=[
            out_blk(d),
            out_blk(d // 2),
            pl.BlockSpec((1, ROUTE_ROWS, tm), lambda bi, j: (bi, 0, out_off // tm + j)),
        ],
        out_shape=out_shape,
        input_output_aliases={} if prev is None else {n_in + i: i for i in range(3)},
        compiler_params=_cparams(("parallel", "arbitrary"), 56),
        name="merge",
    )(*tokens, mod, n1, n2, *branches, p_in, p_in, p_in, pool_w, pool_scale,
      wg, bg, wb, wo, rw_t, rb, *(() if prev is None else prev))


def _rank_kernel(cls_ref, rank_ref, cnt_ref, cnt_s, *, tr):
    @pl.when(pl.program_id(0) == 0)
    def _():
        cnt_s[...] = jnp.zeros_like(cnt_s)

    cls = cls_ref[0]
    cid = lax.broadcasted_iota(jnp.int32, (CLASS_ROWS, tr), 0).astype(F32)
    onehot = cid == cls
    before = lax.broadcasted_iota(jnp.int32, (tr, tr), 0) < lax.broadcasted_iota(jnp.int32, (tr, tr), 1)
    prefix = jnp.dot(jnp.where(onehot, 1.0, 0.0).astype(BF16), jnp.where(before, 1.0, 0.0).astype(BF16),
                     preferred_element_type=F32)
    carry = cnt_s[...][:, 0:1]
    rank_ref[0] = jnp.sum(jnp.where(onehot, prefix + carry, 0.0), axis=0, keepdims=True)
    cnt_s[...] += jnp.sum(jnp.where(onehot, 1.0, 0.0), axis=1, keepdims=True)
    cnt_ref[...] = cnt_s[...]


def _rank(cls_flat, tr=512):
    n = cls_flat.shape[0]
    tr = math.gcd(n, tr)
    cls3 = cls_flat.reshape(n // tr, 1, tr)
    rank, cnt = pl.pallas_call(
        functools.partial(_rank_kernel, tr=tr),
        grid=(n // tr,),
        in_specs=[pl.BlockSpec((1, 1, tr), lambda i: (i, 0, 0))],
        out_specs=[pl.BlockSpec((1, 1, tr), lambda i: (i, 0, 0)), _const_spec((CLASS_ROWS, LANES))],
        out_shape=[jax.ShapeDtypeStruct((n // tr, 1, tr), F32), jax.ShapeDtypeStruct((CLASS_ROWS, LANES), F32)],
        scratch_shapes=[pltpu.VMEM((CLASS_ROWS, LANES), F32)],
        compiler_params=_cparams(("arbitrary",), 32),
        name="rank",
    )(cls3)
    return rank.reshape(n), cnt[:N_CLASSES, 0]


def _sc_layout(n):
    info = plsc.get_sparse_core_info()
    nw = info.num_cores * info.num_subcores
    per_worker = n // nw
    assert per_worker * nw == n
    chunk = max(c for c in range(8, SC_MAX_CHUNK + 1, 8) if per_worker % c == 0)
    return info.num_cores, nw, per_worker // chunk, chunk


def _sc_scatter_rows(src, pos, n_out):
    n, w = src.shape
    nc, nw, k, c = _sc_layout(n)
    mesh = plsc.VectorSubcoreMesh(core_axis_name="c", subcore_axis_name="s")

    @functools.partial(
        pl.kernel, mesh=mesh,
        out_type=jax.ShapeDtypeStruct((n_out, w), src.dtype),
        scratch_types=[pltpu.VMEM((k, c), jnp.int32), pltpu.VMEM((c, w), src.dtype), pltpu.SemaphoreType.DMA],
        name="moe_scatter",
    )
    def scatter(src_hbm, pos_hbm, out_hbm, idx_v, rows_v, sem):
        wid = lax.axis_index("s") * nc + lax.axis_index("c")
        pltpu.sync_copy(pos_hbm.at[wid], idx_v)

        @pl.loop(0, k)
        def _(j):
            off = pl.multiple_of(wid * (k * c) + j * c, 8)
            pltpu.sync_copy(src_hbm.at[pl.ds(off, c)], rows_v)
            pltpu.async_copy(rows_v, out_hbm.at[idx_v.at[j]], sem).wait()

    return scatter(src, pos.reshape(nw, k, c))


def _sc_gather_rows(src, pos):
    n = pos.shape[0]
    w = src.shape[1]
    nc, nw, k, c = _sc_layout(n)
    mesh = plsc.VectorSubcoreMesh(core_axis_name="c", subcore_axis_name="s")

    @functools.partial(
        pl.kernel, mesh=mesh,
        out_type=jax.ShapeDtypeStruct((n, w), src.dtype),
        scratch_types=[pltpu.VMEM((k, c), jnp.int32), pltpu.VMEM((c, w), src.dtype), pltpu.SemaphoreType.DMA],
        name="moe_gather",
    )
    def gather(src_hbm, pos_hbm, out_hbm, idx_v, rows_v, sem):
        wid = lax.axis_index("s") * nc + lax.axis_index("c")
        pltpu.sync_copy(pos_hbm.at[wid], idx_v)

        @pl.loop(0, k)
        def _(j):
            off = pl.multiple_of(wid * (k * c) + j * c, 8)
            pltpu.async_copy(src_hbm.at[idx_v.at[j]], rows_v, sem).wait()
            pltpu.sync_copy(rows_v, out_hbm.at[pl.ds(off, c)])

    return gather(src, pos.reshape(nw, k, c))


def _gmm_kernel(lo_ref, hi_ref, new_ref, nact_ref, h_ref, rw_ref,
                w1a_ref, w1b_ref, w3a_ref, w3b_ref, w2a_ref, w2b_ref, o_ref, w13_s, w2_s):
    t = pl.program_id(0)

    @pl.when(new_ref[t] == 1)
    def _():
        for i, (w1_ref, w3_ref, w2_ref) in enumerate(((w1a_ref, w3a_ref, w2a_ref), (w1b_ref, w3b_ref, w2b_ref))):
            w13_s[i, :, :EXPERT_FF] = w1_ref[0, 0].astype(BF16)
            w13_s[i, :, EXPERT_FF:] = w3_ref[0, 0].astype(BF16)
            w2_s[i] = w2_ref[0, 0].astype(BF16)

    @pl.when(t < nact_ref[0])
    def _():
        xf = _unpack_bf16_pairs(h_ref[...])
        x = xf.astype(BF16)

        def expert(i):
            ab = jnp.dot(x, w13_s[i], preferred_element_type=F32)
            a, gate = ab[:, :EXPERT_FF], ab[:, EXPERT_FF:]
            hid = (a * jax.nn.sigmoid(a)) * gate
            return jnp.dot(hid.astype(BF16), w2_s[i], preferred_element_type=F32)

        aff = [jax.nn.sigmoid(jnp.sum(xf * rw_ref[pl.ds(e_ref[t], 1), :], axis=-1, keepdims=True))
               for e_ref in (lo_ref, hi_ref)]
        total = aff[0] + aff[1]
        y = (aff[0] / total) * expert(0) + (aff[1] / total) * expert(1)
        o_ref[...] = _pack_bf16_pairs(y)


def _gmm(tile_lo, tile_hi, tile_new, n_act, hs, rw_t, w1, w3, w2, layer, tm):
    n_pad, half = hs.shape
    d = 2 * half
    row = lambda t, lo, hi, new, na: (jnp.minimum(t, na[0] - 1), 0)
    e_lo = lambda t, lo, hi, new, na: (layer, lo[jnp.minimum(t, na[0] - 1)], 0, 0)
    e_hi = lambda t, lo, hi, new, na: (layer, hi[jnp.minimum(t, na[0] - 1)], 0, 0)
    up = lambda e: pl.BlockSpec((1, 1, d, EXPERT_FF), e)
    down = lambda e: pl.BlockSpec((1, 1, EXPERT_FF, d), e)
    return pl.pallas_call(
        _gmm_kernel,
        grid_spec=pltpu.PrefetchScalarGridSpec(
            num_scalar_prefetch=4,
            grid=(n_pad // tm,),
            in_specs=[
                pl.BlockSpec((tm, half), row),
                pl.BlockSpec(rw_t.shape, lambda t, lo, hi, new, na: (0, 0)),
                up(e_lo), up(e_hi), up(e_lo), up(e_hi), down(e_lo), down(e_hi),
            ],
            out_specs=pl.BlockSpec((tm, half), row),
            scratch_shapes=[pltpu.VMEM((2, d, 2 * EXPERT_FF), BF16), pltpu.VMEM((2, EXPERT_FF, d), BF16)],
        ),
        out_shape=jax.ShapeDtypeStruct((n_pad, half), jnp.int32),
        compiler_params=_cparams(("arbitrary",), 56),
        name="moe_gmm",
    )(tile_lo, tile_hi, tile_new, n_act, hs, rw_t, w1, w1, w3, w3, w2, w2)


def _moe_routed(h2p, route, rw_t, w1, w3, w2, layer, tm=256):
    b, s, half = h2p.shape
    n = b * s
    n_pad = n + N_CLASSES * tm
    cls = route[:, 0, :].reshape(n)
    rank, counts = _rank(cls)
    counts = counts.astype(jnp.int32)
    padded = (counts + tm - 1) // tm * tm
    ends = jnp.cumsum(padded)
    pos = jnp.take(ends - padded, cls.astype(jnp.int32)) + rank.astype(jnp.int32)
    n_act = (ends[-1] // tm).reshape(1)
    tile_row = jnp.arange(n_pad // tm, dtype=jnp.int32) * tm
    tile_cls = jnp.minimum(jnp.sum(tile_row[:, None] >= ends[None, :], axis=1), N_CLASSES - 1)
    pair_lo, pair_hi = (jnp.asarray(a, jnp.int32) for a in _class_experts())
    hs = _sc_scatter_rows(h2p.reshape(n, half), pos, n_pad)
    prev_cls = jnp.concatenate([jnp.full((1,), -1, tile_cls.dtype), tile_cls[:-1]])
    tile_new = ((tile_cls != prev_cls) & (tile_row < ends[-1])).astype(jnp.int32)
    ys = _gmm(jnp.take(pair_lo, tile_cls), jnp.take(pair_hi, tile_cls), tile_new, n_act, hs, rw_t,
              w1, w3, w2, layer, tm)
    return _sc_gather_rows(ys, pos).reshape(b, s, half)


def _class_experts():
    lo, hi = [], []
    for g in range(N_EXPERTS // EXPERTS_PER_GROUP):
        for i in range(EXPERTS_PER_GROUP):
            for j in range(i + 1, EXPERTS_PER_GROUP):
                lo.append(EXPERTS_PER_GROUP * g + i)
                hi.append(EXPERTS_PER_GROUP * g + j)
    return np.array(lo), np.array(hi)


def _final_residual_kernel(x_ref, y_ref, mod_ref, g_ref, o_ref):
    x = _pending_residual(x_ref[0], y_ref, mod_ref)
    o_ref[0] = x * lax.rsqrt(jnp.mean(x * x, axis=-1, keepdims=True) + EPS) * g_ref[...]


def _final_residual(x1, yp, mod, gain, tm=512):
    b, t, d = x1.shape
    tok = lambda bi, j: (bi, j, 0)
    return pl.pallas_call(
        _final_residual_kernel,
        grid=(b, t // tm),
        in_specs=[
            pl.BlockSpec((1, tm, d), tok),
            pl.BlockSpec((1, tm, d // 2), tok),
            pl.BlockSpec((1, 1, 6 * d), lambda bi, j: (bi, 0, 0)),
            _const_spec((1, d)),
        ],
        out_specs=pl.BlockSpec((1, tm, d), tok),
        out_shape=jax.ShapeDtypeStruct((b, t, d), F32),
        compiler_params=_cparams(("parallel", "arbitrary"), 32),
        name="final_residual",
    )(x1, yp, mod, gain)


ROPE_FREQS = ROPE_AXIS_DIM // 2


def _rope_tables(t_lat, n_ctx):
    rows = t_lat // GRID_W
    row = jnp.repeat(jnp.arange(rows, dtype=F32), GRID_W)
    col = jnp.tile(jnp.arange(GRID_W, dtype=F32), rows)
    inv_freq = ROPE_THETA ** (-jnp.arange(0, ROPE_AXIS_DIM, 2, dtype=F32) / ROPE_AXIS_DIM)
    ang = jnp.stack([row[:, None] * inv_freq, col[:, None] * inv_freq], axis=1)
    lanes = (2, LANES // HEAD_DIM, 2, ROPE_FREQS)
    sign = jnp.array([-1.0, 1.0], F32).reshape(1, 2, 1, 1, 1)
    cos = jnp.broadcast_to(jnp.cos(ang)[:, None, None], (t_lat,) + lanes).reshape(t_lat, LANES)
    sin = jnp.broadcast_to(sign * jnp.sin(ang)[:, None, None], (t_lat,) + lanes).reshape(t_lat, LANES)
    pad = lambda tbl, fill: jnp.concatenate([tbl, jnp.full((n_ctx, LANES), fill, F32)], axis=0)
    return pad(cos, 1.0), pad(sin, 0.0)


def _qk_lanes(w, n_heads):
    lead = w.shape[:-1]
    n_chunks = n_heads // N_KV_HEADS
    w = w.reshape(lead + (N_KV_HEADS, n_chunks, 2, 2, ROPE_FREQS))
    nl = len(lead)
    w = jnp.transpose(w, tuple(range(nl)) + (nl + 1, nl + 3, nl, nl + 2, nl + 4))
    return w.reshape(lead + (n_heads * HEAD_DIM,))


def _permute_heads(w, axis):
    shp = w.shape
    w = w.reshape(shp[:axis] + (N_Q_HEADS, HEAD_DIM) + shp[axis + 1:])
    w = jnp.take(w, jnp.array(HEAD_PERM), axis=axis)
    return w.reshape(shp)


def _square_factor(n):
    r = int(round(math.sqrt(n)))
    assert r * r == n, "sequence lengths must be perfect squares for the two-stage DFT"
    return r


def kernel(x, c, ctx, c_ctx, w_ada, b_ada, norm1, norm2, w_in, q_gain, k_gain, sink, pool_w, pool_scale,
           w_branch, w_gate, b_gate, w_out, router_w, router_bias, w1, w3, w2, norm_f):
    b, t_lat, d = x.shape
    n_ctx = ctx.shape[1]
    s = t_lat + n_ctx
    depth = w_ada.shape[0]
    assert d == D_MODEL and b < MOD_ROWS and t_lat % 256 == 0 and n_ctx % 256 == 0 and t_lat % n_ctx == 0

    tokens = (x, ctx)
    cc = jnp.zeros((MOD_ROWS, d), F32).at[:b].set(c).at[b].set(c_ctx)
    mod_all = _ada(cc, w_ada, b_ada).reshape(depth, MOD_ROWS, 1, 6 * d)

    cos, sin = _rope_tables(t_lat, n_ctx)
    lane_head = (np.arange(BRANCH_W) // LANES) * 2 + (np.arange(BRANCH_W) // (HEAD_DIM // 2)) % 2
    seg = jnp.asarray((lane_head[:, None] == lane_head[None, :]) / HEAD_DIM, BF16)
    cs = jnp.asarray(_channel_dft_table()).astype(BF16)
    f_lat = [jnp.asarray(a).astype(BF16) for a in _fourier_tables(*(_square_factor(t_lat),) * 2)]
    f_ctx = [jnp.asarray(a).astype(BF16) for a in _fourier_tables(*(_square_factor(n_ctx),) * 2)]
    wbias = jnp.asarray(_window_bias(n_ctx))
    rw_hi = router_w.T.astype(BF16)
    rw_t = jnp.concatenate([rw_hi, (router_w.T - rw_hi.astype(F32)).astype(BF16)], axis=0)
    rb = router_bias.reshape(N_EXPERTS, 1)

    for l in range(depth):
        need_ctx = l < depth - 1
        s_out = s if need_ctx else t_lat
        cols = jnp.split(w_in[l], np.cumsum((512, 512, 512, 128, 128, 512, 128))[:], axis=1)
        f_w, p_w, qb_w, kb_w, vb_w, qw_w, kw_w, vw_w = cols
        w_in_l = jnp.concatenate([_qk_lanes(qb_w, N_Q_HEADS), _qk_lanes(qw_w, N_Q_HEADS),
                                  _qk_lanes(kb_w, N_KV_HEADS), _qk_lanes(kw_w, N_KV_HEADS),
                                  f_w, vb_w, vw_w, p_w], axis=1).astype(BF16)
        wb_l = jnp.stack([w_branch[l, 0], _permute_heads(w_branch[l, 1], 0),
                          _permute_heads(w_branch[l, 2], 0), w_branch[l, 3]]).astype(BF16)
        mod = mod_all[l]
        n1 = norm1[l].reshape(1, d)
        n2 = norm2[l].reshape(1, d)
        qg = _qk_lanes(jnp.tile(q_gain[l], N_Q_HEADS), N_Q_HEADS).reshape(1, BRANCH_W)
        kg = _qk_lanes(jnp.tile(k_gain[l], N_KV_HEADS), N_KV_HEADS).reshape(1, LANES)

        zr, zi, p_in, qb, qw, kvb, kvw = _inproj(tokens, t_lat, mod, n1, w_in_l, qg, kg, seg, cs, cos, sin)

        out_a = _fourier(zr, zi, f_lat[0], f_lat[1], t_lat, 0)
        out_b = _gattn(qb, kvb, 0, t_lat, 0, s, tq=512)
        out_c = _wattn(jnp.take(sink[l], jnp.array(HEAD_PERM)) * LOG2E, wbias, qw, kvw, s_out, t_lat)

        weights = (w_gate[l].astype(BF16), b_gate[l].reshape(4, 1, d), wb_l, w_out[l].astype(BF16), rw_t, rb)
        pool_params = (pool_w[l].astype(BF16), pool_scale[l].reshape(1, BRANCH_W))
        pending = len(tokens) == 3
        merged = _merge(tokens if pending else tokens[:1], mod, None, n1, n2, (out_a, out_b, out_c), (0, 0, 0),
                        p_in, pool_params, (0, t_lat), weights, s_out, 0, t_lat, 2 * MERGE_CHAIN_ROWS)
        if need_ctx:
            out_ac = _fourier(zr, zi, f_ctx[0], f_ctx[1], n_ctx, t_lat // n_ctx)
            out_bc = _gattn(qb, kvb, t_lat, n_ctx, t_lat, n_ctx)
            merged = _merge(tokens if pending else tokens[1:], mod, b, n1, n2, (out_ac, out_bc, out_c),
                            (0, 0, t_lat), p_in, pool_params, (t_lat, s), weights, s_out, t_lat, n_ctx,
                            MERGE_CHAIN_ROWS, prev=merged)
        x1, h2p, route = merged
        yp = _moe_routed(h2p, route, router_w.T, w1, w3, w2, l)
        tokens = (x1, yp, mod)

    return _final_residual(*tokens, norm_f.reshape(1, d))
```

```python
import functools
import math

import numpy as np
import jax
import jax.numpy as jnp
from jax import lax
from jax.experimental import pallas as pl
from jax.experimental.pallas import tpu as pltpu
from jax.experimental.pallas import tpu_sc as plsc

F32 = jnp.float32
BF16 = jnp.bfloat16

D_MODEL = 1024
HEAD_DIM = 64
N_Q_HEADS = 8
N_KV_HEADS = 2
GRID_W = 64
ROPE_THETA = 10000.0
ROPE_AXIS_DIM = HEAD_DIM // 2
QBLK = 128
WINDOW = 128
BRANCH_W = 512
GROUP_W = 128
POOL_WINDOWS = (2, 4, 8, 16)
N_EXPERTS = 16
EXPERTS_PER_GROUP = 4
EXPERT_FF = 512
EPS = 1e-6
MOD_ROWS = 16
NEG_BIG = -1e30
LOG2E = math.log2(math.e)
LANES = 128
POOL_HALO = 16
PAIRS_PER_GROUP = 6
N_CLASSES = 24
CLASS_ROWS = 32
ROUTE_ROWS = 8
SC_MAX_CHUNK = 128
WATTN_QBLOCKS = 2
MERGE_CHAIN_ROWS = 256

HEAD_PERM = (0, 4, 1, 5, 2, 6, 3, 7)


def _cparams(sem, vmem_mb):
    return pltpu.CompilerParams(dimension_semantics=sem, vmem_limit_bytes=vmem_mb * 1024 * 1024)


def _const_spec(shape):
    nd = len(shape)
    return pl.BlockSpec(shape, lambda *_: (0,) * nd, pipeline_mode=pl.Buffered(1))


def _ada_kernel(c_ref, w_ref, b_ref, o_ref):
    c = c_ref[...]
    s = c * jax.nn.sigmoid(c)
    o_ref[0] = jnp.dot(s.astype(BF16), w_ref[0].astype(BF16), preferred_element_type=F32) + b_ref[0]


def _ada(cc, w_ada, b_ada):
    depth, d, n = w_ada.shape
    tn = 1536
    return pl.pallas_call(
        _ada_kernel,
        grid=(depth, n // tn),
        in_specs=[
            pl.BlockSpec((MOD_ROWS, d), lambda l, j: (0, 0)),
            pl.BlockSpec((1, d, tn), lambda l, j: (l, 0, j)),
            pl.BlockSpec((1, 1, tn), lambda l, j: (l, 0, j)),
        ],
        out_specs=pl.BlockSpec((1, MOD_ROWS, tn), lambda l, j: (l, 0, j)),
        out_shape=jax.ShapeDtypeStruct((depth, MOD_ROWS, n), F32),
        compiler_params=_cparams(("arbitrary", "arbitrary"), 40),
        name="ada",
    )(cc, w_ada, b_ada.reshape(depth, 1, n))


def _norm_mod(x, gain, shift, scale):
    ms = jnp.mean(x * x, axis=-1, keepdims=True)
    return (x * lax.rsqrt(ms + EPS) * gain) * (1.0 + scale) + shift


def _mod_slices(m):
    d = D_MODEL
    return [m[:, i * d:(i + 1) * d] for i in range(6)]


def _head_norm(z, seg, gain):
    ms = jnp.dot((z * z).astype(BF16), seg, preferred_element_type=F32)
    return z * lax.rsqrt(ms + EPS) * gain


def _rope(z, cos, sin):
    outs = []
    for c in range(z.shape[1] // LANES):
        zc = z[:, c * LANES:(c + 1) * LANES]
        outs.append(zc * cos + pltpu.roll(zc, LANES // 2, 1) * sin)
    return outs[0] if len(outs) == 1 else jnp.concatenate(outs, axis=-1)


def _stream_specs(tm, d, n_lat):
    return [pl.BlockSpec((1, tm, d), lambda bi, j: (bi, jnp.minimum(j, n_lat - 1), 0)),
            pl.BlockSpec((1, tm, d), lambda bi, j: (bi, jnp.maximum(j - n_lat, 0), 0))]


def _pending_residual(x, y_ref, modp_ref, rows=slice(None)):
    return x + _mod_slices(modp_ref[0])[5] * _unpack_bf16_pairs(y_ref[0, rows, :])


def _inproj_kernel(*refs, n_lat, pending):
    if pending:
        x_ref, y_ref, modp_ref = refs[:3]
        x = _pending_residual(x_ref[0], y_ref, modp_ref)
    else:
        xl_ref, xc_ref = refs[:2]
        x = jnp.where(pl.program_id(1) >= n_lat, xc_ref[0], xl_ref[0])
    (mod_ref, n1_ref, w_ref, qg_ref, kg_ref, seg_ref, cs_ref, cos_ref, sin_ref,
     zr_ref, zi_ref, p_ref, qb_ref, qw_ref, kvb_ref, kvw_ref) = refs[3 if pending else 2:]
    sh1, sc1 = _mod_slices(mod_ref[0])[:2]
    h = _norm_mod(x, n1_ref[...], sh1, sc1)
    u = jnp.dot(h.astype(BF16), w_ref[...], preferred_element_type=F32)
    cos, sin = cos_ref[...], sin_ref[...]
    w = BRANCH_W
    seg = seg_ref[...]
    qb = _rope(_head_norm(u[:, 0:w], seg, qg_ref[...]), cos, sin)
    qb_ref[0] = (qb * (HEAD_DIM ** -0.5 * LOG2E)).astype(BF16)
    qw = _rope(u[:, w:2 * w], cos, sin)
    qw_ref[0] = (qw * (HEAD_DIM ** -0.5 * LOG2E)).astype(BF16)
    o = 2 * w
    kb = _rope(_head_norm(u[:, o:o + LANES], seg[:LANES, :LANES], kg_ref[...]), cos, sin)
    kw = _rope(u[:, o + LANES:o + 2 * LANES], cos, sin)
    o += 2 * LANES
    f_in = u[:, o:o + w].astype(BF16)
    zr, zi = [], []
    for g in range(w // GROUP_W):
        z = jnp.dot(f_in[:, g * GROUP_W:(g + 1) * GROUP_W], cs_ref[...], preferred_element_type=F32)
        zr.append(z[:, :GROUP_W])
        zi.append(z[:, GROUP_W:])
    zr_ref[0] = jnp.concatenate(zr, axis=-1).astype(BF16)
    zi_ref[0] = jnp.concatenate(zi, axis=-1).astype(BF16)
    o += w
    vb = u[:, o:o + LANES]
    vw = u[:, o + LANES:o + 2 * LANES]
    kvb_ref[0] = jnp.concatenate([kb, vb, jnp.ones_like(vb)], axis=-1).astype(BF16)
    kvw_ref[0] = jnp.concatenate([kw, vw, jnp.ones_like(vw)], axis=-1).astype(BF16)
    p_ref[0] = u[:, o + 2 * LANES:o + 2 * LANES + w].astype(BF16)


def _inproj(tokens, t_lat, mod, n1, w_in, qg, kg, seg, cs, cos, sin, tm=256):
    pending = len(tokens) == 3
    b, _, d = tokens[0].shape
    s = tokens[0].shape[1] if pending else t_lat + tokens[1].shape[1]
    nw = w_in.shape[1]
    n_lat = t_lat // tm
    tok = lambda bi, j: (bi, j, 0)
    tab = lambda bi, j: (j, 0)
    mod_spec = pl.BlockSpec((1, 1, 6 * d), lambda bi, j: (jnp.where(j >= n_lat, b, bi), 0, 0))
    if pending:
        token_specs = [pl.BlockSpec((1, tm, d), tok), pl.BlockSpec((1, tm, d // 2), tok), mod_spec]
    else:
        token_specs = _stream_specs(tm, d, n_lat)
    widths = (BRANCH_W,) * 5 + (3 * LANES, 3 * LANES)
    return pl.pallas_call(
        functools.partial(_inproj_kernel, n_lat=n_lat, pending=pending),
        grid=(b, s // tm),
        in_specs=token_specs + [
            mod_spec,
            _const_spec((1, d)),
            _const_spec((d, nw)),
            _const_spec((1, BRANCH_W)),
            _const_spec((1, LANES)),
            _const_spec((BRANCH_W, BRANCH_W)),
            _const_spec((GROUP_W, 2 * GROUP_W)),
            pl.BlockSpec((tm, LANES), tab),
            pl.BlockSpec((tm, LANES), tab),
        ],
        out_specs=[pl.BlockSpec((1, tm, wd), tok) for wd in widths],
        out_shape=[jax.ShapeDtypeStruct((b, s, wd), BF16) for wd in widths],
        compiler_params=_cparams(("parallel", "arbitrary"), 48),
        name="inproj",
    )(*tokens, mod, n1, w_in, qg, kg, seg, cs, cos, sin)


def _split_heads(qc, lane):
    first = (lane & (HEAD_DIM // 2)) == 0
    zero = jnp.zeros_like(qc)
    return jnp.concatenate([jnp.where(first, qc, zero), jnp.where(first, zero, qc)], axis=0)


def _gattn_kernel(q_ref, kv_ref, o_ref, q2_s, acc_s, *, sub):
    tq = q_ref.shape[1]
    lane = lax.broadcasted_iota(jnp.int32, (1, LANES), 1)
    nt = (((1,), (1,)), ((), ()))
    n_chunks = BRANCH_W // LANES
    for c in range(n_chunks):
        q2_s[2 * c * tq:(2 * c + 2) * tq, :] = _split_heads(q_ref[0, :, c * LANES:(c + 1) * LANES], lane)
    k = kv_ref[0, :, 0:LANES]
    v = kv_ref[0, :, LANES:3 * LANES]
    for r in range(2 * n_chunks * tq // sub):
        rows = slice(r * sub, (r + 1) * sub)
        s = lax.dot_general(q2_s[rows, :], k, nt, preferred_element_type=F32)
        p = jnp.exp2(s - jnp.max(s, axis=-1, keepdims=True))
        acc_s[rows, :] = jnp.dot(p.astype(BF16), v, preferred_element_type=F32)
    for c in range(n_chunks):
        lo = acc_s[2 * c * tq:(2 * c + 1) * tq, :]
        hi = acc_s[(2 * c + 1) * tq:(2 * c + 2) * tq, :]
        o = jnp.where(lane < HEAD_DIM, lo[:, :LANES] / lo[:, LANES:], hi[:, :LANES] / hi[:, LANES:])
        o_ref[0, :, c * LANES:(c + 1) * LANES] = o.astype(BF16)


def _gattn(qb, kv, q_start, q_len, k_start, k_len, tq=256, sub=128):
    b = qb.shape[0]
    assert q_start % tq == 0 and q_len % tq == 0 and k_start % k_len == 0 and k_len % LANES == 0
    rows = 2 * tq * (BRANCH_W // LANES)
    return pl.pallas_call(
        functools.partial(_gattn_kernel, sub=sub),
        grid=(b, q_len // tq),
        in_specs=[
            pl.BlockSpec((1, tq, BRANCH_W), lambda bi, j: (bi, q_start // tq + j, 0)),
            pl.BlockSpec((1, k_len, 3 * LANES), lambda bi, j: (bi, k_start // k_len, 0)),
        ],
        out_specs=pl.BlockSpec((1, tq, BRANCH_W), lambda bi, j: (bi, j, 0)),
        out_shape=jax.ShapeDtypeStruct((b, q_len, BRANCH_W), BF16),
        scratch_shapes=[pltpu.VMEM((rows, LANES), BF16), pltpu.VMEM((rows, 2 * LANES), F32)],
        compiler_params=_cparams(("parallel", "arbitrary"), 48),
        name="gattn",
    )(qb, kv)


def _window_bias(n_ctx):
    tq = WATTN_QBLOCKS * QBLK
    qi = np.arange(tq)[:, None]
    kj = np.arange(tq + 2 * QBLK)[None, :]
    band = np.abs(kj - WINDOW - qi) <= WINDOW
    blk = kj // QBLK
    variants = [band & (blk != 0), band, band & (blk != WATTN_QBLOCKS + 1), np.zeros_like(band)]
    out = [np.concatenate([np.ones((tq, n_ctx), bool), v], axis=1) for v in variants]
    return np.where(np.stack(out), 0.0, NEG_BIG).astype(np.float32)


def _wattn_kernel(sink_ref, bias_ref, q_ref, *refs, sub):
    o_ref, q2_s, kv_s, acc_s = refs[-4:]
    tq = q_ref.shape[1]
    off = 0
    for blk in refs[:-4]:
        kv_s[off:off + blk.shape[1], :] = blk[0]
        off += blk.shape[1]
    lane = lax.broadcasted_iota(jnp.int32, (1, LANES), 1)
    nt = (((1,), (1,)), ((), ()))
    n_chunks = BRANCH_W // LANES
    for c in range(n_chunks):
        q2_s[2 * c * tq:(2 * c + 2) * tq, :] = _split_heads(q_ref[0, :, c * LANES:(c + 1) * LANES], lane)
    k, v = kv_s[:, 0:LANES], kv_s[:, LANES:3 * LANES]
    for r in range(2 * n_chunks * tq // sub):
        rows = slice(r * sub, (r + 1) * sub)
        q_off = (r * sub) % tq
        sk = sink_ref[(r * sub) // tq]
        s = lax.dot_general(q2_s[rows, :], k, nt, preferred_element_type=F32) + bias_ref[0, q_off:q_off + sub, :]
        m = jnp.maximum(jnp.max(s, axis=-1, keepdims=True), sk)
        pv = jnp.dot(jnp.exp2(s - m).astype(BF16), v, preferred_element_type=F32)
        acc_s[rows, :LANES] = pv[:, :LANES]
        acc_s[rows, LANES:] = pv[:, LANES:] + jnp.exp2(sk - m)
    for c in range(n_chunks):
        lo = acc_s[2 * c * tq:(2 * c + 1) * tq, :]
        hi = acc_s[(2 * c + 1) * tq:(2 * c + 2) * tq, :]
        o = jnp.where(lane < HEAD_DIM, lo[:, :LANES] / lo[:, LANES:], hi[:, :LANES] / hi[:, LANES:])
        o_ref[0, :, c * LANES:(c + 1) * LANES] = o.astype(BF16)


def _wattn(sink, bias, qw, kv, s_out, t_lat):
    b, s, _ = qw.shape
    n_ctx = s - t_lat
    nq = WATTN_QBLOCKS
    tq = nq * QBLK
    assert n_ctx % tq == 0 and t_lat // tq >= 2
    last = s // QBLK - 1
    n_lat = t_lat // tq
    variant = lambda j: jnp.where(j >= n_lat, 3, jnp.where(j == 0, 0, jnp.where(j == n_lat - 1, 2, 1)))
    key_block = lambda off: pl.BlockSpec(
        (1, QBLK, 3 * LANES), lambda bi, j: (bi, jnp.clip(j * nq + off, 0, last), 0))
    rows = 2 * tq * (BRANCH_W // LANES)
    return pl.pallas_call(
        functools.partial(_wattn_kernel, sub=128),
        scratch_shapes=[pltpu.VMEM((rows, LANES), BF16), pltpu.VMEM((bias.shape[2], 3 * LANES), BF16),
                        pltpu.VMEM((rows, 2 * LANES), F32)],
        grid=(b, s_out // tq),
        in_specs=[
            pl.BlockSpec(memory_space=pltpu.SMEM),
            pl.BlockSpec((1,) + bias.shape[1:], lambda bi, j: (variant(j), 0, 0)),
            pl.BlockSpec((1, tq, BRANCH_W), lambda bi, j: (bi, j, 0)),
            pl.BlockSpec((1, n_ctx, 3 * LANES), lambda bi, j: (bi, t_lat // n_ctx, 0)),
        ] + [key_block(off) for off in range(-1, nq + 1)],
        out_specs=pl.BlockSpec((1, tq, BRANCH_W), lambda bi, j: (bi, j, 0)),
        out_shape=jax.ShapeDtypeStruct((b, s_out, BRANCH_W), BF16),
        compiler_params=_cparams(("parallel", "arbitrary"), 32),
        name="wattn",
    )(sink, bias, qw, *([kv] * (nq + 3)))


def _fourier_tables(n1, n2):
    t = n1 * n2
    k2 = np.arange(n2)[None, :, None]
    t2 = np.arange(n2)[None, None, :]
    t1 = np.arange(n1)[:, None, None]
    theta = 2.0 * np.pi * ((k2 * t2 * n1 + k2 * t1) % t) / t
    er, ei = np.cos(theta) / math.sqrt(n2), -np.sin(theta) / math.sqrt(n2)
    e = np.concatenate([np.concatenate([er, -ei], axis=2), np.concatenate([ei, er], axis=2)], axis=1)
    k1 = np.arange(n1)[:, None]
    phi = 2.0 * np.pi * ((k1 * np.arange(n1)[None, :]) % n1) / n1
    dcat = np.concatenate([np.cos(phi), np.sin(phi)], axis=1) / math.sqrt(n1)
    return e.astype(np.float32), dcat.astype(np.float32)


def _channel_dft_table():
    c = np.arange(GROUP_W)
    ang = 2.0 * np.pi * ((c[:, None] * c[None, :]) % GROUP_W) / GROUP_W
    return (np.concatenate([np.cos(ang), -np.sin(ang)], axis=1) / math.sqrt(GROUP_W)).astype(np.float32)


def _fourier_kernel(zr_ref, zi_ref, e_ref, d_ref, o_ref, xr_s, xi_s, yr_s, yi_s, *, n1, n2):
    nc = xr_s.shape[0]
    chunk = lambda c: slice(c * LANES, (c + 1) * LANES)

    def gather(ref, start, size, stride):
        return jnp.concatenate([ref[c, pl.ds(start, size, stride=stride), :] for c in range(nc)], axis=-1)

    for c in range(nc):
        xr_s[c] = zr_ref[0, :, chunk(c)].astype(F32)
        xi_s[c] = zi_ref[0, :, chunk(c)].astype(F32)
    for t1 in range(n1):
        xs = jnp.concatenate([gather(xr_s, t1, n2, n1), gather(xi_s, t1, n2, n1)], axis=0).astype(BF16)
        y = jnp.dot(e_ref[t1], xs, preferred_element_type=F32)
        for c in range(nc):
            yr_s[c, t1 * n2:(t1 + 1) * n2, :] = y[:n2, chunk(c)]
            yi_s[c, t1 * n2:(t1 + 1) * n2, :] = y[n2:, chunk(c)]
    for k2 in range(n2):
        ys = jnp.concatenate([gather(yr_s, k2, n1, n2), gather(yi_s, k2, n1, n2)], axis=0).astype(BF16)
        o = jnp.dot(d_ref[...], ys, preferred_element_type=F32)
        for c in range(nc):
            xr_s[c, pl.ds(k2, n1, stride=n2), :] = o[:, chunk(c)]
    for c in range(nc):
        o_ref[0, :, chunk(c)] = xr_s[c].astype(BF16)


def _fourier(zr, zi, e_tab, d_tab, t_len, row_block, cw=256):
    b = zr.shape[0]
    n1 = d_tab.shape[0]
    n2 = t_len // n1
    zspec = pl.BlockSpec((1, t_len, cw), lambda bi, j: (bi, row_block, j))
    return pl.pallas_call(
        functools.partial(_fourier_kernel, n1=n1, n2=n2),
        grid=(b, BRANCH_W // cw),
        in_specs=[zspec, zspec, _const_spec(e_tab.shape), _const_spec(d_tab.shape)],
        out_specs=pl.BlockSpec((1, t_len, cw), lambda bi, j: (bi, 0, j)),
        out_shape=jax.ShapeDtypeStruct((b, t_len, BRANCH_W), BF16),
        scratch_shapes=[pltpu.VMEM((cw // LANES, t_len, LANES), F32)] * 4,
        compiler_params=_cparams(("parallel", "arbitrary"), 48),
        name="fourier",
    )(zr, zi, e_tab, d_tab)


def _pool_rows(ext, pos, n, w_ref, scale):
    n_ext = ext.shape[0]
    rows = n_ext - 2 * POOL_HALO
    back = lambda v, k: pltpu.roll(v, k, 0)
    fwd = lambda v, k: pltpu.roll(v, n_ext - k, 0)
    outs = []
    for gi, w in enumerate(POOL_WINDOWS):
        e = ext[:, gi * GROUP_W:(gi + 1) * GROUP_W]
        wsum = e + back(e, 1)
        half = 1
        while 2 * half < w:
            wsum = back(wsum, half) + fwd(wsum, half)
            half *= 2
        own = slice(POOL_HALO, POOL_HALO + rows)
        cnt = jnp.minimum(pos + w // 2, n) - jnp.maximum(pos - w // 2, 0)
        pooled = wsum[own] / cnt.astype(F32) - e[own]
        outs.append(jnp.dot(pooled.astype(BF16), w_ref[gi], preferred_element_type=F32))
    return (jnp.concatenate(outs, axis=-1) * scale).astype(BF16)


def _route(logits_t, bias):
    aff = jax.nn.sigmoid(logits_t)
    sel = aff + bias
    neg = -jnp.inf
    firsts, seconds, scores = [], [], []
    for g in range(N_EXPERTS // EXPERTS_PER_GROUP):
        s = [sel[EXPERTS_PER_GROUP * g + k:EXPERTS_PER_GROUP * g + k + 1, :] for k in range(EXPERTS_PER_GROUP)]
        m1 = jnp.maximum(jnp.maximum(s[0], s[1]), jnp.maximum(s[2], s[3]))
        i1 = jnp.where(s[0] == m1, 0, jnp.where(s[1] == m1, 1, jnp.where(s[2] == m1, 2, 3)))
        r = [jnp.where(i1 == k, neg, s[k]) for k in range(EXPERTS_PER_GROUP)]
        m2 = jnp.maximum(jnp.maximum(r[0], r[1]), jnp.maximum(r[2], r[3]))
        i2 = jnp.where(r[0] == m2, 0, jnp.where(r[1] == m2, 1, jnp.where(r[2] == m2, 2, 3)))
        firsts.append(i1 + EXPERTS_PER_GROUP * g)
        seconds.append(i2 + EXPERTS_PER_GROUP * g)
        scores.append(m1 + m2)
    best = jnp.maximum(jnp.maximum(scores[0], scores[1]), jnp.maximum(scores[2], scores[3]))
    pick = lambda v: jnp.where(scores[0] == best, v[0], jnp.where(scores[1] == best, v[1],
                                                                 jnp.where(scores[2] == best, v[2], v[3])))
    e1, e2 = pick(firsts), pick(seconds)
    lo = jnp.minimum(e1, e2) & (EXPERTS_PER_GROUP - 1)
    hi = jnp.maximum(e1, e2) & (EXPERTS_PER_GROUP - 1)
    pair = jnp.where(lo == 0, 0, jnp.where(lo == 1, 3, 5)) + hi - lo - 1
    cls = ((e1 >> 2) * PAIRS_PER_GROUP + pair).astype(F32)
    return jnp.concatenate([cls] + [jnp.zeros_like(cls)] * (ROUTE_ROWS - 1), axis=0)


def _pack_bf16_pairs(v):
    w = v.shape[1] // 2
    bits = pltpu.bitcast(v.astype(BF16).astype(F32), jnp.uint32)
    return pltpu.bitcast(bits[:, :w] | (bits[:, w:] >> 16), jnp.int32)


def _unpack_bf16_pairs(p):
    bits = pltpu.bitcast(p, jnp.uint32)
    hi = pltpu.bitcast(bits & jnp.uint32(0xFFFF0000), F32)
    lo = pltpu.bitcast(bits << 16, F32)
    return jnp.concatenate([hi, lo], axis=-1)


def _merge_kernel(*refs, n_real, n_fill, pending, **static):
    n_tok = 3 if pending else 1
    (mod_ref, n1_ref, n2_ref, a_ref, b_ref, c_ref, pc_ref, pp_ref, pn_ref, pw_ref, psc_ref,
     wg_ref, bg_ref, wb_ref, wo_ref, rw_ref, rb_ref) = refs[n_tok:n_tok + 17]
    outs = refs[-4:-1]
    ins = (refs[:n_tok], mod_ref, n1_ref, n2_ref, a_ref, b_ref, c_ref,
           (pc_ref, pp_ref, pn_ref, pw_ref, psc_ref),
           wg_ref, bg_ref, wb_ref, wo_ref, rw_ref, rb_ref)
    if n_fill:
        @pl.when(pl.program_id(1) >= n_real)
        def _():
            for ref in outs:
                ref[...] = jnp.zeros_like(ref)

        pl.when(pl.program_id(1) < n_real)(lambda: _merge_tile(ins, outs, refs[-1], **static))
    else:
        _merge_tile(ins, outs, refs[-1], **static)


def _merge_tile(ins, outs, merged_s, *, nw, sub, p_off, seq_lo, seq_hi):
    (tok_refs, mod_ref, n1_ref, n2_ref, a_ref, b_ref, c_ref, pool_refs,
     wg_ref, bg_ref, wb_ref, wo_ref, rw_ref, rb_ref) = ins
    xo_ref, h2_ref, route_ref = outs
    pc_ref, pp_ref, pn_ref, pw_ref, psc_ref = pool_refs
    x_ref = tok_refs[0]
    tm = x_ref.shape[1]
    sh1, sc1, g1, sh2, sc2, _ = _mod_slices(mod_ref[0])
    row0 = p_off + pl.program_id(1) * tm
    ext = jnp.concatenate([pp_ref[0], pc_ref[0], pn_ref[0]], axis=0).astype(F32)
    gpos = row0 - POOL_HALO + lax.broadcasted_iota(jnp.int32, (tm + 2 * POOL_HALO, 1), 0)
    ext = jnp.where((gpos >= seq_lo) & (gpos < seq_hi), ext, 0.0)
    for r in range(tm // sub):
        rows = slice(r * sub, (r + 1) * sub)
        x = x_ref[0, rows, :]
        if len(tok_refs) == 3:
            x = _pending_residual(x, tok_refs[1], tok_refs[2], rows)
        hb = _norm_mod(x, n1_ref[...], sh1, sc1).astype(BF16)
        pos = row0 - seq_lo + r * sub + lax.broadcasted_iota(jnp.int32, (sub, 1), 0)
        pooled = _pool_rows(ext[r * sub:(r + 1) * sub + 2 * POOL_HALO], pos, seq_hi - seq_lo,
                            pw_ref, psc_ref[...])
        branches = (a_ref[0, rows, :], b_ref[0, rows, :], c_ref[0, rows, :], pooled)
        for n in range(D_MODEL // nw):
            cols = slice(n * nw, (n + 1) * nw)
            merged = None
            for i, br in enumerate(branches):
                gate = jax.nn.sigmoid(
                    jnp.dot(hb, wg_ref[i, :, cols], preferred_element_type=F32) + bg_ref[i, :, cols])
                term = gate * jnp.dot(br, wb_ref[i, :, cols], preferred_element_type=F32)
                merged = term if merged is None else merged + term
            merged_s[rows, cols] = merged.astype(BF16)
        y = jnp.dot(merged_s[rows, :], wo_ref[...], preferred_element_type=F32)
        xn = x + g1 * y
        xo_ref[0, rows, :] = xn
        h2 = _norm_mod(xn, n2_ref[...], sh2, sc2)
        h2_ref[0, rows, :] = _pack_bf16_pairs(h2)
        h_hi = h2.astype(BF16)
        h_lo = (h2 - h_hi.astype(F32)).astype(BF16)
        nt = (((1,), (1,)), ((), ()))
        by_hi = lax.dot_general(rw_ref[...], h_hi, nt, preferred_element_type=F32)
        by_lo = lax.dot_general(rw_ref[:N_EXPERTS, :], h_lo, nt, preferred_element_type=F32)
        logits_t = by_hi[:N_EXPERTS] + by_hi[N_EXPERTS:] + by_lo
        route_ref[0, :, rows] = _route(logits_t, rb_ref[...])


def _merge(tokens, mod, mod_row, n1, n2, branches, offsets, p_in, pool_params, seq, weights, s_out, out_off,
           rows, tm, prev=None):
    pending = len(tokens) == 3
    b, _, d = tokens[0].shape
    wg, bg, wb, wo, rw_t, rb = weights
    pool_w, pool_scale = pool_params
    seq_lo, seq_hi = seq
    n_real = rows // tm
    n_fill = -(-(s_out - out_off - rows) // tm) if prev is None else 0
    step = lambda j: jnp.minimum(j, n_real - 1)
    blk = lambda width, off: pl.BlockSpec((1, tm, width), lambda bi, j: (bi, off // tm + step(j), 0))
    out_blk = lambda width: pl.BlockSpec((1, tm, width), lambda bi, j: (bi, out_off // tm + j, 0))
    assert all(o % tm == 0 for o in offsets) and out_off % tm == 0 and rows % tm == 0 and seq_lo % tm == 0
    hb = tm // POOL_HALO
    last_halo = p_in.shape[1] // POOL_HALO - 1
    halo = lambda shift: pl.BlockSpec(
        (1, POOL_HALO, BRANCH_W),
        lambda bi, j: (bi, jnp.clip((seq_lo // tm + step(j) + shift) * hb - 1 + shift, 0, last_halo), 0))
    out_shape = [
        jax.ShapeDtypeStruct((b, s_out, d), F32),
        jax.ShapeDtypeStruct((b, s_out, d // 2), jnp.int32),
        jax.ShapeDtypeStruct((b, ROUTE_ROWS, s_out), F32),
    ]
    mod_spec = pl.BlockSpec((1, 1, 6 * d), lambda bi, j: (bi if mod_row is None else mod_row, 0, 0))
    tok_off = out_off if pending else 0
    token_specs = [blk(d, tok_off)] + ([blk(d // 2, tok_off), mod_spec] if pending else [])
    n_in = len(tokens) + 17
    extra_specs = [] if prev is None else [pl.BlockSpec(memory_space=pl.ANY)] * 3
    return pl.pallas_call(
        functools.partial(_merge_kernel, nw=512, sub=MERGE_CHAIN_ROWS, n_real=n_real, n_fill=n_fill,
                          pending=pending, p_off=seq_lo, seq_lo=seq_lo, seq_hi=seq_hi),
        scratch_shapes=[pltpu.VMEM((tm, d), BF16)],
        grid=(b, n_real + n_fill),
        in_specs=token_specs + [
            mod_spec,
            _const_spec((1, d)), _const_spec((1, d)),
            *[blk(BRANCH_W, off) for off in offsets],
            blk(BRANCH_W, seq_lo), halo(0), halo(1),
            _const_spec(pool_w.shape), _const_spec((1, BRANCH_W)),
            _const_spec(wg.shape), _const_spec(bg.shape), _const_spec(wb.shape), _const_spec(wo.shape),
            _const_spec(rw_t.shape), _const_spec(rb.shape),
        ] + extra_specs,
        out_specs=[
            out_blk(d),
            out_blk(d // 2),
            pl.BlockSpec((1, ROUTE_ROWS, tm), lambda bi, j: (bi, 0, out_off // tm + j)),
        ],
        out_shape=out_shape,
        input_output_aliases={} if prev is None else {n_in + i: i for i in range(3)},
        compiler_params=_cparams(("parallel", "arbitrary"), 56),
        name="merge",
    )(*tokens, mod, n1, n2, *branches, p_in, p_in, p_in, pool_w, pool_scale,
      wg, bg, wb, wo, rw_t, rb, *(() if prev is None else prev))


def _rank_kernel(cls_ref, rank_ref, cnt_ref, cnt_s, *, tr):
    @pl.when(pl.program_id(0) == 0)
    def _():
        cnt_s[...] = jnp.zeros_like(cnt_s)

    cls = cls_ref[0]
    cid = lax.broadcasted_iota(jnp.int32, (CLASS_ROWS, tr), 0).astype(F32)
    onehot = cid == cls
    before = lax.broadcasted_iota(jnp.int32, (tr, tr), 0) < lax.broadcasted_iota(jnp.int32, (tr, tr), 1)
    prefix = jnp.dot(jnp.where(onehot, 1.0, 0.0).astype(BF16), jnp.where(before, 1.0, 0.0).astype(BF16),
                     preferred_element_type=F32)
    carry = cnt_s[...][:, 0:1]
    rank_ref[0] = jnp.sum(jnp.where(onehot, prefix + carry, 0.0), axis=0, keepdims=True)
    cnt_s[...] += jnp.sum(jnp.where(onehot, 1.0, 0.0), axis=1, keepdims=True)
    cnt_ref[...] = cnt_s[...]


def _rank(cls_flat, tr=512):
    n = cls_flat.shape[0]
    tr = math.gcd(n, tr)
    cls3 = cls_flat.reshape(n // tr, 1, tr)
    rank, cnt = pl.pallas_call(
        functools.partial(_rank_kernel, tr=tr),
        grid=(n // tr,),
        in_specs=[pl.BlockSpec((1, 1, tr), lambda i: (i, 0, 0))],
        out_specs=[pl.BlockSpec((1, 1, tr), lambda i: (i, 0, 0)), _const_spec((CLASS_ROWS, LANES))],
        out_shape=[jax.ShapeDtypeStruct((n // tr, 1, tr), F32), jax.ShapeDtypeStruct((CLASS_ROWS, LANES), F32)],
        scratch_shapes=[pltpu.VMEM((CLASS_ROWS, LANES), F32)],
        compiler_params=_cparams(("arbitrary",), 32),
        name="rank",
    )(cls3)
    return rank.reshape(n), cnt[:N_CLASSES, 0]


def _sc_layout(n):
    info = plsc.get_sparse_core_info()
    nw = info.num_cores * info.num_subcores
    per_worker = n // nw
    assert per_worker * nw == n
    chunk = max(c for c in range(8, SC_MAX_CHUNK + 1, 8) if per_worker % c == 0)
    return info.num_cores, nw, per_worker // chunk, chunk


def _sc_scatter_rows(src, pos, n_out):
    n, w = src.shape
    nc, nw, k, c = _sc_layout(n)
    mesh = plsc.VectorSubcoreMesh(core_axis_name="c", subcore_axis_name="s")

    @functools.partial(
        pl.kernel, mesh=mesh,
        out_type=jax.ShapeDtypeStruct((n_out, w), src.dtype),
        scratch_types=[pltpu.VMEM((k, c), jnp.int32), pltpu.VMEM((c, w), src.dtype), pltpu.SemaphoreType.DMA],
        name="moe_scatter",
    )
    def scatter(src_hbm, pos_hbm, out_hbm, idx_v, rows_v, sem):
        wid = lax.axis_index("s") * nc + lax.axis_index("c")
        pltpu.sync_copy(pos_hbm.at[wid], idx_v)

        @pl.loop(0, k)
        def _(j):
            off = pl.multiple_of(wid * (k * c) + j * c, 8)
            pltpu.sync_copy(src_hbm.at[pl.ds(off, c)], rows_v)
            pltpu.async_copy(rows_v, out_hbm.at[idx_v.at[j]], sem).wait()

    return scatter(src, pos.reshape(nw, k, c))


def _sc_gather_rows(src, pos):
    n = pos.shape[0]
    w = src.shape[1]
    nc, nw, k, c = _sc_layout(n)
    mesh = plsc.VectorSubcoreMesh(core_axis_name="c", subcore_axis_name="s")

    @functools.partial(
        pl.kernel, mesh=mesh,
        out_type=jax.ShapeDtypeStruct((n, w), src.dtype),
        scratch_types=[pltpu.VMEM((k, c), jnp.int32), pltpu.VMEM((c, w), src.dtype), pltpu.SemaphoreType.DMA],
        name="moe_gather",
    )
    def gather(src_hbm, pos_hbm, out_hbm, idx_v, rows_v, sem):
        wid = lax.axis_index("s") * nc + lax.axis_index("c")
        pltpu.sync_copy(pos_hbm.at[wid], idx_v)

        @pl.loop(0, k)
        def _(j):
            off = pl.multiple_of(wid * (k * c) + j * c, 8)
            pltpu.async_copy(src_hbm.at[idx_v.at[j]], rows_v, sem).wait()
            pltpu.sync_copy(rows_v, out_hbm.at[pl.ds(off, c)])

    return gather(src, pos.reshape(nw, k, c))


def _gmm_kernel(lo_ref, hi_ref, new_ref, nact_ref, h_ref, rw_ref,
                w1a_ref, w1b_ref, w3a_ref, w3b_ref, w2a_ref, w2b_ref, o_ref, w13_s, w2_s):
    t = pl.program_id(0)

    @pl.when(new_ref[t] == 1)
    def _():
        for i, (w1_ref, w3_ref, w2_ref) in enumerate(((w1a_ref, w3a_ref, w2a_ref), (w1b_ref, w3b_ref, w2b_ref))):
            w13_s[i, :, :EXPERT_FF] = w1_ref[0, 0].astype(BF16)
            w13_s[i, :, EXPERT_FF:] = w3_ref[0, 0].astype(BF16)
            w2_s[i] = w2_ref[0, 0].astype(BF16)

    @pl.when(t < nact_ref[0])
    def _():
        xf = _unpack_bf16_pairs(h_ref[...])
        x = xf.astype(BF16)

        def expert(i):
            ab = jnp.dot(x, w13_s[i], preferred_element_type=F32)
            a, gate = ab[:, :EXPERT_FF], ab[:, EXPERT_FF:]
            hid = (a * jax.nn.sigmoid(a)) * gate
            return jnp.dot(hid.astype(BF16), w2_s[i], preferred_element_type=F32)

        aff = [jax.nn.sigmoid(jnp.sum(xf * rw_ref[pl.ds(e_ref[t], 1), :], axis=-1, keepdims=True))
               for e_ref in (lo_ref, hi_ref)]
        total = aff[0] + aff[1]
        y = (aff[0] / total) * expert(0) + (aff[1] / total) * expert(1)
        o_ref[...] = _pack_bf16_pairs(y)


def _gmm(tile_lo, tile_hi, tile_new, n_act, hs, rw_t, w1, w3, w2, layer, tm):
    n_pad, half = hs.shape
    d = 2 * half
    row = lambda t, lo, hi, new, na: (jnp.minimum(t, na[0] - 1), 0)
    e_lo = lambda t, lo, hi, new, na: (layer, lo[jnp.minimum(t, na[0] - 1)], 0, 0)
    e_hi = lambda t, lo, hi, new, na: (layer, hi[jnp.minimum(t, na[0] - 1)], 0, 0)
    up = lambda e: pl.BlockSpec((1, 1, d, EXPERT_FF), e)
    down = lambda e: pl.BlockSpec((1, 1, EXPERT_FF, d), e)
    return pl.pallas_call(
        _gmm_kernel,
        grid_spec=pltpu.PrefetchScalarGridSpec(
            num_scalar_prefetch=4,
            grid=(n_pad // tm,),
            in_specs=[
                pl.BlockSpec((tm, half), row),
                pl.BlockSpec(rw_t.shape, lambda t, lo, hi, new, na: (0, 0)),
                up(e_lo), up(e_hi), up(e_lo), up(e_hi), down(e_lo), down(e_hi),
            ],
            out_specs=pl.BlockSpec((tm, half), row),
            scratch_shapes=[pltpu.VMEM((2, d, 2 * EXPERT_FF), BF16), pltpu.VMEM((2, EXPERT_FF, d), BF16)],
        ),
        out_shape=jax.ShapeDtypeStruct((n_pad, half), jnp.int32),
        compiler_params=_cparams(("arbitrary",), 56),
        name="moe_gmm",
    )(tile_lo, tile_hi, tile_new, n_act, hs, rw_t, w1, w1, w3, w3, w2, w2)


def _moe_routed(h2p, route, rw_t, w1, w3, w2, layer, tm=256):
    b, s, half = h2p.shape
    n = b * s
    n_pad = n + N_CLASSES * tm
    cls = route[:, 0, :].reshape(n)
    rank, counts = _rank(cls)
    counts = counts.astype(jnp.int32)
    padded = (counts + tm - 1) // tm * tm
    ends = jnp.cumsum(padded)
    pos = jnp.take(ends - padded, cls.astype(jnp.int32)) + rank.astype(jnp.int32)
    n_act = (ends[-1] // tm).reshape(1)
    tile_row = jnp.arange(n_pad // tm, dtype=jnp.int32) * tm
    tile_cls = jnp.minimum(jnp.sum(tile_row[:, None] >= ends[None, :], axis=1), N_CLASSES - 1)
    pair_lo, pair_hi = (jnp.asarray(a, jnp.int32) for a in _class_experts())
    hs = _sc_scatter_rows(h2p.reshape(n, half), pos, n_pad)
    prev_cls = jnp.concatenate([jnp.full((1,), -1, tile_cls.dtype), tile_cls[:-1]])
    tile_new = ((tile_cls != prev_cls) & (tile_row < ends[-1])).astype(jnp.int32)
    ys = _gmm(jnp.take(pair_lo, tile_cls), jnp.take(pair_hi, tile_cls), tile_new, n_act, hs, rw_t,
              w1, w3, w2, layer, tm)
    return _sc_gather_rows(ys, pos).reshape(b, s, half)


def _class_experts():
    lo, hi = [], []
    for g in range(N_EXPERTS // EXPERTS_PER_GROUP):
        for i in range(EXPERTS_PER_GROUP):
            for j in range(i + 1, EXPERTS_PER_GROUP):
                lo.append(EXPERTS_PER_GROUP * g + i)
                hi.append(EXPERTS_PER_GROUP * g + j)
    return np.array(lo), np.array(hi)


def _final_residual_kernel(x_ref, y_ref, mod_ref, g_ref, o_ref):
    x = _pending_residual(x_ref[0], y_ref, mod_ref)
    o_ref[0] = x * lax.rsqrt(jnp.mean(x * x, axis=-1, keepdims=True) + EPS) * g_ref[...]


def _final_residual(x1, yp, mod, gain, tm=512):
    b, t, d = x1.shape
    tok = lambda bi, j: (bi, j, 0)
    return pl.pallas_call(
        _final_residual_kernel,
        grid=(b, t // tm),
        in_specs=[
            pl.BlockSpec((1, tm, d), tok),
            pl.BlockSpec((1, tm, d // 2), tok),
            pl.BlockSpec((1, 1, 6 * d), lambda bi, j: (bi, 0, 0)),
            _const_spec((1, d)),
        ],
        out_specs=pl.BlockSpec((1, tm, d), tok),
        out_shape=jax.ShapeDtypeStruct((b, t, d), F32),
        compiler_params=_cparams(("parallel", "arbitrary"), 32),
        name="final_residual",
    )(x1, yp, mod, gain)


ROPE_FREQS = ROPE_AXIS_DIM // 2


def _rope_tables(t_lat, n_ctx):
    rows = t_lat // GRID_W
    row = jnp.repeat(jnp.arange(rows, dtype=F32), GRID_W)
    col = jnp.tile(jnp.arange(GRID_W, dtype=F32), rows)
    inv_freq = ROPE_THETA ** (-jnp.arange(0, ROPE_AXIS_DIM, 2, dtype=F32) / ROPE_AXIS_DIM)
    ang = jnp.stack([row[:, None] * inv_freq, col[:, None] * inv_freq], axis=1)
    lanes = (2, LANES // HEAD_DIM, 2, ROPE_FREQS)
    sign = jnp.array([-1.0, 1.0], F32).reshape(1, 2, 1, 1, 1)
    cos = jnp.broadcast_to(jnp.cos(ang)[:, None, None], (t_lat,) + lanes).reshape(t_lat, LANES)
    sin = jnp.broadcast_to(sign * jnp.sin(ang)[:, None, None], (t_lat,) + lanes).reshape(t_lat, LANES)
    pad = lambda tbl, fill: jnp.concatenate([tbl, jnp.full((n_ctx, LANES), fill, F32)], axis=0)
    return pad(cos, 1.0), pad(sin, 0.0)


def _qk_lanes(w, n_heads):
    lead = w.shape[:-1]
    n_chunks = n_heads // N_KV_HEADS
    w = w.reshape(lead + (N_KV_HEADS, n_chunks, 2, 2, ROPE_FREQS))
    nl = len(lead)
    w = jnp.transpose(w, tuple(range(nl)) + (nl + 1, nl + 3, nl, nl + 2, nl + 4))
    return w.reshape(lead + (n_heads * HEAD_DIM,))


def _permute_heads(w, axis):
    shp = w.shape
    w = w.reshape(shp[:axis] + (N_Q_HEADS, HEAD_DIM) + shp[axis + 1:])
    w = jnp.take(w, jnp.array(HEAD_PERM), axis=axis)
    return w.reshape(shp)


def _square_factor(n):
    r = int(round(math.sqrt(n)))
    assert r * r == n, "sequence lengths must be perfect squares for the two-stage DFT"
    return r


def kernel(x, c, ctx, c_ctx, w_ada, b_ada, norm1, norm2, w_in, q_gain, k_gain, sink, pool_w, pool_scale,
           w_branch, w_gate, b_gate, w_out, router_w, router_bias, w1, w3, w2, norm_f):
    b, t_lat, d = x.shape
    n_ctx = ctx.shape[1]
    s = t_lat + n_ctx
    depth = w_ada.shape[0]
    assert d == D_MODEL and b < MOD_ROWS and t_lat % 256 == 0 and n_ctx % 256 == 0 and t_lat % n_ctx == 0

    tokens = (x, ctx)
    cc = jnp.zeros((MOD_ROWS, d), F32).at[:b].set(c).at[b].set(c_ctx)
    mod_all = _ada(cc, w_ada, b_ada).reshape(depth, MOD_ROWS, 1, 6 * d)

    cos, sin = _rope_tables(t_lat, n_ctx)
    lane_head = (np.arange(BRANCH_W) // LANES) * 2 + (np.arange(BRANCH_W) // (HEAD_DIM // 2)) % 2
    seg = jnp.asarray((lane_head[:, None] == lane_head[None, :]) / HEAD_DIM, BF16)
    cs = jnp.asarray(_channel_dft_table()).astype(BF16)
    f_lat = [jnp.asarray(a).astype(BF16) for a in _fourier_tables(*(_square_factor(t_lat),) * 2)]
    f_ctx = [jnp.asarray(a).astype(BF16) for a in _fourier_tables(*(_square_factor(n_ctx),) * 2)]
    wbias = jnp.asarray(_window_bias(n_ctx))
    rw_hi = router_w.T.astype(BF16)
    rw_t = jnp.concatenate([rw_hi, (router_w.T - rw_hi.astype(F32)).astype(BF16)], axis=0)
    rb = router_bias.reshape(N_EXPERTS, 1)

    for l in range(depth):
        need_ctx = l < depth - 1
        s_out = s if need_ctx else t_lat
        cols = jnp.split(w_in[l], np.cumsum((512, 512, 512, 128, 128, 512, 128))[:], axis=1)
        f_w, p_w, qb_w, kb_w, vb_w, qw_w, kw_w, vw_w = cols
        w_in_l = jnp.concatenate([_qk_lanes(qb_w, N_Q_HEADS), _qk_lanes(qw_w, N_Q_HEADS),
                                  _qk_lanes(kb_w, N_KV_HEADS), _qk_lanes(kw_w, N_KV_HEADS),
                                  f_w, vb_w, vw_w, p_w], axis=1).astype(BF16)
        wb_l = jnp.stack([w_branch[l, 0], _permute_heads(w_branch[l, 1], 0),
                          _permute_heads(w_branch[l, 2], 0), w_branch[l, 3]]).astype(BF16)
        mod = mod_all[l]
        n1 = norm1[l].reshape(1, d)
        n2 = norm2[l].reshape(1, d)
        qg = _qk_lanes(jnp.tile(q_gain[l], N_Q_HEADS), N_Q_HEADS).reshape(1, BRANCH_W)
        kg = _qk_lanes(jnp.tile(k_gain[l], N_KV_HEADS), N_KV_HEADS).reshape(1, LANES)

        zr, zi, p_in, qb, qw, kvb, kvw = _inproj(tokens, t_lat, mod, n1, w_in_l, qg, kg, seg, cs, cos, sin)

        out_a = _fourier(zr, zi, f_lat[0], f_lat[1], t_lat, 0)
        out_b = _gattn(qb, kvb, 0, t_lat, 0, s, tq=512)
        out_c = _wattn(jnp.take(sink[l], jnp.array(HEAD_PERM)) * LOG2E, wbias, qw, kvw, s_out, t_lat)

        weights = (w_gate[l].astype(BF16), b_gate[l].reshape(4, 1, d), wb_l, w_out[l].astype(BF16), rw_t, rb)
        pool_params = (pool_w[l].astype(BF16), pool_scale[l].reshape(1, BRANCH_W))
        pending = len(tokens) == 3
        merged = _merge(tokens if pending else tokens[:1], mod, None, n1, n2, (out_a, out_b, out_c), (0, 0, 0),
                        p_in, pool_params, (0, t_lat), weights, s_out, 0, t_lat, 4 * MERGE_CHAIN_ROWS)
        if need_ctx:
            out_ac = _fourier(zr, zi, f_ctx[0], f_ctx[1], n_ctx, t_lat // n_ctx)
            out_bc = _gattn(qb, kvb, t_lat, n_ctx, t_lat, n_ctx)
            merged = _merge(tokens if pending else tokens[1:], mod, b, n1, n2, (out_ac, out_bc, out_c),
                            (0, 0, t_lat), p_in, pool_params, (t_lat, s), weights, s_out, t_lat, n_ctx,
                            MERGE_CHAIN_ROWS, prev=merged)
        x1, h2p, route = merged
        yp = _moe_routed(h2p, route, router_w.T, w1, w3, w2, l)
        tokens = (x1, yp, mod)

    return _final_residual(*tokens, norm_f.reshape(1, d))
```

```python
import functools
import math

import numpy as np
import jax
import jax.numpy as jnp
from jax import lax
from jax.experimental import pallas as pl
from jax.experimental.pallas import tpu as pltpu
from jax.experimental.pallas import tpu_sc as plsc

F32 = jnp.float32
BF16 = jnp.bfloat16

D_MODEL = 1024
HEAD_DIM = 64
N_Q_HEADS = 8
N_KV_HEADS = 2
GRID_W = 64
ROPE_THETA = 10000.0
ROPE_AXIS_DIM = HEAD_DIM // 2
QBLK = 128
WINDOW = 128
BRANCH_W = 512
GROUP_W = 128
POOL_WINDOWS = (2, 4, 8, 16)
N_EXPERTS = 16
EXPERTS_PER_GROUP = 4
EXPERT_FF = 512
EPS = 1e-6
MOD_ROWS = 16
NEG_BIG = -1e30
LOG2E = math.log2(math.e)
LANES = 128
POOL_HALO = 16
PAIRS_PER_GROUP = 6
N_CLASSES = 24
CLASS_ROWS = 32
ROUTE_ROWS = 8
SC_MAX_CHUNK = 128
WATTN_QBLOCKS = 2
MERGE_CHAIN_ROWS = 256

HEAD_PERM = (0, 4, 1, 5, 2, 6, 3, 7)


def _cparams(sem, vmem_mb):
    return pltpu.CompilerParams(dimension_semantics=sem, vmem_limit_bytes=vmem_mb * 1024 * 1024)


def _const_spec(shape):
    nd = len(shape)
    return pl.BlockSpec(shape, lambda *_: (0,) * nd)


def _ada_kernel(c_ref, w_ref, b_ref, o_ref):
    c = c_ref[...]
    s = c * jax.nn.sigmoid(c)
    o_ref[0] = jnp.dot(s.astype(BF16), w_ref[0].astype(BF16), preferred_element_type=F32) + b_ref[0]


def _ada(cc, w_ada, b_ada):
    depth, d, n = w_ada.shape
    tn = 1536
    return pl.pallas_call(
        _ada_kernel,
        grid=(depth, n // tn),
        in_specs=[
            pl.BlockSpec((MOD_ROWS, d), lambda l, j: (0, 0)),
            pl.BlockSpec((1, d, tn), lambda l, j: (l, 0, j)),
            pl.BlockSpec((1, 1, tn), lambda l, j: (l, 0, j)),
        ],
        out_specs=pl.BlockSpec((1, MOD_ROWS, tn), lambda l, j: (l, 0, j)),
        out_shape=jax.ShapeDtypeStruct((depth, MOD_ROWS, n), F32),
        compiler_params=_cparams(("arbitrary", "arbitrary"), 40),
        name="ada",
    )(cc, w_ada, b_ada.reshape(depth, 1, n))


def _norm_mod(x, gain, shift, scale):
    ms = jnp.mean(x * x, axis=-1, keepdims=True)
    return (x * lax.rsqrt(ms + EPS) * gain) * (1.0 + scale) + shift


def _mod_slices(m):
    d = D_MODEL
    return [m[:, i * d:(i + 1) * d] for i in range(6)]


def _head_norm(z, seg, gain):
    ms = jnp.dot((z * z).astype(BF16), seg, preferred_element_type=F32)
    return z * lax.rsqrt(ms + EPS) * gain


def _rope(z, cos, sin):
    outs = []
    for c in range(z.shape[1] // LANES):
        zc = z[:, c * LANES:(c + 1) * LANES]
        outs.append(zc * cos + pltpu.roll(zc, LANES // 2, 1) * sin)
    return outs[0] if len(outs) == 1 else jnp.concatenate(outs, axis=-1)


def _stream_specs(tm, d, n_lat):
    return [pl.BlockSpec((1, tm, d), lambda bi, j: (bi, jnp.minimum(j, n_lat - 1), 0)),
            pl.BlockSpec((1, tm, d), lambda bi, j: (bi, jnp.maximum(j - n_lat, 0), 0))]


def _pending_residual(x, y_ref, modp_ref, rows=slice(None)):
    return x + _mod_slices(modp_ref[0])[5] * _unpack_bf16_pairs(y_ref[0, rows, :])


def _inproj_kernel(*refs, n_lat, pending):
    if pending:
        x_ref, y_ref, modp_ref = refs[:3]
        x = _pending_residual(x_ref[0], y_ref, modp_ref)
    else:
        xl_ref, xc_ref = refs[:2]
        x = jnp.where(pl.program_id(1) >= n_lat, xc_ref[0], xl_ref[0])
    (mod_ref, n1_ref, w_ref, qg_ref, kg_ref, seg_ref, cs_ref, cos_ref, sin_ref,
     zr_ref, zi_ref, p_ref, qb_ref, qw_ref, kvb_ref, kvw_ref) = refs[3 if pending else 2:]
    sh1, sc1 = _mod_slices(mod_ref[0])[:2]
    h = _norm_mod(x, n1_ref[...], sh1, sc1)
    u = jnp.dot(h.astype(BF16), w_ref[...], preferred_element_type=F32)
    cos, sin = cos_ref[...], sin_ref[...]
    w = BRANCH_W
    seg = seg_ref[...]
    qb = _rope(_head_norm(u[:, 0:w], seg, qg_ref[...]), cos, sin)
    qb_ref[0] = (qb * (HEAD_DIM ** -0.5 * LOG2E)).astype(BF16)
    qw = _rope(u[:, w:2 * w], cos, sin)
    qw_ref[0] = (qw * (HEAD_DIM ** -0.5 * LOG2E)).astype(BF16)
    o = 2 * w
    kb = _rope(_head_norm(u[:, o:o + LANES], seg[:LANES, :LANES], kg_ref[...]), cos, sin)
    kw = _rope(u[:, o + LANES:o + 2 * LANES], cos, sin)
    o += 2 * LANES
    f_in = u[:, o:o + w].astype(BF16)
    zr, zi = [], []
    for g in range(w // GROUP_W):
        z = jnp.dot(f_in[:, g * GROUP_W:(g + 1) * GROUP_W], cs_ref[...], preferred_element_type=F32)
        zr.append(z[:, :GROUP_W])
        zi.append(z[:, GROUP_W:])
    zr_ref[0] = jnp.concatenate(zr, axis=-1).astype(BF16)
    zi_ref[0] = jnp.concatenate(zi, axis=-1).astype(BF16)
    o += w
    vb = u[:, o:o + LANES]
    vw = u[:, o + LANES:o + 2 * LANES]
    kvb_ref[0] = jnp.concatenate([kb, vb, jnp.ones_like(vb)], axis=-1).astype(BF16)
    kvw_ref[0] = jnp.concatenate([kw, vw, jnp.ones_like(vw)], axis=-1).astype(BF16)
    p_ref[0] = u[:, o + 2 * LANES:o + 2 * LANES + w].astype(BF16)


def _inproj(tokens, t_lat, mod, n1, w_in, qg, kg, seg, cs, cos, sin, tm=256):
    pending = len(tokens) == 3
    b, _, d = tokens[0].shape
    s = tokens[0].shape[1] if pending else t_lat + tokens[1].shape[1]
    nw = w_in.shape[1]
    n_lat = t_lat // tm
    tok = lambda bi, j: (bi, j, 0)
    tab = lambda bi, j: (j, 0)
    mod_spec = pl.BlockSpec((1, 1, 6 * d), lambda bi, j: (jnp.where(j >= n_lat, b, bi), 0, 0))
    if pending:
        token_specs = [pl.BlockSpec((1, tm, d), tok), pl.BlockSpec((1, tm, d // 2), tok), mod_spec]
    else:
        token_specs = _stream_specs(tm, d, n_lat)
    widths = (BRANCH_W,) * 5 + (3 * LANES, 3 * LANES)
    return pl.pallas_call(
        functools.partial(_inproj_kernel, n_lat=n_lat, pending=pending),
        grid=(b, s // tm),
        in_specs=token_specs + [
            mod_spec,
            _const_spec((1, d)),
            _const_spec((d, nw)),
            _const_spec((1, BRANCH_W)),
            _const_spec((1, LANES)),
            _const_spec((BRANCH_W, BRANCH_W)),
            _const_spec((GROUP_W, 2 * GROUP_W)),
            pl.BlockSpec((tm, LANES), tab),
            pl.BlockSpec((tm, LANES), tab),
        ],
        out_specs=[pl.BlockSpec((1, tm, wd), tok) for wd in widths],
        out_shape=[jax.ShapeDtypeStruct((b, s, wd), BF16) for wd in widths],
        compiler_params=_cparams(("parallel", "arbitrary"), 48),
        name="inproj",
    )(*tokens, mod, n1, w_in, qg, kg, seg, cs, cos, sin)


def _split_heads(qc, lane):
    first = (lane & (HEAD_DIM // 2)) == 0
    zero = jnp.zeros_like(qc)
    return jnp.concatenate([jnp.where(first, qc, zero), jnp.where(first, zero, qc)], axis=0)


def _gattn_kernel(q_ref, kv_ref, o_ref, q2_s, acc_s, *, sub):
    tq = q_ref.shape[1]
    lane = lax.broadcasted_iota(jnp.int32, (1, LANES), 1)
    nt = (((1,), (1,)), ((), ()))
    n_chunks = BRANCH_W // LANES
    for c in range(n_chunks):
        q2_s[2 * c * tq:(2 * c + 2) * tq, :] = _split_heads(q_ref[0, :, c * LANES:(c + 1) * LANES], lane)
    k = kv_ref[0, :, 0:LANES]
    v = kv_ref[0, :, LANES:3 * LANES]
    for r in range(2 * n_chunks * tq // sub):
        rows = slice(r * sub, (r + 1) * sub)
        s = lax.dot_general(q2_s[rows, :], k, nt, preferred_element_type=F32)
        p = jnp.exp2(s - jnp.max(s, axis=-1, keepdims=True))
        acc_s[rows, :] = jnp.dot(p.astype(BF16), v, preferred_element_type=F32)
    for c in range(n_chunks):
        lo = acc_s[2 * c * tq:(2 * c + 1) * tq, :]
        hi = acc_s[(2 * c + 1) * tq:(2 * c + 2) * tq, :]
        o = jnp.where(lane < HEAD_DIM, lo[:, :LANES] / lo[:, LANES:], hi[:, :LANES] / hi[:, LANES:])
        o_ref[0, :, c * LANES:(c + 1) * LANES] = o.astype(BF16)


def _gattn(qb, kv, q_start, q_len, k_start, k_len, tq=256, sub=128):
    b = qb.shape[0]
    assert q_start % tq == 0 and q_len % tq == 0 and k_start % k_len == 0 and k_len % LANES == 0
    rows = 2 * tq * (BRANCH_W // LANES)
    return pl.pallas_call(
        functools.partial(_gattn_kernel, sub=sub),
        grid=(b, q_len // tq),
        in_specs=[
            pl.BlockSpec((1, tq, BRANCH_W), lambda bi, j: (bi, q_start // tq + j, 0)),
            pl.BlockSpec((1, k_len, 3 * LANES), lambda bi, j: (bi, k_start // k_len, 0)),
        ],
        out_specs=pl.BlockSpec((1, tq, BRANCH_W), lambda bi, j: (bi, j, 0)),
        out_shape=jax.ShapeDtypeStruct((b, q_len, BRANCH_W), BF16),
        scratch_shapes=[pltpu.VMEM((rows, LANES), BF16), pltpu.VMEM((rows, 2 * LANES), F32)],
        compiler_params=_cparams(("parallel", "arbitrary"), 48),
        name="gattn",
    )(qb, kv)


def _window_bias(n_ctx):
    tq = WATTN_QBLOCKS * QBLK
    qi = np.arange(tq)[:, None]
    kj = np.arange(tq + 2 * QBLK)[None, :]
    band = np.abs(kj - WINDOW - qi) <= WINDOW
    blk = kj // QBLK
    variants = [band & (blk != 0), band, band & (blk != WATTN_QBLOCKS + 1), np.zeros_like(band)]
    out = [np.concatenate([np.ones((tq, n_ctx), bool), v], axis=1) for v in variants]
    return np.where(np.stack(out), 0.0, NEG_BIG).astype(np.float32)


def _wattn_kernel(sink_ref, bias_ref, q_ref, *refs, sub):
    o_ref, q2_s, kv_s, acc_s = refs[-4:]
    tq = q_ref.shape[1]
    off = 0
    for blk in refs[:-4]:
        kv_s[off:off + blk.shape[1], :] = blk[0]
        off += blk.shape[1]
    lane = lax.broadcasted_iota(jnp.int32, (1, LANES), 1)
    nt = (((1,), (1,)), ((), ()))
    n_chunks = BRANCH_W // LANES
    for c in range(n_chunks):
        q2_s[2 * c * tq:(2 * c + 2) * tq, :] = _split_heads(q_ref[0, :, c * LANES:(c + 1) * LANES], lane)
    k, v = kv_s[:, 0:LANES], kv_s[:, LANES:3 * LANES]
    for r in range(2 * n_chunks * tq // sub):
        rows = slice(r * sub, (r + 1) * sub)
        q_off = (r * sub) % tq
        sk = sink_ref[(r * sub) // tq]
        s = lax.dot_general(q2_s[rows, :], k, nt, preferred_element_type=F32) + bias_ref[0, q_off:q_off + sub, :]
        m = jnp.maximum(jnp.max(s, axis=-1, keepdims=True), sk)
        pv = jnp.dot(jnp.exp2(s - m).astype(BF16), v, preferred_element_type=F32)
        acc_s[rows, :LANES] = pv[:, :LANES]
        acc_s[rows, LANES:] = pv[:, LANES:] + jnp.exp2(sk - m)
    for c in range(n_chunks):
        lo = acc_s[2 * c * tq:(2 * c + 1) * tq, :]
        hi = acc_s[(2 * c + 1) * tq:(2 * c + 2) * tq, :]
        o = jnp.where(lane < HEAD_DIM, lo[:, :LANES] / lo[:, LANES:], hi[:, :LANES] / hi[:, LANES:])
        o_ref[0, :, c * LANES:(c + 1) * LANES] = o.astype(BF16)


def _wattn(sink, bias, qw, kv, s_out, t_lat):
    b, s, _ = qw.shape
    n_ctx = s - t_lat
    nq = WATTN_QBLOCKS
    tq = nq * QBLK
    assert n_ctx % tq == 0 and t_lat // tq >= 2
    last = s // QBLK - 1
    n_lat = t_lat // tq
    variant = lambda j: jnp.where(j >= n_lat, 3, jnp.where(j == 0, 0, jnp.where(j == n_lat - 1, 2, 1)))
    key_block = lambda off: pl.BlockSpec(
        (1, QBLK, 3 * LANES), lambda bi, j: (bi, jnp.clip(j * nq + off, 0, last), 0))
    rows = 2 * tq * (BRANCH_W // LANES)
    return pl.pallas_call(
        functools.partial(_wattn_kernel, sub=128),
        scratch_shapes=[pltpu.VMEM((rows, LANES), BF16), pltpu.VMEM((bias.shape[2], 3 * LANES), BF16),
                        pltpu.VMEM((rows, 2 * LANES), F32)],
        grid=(b, s_out // tq),
        in_specs=[
            pl.BlockSpec(memory_space=pltpu.SMEM),
            pl.BlockSpec((1,) + bias.shape[1:], lambda bi, j: (variant(j), 0, 0)),
            pl.BlockSpec((1, tq, BRANCH_W), lambda bi, j: (bi, j, 0)),
            pl.BlockSpec((1, n_ctx, 3 * LANES), lambda bi, j: (bi, t_lat // n_ctx, 0)),
        ] + [key_block(off) for off in range(-1, nq + 1)],
        out_specs=pl.BlockSpec((1, tq, BRANCH_W), lambda bi, j: (bi, j, 0)),
        out_shape=jax.ShapeDtypeStruct((b, s_out, BRANCH_W), BF16),
        compiler_params=_cparams(("parallel", "arbitrary"), 32),
        name="wattn",
    )(sink, bias, qw, *([kv] * (nq + 3)))


def _fourier_tables(n1, n2):
    t = n1 * n2
    k2 = np.arange(n2)[None, :, None]
    t2 = np.arange(n2)[None, None, :]
    t1 = np.arange(n1)[:, None, None]
    theta = 2.0 * np.pi * ((k2 * t2 * n1 + k2 * t1) % t) / t
    er, ei = np.cos(theta) / math.sqrt(n2), -np.sin(theta) / math.sqrt(n2)
    e = np.concatenate([np.concatenate([er, -ei], axis=2), np.concatenate([ei, er], axis=2)], axis=1)
    k1 = np.arange(n1)[:, None]
    phi = 2.0 * np.pi * ((k1 * np.arange(n1)[None, :]) % n1) / n1
    dcat = np.concatenate([np.cos(phi), np.sin(phi)], axis=1) / math.sqrt(n1)
    return e.astype(np.float32), dcat.astype(np.float32)


def _channel_dft_table():
    c = np.arange(GROUP_W)
    ang = 2.0 * np.pi * ((c[:, None] * c[None, :]) % GROUP_W) / GROUP_W
    return (np.concatenate([np.cos(ang), -np.sin(ang)], axis=1) / math.sqrt(GROUP_W)).astype(np.float32)


def _fourier_kernel(zr_ref, zi_ref, e_ref, d_ref, o_ref, xr_s, xi_s, yr_s, yi_s, *, n1, n2):
    nc = xr_s.shape[0]
    chunk = lambda c: slice(c * LANES, (c + 1) * LANES)

    def gather(ref, start, size, stride):
        return jnp.concatenate([ref[c, pl.ds(start, size, stride=stride), :] for c in range(nc)], axis=-1)

    for c in range(nc):
        xr_s[c] = zr_ref[0, :, chunk(c)].astype(F32)
        xi_s[c] = zi_ref[0, :, chunk(c)].astype(F32)
    for t1 in range(n1):
        xs = jnp.concatenate([gather(xr_s, t1, n2, n1), gather(xi_s, t1, n2, n1)], axis=0).astype(BF16)
        y = jnp.dot(e_ref[t1], xs, preferred_element_type=F32)
        for c in range(nc):
            yr_s[c, t1 * n2:(t1 + 1) * n2, :] = y[:n2, chunk(c)]
            yi_s[c, t1 * n2:(t1 + 1) * n2, :] = y[n2:, chunk(c)]
    for k2 in range(n2):
        ys = jnp.concatenate([gather(yr_s, k2, n1, n2), gather(yi_s, k2, n1, n2)], axis=0).astype(BF16)
        o = jnp.dot(d_ref[...], ys, preferred_element_type=F32)
        for c in range(nc):
            xr_s[c, pl.ds(k2, n1, stride=n2), :] = o[:, chunk(c)]
    for c in range(nc):
        o_ref[0, :, chunk(c)] = xr_s[c].astype(BF16)


def _fourier(zr, zi, e_tab, d_tab, t_len, row_block, cw=256):
    b = zr.shape[0]
    n1 = d_tab.shape[0]
    n2 = t_len // n1
    zspec = pl.BlockSpec((1, t_len, cw), lambda bi, j: (bi, row_block, j))
    return pl.pallas_call(
        functools.partial(_fourier_kernel, n1=n1, n2=n2),
        grid=(b, BRANCH_W // cw),
        in_specs=[zspec, zspec, _const_spec(e_tab.shape), _const_spec(d_tab.shape)],
        out_specs=pl.BlockSpec((1, t_len, cw), lambda bi, j: (bi, 0, j)),
        out_shape=jax.ShapeDtypeStruct((b, t_len, BRANCH_W), BF16),
        scratch_shapes=[pltpu.VMEM((cw // LANES, t_len, LANES), F32)] * 4,
        compiler_params=_cparams(("parallel", "arbitrary"), 48),
        name="fourier",
    )(zr, zi, e_tab, d_tab)


def _pool_rows(ext, pos, n, w_ref, scale):
    n_ext = ext.shape[0]
    rows = n_ext - 2 * POOL_HALO
    back = lambda v, k: pltpu.roll(v, k, 0)
    fwd = lambda v, k: pltpu.roll(v, n_ext - k, 0)
    outs = []
    for gi, w in enumerate(POOL_WINDOWS):
        e = ext[:, gi * GROUP_W:(gi + 1) * GROUP_W]
        wsum = e + back(e, 1)
        half = 1
        while 2 * half < w:
            wsum = back(wsum, half) + fwd(wsum, half)
            half *= 2
        own = slice(POOL_HALO, POOL_HALO + rows)
        cnt = jnp.minimum(pos + w // 2, n) - jnp.maximum(pos - w // 2, 0)
        pooled = wsum[own] / cnt.astype(F32) - e[own]
        outs.append(jnp.dot(pooled.astype(BF16), w_ref[gi], preferred_element_type=F32))
    return (jnp.concatenate(outs, axis=-1) * scale).astype(BF16)


def _route(logits_t, bias):
    aff = jax.nn.sigmoid(logits_t)
    sel = aff + bias
    neg = -jnp.inf
    firsts, seconds, scores = [], [], []
    for g in range(N_EXPERTS // EXPERTS_PER_GROUP):
        s = [sel[EXPERTS_PER_GROUP * g + k:EXPERTS_PER_GROUP * g + k + 1, :] for k in range(EXPERTS_PER_GROUP)]
        m1 = jnp.maximum(jnp.maximum(s[0], s[1]), jnp.maximum(s[2], s[3]))
        i1 = jnp.where(s[0] == m1, 0, jnp.where(s[1] == m1, 1, jnp.where(s[2] == m1, 2, 3)))
        r = [jnp.where(i1 == k, neg, s[k]) for k in range(EXPERTS_PER_GROUP)]
        m2 = jnp.maximum(jnp.maximum(r[0], r[1]), jnp.maximum(r[2], r[3]))
        i2 = jnp.where(r[0] == m2, 0, jnp.where(r[1] == m2, 1, jnp.where(r[2] == m2, 2, 3)))
        firsts.append(i1 + EXPERTS_PER_GROUP * g)
        seconds.append(i2 + EXPERTS_PER_GROUP * g)
        scores.append(m1 + m2)
    best = jnp.maximum(jnp.maximum(scores[0], scores[1]), jnp.maximum(scores[2], scores[3]))
    pick = lambda v: jnp.where(scores[0] == best, v[0], jnp.where(scores[1] == best, v[1],
                                                                 jnp.where(scores[2] == best, v[2], v[3])))
    e1, e2 = pick(firsts), pick(seconds)
    lo = jnp.minimum(e1, e2) & (EXPERTS_PER_GROUP - 1)
    hi = jnp.maximum(e1, e2) & (EXPERTS_PER_GROUP - 1)
    pair = jnp.where(lo == 0, 0, jnp.where(lo == 1, 3, 5)) + hi - lo - 1
    cls = ((e1 >> 2) * PAIRS_PER_GROUP + pair).astype(F32)
    return jnp.concatenate([cls] + [jnp.zeros_like(cls)] * (ROUTE_ROWS - 1), axis=0)


def _pack_bf16_pairs(v):
    w = v.shape[1] // 2
    bits = pltpu.bitcast(v.astype(BF16).astype(F32), jnp.uint32)
    return pltpu.bitcast(bits[:, :w] | (bits[:, w:] >> 16), jnp.int32)


def _unpack_bf16_pairs(p):
    bits = pltpu.bitcast(p, jnp.uint32)
    hi = pltpu.bitcast(bits & jnp.uint32(0xFFFF0000), F32)
    lo = pltpu.bitcast(bits << 16, F32)
    return jnp.concatenate([hi, lo], axis=-1)


def _merge_kernel(*refs, n_real, n_fill, pending, **static):
    n_tok = 3 if pending else 1
    (mod_ref, n1_ref, n2_ref, a_ref, b_ref, c_ref, pc_ref, pp_ref, pn_ref, pw_ref, psc_ref,
     wg_ref, bg_ref, wb_ref, wo_ref, rw_ref, rb_ref) = refs[n_tok:n_tok + 17]
    outs = refs[-4:-1]
    ins = (refs[:n_tok], mod_ref, n1_ref, n2_ref, a_ref, b_ref, c_ref,
           (pc_ref, pp_ref, pn_ref, pw_ref, psc_ref),
           wg_ref, bg_ref, wb_ref, wo_ref, rw_ref, rb_ref)
    if n_fill:
        @pl.when(pl.program_id(1) >= n_real)
        def _():
            for ref in outs:
                ref[...] = jnp.zeros_like(ref)

        pl.when(pl.program_id(1) < n_real)(lambda: _merge_tile(ins, outs, refs[-1], **static))
    else:
        _merge_tile(ins, outs, refs[-1], **static)


def _merge_tile(ins, outs, merged_s, *, nw, sub, p_off, seq_lo, seq_hi):
    (tok_refs, mod_ref, n1_ref, n2_ref, a_ref, b_ref, c_ref, pool_refs,
     wg_ref, bg_ref, wb_ref, wo_ref, rw_ref, rb_ref) = ins
    xo_ref, h2_ref, route_ref = outs
    pc_ref, pp_ref, pn_ref, pw_ref, psc_ref = pool_refs
    x_ref = tok_refs[0]
    tm = x_ref.shape[1]
    sh1, sc1, g1, sh2, sc2, _ = _mod_slices(mod_ref[0])
    row0 = p_off + pl.program_id(1) * tm
    ext = jnp.concatenate([pp_ref[0], pc_ref[0], pn_ref[0]], axis=0).astype(F32)
    gpos = row0 - POOL_HALO + lax.broadcasted_iota(jnp.int32, (tm + 2 * POOL_HALO, 1), 0)
    ext = jnp.where((gpos >= seq_lo) & (gpos < seq_hi), ext, 0.0)
    for r in range(tm // sub):
        rows = slice(r * sub, (r + 1) * sub)
        x = x_ref[0, rows, :]
        if len(tok_refs) == 3:
            x = _pending_residual(x, tok_refs[1], tok_refs[2], rows)
        hb = _norm_mod(x, n1_ref[...], sh1, sc1).astype(BF16)
        pos = row0 - seq_lo + r * sub + lax.broadcasted_iota(jnp.int32, (sub, 1), 0)
        pooled = _pool_rows(ext[r * sub:(r + 1) * sub + 2 * POOL_HALO], pos, seq_hi - seq_lo,
                            pw_ref, psc_ref[...])
        branches = (a_ref[0, rows, :], b_ref[0, rows, :], c_ref[0, rows, :], pooled)
        for n in range(D_MODEL // nw):
            cols = slice(n * nw, (n + 1) * nw)
            merged = None
            for i, br in enumerate(branches):
                gate = jax.nn.sigmoid(
                    jnp.dot(hb, wg_ref[i, :, cols], preferred_element_type=F32) + bg_ref[i, :, cols])
                term = gate * jnp.dot(br, wb_ref[i, :, cols], preferred_element_type=F32)
                merged = term if merged is None else merged + term
            merged_s[rows, cols] = merged.astype(BF16)
        y = jnp.dot(merged_s[rows, :], wo_ref[...], preferred_element_type=F32)
        xn = x + g1 * y
        xo_ref[0, rows, :] = xn
        h2 = _norm_mod(xn, n2_ref[...], sh2, sc2)
        h2_ref[0, rows, :] = _pack_bf16_pairs(h2)
        h_hi = h2.astype(BF16)
        h_lo = (h2 - h_hi.astype(F32)).astype(BF16)
        nt = (((1,), (1,)), ((), ()))
        by_hi = lax.dot_general(rw_ref[...], h_hi, nt, preferred_element_type=F32)
        by_lo = lax.dot_general(rw_ref[:N_EXPERTS, :], h_lo, nt, preferred_element_type=F32)
        logits_t = by_hi[:N_EXPERTS] + by_hi[N_EXPERTS:] + by_lo
        route_ref[0, :, rows] = _route(logits_t, rb_ref[...])


def _merge(tokens, mod, mod_row, n1, n2, branches, offsets, p_in, pool_params, seq, weights, s_out, out_off,
           rows, tm, prev=None):
    pending = len(tokens) == 3
    b, _, d = tokens[0].shape
    wg, bg, wb, wo, rw_t, rb = weights
    pool_w, pool_scale = pool_params
    seq_lo, seq_hi = seq
    n_real = rows // tm
    n_fill = -(-(s_out - out_off - rows) // tm) if prev is None else 0
    step = lambda j: jnp.minimum(j, n_real - 1)
    blk = lambda width, off: pl.BlockSpec((1, tm, width), lambda bi, j: (bi, off // tm + step(j), 0))
    out_blk = lambda width: pl.BlockSpec((1, tm, width), lambda bi, j: (bi, out_off // tm + j, 0))
    assert all(o % tm == 0 for o in offsets) and out_off % tm == 0 and rows % tm == 0 and seq_lo % tm == 0
    hb = tm // POOL_HALO
    last_halo = p_in.shape[1] // POOL_HALO - 1
    halo = lambda shift: pl.BlockSpec(
        (1, POOL_HALO, BRANCH_W),
        lambda bi, j: (bi, jnp.clip((seq_lo // tm + step(j) + shift) * hb - 1 + shift, 0, last_halo), 0))
    out_shape = [
        jax.ShapeDtypeStruct((b, s_out, d), F32),
        jax.ShapeDtypeStruct((b, s_out, d // 2), jnp.int32),
        jax.ShapeDtypeStruct((b, ROUTE_ROWS, s_out), F32),
    ]
    mod_spec = pl.BlockSpec((1, 1, 6 * d), lambda bi, j: (bi if mod_row is None else mod_row, 0, 0))
    tok_off = out_off if pending else 0
    token_specs = [blk(d, tok_off)] + ([blk(d // 2, tok_off), mod_spec] if pending else [])
    n_in = len(tokens) + 17
    extra_specs = [] if prev is None else [pl.BlockSpec(memory_space=pl.ANY)] * 3
    return pl.pallas_call(
        functools.partial(_merge_kernel, nw=512, sub=MERGE_CHAIN_ROWS, n_real=n_real, n_fill=n_fill,
                          pending=pending, p_off=seq_lo, seq_lo=seq_lo, seq_hi=seq_hi),
        scratch_shapes=[pltpu.VMEM((tm, d), BF16)],
        grid=(b, n_real + n_fill),
        in_specs=token_specs + [
            mod_spec,
            _const_spec((1, d)), _const_spec((1, d)),
            *[blk(BRANCH_W, off) for off in offsets],
            blk(BRANCH_W, seq_lo), halo(0), halo(1),
            _const_spec(pool_w.shape), _const_spec((1, BRANCH_W)),
            _const_spec(wg.shape), _const_spec(bg.shape), _const_spec(wb.shape), _const_spec(wo.shape),
            _const_spec(rw_t.shape), _const_spec(rb.shape),
        ] + extra_specs,
        out_specs=[
            out_blk(d),
            out_blk(d // 2),
            pl.BlockSpec((1, ROUTE_ROWS, tm), lambda bi, j: (bi, 0, out_off // tm + j)),
        ],
        out_shape=out_shape,
        input_output_aliases={} if prev is None else {n_in + i: i for i in range(3)},
        compiler_params=_cparams(("parallel", "arbitrary"), 56),
        name="merge",
    )(*tokens, mod, n1, n2, *branches, p_in, p_in, p_in, pool_w, pool_scale,
      wg, bg, wb, wo, rw_t, rb, *(() if prev is None else prev))


def _rank_kernel(cls_ref, rank_ref, cnt_ref, cnt_s, *, tr):
    @pl.when(pl.program_id(0) == 0)
    def _():
        cnt_s[...] = jnp.zeros_like(cnt_s)

    cls = cls_ref[0]
    cid = lax.broadcasted_iota(jnp.int32, (CLASS_ROWS, tr), 0).astype(F32)
    onehot = cid == cls
    before = lax.broadcasted_iota(jnp.int32, (tr, tr), 0) < lax.broadcasted_iota(jnp.int32, (tr, tr), 1)
    prefix = jnp.dot(jnp.where(onehot, 1.0, 0.0).astype(BF16), jnp.where(before, 1.0, 0.0).astype(BF16),
                     preferred_element_type=F32)
    carry = cnt_s[...][:, 0:1]
    rank_ref[0] = jnp.sum(jnp.where(onehot, prefix + carry, 0.0), axis=0, keepdims=True)
    cnt_s[...] += jnp.sum(jnp.where(onehot, 1.0, 0.0), axis=1, keepdims=True)
    cnt_ref[...] = cnt_s[...]


def _rank(cls_flat, tr=512):
    n = cls_flat.shape[0]
    tr = math.gcd(n, tr)
    cls3 = cls_flat.reshape(n // tr, 1, tr)
    rank, cnt = pl.pallas_call(
        functools.partial(_rank_kernel, tr=tr),
        grid=(n // tr,),
        in_specs=[pl.BlockSpec((1, 1, tr), lambda i: (i, 0, 0))],
        out_specs=[pl.BlockSpec((1, 1, tr), lambda i: (i, 0, 0)), _const_spec((CLASS_ROWS, LANES))],
        out_shape=[jax.ShapeDtypeStruct((n // tr, 1, tr), F32), jax.ShapeDtypeStruct((CLASS_ROWS, LANES), F32)],
        scratch_shapes=[pltpu.VMEM((CLASS_ROWS, LANES), F32)],
        compiler_params=_cparams(("arbitrary",), 32),
        name="rank",
    )(cls3)
    return rank.reshape(n), cnt[:N_CLASSES, 0]


def _sc_layout(n):
    info = plsc.get_sparse_core_info()
    nw = info.num_cores * info.num_subcores
    per_worker = n // nw
    assert per_worker * nw == n
    chunk = max(c for c in range(8, SC_MAX_CHUNK + 1, 8) if per_worker % c == 0)
    return info.num_cores, nw, per_worker // chunk, chunk


def _sc_scatter_rows(src, pos, n_out):
    n, w = src.shape
    nc, nw, k, c = _sc_layout(n)
    mesh = plsc.VectorSubcoreMesh(core_axis_name="c", subcore_axis_name="s")

    @functools.partial(
        pl.kernel, mesh=mesh,
        out_type=jax.ShapeDtypeStruct((n_out, w), src.dtype),
        scratch_types=[pltpu.VMEM((k, c), jnp.int32), pltpu.VMEM((c, w), src.dtype), pltpu.SemaphoreType.DMA],
        name="moe_scatter",
    )
    def scatter(src_hbm, pos_hbm, out_hbm, idx_v, rows_v, sem):
        wid = lax.axis_index("s") * nc + lax.axis_index("c")
        pltpu.sync_copy(pos_hbm.at[wid], idx_v)

        @pl.loop(0, k)
        def _(j):
            off = pl.multiple_of(wid * (k * c) + j * c, 8)
            pltpu.sync_copy(src_hbm.at[pl.ds(off, c)], rows_v)
            pltpu.async_copy(rows_v, out_hbm.at[idx_v.at[j]], sem).wait()

    return scatter(src, pos.reshape(nw, k, c))


def _sc_gather_rows(src, pos):
    n = pos.shape[0]
    w = src.shape[1]
    nc, nw, k, c = _sc_layout(n)
    mesh = plsc.VectorSubcoreMesh(core_axis_name="c", subcore_axis_name="s")

    @functools.partial(
        pl.kernel, mesh=mesh,
        out_type=jax.ShapeDtypeStruct((n, w), src.dtype),
        scratch_types=[pltpu.VMEM((k, c), jnp.int32), pltpu.VMEM((c, w), src.dtype), pltpu.SemaphoreType.DMA],
        name="moe_gather",
    )
    def gather(src_hbm, pos_hbm, out_hbm, idx_v, rows_v, sem):
        wid = lax.axis_index("s") * nc + lax.axis_index("c")
        pltpu.sync_copy(pos_hbm.at[wid], idx_v)

        @pl.loop(0, k)
        def _(j):
            off = pl.multiple_of(wid * (k * c) + j * c, 8)
            pltpu.async_copy(src_hbm.at[idx_v.at[j]], rows_v, sem).wait()
            pltpu.sync_copy(rows_v, out_hbm.at[pl.ds(off, c)])

    return gather(src, pos.reshape(nw, k, c))


def _gmm_kernel(lo_ref, hi_ref, new_ref, nact_ref, h_ref, rw_ref,
                w1a_ref, w1b_ref, w3a_ref, w3b_ref, w2a_ref, w2b_ref, o_ref, w13_s, w2_s):
    t = pl.program_id(0)

    @pl.when(new_ref[t] == 1)
    def _():
        for i, (w1_ref, w3_ref, w2_ref) in enumerate(((w1a_ref, w3a_ref, w2a_ref), (w1b_ref, w3b_ref, w2b_ref))):
            w13_s[i, :, :EXPERT_FF] = w1_ref[0, 0].astype(BF16)
            w13_s[i, :, EXPERT_FF:] = w3_ref[0, 0].astype(BF16)
            w2_s[i] = w2_ref[0, 0].astype(BF16)

    @pl.when(t < nact_ref[0])
    def _():
        xf = _unpack_bf16_pairs(h_ref[...])
        x = xf.astype(BF16)

        def expert(i):
            ab = jnp.dot(x, w13_s[i], preferred_element_type=F32)
            a, gate = ab[:, :EXPERT_FF], ab[:, EXPERT_FF:]
            hid = (a * jax.nn.sigmoid(a)) * gate
            return jnp.dot(hid.astype(BF16), w2_s[i], preferred_element_type=F32)

        aff = [jax.nn.sigmoid(jnp.sum(xf * rw_ref[pl.ds(e_ref[t], 1), :], axis=-1, keepdims=True))
               for e_ref in (lo_ref, hi_ref)]
        total = aff[0] + aff[1]
        y = (aff[0] / total) * expert(0) + (aff[1] / total) * expert(1)
        o_ref[...] = _pack_bf16_pairs(y)


def _gmm(tile_lo, tile_hi, tile_new, n_act, hs, rw_t, w1, w3, w2, layer, tm):
    n_pad, half = hs.shape
    d = 2 * half
    row = lambda t, lo, hi, new, na: (jnp.minimum(t, na[0] - 1), 0)
    e_lo = lambda t, lo, hi, new, na: (layer, lo[jnp.minimum(t, na[0] - 1)], 0, 0)
    e_hi = lambda t, lo, hi, new, na: (layer, hi[jnp.minimum(t, na[0] - 1)], 0, 0)
    up = lambda e: pl.BlockSpec((1, 1, d, EXPERT_FF), e)
    down = lambda e: pl.BlockSpec((1, 1, EXPERT_FF, d), e)
    return pl.pallas_call(
        _gmm_kernel,
        grid_spec=pltpu.PrefetchScalarGridSpec(
            num_scalar_prefetch=4,
            grid=(n_pad // tm,),
            in_specs=[
                pl.BlockSpec((tm, half), row),
                pl.BlockSpec(rw_t.shape, lambda t, lo, hi, new, na: (0, 0)),
                up(e_lo), up(e_hi), up(e_lo), up(e_hi), down(e_lo), down(e_hi),
            ],
            out_specs=pl.BlockSpec((tm, half), row),
            scratch_shapes=[pltpu.VMEM((2, d, 2 * EXPERT_FF), BF16), pltpu.VMEM((2, EXPERT_FF, d), BF16)],
        ),
        out_shape=jax.ShapeDtypeStruct((n_pad, half), jnp.int32),
        compiler_params=_cparams(("arbitrary",), 56),
        name="moe_gmm",
    )(tile_lo, tile_hi, tile_new, n_act, hs, rw_t, w1, w1, w3, w3, w2, w2)


def _moe_routed(h2p, route, rw_t, w1, w3, w2, layer, tm=256):
    b, s, half = h2p.shape
    n = b * s
    n_pad = n + N_CLASSES * tm
    cls = route[:, 0, :].reshape(n)
    rank, counts = _rank(cls)
    counts = counts.astype(jnp.int32)
    padded = (counts + tm - 1) // tm * tm
    ends = jnp.cumsum(padded)
    pos = jnp.take(ends - padded, cls.astype(jnp.int32)) + rank.astype(jnp.int32)
    n_act = (ends[-1] // tm).reshape(1)
    tile_row = jnp.arange(n_pad // tm, dtype=jnp.int32) * tm
    tile_cls = jnp.minimum(jnp.sum(tile_row[:, None] >= ends[None, :], axis=1), N_CLASSES - 1)
    pair_lo, pair_hi = (jnp.asarray(a, jnp.int32) for a in _class_experts())
    hs = _sc_scatter_rows(h2p.reshape(n, half), pos, n_pad)
    prev_cls = jnp.concatenate([jnp.full((1,), -1, tile_cls.dtype), tile_cls[:-1]])
    tile_new = ((tile_cls != prev_cls) & (tile_row < ends[-1])).astype(jnp.int32)
    ys = _gmm(jnp.take(pair_lo, tile_cls), jnp.take(pair_hi, tile_cls), tile_new, n_act, hs, rw_t,
              w1, w3, w2, layer, tm)
    return _sc_gather_rows(ys, pos).reshape(b, s, half)


def _class_experts():
    lo, hi = [], []
    for g in range(N_EXPERTS // EXPERTS_PER_GROUP):
        for i in range(EXPERTS_PER_GROUP):
            for j in range(i + 1, EXPERTS_PER_GROUP):
                lo.append(EXPERTS_PER_GROUP * g + i)
                hi.append(EXPERTS_PER_GROUP * g + j)
    return np.array(lo), np.array(hi)


def _final_residual_kernel(x_ref, y_ref, mod_ref, g_ref, o_ref):
    x = _pending_residual(x_ref[0], y_ref, mod_ref)
    o_ref[0] = x * lax.rsqrt(jnp.mean(x * x, axis=-1, keepdims=True) + EPS) * g_ref[...]


def _final_residual(x1, yp, mod, gain, tm=512):
    b, t, d = x1.shape
    tok = lambda bi, j: (bi, j, 0)
    return pl.pallas_call(
        _final_residual_kernel,
        grid=(b, t // tm),
        in_specs=[
            pl.BlockSpec((1, tm, d), tok),
            pl.BlockSpec((1, tm, d // 2), tok),
            pl.BlockSpec((1, 1, 6 * d), lambda bi, j: (bi, 0, 0)),
            _const_spec((1, d)),
        ],
        out_specs=pl.BlockSpec((1, tm, d), tok),
        out_shape=jax.ShapeDtypeStruct((b, t, d), F32),
        compiler_params=_cparams(("parallel", "arbitrary"), 32),
        name="final_residual",
    )(x1, yp, mod, gain)


ROPE_FREQS = ROPE_AXIS_DIM // 2


def _rope_tables(t_lat, n_ctx):
    rows = t_lat // GRID_W
    row = jnp.repeat(jnp.arange(rows, dtype=F32), GRID_W)
    col = jnp.tile(jnp.arange(GRID_W, dtype=F32), rows)
    inv_freq = ROPE_THETA ** (-jnp.arange(0, ROPE_AXIS_DIM, 2, dtype=F32) / ROPE_AXIS_DIM)
    ang = jnp.stack([row[:, None] * inv_freq, col[:, None] * inv_freq], axis=1)
    lanes = (2, LANES // HEAD_DIM, 2, ROPE_FREQS)
    sign = jnp.array([-1.0, 1.0], F32).reshape(1, 2, 1, 1, 1)
    cos = jnp.broadcast_to(jnp.cos(ang)[:, None, None], (t_lat,) + lanes).reshape(t_lat, LANES)
    sin = jnp.broadcast_to(sign * jnp.sin(ang)[:, None, None], (t_lat,) + lanes).reshape(t_lat, LANES)
    pad = lambda tbl, fill: jnp.concatenate([tbl, jnp.full((n_ctx, LANES), fill, F32)], axis=0)
    return pad(cos, 1.0), pad(sin, 0.0)


def _qk_lanes(w, n_heads):
    lead = w.shape[:-1]
    n_chunks = n_heads // N_KV_HEADS
    w = w.reshape(lead + (N_KV_HEADS, n_chunks, 2, 2, ROPE_FREQS))
    nl = len(lead)
    w = jnp.transpose(w, tuple(range(nl)) + (nl + 1, nl + 3, nl, nl + 2, nl + 4))
    return w.reshape(lead + (n_heads * HEAD_DIM,))


def _permute_heads(w, axis):
    shp = w.shape
    w = w.reshape(shp[:axis] + (N_Q_HEADS, HEAD_DIM) + shp[axis + 1:])
    w = jnp.take(w, jnp.array(HEAD_PERM), axis=axis)
    return w.reshape(shp)


def _square_factor(n):
    r = int(round(math.sqrt(n)))
    assert r * r == n, "sequence lengths must be perfect squares for the two-stage DFT"
    return r


def kernel(x, c, ctx, c_ctx, w_ada, b_ada, norm1, norm2, w_in, q_gain, k_gain, sink, pool_w, pool_scale,
           w_branch, w_gate, b_gate, w_out, router_w, router_bias, w1, w3, w2, norm_f):
    b, t_lat, d = x.shape
    n_ctx = ctx.shape[1]
    s = t_lat + n_ctx
    depth = w_ada.shape[0]
    assert d == D_MODEL and b < MOD_ROWS and t_lat % 256 == 0 and n_ctx % 256 == 0 and t_lat % n_ctx == 0

    tokens = (x, ctx)
    cc = jnp.zeros((MOD_ROWS, d), F32).at[:b].set(c).at[b].set(c_ctx)
    mod_all = _ada(cc, w_ada, b_ada).reshape(depth, MOD_ROWS, 1, 6 * d)

    cos, sin = _rope_tables(t_lat, n_ctx)
    lane_head = (np.arange(BRANCH_W) // LANES) * 2 + (np.arange(BRANCH_W) // (HEAD_DIM // 2)) % 2
    seg = jnp.asarray((lane_head[:, None] == lane_head[None, :]) / HEAD_DIM, BF16)
    cs = jnp.asarray(_channel_dft_table()).astype(BF16)
    f_lat = [jnp.asarray(a).astype(BF16) for a in _fourier_tables(*(_square_factor(t_lat),) * 2)]
    f_ctx = [jnp.asarray(a).astype(BF16) for a in _fourier_tables(*(_square_factor(n_ctx),) * 2)]
    wbias = jnp.asarray(_window_bias(n_ctx))
    rw_hi = router_w.T.astype(BF16)
    rw_t = jnp.concatenate([rw_hi, (router_w.T - rw_hi.astype(F32)).astype(BF16)], axis=0)
    rb = router_bias.reshape(N_EXPERTS, 1)

    for l in range(depth):
        need_ctx = l < depth - 1
        s_out = s if need_ctx else t_lat
        cols = jnp.split(w_in[l], np.cumsum((512, 512, 512, 128, 128, 512, 128))[:], axis=1)
        f_w, p_w, qb_w, kb_w, vb_w, qw_w, kw_w, vw_w = cols
        w_in_l = jnp.concatenate([_qk_lanes(qb_w, N_Q_HEADS), _qk_lanes(qw_w, N_Q_HEADS),
                                  _qk_lanes(kb_w, N_KV_HEADS), _qk_lanes(kw_w, N_KV_HEADS),
                                  f_w, vb_w, vw_w, p_w], axis=1).astype(BF16)
        wb_l = jnp.stack([w_branch[l, 0], _permute_heads(w_branch[l, 1], 0),
                          _permute_heads(w_branch[l, 2], 0), w_branch[l, 3]]).astype(BF16)
        mod = mod_all[l]
        n1 = norm1[l].reshape(1, d)
        n2 = norm2[l].reshape(1, d)
        qg = _qk_lanes(jnp.tile(q_gain[l], N_Q_HEADS), N_Q_HEADS).reshape(1, BRANCH_W)
        kg = _qk_lanes(jnp.tile(k_gain[l], N_KV_HEADS), N_KV_HEADS).reshape(1, LANES)

        zr, zi, p_in, qb, qw, kvb, kvw = _inproj(tokens, t_lat, mod, n1, w_in_l, qg, kg, seg, cs, cos, sin)

        out_a = _fourier(zr, zi, f_lat[0], f_lat[1], t_lat, 0)
        out_b = _gattn(qb, kvb, 0, t_lat, 0, s, tq=512)
        out_c = _wattn(jnp.take(sink[l], jnp.array(HEAD_PERM)) * LOG2E, wbias, qw, kvw, s_out, t_lat)

        weights = (w_gate[l].astype(BF16), b_gate[l].reshape(4, 1, d), wb_l, w_out[l].astype(BF16), rw_t, rb)
        pool_params = (pool_w[l].astype(BF16), pool_scale[l].reshape(1, BRANCH_W))
        pending = len(tokens) == 3
        merged = _merge(tokens if pending else tokens[:1], mod, None, n1, n2, (out_a, out_b, out_c), (0, 0, 0),
                        p_in, pool_params, (0, t_lat), weights, s_out, 0, t_lat, 2 * MERGE_CHAIN_ROWS)
        if need_ctx:
            out_ac = _fourier(zr, zi, f_ctx[0], f_ctx[1], n_ctx, t_lat // n_ctx)
            out_bc = _gattn(qb, kvb, t_lat, n_ctx, t_lat, n_ctx)
            merged = _merge(tokens if pending else tokens[1:], mod, b, n1, n2, (out_ac, out_bc, out_c),
                            (0, 0, t_lat), p_in, pool_params, (t_lat, s), weights, s_out, t_lat, n_ctx,
                            MERGE_CHAIN_ROWS, prev=merged)
        x1, h2p, route = merged
        yp = _moe_routed(h2p, route, router_w.T, w1, w3, w2, l)
        tokens = (x1, yp, mod)

    return _final_residual(*tokens, norm_f.reshape(1, d))
```

```python
import functools
import math

import numpy as np
import jax
import jax.numpy as jnp
from jax import lax
from jax.experimental import pallas as pl
from jax.experimental.pallas import tpu as pltpu
from jax.experimental.pallas import tpu_sc as plsc

F32 = jnp.float32
BF16 = jnp.bfloat16

D_MODEL = 1024
HEAD_DIM = 64
N_Q_HEADS = 8
N_KV_HEADS = 2
GRID_W = 64
ROPE_THETA = 10000.0
ROPE_AXIS_DIM = HEAD_DIM // 2
QBLK = 128
WINDOW = 128
BRANCH_W = 512
GROUP_W = 128
POOL_WINDOWS = (2, 4, 8, 16)
N_EXPERTS = 16
EXPERTS_PER_GROUP = 4
EXPERT_FF = 512
EPS = 1e-6
MOD_ROWS = 16
NEG_BIG = -1e30
LOG2E = math.log2(math.e)
LANES = 128
POOL_HALO = 16
PAIRS_PER_GROUP = 6
N_CLASSES = 24
CLASS_ROWS = 32
ROUTE_ROWS = 8
SC_MAX_CHUNK = 128
WATTN_QBLOCKS = 2
MERGE_CHAIN_ROWS = 256
FOURIER_SLAB_PAD = 4

HEAD_PERM = (0, 4, 1, 5, 2, 6, 3, 7)


def _cparams(sem, vmem_mb):
    return pltpu.CompilerParams(dimension_semantics=sem, vmem_limit_bytes=vmem_mb * 1024 * 1024)


def _const_spec(shape):
    nd = len(shape)
    return pl.BlockSpec(shape, lambda *_: (0,) * nd)


def _ada_kernel(c_ref, w_ref, b_ref, o_ref):
    c = c_ref[...]
    s = c * jax.nn.sigmoid(c)
    o_ref[0] = jnp.dot(s.astype(BF16), w_ref[0].astype(BF16), preferred_element_type=F32) + b_ref[0]


def _ada(cc, w_ada, b_ada):
    depth, d, n = w_ada.shape
    tn = 1536
    return pl.pallas_call(
        _ada_kernel,
        grid=(depth, n // tn),
        in_specs=[
            pl.BlockSpec((MOD_ROWS, d), lambda l, j: (0, 0)),
            pl.BlockSpec((1, d, tn), lambda l, j: (l, 0, j)),
            pl.BlockSpec((1, 1, tn), lambda l, j: (l, 0, j)),
        ],
        out_specs=pl.BlockSpec((1, MOD_ROWS, tn), lambda l, j: (l, 0, j)),
        out_shape=jax.ShapeDtypeStruct((depth, MOD_ROWS, n), F32),
        compiler_params=_cparams(("arbitrary", "arbitrary"), 40),
        name="ada",
    )(cc, w_ada, b_ada.reshape(depth, 1, n))


def _norm_mod(x, gain, shift, scale):
    ms = jnp.mean(x * x, axis=-1, keepdims=True)
    return (x * lax.rsqrt(ms + EPS) * gain) * (1.0 + scale) + shift


def _mod_slices(m):
    d = D_MODEL
    return [m[:, i * d:(i + 1) * d] for i in range(6)]


def _head_norm(z, seg, gain):
    ms = jnp.dot((z * z).astype(BF16), seg, preferred_element_type=F32)
    return z * lax.rsqrt(ms + EPS) * gain


def _rope(z, cos, sin):
    outs = []
    for c in range(z.shape[1] // LANES):
        zc = z[:, c * LANES:(c + 1) * LANES]
        outs.append(zc * cos + pltpu.roll(zc, LANES // 2, 1) * sin)
    return outs[0] if len(outs) == 1 else jnp.concatenate(outs, axis=-1)


def _stream_specs(tm, d, n_lat):
    return [pl.BlockSpec((1, tm, d), lambda bi, j: (bi, jnp.minimum(j, n_lat - 1), 0)),
            pl.BlockSpec((1, tm, d), lambda bi, j: (bi, jnp.maximum(j - n_lat, 0), 0))]


def _pending_residual(x, y_ref, modp_ref, rows=slice(None)):
    return x + _mod_slices(modp_ref[0])[5] * _unpack_bf16_pairs(y_ref[0, rows, :])


def _inproj_kernel(*refs, n_lat, pending):
    if pending:
        x_ref, y_ref, modp_ref = refs[:3]
        x = _pending_residual(x_ref[0], y_ref, modp_ref)
    else:
        xl_ref, xc_ref = refs[:2]
        x = jnp.where(pl.program_id(1) >= n_lat, xc_ref[0], xl_ref[0])
    (mod_ref, n1_ref, w_ref, qg_ref, kg_ref, seg_ref, cs_ref, cos_ref, sin_ref,
     zr_ref, zi_ref, p_ref, qb_ref, qw_ref, kvb_ref, kvw_ref) = refs[3 if pending else 2:]
    sh1, sc1 = _mod_slices(mod_ref[0])[:2]
    h = _norm_mod(x, n1_ref[...], sh1, sc1)
    u = jnp.dot(h.astype(BF16), w_ref[...], preferred_element_type=F32)
    cos, sin = cos_ref[...], sin_ref[...]
    w = BRANCH_W
    seg = seg_ref[...]
    qb = _rope(_head_norm(u[:, 0:w], seg, qg_ref[...]), cos, sin)
    qb_ref[0] = (qb * (HEAD_DIM ** -0.5 * LOG2E)).astype(BF16)
    qw = _rope(u[:, w:2 * w], cos, sin)
    qw_ref[0] = (qw * (HEAD_DIM ** -0.5 * LOG2E)).astype(BF16)
    o = 2 * w
    kb = _rope(_head_norm(u[:, o:o + LANES], seg[:LANES, :LANES], kg_ref[...]), cos, sin)
    kw = _rope(u[:, o + LANES:o + 2 * LANES], cos, sin)
    o += 2 * LANES
    f_in = u[:, o:o + w].astype(BF16)
    zr, zi = [], []
    for g in range(w // GROUP_W):
        z = jnp.dot(f_in[:, g * GROUP_W:(g + 1) * GROUP_W], cs_ref[...], preferred_element_type=F32)
        zr.append(z[:, :GROUP_W])
        zi.append(z[:, GROUP_W:])
    zr_ref[0] = jnp.concatenate(zr, axis=-1).astype(BF16)
    zi_ref[0] = jnp.concatenate(zi, axis=-1).astype(BF16)
    o += w
    vb = u[:, o:o + LANES]
    vw = u[:, o + LANES:o + 2 * LANES]
    kvb_ref[0] = jnp.concatenate([kb, vb, jnp.ones_like(vb)], axis=-1).astype(BF16)
    kvw_ref[0] = jnp.concatenate([kw, vw, jnp.ones_like(vw)], axis=-1).astype(BF16)
    p_ref[0] = u[:, o + 2 * LANES:o + 2 * LANES + w].astype(BF16)


def _inproj(tokens, t_lat, mod, n1, w_in, qg, kg, seg, cs, cos, sin, tm=256):
    pending = len(tokens) == 3
    b, _, d = tokens[0].shape
    s = tokens[0].shape[1] if pending else t_lat + tokens[1].shape[1]
    nw = w_in.shape[1]
    n_lat = t_lat // tm
    tok = lambda bi, j: (bi, j, 0)
    tab = lambda bi, j: (j, 0)
    mod_spec = pl.BlockSpec((1, 1, 6 * d), lambda bi, j: (jnp.where(j >= n_lat, b, bi), 0, 0))
    if pending:
        token_specs = [pl.BlockSpec((1, tm, d), tok), pl.BlockSpec((1, tm, d // 2), tok), mod_spec]
    else:
        token_specs = _stream_specs(tm, d, n_lat)
    widths = (BRANCH_W,) * 5 + (3 * LANES, 3 * LANES)
    return pl.pallas_call(
        functools.partial(_inproj_kernel, n_lat=n_lat, pending=pending),
        grid=(b, s // tm),
        in_specs=token_specs + [
            mod_spec,
            _const_spec((1, d)),
            _const_spec((d, nw)),
            _const_spec((1, BRANCH_W)),
            _const_spec((1, LANES)),
            _const_spec((BRANCH_W, BRANCH_W)),
            _const_spec((GROUP_W, 2 * GROUP_W)),
            pl.BlockSpec((tm, LANES), tab),
            pl.BlockSpec((tm, LANES), tab),
        ],
        out_specs=[pl.BlockSpec((1, tm, wd), tok) for wd in widths],
        out_shape=[jax.ShapeDtypeStruct((b, s, wd), BF16) for wd in widths],
        compiler_params=_cparams(("parallel", "arbitrary"), 48),
        name="inproj",
    )(*tokens, mod, n1, w_in, qg, kg, seg, cs, cos, sin)


def _split_heads(qc, lane):
    first = (lane & (HEAD_DIM // 2)) == 0
    zero = jnp.zeros_like(qc)
    return jnp.concatenate([jnp.where(first, qc, zero), jnp.where(first, zero, qc)], axis=0)


def _gattn_kernel(q_ref, kv_ref, o_ref, q2_s, acc_s, *, sub):
    tq = q_ref.shape[1]
    lane = lax.broadcasted_iota(jnp.int32, (1, LANES), 1)
    nt = (((1,), (1,)), ((), ()))
    n_chunks = BRANCH_W // LANES
    for c in range(n_chunks):
        q2_s[2 * c * tq:(2 * c + 2) * tq, :] = _split_heads(q_ref[0, :, c * LANES:(c + 1) * LANES], lane)
    k = kv_ref[0, :, 0:LANES]
    v = kv_ref[0, :, LANES:3 * LANES]
    for r in range(2 * n_chunks * tq // sub):
        rows = slice(r * sub, (r + 1) * sub)
        s = lax.dot_general(q2_s[rows, :], k, nt, preferred_element_type=F32)
        p = jnp.exp2(s - jnp.max(s, axis=-1, keepdims=True))
        acc_s[rows, :] = jnp.dot(p.astype(BF16), v, preferred_element_type=F32)
    for c in range(n_chunks):
        lo = acc_s[2 * c * tq:(2 * c + 1) * tq, :]
        hi = acc_s[(2 * c + 1) * tq:(2 * c + 2) * tq, :]
        o = jnp.where(lane < HEAD_DIM, lo[:, :LANES] / lo[:, LANES:], hi[:, :LANES] / hi[:, LANES:])
        o_ref[0, :, c * LANES:(c + 1) * LANES] = o.astype(BF16)


def _gattn(qb, kv, q_start, q_len, k_start, k_len, tq=256, sub=128):
    b = qb.shape[0]
    assert q_start % tq == 0 and q_len % tq == 0 and k_start % k_len == 0 and k_len % LANES == 0
    rows = 2 * tq * (BRANCH_W // LANES)
    return pl.pallas_call(
        functools.partial(_gattn_kernel, sub=sub),
        grid=(b, q_len // tq),
        in_specs=[
            pl.BlockSpec((1, tq, BRANCH_W), lambda bi, j: (bi, q_start // tq + j, 0)),
            pl.BlockSpec((1, k_len, 3 * LANES), lambda bi, j: (bi, k_start // k_len, 0)),
        ],
        out_specs=pl.BlockSpec((1, tq, BRANCH_W), lambda bi, j: (bi, j, 0)),
        out_shape=jax.ShapeDtypeStruct((b, q_len, BRANCH_W), BF16),
        scratch_shapes=[pltpu.VMEM((rows, LANES), BF16), pltpu.VMEM((rows, 2 * LANES), F32)],
        compiler_params=_cparams(("parallel", "arbitrary"), 48),
        name="gattn",
    )(qb, kv)


def _window_bias(n_ctx):
    tq = WATTN_QBLOCKS * QBLK
    qi = np.arange(tq)[:, None]
    kj = np.arange(tq + 2 * QBLK)[None, :]
    band = np.abs(kj - WINDOW - qi) <= WINDOW
    blk = kj // QBLK
    variants = [band & (blk != 0), band, band & (blk != WATTN_QBLOCKS + 1), np.zeros_like(band)]
    out = [np.concatenate([np.ones((tq, n_ctx), bool), v], axis=1) for v in variants]
    return np.where(np.stack(out), 0.0, NEG_BIG).astype(np.float32)


def _wattn_kernel(sink_ref, bias_ref, q_ref, *refs, sub):
    o_ref, q2_s, kv_s, acc_s = refs[-4:]
    tq = q_ref.shape[1]
    off = 0
    for blk in refs[:-4]:
        kv_s[off:off + blk.shape[1], :] = blk[0]
        off += blk.shape[1]
    lane = lax.broadcasted_iota(jnp.int32, (1, LANES), 1)
    nt = (((1,), (1,)), ((), ()))
    n_chunks = BRANCH_W // LANES
    for c in range(n_chunks):
        q2_s[2 * c * tq:(2 * c + 2) * tq, :] = _split_heads(q_ref[0, :, c * LANES:(c + 1) * LANES], lane)
    k, v = kv_s[:, 0:LANES], kv_s[:, LANES:3 * LANES]
    for r in range(2 * n_chunks * tq // sub):
        rows = slice(r * sub, (r + 1) * sub)
        q_off = (r * sub) % tq
        sk = sink_ref[(r * sub) // tq]
        s = lax.dot_general(q2_s[rows, :], k, nt, preferred_element_type=F32) + bias_ref[0, q_off:q_off + sub, :]
        m = jnp.maximum(jnp.max(s, axis=-1, keepdims=True), sk)
        pv = jnp.dot(jnp.exp2(s - m).astype(BF16), v, preferred_element_type=F32)
        acc_s[rows, :LANES] = pv[:, :LANES]
        acc_s[rows, LANES:] = pv[:, LANES:] + jnp.exp2(sk - m)
    for c in range(n_chunks):
        lo = acc_s[2 * c * tq:(2 * c + 1) * tq, :]
        hi = acc_s[(2 * c + 1) * tq:(2 * c + 2) * tq, :]
        o = jnp.where(lane < HEAD_DIM, lo[:, :LANES] / lo[:, LANES:], hi[:, :LANES] / hi[:, LANES:])
        o_ref[0, :, c * LANES:(c + 1) * LANES] = o.astype(BF16)


def _wattn(sink, bias, qw, kv, s_out, t_lat):
    b, s, _ = qw.shape
    n_ctx = s - t_lat
    nq = WATTN_QBLOCKS
    tq = nq * QBLK
    assert n_ctx % tq == 0 and t_lat // tq >= 2
    last = s // QBLK - 1
    n_lat = t_lat // tq
    variant = lambda j: jnp.where(j >= n_lat, 3, jnp.where(j == 0, 0, jnp.where(j == n_lat - 1, 2, 1)))
    key_block = lambda off: pl.BlockSpec(
        (1, QBLK, 3 * LANES), lambda bi, j: (bi, jnp.clip(j * nq + off, 0, last), 0))
    rows = 2 * tq * (BRANCH_W // LANES)
    return pl.pallas_call(
        functools.partial(_wattn_kernel, sub=128),
        scratch_shapes=[pltpu.VMEM((rows, LANES), BF16), pltpu.VMEM((bias.shape[2], 3 * LANES), BF16),
                        pltpu.VMEM((rows, 2 * LANES), F32)],
        grid=(b, s_out // tq),
        in_specs=[
            pl.BlockSpec(memory_space=pltpu.SMEM),
            pl.BlockSpec((1,) + bias.shape[1:], lambda bi, j: (variant(j), 0, 0)),
            pl.BlockSpec((1, tq, BRANCH_W), lambda bi, j: (bi, j, 0)),
            pl.BlockSpec((1, n_ctx, 3 * LANES), lambda bi, j: (bi, t_lat // n_ctx, 0)),
        ] + [key_block(off) for off in range(-1, nq + 1)],
        out_specs=pl.BlockSpec((1, tq, BRANCH_W), lambda bi, j: (bi, j, 0)),
        out_shape=jax.ShapeDtypeStruct((b, s_out, BRANCH_W), BF16),
        compiler_params=_cparams(("parallel", "arbitrary"), 32),
        name="wattn",
    )(sink, bias, qw, *([kv] * (nq + 3)))


def _fourier_tables(n1, n2):
    t = n1 * n2
    k2 = np.arange(n2)[None, :, None]
    t2 = np.arange(n2)[None, None, :]
    t1 = np.arange(n1)[:, None, None]
    theta = 2.0 * np.pi * ((k2 * t2 * n1 + k2 * t1) % t) / t
    er, ei = np.cos(theta) / math.sqrt(n2), -np.sin(theta) / math.sqrt(n2)
    e = np.concatenate([np.concatenate([er, -ei], axis=2), np.concatenate([ei, er], axis=2)], axis=1)
    k1 = np.arange(n1)[:, None]
    phi = 2.0 * np.pi * ((k1 * np.arange(n1)[None, :]) % n1) / n1
    dcat = np.concatenate([np.cos(phi), np.sin(phi)], axis=1) / math.sqrt(n1)
    return e.astype(np.float32), dcat.astype(np.float32)


def _channel_dft_table():
    c = np.arange(GROUP_W)
    ang = 2.0 * np.pi * ((c[:, None] * c[None, :]) % GROUP_W) / GROUP_W
    return (np.concatenate([np.cos(ang), -np.sin(ang)], axis=1) / math.sqrt(GROUP_W)).astype(np.float32)


def _fourier_kernel(zr_ref, zi_ref, e_ref, d_ref, o_ref, xr_s, xi_s, yr_s, yi_s, *, n):
    nc = xr_s.shape[0]
    pitch = n + FOURIER_SLAB_PAD
    chunk = lambda c: slice(c * LANES, (c + 1) * LANES)
    slab = lambda i: slice(i * pitch, i * pitch + n)

    def gather(ref, start):
        return jnp.concatenate([ref[c, pl.ds(start, n, stride=pitch), :] for c in range(nc)], axis=-1)

    for c in range(nc):
        for t2 in range(n):
            xr_s[c, slab(t2), :] = zr_ref[0, t2 * n:(t2 + 1) * n, chunk(c)].astype(F32)
            xi_s[c, slab(t2), :] = zi_ref[0, t2 * n:(t2 + 1) * n, chunk(c)].astype(F32)
    for t1 in range(n):
        xs = jnp.concatenate([gather(xr_s, t1), gather(xi_s, t1)], axis=0).astype(BF16)
        y = jnp.dot(e_ref[t1], xs, preferred_element_type=F32)
        for c in range(nc):
            yr_s[c, slab(t1), :] = y[:n, chunk(c)]
            yi_s[c, slab(t1), :] = y[n:, chunk(c)]
    for k2 in range(n):
        ys = jnp.concatenate([gather(yr_s, k2), gather(yi_s, k2)], axis=0).astype(BF16)
        o = jnp.dot(d_ref[...], ys, preferred_element_type=F32)
        for c in range(nc):
            xr_s[c, pl.ds(k2, n, stride=pitch), :] = o[:, chunk(c)]
    for c in range(nc):
        for k1 in range(n):
            o_ref[0, k1 * n:(k1 + 1) * n, chunk(c)] = xr_s[c, slab(k1), :].astype(BF16)


def _fourier(zr, zi, e_tab, d_tab, t_len, row_block, cw=256):
    b = zr.shape[0]
    n = d_tab.shape[0]
    assert n * n == t_len
    zspec = pl.BlockSpec((1, t_len, cw), lambda bi, j: (bi, row_block, j))
    return pl.pallas_call(
        functools.partial(_fourier_kernel, n=n),
        grid=(b, BRANCH_W // cw),
        in_specs=[zspec, zspec, _const_spec(e_tab.shape), _const_spec(d_tab.shape)],
        out_specs=pl.BlockSpec((1, t_len, cw), lambda bi, j: (bi, 0, j)),
        out_shape=jax.ShapeDtypeStruct((b, t_len, BRANCH_W), BF16),
        scratch_shapes=[pltpu.VMEM((cw // LANES, n * (n + FOURIER_SLAB_PAD), LANES), F32)] * 4,
        compiler_params=_cparams(("parallel", "arbitrary"), 48),
        name="fourier",
    )(zr, zi, e_tab, d_tab)


def _pool_rows(ext, pos, n, w_ref, scale):
    n_ext = ext.shape[0]
    rows = n_ext - 2 * POOL_HALO
    back = lambda v, k: pltpu.roll(v, k, 0)
    fwd = lambda v, k: pltpu.roll(v, n_ext - k, 0)
    outs = []
    for gi, w in enumerate(POOL_WINDOWS):
        e = ext[:, gi * GROUP_W:(gi + 1) * GROUP_W]
        wsum = e + back(e, 1)
        half = 1
        while 2 * half < w:
            wsum = back(wsum, half) + fwd(wsum, half)
            half *= 2
        own = slice(POOL_HALO, POOL_HALO + rows)
        cnt = jnp.minimum(pos + w // 2, n) - jnp.maximum(pos - w // 2, 0)
        pooled = wsum[own] / cnt.astype(F32) - e[own]
        outs.append(jnp.dot(pooled.astype(BF16), w_ref[gi], preferred_element_type=F32))
    return (jnp.concatenate(outs, axis=-1) * scale).astype(BF16)


def _route(logits_t, bias):
    aff = jax.nn.sigmoid(logits_t)
    sel = aff + bias
    neg = -jnp.inf
    firsts, seconds, scores = [], [], []
    for g in range(N_EXPERTS // EXPERTS_PER_GROUP):
        s = [sel[EXPERTS_PER_GROUP * g + k:EXPERTS_PER_GROUP * g + k + 1, :] for k in range(EXPERTS_PER_GROUP)]
        m1 = jnp.maximum(jnp.maximum(s[0], s[1]), jnp.maximum(s[2], s[3]))
        i1 = jnp.where(s[0] == m1, 0, jnp.where(s[1] == m1, 1, jnp.where(s[2] == m1, 2, 3)))
        r = [jnp.where(i1 == k, neg, s[k]) for k in range(EXPERTS_PER_GROUP)]
        m2 = jnp.maximum(jnp.maximum(r[0], r[1]), jnp.maximum(r[2], r[3]))
        i2 = jnp.where(r[0] == m2, 0, jnp.where(r[1] == m2, 1, jnp.where(r[2] == m2, 2, 3)))
        firsts.append(i1 + EXPERTS_PER_GROUP * g)
        seconds.append(i2 + EXPERTS_PER_GROUP * g)
        scores.append(m1 + m2)
    best = jnp.maximum(jnp.maximum(scores[0], scores[1]), jnp.maximum(scores[2], scores[3]))
    pick = lambda v: jnp.where(scores[0] == best, v[0], jnp.where(scores[1] == best, v[1],
                                                                 jnp.where(scores[2] == best, v[2], v[3])))
    e1, e2 = pick(firsts), pick(seconds)
    lo = jnp.minimum(e1, e2) & (EXPERTS_PER_GROUP - 1)
    hi = jnp.maximum(e1, e2) & (EXPERTS_PER_GROUP - 1)
    pair = jnp.where(lo == 0, 0, jnp.where(lo == 1, 3, 5)) + hi - lo - 1
    cls = ((e1 >> 2) * PAIRS_PER_GROUP + pair).astype(F32)
    return jnp.concatenate([cls] + [jnp.zeros_like(cls)] * (ROUTE_ROWS - 1), axis=0)


def _pack_bf16_pairs(v):
    w = v.shape[1] // 2
    bits = pltpu.bitcast(v.astype(BF16).astype(F32), jnp.uint32)
    return pltpu.bitcast(bits[:, :w] | (bits[:, w:] >> 16), jnp.int32)


def _unpack_bf16_pairs(p):
    bits = pltpu.bitcast(p, jnp.uint32)
    hi = pltpu.bitcast(bits & jnp.uint32(0xFFFF0000), F32)
    lo = pltpu.bitcast(bits << 16, F32)
    return jnp.concatenate([hi, lo], axis=-1)


def _merge_kernel(*refs, n_real, n_fill, pending, **static):
    n_tok = 3 if pending else 1
    (mod_ref, n1_ref, n2_ref, a_ref, b_ref, c_ref, pc_ref, pp_ref, pn_ref, pw_ref, psc_ref,
     wg_ref, bg_ref, wb_ref, wo_ref, rw_ref, rb_ref) = refs[n_tok:n_tok + 17]
    outs = refs[-4:-1]
    ins = (refs[:n_tok], mod_ref, n1_ref, n2_ref, a_ref, b_ref, c_ref,
           (pc_ref, pp_ref, pn_ref, pw_ref, psc_ref),
           wg_ref, bg_ref, wb_ref, wo_ref, rw_ref, rb_ref)
    if n_fill:
        @pl.when(pl.program_id(1) >= n_real)
        def _():
            for ref in outs:
                ref[...] = jnp.zeros_like(ref)

        pl.when(pl.program_id(1) < n_real)(lambda: _merge_tile(ins, outs, refs[-1], **static))
    else:
        _merge_tile(ins, outs, refs[-1], **static)


def _merge_tile(ins, outs, merged_s, *, nw, sub, p_off, seq_lo, seq_hi):
    (tok_refs, mod_ref, n1_ref, n2_ref, a_ref, b_ref, c_ref, pool_refs,
     wg_ref, bg_ref, wb_ref, wo_ref, rw_ref, rb_ref) = ins
    xo_ref, h2_ref, route_ref = outs
    pc_ref, pp_ref, pn_ref, pw_ref, psc_ref = pool_refs
    x_ref = tok_refs[0]
    tm = x_ref.shape[1]
    sh1, sc1, g1, sh2, sc2, _ = _mod_slices(mod_ref[0])
    row0 = p_off + pl.program_id(1) * tm
    ext = jnp.concatenate([pp_ref[0], pc_ref[0], pn_ref[0]], axis=0).astype(F32)
    gpos = row0 - POOL_HALO + lax.broadcasted_iota(jnp.int32, (tm + 2 * POOL_HALO, 1), 0)
    ext = jnp.where((gpos >= seq_lo) & (gpos < seq_hi), ext, 0.0)
    for r in range(tm // sub):
        rows = slice(r * sub, (r + 1) * sub)
        x = x_ref[0, rows, :]
        if len(tok_refs) == 3:
            x = _pending_residual(x, tok_refs[1], tok_refs[2], rows)
        hb = _norm_mod(x, n1_ref[...], sh1, sc1).astype(BF16)
        pos = row0 - seq_lo + r * sub + lax.broadcasted_iota(jnp.int32, (sub, 1), 0)
        pooled = _pool_rows(ext[r * sub:(r + 1) * sub + 2 * POOL_HALO], pos, seq_hi - seq_lo,
                            pw_ref, psc_ref[...])
        branches = (a_ref[0, rows, :], b_ref[0, rows, :], c_ref[0, rows, :], pooled)
        for n in range(D_MODEL // nw):
            cols = slice(n * nw, (n + 1) * nw)
            merged = None
            for i, br in enumerate(branches):
                gate = jax.nn.sigmoid(
                    jnp.dot(hb, wg_ref[i, :, cols], preferred_element_type=F32) + bg_ref[i, :, cols])
                term = gate * jnp.dot(br, wb_ref[i, :, cols], preferred_element_type=F32)
                merged = term if merged is None else merged + term
            merged_s[rows, cols] = merged.astype(BF16)
        y = jnp.dot(merged_s[rows, :], wo_ref[...], preferred_element_type=F32)
        xn = x + g1 * y
        xo_ref[0, rows, :] = xn
        h2 = _norm_mod(xn, n2_ref[...], sh2, sc2)
        h2_ref[0, rows, :] = _pack_bf16_pairs(h2)
        h_hi = h2.astype(BF16)
        h_lo = (h2 - h_hi.astype(F32)).astype(BF16)
        nt = (((1,), (1,)), ((), ()))
        by_hi = lax.dot_general(rw_ref[...], h_hi, nt, preferred_element_type=F32)
        by_lo = lax.dot_general(rw_ref[:N_EXPERTS, :], h_lo, nt, preferred_element_type=F32)
        logits_t = by_hi[:N_EXPERTS] + by_hi[N_EXPERTS:] + by_lo
        route_ref[0, :, rows] = _route(logits_t, rb_ref[...])


def _merge(tokens, mod, mod_row, n1, n2, branches, offsets, p_in, pool_params, seq, weights, s_out, out_off,
           rows, tm, prev=None):
    pending = len(tokens) == 3
    b, _, d = tokens[0].shape
    wg, bg, wb, wo, rw_t, rb = weights
    pool_w, pool_scale = pool_params
    seq_lo, seq_hi = seq
    n_real = rows // tm
    n_fill = -(-(s_out - out_off - rows) // tm) if prev is None else 0
    step = lambda j: jnp.minimum(j, n_real - 1)
    blk = lambda width, off: pl.BlockSpec((1, tm, width), lambda bi, j: (bi, off // tm + step(j), 0))
    out_blk = lambda width: pl.BlockSpec((1, tm, width), lambda bi, j: (bi, out_off // tm + j, 0))
    assert all(o % tm == 0 for o in offsets) and out_off % tm == 0 and rows % tm == 0 and seq_lo % tm == 0
    hb = tm // POOL_HALO
    last_halo = p_in.shape[1] // POOL_HALO - 1
    halo = lambda shift: pl.BlockSpec(
        (1, POOL_HALO, BRANCH_W),
        lambda bi, j: (bi, jnp.clip((seq_lo // tm + step(j) + shift) * hb - 1 + shift, 0, last_halo), 0))
    out_shape = [
        jax.ShapeDtypeStruct((b, s_out, d), F32),
        jax.ShapeDtypeStruct((b, s_out, d // 2), jnp.int32),
        jax.ShapeDtypeStruct((b, ROUTE_ROWS, s_out), F32),
    ]
    mod_spec = pl.BlockSpec((1, 1, 6 * d), lambda bi, j: (bi if mod_row is None else mod_row, 0, 0))
    tok_off = out_off if pending else 0
    token_specs = [blk(d, tok_off)] + ([blk(d // 2, tok_off), mod_spec] if pending else [])
    n_in = len(tokens) + 17
    extra_specs = [] if prev is None else [pl.BlockSpec(memory_space=pl.ANY)] * 3
    return pl.pallas_call(
        functools.partial(_merge_kernel, nw=512, sub=MERGE_CHAIN_ROWS, n_real=n_real, n_fill=n_fill,
                          pending=pending, p_off=seq_lo, seq_lo=seq_lo, seq_hi=seq_hi),
        scratch_shapes=[pltpu.VMEM((tm, d), BF16)],
        grid=(b, n_real + n_fill),
        in_specs=token_specs + [
            mod_spec,
            _const_spec((1, d)), _const_spec((1, d)),
            *[blk(BRANCH_W, off) for off in offsets],
            blk(BRANCH_W, seq_lo), halo(0), halo(1),
            _const_spec(pool_w.shape), _const_spec((1, BRANCH_W)),
            _const_spec(wg.shape), _const_spec(bg.shape), _const_spec(wb.shape), _const_spec(wo.shape),
            _const_spec(rw_t.shape), _const_spec(rb.shape),
        ] + extra_specs,
        out_specs=[
            out_blk(d),
            out_blk(d // 2),
            pl.BlockSpec((1, ROUTE_ROWS, tm), lambda bi, j: (bi, 0, out_off // tm + j)),
        ],
        out_shape=out_shape,
        input_output_aliases={} if prev is None else {n_in + i: i for i in range(3)},
        compiler_params=_cparams(("parallel", "arbitrary"), 56),
        name="merge",
    )(*tokens, mod, n1, n2, *branches, p_in, p_in, p_in, pool_w, pool_scale,
      wg, bg, wb, wo, rw_t, rb, *(() if prev is None else prev))


def _rank_kernel(cls_ref, rank_ref, cnt_ref, cnt_s, *, tr):
    @pl.when(pl.program_id(0) == 0)
    def _():
        cnt_s[...] = jnp.zeros_like(cnt_s)

    cls = cls_ref[0]
    cid = lax.broadcasted_iota(jnp.int32, (CLASS_ROWS, tr), 0).astype(F32)
    onehot = cid == cls
    before = lax.broadcasted_iota(jnp.int32, (tr, tr), 0) < lax.broadcasted_iota(jnp.int32, (tr, tr), 1)
    prefix = jnp.dot(jnp.where(onehot, 1.0, 0.0).astype(BF16), jnp.where(before, 1.0, 0.0).astype(BF16),
                     preferred_element_type=F32)
    carry = cnt_s[...][:, 0:1]
    rank_ref[0] = jnp.sum(jnp.where(onehot, prefix + carry, 0.0), axis=0, keepdims=True)
    cnt_s[...] += jnp.sum(jnp.where(onehot, 1.0, 0.0), axis=1, keepdims=True)
    cnt_ref[...] = cnt_s[...]


def _rank(cls_flat, tr=512):
    n = cls_flat.shape[0]
    tr = math.gcd(n, tr)
    cls3 = cls_flat.reshape(n // tr, 1, tr)
    rank, cnt = pl.pallas_call(
        functools.partial(_rank_kernel, tr=tr),
        grid=(n // tr,),
        in_specs=[pl.BlockSpec((1, 1, tr), lambda i: (i, 0, 0))],
        out_specs=[pl.BlockSpec((1, 1, tr), lambda i: (i, 0, 0)), _const_spec((CLASS_ROWS, LANES))],
        out_shape=[jax.ShapeDtypeStruct((n // tr, 1, tr), F32), jax.ShapeDtypeStruct((CLASS_ROWS, LANES), F32)],
        scratch_shapes=[pltpu.VMEM((CLASS_ROWS, LANES), F32)],
        compiler_params=_cparams(("arbitrary",), 32),
        name="rank",
    )(cls3)
    return rank.reshape(n), cnt[:N_CLASSES, 0]


def _sc_layout(n):
    info = plsc.get_sparse_core_info()
    nw = info.num_cores * info.num_subcores
    per_worker = n // nw
    assert per_worker * nw == n
    chunk = max(c for c in range(8, SC_MAX_CHUNK + 1, 8) if per_worker % c == 0)
    return info.num_cores, nw, per_worker // chunk, chunk


def _sc_scatter_rows(src, pos, n_out):
    n, w = src.shape
    nc, nw, k, c = _sc_layout(n)
    mesh = plsc.VectorSubcoreMesh(core_axis_name="c", subcore_axis_name="s")

    @functools.partial(
        pl.kernel, mesh=mesh,
        out_type=jax.ShapeDtypeStruct((n_out, w), src.dtype),
        scratch_types=[pltpu.VMEM((k, c), jnp.int32), pltpu.VMEM((c, w), src.dtype), pltpu.SemaphoreType.DMA],
        name="moe_scatter",
    )
    def scatter(src_hbm, pos_hbm, out_hbm, idx_v, rows_v, sem):
        wid = lax.axis_index("s") * nc + lax.axis_index("c")
        pltpu.sync_copy(pos_hbm.at[wid], idx_v)

        @pl.loop(0, k)
        def _(j):
            off = pl.multiple_of(wid * (k * c) + j * c, 8)
            pltpu.sync_copy(src_hbm.at[pl.ds(off, c)], rows_v)
            pltpu.async_copy(rows_v, out_hbm.at[idx_v.at[j]], sem).wait()

    return scatter(src, pos.reshape(nw, k, c))


def _sc_gather_rows(src, pos):
    n = pos.shape[0]
    w = src.shape[1]
    nc, nw, k, c = _sc_layout(n)
    mesh = plsc.VectorSubcoreMesh(core_axis_name="c", subcore_axis_name="s")

    @functools.partial(
        pl.kernel, mesh=mesh,
        out_type=jax.ShapeDtypeStruct((n, w), src.dtype),
        scratch_types=[pltpu.VMEM((k, c), jnp.int32), pltpu.VMEM((c, w), src.dtype), pltpu.SemaphoreType.DMA],
        name="moe_gather",
    )
    def gather(src_hbm, pos_hbm, out_hbm, idx_v, rows_v, sem):
        wid = lax.axis_index("s") * nc + lax.axis_index("c")
        pltpu.sync_copy(pos_hbm.at[wid], idx_v)

        @pl.loop(0, k)
        def _(j):
            off = pl.multiple_of(wid * (k * c) + j * c, 8)
            pltpu.async_copy(src_hbm.at[idx_v.at[j]], rows_v, sem).wait()
            pltpu.sync_copy(rows_v, out_hbm.at[pl.ds(off, c)])

    return gather(src, pos.reshape(nw, k, c))


def _gmm_kernel(lo_ref, hi_ref, new_ref, nact_ref, h_ref, rw_ref,
                w1a_ref, w1b_ref, w3a_ref, w3b_ref, w2a_ref, w2b_ref, o_ref, w13_s, w2_s):
    t = pl.program_id(0)

    @pl.when(new_ref[t] == 1)
    def _():
        for i, (w1_ref, w3_ref, w2_ref) in enumerate(((w1a_ref, w3a_ref, w2a_ref), (w1b_ref, w3b_ref, w2b_ref))):
            w13_s[i, :, :EXPERT_FF] = w1_ref[0, 0].astype(BF16)
            w13_s[i, :, EXPERT_FF:] = w3_ref[0, 0].astype(BF16)
            w2_s[i] = w2_ref[0, 0].astype(BF16)

    @pl.when(t < nact_ref[0])
    def _():
        xf = _unpack_bf16_pairs(h_ref[...])
        x = xf.astype(BF16)

        def expert(i):
            ab = jnp.dot(x, w13_s[i], preferred_element_type=F32)
            a, gate = ab[:, :EXPERT_FF], ab[:, EXPERT_FF:]
            hid = (a * jax.nn.sigmoid(a)) * gate
            return jnp.dot(hid.astype(BF16), w2_s[i], preferred_element_type=F32)

        aff = [jax.nn.sigmoid(jnp.sum(xf * rw_ref[pl.ds(e_ref[t], 1), :], axis=-1, keepdims=True))
               for e_ref in (lo_ref, hi_ref)]
        total = aff[0] + aff[1]
        y = (aff[0] / total) * expert(0) + (aff[1] / total) * expert(1)
        o_ref[...] = _pack_bf16_pairs(y)


def _gmm(tile_lo, tile_hi, tile_new, n_act, hs, rw_t, w1, w3, w2, layer, tm):
    n_pad, half = hs.shape
    d = 2 * half
    row = lambda t, lo, hi, new, na: (jnp.minimum(t, na[0] - 1), 0)
    e_lo = lambda t, lo, hi, new, na: (layer, lo[jnp.minimum(t, na[0] - 1)], 0, 0)
    e_hi = lambda t, lo, hi, new, na: (layer, hi[jnp.minimum(t, na[0] - 1)], 0, 0)
    up = lambda e: pl.BlockSpec((1, 1, d, EXPERT_FF), e)
    down = lambda e: pl.BlockSpec((1, 1, EXPERT_FF, d), e)
    return pl.pallas_call(
        _gmm_kernel,
        grid_spec=pltpu.PrefetchScalarGridSpec(
            num_scalar_prefetch=4,
            grid=(n_pad // tm,),
            in_specs=[
                pl.BlockSpec((tm, half), row),
                pl.BlockSpec(rw_t.shape, lambda t, lo, hi, new, na: (0, 0)),
                up(e_lo), up(e_hi), up(e_lo), up(e_hi), down(e_lo), down(e_hi),
            ],
            out_specs=pl.BlockSpec((tm, half), row),
            scratch_shapes=[pltpu.VMEM((2, d, 2 * EXPERT_FF), BF16), pltpu.VMEM((2, EXPERT_FF, d), BF16)],
        ),
        out_shape=jax.ShapeDtypeStruct((n_pad, half), jnp.int32),
        compiler_params=_cparams(("arbitrary",), 56),
        name="moe_gmm",
    )(tile_lo, tile_hi, tile_new, n_act, hs, rw_t, w1, w1, w3, w3, w2, w2)


def _moe_routed(h2p, route, rw_t, w1, w3, w2, layer, tm=256):
    b, s, half = h2p.shape
    n = b * s
    n_pad = n + N_CLASSES * tm
    cls = route[:, 0, :].reshape(n)
    rank, counts = _rank(cls)
    counts = counts.astype(jnp.int32)
    padded = (counts + tm - 1) // tm * tm
    ends = jnp.cumsum(padded)
    pos = jnp.take(ends - padded, cls.astype(jnp.int32)) + rank.astype(jnp.int32)
    n_act = (ends[-1] // tm).reshape(1)
    tile_row = jnp.arange(n_pad // tm, dtype=jnp.int32) * tm
    tile_cls = jnp.minimum(jnp.sum(tile_row[:, None] >= ends[None, :], axis=1), N_CLASSES - 1)
    pair_lo, pair_hi = (jnp.asarray(a, jnp.int32) for a in _class_experts())
    hs = _sc_scatter_rows(h2p.reshape(n, half), pos, n_pad)
    prev_cls = jnp.concatenate([jnp.full((1,), -1, tile_cls.dtype), tile_cls[:-1]])
    tile_new = ((tile_cls != prev_cls) & (tile_row < ends[-1])).astype(jnp.int32)
    ys = _gmm(jnp.take(pair_lo, tile_cls), jnp.take(pair_hi, tile_cls), tile_new, n_act, hs, rw_t,
              w1, w3, w2, layer, tm)
    return _sc_gather_rows(ys, pos).reshape(b, s, half)


def _class_experts():
    lo, hi = [], []
    for g in range(N_EXPERTS // EXPERTS_PER_GROUP):
        for i in range(EXPERTS_PER_GROUP):
            for j in range(i + 1, EXPERTS_PER_GROUP):
                lo.append(EXPERTS_PER_GROUP * g + i)
                hi.append(EXPERTS_PER_GROUP * g + j)
    return np.array(lo), np.array(hi)


def _final_residual_kernel(x_ref, y_ref, mod_ref, g_ref, o_ref):
    x = _pending_residual(x_ref[0], y_ref, mod_ref)
    o_ref[0] = x * lax.rsqrt(jnp.mean(x * x, axis=-1, keepdims=True) + EPS) * g_ref[...]


def _final_residual(x1, yp, mod, gain, tm=512):
    b, t, d = x1.shape
    tok = lambda bi, j: (bi, j, 0)
    return pl.pallas_call(
        _final_residual_kernel,
        grid=(b, t // tm),
        in_specs=[
            pl.BlockSpec((1, tm, d), tok),
            pl.BlockSpec((1, tm, d // 2), tok),
            pl.BlockSpec((1, 1, 6 * d), lambda bi, j: (bi, 0, 0)),
            _const_spec((1, d)),
        ],
        out_specs=pl.BlockSpec((1, tm, d), tok),
        out_shape=jax.ShapeDtypeStruct((b, t, d), F32),
        compiler_params=_cparams(("parallel", "arbitrary"), 32),
        name="final_residual",
    )(x1, yp, mod, gain)


ROPE_FREQS = ROPE_AXIS_DIM // 2


def _rope_tables(t_lat, n_ctx):
    rows = t_lat // GRID_W
    row = jnp.repeat(jnp.arange(rows, dtype=F32), GRID_W)
    col = jnp.tile(jnp.arange(GRID_W, dtype=F32), rows)
    inv_freq = ROPE_THETA ** (-jnp.arange(0, ROPE_AXIS_DIM, 2, dtype=F32) / ROPE_AXIS_DIM)
    ang = jnp.stack([row[:, None] * inv_freq, col[:, None] * inv_freq], axis=1)
    lanes = (2, LANES // HEAD_DIM, 2, ROPE_FREQS)
    sign = jnp.array([-1.0, 1.0], F32).reshape(1, 2, 1, 1, 1)
    cos = jnp.broadcast_to(jnp.cos(ang)[:, None, None], (t_lat,) + lanes).reshape(t_lat, LANES)
    sin = jnp.broadcast_to(sign * jnp.sin(ang)[:, None, None], (t_lat,) + lanes).reshape(t_lat, LANES)
    pad = lambda tbl, fill: jnp.concatenate([tbl, jnp.full((n_ctx, LANES), fill, F32)], axis=0)
    return pad(cos, 1.0), pad(sin, 0.0)


def _qk_lanes(w, n_heads):
    lead = w.shape[:-1]
    n_chunks = n_heads // N_KV_HEADS
    w = w.reshape(lead + (N_KV_HEADS, n_chunks, 2, 2, ROPE_FREQS))
    nl = len(lead)
    w = jnp.transpose(w, tuple(range(nl)) + (nl + 1, nl + 3, nl, nl + 2, nl + 4))
    return w.reshape(lead + (n_heads * HEAD_DIM,))


def _permute_heads(w, axis):
    shp = w.shape
    w = w.reshape(shp[:axis] + (N_Q_HEADS, HEAD_DIM) + shp[axis + 1:])
    w = jnp.take(w, jnp.array(HEAD_PERM), axis=axis)
    return w.reshape(shp)


def _square_factor(n):
    r = int(round(math.sqrt(n)))
    assert r * r == n, "sequence lengths must be perfect squares for the two-stage DFT"
    return r


def kernel(x, c, ctx, c_ctx, w_ada, b_ada, norm1, norm2, w_in, q_gain, k_gain, sink, pool_w, pool_scale,
           w_branch, w_gate, b_gate, w_out, router_w, router_bias, w1, w3, w2, norm_f):
    b, t_lat, d = x.shape
    n_ctx = ctx.shape[1]
    s = t_lat + n_ctx
    depth = w_ada.shape[0]
    assert d == D_MODEL and b < MOD_ROWS and t_lat % 256 == 0 and n_ctx % 256 == 0 and t_lat % n_ctx == 0

    tokens = (x, ctx)
    cc = jnp.zeros((MOD_ROWS, d), F32).at[:b].set(c).at[b].set(c_ctx)
    mod_all = _ada(cc, w_ada, b_ada).reshape(depth, MOD_ROWS, 1, 6 * d)

    cos, sin = _rope_tables(t_lat, n_ctx)
    lane_head = (np.arange(BRANCH_W) // LANES) * 2 + (np.arange(BRANCH_W) // (HEAD_DIM // 2)) % 2
    seg = jnp.asarray((lane_head[:, None] == lane_head[None, :]) / HEAD_DIM, BF16)
    cs = jnp.asarray(_channel_dft_table()).astype(BF16)
    f_lat = [jnp.asarray(a).astype(BF16) for a in _fourier_tables(*(_square_factor(t_lat),) * 2)]
    f_ctx = [jnp.asarray(a).astype(BF16) for a in _fourier_tables(*(_square_factor(n_ctx),) * 2)]
    wbias = jnp.asarray(_window_bias(n_ctx))
    rw_hi = router_w.T.astype(BF16)
    rw_t = jnp.concatenate([rw_hi, (router_w.T - rw_hi.astype(F32)).astype(BF16)], axis=0)
    rb = router_bias.reshape(N_EXPERTS, 1)

    for l in range(depth):
        need_ctx = l < depth - 1
        s_out = s if need_ctx else t_lat
        cols = jnp.split(w_in[l], np.cumsum((512, 512, 512, 128, 128, 512, 128))[:], axis=1)
        f_w, p_w, qb_w, kb_w, vb_w, qw_w, kw_w, vw_w = cols
        w_in_l = jnp.concatenate([_qk_lanes(qb_w, N_Q_HEADS), _qk_lanes(qw_w, N_Q_HEADS),
                                  _qk_lanes(kb_w, N_KV_HEADS), _qk_lanes(kw_w, N_KV_HEADS),
                                  f_w, vb_w, vw_w, p_w], axis=1).astype(BF16)
        wb_l = jnp.stack([w_branch[l, 0], _permute_heads(w_branch[l, 1], 0),
                          _permute_heads(w_branch[l, 2], 0), w_branch[l, 3]]).astype(BF16)
        mod = mod_all[l]
        n1 = norm1[l].reshape(1, d)
        n2 = norm2[l].reshape(1, d)
        qg = _qk_lanes(jnp.tile(q_gain[l], N_Q_HEADS), N_Q_HEADS).reshape(1, BRANCH_W)
        kg = _qk_lanes(jnp.tile(k_gain[l], N_KV_HEADS), N_KV_HEADS).reshape(1, LANES)

        zr, zi, p_in, qb, qw, kvb, kvw = _inproj(tokens, t_lat, mod, n1, w_in_l, qg, kg, seg, cs, cos, sin)

        out_a = _fourier(zr, zi, f_lat[0], f_lat[1], t_lat, 0)
        out_b = _gattn(qb, kvb, 0, t_lat, 0, s, tq=512)
        out_c = _wattn(jnp.take(sink[l], jnp.array(HEAD_PERM)) * LOG2E, wbias, qw, kvw, s_out, t_lat)

        weights = (w_gate[l].astype(BF16), b_gate[l].reshape(4, 1, d), wb_l, w_out[l].astype(BF16), rw_t, rb)
        pool_params = (pool_w[l].astype(BF16), pool_scale[l].reshape(1, BRANCH_W))
        pending = len(tokens) == 3
        merged = _merge(tokens if pending else tokens[:1], mod, None, n1, n2, (out_a, out_b, out_c), (0, 0, 0),
                        p_in, pool_params, (0, t_lat), weights, s_out, 0, t_lat, 2 * MERGE_CHAIN_ROWS)
        if need_ctx:
            out_ac = _fourier(zr, zi, f_ctx[0], f_ctx[1], n_ctx, t_lat // n_ctx)
            out_bc = _gattn(qb, kvb, t_lat, n_ctx, t_lat, n_ctx)
            merged = _merge(tokens if pending else tokens[1:], mod, b, n1, n2, (out_ac, out_bc, out_c),
                            (0, 0, t_lat), p_in, pool_params, (t_lat, s), weights, s_out, t_lat, n_ctx,
                            MERGE_CHAIN_ROWS, prev=merged)
        x1, h2p, route = merged
        yp = _moe_routed(h2p, route, router_w.T, w1, w3, w2, l)
        tokens = (x1, yp, mod)

    return _final_residual(*tokens, norm_f.reshape(1, d))
```

```python
import functools
import math

import numpy as np
import jax
import jax.numpy as jnp
from jax import lax
from jax.experimental import pallas as pl
from jax.experimental.pallas import tpu as pltpu
from jax.experimental.pallas import tpu_sc as plsc

F32 = jnp.float32
BF16 = jnp.bfloat16

D_MODEL = 1024
HEAD_DIM = 64
N_Q_HEADS = 8
N_KV_HEADS = 2
GRID_W = 64
ROPE_THETA = 10000.0
ROPE_AXIS_DIM = HEAD_DIM // 2
QBLK = 128
WINDOW = 128
BRANCH_W = 512
GROUP_W = 128
POOL_WINDOWS = (2, 4, 8, 16)
N_EXPERTS = 16
EXPERTS_PER_GROUP = 4
EXPERT_FF = 512
EPS = 1e-6
MOD_ROWS = 16
NEG_BIG = -1e30
LOG2E = math.log2(math.e)
LANES = 128
POOL_HALO = 16
PAIRS_PER_GROUP = 6
N_CLASSES = 24
CLASS_ROWS = 32
ROUTE_ROWS = 8
SC_MAX_CHUNK = 64
WATTN_QBLOCKS = 2
MERGE_CHAIN_ROWS = 256
FOURIER_SLAB_PAD = 4

HEAD_PERM = (0, 4, 1, 5, 2, 6, 3, 7)


def _cparams(sem, vmem_mb):
    return pltpu.CompilerParams(dimension_semantics=sem, vmem_limit_bytes=vmem_mb * 1024 * 1024)


def _const_spec(shape):
    nd = len(shape)
    return pl.BlockSpec(shape, lambda *_: (0,) * nd)


def _ada_kernel(c_ref, w_ref, b_ref, o_ref):
    c = c_ref[...]
    s = c * jax.nn.sigmoid(c)
    o_ref[0] = jnp.dot(s.astype(BF16), w_ref[0].astype(BF16), preferred_element_type=F32) + b_ref[0]


def _ada(cc, w_ada, b_ada):
    depth, d, n = w_ada.shape
    tn = 1536
    return pl.pallas_call(
        _ada_kernel,
        grid=(depth, n // tn),
        in_specs=[
            pl.BlockSpec((MOD_ROWS, d), lambda l, j: (0, 0)),
            pl.BlockSpec((1, d, tn), lambda l, j: (l, 0, j)),
            pl.BlockSpec((1, 1, tn), lambda l, j: (l, 0, j)),
        ],
        out_specs=pl.BlockSpec((1, MOD_ROWS, tn), lambda l, j: (l, 0, j)),
        out_shape=jax.ShapeDtypeStruct((depth, MOD_ROWS, n), F32),
        compiler_params=_cparams(("arbitrary", "arbitrary"), 40),
        name="ada",
    )(cc, w_ada, b_ada.reshape(depth, 1, n))


def _norm_mod(x, gain, shift, scale):
    ms = jnp.mean(x * x, axis=-1, keepdims=True)
    return (x * lax.rsqrt(ms + EPS) * gain) * (1.0 + scale) + shift


def _mod_slices(m):
    d = D_MODEL
    return [m[:, i * d:(i + 1) * d] for i in range(6)]


def _head_norm(z, seg, gain):
    ms = jnp.dot((z * z).astype(BF16), seg, preferred_element_type=F32)
    return z * lax.rsqrt(ms + EPS) * gain


def _rope(z, cos, sin):
    outs = []
    for c in range(z.shape[1] // LANES):
        zc = z[:, c * LANES:(c + 1) * LANES]
        outs.append(zc * cos + pltpu.roll(zc, LANES // 2, 1) * sin)
    return outs[0] if len(outs) == 1 else jnp.concatenate(outs, axis=-1)


def _stream_specs(tm, d, n_lat):
    return [pl.BlockSpec((1, tm, d), lambda bi, j: (bi, jnp.minimum(j, n_lat - 1), 0)),
            pl.BlockSpec((1, tm, d), lambda bi, j: (bi, jnp.maximum(j - n_lat, 0), 0))]


def _pending_residual(x, y_ref, modp_ref, rows=slice(None)):
    return x + _mod_slices(modp_ref[0])[5] * _unpack_bf16_pairs(y_ref[0, rows, :])


def _inproj_kernel(*refs, n_lat, pending):
    if pending:
        x_ref, y_ref, modp_ref = refs[:3]
        x = _pending_residual(x_ref[0], y_ref, modp_ref)
    else:
        xl_ref, xc_ref = refs[:2]
        x = jnp.where(pl.program_id(1) >= n_lat, xc_ref[0], xl_ref[0])
    (mod_ref, n1_ref, w_ref, qg_ref, kg_ref, seg_ref, cs_ref, cos_ref, sin_ref,
     zr_ref, zi_ref, p_ref, qb_ref, qw_ref, kvb_ref, kvw_ref) = refs[3 if pending else 2:]
    sh1, sc1 = _mod_slices(mod_ref[0])[:2]
    h = _norm_mod(x, n1_ref[...], sh1, sc1)
    u = jnp.dot(h.astype(BF16), w_ref[...], preferred_element_type=F32)
    cos, sin = cos_ref[...], sin_ref[...]
    w = BRANCH_W
    seg = seg_ref[...]
    qb = _rope(_head_norm(u[:, 0:w], seg, qg_ref[...]), cos, sin)
    qb_ref[0] = (qb * (HEAD_DIM ** -0.5 * LOG2E)).astype(BF16)
    qw = _rope(u[:, w:2 * w], cos, sin)
    qw_ref[0] = (qw * (HEAD_DIM ** -0.5 * LOG2E)).astype(BF16)
    o = 2 * w
    kb = _rope(_head_norm(u[:, o:o + LANES], seg[:LANES, :LANES], kg_ref[...]), cos, sin)
    kw = _rope(u[:, o + LANES:o + 2 * LANES], cos, sin)
    o += 2 * LANES
    f_in = u[:, o:o + w].astype(BF16)
    zr, zi = [], []
    for g in range(w // GROUP_W):
        z = jnp.dot(f_in[:, g * GROUP_W:(g + 1) * GROUP_W], cs_ref[...], preferred_element_type=F32)
        zr.append(z[:, :GROUP_W])
        zi.append(z[:, GROUP_W:])
    zr_ref[0] = jnp.concatenate(zr, axis=-1).astype(BF16)
    zi_ref[0] = jnp.concatenate(zi, axis=-1).astype(BF16)
    o += w
    vb = u[:, o:o + LANES]
    vw = u[:, o + LANES:o + 2 * LANES]
    kvb_ref[0] = jnp.concatenate([kb, vb, jnp.ones_like(vb)], axis=-1).astype(BF16)
    kvw_ref[0] = jnp.concatenate([kw, vw, jnp.ones_like(vw)], axis=-1).astype(BF16)
    p_ref[0] = u[:, o + 2 * LANES:o + 2 * LANES + w].astype(BF16)


def _inproj(tokens, t_lat, mod, n1, w_in, qg, kg, seg, cs, cos, sin, tm=256):
    pending = len(tokens) == 3
    b, _, d = tokens[0].shape
    s = tokens[0].shape[1] if pending else t_lat + tokens[1].shape[1]
    nw = w_in.shape[1]
    n_lat = t_lat // tm
    tok = lambda bi, j: (bi, j, 0)
    tab = lambda bi, j: (j, 0)
    mod_spec = pl.BlockSpec((1, 1, 6 * d), lambda bi, j: (jnp.where(j >= n_lat, b, bi), 0, 0))
    if pending:
        token_specs = [pl.BlockSpec((1, tm, d), tok), pl.BlockSpec((1, tm, d // 2), tok), mod_spec]
    else:
        token_specs = _stream_specs(tm, d, n_lat)
    widths = (BRANCH_W,) * 5 + (3 * LANES, 3 * LANES)
    return pl.pallas_call(
        functools.partial(_inproj_kernel, n_lat=n_lat, pending=pending),
        grid=(b, s // tm),
        in_specs=token_specs + [
            mod_spec,
            _const_spec((1, d)),
            _const_spec((d, nw)),
            _const_spec((1, BRANCH_W)),
            _const_spec((1, LANES)),
            _const_spec((BRANCH_W, BRANCH_W)),
            _const_spec((GROUP_W, 2 * GROUP_W)),
            pl.BlockSpec((tm, LANES), tab),
            pl.BlockSpec((tm, LANES), tab),
        ],
        out_specs=[pl.BlockSpec((1, tm, wd), tok) for wd in widths],
        out_shape=[jax.ShapeDtypeStruct((b, s, wd), BF16) for wd in widths],
        compiler_params=_cparams(("parallel", "arbitrary"), 48),
        name="inproj",
    )(*tokens, mod, n1, w_in, qg, kg, seg, cs, cos, sin)


def _split_heads(qc, lane):
    first = (lane & (HEAD_DIM // 2)) == 0
    zero = jnp.zeros_like(qc)
    return jnp.concatenate([jnp.where(first, qc, zero), jnp.where(first, zero, qc)], axis=0)


def _gattn_kernel(q_ref, kv_ref, o_ref, q2_s, acc_s, *, sub):
    tq = q_ref.shape[1]
    lane = lax.broadcasted_iota(jnp.int32, (1, LANES), 1)
    nt = (((1,), (1,)), ((), ()))
    n_chunks = BRANCH_W // LANES
    for c in range(n_chunks):
        q2_s[2 * c * tq:(2 * c + 2) * tq, :] = _split_heads(q_ref[0, :, c * LANES:(c + 1) * LANES], lane)
    k = kv_ref[0, :, 0:LANES]
    v = kv_ref[0, :, LANES:3 * LANES]
    for r in range(2 * n_chunks * tq // sub):
        rows = slice(r * sub, (r + 1) * sub)
        s = lax.dot_general(q2_s[rows, :], k, nt, preferred_element_type=F32)
        p = jnp.exp2(s - jnp.max(s, axis=-1, keepdims=True))
        acc_s[rows, :] = jnp.dot(p.astype(BF16), v, preferred_element_type=F32)
    for c in range(n_chunks):
        lo = acc_s[2 * c * tq:(2 * c + 1) * tq, :]
        hi = acc_s[(2 * c + 1) * tq:(2 * c + 2) * tq, :]
        o = jnp.where(lane < HEAD_DIM, lo[:, :LANES] / lo[:, LANES:], hi[:, :LANES] / hi[:, LANES:])
        o_ref[0, :, c * LANES:(c + 1) * LANES] = o.astype(BF16)


def _gattn(qb, kv, q_start, q_len, k_start, k_len, tq=256, sub=128):
    b = qb.shape[0]
    assert q_start % tq == 0 and q_len % tq == 0 and k_start % k_len == 0 and k_len % LANES == 0
    rows = 2 * tq * (BRANCH_W // LANES)
    return pl.pallas_call(
        functools.partial(_gattn_kernel, sub=sub),
        grid=(b, q_len // tq),
        in_specs=[
            pl.BlockSpec((1, tq, BRANCH_W), lambda bi, j: (bi, q_start // tq + j, 0)),
            pl.BlockSpec((1, k_len, 3 * LANES), lambda bi, j: (bi, k_start // k_len, 0)),
        ],
        out_specs=pl.BlockSpec((1, tq, BRANCH_W), lambda bi, j: (bi, j, 0)),
        out_shape=jax.ShapeDtypeStruct((b, q_len, BRANCH_W), BF16),
        scratch_shapes=[pltpu.VMEM((rows, LANES), BF16), pltpu.VMEM((rows, 2 * LANES), F32)],
        compiler_params=_cparams(("parallel", "arbitrary"), 48),
        name="gattn",
    )(qb, kv)


def _window_bias(n_ctx):
    tq = WATTN_QBLOCKS * QBLK
    qi = np.arange(tq)[:, None]
    kj = np.arange(tq + 2 * QBLK)[None, :]
    band = np.abs(kj - WINDOW - qi) <= WINDOW
    blk = kj // QBLK
    variants = [band & (blk != 0), band, band & (blk != WATTN_QBLOCKS + 1), np.zeros_like(band)]
    out = [np.concatenate([np.ones((tq, n_ctx), bool), v], axis=1) for v in variants]
    return np.where(np.stack(out), 0.0, NEG_BIG).astype(np.float32)


def _wattn_kernel(sink_ref, bias_ref, q_ref, *refs, sub):
    o_ref, q2_s, kv_s, acc_s = refs[-4:]
    tq = q_ref.shape[1]
    off = 0
    for blk in refs[:-4]:
        kv_s[off:off + blk.shape[1], :] = blk[0]
        off += blk.shape[1]
    lane = lax.broadcasted_iota(jnp.int32, (1, LANES), 1)
    nt = (((1,), (1,)), ((), ()))
    n_chunks = BRANCH_W // LANES
    for c in range(n_chunks):
        q2_s[2 * c * tq:(2 * c + 2) * tq, :] = _split_heads(q_ref[0, :, c * LANES:(c + 1) * LANES], lane)
    k, v = kv_s[:, 0:LANES], kv_s[:, LANES:3 * LANES]
    for r in range(2 * n_chunks * tq // sub):
        rows = slice(r * sub, (r + 1) * sub)
        q_off = (r * sub) % tq
        sk = sink_ref[(r * sub) // tq]
        s = lax.dot_general(q2_s[rows, :], k, nt, preferred_element_type=F32) + bias_ref[0, q_off:q_off + sub, :]
        m = jnp.maximum(jnp.max(s, axis=-1, keepdims=True), sk)
        pv = jnp.dot(jnp.exp2(s - m).astype(BF16), v, preferred_element_type=F32)
        acc_s[rows, :LANES] = pv[:, :LANES]
        acc_s[rows, LANES:] = pv[:, LANES:] + jnp.exp2(sk - m)
    for c in range(n_chunks):
        lo = acc_s[2 * c * tq:(2 * c + 1) * tq, :]
        hi = acc_s[(2 * c + 1) * tq:(2 * c + 2) * tq, :]
        o = jnp.where(lane < HEAD_DIM, lo[:, :LANES] / lo[:, LANES:], hi[:, :LANES] / hi[:, LANES:])
        o_ref[0, :, c * LANES:(c + 1) * LANES] = o.astype(BF16)


def _wattn(sink, bias, qw, kv, s_out, t_lat):
    b, s, _ = qw.shape
    n_ctx = s - t_lat
    nq = WATTN_QBLOCKS
    tq = nq * QBLK
    assert n_ctx % tq == 0 and t_lat // tq >= 2
    last = s // QBLK - 1
    n_lat = t_lat // tq
    variant = lambda j: jnp.where(j >= n_lat, 3, jnp.where(j == 0, 0, jnp.where(j == n_lat - 1, 2, 1)))
    key_block = lambda off: pl.BlockSpec(
        (1, QBLK, 3 * LANES), lambda bi, j: (bi, jnp.clip(j * nq + off, 0, last), 0))
    rows = 2 * tq * (BRANCH_W // LANES)
    return pl.pallas_call(
        functools.partial(_wattn_kernel, sub=128),
        scratch_shapes=[pltpu.VMEM((rows, LANES), BF16), pltpu.VMEM((bias.shape[2], 3 * LANES), BF16),
                        pltpu.VMEM((rows, 2 * LANES), F32)],
        grid=(b, s_out // tq),
        in_specs=[
            pl.BlockSpec(memory_space=pltpu.SMEM),
            pl.BlockSpec((1,) + bias.shape[1:], lambda bi, j: (variant(j), 0, 0)),
            pl.BlockSpec((1, tq, BRANCH_W), lambda bi, j: (bi, j, 0)),
            pl.BlockSpec((1, n_ctx, 3 * LANES), lambda bi, j: (bi, t_lat // n_ctx, 0)),
        ] + [key_block(off) for off in range(-1, nq + 1)],
        out_specs=pl.BlockSpec((1, tq, BRANCH_W), lambda bi, j: (bi, j, 0)),
        out_shape=jax.ShapeDtypeStruct((b, s_out, BRANCH_W), BF16),
        compiler_params=_cparams(("parallel", "arbitrary"), 32),
        name="wattn",
    )(sink, bias, qw, *([kv] * (nq + 3)))


def _fourier_tables(n1, n2):
    t = n1 * n2
    k2 = np.arange(n2)[None, :, None]
    t2 = np.arange(n2)[None, None, :]
    t1 = np.arange(n1)[:, None, None]
    theta = 2.0 * np.pi * ((k2 * t2 * n1 + k2 * t1) % t) / t
    er, ei = np.cos(theta) / math.sqrt(n2), -np.sin(theta) / math.sqrt(n2)
    e = np.concatenate([np.concatenate([er, -ei], axis=2), np.concatenate([ei, er], axis=2)], axis=1)
    k1 = np.arange(n1)[:, None]
    phi = 2.0 * np.pi * ((k1 * np.arange(n1)[None, :]) % n1) / n1
    dcat = np.concatenate([np.cos(phi), np.sin(phi)], axis=1) / math.sqrt(n1)
    return e.astype(np.float32), dcat.astype(np.float32)


def _channel_dft_table():
    c = np.arange(GROUP_W)
    ang = 2.0 * np.pi * ((c[:, None] * c[None, :]) % GROUP_W) / GROUP_W
    return (np.concatenate([np.cos(ang), -np.sin(ang)], axis=1) / math.sqrt(GROUP_W)).astype(np.float32)


def _fourier_kernel(zr_ref, zi_ref, e_ref, d_ref, o_ref, xr_s, xi_s, yr_s, yi_s, *, n):
    nc = xr_s.shape[0]
    pitch = n + FOURIER_SLAB_PAD
    chunk = lambda c: slice(c * LANES, (c + 1) * LANES)
    slab = lambda i: slice(i * pitch, i * pitch + n)

    def gather(ref, start):
        return jnp.concatenate([ref[c, pl.ds(start, n, stride=pitch), :] for c in range(nc)], axis=-1)

    for c in range(nc):
        for t2 in range(n):
            xr_s[c, slab(t2), :] = zr_ref[0, t2 * n:(t2 + 1) * n, chunk(c)].astype(F32)
            xi_s[c, slab(t2), :] = zi_ref[0, t2 * n:(t2 + 1) * n, chunk(c)].astype(F32)
    for t1 in range(n):
        xs = jnp.concatenate([gather(xr_s, t1), gather(xi_s, t1)], axis=0).astype(BF16)
        y = jnp.dot(e_ref[t1], xs, preferred_element_type=F32)
        for c in range(nc):
            yr_s[c, slab(t1), :] = y[:n, chunk(c)]
            yi_s[c, slab(t1), :] = y[n:, chunk(c)]
    for k2 in range(n):
        ys = jnp.concatenate([gather(yr_s, k2), gather(yi_s, k2)], axis=0).astype(BF16)
        o = jnp.dot(d_ref[...], ys, preferred_element_type=F32)
        for c in range(nc):
            xr_s[c, pl.ds(k2, n, stride=pitch), :] = o[:, chunk(c)]
    for c in range(nc):
        for k1 in range(n):
            o_ref[0, k1 * n:(k1 + 1) * n, chunk(c)] = xr_s[c, slab(k1), :].astype(BF16)


def _fourier(zr, zi, e_tab, d_tab, t_len, row_block, cw=256):
    b = zr.shape[0]
    n = d_tab.shape[0]
    assert n * n == t_len
    zspec = pl.BlockSpec((1, t_len, cw), lambda bi, j: (bi, row_block, j))
    return pl.pallas_call(
        functools.partial(_fourier_kernel, n=n),
        grid=(b, BRANCH_W // cw),
        in_specs=[zspec, zspec, _const_spec(e_tab.shape), _const_spec(d_tab.shape)],
        out_specs=pl.BlockSpec((1, t_len, cw), lambda bi, j: (bi, 0, j)),
        out_shape=jax.ShapeDtypeStruct((b, t_len, BRANCH_W), BF16),
        scratch_shapes=[pltpu.VMEM((cw // LANES, n * (n + FOURIER_SLAB_PAD), LANES), F32)] * 4,
        compiler_params=_cparams(("parallel", "arbitrary"), 48),
        name="fourier",
    )(zr, zi, e_tab, d_tab)


def _pool_rows(ext, pos, n, w_ref, scale):
    n_ext = ext.shape[0]
    rows = n_ext - 2 * POOL_HALO
    back = lambda v, k: pltpu.roll(v, k, 0)
    fwd = lambda v, k: pltpu.roll(v, n_ext - k, 0)
    outs = []
    for gi, w in enumerate(POOL_WINDOWS):
        e = ext[:, gi * GROUP_W:(gi + 1) * GROUP_W]
        wsum = e + back(e, 1)
        half = 1
        while 2 * half < w:
            wsum = back(wsum, half) + fwd(wsum, half)
            half *= 2
        own = slice(POOL_HALO, POOL_HALO + rows)
        cnt = jnp.minimum(pos + w // 2, n) - jnp.maximum(pos - w // 2, 0)
        pooled = wsum[own] / cnt.astype(F32) - e[own]
        outs.append(jnp.dot(pooled.astype(BF16), w_ref[gi], preferred_element_type=F32))
    return (jnp.concatenate(outs, axis=-1) * scale).astype(BF16)


def _route(logits_t, bias):
    aff = jax.nn.sigmoid(logits_t)
    sel = aff + bias
    neg = -jnp.inf
    firsts, seconds, scores = [], [], []
    for g in range(N_EXPERTS // EXPERTS_PER_GROUP):
        s = [sel[EXPERTS_PER_GROUP * g + k:EXPERTS_PER_GROUP * g + k + 1, :] for k in range(EXPERTS_PER_GROUP)]
        m1 = jnp.maximum(jnp.maximum(s[0], s[1]), jnp.maximum(s[2], s[3]))
        i1 = jnp.where(s[0] == m1, 0, jnp.where(s[1] == m1, 1, jnp.where(s[2] == m1, 2, 3)))
        r = [jnp.where(i1 == k, neg, s[k]) for k in range(EXPERTS_PER_GROUP)]
        m2 = jnp.maximum(jnp.maximum(r[0], r[1]), jnp.maximum(r[2], r[3]))
        i2 = jnp.where(r[0] == m2, 0, jnp.where(r[1] == m2, 1, jnp.where(r[2] == m2, 2, 3)))
        firsts.append(i1 + EXPERTS_PER_GROUP * g)
        seconds.append(i2 + EXPERTS_PER_GROUP * g)
        scores.append(m1 + m2)
    best = jnp.maximum(jnp.maximum(scores[0], scores[1]), jnp.maximum(scores[2], scores[3]))
    pick = lambda v: jnp.where(scores[0] == best, v[0], jnp.where(scores[1] == best, v[1],
                                                                 jnp.where(scores[2] == best, v[2], v[3])))
    e1, e2 = pick(firsts), pick(seconds)
    lo = jnp.minimum(e1, e2) & (EXPERTS_PER_GROUP - 1)
    hi = jnp.maximum(e1, e2) & (EXPERTS_PER_GROUP - 1)
    pair = jnp.where(lo == 0, 0, jnp.where(lo == 1, 3, 5)) + hi - lo - 1
    cls = ((e1 >> 2) * PAIRS_PER_GROUP + pair).astype(F32)
    return jnp.concatenate([cls] + [jnp.zeros_like(cls)] * (ROUTE_ROWS - 1), axis=0)


def _pack_bf16_pairs(v):
    w = v.shape[1] // 2
    bits = pltpu.bitcast(v.astype(BF16).astype(F32), jnp.uint32)
    return pltpu.bitcast(bits[:, :w] | (bits[:, w:] >> 16), jnp.int32)


def _unpack_bf16_pairs(p):
    bits = pltpu.bitcast(p, jnp.uint32)
    hi = pltpu.bitcast(bits & jnp.uint32(0xFFFF0000), F32)
    lo = pltpu.bitcast(bits << 16, F32)
    return jnp.concatenate([hi, lo], axis=-1)


def _merge_kernel(*refs, n_real, n_fill, pending, **static):
    n_tok = 3 if pending else 1
    (mod_ref, n1_ref, n2_ref, a_ref, b_ref, c_ref, pc_ref, pp_ref, pn_ref, pw_ref, psc_ref,
     wg_ref, bg_ref, wb_ref, wo_ref, rw_ref, rb_ref) = refs[n_tok:n_tok + 17]
    outs = refs[-4:-1]
    ins = (refs[:n_tok], mod_ref, n1_ref, n2_ref, a_ref, b_ref, c_ref,
           (pc_ref, pp_ref, pn_ref, pw_ref, psc_ref),
           wg_ref, bg_ref, wb_ref, wo_ref, rw_ref, rb_ref)
    if n_fill:
        @pl.when(pl.program_id(1) >= n_real)
        def _():
            for ref in outs:
                ref[...] = jnp.zeros_like(ref)

        pl.when(pl.program_id(1) < n_real)(lambda: _merge_tile(ins, outs, refs[-1], **static))
    else:
        _merge_tile(ins, outs, refs[-1], **static)


def _merge_tile(ins, outs, merged_s, *, nw, sub, p_off, seq_lo, seq_hi):
    (tok_refs, mod_ref, n1_ref, n2_ref, a_ref, b_ref, c_ref, pool_refs,
     wg_ref, bg_ref, wb_ref, wo_ref, rw_ref, rb_ref) = ins
    xo_ref, h2_ref, route_ref = outs
    pc_ref, pp_ref, pn_ref, pw_ref, psc_ref = pool_refs
    x_ref = tok_refs[0]
    tm = x_ref.shape[1]
    sh1, sc1, g1, sh2, sc2, _ = _mod_slices(mod_ref[0])
    row0 = p_off + pl.program_id(1) * tm
    ext = jnp.concatenate([pp_ref[0], pc_ref[0], pn_ref[0]], axis=0).astype(F32)
    gpos = row0 - POOL_HALO + lax.broadcasted_iota(jnp.int32, (tm + 2 * POOL_HALO, 1), 0)
    ext = jnp.where((gpos >= seq_lo) & (gpos < seq_hi), ext, 0.0)
    for r in range(tm // sub):
        rows = slice(r * sub, (r + 1) * sub)
        x = x_ref[0, rows, :]
        if len(tok_refs) == 3:
            x = _pending_residual(x, tok_refs[1], tok_refs[2], rows)
        hb = _norm_mod(x, n1_ref[...], sh1, sc1).astype(BF16)
        pos = row0 - seq_lo + r * sub + lax.broadcasted_iota(jnp.int32, (sub, 1), 0)
        pooled = _pool_rows(ext[r * sub:(r + 1) * sub + 2 * POOL_HALO], pos, seq_hi - seq_lo,
                            pw_ref, psc_ref[...])
        branches = (a_ref[0, rows, :], b_ref[0, rows, :], c_ref[0, rows, :], pooled)
        for n in range(D_MODEL // nw):
            cols = slice(n * nw, (n + 1) * nw)
            merged = None
            for i, br in enumerate(branches):
                gate = jax.nn.sigmoid(
                    jnp.dot(hb, wg_ref[i, :, cols], preferred_element_type=F32) + bg_ref[i, :, cols])
                term = gate * jnp.dot(br, wb_ref[i, :, cols], preferred_element_type=F32)
                merged = term if merged is None else merged + term
            merged_s[rows, cols] = merged.astype(BF16)
        y = jnp.dot(merged_s[rows, :], wo_ref[...], preferred_element_type=F32)
        xn = x + g1 * y
        xo_ref[0, rows, :] = xn
        h2 = _norm_mod(xn, n2_ref[...], sh2, sc2)
        h2_ref[0, rows, :] = _pack_bf16_pairs(h2)
        h_hi = h2.astype(BF16)
        h_lo = (h2 - h_hi.astype(F32)).astype(BF16)
        nt = (((1,), (1,)), ((), ()))
        by_hi = lax.dot_general(rw_ref[...], h_hi, nt, preferred_element_type=F32)
        by_lo = lax.dot_general(rw_ref[:N_EXPERTS, :], h_lo, nt, preferred_element_type=F32)
        logits_t = by_hi[:N_EXPERTS] + by_hi[N_EXPERTS:] + by_lo
        route_ref[0, :, rows] = _route(logits_t, rb_ref[...])


def _merge(tokens, mod, mod_row, n1, n2, branches, offsets, p_in, pool_params, seq, weights, s_out, out_off,
           rows, tm, prev=None):
    pending = len(tokens) == 3
    b, _, d = tokens[0].shape
    wg, bg, wb, wo, rw_t, rb = weights
    pool_w, pool_scale = pool_params
    seq_lo, seq_hi = seq
    n_real = rows // tm
    n_fill = -(-(s_out - out_off - rows) // tm) if prev is None else 0
    step = lambda j: jnp.minimum(j, n_real - 1)
    blk = lambda width, off: pl.BlockSpec((1, tm, width), lambda bi, j: (bi, off // tm + step(j), 0))
    out_blk = lambda width: pl.BlockSpec((1, tm, width), lambda bi, j: (bi, out_off // tm + j, 0))
    assert all(o % tm == 0 for o in offsets) and out_off % tm == 0 and rows % tm == 0 and seq_lo % tm == 0
    hb = tm // POOL_HALO
    last_halo = p_in.shape[1] // POOL_HALO - 1
    halo = lambda shift: pl.BlockSpec(
        (1, POOL_HALO, BRANCH_W),
        lambda bi, j: (bi, jnp.clip((seq_lo // tm + step(j) + shift) * hb - 1 + shift, 0, last_halo), 0))
    out_shape = [
        jax.ShapeDtypeStruct((b, s_out, d), F32),
        jax.ShapeDtypeStruct((b, s_out, d // 2), jnp.int32),
        jax.ShapeDtypeStruct((b, ROUTE_ROWS, s_out), F32),
    ]
    mod_spec = pl.BlockSpec((1, 1, 6 * d), lambda bi, j: (bi if mod_row is None else mod_row, 0, 0))
    tok_off = out_off if pending else 0
    token_specs = [blk(d, tok_off)] + ([blk(d // 2, tok_off), mod_spec] if pending else [])
    n_in = len(tokens) + 17
    extra_specs = [] if prev is None else [pl.BlockSpec(memory_space=pl.ANY)] * 3
    return pl.pallas_call(
        functools.partial(_merge_kernel, nw=512, sub=MERGE_CHAIN_ROWS, n_real=n_real, n_fill=n_fill,
                          pending=pending, p_off=seq_lo, seq_lo=seq_lo, seq_hi=seq_hi),
        scratch_shapes=[pltpu.VMEM((tm, d), BF16)],
        grid=(b, n_real + n_fill),
        in_specs=token_specs + [
            mod_spec,
            _const_spec((1, d)), _const_spec((1, d)),
            *[blk(BRANCH_W, off) for off in offsets],
            blk(BRANCH_W, seq_lo), halo(0), halo(1),
            _const_spec(pool_w.shape), _const_spec((1, BRANCH_W)),
            _const_spec(wg.shape), _const_spec(bg.shape), _const_spec(wb.shape), _const_spec(wo.shape),
            _const_spec(rw_t.shape), _const_spec(rb.shape),
        ] + extra_specs,
        out_specs=[
            out_blk(d),
            out_blk(d // 2),
            pl.BlockSpec((1, ROUTE_ROWS, tm), lambda bi, j: (bi, 0, out_off // tm + j)),
        ],
        out_shape=out_shape,
        input_output_aliases={} if prev is None else {n_in + i: i for i in range(3)},
        compiler_params=_cparams(("parallel", "arbitrary"), 56),
        name="merge",
    )(*tokens, mod, n1, n2, *branches, p_in, p_in, p_in, pool_w, pool_scale,
      wg, bg, wb, wo, rw_t, rb, *(() if prev is None else prev))


def _rank_kernel(cls_ref, rank_ref, cnt_ref, cnt_s, *, tr):
    @pl.when(pl.program_id(0) == 0)
    def _():
        cnt_s[...] = jnp.zeros_like(cnt_s)

    cls = cls_ref[0]
    cid = lax.broadcasted_iota(jnp.int32, (CLASS_ROWS, tr), 0).astype(F32)
    onehot = cid == cls
    before = lax.broadcasted_iota(jnp.int32, (tr, tr), 0) < lax.broadcasted_iota(jnp.int32, (tr, tr), 1)
    prefix = jnp.dot(jnp.where(onehot, 1.0, 0.0).astype(BF16), jnp.where(before, 1.0, 0.0).astype(BF16),
                     preferred_element_type=F32)
    carry = cnt_s[...][:, 0:1]
    rank_ref[0] = jnp.sum(jnp.where(onehot, prefix + carry, 0.0), axis=0, keepdims=True)
    cnt_s[...] += jnp.sum(jnp.where(onehot, 1.0, 0.0), axis=1, keepdims=True)
    cnt_ref[...] = cnt_s[...]


def _rank(cls_flat, tr=512):
    n = cls_flat.shape[0]
    tr = math.gcd(n, tr)
    cls3 = cls_flat.reshape(n // tr, 1, tr)
    rank, cnt = pl.pallas_call(
        functools.partial(_rank_kernel, tr=tr),
        grid=(n // tr,),
        in_specs=[pl.BlockSpec((1, 1, tr), lambda i: (i, 0, 0))],
        out_specs=[pl.BlockSpec((1, 1, tr), lambda i: (i, 0, 0)), _const_spec((CLASS_ROWS, LANES))],
        out_shape=[jax.ShapeDtypeStruct((n // tr, 1, tr), F32), jax.ShapeDtypeStruct((CLASS_ROWS, LANES), F32)],
        scratch_shapes=[pltpu.VMEM((CLASS_ROWS, LANES), F32)],
        compiler_params=_cparams(("arbitrary",), 32),
        name="rank",
    )(cls3)
    return rank.reshape(n), cnt[:N_CLASSES, 0]


def _sc_layout(n):
    info = plsc.get_sparse_core_info()
    nw = info.num_cores * info.num_subcores
    per_worker = n // nw
    assert per_worker * nw == n
    chunk = max(c for c in range(8, SC_MAX_CHUNK + 1, 8) if per_worker % c == 0)
    return info.num_cores, nw, per_worker // chunk, chunk


def _sc_scatter_rows(src, pos, n_out):
    n, w = src.shape
    nc, nw, k, c = _sc_layout(n)
    mesh = plsc.VectorSubcoreMesh(core_axis_name="c", subcore_axis_name="s")

    @functools.partial(
        pl.kernel, mesh=mesh,
        out_type=jax.ShapeDtypeStruct((n_out, w), src.dtype),
        scratch_types=[pltpu.VMEM((k, c), jnp.int32), pltpu.VMEM((2, c, w), src.dtype),
                       pltpu.SemaphoreType.DMA((2,)), pltpu.SemaphoreType.DMA((2,))],
        name="moe_scatter",
    )
    def scatter(src_hbm, pos_hbm, out_hbm, idx_v, rows_v, sem_in, sem_out):
        wid = lax.axis_index("s") * nc + lax.axis_index("c")
        pltpu.sync_copy(pos_hbm.at[wid], idx_v)
        read = lambda j: pltpu.async_copy(
            src_hbm.at[pl.ds(pl.multiple_of(wid * (k * c) + j * c, 8), c)], rows_v.at[j % 2], sem_in.at[j % 2])
        reads, writes = [read(0)], []
        for j in range(k):
            reads[j].wait()
            if j >= 1:
                writes[j - 1].wait()
            if j + 1 < k:
                reads.append(read(j + 1))
            writes.append(pltpu.async_copy(rows_v.at[j % 2], out_hbm.at[idx_v.at[j]], sem_out.at[j % 2]))
        writes[k - 1].wait()

    return scatter(src, pos.reshape(nw, k, c))


def _sc_gather_rows(src, pos):
    n = pos.shape[0]
    w = src.shape[1]
    nc, nw, k, c = _sc_layout(n)
    mesh = plsc.VectorSubcoreMesh(core_axis_name="c", subcore_axis_name="s")

    @functools.partial(
        pl.kernel, mesh=mesh,
        out_type=jax.ShapeDtypeStruct((n, w), src.dtype),
        scratch_types=[pltpu.VMEM((k, c), jnp.int32), pltpu.VMEM((2, c, w), src.dtype),
                       pltpu.SemaphoreType.DMA((2,)), pltpu.SemaphoreType.DMA((2,))],
        name="moe_gather",
    )
    def gather(src_hbm, pos_hbm, out_hbm, idx_v, rows_v, sem_in, sem_out):
        wid = lax.axis_index("s") * nc + lax.axis_index("c")
        pltpu.sync_copy(pos_hbm.at[wid], idx_v)
        read = lambda j: pltpu.async_copy(src_hbm.at[idx_v.at[j]], rows_v.at[j % 2], sem_in.at[j % 2])
        reads, writes = [read(0)], []
        for j in range(k):
            reads[j].wait()
            if j >= 1:
                writes[j - 1].wait()
            if j + 1 < k:
                reads.append(read(j + 1))
            off = pl.multiple_of(wid * (k * c) + j * c, 8)
            writes.append(pltpu.async_copy(rows_v.at[j % 2], out_hbm.at[pl.ds(off, c)], sem_out.at[j % 2]))
        writes[k - 1].wait()

    return gather(src, pos.reshape(nw, k, c))


def _gmm_kernel(lo_ref, hi_ref, new_ref, nact_ref, h_ref, rw_ref,
                w1a_ref, w1b_ref, w3a_ref, w3b_ref, w2a_ref, w2b_ref, o_ref, w13_s, w2_s):
    t = pl.program_id(0)

    @pl.when(new_ref[t] == 1)
    def _():
        for i, (w1_ref, w3_ref, w2_ref) in enumerate(((w1a_ref, w3a_ref, w2a_ref), (w1b_ref, w3b_ref, w2b_ref))):
            w13_s[i, :, :EXPERT_FF] = w1_ref[0, 0].astype(BF16)
            w13_s[i, :, EXPERT_FF:] = w3_ref[0, 0].astype(BF16)
            w2_s[i] = w2_ref[0, 0].astype(BF16)

    @pl.when(t < nact_ref[0])
    def _():
        xf = _unpack_bf16_pairs(h_ref[...])
        x = xf.astype(BF16)

        def expert(i):
            ab = jnp.dot(x, w13_s[i], preferred_element_type=F32)
            a, gate = ab[:, :EXPERT_FF], ab[:, EXPERT_FF:]
            hid = (a * jax.nn.sigmoid(a)) * gate
            return jnp.dot(hid.astype(BF16), w2_s[i], preferred_element_type=F32)

        aff = [jax.nn.sigmoid(jnp.sum(xf * rw_ref[pl.ds(e_ref[t], 1), :], axis=-1, keepdims=True))
               for e_ref in (lo_ref, hi_ref)]
        total = aff[0] + aff[1]
        y = (aff[0] / total) * expert(0) + (aff[1] / total) * expert(1)
        o_ref[...] = _pack_bf16_pairs(y)


def _gmm(tile_lo, tile_hi, tile_new, n_act, hs, rw_t, w1, w3, w2, layer, tm):
    n_pad, half = hs.shape
    d = 2 * half
    row = lambda t, lo, hi, new, na: (jnp.minimum(t, na[0] - 1), 0)
    e_lo = lambda t, lo, hi, new, na: (layer, lo[jnp.minimum(t, na[0] - 1)], 0, 0)
    e_hi = lambda t, lo, hi, new, na: (layer, hi[jnp.minimum(t, na[0] - 1)], 0, 0)
    up = lambda e: pl.BlockSpec((1, 1, d, EXPERT_FF), e)
    down = lambda e: pl.BlockSpec((1, 1, EXPERT_FF, d), e)
    return pl.pallas_call(
        _gmm_kernel,
        grid_spec=pltpu.PrefetchScalarGridSpec(
            num_scalar_prefetch=4,
            grid=(n_pad // tm,),
            in_specs=[
                pl.BlockSpec((tm, half), row),
                pl.BlockSpec(rw_t.shape, lambda t, lo, hi, new, na: (0, 0)),
                up(e_lo), up(e_hi), up(e_lo), up(e_hi), down(e_lo), down(e_hi),
            ],
            out_specs=pl.BlockSpec((tm, half), row),
            scratch_shapes=[pltpu.VMEM((2, d, 2 * EXPERT_FF), BF16), pltpu.VMEM((2, EXPERT_FF, d), BF16)],
        ),
        out_shape=jax.ShapeDtypeStruct((n_pad, half), jnp.int32),
        compiler_params=_cparams(("arbitrary",), 56),
        name="moe_gmm",
    )(tile_lo, tile_hi, tile_new, n_act, hs, rw_t, w1, w1, w3, w3, w2, w2)


def _moe_routed(h2p, route, rw_t, w1, w3, w2, layer, tm=256):
    b, s, half = h2p.shape
    n = b * s
    n_pad = n + N_CLASSES * tm
    cls = route[:, 0, :].reshape(n)
    rank, counts = _rank(cls)
    counts = counts.astype(jnp.int32)
    padded = (counts + tm - 1) // tm * tm
    ends = jnp.cumsum(padded)
    pos = jnp.take(ends - padded, cls.astype(jnp.int32)) + rank.astype(jnp.int32)
    n_act = (ends[-1] // tm).reshape(1)
    tile_row = jnp.arange(n_pad // tm, dtype=jnp.int32) * tm
    tile_cls = jnp.minimum(jnp.sum(tile_row[:, None] >= ends[None, :], axis=1), N_CLASSES - 1)
    pair_lo, pair_hi = (jnp.asarray(a, jnp.int32) for a in _class_experts())
    hs = _sc_scatter_rows(h2p.reshape(n, half), pos, n_pad)
    prev_cls = jnp.concatenate([jnp.full((1,), -1, tile_cls.dtype), tile_cls[:-1]])
    tile_new = ((tile_cls != prev_cls) & (tile_row < ends[-1])).astype(jnp.int32)
    ys = _gmm(jnp.take(pair_lo, tile_cls), jnp.take(pair_hi, tile_cls), tile_new, n_act, hs, rw_t,
              w1, w3, w2, layer, tm)
    return _sc_gather_rows(ys, pos).reshape(b, s, half)


def _class_experts():
    lo, hi = [], []
    for g in range(N_EXPERTS // EXPERTS_PER_GROUP):
        for i in range(EXPERTS_PER_GROUP):
            for j in range(i + 1, EXPERTS_PER_GROUP):
                lo.append(EXPERTS_PER_GROUP * g + i)
                hi.append(EXPERTS_PER_GROUP * g + j)
    return np.array(lo), np.array(hi)


def _final_residual_kernel(x_ref, y_ref, mod_ref, g_ref, o_ref):
    x = _pending_residual(x_ref[0], y_ref, mod_ref)
    o_ref[0] = x * lax.rsqrt(jnp.mean(x * x, axis=-1, keepdims=True) + EPS) * g_ref[...]


def _final_residual(x1, yp, mod, gain, tm=512):
    b, t, d = x1.shape
    tok = lambda bi, j: (bi, j, 0)
    return pl.pallas_call(
        _final_residual_kernel,
        grid=(b, t // tm),
        in_specs=[
            pl.BlockSpec((1, tm, d), tok),
            pl.BlockSpec((1, tm, d // 2), tok),
            pl.BlockSpec((1, 1, 6 * d), lambda bi, j: (bi, 0, 0)),
            _const_spec((1, d)),
        ],
        out_specs=pl.BlockSpec((1, tm, d), tok),
        out_shape=jax.ShapeDtypeStruct((b, t, d), F32),
        compiler_params=_cparams(("parallel", "arbitrary"), 32),
        name="final_residual",
    )(x1, yp, mod, gain)


ROPE_FREQS = ROPE_AXIS_DIM // 2


def _rope_tables(t_lat, n_ctx):
    rows = t_lat // GRID_W
    row = jnp.repeat(jnp.arange(rows, dtype=F32), GRID_W)
    col = jnp.tile(jnp.arange(GRID_W, dtype=F32), rows)
    inv_freq = ROPE_THETA ** (-jnp.arange(0, ROPE_AXIS_DIM, 2, dtype=F32) / ROPE_AXIS_DIM)
    ang = jnp.stack([row[:, None] * inv_freq, col[:, None] * inv_freq], axis=1)
    lanes = (2, LANES // HEAD_DIM, 2, ROPE_FREQS)
    sign = jnp.array([-1.0, 1.0], F32).reshape(1, 2, 1, 1, 1)
    cos = jnp.broadcast_to(jnp.cos(ang)[:, None, None], (t_lat,) + lanes).reshape(t_lat, LANES)
    sin = jnp.broadcast_to(sign * jnp.sin(ang)[:, None, None], (t_lat,) + lanes).reshape(t_lat, LANES)
    pad = lambda tbl, fill: jnp.concatenate([tbl, jnp.full((n_ctx, LANES), fill, F32)], axis=0)
    return pad(cos, 1.0), pad(sin, 0.0)


def _qk_lanes(w, n_heads):
    lead = w.shape[:-1]
    n_chunks = n_heads // N_KV_HEADS
    w = w.reshape(lead + (N_KV_HEADS, n_chunks, 2, 2, ROPE_FREQS))
    nl = len(lead)
    w = jnp.transpose(w, tuple(range(nl)) + (nl + 1, nl + 3, nl, nl + 2, nl + 4))
    return w.reshape(lead + (n_heads * HEAD_DIM,))


def _permute_heads(w, axis):
    shp = w.shape
    w = w.reshape(shp[:axis] + (N_Q_HEADS, HEAD_DIM) + shp[axis + 1:])
    w = jnp.take(w, jnp.array(HEAD_PERM), axis=axis)
    return w.reshape(shp)


def _square_factor(n):
    r = int(round(math.sqrt(n)))
    assert r * r == n, "sequence lengths must be perfect squares for the two-stage DFT"
    return r


def kernel(x, c, ctx, c_ctx, w_ada, b_ada, norm1, norm2, w_in, q_gain, k_gain, sink, pool_w, pool_scale,
           w_branch, w_gate, b_gate, w_out, router_w, router_bias, w1, w3, w2, norm_f):
    b, t_lat, d = x.shape
    n_ctx = ctx.shape[1]
    s = t_lat + n_ctx
    depth = w_ada.shape[0]
    assert d == D_MODEL and b < MOD_ROWS and t_lat % 256 == 0 and n_ctx % 256 == 0 and t_lat % n_ctx == 0

    tokens = (x, ctx)
    cc = jnp.zeros((MOD_ROWS, d), F32).at[:b].set(c).at[b].set(c_ctx)
    mod_all = _ada(cc, w_ada, b_ada).reshape(depth, MOD_ROWS, 1, 6 * d)

    cos, sin = _rope_tables(t_lat, n_ctx)
    lane_head = (np.arange(BRANCH_W) // LANES) * 2 + (np.arange(BRANCH_W) // (HEAD_DIM // 2)) % 2
    seg = jnp.asarray((lane_head[:, None] == lane_head[None, :]) / HEAD_DIM, BF16)
    cs = jnp.asarray(_channel_dft_table()).astype(BF16)
    f_lat = [jnp.asarray(a).astype(BF16) for a in _fourier_tables(*(_square_factor(t_lat),) * 2)]
    f_ctx = [jnp.asarray(a).astype(BF16) for a in _fourier_tables(*(_square_factor(n_ctx),) * 2)]
    wbias = jnp.asarray(_window_bias(n_ctx))
    rw_hi = router_w.T.astype(BF16)
    rw_t = jnp.concatenate([rw_hi, (router_w.T - rw_hi.astype(F32)).astype(BF16)], axis=0)
    rb = router_bias.reshape(N_EXPERTS, 1)

    for l in range(depth):
        need_ctx = l < depth - 1
        s_out = s if need_ctx else t_lat
        cols = jnp.split(w_in[l], np.cumsum((512, 512, 512, 128, 128, 512, 128))[:], axis=1)
        f_w, p_w, qb_w, kb_w, vb_w, qw_w, kw_w, vw_w = cols
        w_in_l = jnp.concatenate([_qk_lanes(qb_w, N_Q_HEADS), _qk_lanes(qw_w, N_Q_HEADS),
                                  _qk_lanes(kb_w, N_KV_HEADS), _qk_lanes(kw_w, N_KV_HEADS),
                                  f_w, vb_w, vw_w, p_w], axis=1).astype(BF16)
        wb_l = jnp.stack([w_branch[l, 0], _permute_heads(w_branch[l, 1], 0),
                          _permute_heads(w_branch[l, 2], 0), w_branch[l, 3]]).astype(BF16)
        mod = mod_all[l]
        n1 = norm1[l].reshape(1, d)
        n2 = norm2[l].reshape(1, d)
        qg = _qk_lanes(jnp.tile(q_gain[l], N_Q_HEADS), N_Q_HEADS).reshape(1, BRANCH_W)
        kg = _qk_lanes(jnp.tile(k_gain[l], N_KV_HEADS), N_KV_HEADS).reshape(1, LANES)

        zr, zi, p_in, qb, qw, kvb, kvw = _inproj(tokens, t_lat, mod, n1, w_in_l, qg, kg, seg, cs, cos, sin)

        out_a = _fourier(zr, zi, f_lat[0], f_lat[1], t_lat, 0)
        out_b = _gattn(qb, kvb, 0, t_lat, 0, s, tq=512)
        out_c = _wattn(jnp.take(sink[l], jnp.array(HEAD_PERM)) * LOG2E, wbias, qw, kvw, s_out, t_lat)

        weights = (w_gate[l].astype(BF16), b_gate[l].reshape(4, 1, d), wb_l, w_out[l].astype(BF16), rw_t, rb)
        pool_params = (pool_w[l].astype(BF16), pool_scale[l].reshape(1, BRANCH_W))
        pending = len(tokens) == 3
        merged = _merge(tokens if pending else tokens[:1], mod, None, n1, n2, (out_a, out_b, out_c), (0, 0, 0),
                        p_in, pool_params, (0, t_lat), weights, s_out, 0, t_lat, 2 * MERGE_CHAIN_ROWS)
        if need_ctx:
            out_ac = _fourier(zr, zi, f_ctx[0], f_ctx[1], n_ctx, t_lat // n_ctx)
            out_bc = _gattn(qb, kvb, t_lat, n_ctx, t_lat, n_ctx)
            merged = _merge(tokens if pending else tokens[1:], mod, b, n1, n2, (out_ac, out_bc, out_c),
                            (0, 0, t_lat), p_in, pool_params, (t_lat, s), weights, s_out, t_lat, n_ctx,
                            MERGE_CHAIN_ROWS, prev=merged)
        x1, h2p, route = merged
        yp = _moe_routed(h2p, route, router_w.T, w1, w3, w2, l)
        tokens = (x1, yp, mod)

    return _final_residual(*tokens, norm_f.reshape(1, d))
```

```python
import functools
import math

import numpy as np
import jax
import jax.numpy as jnp
from jax import lax
from jax.experimental import pallas as pl
from jax.experimental.pallas import tpu as pltpu
from jax.experimental.pallas import tpu_sc as plsc

F32 = jnp.float32
BF16 = jnp.bfloat16

D_MODEL = 1024
HEAD_DIM = 64
N_Q_HEADS = 8
N_KV_HEADS = 2
GRID_W = 64
ROPE_THETA = 10000.0
ROPE_AXIS_DIM = HEAD_DIM // 2
QBLK = 128
WINDOW = 128
BRANCH_W = 512
GROUP_W = 128
POOL_WINDOWS = (2, 4, 8, 16)
N_EXPERTS = 16
EXPERTS_PER_GROUP = 4
EXPERT_FF = 512
EPS = 1e-6
MOD_ROWS = 16
NEG_BIG = -1e30
LOG2E = math.log2(math.e)
LANES = 128
POOL_HALO = 16
PAIRS_PER_GROUP = 6
N_CLASSES = 24
CLASS_ROWS = 32
ROUTE_ROWS = 8
SC_MAX_CHUNK = 128
WATTN_QBLOCKS = 2
MERGE_CHAIN_ROWS = 256
FOURIER_SLAB_PAD = 4

HEAD_PERM = (0, 4, 1, 5, 2, 6, 3, 7)


def _cparams(sem, vmem_mb):
    return pltpu.CompilerParams(dimension_semantics=sem, vmem_limit_bytes=vmem_mb * 1024 * 1024)


def _const_spec(shape):
    nd = len(shape)
    return pl.BlockSpec(shape, lambda *_: (0,) * nd)


def _ada_kernel(c_ref, w_ref, b_ref, o_ref):
    c = c_ref[...]
    s = c * jax.nn.sigmoid(c)
    o_ref[0] = jnp.dot(s.astype(BF16), w_ref[0].astype(BF16), preferred_element_type=F32) + b_ref[0]


def _ada(cc, w_ada, b_ada):
    depth, d, n = w_ada.shape
    tn = 1536
    return pl.pallas_call(
        _ada_kernel,
        grid=(depth, n // tn),
        in_specs=[
            pl.BlockSpec((MOD_ROWS, d), lambda l, j: (0, 0)),
            pl.BlockSpec((1, d, tn), lambda l, j: (l, 0, j)),
            pl.BlockSpec((1, 1, tn), lambda l, j: (l, 0, j)),
        ],
        out_specs=pl.BlockSpec((1, MOD_ROWS, tn), lambda l, j: (l, 0, j)),
        out_shape=jax.ShapeDtypeStruct((depth, MOD_ROWS, n), F32),
        compiler_params=_cparams(("arbitrary", "arbitrary"), 40),
        name="ada",
    )(cc, w_ada, b_ada.reshape(depth, 1, n))


def _norm_mod(x, gain, shift, scale):
    ms = jnp.mean(x * x, axis=-1, keepdims=True)
    return (x * lax.rsqrt(ms + EPS) * gain) * (1.0 + scale) + shift


def _mod_slices(m):
    d = D_MODEL
    return [m[:, i * d:(i + 1) * d] for i in range(6)]


def _head_norm(z, seg, gain):
    ms = jnp.dot((z * z).astype(BF16), seg, preferred_element_type=F32)
    return z * lax.rsqrt(ms + EPS) * gain


def _rope(z, cos, sin):
    outs = []
    for c in range(z.shape[1] // LANES):
        zc = z[:, c * LANES:(c + 1) * LANES]
        outs.append(zc * cos + pltpu.roll(zc, LANES // 2, 1) * sin)
    return outs[0] if len(outs) == 1 else jnp.concatenate(outs, axis=-1)


def _stream_specs(tm, d, n_lat):
    return [pl.BlockSpec((1, tm, d), lambda bi, j: (bi, jnp.minimum(j, n_lat - 1), 0)),
            pl.BlockSpec((1, tm, d), lambda bi, j: (bi, jnp.maximum(j - n_lat, 0), 0))]


def _pending_residual(x, y_ref, modp_ref, rows=slice(None)):
    return x + _mod_slices(modp_ref[0])[5] * _unpack_bf16_pairs(y_ref[0, rows, :])


def _inproj_kernel(*refs, n_lat, pending):
    if pending:
        x_ref, y_ref, modp_ref = refs[:3]
        x = _pending_residual(x_ref[0], y_ref, modp_ref)
    else:
        xl_ref, xc_ref = refs[:2]
        x = jnp.where(pl.program_id(1) >= n_lat, xc_ref[0], xl_ref[0])
    (mod_ref, n1_ref, w_ref, qg_ref, kg_ref, seg_ref, cs_ref, cos_ref, sin_ref,
     zr_ref, zi_ref, p_ref, qb_ref, qw_ref, kvb_ref, kvw_ref) = refs[3 if pending else 2:]
    sh1, sc1 = _mod_slices(mod_ref[0])[:2]
    h = _norm_mod(x, n1_ref[...], sh1, sc1)
    u = jnp.dot(h.astype(BF16), w_ref[...], preferred_element_type=F32)
    cos, sin = cos_ref[...], sin_ref[...]
    w = BRANCH_W
    seg = seg_ref[...]
    qb = _rope(_head_norm(u[:, 0:w], seg, qg_ref[...]), cos, sin)
    qb_ref[0] = (qb * (HEAD_DIM ** -0.5 * LOG2E)).astype(BF16)
    qw = _rope(u[:, w:2 * w], cos, sin)
    qw_ref[0] = (qw * (HEAD_DIM ** -0.5 * LOG2E)).astype(BF16)
    o = 2 * w
    kb = _rope(_head_norm(u[:, o:o + LANES], seg[:LANES, :LANES], kg_ref[...]), cos, sin)
    kw = _rope(u[:, o + LANES:o + 2 * LANES], cos, sin)
    o += 2 * LANES
    f_in = u[:, o:o + w].astype(BF16)
    zr, zi = [], []
    for g in range(w // GROUP_W):
        z = jnp.dot(f_in[:, g * GROUP_W:(g + 1) * GROUP_W], cs_ref[...], preferred_element_type=F32)
        zr.append(z[:, :GROUP_W])
        zi.append(z[:, GROUP_W:])
    zr_ref[0] = jnp.concatenate(zr, axis=-1).astype(BF16)
    zi_ref[0] = jnp.concatenate(zi, axis=-1).astype(BF16)
    o += w
    vb = u[:, o:o + LANES]
    vw = u[:, o + LANES:o + 2 * LANES]
    kvb_ref[0] = jnp.concatenate([kb, vb, jnp.ones_like(vb)], axis=-1).astype(BF16)
    kvw_ref[0] = jnp.concatenate([kw, vw, jnp.ones_like(vw)], axis=-1).astype(BF16)
    p_ref[0] = u[:, o + 2 * LANES:o + 2 * LANES + w].astype(BF16)


def _inproj(tokens, t_lat, mod, n1, w_in, qg, kg, seg, cs, cos, sin, tm=256):
    pending = len(tokens) == 3
    b, _, d = tokens[0].shape
    s = tokens[0].shape[1] if pending else t_lat + tokens[1].shape[1]
    nw = w_in.shape[1]
    n_lat = t_lat // tm
    tok = lambda bi, j: (bi, j, 0)
    tab = lambda bi, j: (j, 0)
    mod_spec = pl.BlockSpec((1, 1, 6 * d), lambda bi, j: (jnp.where(j >= n_lat, b, bi), 0, 0))
    if pending:
        token_specs = [pl.BlockSpec((1, tm, d), tok), pl.BlockSpec((1, tm, d // 2), tok), mod_spec]
    else:
        token_specs = _stream_specs(tm, d, n_lat)
    widths = (BRANCH_W,) * 5 + (3 * LANES, 3 * LANES)
    return pl.pallas_call(
        functools.partial(_inproj_kernel, n_lat=n_lat, pending=pending),
        grid=(b, s // tm),
        in_specs=token_specs + [
            mod_spec,
            _const_spec((1, d)),
            _const_spec((d, nw)),
            _const_spec((1, BRANCH_W)),
            _const_spec((1, LANES)),
            _const_spec((BRANCH_W, BRANCH_W)),
            _const_spec((GROUP_W, 2 * GROUP_W)),
            pl.BlockSpec((tm, LANES), tab),
            pl.BlockSpec((tm, LANES), tab),
        ],
        out_specs=[pl.BlockSpec((1, tm, wd), tok) for wd in widths],
        out_shape=[jax.ShapeDtypeStruct((b, s, wd), BF16) for wd in widths],
        compiler_params=_cparams(("parallel", "arbitrary"), 48),
        name="inproj",
    )(*tokens, mod, n1, w_in, qg, kg, seg, cs, cos, sin)


def _split_heads(qc, lane):
    first = (lane & (HEAD_DIM // 2)) == 0
    zero = jnp.zeros_like(qc)
    return jnp.concatenate([jnp.where(first, qc, zero), jnp.where(first, zero, qc)], axis=0)


def _gattn_kernel(q_ref, kv_ref, o_ref, q2_s, acc_s, *, sub):
    tq = q_ref.shape[1]
    lane = lax.broadcasted_iota(jnp.int32, (1, LANES), 1)
    nt = (((1,), (1,)), ((), ()))
    n_chunks = BRANCH_W // LANES
    for c in range(n_chunks):
        q2_s[2 * c * tq:(2 * c + 2) * tq, :] = _split_heads(q_ref[0, :, c * LANES:(c + 1) * LANES], lane)
    k = kv_ref[0, :, 0:LANES]
    v = kv_ref[0, :, LANES:3 * LANES]
    for r in range(2 * n_chunks * tq // sub):
        rows = slice(r * sub, (r + 1) * sub)
        s = lax.dot_general(q2_s[rows, :], k, nt, preferred_element_type=F32)
        p = jnp.exp2(s - jnp.max(s, axis=-1, keepdims=True))
        acc_s[rows, :] = jnp.dot(p.astype(BF16), v, preferred_element_type=F32)
    for c in range(n_chunks):
        lo = acc_s[2 * c * tq:(2 * c + 1) * tq, :]
        hi = acc_s[(2 * c + 1) * tq:(2 * c + 2) * tq, :]
        o = jnp.where(lane < HEAD_DIM, lo[:, :LANES] / lo[:, LANES:], hi[:, :LANES] / hi[:, LANES:])
        o_ref[0, :, c * LANES:(c + 1) * LANES] = o.astype(BF16)


def _gattn(qb, kv, q_start, q_len, k_start, k_len, tq=256, sub=128):
    b = qb.shape[0]
    assert q_start % tq == 0 and q_len % tq == 0 and k_start % k_len == 0 and k_len % LANES == 0
    rows = 2 * tq * (BRANCH_W // LANES)
    return pl.pallas_call(
        functools.partial(_gattn_kernel, sub=sub),
        grid=(b, q_len // tq),
        in_specs=[
            pl.BlockSpec((1, tq, BRANCH_W), lambda bi, j: (bi, q_start // tq + j, 0)),
            pl.BlockSpec((1, k_len, 3 * LANES), lambda bi, j: (bi, k_start // k_len, 0)),
        ],
        out_specs=pl.BlockSpec((1, tq, BRANCH_W), lambda bi, j: (bi, j, 0)),
        out_shape=jax.ShapeDtypeStruct((b, q_len, BRANCH_W), BF16),
        scratch_shapes=[pltpu.VMEM((rows, LANES), BF16), pltpu.VMEM((rows, 2 * LANES), F32)],
        compiler_params=_cparams(("parallel", "arbitrary"), 48),
        name="gattn",
    )(qb, kv)


def _window_bias(n_ctx):
    tq = WATTN_QBLOCKS * QBLK
    qi = np.arange(tq)[:, None]
    kj = np.arange(tq + 2 * QBLK)[None, :]
    band = np.abs(kj - WINDOW - qi) <= WINDOW
    blk = kj // QBLK
    variants = [band & (blk != 0), band, band & (blk != WATTN_QBLOCKS + 1), np.zeros_like(band)]
    out = [np.concatenate([np.ones((tq, n_ctx), bool), v], axis=1) for v in variants]
    return np.where(np.stack(out), 0.0, NEG_BIG).astype(np.float32)


def _wattn_kernel(sink_ref, bias_ref, q_ref, *refs, sub):
    o_ref, q2_s, kv_s, acc_s = refs[-4:]
    tq = q_ref.shape[1]
    off = 0
    for blk in refs[:-4]:
        kv_s[off:off + blk.shape[1], :] = blk[0]
        off += blk.shape[1]
    lane = lax.broadcasted_iota(jnp.int32, (1, LANES), 1)
    nt = (((1,), (1,)), ((), ()))
    n_chunks = BRANCH_W // LANES
    for c in range(n_chunks):
        q2_s[2 * c * tq:(2 * c + 2) * tq, :] = _split_heads(q_ref[0, :, c * LANES:(c + 1) * LANES], lane)
    k, v = kv_s[:, 0:LANES], kv_s[:, LANES:3 * LANES]
    for r in range(2 * n_chunks * tq // sub):
        rows = slice(r * sub, (r + 1) * sub)
        q_off = (r * sub) % tq
        sk = sink_ref[(r * sub) // tq]
        s = lax.dot_general(q2_s[rows, :], k, nt, preferred_element_type=F32) + bias_ref[0, q_off:q_off + sub, :]
        m = jnp.maximum(jnp.max(s, axis=-1, keepdims=True), sk)
        pv = jnp.dot(jnp.exp2(s - m).astype(BF16), v, preferred_element_type=F32)
        acc_s[rows, :LANES] = pv[:, :LANES]
        acc_s[rows, LANES:] = pv[:, LANES:] + jnp.exp2(sk - m)
    for c in range(n_chunks):
        lo = acc_s[2 * c * tq:(2 * c + 1) * tq, :]
        hi = acc_s[(2 * c + 1) * tq:(2 * c + 2) * tq, :]
        o = jnp.where(lane < HEAD_DIM, lo[:, :LANES] / lo[:, LANES:], hi[:, :LANES] / hi[:, LANES:])
        o_ref[0, :, c * LANES:(c + 1) * LANES] = o.astype(BF16)


def _wattn(sink, bias, qw, kv, s_out, t_lat):
    b, s, _ = qw.shape
    n_ctx = s - t_lat
    nq = WATTN_QBLOCKS
    tq = nq * QBLK
    assert n_ctx % tq == 0 and t_lat // tq >= 2
    last = s // QBLK - 1
    n_lat = t_lat // tq
    variant = lambda j: jnp.where(j >= n_lat, 3, jnp.where(j == 0, 0, jnp.where(j == n_lat - 1, 2, 1)))
    key_block = lambda off: pl.BlockSpec(
        (1, QBLK, 3 * LANES), lambda bi, j: (bi, jnp.clip(j * nq + off, 0, last), 0))
    rows = 2 * tq * (BRANCH_W // LANES)
    return pl.pallas_call(
        functools.partial(_wattn_kernel, sub=128),
        scratch_shapes=[pltpu.VMEM((rows, LANES), BF16), pltpu.VMEM((bias.shape[2], 3 * LANES), BF16),
                        pltpu.VMEM((rows, 2 * LANES), F32)],
        grid=(b, s_out // tq),
        in_specs=[
            pl.BlockSpec(memory_space=pltpu.SMEM),
            pl.BlockSpec((1,) + bias.shape[1:], lambda bi, j: (variant(j), 0, 0)),
            pl.BlockSpec((1, tq, BRANCH_W), lambda bi, j: (bi, j, 0)),
            pl.BlockSpec((1, n_ctx, 3 * LANES), lambda bi, j: (bi, t_lat // n_ctx, 0)),
        ] + [key_block(off) for off in range(-1, nq + 1)],
        out_specs=pl.BlockSpec((1, tq, BRANCH_W), lambda bi, j: (bi, j, 0)),
        out_shape=jax.ShapeDtypeStruct((b, s_out, BRANCH_W), BF16),
        compiler_params=_cparams(("parallel", "arbitrary"), 32),
        name="wattn",
    )(sink, bias, qw, *([kv] * (nq + 3)))


def _fourier_tables(n1, n2):
    t = n1 * n2
    k2 = np.arange(n2)[None, :, None]
    t2 = np.arange(n2)[None, None, :]
    t1 = np.arange(n1)[:, None, None]
    theta = 2.0 * np.pi * ((k2 * t2 * n1 + k2 * t1) % t) / t
    er, ei = np.cos(theta) / math.sqrt(n2), -np.sin(theta) / math.sqrt(n2)
    e = np.concatenate([np.concatenate([er, -ei], axis=2), np.concatenate([ei, er], axis=2)], axis=1)
    k1 = np.arange(n1)[:, None]
    phi = 2.0 * np.pi * ((k1 * np.arange(n1)[None, :]) % n1) / n1
    dcat = np.concatenate([np.cos(phi), np.sin(phi)], axis=1) / math.sqrt(n1)
    return e.astype(np.float32), dcat.astype(np.float32)


def _channel_dft_table():
    c = np.arange(GROUP_W)
    ang = 2.0 * np.pi * ((c[:, None] * c[None, :]) % GROUP_W) / GROUP_W
    return (np.concatenate([np.cos(ang), -np.sin(ang)], axis=1) / math.sqrt(GROUP_W)).astype(np.float32)


def _fourier_kernel(zr_ref, zi_ref, e_ref, d_ref, o_ref, xr_s, xi_s, yr_s, yi_s, *, n):
    nc = xr_s.shape[0]
    pitch = n + FOURIER_SLAB_PAD
    chunk = lambda c: slice(c * LANES, (c + 1) * LANES)
    slab = lambda i: slice(i * pitch, i * pitch + n)

    def gather(ref, start):
        return jnp.concatenate([ref[c, pl.ds(start, n, stride=pitch), :] for c in range(nc)], axis=-1)

    for c in range(nc):
        for t2 in range(n):
            xr_s[c, slab(t2), :] = zr_ref[0, t2 * n:(t2 + 1) * n, chunk(c)].astype(F32)
            xi_s[c, slab(t2), :] = zi_ref[0, t2 * n:(t2 + 1) * n, chunk(c)].astype(F32)
    for t1 in range(n):
        xs = jnp.concatenate([gather(xr_s, t1), gather(xi_s, t1)], axis=0).astype(BF16)
        y = jnp.dot(e_ref[t1], xs, preferred_element_type=F32)
        for c in range(nc):
            yr_s[c, slab(t1), :] = y[:n, chunk(c)]
            yi_s[c, slab(t1), :] = y[n:, chunk(c)]
    for k2 in range(n):
        ys = jnp.concatenate([gather(yr_s, k2), gather(yi_s, k2)], axis=0).astype(BF16)
        o = jnp.dot(d_ref[...], ys, preferred_element_type=F32)
        for c in range(nc):
            xr_s[c, pl.ds(k2, n, stride=pitch), :] = o[:, chunk(c)]
    for c in range(nc):
        for k1 in range(n):
            o_ref[0, k1 * n:(k1 + 1) * n, chunk(c)] = xr_s[c, slab(k1), :].astype(BF16)


def _fourier(zr, zi, e_tab, d_tab, t_len, row_block, cw=256):
    b = zr.shape[0]
    n = d_tab.shape[0]
    assert n * n == t_len
    zspec = pl.BlockSpec((1, t_len, cw), lambda bi, j: (bi, row_block, j))
    return pl.pallas_call(
        functools.partial(_fourier_kernel, n=n),
        grid=(b, BRANCH_W // cw),
        in_specs=[zspec, zspec, _const_spec(e_tab.shape), _const_spec(d_tab.shape)],
        out_specs=pl.BlockSpec((1, t_len, cw), lambda bi, j: (bi, 0, j)),
        out_shape=jax.ShapeDtypeStruct((b, t_len, BRANCH_W), BF16),
        scratch_shapes=[pltpu.VMEM((cw // LANES, n * (n + FOURIER_SLAB_PAD), LANES), F32)] * 4,
        compiler_params=_cparams(("parallel", "arbitrary"), 48),
        name="fourier",
    )(zr, zi, e_tab, d_tab)


def _pool_rows(ext, pos, n, w_ref, scale):
    n_ext = ext.shape[0]
    rows = n_ext - 2 * POOL_HALO
    back = lambda v, k: pltpu.roll(v, k, 0)
    fwd = lambda v, k: pltpu.roll(v, n_ext - k, 0)
    outs = []
    for gi, w in enumerate(POOL_WINDOWS):
        e = ext[:, gi * GROUP_W:(gi + 1) * GROUP_W]
        wsum = e + back(e, 1)
        half = 1
        while 2 * half < w:
            wsum = back(wsum, half) + fwd(wsum, half)
            half *= 2
        own = slice(POOL_HALO, POOL_HALO + rows)
        cnt = jnp.minimum(pos + w // 2, n) - jnp.maximum(pos - w // 2, 0)
        pooled = wsum[own] / cnt.astype(F32) - e[own]
        outs.append(jnp.dot(pooled.astype(BF16), w_ref[gi], preferred_element_type=F32))
    return (jnp.concatenate(outs, axis=-1) * scale).astype(BF16)


def _route(logits_t, bias):
    aff = jax.nn.sigmoid(logits_t)
    sel = aff + bias
    neg = -jnp.inf
    firsts, seconds, scores = [], [], []
    for g in range(N_EXPERTS // EXPERTS_PER_GROUP):
        s = [sel[EXPERTS_PER_GROUP * g + k:EXPERTS_PER_GROUP * g + k + 1, :] for k in range(EXPERTS_PER_GROUP)]
        m1 = jnp.maximum(jnp.maximum(s[0], s[1]), jnp.maximum(s[2], s[3]))
        i1 = jnp.where(s[0] == m1, 0, jnp.where(s[1] == m1, 1, jnp.where(s[2] == m1, 2, 3)))
        r = [jnp.where(i1 == k, neg, s[k]) for k in range(EXPERTS_PER_GROUP)]
        m2 = jnp.maximum(jnp.maximum(r[0], r[1]), jnp.maximum(r[2], r[3]))
        i2 = jnp.where(r[0] == m2, 0, jnp.where(r[1] == m2, 1, jnp.where(r[2] == m2, 2, 3)))
        firsts.append(i1 + EXPERTS_PER_GROUP * g)
        seconds.append(i2 + EXPERTS_PER_GROUP * g)
        scores.append(m1 + m2)
    best = jnp.maximum(jnp.maximum(scores[0], scores[1]), jnp.maximum(scores[2], scores[3]))
    pick = lambda v: jnp.where(scores[0] == best, v[0], jnp.where(scores[1] == best, v[1],
                                                                 jnp.where(scores[2] == best, v[2], v[3])))
    e1, e2 = pick(firsts), pick(seconds)
    lo = jnp.minimum(e1, e2) & (EXPERTS_PER_GROUP - 1)
    hi = jnp.maximum(e1, e2) & (EXPERTS_PER_GROUP - 1)
    pair = jnp.where(lo == 0, 0, jnp.where(lo == 1, 3, 5)) + hi - lo - 1
    cls = ((e1 >> 2) * PAIRS_PER_GROUP + pair).astype(F32)
    return jnp.concatenate([cls] + [jnp.zeros_like(cls)] * (ROUTE_ROWS - 1), axis=0)


def _pack_bf16_pairs(v):
    w = v.shape[1] // 2
    bits = pltpu.bitcast(v.astype(BF16).astype(F32), jnp.uint32)
    return pltpu.bitcast(bits[:, :w] | (bits[:, w:] >> 16), jnp.int32)


def _unpack_bf16_pairs(p):
    bits = pltpu.bitcast(p, jnp.uint32)
    hi = pltpu.bitcast(bits & jnp.uint32(0xFFFF0000), F32)
    lo = pltpu.bitcast(bits << 16, F32)
    return jnp.concatenate([hi, lo], axis=-1)


def _merge_kernel(*refs, n_real, n_fill, pending, **static):
    n_tok = 3 if pending else 1
    (mod_ref, n1_ref, n2_ref, a_ref, b_ref, c_ref, pc_ref, pp_ref, pn_ref, pw_ref, psc_ref,
     wg_ref, bg_ref, wb_ref, wo_ref, rw_ref, rb_ref) = refs[n_tok:n_tok + 17]
    outs = refs[-4:-1]
    ins = (refs[:n_tok], mod_ref, n1_ref, n2_ref, a_ref, b_ref, c_ref,
           (pc_ref, pp_ref, pn_ref, pw_ref, psc_ref),
           wg_ref, bg_ref, wb_ref, wo_ref, rw_ref, rb_ref)
    if n_fill:
        @pl.when(pl.program_id(1) >= n_real)
        def _():
            for ref in outs:
                ref[...] = jnp.zeros_like(ref)

        pl.when(pl.program_id(1) < n_real)(lambda: _merge_tile(ins, outs, refs[-1], **static))
    else:
        _merge_tile(ins, outs, refs[-1], **static)


def _merge_tile(ins, outs, merged_s, *, nw, sub, p_off, seq_lo, seq_hi):
    (tok_refs, mod_ref, n1_ref, n2_ref, a_ref, b_ref, c_ref, pool_refs,
     wg_ref, bg_ref, wb_ref, wo_ref, rw_ref, rb_ref) = ins
    xo_ref, h2_ref, route_ref = outs
    pc_ref, pp_ref, pn_ref, pw_ref, psc_ref = pool_refs
    x_ref = tok_refs[0]
    tm = x_ref.shape[1]
    sh1, sc1, g1, sh2, sc2, _ = _mod_slices(mod_ref[0])
    row0 = p_off + pl.program_id(1) * tm
    ext = jnp.concatenate([pp_ref[0], pc_ref[0], pn_ref[0]], axis=0).astype(F32)
    gpos = row0 - POOL_HALO + lax.broadcasted_iota(jnp.int32, (tm + 2 * POOL_HALO, 1), 0)
    ext = jnp.where((gpos >= seq_lo) & (gpos < seq_hi), ext, 0.0)
    for r in range(tm // sub):
        rows = slice(r * sub, (r + 1) * sub)
        x = x_ref[0, rows, :]
        if len(tok_refs) == 3:
            x = _pending_residual(x, tok_refs[1], tok_refs[2], rows)
        hb = _norm_mod(x, n1_ref[...], sh1, sc1).astype(BF16)
        pos = row0 - seq_lo + r * sub + lax.broadcasted_iota(jnp.int32, (sub, 1), 0)
        pooled = _pool_rows(ext[r * sub:(r + 1) * sub + 2 * POOL_HALO], pos, seq_hi - seq_lo,
                            pw_ref, psc_ref[...])
        branches = (a_ref[0, rows, :], b_ref[0, rows, :], c_ref[0, rows, :], pooled)
        for n in range(D_MODEL // nw):
            cols = slice(n * nw, (n + 1) * nw)
            merged = None
            for i, br in enumerate(branches):
                gate = jax.nn.sigmoid(
                    jnp.dot(hb, wg_ref[i, :, cols], preferred_element_type=F32) + bg_ref[i, :, cols])
                term = gate * jnp.dot(br, wb_ref[i, :, cols], preferred_element_type=F32)
                merged = term if merged is None else merged + term
            merged_s[rows, cols] = merged.astype(BF16)
        y = jnp.dot(merged_s[rows, :], wo_ref[...], preferred_element_type=F32)
        xn = x + g1 * y
        xo_ref[0, rows, :] = xn
        h2 = _norm_mod(xn, n2_ref[...], sh2, sc2)
        h2_ref[0, rows, :] = _pack_bf16_pairs(h2)
        h_hi = h2.astype(BF16)
        h_lo = (h2 - h_hi.astype(F32)).astype(BF16)
        nt = (((1,), (1,)), ((), ()))
        by_hi = lax.dot_general(rw_ref[...], h_hi, nt, preferred_element_type=F32)
        by_lo = lax.dot_general(rw_ref[:N_EXPERTS, :], h_lo, nt, preferred_element_type=F32)
        logits_t = by_hi[:N_EXPERTS] + by_hi[N_EXPERTS:] + by_lo
        route_ref[0, :, rows] = _route(logits_t, rb_ref[...])


def _merge(tokens, mod, mod_row, n1, n2, branches, offsets, p_in, pool_params, seq, weights, s_out, out_off,
           rows, tm, prev=None):
    pending = len(tokens) == 3
    b, _, d = tokens[0].shape
    wg, bg, wb, wo, rw_t, rb = weights
    pool_w, pool_scale = pool_params
    seq_lo, seq_hi = seq
    n_real = rows // tm
    n_fill = -(-(s_out - out_off - rows) // tm) if prev is None else 0
    step = lambda j: jnp.minimum(j, n_real - 1)
    blk = lambda width, off: pl.BlockSpec((1, tm, width), lambda bi, j: (bi, off // tm + step(j), 0))
    out_blk = lambda width: pl.BlockSpec((1, tm, width), lambda bi, j: (bi, out_off // tm + j, 0))
    assert all(o % tm == 0 for o in offsets) and out_off % tm == 0 and rows % tm == 0 and seq_lo % tm == 0
    hb = tm // POOL_HALO
    last_halo = p_in.shape[1] // POOL_HALO - 1
    halo = lambda shift: pl.BlockSpec(
        (1, POOL_HALO, BRANCH_W),
        lambda bi, j: (bi, jnp.clip((seq_lo // tm + step(j) + shift) * hb - 1 + shift, 0, last_halo), 0))
    out_shape = [
        jax.ShapeDtypeStruct((b, s_out, d), F32),
        jax.ShapeDtypeStruct((b, s_out, d // 2), jnp.int32),
        jax.ShapeDtypeStruct((b, ROUTE_ROWS, s_out), F32),
    ]
    mod_spec = pl.BlockSpec((1, 1, 6 * d), lambda bi, j: (bi if mod_row is None else mod_row, 0, 0))
    tok_off = out_off if pending else 0
    token_specs = [blk(d, tok_off)] + ([blk(d // 2, tok_off), mod_spec] if pending else [])
    n_in = len(tokens) + 17
    extra_specs = [] if prev is None else [pl.BlockSpec(memory_space=pl.ANY)] * 3
    return pl.pallas_call(
        functools.partial(_merge_kernel, nw=512, sub=MERGE_CHAIN_ROWS, n_real=n_real, n_fill=n_fill,
                          pending=pending, p_off=seq_lo, seq_lo=seq_lo, seq_hi=seq_hi),
        scratch_shapes=[pltpu.VMEM((tm, d), BF16)],
        grid=(b, n_real + n_fill),
        in_specs=token_specs + [
            mod_spec,
            _const_spec((1, d)), _const_spec((1, d)),
            *[blk(BRANCH_W, off) for off in offsets],
            blk(BRANCH_W, seq_lo), halo(0), halo(1),
            _const_spec(pool_w.shape), _const_spec((1, BRANCH_W)),
            _const_spec(wg.shape), _const_spec(bg.shape), _const_spec(wb.shape), _const_spec(wo.shape),
            _const_spec(rw_t.shape), _const_spec(rb.shape),
        ] + extra_specs,
        out_specs=[
            out_blk(d),
            out_blk(d // 2),
            pl.BlockSpec((1, ROUTE_ROWS, tm), lambda bi, j: (bi, 0, out_off // tm + j)),
        ],
        out_shape=out_shape,
        input_output_aliases={} if prev is None else {n_in + i: i for i in range(3)},
        compiler_params=_cparams(("parallel", "arbitrary"), 56),
        name="merge",
    )(*tokens, mod, n1, n2, *branches, p_in, p_in, p_in, pool_w, pool_scale,
      wg, bg, wb, wo, rw_t, rb, *(() if prev is None else prev))


def _rank_kernel(cls_ref, rank_ref, cnt_ref, cnt_s, *, tr):
    @pl.when(pl.program_id(0) == 0)
    def _():
        cnt_s[...] = jnp.zeros_like(cnt_s)

    cls = cls_ref[0]
    cid = lax.broadcasted_iota(jnp.int32, (CLASS_ROWS, tr), 0).astype(F32)
    onehot = cid == cls
    before = lax.broadcasted_iota(jnp.int32, (tr, tr), 0) < lax.broadcasted_iota(jnp.int32, (tr, tr), 1)
    prefix = jnp.dot(jnp.where(onehot, 1.0, 0.0).astype(BF16), jnp.where(before, 1.0, 0.0).astype(BF16),
                     preferred_element_type=F32)
    carry = cnt_s[...][:, 0:1]
    rank_ref[0] = jnp.sum(jnp.where(onehot, prefix + carry, 0.0), axis=0, keepdims=True)
    cnt_s[...] += jnp.sum(jnp.where(onehot, 1.0, 0.0), axis=1, keepdims=True)
    cnt_ref[...] = cnt_s[...]


def _rank(cls_flat, tr=512):
    n = cls_flat.shape[0]
    tr = math.gcd(n, tr)
    cls3 = cls_flat.reshape(n // tr, 1, tr)
    rank, cnt = pl.pallas_call(
        functools.partial(_rank_kernel, tr=tr),
        grid=(n // tr,),
        in_specs=[pl.BlockSpec((1, 1, tr), lambda i: (i, 0, 0))],
        out_specs=[pl.BlockSpec((1, 1, tr), lambda i: (i, 0, 0)), _const_spec((CLASS_ROWS, LANES))],
        out_shape=[jax.ShapeDtypeStruct((n // tr, 1, tr), F32), jax.ShapeDtypeStruct((CLASS_ROWS, LANES), F32)],
        scratch_shapes=[pltpu.VMEM((CLASS_ROWS, LANES), F32)],
        compiler_params=_cparams(("arbitrary",), 32),
        name="rank",
    )(cls3)
    return rank.reshape(n), cnt[:N_CLASSES, 0]


def _sc_layout(n):
    info = plsc.get_sparse_core_info()
    nw = info.num_cores * info.num_subcores
    per_worker = n // nw
    assert per_worker * nw == n
    chunk = max(c for c in range(8, SC_MAX_CHUNK + 1, 8) if per_worker % c == 0)
    return info.num_cores, nw, per_worker // chunk, chunk


def _sc_scatter_rows(src, pos, n_out):
    n, w = src.shape
    nc, nw, k, c = _sc_layout(n)
    mesh = plsc.VectorSubcoreMesh(core_axis_name="c", subcore_axis_name="s")

    @functools.partial(
        pl.kernel, mesh=mesh,
        out_type=jax.ShapeDtypeStruct((n_out, w), src.dtype),
        scratch_types=[pltpu.VMEM((k, c), jnp.int32), pltpu.VMEM((c, w), src.dtype), pltpu.SemaphoreType.DMA],
        name="moe_scatter",
    )
    def scatter(src_hbm, pos_hbm, out_hbm, idx_v, rows_v, sem):
        wid = lax.axis_index("s") * nc + lax.axis_index("c")
        pltpu.sync_copy(pos_hbm.at[wid], idx_v)

        @pl.loop(0, k)
        def _(j):
            off = pl.multiple_of(wid * (k * c) + j * c, 8)
            pltpu.sync_copy(src_hbm.at[pl.ds(off, c)], rows_v)
            pltpu.async_copy(rows_v, out_hbm.at[idx_v.at[j]], sem).wait()

    return scatter(src, pos.reshape(nw, k, c))


def _sc_gather_rows(src, pos):
    n = pos.shape[0]
    w = src.shape[1]
    nc, nw, k, c = _sc_layout(n)
    mesh = plsc.VectorSubcoreMesh(core_axis_name="c", subcore_axis_name="s")

    @functools.partial(
        pl.kernel, mesh=mesh,
        out_type=jax.ShapeDtypeStruct((n, w), src.dtype),
        scratch_types=[pltpu.VMEM((k, c), jnp.int32), pltpu.VMEM((c, w), src.dtype), pltpu.SemaphoreType.DMA],
        name="moe_gather",
    )
    def gather(src_hbm, pos_hbm, out_hbm, idx_v, rows_v, sem):
        wid = lax.axis_index("s") * nc + lax.axis_index("c")
        pltpu.sync_copy(pos_hbm.at[wid], idx_v)

        @pl.loop(0, k)
        def _(j):
            off = pl.multiple_of(wid * (k * c) + j * c, 8)
            pltpu.async_copy(src_hbm.at[idx_v.at[j]], rows_v, sem).wait()
            pltpu.sync_copy(rows_v, out_hbm.at[pl.ds(off, c)])

    return gather(src, pos.reshape(nw, k, c))


def _gmm_kernel(lo_ref, hi_ref, new_ref, nact_ref, h_ref, rw_ref,
                w1a_ref, w1b_ref, w3a_ref, w3b_ref, w2a_ref, w2b_ref, o_ref, w13_s, w2_s):
    t = pl.program_id(0)

    @pl.when(new_ref[t] == 1)
    def _():
        for i, (w1_ref, w3_ref, w2_ref) in enumerate(((w1a_ref, w3a_ref, w2a_ref), (w1b_ref, w3b_ref, w2b_ref))):
            w13_s[i, :, :EXPERT_FF] = w1_ref[0, 0].astype(BF16)
            w13_s[i, :, EXPERT_FF:] = w3_ref[0, 0].astype(BF16)
            w2_s[i * EXPERT_FF:(i + 1) * EXPERT_FF, :] = w2_ref[0, 0].astype(BF16)

    @pl.when(t < nact_ref[0])
    def _():
        xf = _unpack_bf16_pairs(h_ref[...])
        x = xf.astype(BF16)
        aff = [jax.nn.sigmoid(jnp.sum(xf * rw_ref[pl.ds(e_ref[t], 1), :], axis=-1, keepdims=True))
               for e_ref in (lo_ref, hi_ref)]
        total = aff[0] + aff[1]

        def hidden(i):
            ab = jnp.dot(x, w13_s[i], preferred_element_type=F32)
            a, gate = ab[:, :EXPERT_FF], ab[:, EXPERT_FF:]
            return (aff[i] / total) * ((a * jax.nn.sigmoid(a)) * gate)

        hid = jnp.concatenate([hidden(0), hidden(1)], axis=-1).astype(BF16)
        o_ref[...] = _pack_bf16_pairs(jnp.dot(hid, w2_s[...], preferred_element_type=F32))


def _gmm(tile_lo, tile_hi, tile_new, n_act, hs, rw_t, w1, w3, w2, layer, tm):
    n_pad, half = hs.shape
    d = 2 * half
    row = lambda t, lo, hi, new, na: (jnp.minimum(t, na[0] - 1), 0)
    e_lo = lambda t, lo, hi, new, na: (layer, lo[jnp.minimum(t, na[0] - 1)], 0, 0)
    e_hi = lambda t, lo, hi, new, na: (layer, hi[jnp.minimum(t, na[0] - 1)], 0, 0)
    up = lambda e: pl.BlockSpec((1, 1, d, EXPERT_FF), e)
    down = lambda e: pl.BlockSpec((1, 1, EXPERT_FF, d), e)
    return pl.pallas_call(
        _gmm_kernel,
        grid_spec=pltpu.PrefetchScalarGridSpec(
            num_scalar_prefetch=4,
            grid=(n_pad // tm,),
            in_specs=[
                pl.BlockSpec((tm, half), row),
                pl.BlockSpec(rw_t.shape, lambda t, lo, hi, new, na: (0, 0)),
                up(e_lo), up(e_hi), up(e_lo), up(e_hi), down(e_lo), down(e_hi),
            ],
            out_specs=pl.BlockSpec((tm, half), row),
            scratch_shapes=[pltpu.VMEM((2, d, 2 * EXPERT_FF), BF16), pltpu.VMEM((2 * EXPERT_FF, d), BF16)],
        ),
        out_shape=jax.ShapeDtypeStruct((n_pad, half), jnp.int32),
        compiler_params=_cparams(("arbitrary",), 56),
        name="moe_gmm",
    )(tile_lo, tile_hi, tile_new, n_act, hs, rw_t, w1, w1, w3, w3, w2, w2)


def _moe_routed(h2p, route, rw_t, w1, w3, w2, layer, tm=256):
    b, s, half = h2p.shape
    n = b * s
    n_pad = n + N_CLASSES * tm
    cls = route[:, 0, :].reshape(n)
    rank, counts = _rank(cls)
    counts = counts.astype(jnp.int32)
    padded = (counts + tm - 1) // tm * tm
    ends = jnp.cumsum(padded)
    pos = jnp.take(ends - padded, cls.astype(jnp.int32)) + rank.astype(jnp.int32)
    n_act = (ends[-1] // tm).reshape(1)
    tile_row = jnp.arange(n_pad // tm, dtype=jnp.int32) * tm
    tile_cls = jnp.minimum(jnp.sum(tile_row[:, None] >= ends[None, :], axis=1), N_CLASSES - 1)
    pair_lo, pair_hi = (jnp.asarray(a, jnp.int32) for a in _class_experts())
    hs = _sc_scatter_rows(h2p.reshape(n, half), pos, n_pad)
    prev_cls = jnp.concatenate([jnp.full((1,), -1, tile_cls.dtype), tile_cls[:-1]])
    tile_new = ((tile_cls != prev_cls) & (tile_row < ends[-1])).astype(jnp.int32)
    ys = _gmm(jnp.take(pair_lo, tile_cls), jnp.take(pair_hi, tile_cls), tile_new, n_act, hs, rw_t,
              w1, w3, w2, layer, tm)
    return _sc_gather_rows(ys, pos).reshape(b, s, half)


def _class_experts():
    lo, hi = [], []
    for g in range(N_EXPERTS // EXPERTS_PER_GROUP):
        for i in range(EXPERTS_PER_GROUP):
            for j in range(i + 1, EXPERTS_PER_GROUP):
                lo.append(EXPERTS_PER_GROUP * g + i)
                hi.append(EXPERTS_PER_GROUP * g + j)
    return np.array(lo), np.array(hi)


def _final_residual_kernel(x_ref, y_ref, mod_ref, g_ref, o_ref):
    x = _pending_residual(x_ref[0], y_ref, mod_ref)
    o_ref[0] = x * lax.rsqrt(jnp.mean(x * x, axis=-1, keepdims=True) + EPS) * g_ref[...]


def _final_residual(x1, yp, mod, gain, tm=512):
    b, t, d = x1.shape
    tok = lambda bi, j: (bi, j, 0)
    return pl.pallas_call(
        _final_residual_kernel,
        grid=(b, t // tm),
        in_specs=[
            pl.BlockSpec((1, tm, d), tok),
            pl.BlockSpec((1, tm, d // 2), tok),
            pl.BlockSpec((1, 1, 6 * d), lambda bi, j: (bi, 0, 0)),
            _const_spec((1, d)),
        ],
        out_specs=pl.BlockSpec((1, tm, d), tok),
        out_shape=jax.ShapeDtypeStruct((b, t, d), F32),
        compiler_params=_cparams(("parallel", "arbitrary"), 32),
        name="final_residual",
    )(x1, yp, mod, gain)


ROPE_FREQS = ROPE_AXIS_DIM // 2


def _rope_tables(t_lat, n_ctx):
    rows = t_lat // GRID_W
    row = jnp.repeat(jnp.arange(rows, dtype=F32), GRID_W)
    col = jnp.tile(jnp.arange(GRID_W, dtype=F32), rows)
    inv_freq = ROPE_THETA ** (-jnp.arange(0, ROPE_AXIS_DIM, 2, dtype=F32) / ROPE_AXIS_DIM)
    ang = jnp.stack([row[:, None] * inv_freq, col[:, None] * inv_freq], axis=1)
    lanes = (2, LANES // HEAD_DIM, 2, ROPE_FREQS)
    sign = jnp.array([-1.0, 1.0], F32).reshape(1, 2, 1, 1, 1)
    cos = jnp.broadcast_to(jnp.cos(ang)[:, None, None], (t_lat,) + lanes).reshape(t_lat, LANES)
    sin = jnp.broadcast_to(sign * jnp.sin(ang)[:, None, None], (t_lat,) + lanes).reshape(t_lat, LANES)
    pad = lambda tbl, fill: jnp.concatenate([tbl, jnp.full((n_ctx, LANES), fill, F32)], axis=0)
    return pad(cos, 1.0), pad(sin, 0.0)


def _qk_lanes(w, n_heads):
    lead = w.shape[:-1]
    n_chunks = n_heads // N_KV_HEADS
    w = w.reshape(lead + (N_KV_HEADS, n_chunks, 2, 2, ROPE_FREQS))
    nl = len(lead)
    w = jnp.transpose(w, tuple(range(nl)) + (nl + 1, nl + 3, nl, nl + 2, nl + 4))
    return w.reshape(lead + (n_heads * HEAD_DIM,))


def _permute_heads(w, axis):
    shp = w.shape
    w = w.reshape(shp[:axis] + (N_Q_HEADS, HEAD_DIM) + shp[axis + 1:])
    w = jnp.take(w, jnp.array(HEAD_PERM), axis=axis)
    return w.reshape(shp)


def _square_factor(n):
    r = int(round(math.sqrt(n)))
    assert r * r == n, "sequence lengths must be perfect squares for the two-stage DFT"
    return r


def kernel(x, c, ctx, c_ctx, w_ada, b_ada, norm1, norm2, w_in, q_gain, k_gain, sink, pool_w, pool_scale,
           w_branch, w_gate, b_gate, w_out, router_w, router_bias, w1, w3, w2, norm_f):
    b, t_lat, d = x.shape
    n_ctx = ctx.shape[1]
    s = t_lat + n_ctx
    depth = w_ada.shape[0]
    assert d == D_MODEL and b < MOD_ROWS and t_lat % 256 == 0 and n_ctx % 256 == 0 and t_lat % n_ctx == 0

    tokens = (x, ctx)
    cc = jnp.zeros((MOD_ROWS, d), F32).at[:b].set(c).at[b].set(c_ctx)
    mod_all = _ada(cc, w_ada, b_ada).reshape(depth, MOD_ROWS, 1, 6 * d)

    cos, sin = _rope_tables(t_lat, n_ctx)
    lane_head = (np.arange(BRANCH_W) // LANES) * 2 + (np.arange(BRANCH_W) // (HEAD_DIM // 2)) % 2
    seg = jnp.asarray((lane_head[:, None] == lane_head[None, :]) / HEAD_DIM, BF16)
    cs = jnp.asarray(_channel_dft_table()).astype(BF16)
    f_lat = [jnp.asarray(a).astype(BF16) for a in _fourier_tables(*(_square_factor(t_lat),) * 2)]
    f_ctx = [jnp.asarray(a).astype(BF16) for a in _fourier_tables(*(_square_factor(n_ctx),) * 2)]
    wbias = jnp.asarray(_window_bias(n_ctx))
    rw_hi = router_w.T.astype(BF16)
    rw_t = jnp.concatenate([rw_hi, (router_w.T - rw_hi.astype(F32)).astype(BF16)], axis=0)
    rb = router_bias.reshape(N_EXPERTS, 1)

    for l in range(depth):
        need_ctx = l < depth - 1
        s_out = s if need_ctx else t_lat
        cols = jnp.split(w_in[l], np.cumsum((512, 512, 512, 128, 128, 512, 128))[:], axis=1)
        f_w, p_w, qb_w, kb_w, vb_w, qw_w, kw_w, vw_w = cols
        w_in_l = jnp.concatenate([_qk_lanes(qb_w, N_Q_HEADS), _qk_lanes(qw_w, N_Q_HEADS),
                                  _qk_lanes(kb_w, N_KV_HEADS), _qk_lanes(kw_w, N_KV_HEADS),
                                  f_w, vb_w, vw_w, p_w], axis=1).astype(BF16)
        wb_l = jnp.stack([w_branch[l, 0], _permute_heads(w_branch[l, 1], 0),
                          _permute_heads(w_branch[l, 2], 0), w_branch[l, 3]]).astype(BF16)
        mod = mod_all[l]
        n1 = norm1[l].reshape(1, d)
        n2 = norm2[l].reshape(1, d)
        qg = _qk_lanes(jnp.tile(q_gain[l], N_Q_HEADS), N_Q_HEADS).reshape(1, BRANCH_W)
        kg = _qk_lanes(jnp.tile(k_gain[l], N_KV_HEADS), N_KV_HEADS).reshape(1, LANES)

        zr, zi, p_in, qb, qw, kvb, kvw = _inproj(tokens, t_lat, mod, n1, w_in_l, qg, kg, seg, cs, cos, sin)

        out_a = _fourier(zr, zi, f_lat[0], f_lat[1], t_lat, 0)
        out_b = _gattn(qb, kvb, 0, t_lat, 0, s, tq=512)
        out_c = _wattn(jnp.take(sink[l], jnp.array(HEAD_PERM)) * LOG2E, wbias, qw, kvw, s_out, t_lat)

        weights = (w_gate[l].astype(BF16), b_gate[l].reshape(4, 1, d), wb_l, w_out[l].astype(BF16), rw_t, rb)
        pool_params = (pool_w[l].astype(BF16), pool_scale[l].reshape(1, BRANCH_W))
        pending = len(tokens) == 3
        merged = _merge(tokens if pending else tokens[:1], mod, None, n1, n2, (out_a, out_b, out_c), (0, 0, 0),
                        p_in, pool_params, (0, t_lat), weights, s_out, 0, t_lat, 2 * MERGE_CHAIN_ROWS)
        if need_ctx:
            out_ac = _fourier(zr, zi, f_ctx[0], f_ctx[1], n_ctx, t_lat // n_ctx)
            out_bc = _gattn(qb, kvb, t_lat, n_ctx, t_lat, n_ctx)
            merged = _merge(tokens if pending else tokens[1:], mod, b, n1, n2, (out_ac, out_bc, out_c),
                            (0, 0, t_lat), p_in, pool_params, (t_lat, s), weights, s_out, t_lat, n_ctx,
                            MERGE_CHAIN_ROWS, prev=merged)
        x1, h2p, route = merged
        yp = _moe_routed(h2p, route, router_w.T, w1, w3, w2, l)
        tokens = (x1, yp, mod)

    return _final_residual(*tokens, norm_f.reshape(1, d))
```

```python
import functools
import math

import numpy as np
import jax
import jax.numpy as jnp
from jax import lax
from jax.experimental import pallas as pl
from jax.experimental.pallas import tpu as pltpu
from jax.experimental.pallas import tpu_sc as plsc

F32 = jnp.float32
BF16 = jnp.bfloat16

D_MODEL = 1024
HEAD_DIM = 64
N_Q_HEADS = 8
N_KV_HEADS = 2
GRID_W = 64
ROPE_THETA = 10000.0
ROPE_AXIS_DIM = HEAD_DIM // 2
QBLK = 128
WINDOW = 128
BRANCH_W = 512
GROUP_W = 128
POOL_WINDOWS = (2, 4, 8, 16)
N_EXPERTS = 16
EXPERTS_PER_GROUP = 4
EXPERT_FF = 512
EPS = 1e-6
MOD_ROWS = 16
NEG_BIG = -1e30
LOG2E = math.log2(math.e)
LANES = 128
POOL_HALO = 16
PAIRS_PER_GROUP = 6
N_CLASSES = 24
CLASS_ROWS = 32
ROUTE_ROWS = 8
SC_MAX_CHUNK = 128
WATTN_QBLOCKS = 2
MERGE_CHAIN_ROWS = 256
FOURIER_SLAB_PAD = 4

HEAD_PERM = (0, 4, 1, 5, 2, 6, 3, 7)


def _cparams(sem, vmem_mb):
    return pltpu.CompilerParams(dimension_semantics=sem, vmem_limit_bytes=vmem_mb * 1024 * 1024)


def _const_spec(shape):
    nd = len(shape)
    return pl.BlockSpec(shape, lambda *_: (0,) * nd)


def _ada_kernel(c_ref, w_ref, b_ref, o_ref):
    c = c_ref[...]
    s = c * jax.nn.sigmoid(c)
    o_ref[0] = jnp.dot(s.astype(BF16), w_ref[0].astype(BF16), preferred_element_type=F32) + b_ref[0]


def _ada(cc, w_ada, b_ada):
    depth, d, n = w_ada.shape
    tn = 1536
    return pl.pallas_call(
        _ada_kernel,
        grid=(depth, n // tn),
        in_specs=[
            pl.BlockSpec((MOD_ROWS, d), lambda l, j: (0, 0)),
            pl.BlockSpec((1, d, tn), lambda l, j: (l, 0, j)),
            pl.BlockSpec((1, 1, tn), lambda l, j: (l, 0, j)),
        ],
        out_specs=pl.BlockSpec((1, MOD_ROWS, tn), lambda l, j: (l, 0, j)),
        out_shape=jax.ShapeDtypeStruct((depth, MOD_ROWS, n), F32),
        compiler_params=_cparams(("arbitrary", "arbitrary"), 40),
        name="ada",
    )(cc, w_ada, b_ada.reshape(depth, 1, n))


def _norm_mod(x, gain, shift, scale):
    ms = jnp.mean(x * x, axis=-1, keepdims=True)
    return (x * lax.rsqrt(ms + EPS) * gain) * (1.0 + scale) + shift


def _mod_slices(m):
    d = D_MODEL
    return [m[:, i * d:(i + 1) * d] for i in range(6)]


def _head_norm(z, seg, gain):
    ms = jnp.dot((z * z).astype(BF16), seg, preferred_element_type=F32)
    return z * lax.rsqrt(ms + EPS) * gain


def _rope(z, cos, sin):
    outs = []
    for c in range(z.shape[1] // LANES):
        zc = z[:, c * LANES:(c + 1) * LANES]
        outs.append(zc * cos + pltpu.roll(zc, LANES // 2, 1) * sin)
    return outs[0] if len(outs) == 1 else jnp.concatenate(outs, axis=-1)


def _stream_specs(tm, d, n_lat):
    return [pl.BlockSpec((1, tm, d), lambda bi, j: (bi, jnp.minimum(j, n_lat - 1), 0)),
            pl.BlockSpec((1, tm, d), lambda bi, j: (bi, jnp.maximum(j - n_lat, 0), 0))]


def _pending_residual(x, y_ref, modp_ref, rows=slice(None)):
    return x + _mod_slices(modp_ref[0])[5] * _unpack_bf16_pairs(y_ref[0, rows, :])


def _inproj_kernel(*refs, n_lat, pending):
    if pending:
        x_ref, y_ref, modp_ref = refs[:3]
        x = _pending_residual(x_ref[0], y_ref, modp_ref)
    else:
        xl_ref, xc_ref = refs[:2]
        x = jnp.where(pl.program_id(1) >= n_lat, xc_ref[0], xl_ref[0])
    (mod_ref, n1_ref, w_ref, qg_ref, kg_ref, seg_ref, cs_ref, cos_ref, sin_ref,
     zr_ref, zi_ref, p_ref, qb_ref, qw_ref, kvb_ref, kvw_ref) = refs[3 if pending else 2:]
    sh1, sc1 = _mod_slices(mod_ref[0])[:2]
    h = _norm_mod(x, n1_ref[...], sh1, sc1)
    u = jnp.dot(h.astype(BF16), w_ref[...], preferred_element_type=F32)
    cos, sin = cos_ref[...], sin_ref[...]
    w = BRANCH_W
    seg = seg_ref[...]
    qb = _rope(_head_norm(u[:, 0:w], seg, qg_ref[...]), cos, sin)
    qb_ref[0] = (qb * (HEAD_DIM ** -0.5 * LOG2E)).astype(BF16)
    qw = _rope(u[:, w:2 * w], cos, sin)
    qw_ref[0] = (qw * (HEAD_DIM ** -0.5 * LOG2E)).astype(BF16)
    o = 2 * w
    kb = _rope(_head_norm(u[:, o:o + LANES], seg[:LANES, :LANES], kg_ref[...]), cos, sin)
    kw = _rope(u[:, o + LANES:o + 2 * LANES], cos, sin)
    o += 2 * LANES
    f_in = u[:, o:o + w].astype(BF16)
    zr, zi = [], []
    for g in range(w // GROUP_W):
        z = jnp.dot(f_in[:, g * GROUP_W:(g + 1) * GROUP_W], cs_ref[...], preferred_element_type=F32)
        zr.append(z[:, :GROUP_W])
        zi.append(z[:, GROUP_W:])
    zr_ref[0] = jnp.concatenate(zr, axis=-1).astype(BF16)
    zi_ref[0] = jnp.concatenate(zi, axis=-1).astype(BF16)
    o += w
    vb = u[:, o:o + LANES]
    vw = u[:, o + LANES:o + 2 * LANES]
    kvb_ref[0] = jnp.concatenate([kb, vb, jnp.ones_like(vb)], axis=-1).astype(BF16)
    kvw_ref[0] = jnp.concatenate([kw, vw, jnp.ones_like(vw)], axis=-1).astype(BF16)
    p_ref[0] = u[:, o + 2 * LANES:o + 2 * LANES + w].astype(BF16)


def _inproj(tokens, t_lat, mod, n1, w_in, qg, kg, seg, cs, cos, sin, tm=256):
    pending = len(tokens) == 3
    b, _, d = tokens[0].shape
    s = tokens[0].shape[1] if pending else t_lat + tokens[1].shape[1]
    nw = w_in.shape[1]
    n_lat = t_lat // tm
    tok = lambda bi, j: (bi, j, 0)
    tab = lambda bi, j: (j, 0)
    mod_spec = pl.BlockSpec((1, 1, 6 * d), lambda bi, j: (jnp.where(j >= n_lat, b, bi), 0, 0))
    if pending:
        token_specs = [pl.BlockSpec((1, tm, d), tok), pl.BlockSpec((1, tm, d // 2), tok), mod_spec]
    else:
        token_specs = _stream_specs(tm, d, n_lat)
    widths = (BRANCH_W,) * 5 + (3 * LANES, 3 * LANES)
    return pl.pallas_call(
        functools.partial(_inproj_kernel, n_lat=n_lat, pending=pending),
        grid=(b, s // tm),
        in_specs=token_specs + [
            mod_spec,
            _const_spec((1, d)),
            _const_spec((d, nw)),
            _const_spec((1, BRANCH_W)),
            _const_spec((1, LANES)),
            _const_spec((BRANCH_W, BRANCH_W)),
            _const_spec((GROUP_W, 2 * GROUP_W)),
            pl.BlockSpec((tm, LANES), tab),
            pl.BlockSpec((tm, LANES), tab),
        ],
        out_specs=[pl.BlockSpec((1, tm, wd), tok) for wd in widths],
        out_shape=[jax.ShapeDtypeStruct((b, s, wd), BF16) for wd in widths],
        compiler_params=_cparams(("parallel", "arbitrary"), 48),
        name="inproj",
    )(*tokens, mod, n1, w_in, qg, kg, seg, cs, cos, sin)


def _split_heads(qc, lane):
    first = (lane & (HEAD_DIM // 2)) == 0
    zero = jnp.zeros_like(qc)
    return jnp.concatenate([jnp.where(first, qc, zero), jnp.where(first, zero, qc)], axis=0)


def _gattn_kernel(q_ref, kv_ref, o_ref, q2_s, acc_s, *, sub):
    tq = q_ref.shape[1]
    lane = lax.broadcasted_iota(jnp.int32, (1, LANES), 1)
    nt = (((1,), (1,)), ((), ()))
    n_chunks = BRANCH_W // LANES
    for c in range(n_chunks):
        q2_s[2 * c * tq:(2 * c + 2) * tq, :] = _split_heads(q_ref[0, :, c * LANES:(c + 1) * LANES], lane)
    k = kv_ref[0, :, 0:LANES]
    v = kv_ref[0, :, LANES:3 * LANES]
    for r in range(2 * n_chunks * tq // sub):
        rows = slice(r * sub, (r + 1) * sub)
        s = lax.dot_general(q2_s[rows, :], k, nt, preferred_element_type=F32)
        p = jnp.exp2(s - jnp.max(s, axis=-1, keepdims=True))
        acc_s[rows, :] = jnp.dot(p.astype(BF16), v, preferred_element_type=F32)
    for c in range(n_chunks):
        lo = acc_s[2 * c * tq:(2 * c + 1) * tq, :]
        hi = acc_s[(2 * c + 1) * tq:(2 * c + 2) * tq, :]
        o = jnp.where(lane < HEAD_DIM, lo[:, :LANES] / lo[:, LANES:], hi[:, :LANES] / hi[:, LANES:])
        o_ref[0, :, c * LANES:(c + 1) * LANES] = o.astype(BF16)


def _gattn(qb, kv, q_start, q_len, k_start, k_len, tq=256, sub=128):
    b = qb.shape[0]
    assert q_start % tq == 0 and q_len % tq == 0 and k_start % k_len == 0 and k_len % LANES == 0
    rows = 2 * tq * (BRANCH_W // LANES)
    return pl.pallas_call(
        functools.partial(_gattn_kernel, sub=sub),
        grid=(b, q_len // tq),
        in_specs=[
            pl.BlockSpec((1, tq, BRANCH_W), lambda bi, j: (bi, q_start // tq + j, 0)),
            pl.BlockSpec((1, k_len, 3 * LANES), lambda bi, j: (bi, k_start // k_len, 0)),
        ],
        out_specs=pl.BlockSpec((1, tq, BRANCH_W), lambda bi, j: (bi, j, 0)),
        out_shape=jax.ShapeDtypeStruct((b, q_len, BRANCH_W), BF16),
        scratch_shapes=[pltpu.VMEM((rows, LANES), BF16), pltpu.VMEM((rows, 2 * LANES), F32)],
        compiler_params=_cparams(("parallel", "arbitrary"), 48),
        name="gattn",
    )(qb, kv)


def _window_bias(n_ctx):
    tq = WATTN_QBLOCKS * QBLK
    qi = np.arange(tq)[:, None]
    kj = np.arange(tq + 2 * QBLK)[None, :]
    band = np.abs(kj - WINDOW - qi) <= WINDOW
    blk = kj // QBLK
    variants = [band & (blk != 0), band, band & (blk != WATTN_QBLOCKS + 1), np.zeros_like(band)]
    out = [np.concatenate([np.ones((tq, n_ctx), bool), v], axis=1) for v in variants]
    return np.where(np.stack(out), 0.0, NEG_BIG).astype(np.float32)


def _wattn_kernel(sink_ref, bias_ref, q_ref, *refs, sub):
    o_ref, q2_s, kv_s, acc_s = refs[-4:]
    tq = q_ref.shape[1]
    off = 0
    for blk in refs[:-4]:
        kv_s[off:off + blk.shape[1], :] = blk[0]
        off += blk.shape[1]
    lane = lax.broadcasted_iota(jnp.int32, (1, LANES), 1)
    nt = (((1,), (1,)), ((), ()))
    n_chunks = BRANCH_W // LANES
    for c in range(n_chunks):
        q2_s[2 * c * tq:(2 * c + 2) * tq, :] = _split_heads(q_ref[0, :, c * LANES:(c + 1) * LANES], lane)
    k, v = kv_s[:, 0:LANES], kv_s[:, LANES:3 * LANES]
    for r in range(2 * n_chunks * tq // sub):
        rows = slice(r * sub, (r + 1) * sub)
        q_off = (r * sub) % tq
        sk = sink_ref[(r * sub) // tq]
        s = lax.dot_general(q2_s[rows, :], k, nt, preferred_element_type=F32) + bias_ref[0, q_off:q_off + sub, :]
        m = jnp.maximum(jnp.max(s, axis=-1, keepdims=True), sk)
        pv = jnp.dot(jnp.exp2(s - m).astype(BF16), v, preferred_element_type=F32)
        acc_s[rows, :LANES] = pv[:, :LANES]
        acc_s[rows, LANES:] = pv[:, LANES:] + jnp.exp2(sk - m)
    for c in range(n_chunks):
        lo = acc_s[2 * c * tq:(2 * c + 1) * tq, :]
        hi = acc_s[(2 * c + 1) * tq:(2 * c + 2) * tq, :]
        o = jnp.where(lane < HEAD_DIM, lo[:, :LANES] / lo[:, LANES:], hi[:, :LANES] / hi[:, LANES:])
        o_ref[0, :, c * LANES:(c + 1) * LANES] = o.astype(BF16)


def _wattn(sink, bias, qw, kv, s_out, t_lat):
    b, s, _ = qw.shape
    n_ctx = s - t_lat
    nq = WATTN_QBLOCKS
    tq = nq * QBLK
    assert n_ctx % tq == 0 and t_lat // tq >= 2
    last = s // QBLK - 1
    n_lat = t_lat // tq
    variant = lambda j: jnp.where(j >= n_lat, 3, jnp.where(j == 0, 0, jnp.where(j == n_lat - 1, 2, 1)))
    key_block = lambda off: pl.BlockSpec(
        (1, QBLK, 3 * LANES), lambda bi, j: (bi, jnp.clip(j * nq + off, 0, last), 0))
    rows = 2 * tq * (BRANCH_W // LANES)
    return pl.pallas_call(
        functools.partial(_wattn_kernel, sub=128),
        scratch_shapes=[pltpu.VMEM((rows, LANES), BF16), pltpu.VMEM((bias.shape[2], 3 * LANES), BF16),
                        pltpu.VMEM((rows, 2 * LANES), F32)],
        grid=(b, s_out // tq),
        in_specs=[
            pl.BlockSpec(memory_space=pltpu.SMEM),
            pl.BlockSpec((1,) + bias.shape[1:], lambda bi, j: (variant(j), 0, 0)),
            pl.BlockSpec((1, tq, BRANCH_W), lambda bi, j: (bi, j, 0)),
            pl.BlockSpec((1, n_ctx, 3 * LANES), lambda bi, j: (bi, t_lat // n_ctx, 0)),
        ] + [key_block(off) for off in range(-1, nq + 1)],
        out_specs=pl.BlockSpec((1, tq, BRANCH_W), lambda bi, j: (bi, j, 0)),
        out_shape=jax.ShapeDtypeStruct((b, s_out, BRANCH_W), BF16),
        compiler_params=_cparams(("parallel", "arbitrary"), 32),
        name="wattn",
    )(sink, bias, qw, *([kv] * (nq + 3)))


def _fourier_tables(n1, n2):
    t = n1 * n2
    k2 = np.arange(n2)[None, :, None]
    t2 = np.arange(n2)[None, None, :]
    t1 = np.arange(n1)[:, None, None]
    theta = 2.0 * np.pi * ((k2 * t2 * n1 + k2 * t1) % t) / t
    er, ei = np.cos(theta) / math.sqrt(n2), -np.sin(theta) / math.sqrt(n2)
    e = np.concatenate([np.concatenate([er, -ei], axis=2), np.concatenate([ei, er], axis=2)], axis=1)
    k1 = np.arange(n1)[:, None]
    phi = 2.0 * np.pi * ((k1 * np.arange(n1)[None, :]) % n1) / n1
    dcat = np.concatenate([np.cos(phi), np.sin(phi)], axis=1) / math.sqrt(n1)
    return e.astype(np.float32), dcat.astype(np.float32)


def _channel_dft_table():
    c = np.arange(GROUP_W)
    ang = 2.0 * np.pi * ((c[:, None] * c[None, :]) % GROUP_W) / GROUP_W
    return (np.concatenate([np.cos(ang), -np.sin(ang)], axis=1) / math.sqrt(GROUP_W)).astype(np.float32)


def _fourier_kernel(zr_ref, zi_ref, e_ref, d_ref, o_ref, xr_s, xi_s, yr_s, yi_s, *, n):
    nc = xr_s.shape[0]
    pitch = n + FOURIER_SLAB_PAD
    chunk = lambda c: slice(c * LANES, (c + 1) * LANES)
    slab = lambda i: slice(i * pitch, i * pitch + n)

    def gather(ref, start):
        return jnp.concatenate([ref[c, pl.ds(start, n, stride=pitch), :] for c in range(nc)], axis=-1)

    for c in range(nc):
        for t2 in range(n):
            xr_s[c, slab(t2), :] = zr_ref[0, t2 * n:(t2 + 1) * n, chunk(c)].astype(F32)
            xi_s[c, slab(t2), :] = zi_ref[0, t2 * n:(t2 + 1) * n, chunk(c)].astype(F32)
    for t1 in range(n):
        xs = jnp.concatenate([gather(xr_s, t1), gather(xi_s, t1)], axis=0).astype(BF16)
        y = jnp.dot(e_ref[t1], xs, preferred_element_type=F32)
        for c in range(nc):
            yr_s[c, slab(t1), :] = y[:n, chunk(c)]
            yi_s[c, slab(t1), :] = y[n:, chunk(c)]
    for k2 in range(n):
        ys = jnp.concatenate([gather(yr_s, k2), gather(yi_s, k2)], axis=0).astype(BF16)
        o = jnp.dot(d_ref[...], ys, preferred_element_type=F32)
        for c in range(nc):
            xr_s[c, pl.ds(k2, n, stride=pitch), :] = o[:, chunk(c)]
    for c in range(nc):
        for k1 in range(n):
            o_ref[0, k1 * n:(k1 + 1) * n, chunk(c)] = xr_s[c, slab(k1), :].astype(BF16)


def _fourier(zr, zi, e_tab, d_tab, t_len, row_block, cw=256):
    b = zr.shape[0]
    n = d_tab.shape[0]
    assert n * n == t_len
    zspec = pl.BlockSpec((1, t_len, cw), lambda bi, j: (bi, row_block, j))
    return pl.pallas_call(
        functools.partial(_fourier_kernel, n=n),
        grid=(b, BRANCH_W // cw),
        in_specs=[zspec, zspec, _const_spec(e_tab.shape), _const_spec(d_tab.shape)],
        out_specs=pl.BlockSpec((1, t_len, cw), lambda bi, j: (bi, 0, j)),
        out_shape=jax.ShapeDtypeStruct((b, t_len, BRANCH_W), BF16),
        scratch_shapes=[pltpu.VMEM((cw // LANES, n * (n + FOURIER_SLAB_PAD), LANES), F32)] * 4,
        compiler_params=_cparams(("parallel", "arbitrary"), 48),
        name="fourier",
    )(zr, zi, e_tab, d_tab)


def _pool_rows(ext, pos, n, w_ref, scale):
    n_ext = ext.shape[0]
    rows = n_ext - 2 * POOL_HALO
    back = lambda v, k: pltpu.roll(v, k, 0)
    fwd = lambda v, k: pltpu.roll(v, n_ext - k, 0)
    outs = []
    for gi, w in enumerate(POOL_WINDOWS):
        e = ext[:, gi * GROUP_W:(gi + 1) * GROUP_W]
        wsum = e + back(e, 1)
        half = 1
        while 2 * half < w:
            wsum = back(wsum, half) + fwd(wsum, half)
            half *= 2
        own = slice(POOL_HALO, POOL_HALO + rows)
        cnt = jnp.minimum(pos + w // 2, n) - jnp.maximum(pos - w // 2, 0)
        pooled = wsum[own] / cnt.astype(F32) - e[own]
        outs.append(jnp.dot(pooled.astype(BF16), w_ref[gi], preferred_element_type=F32))
    return (jnp.concatenate(outs, axis=-1) * scale).astype(BF16)


def _route(logits_t, bias):
    aff = jax.nn.sigmoid(logits_t)
    sel = aff + bias
    neg = -jnp.inf
    firsts, seconds, scores = [], [], []
    for g in range(N_EXPERTS // EXPERTS_PER_GROUP):
        s = [sel[EXPERTS_PER_GROUP * g + k:EXPERTS_PER_GROUP * g + k + 1, :] for k in range(EXPERTS_PER_GROUP)]
        m1 = jnp.maximum(jnp.maximum(s[0], s[1]), jnp.maximum(s[2], s[3]))
        i1 = jnp.where(s[0] == m1, 0, jnp.where(s[1] == m1, 1, jnp.where(s[2] == m1, 2, 3)))
        r = [jnp.where(i1 == k, neg, s[k]) for k in range(EXPERTS_PER_GROUP)]
        m2 = jnp.maximum(jnp.maximum(r[0], r[1]), jnp.maximum(r[2], r[3]))
        i2 = jnp.where(r[0] == m2, 0, jnp.where(r[1] == m2, 1, jnp.where(r[2] == m2, 2, 3)))
        firsts.append(i1 + EXPERTS_PER_GROUP * g)
        seconds.append(i2 + EXPERTS_PER_GROUP * g)
        scores.append(m1 + m2)
    best = jnp.maximum(jnp.maximum(scores[0], scores[1]), jnp.maximum(scores[2], scores[3]))
    pick = lambda v: jnp.where(scores[0] == best, v[0], jnp.where(scores[1] == best, v[1],
                                                                 jnp.where(scores[2] == best, v[2], v[3])))
    e1, e2 = pick(firsts), pick(seconds)
    lo = jnp.minimum(e1, e2) & (EXPERTS_PER_GROUP - 1)
    hi = jnp.maximum(e1, e2) & (EXPERTS_PER_GROUP - 1)
    pair = jnp.where(lo == 0, 0, jnp.where(lo == 1, 3, 5)) + hi - lo - 1
    cls = ((e1 >> 2) * PAIRS_PER_GROUP + pair).astype(F32)
    return jnp.concatenate([cls] + [jnp.zeros_like(cls)] * (ROUTE_ROWS - 1), axis=0)


def _pack_bf16_pairs(v):
    w = v.shape[1] // 2
    bits = pltpu.bitcast(v.astype(BF16).astype(F32), jnp.uint32)
    return pltpu.bitcast(bits[:, :w] | (bits[:, w:] >> 16), jnp.int32)


def _unpack_bf16_pairs(p):
    bits = pltpu.bitcast(p, jnp.uint32)
    hi = pltpu.bitcast(bits & jnp.uint32(0xFFFF0000), F32)
    lo = pltpu.bitcast(bits << 16, F32)
    return jnp.concatenate([hi, lo], axis=-1)


def _merge_kernel(*refs, n_real, n_fill, pending, **static):
    n_tok = 3 if pending else 1
    (mod_ref, n1_ref, n2_ref, a_ref, b_ref, c_ref, pc_ref, pp_ref, pn_ref, pw_ref, psc_ref,
     wg_ref, bg_ref, wb_ref, wo_ref, rw_ref, rb_ref) = refs[n_tok:n_tok + 17]
    outs = refs[-4:-1]
    ins = (refs[:n_tok], mod_ref, n1_ref, n2_ref, a_ref, b_ref, c_ref,
           (pc_ref, pp_ref, pn_ref, pw_ref, psc_ref),
           wg_ref, bg_ref, wb_ref, wo_ref, rw_ref, rb_ref)
    if n_fill:
        @pl.when(pl.program_id(1) >= n_real)
        def _():
            for ref in outs:
                ref[...] = jnp.zeros_like(ref)

        pl.when(pl.program_id(1) < n_real)(lambda: _merge_tile(ins, outs, refs[-1], **static))
    else:
        _merge_tile(ins, outs, refs[-1], **static)


def _merge_tile(ins, outs, merged_s, *, nw, sub, p_off, seq_lo, seq_hi):
    (tok_refs, mod_ref, n1_ref, n2_ref, a_ref, b_ref, c_ref, pool_refs,
     wg_ref, bg_ref, wb_ref, wo_ref, rw_ref, rb_ref) = ins
    xo_ref, h2_ref, route_ref = outs
    pc_ref, pp_ref, pn_ref, pw_ref, psc_ref = pool_refs
    x_ref = tok_refs[0]
    tm = x_ref.shape[1]
    sh1, sc1, g1, sh2, sc2, _ = _mod_slices(mod_ref[0])
    row0 = p_off + pl.program_id(1) * tm
    ext = jnp.concatenate([pp_ref[0], pc_ref[0], pn_ref[0]], axis=0).astype(F32)
    gpos = row0 - POOL_HALO + lax.broadcasted_iota(jnp.int32, (tm + 2 * POOL_HALO, 1), 0)
    ext = jnp.where((gpos >= seq_lo) & (gpos < seq_hi), ext, 0.0)
    for r in range(tm // sub):
        rows = slice(r * sub, (r + 1) * sub)
        x = x_ref[0, rows, :]
        if len(tok_refs) == 3:
            x = _pending_residual(x, tok_refs[1], tok_refs[2], rows)
        hb = _norm_mod(x, n1_ref[...], sh1, sc1).astype(BF16)
        pos = row0 - seq_lo + r * sub + lax.broadcasted_iota(jnp.int32, (sub, 1), 0)
        pooled = _pool_rows(ext[r * sub:(r + 1) * sub + 2 * POOL_HALO], pos, seq_hi - seq_lo,
                            pw_ref, psc_ref[...])
        branches = (a_ref[0, rows, :], b_ref[0, rows, :], c_ref[0, rows, :], pooled)
        for n in range(D_MODEL // nw):
            cols = slice(n * nw, (n + 1) * nw)
            merged = None
            for i, br in enumerate(branches):
                gate = jax.nn.sigmoid(
                    jnp.dot(hb, wg_ref[i, :, cols], preferred_element_type=F32) + bg_ref[i, :, cols])
                term = gate * jnp.dot(br, wb_ref[i, :, cols], preferred_element_type=F32)
                merged = term if merged is None else merged + term
            merged_s[rows, cols] = merged.astype(BF16)
        y = jnp.dot(merged_s[rows, :], wo_ref[...], preferred_element_type=F32)
        xn = x + g1 * y
        xo_ref[0, rows, :] = xn
        h2 = _norm_mod(xn, n2_ref[...], sh2, sc2)
        h2_ref[0, rows, :] = _pack_bf16_pairs(h2)
        h_hi = h2.astype(BF16)
        h_lo = (h2 - h_hi.astype(F32)).astype(BF16)
        nt = (((1,), (1,)), ((), ()))
        by_hi = lax.dot_general(rw_ref[...], h_hi, nt, preferred_element_type=F32)
        by_lo = lax.dot_general(rw_ref[:N_EXPERTS, :], h_lo, nt, preferred_element_type=F32)
        logits_t = by_hi[:N_EXPERTS] + by_hi[N_EXPERTS:] + by_lo
        route_ref[0, :, rows] = _route(logits_t, rb_ref[...])


def _merge(tokens, mod, mod_row, n1, n2, branches, offsets, p_in, pool_params, seq, weights, s_out, out_off,
           rows, tm, prev=None):
    pending = len(tokens) == 3
    b, _, d = tokens[0].shape
    wg, bg, wb, wo, rw_t, rb = weights
    pool_w, pool_scale = pool_params
    seq_lo, seq_hi = seq
    n_real = rows // tm
    n_fill = -(-(s_out - out_off - rows) // tm) if prev is None else 0
    step = lambda j: jnp.minimum(j, n_real - 1)
    blk = lambda width, off: pl.BlockSpec((1, tm, width), lambda bi, j: (bi, off // tm + step(j), 0))
    out_blk = lambda width: pl.BlockSpec((1, tm, width), lambda bi, j: (bi, out_off // tm + j, 0))
    assert all(o % tm == 0 for o in offsets) and out_off % tm == 0 and rows % tm == 0 and seq_lo % tm == 0
    hb = tm // POOL_HALO
    last_halo = p_in.shape[1] // POOL_HALO - 1
    halo = lambda shift: pl.BlockSpec(
        (1, POOL_HALO, BRANCH_W),
        lambda bi, j: (bi, jnp.clip((seq_lo // tm + step(j) + shift) * hb - 1 + shift, 0, last_halo), 0))
    out_shape = [
        jax.ShapeDtypeStruct((b, s_out, d), F32),
        jax.ShapeDtypeStruct((b, s_out, d // 2), jnp.int32),
        jax.ShapeDtypeStruct((b, ROUTE_ROWS, s_out), F32),
    ]
    mod_spec = pl.BlockSpec((1, 1, 6 * d), lambda bi, j: (bi if mod_row is None else mod_row, 0, 0))
    tok_off = out_off if pending else 0
    token_specs = [blk(d, tok_off)] + ([blk(d // 2, tok_off), mod_spec] if pending else [])
    n_in = len(tokens) + 17
    extra_specs = [] if prev is None else [pl.BlockSpec(memory_space=pl.ANY)] * 3
    return pl.pallas_call(
        functools.partial(_merge_kernel, nw=512, sub=MERGE_CHAIN_ROWS, n_real=n_real, n_fill=n_fill,
                          pending=pending, p_off=seq_lo, seq_lo=seq_lo, seq_hi=seq_hi),
        scratch_shapes=[pltpu.VMEM((tm, d), BF16)],
        grid=(b, n_real + n_fill),
        in_specs=token_specs + [
            mod_spec,
            _const_spec((1, d)), _const_spec((1, d)),
            *[blk(BRANCH_W, off) for off in offsets],
            blk(BRANCH_W, seq_lo), halo(0), halo(1),
            _const_spec(pool_w.shape), _const_spec((1, BRANCH_W)),
            _const_spec(wg.shape), _const_spec(bg.shape), _const_spec(wb.shape), _const_spec(wo.shape),
            _const_spec(rw_t.shape), _const_spec(rb.shape),
        ] + extra_specs,
        out_specs=[
            out_blk(d),
            out_blk(d // 2),
            pl.BlockSpec((1, ROUTE_ROWS, tm), lambda bi, j: (bi, 0, out_off // tm + j)),
        ],
        out_shape=out_shape,
        input_output_aliases={} if prev is None else {n_in + i: i for i in range(3)},
        compiler_params=_cparams(("parallel", "arbitrary"), 56),
        name="merge",
    )(*tokens, mod, n1, n2, *branches, p_in, p_in, p_in, pool_w, pool_scale,
      wg, bg, wb, wo, rw_t, rb, *(() if prev is None else prev))


def _rank_kernel(cls_ref, rank_ref, cnt_ref, cnt_s, *, tr):
    @pl.when(pl.program_id(0) == 0)
    def _():
        cnt_s[...] = jnp.zeros_like(cnt_s)

    cls = cls_ref[0]
    cid = lax.broadcasted_iota(jnp.int32, (CLASS_ROWS, tr), 0).astype(F32)
    onehot = cid == cls
    before = lax.broadcasted_iota(jnp.int32, (tr, tr), 0) < lax.broadcasted_iota(jnp.int32, (tr, tr), 1)
    prefix = jnp.dot(jnp.where(onehot, 1.0, 0.0).astype(BF16), jnp.where(before, 1.0, 0.0).astype(BF16),
                     preferred_element_type=F32)
    carry = cnt_s[...][:, 0:1]
    rank_ref[0] = jnp.sum(jnp.where(onehot, prefix + carry, 0.0), axis=0, keepdims=True)
    cnt_s[...] += jnp.sum(jnp.where(onehot, 1.0, 0.0), axis=1, keepdims=True)
    cnt_ref[...] = cnt_s[...]


def _rank(cls_flat, tr=512):
    n = cls_flat.shape[0]
    tr = math.gcd(n, tr)
    cls3 = cls_flat.reshape(n // tr, 1, tr)
    rank, cnt = pl.pallas_call(
        functools.partial(_rank_kernel, tr=tr),
        grid=(n // tr,),
        in_specs=[pl.BlockSpec((1, 1, tr), lambda i: (i, 0, 0))],
        out_specs=[pl.BlockSpec((1, 1, tr), lambda i: (i, 0, 0)), _const_spec((CLASS_ROWS, LANES))],
        out_shape=[jax.ShapeDtypeStruct((n // tr, 1, tr), F32), jax.ShapeDtypeStruct((CLASS_ROWS, LANES), F32)],
        scratch_shapes=[pltpu.VMEM((CLASS_ROWS, LANES), F32)],
        compiler_params=_cparams(("arbitrary",), 32),
        name="rank",
    )(cls3)
    return rank.reshape(n), cnt[:N_CLASSES, 0]


def _sc_layout(n):
    info = plsc.get_sparse_core_info()
    nw = info.num_cores * info.num_subcores
    per_worker = n // nw
    assert per_worker * nw == n
    chunk = max(c for c in range(8, SC_MAX_CHUNK + 1, 8) if per_worker % c == 0)
    return info.num_cores, nw, per_worker // chunk, chunk


def _sc_scatter_rows(src, pos, n_out):
    n, w = src.shape
    nc, nw, k, c = _sc_layout(n)
    mesh = plsc.VectorSubcoreMesh(core_axis_name="c", subcore_axis_name="s")

    @functools.partial(
        pl.kernel, mesh=mesh,
        out_type=jax.ShapeDtypeStruct((n_out, w), src.dtype),
        scratch_types=[pltpu.VMEM((k, c), jnp.int32), pltpu.VMEM((c, w), src.dtype), pltpu.SemaphoreType.DMA],
        name="moe_scatter",
    )
    def scatter(src_hbm, pos_hbm, out_hbm, idx_v, rows_v, sem):
        wid = lax.axis_index("s") * nc + lax.axis_index("c")
        pltpu.sync_copy(pos_hbm.at[wid], idx_v)

        @pl.loop(0, k)
        def _(j):
            off = pl.multiple_of(wid * (k * c) + j * c, 8)
            pltpu.sync_copy(src_hbm.at[pl.ds(off, c)], rows_v)
            pltpu.async_copy(rows_v, out_hbm.at[idx_v.at[j]], sem).wait()

    return scatter(src, pos.reshape(nw, k, c))


def _sc_gather_rows(src, pos):
    n = pos.shape[0]
    w = src.shape[1]
    nc, nw, k, c = _sc_layout(n)
    mesh = plsc.VectorSubcoreMesh(core_axis_name="c", subcore_axis_name="s")

    @functools.partial(
        pl.kernel, mesh=mesh,
        out_type=jax.ShapeDtypeStruct((n, w), src.dtype),
        scratch_types=[pltpu.VMEM((k, c), jnp.int32), pltpu.VMEM((c, w), src.dtype), pltpu.SemaphoreType.DMA],
        name="moe_gather",
    )
    def gather(src_hbm, pos_hbm, out_hbm, idx_v, rows_v, sem):
        wid = lax.axis_index("s") * nc + lax.axis_index("c")
        pltpu.sync_copy(pos_hbm.at[wid], idx_v)

        @pl.loop(0, k)
        def _(j):
            off = pl.multiple_of(wid * (k * c) + j * c, 8)
            pltpu.async_copy(src_hbm.at[idx_v.at[j]], rows_v, sem).wait()
            pltpu.sync_copy(rows_v, out_hbm.at[pl.ds(off, c)])

    return gather(src, pos.reshape(nw, k, c))


def _gmm_kernel(lo_ref, hi_ref, new_ref, nact_ref, h_ref, rw_ref,
                w1a_ref, w1b_ref, w3a_ref, w3b_ref, w2a_ref, w2b_ref, o_ref, w13_s, w2_s):
    t = pl.program_id(0)

    @pl.when(new_ref[t] == 1)
    def _():
        for i, (w1_ref, w3_ref, w2_ref) in enumerate(((w1a_ref, w3a_ref, w2a_ref), (w1b_ref, w3b_ref, w2b_ref))):
            w13_s[:, 2 * i * EXPERT_FF:(2 * i + 1) * EXPERT_FF] = w1_ref[0, 0].astype(BF16)
            w13_s[:, (2 * i + 1) * EXPERT_FF:(2 * i + 2) * EXPERT_FF] = w3_ref[0, 0].astype(BF16)
            w2_s[i * EXPERT_FF:(i + 1) * EXPERT_FF, :] = w2_ref[0, 0].astype(BF16)

    @pl.when(t < nact_ref[0])
    def _():
        xf = _unpack_bf16_pairs(h_ref[...])
        x = xf.astype(BF16)
        aff = [jax.nn.sigmoid(jnp.sum(xf * rw_ref[pl.ds(e_ref[t], 1), :], axis=-1, keepdims=True))
               for e_ref in (lo_ref, hi_ref)]
        total = aff[0] + aff[1]

        ab = jnp.dot(x, w13_s[...], preferred_element_type=F32)

        def hidden(i):
            a = ab[:, 2 * i * EXPERT_FF:(2 * i + 1) * EXPERT_FF]
            gate = ab[:, (2 * i + 1) * EXPERT_FF:(2 * i + 2) * EXPERT_FF]
            return (aff[i] / total) * ((a * jax.nn.sigmoid(a)) * gate)

        hid = jnp.concatenate([hidden(0), hidden(1)], axis=-1).astype(BF16)
        o_ref[...] = _pack_bf16_pairs(jnp.dot(hid, w2_s[...], preferred_element_type=F32))


def _gmm(tile_lo, tile_hi, tile_new, n_act, hs, rw_t, w1, w3, w2, layer, tm):
    n_pad, half = hs.shape
    d = 2 * half
    row = lambda t, lo, hi, new, na: (jnp.minimum(t, na[0] - 1), 0)
    e_lo = lambda t, lo, hi, new, na: (layer, lo[jnp.minimum(t, na[0] - 1)], 0, 0)
    e_hi = lambda t, lo, hi, new, na: (layer, hi[jnp.minimum(t, na[0] - 1)], 0, 0)
    up = lambda e: pl.BlockSpec((1, 1, d, EXPERT_FF), e)
    down = lambda e: pl.BlockSpec((1, 1, EXPERT_FF, d), e)
    return pl.pallas_call(
        _gmm_kernel,
        grid_spec=pltpu.PrefetchScalarGridSpec(
            num_scalar_prefetch=4,
            grid=(n_pad // tm,),
            in_specs=[
                pl.BlockSpec((tm, half), row),
                pl.BlockSpec(rw_t.shape, lambda t, lo, hi, new, na: (0, 0)),
                up(e_lo), up(e_hi), up(e_lo), up(e_hi), down(e_lo), down(e_hi),
            ],
            out_specs=pl.BlockSpec((tm, half), row),
            scratch_shapes=[pltpu.VMEM((d, 4 * EXPERT_FF), BF16), pltpu.VMEM((2 * EXPERT_FF, d), BF16)],
        ),
        out_shape=jax.ShapeDtypeStruct((n_pad, half), jnp.int32),
        compiler_params=_cparams(("arbitrary",), 56),
        name="moe_gmm",
    )(tile_lo, tile_hi, tile_new, n_act, hs, rw_t, w1, w1, w3, w3, w2, w2)


def _moe_routed(h2p, route, rw_t, w1, w3, w2, layer, tm=256):
    b, s, half = h2p.shape
    n = b * s
    n_pad = n + N_CLASSES * tm
    cls = route[:, 0, :].reshape(n)
    rank, counts = _rank(cls)
    counts = counts.astype(jnp.int32)
    padded = (counts + tm - 1) // tm * tm
    ends = jnp.cumsum(padded)
    pos = jnp.take(ends - padded, cls.astype(jnp.int32)) + rank.astype(jnp.int32)
    n_act = (ends[-1] // tm).reshape(1)
    tile_row = jnp.arange(n_pad // tm, dtype=jnp.int32) * tm
    tile_cls = jnp.minimum(jnp.sum(tile_row[:, None] >= ends[None, :], axis=1), N_CLASSES - 1)
    pair_lo, pair_hi = (jnp.asarray(a, jnp.int32) for a in _class_experts())
    hs = _sc_scatter_rows(h2p.reshape(n, half), pos, n_pad)
    prev_cls = jnp.concatenate([jnp.full((1,), -1, tile_cls.dtype), tile_cls[:-1]])
    tile_new = ((tile_cls != prev_cls) & (tile_row < ends[-1])).astype(jnp.int32)
    ys = _gmm(jnp.take(pair_lo, tile_cls), jnp.take(pair_hi, tile_cls), tile_new, n_act, hs, rw_t,
              w1, w3, w2, layer, tm)
    return _sc_gather_rows(ys, pos).reshape(b, s, half)


def _class_experts():
    lo, hi = [], []
    for g in range(N_EXPERTS // EXPERTS_PER_GROUP):
        for i in range(EXPERTS_PER_GROUP):
            for j in range(i + 1, EXPERTS_PER_GROUP):
                lo.append(EXPERTS_PER_GROUP * g + i)
                hi.append(EXPERTS_PER_GROUP * g + j)
    return np.array(lo), np.array(hi)


def _final_residual_kernel(x_ref, y_ref, mod_ref, g_ref, o_ref):
    x = _pending_residual(x_ref[0], y_ref, mod_ref)
    o_ref[0] = x * lax.rsqrt(jnp.mean(x * x, axis=-1, keepdims=True) + EPS) * g_ref[...]


def _final_residual(x1, yp, mod, gain, tm=512):
    b, t, d = x1.shape
    tok = lambda bi, j: (bi, j, 0)
    return pl.pallas_call(
        _final_residual_kernel,
        grid=(b, t // tm),
        in_specs=[
            pl.BlockSpec((1, tm, d), tok),
            pl.BlockSpec((1, tm, d // 2), tok),
            pl.BlockSpec((1, 1, 6 * d), lambda bi, j: (bi, 0, 0)),
            _const_spec((1, d)),
        ],
        out_specs=pl.BlockSpec((1, tm, d), tok),
        out_shape=jax.ShapeDtypeStruct((b, t, d), F32),
        compiler_params=_cparams(("parallel", "arbitrary"), 32),
        name="final_residual",
    )(x1, yp, mod, gain)


ROPE_FREQS = ROPE_AXIS_DIM // 2


def _rope_tables(t_lat, n_ctx):
    rows = t_lat // GRID_W
    row = jnp.repeat(jnp.arange(rows, dtype=F32), GRID_W)
    col = jnp.tile(jnp.arange(GRID_W, dtype=F32), rows)
    inv_freq = ROPE_THETA ** (-jnp.arange(0, ROPE_AXIS_DIM, 2, dtype=F32) / ROPE_AXIS_DIM)
    ang = jnp.stack([row[:, None] * inv_freq, col[:, None] * inv_freq], axis=1)
    lanes = (2, LANES // HEAD_DIM, 2, ROPE_FREQS)
    sign = jnp.array([-1.0, 1.0], F32).reshape(1, 2, 1, 1, 1)
    cos = jnp.broadcast_to(jnp.cos(ang)[:, None, None], (t_lat,) + lanes).reshape(t_lat, LANES)
    sin = jnp.broadcast_to(sign * jnp.sin(ang)[:, None, None], (t_lat,) + lanes).reshape(t_lat, LANES)
    pad = lambda tbl, fill: jnp.concatenate([tbl, jnp.full((n_ctx, LANES), fill, F32)], axis=0)
    return pad(cos, 1.0), pad(sin, 0.0)


def _qk_lanes(w, n_heads):
    lead = w.shape[:-1]
    n_chunks = n_heads // N_KV_HEADS
    w = w.reshape(lead + (N_KV_HEADS, n_chunks, 2, 2, ROPE_FREQS))
    nl = len(lead)
    w = jnp.transpose(w, tuple(range(nl)) + (nl + 1, nl + 3, nl, nl + 2, nl + 4))
    return w.reshape(lead + (n_heads * HEAD_DIM,))


def _permute_heads(w, axis):
    shp = w.shape
    w = w.reshape(shp[:axis] + (N_Q_HEADS, HEAD_DIM) + shp[axis + 1:])
    w = jnp.take(w, jnp.array(HEAD_PERM), axis=axis)
    return w.reshape(shp)


def _square_factor(n):
    r = int(round(math.sqrt(n)))
    assert r * r == n, "sequence lengths must be perfect squares for the two-stage DFT"
    return r


def kernel(x, c, ctx, c_ctx, w_ada, b_ada, norm1, norm2, w_in, q_gain, k_gain, sink, pool_w, pool_scale,
           w_branch, w_gate, b_gate, w_out, router_w, router_bias, w1, w3, w2, norm_f):
    b, t_lat, d = x.shape
    n_ctx = ctx.shape[1]
    s = t_lat + n_ctx
    depth = w_ada.shape[0]
    assert d == D_MODEL and b < MOD_ROWS and t_lat % 256 == 0 and n_ctx % 256 == 0 and t_lat % n_ctx == 0

    tokens = (x, ctx)
    cc = jnp.zeros((MOD_ROWS, d), F32).at[:b].set(c).at[b].set(c_ctx)
    mod_all = _ada(cc, w_ada, b_ada).reshape(depth, MOD_ROWS, 1, 6 * d)

    cos, sin = _rope_tables(t_lat, n_ctx)
    lane_head = (np.arange(BRANCH_W) // LANES) * 2 + (np.arange(BRANCH_W) // (HEAD_DIM // 2)) % 2
    seg = jnp.asarray((lane_head[:, None] == lane_head[None, :]) / HEAD_DIM, BF16)
    cs = jnp.asarray(_channel_dft_table()).astype(BF16)
    f_lat = [jnp.asarray(a).astype(BF16) for a in _fourier_tables(*(_square_factor(t_lat),) * 2)]
    f_ctx = [jnp.asarray(a).astype(BF16) for a in _fourier_tables(*(_square_factor(n_ctx),) * 2)]
    wbias = jnp.asarray(_window_bias(n_ctx))
    rw_hi = router_w.T.astype(BF16)
    rw_t = jnp.concatenate([rw_hi, (router_w.T - rw_hi.astype(F32)).astype(BF16)], axis=0)
    rb = router_bias.reshape(N_EXPERTS, 1)

    for l in range(depth):
        need_ctx = l < depth - 1
        s_out = s if need_ctx else t_lat
        cols = jnp.split(w_in[l], np.cumsum((512, 512, 512, 128, 128, 512, 128))[:], axis=1)
        f_w, p_w, qb_w, kb_w, vb_w, qw_w, kw_w, vw_w = cols
        w_in_l = jnp.concatenate([_qk_lanes(qb_w, N_Q_HEADS), _qk_lanes(qw_w, N_Q_HEADS),
                                  _qk_lanes(kb_w, N_KV_HEADS), _qk_lanes(kw_w, N_KV_HEADS),
                                  f_w, vb_w, vw_w, p_w], axis=1).astype(BF16)
        wb_l = jnp.stack([w_branch[l, 0], _permute_heads(w_branch[l, 1], 0),
                          _permute_heads(w_branch[l, 2], 0), w_branch[l, 3]]).astype(BF16)
        mod = mod_all[l]
        n1 = norm1[l].reshape(1, d)
        n2 = norm2[l].reshape(1, d)
        qg = _qk_lanes(jnp.tile(q_gain[l], N_Q_HEADS), N_Q_HEADS).reshape(1, BRANCH_W)
        kg = _qk_lanes(jnp.tile(k_gain[l], N_KV_HEADS), N_KV_HEADS).reshape(1, LANES)

        zr, zi, p_in, qb, qw, kvb, kvw = _inproj(tokens, t_lat, mod, n1, w_in_l, qg, kg, seg, cs, cos, sin)

        out_a = _fourier(zr, zi, f_lat[0], f_lat[1], t_lat, 0)
        out_b = _gattn(qb, kvb, 0, t_lat, 0, s, tq=512)
        out_c = _wattn(jnp.take(sink[l], jnp.array(HEAD_PERM)) * LOG2E, wbias, qw, kvw, s_out, t_lat)

        weights = (w_gate[l].astype(BF16), b_gate[l].reshape(4, 1, d), wb_l, w_out[l].astype(BF16), rw_t, rb)
        pool_params = (pool_w[l].astype(BF16), pool_scale[l].reshape(1, BRANCH_W))
        pending = len(tokens) == 3
        merged = _merge(tokens if pending else tokens[:1], mod, None, n1, n2, (out_a, out_b, out_c), (0, 0, 0),
                        p_in, pool_params, (0, t_lat), weights, s_out, 0, t_lat, 2 * MERGE_CHAIN_ROWS)
        if need_ctx:
            out_ac = _fourier(zr, zi, f_ctx[0], f_ctx[1], n_ctx, t_lat // n_ctx)
            out_bc = _gattn(qb, kvb, t_lat, n_ctx, t_lat, n_ctx)
            merged = _merge(tokens if pending else tokens[1:], mod, b, n1, n2, (out_ac, out_bc, out_c),
                            (0, 0, t_lat), p_in, pool_params, (t_lat, s), weights, s_out, t_lat, n_ctx,
                            MERGE_CHAIN_ROWS, prev=merged)
        x1, h2p, route = merged
        yp = _moe_routed(h2p, route, router_w.T, w1, w3, w2, l)
        tokens = (x1, yp, mod)

    return _final_residual(*tokens, norm_f.reshape(1, d))
```
